```python
import math
import jax, jax.numpy as jnp
from jax import lax
import numpy as np

D_MODEL = 1024
BATCH = 4
SEQ = 4096
DEPTH = 1
DEC_BATCH = 16
DEC_SEQ = 16
PAST_LEN = 2048

CHUNK = 64
WINDOW = 128
N_BACK = WINDOW // CHUNK
ATTN_WIDTH = D_MODEL // 2
HEAD_DIM = 64
N_HEADS = ATTN_WIDTH // HEAD_DIM
N_KV_HEADS = 2
Q_PER_KV = N_HEADS // N_KV_HEADS
KV_WIDTH = N_KV_HEADS * HEAD_DIM
SSM_WIDTH = D_MODEL - ATTN_WIDTH
SSM_GROUP = 16
SSM_GROUPS = SSM_WIDTH // SSM_GROUP
SSM_STATE = 64
PROJ_WIDTH = ATTN_WIDTH + 2 * KV_WIDTH + SSM_WIDTH
N_EXPERT_GROUPS = 4
EXPERTS_PER_GROUP = 8
N_EXPERTS = N_EXPERT_GROUPS * EXPERTS_PER_GROUP
TOP_K = 2
D_EXPERT = D_MODEL // 2
EPS = 1e-6
NEG = -1e30

kernel_name = 'hymba_swa_sink_s5_hmoe_stream_step'


def _rmsnorm(x, g):
    xf = x.astype(jnp.float32)
    y = xf * lax.rsqrt(jnp.mean(xf * xf, axis=-1, keepdims=True) + EPS)
    return (y * g.astype(jnp.float32)).astype(x.dtype)


def _project(h, w_in):
    b, L = h.shape[:2]
    z = h @ w_in
    q = z[..., :ATTN_WIDTH].reshape(b, L, N_HEADS, HEAD_DIM)
    k = z[..., ATTN_WIDTH:ATTN_WIDTH + KV_WIDTH].reshape(b, L, N_KV_HEADS, HEAD_DIM)
    v = z[..., ATTN_WIDTH + KV_WIDTH:ATTN_WIDTH + 2 * KV_WIDTH].reshape(b, L, N_KV_HEADS, HEAD_DIM)
    u = z[..., ATTN_WIDTH + 2 * KV_WIDTH:]
    return q, k, v, u


def _sink_attention(q, k, v, valid, sinks):
    s = jnp.einsum('bnqhgd,bnkhd->bnhgqk', q.astype(jnp.float32), k.astype(jnp.float32)) * (HEAD_DIM ** -0.5)
    s = jnp.where(valid[None, :, None, None, None, :], s, NEG)
    sink = sinks.astype(jnp.float32).reshape(1, 1, N_KV_HEADS, Q_PER_KV, 1, 1)
    m = jnp.maximum(jnp.max(s, axis=-1, keepdims=True), sink)
    p = jnp.exp(s - m)
    denom = jnp.sum(p, axis=-1, keepdims=True) + jnp.exp(sink - m)
    return jnp.einsum('bnhgqk,bnkhd->bnqhgd', p / denom, v.astype(jnp.float32))


def _attn_prompt(q, k, v, sinks):
    b, L = q.shape[:2]
    nc = L // CHUNK
    qb = q.reshape(b, nc, CHUNK, N_KV_HEADS, Q_PER_KV, HEAD_DIM)
    pad = ((0, 0), (N_BACK * CHUNK, 0), (0, 0), (0, 0))
    kp = jnp.pad(k, pad).reshape(b, nc + N_BACK, CHUNK, N_KV_HEADS, HEAD_DIM)
    vp = jnp.pad(v, pad).reshape(b, nc + N_BACK, CHUNK, N_KV_HEADS, HEAD_DIM)
    kb = jnp.concatenate([kp[:, j:j + nc] for j in range(N_BACK + 1)], axis=2)
    vb = jnp.concatenate([vp[:, j:j + nc] for j in range(N_BACK + 1)], axis=2)
    key_chunk = jnp.arange(nc)[:, None] - N_BACK + jnp.arange((N_BACK + 1) * CHUNK)[None, :] // CHUNK
    o = _sink_attention(qb, kb, vb, key_chunk >= 0, sinks)
    return o.reshape(b, L, ATTN_WIDTH)


def _attn_sample(q, k_new, v_new, cache_k, cache_v, sinks):
    b, L = q.shape[:2]
    wc = cache_k.shape[1]
    qb = q.reshape(b, 1, L, N_KV_HEADS, Q_PER_KV, HEAD_DIM)
    k_all = jnp.concatenate([cache_k.astype(k_new.dtype), k_new], axis=1)
    v_all = jnp.concatenate([cache_v.astype(v_new.dtype), v_new], axis=1)
    valid = jnp.ones((1, wc + L), dtype=bool)
    o = _sink_attention(qb, k_all[:, None], v_all[:, None], valid, sinks)
    return o.reshape(b, L, ATTN_WIDTH), k_all[:, L:], v_all[:, L:]


def _ssm_discretise(a_re, a_im, log_dt, b_re, b_im):
    dt = jnp.exp(log_dt.astype(jnp.float32))[:, None]
    lr = a_re.astype(jnp.float32)
    li = a_im.astype(jnp.float32)
    mag = jnp.exp(lr * dt)
    ang = li * dt
    abar_re = mag * jnp.cos(ang)
    abar_im = mag * jnp.sin(ang)
    den = lr * lr + li * li
    nr = abar_re - 1.0
    ni = abar_im
    f_re = ((nr * lr + ni * li) / den)[..., None]
    f_im = ((ni * lr - nr * li) / den)[..., None]
    br = b_re.astype(jnp.float32)
    bi = b_im.astype(jnp.float32)
    bbar_re = f_re * br - f_im * bi
    bbar_im = f_re * bi + f_im * br
    return abar_re, abar_im, bbar_re, bbar_im


def _cplx_combine(e1, e2):
    a1r, a1i, b1r, b1i = e1
    a2r, a2i, b2r, b2i = e2
    return (a2r * a1r - a2i * a1i,
            a2r * a1i + a2i * a1r,
            a2r * b1r - a2i * b1i + b2r,
            a2r * b1i + a2i * b1r + b2i)


def _ssm(u, h0_re, h0_im, disc, c_re, c_im, d_skip, w_glu, b_glu):
    abar_re, abar_im, bbar_re, bbar_im = disc
    b, L = u.shape[:2]
    ug = u.astype(jnp.float32).reshape(b, L, SSM_GROUPS, SSM_GROUP)
    bu_re = jnp.einsum('blgc,gpc->blgp', ug, bbar_re)
    bu_im = jnp.einsum('blgc,gpc->blgp', ug, bbar_im)
    h0r = h0_re.astype(jnp.float32)
    h0i = h0_im.astype(jnp.float32)
    bu_re = bu_re.at[:, 0].add(abar_re * h0r - abar_im * h0i)
    bu_im = bu_im.at[:, 0].add(abar_re * h0i + abar_im * h0r)
    a_r = jnp.broadcast_to(abar_re, bu_re.shape)
    a_i = jnp.broadcast_to(abar_im, bu_re.shape)
    _, _, h_re, h_im = lax.associative_scan(_cplx_combine, (a_r, a_i, bu_re, bu_im), axis=1)
    y = (jnp.einsum('blgp,gcp->blgc', h_re, c_re.astype(jnp.float32))
         - jnp.einsum('blgp,gcp->blgc', h_im, c_im.astype(jnp.float32))
         + d_skip.astype(jnp.float32).reshape(SSM_GROUPS, SSM_GROUP) * ug)
    y = jax.nn.gelu(y.reshape(b, L, SSM_WIDTH))
    y = y * jax.nn.sigmoid(y @ w_glu.astype(jnp.float32) + b_glu.astype(jnp.float32))
    return y.astype(u.dtype), h_re[:, -1], h_im[:, -1]


def _hier_moe(x, w_rg, b_rg, w_re, b_re, w_g, w_u, w_d):
    t = x.shape[0]
    lg = (x @ w_rg).astype(jnp.float32) + b_rg.astype(jnp.float32)
    pg = jax.nn.softmax(lg, axis=-1)
    _, gsel = lax.top_k(lg, 1)
    g_onehot = jax.nn.one_hot(gsel[:, 0], N_EXPERT_GROUPS, dtype=jnp.float32)
    le = ((x @ w_re).astype(jnp.float32) + b_re.astype(jnp.float32)).reshape(t, N_EXPERT_GROUPS, EXPERTS_PER_GROUP)
    le_g = jnp.einsum('tge,tg->te', le, g_onehot)
    tv, ti = lax.top_k(le_g, TOP_K)
    w = jax.nn.softmax(tv, axis=-1) * jnp.sum(pg * g_onehot, axis=-1, keepdims=True)
    eid = gsel * EXPERTS_PER_GROUP + ti
    gates = jnp.sum(jax.nn.one_hot(eid, N_EXPERTS, dtype=jnp.float32) * w[..., None], axis=1)
    y = jnp.zeros((t, D_MODEL), jnp.float32)
    for e in range(N_EXPERTS):
        h = jax.nn.silu(x @ w_g[e]) * (x @ w_u[e])
        y = y + gates[:, e:e + 1] * (h @ w_d[e]).astype(jnp.float32)
    return y.astype(x.dtype)


def setup_inputs(seed: int = 0) -> dict:
    key = jax.random.key(seed)
    ks = jax.random.split(key, 32)
    f32 = jnp.float32
    nrm = lambda k, shape, s: jax.random.normal(k, shape, f32) * s
    wc = min(WINDOW, PAST_LEN)
    n_idx = jnp.arange(SSM_STATE, dtype=f32)
    log_dt = jax.random.uniform(ks[8], (DEPTH, SSM_GROUPS), f32, math.log(1e-3), math.log(1e-1))
    return {
        'x_prompt': nrm(ks[0], (BATCH, SEQ, D_MODEL), 1.0),
        'x_sample': nrm(ks[1], (DEC_BATCH, DEC_SEQ, D_MODEL), 1.0),
        'cache_k': nrm(ks[2], (DEPTH, DEC_BATCH, wc, N_KV_HEADS, HEAD_DIM), 1.0),
        'cache_v': nrm(ks[3], (DEPTH, DEC_BATCH, wc, N_KV_HEADS, HEAD_DIM), 1.0),
        'state_ssm_re': nrm(ks[4], (DEPTH, DEC_BATCH, SSM_GROUPS, SSM_STATE), 0.5),
        'state_ssm_im': nrm(ks[5], (DEPTH, DEC_BATCH, SSM_GROUPS, SSM_STATE), 0.5),
        'g_norm_mix': 1.0 + nrm(ks[6], (DEPTH, D_MODEL), 0.02),
        'w_in': nrm(ks[7], (DEPTH, D_MODEL, PROJ_WIDTH), D_MODEL ** -0.5),
        'attn_sinks': nrm(ks[9], (DEPTH, N_HEADS), 0.5),
        'ssm_a_re': -0.5 + nrm(ks[10], (DEPTH, SSM_GROUPS, SSM_STATE), 0.01),
        'ssm_a_im': math.pi * n_idx + nrm(ks[11], (DEPTH, SSM_GROUPS, SSM_STATE), 0.01),
        'ssm_log_dt': log_dt,
        'ssm_b_re': nrm(ks[12], (DEPTH, SSM_GROUPS, SSM_STATE, SSM_GROUP), (2 * SSM_GROUP) ** -0.5),
        'ssm_b_im': nrm(ks[13], (DEPTH, SSM_GROUPS, SSM_STATE, SSM_GROUP), (2 * SSM_GROUP) ** -0.5),
        'ssm_c_re': nrm(ks[14], (DEPTH, SSM_GROUPS, SSM_GROUP, SSM_STATE), (2 * SSM_STATE) ** -0.5),
        'ssm_c_im': nrm(ks[15], (DEPTH, SSM_GROUPS, SSM_GROUP, SSM_STATE), (2 * SSM_STATE) ** -0.5),
        'ssm_d': 1.0 + nrm(ks[16], (DEPTH, SSM_WIDTH), 0.1),
        'w_glu': nrm(ks[17], (DEPTH, SSM_WIDTH, SSM_WIDTH), SSM_WIDTH ** -0.5),
        'b_glu': nrm(ks[18], (DEPTH, SSM_WIDTH), 0.02),
        'g_attn_out': 1.0 + nrm(ks[19], (DEPTH, ATTN_WIDTH), 0.02),
        'g_ssm_out': 1.0 + nrm(ks[20], (DEPTH, SSM_WIDTH), 0.02),
        'w_out': nrm(ks[21], (DEPTH, D_MODEL, D_MODEL), D_MODEL ** -0.5),
        'g_norm_ffn': 1.0 + nrm(ks[22], (DEPTH, D_MODEL), 0.02),
        'w_router_group': nrm(ks[23], (DEPTH, D_MODEL, N_EXPERT_GROUPS), D_MODEL ** -0.5),
        'b_router_group': nrm(ks[24], (DEPTH, N_EXPERT_GROUPS), 0.01),
        'w_router_expert': nrm(ks[25], (DEPTH, D_MODEL, N_EXPERTS), D_MODEL ** -0.5),
        'b_router_expert': nrm(ks[26], (DEPTH, N_EXPERTS), 0.01),
        'w_exp_gate': nrm(ks[27], (DEPTH, N_EXPERTS, D_MODEL, D_EXPERT), D_MODEL ** -0.5),
        'w_exp_up': nrm(ks[28], (DEPTH, N_EXPERTS, D_MODEL, D_EXPERT), D_MODEL ** -0.5),
        'w_exp_down': nrm(ks[29], (DEPTH, N_EXPERTS, D_EXPERT, D_MODEL), D_EXPERT ** -0.5),
        'g_final': 1.0 + nrm(ks[30], (D_MODEL,), 0.02),
    }


def reference(x_prompt, x_sample, cache_k, cache_v, state_ssm_re, state_ssm_im,
              g_norm_mix, w_in, attn_sinks, ssm_a_re, ssm_a_im, ssm_log_dt,
              ssm_b_re, ssm_b_im, ssm_c_re, ssm_c_im, ssm_d, w_glu, b_glu,
              g_attn_out, g_ssm_out, w_out, g_norm_ffn,
              w_router_group, b_router_group, w_router_expert, b_router_expert,
              w_exp_gate, w_exp_up, w_exp_down, g_final):
    xp, xs = x_prompt, x_sample
    bp, lp = xp.shape[:2]
    bs, ls = xs.shape[:2]
    wc_prompt = min(WINDOW, lp)
    kp_l, vp_l, srp_l, sip_l, ks_l, vs_l, srs_l, sis_l = [], [], [], [], [], [], [], []
    for l in range(DEPTH):
        hp = _rmsnorm(xp, g_norm_mix[l])
        hs = _rmsnorm(xs, g_norm_mix[l])
        qp, kp, vp, up = _project(hp, w_in[l])
        qs, kq, vq, us = _project(hs, w_in[l])
        ap = _attn_prompt(qp, kp, vp, attn_sinks[l]).astype(xp.dtype)
        a_s, k_roll, v_roll = _attn_sample(qs, kq, vq, cache_k[l], cache_v[l], attn_sinks[l])
        a_s = a_s.astype(xs.dtype)
        disc = _ssm_discretise(ssm_a_re[l], ssm_a_im[l], ssm_log_dt[l], ssm_b_re[l], ssm_b_im[l])
        h0 = jnp.zeros((bp, SSM_GROUPS, SSM_STATE), jnp.float32)
        sp, hr_p, hi_p = _ssm(up, h0, h0, disc, ssm_c_re[l], ssm_c_im[l], ssm_d[l], w_glu[l], b_glu[l])
        ss, hr_s, hi_s = _ssm(us, state_ssm_re[l], state_ssm_im[l], disc, ssm_c_re[l], ssm_c_im[l],
                              ssm_d[l], w_glu[l], b_glu[l])
        mix_p = jnp.concatenate([_rmsnorm(ap, g_attn_out[l]), _rmsnorm(sp, g_ssm_out[l])], axis=-1) @ w_out[l]
        mix_s = jnp.concatenate([_rmsnorm(a_s, g_attn_out[l]), _rmsnorm(ss, g_ssm_out[l])], axis=-1) @ w_out[l]
        xp = xp + mix_p
        xs = xs + mix_s
        tp = bp * lp
        hf = jnp.concatenate([_rmsnorm(xp, g_norm_ffn[l]).reshape(tp, D_MODEL),
                              _rmsnorm(xs, g_norm_ffn[l]).reshape(bs * ls, D_MODEL)], axis=0)
        f = _hier_moe(hf, w_router_group[l], b_router_group[l], w_router_expert[l], b_router_expert[l],
                      w_exp_gate[l], w_exp_up[l], w_exp_down[l])
        xp = xp + f[:tp].reshape(bp, lp, D_MODEL)
        xs = xs + f[tp:].reshape(bs, ls, D_MODEL)
        kp_l.append(kp[:, lp - wc_prompt:].astype(cache_k.dtype))
        vp_l.append(vp[:, lp - wc_prompt:].astype(cache_v.dtype))
        srp_l.append(hr_p.astype(state_ssm_re.dtype))
        sip_l.append(hi_p.astype(state_ssm_im.dtype))
        ks_l.append(k_roll.astype(cache_k.dtype))
        vs_l.append(v_roll.astype(cache_v.dtype))
        srs_l.append(hr_s.astype(state_ssm_re.dtype))
        sis_l.append(hi_s.astype(state_ssm_im.dtype))
    y_prompt = _rmsnorm(xp, g_final)
    y_sample = _rmsnorm(xs, g_final)
    new_k_prompt = jnp.stack(kp_l, 0)
    new_v_prompt = jnp.stack(vp_l, 0)
    new_ssm_re_prompt = jnp.stack(srp_l, 0)
    new_ssm_im_prompt = jnp.stack(sip_l, 0)
    new_k_sample = jnp.stack(ks_l, 0)
    new_v_sample = jnp.stack(vs_l, 0)
    new_ssm_re_sample = jnp.stack(srs_l, 0)
    new_ssm_im_sample = jnp.stack(sis_l, 0)
    return (y_prompt, y_sample, new_k_prompt, new_v_prompt, new_ssm_re_prompt, new_ssm_im_prompt,
            new_k_sample, new_v_sample, new_ssm_re_sample, new_ssm_im_sample)
```

```python
import functools
import math

import jax
import jax.numpy as jnp
from jax import lax
from jax.experimental import pallas as pl
from jax.experimental.pallas import tpu as pltpu

f32, bf16, i32 = jnp.float32, jnp.bfloat16, jnp.int32

D_MODEL = 1024
CHUNK = 64
N_BACK = 2
WINDOW = 128
ATTN_WIDTH = 512
HEAD_DIM = 64
N_KV_HEADS = 2
Q_PER_KV = 4
KV_WIDTH = 128
SSM_WIDTH = 512
SSM_GROUP = 16
SSM_GROUPS = 32
SSM_STATE = 64
PROJ_WIDTH = 1280
N_EXPERT_GROUPS = 4
EXPERTS_PER_GROUP = 8
N_EXPERTS = 32
D_EXPERT = 512
EPS = 1e-6
NEG = -1e30

LANES = 128
SSM_STEPS = 16
SSM_BLOCKS = SSM_WIDTH // LANES
GROUPS_PER_BLOCK = LANES // SSM_GROUP
STATE_COLS = GROUPS_PER_BLOCK * SSM_STATE
ROW_TILE = 256
VMEM_LIMIT = 56 * 1024 * 1024


def _cparams(n_axes=1, limit=VMEM_LIMIT):
    return pltpu.CompilerParams(dimension_semantics=("arbitrary",) * n_axes, vmem_limit_bytes=limit)


def _rms(x, g):
    return x * lax.rsqrt(jnp.mean(x * x, axis=-1, keepdims=True) + EPS) * g


def _bdot(a, b):
    return jnp.dot(a.astype(bf16), b.astype(bf16), preferred_element_type=f32)


def _proj_body(x_ref, g_ref, w_ref, q_ref, k_ref, v_ref, u_ref):
    h = _rms(x_ref[...], g_ref[...])
    z = _bdot(h, w_ref[...])
    q_ref[...] = z[:, :ATTN_WIDTH] * (HEAD_DIM ** -0.5)
    k_ref[...] = z[:, ATTN_WIDTH:ATTN_WIDTH + KV_WIDTH]
    v_ref[...] = z[:, ATTN_WIDTH + KV_WIDTH:ATTN_WIDTH + 2 * KV_WIDTH]
    u_ref[...] = z[:, ATTN_WIDTH + 2 * KV_WIDTH:]


def _proj(x2d, g, w_bf, tm):
    t = x2d.shape[0]
    row = lambda n: pl.BlockSpec((tm, n), lambda i: (i, 0))
    full = lambda a: pl.BlockSpec(a.shape, lambda i: (0,) * a.ndim)
    return pl.pallas_call(
        _proj_body,
        grid=(t // tm,),
        in_specs=[row(D_MODEL), full(g), full(w_bf)],
        out_specs=[row(ATTN_WIDTH), row(KV_WIDTH), row(KV_WIDTH), row(SSM_WIDTH)],
        out_shape=[jax.ShapeDtypeStruct((t, n), f32) for n in (ATTN_WIDTH, KV_WIDTH, KV_WIDTH, SSM_WIDTH)],
        compiler_params=_cparams(),
        name="proj",
    )(x2d, g, w_bf)


def _sink_column(sink_ref, kv, rows_per_head):
    r = lax.broadcasted_iota(i32, (Q_PER_KV * rows_per_head, 1), 0)
    col = jnp.full((Q_PER_KV * rows_per_head, 1), sink_ref[kv * Q_PER_KV], f32)
    for j in range(1, Q_PER_KV):
        col = jnp.where(r >= j * rows_per_head, sink_ref[kv * Q_PER_KV + j], col)
    return col


def _attend(qs, kc, vc, sink_col, valid):
    s = lax.dot_general(qs.astype(bf16), kc.astype(bf16), (((1,), (1,)), ((), ())), preferred_element_type=f32)
    if valid is not None:
        s = jnp.where(valid, s, NEG)
    m = jnp.maximum(jnp.max(s, axis=-1, keepdims=True), sink_col)
    p = jnp.exp(s - m)
    denom = jnp.sum(p, axis=-1, keepdims=True) + jnp.exp(sink_col - m)
    return _bdot(p, vc) / denom


def _heads_attend(q, k, v, sink_ref, valid):
    rows = q.shape[0]
    pieces = []
    for kv in range(N_KV_HEADS):
        qs = jnp.concatenate(
            [q[:, (kv * Q_PER_KV + j) * HEAD_DIM:(kv * Q_PER_KV + j + 1) * HEAD_DIM] for j in range(Q_PER_KV)], axis=0)
        o = _attend(qs, k[:, kv * HEAD_DIM:(kv + 1) * HEAD_DIM], v[:, kv * HEAD_DIM:(kv + 1) * HEAD_DIM],
                    _sink_column(sink_ref, kv, rows), valid)
        pieces += [o[j * rows:(j + 1) * rows] for j in range(Q_PER_KV)]
    return jnp.concatenate(pieces, axis=1)


ATTN_TILE = 256
CHUNKS_PER_TILE = ATTN_TILE // CHUNK
KEY_SPAN = (N_BACK + 1) * CHUNK


def _attn_prompt_body(sink_ref, q_ref, kp_ref, kc_ref, vp_ref, vc_ref, g_ref, o_ref):
    i = pl.program_id(1)
    kwin = jnp.concatenate([kp_ref[0], kc_ref[0]], axis=0)
    vwin = jnp.concatenate([vp_ref[0], vc_ref[0]], axis=0)
    key_chunk = lax.broadcasted_iota(i32, (1, KEY_SPAN), 1) // CHUNK
    for c in range(CHUNKS_PER_TILE):
        valid = (i * CHUNKS_PER_TILE + c - N_BACK + key_chunk) >= 0
        o = _heads_attend(q_ref[0, c * CHUNK:(c + 1) * CHUNK, :], kwin[c * CHUNK:c * CHUNK + KEY_SPAN],
                          vwin[c * CHUNK:c * CHUNK + KEY_SPAN], sink_ref, valid)
        o_ref[0, c * CHUNK:(c + 1) * CHUNK, :] = _rms(o, g_ref[...]).astype(bf16)


def _attn_prompt(q, k, v, sinks, g):
    b, l, _ = q.shape
    back = N_BACK * CHUNK
    per = ATTN_TILE // back
    prev = pl.BlockSpec((1, back, KV_WIDTH), lambda bi, i: (bi, jnp.maximum(i * per - 1, 0), 0))
    cur = pl.BlockSpec((1, ATTN_TILE, KV_WIDTH), lambda bi, i: (bi, i, 0))
    return pl.pallas_call(
        _attn_prompt_body,
        grid=(b, l // ATTN_TILE),
        in_specs=[pl.BlockSpec(memory_space=pltpu.SMEM),
                  pl.BlockSpec((1, ATTN_TILE, ATTN_WIDTH), lambda bi, i: (bi, i, 0)),
                  prev, cur, prev, cur,
                  pl.BlockSpec((1, ATTN_WIDTH), lambda bi, i: (0, 0))],
        out_specs=pl.BlockSpec((1, ATTN_TILE, ATTN_WIDTH), lambda bi, i: (bi, i, 0)),
        out_shape=jax.ShapeDtypeStruct((b, l, ATTN_WIDTH), bf16),
        compiler_params=_cparams(2),
        name="attn_prompt",
    )(sinks, q, k, k, v, v, g)


def _attn_sample_body(sink_ref, q_ref, kn_ref, vn_ref, ck_ref, cv_ref, g_ref, o_ref, nk_ref, nv_ref):
    kall = jnp.concatenate([ck_ref[0], kn_ref[0]], axis=0)
    vall = jnp.concatenate([cv_ref[0], vn_ref[0]], axis=0)
    o = _heads_attend(q_ref[0], kall, vall, sink_ref, None)
    o_ref[0] = _rms(o, g_ref[...]).astype(bf16)
    n_new = kn_ref.shape[1]
    nk_ref[0] = kall[n_new:]
    nv_ref[0] = vall[n_new:]


def _attn_sample(q, k_new, v_new, cache_k, cache_v, sinks, g):
    b, l, _ = q.shape
    wc = cache_k.shape[1]
    blk = lambda r, n: pl.BlockSpec((1, r, n), lambda bi: (bi, 0, 0))
    return pl.pallas_call(
        _attn_sample_body,
        grid=(b,),
        in_specs=[pl.BlockSpec(memory_space=pltpu.SMEM), blk(l, ATTN_WIDTH), blk(l, KV_WIDTH), blk(l, KV_WIDTH),
                  blk(wc, KV_WIDTH), blk(wc, KV_WIDTH), pl.BlockSpec((1, ATTN_WIDTH), lambda bi: (0, 0))],
        out_specs=[blk(l, ATTN_WIDTH), blk(wc, KV_WIDTH), blk(wc, KV_WIDTH)],
        out_shape=[jax.ShapeDtypeStruct((b, l, ATTN_WIDTH), bf16),
                   jax.ShapeDtypeStruct((b, wc, KV_WIDTH), f32), jax.ShapeDtypeStruct((b, wc, KV_WIDTH), f32)],
        compiler_params=_cparams(),
        name="attn_sample",
    )(sinks, q, k_new, v_new, cache_k, cache_v, g)


def _ssm_weights(a_re, a_im, log_dt, b_re, b_im, c_re, c_im):
    hp = lax.Precision.HIGHEST
    dt = jnp.exp(log_dt)[:, None]
    mag = jnp.exp(a_re * dt)
    ar, ai = mag * jnp.cos(a_im * dt), mag * jnp.sin(a_im * dt)
    den = a_re * a_re + a_im * a_im
    nr, ni = ar - 1.0, ai
    fr = ((nr * a_re + ni * a_im) / den)[..., None]
    fi = ((ni * a_re - nr * a_im) / den)[..., None]
    bbr, bbi = fr * b_re - fi * b_im, fr * b_im + fi * b_re
    pr, pi = [jnp.ones_like(ar)], [jnp.zeros_like(ar)]
    for _ in range(SSM_STEPS):
        pr, pi = pr + [pr[-1] * ar - pi[-1] * ai], pi + [pr[-1] * ai + pi[-1] * ar]
    rev_r, rev_i = jnp.stack(pr[SSM_STEPS - 1::-1]), jnp.stack(pi[SSM_STEPS - 1::-1])
    pr, pi = jnp.stack(pr), jnp.stack(pi)
    cpr = c_re[None] * pr[:SSM_STEPS, :, None, :] - c_im[None] * pi[:SSM_STEPS, :, None, :]
    cpi = c_re[None] * pi[:SSM_STEPS, :, None, :] + c_im[None] * pr[:SSM_STEPS, :, None, :]
    kl = (jnp.einsum("lgdp,gpc->lgcd", cpr, bbr, precision=hp, preferred_element_type=f32)
          - jnp.einsum("lgdp,gpc->lgcd", cpi, bbi, precision=hp, preferred_element_type=f32))
    kl = jnp.concatenate([jnp.zeros_like(kl[:1]), kl], axis=0)
    nd = SSM_STEPS // 2
    lag = 2 * jnp.arange(nd)[:, None, None] + jnp.arange(2)[None, None, :] - jnp.arange(2)[None, :, None]
    ksel = kl[lag + 1]
    ksel = ksel.reshape(nd, 2, 2, SSM_BLOCKS, GROUPS_PER_BLOCK, SSM_GROUP, SSM_GROUP)
    eye = jnp.eye(GROUPS_PER_BLOCK, dtype=f32)
    toep = (ksel.transpose(3, 0, 1, 4, 5, 2, 6)[:, :, :, :, :, :, None, :]
            * eye[None, None, None, :, None, None, :, None]).reshape(SSM_BLOCKS, nd, 2 * LANES, 2 * LANES)
    bbr_t, bbi_t = bbr.transpose(0, 2, 1), bbi.transpose(0, 2, 1)
    wo_r = rev_r[:, :, None, :] * bbr_t[None] - rev_i[:, :, None, :] * bbi_t[None]
    wo_i = rev_r[:, :, None, :] * bbi_t[None] + rev_i[:, :, None, :] * bbr_t[None]
    wo = jnp.stack([wo_r, wo_i]).reshape(2, SSM_STEPS, SSM_BLOCKS, GROUPS_PER_BLOCK, SSM_GROUP, SSM_STATE)
    spread = eye[None, None, :, None, None, :, None]
    wout = (wo.transpose(2, 1, 3, 4, 0, 5)[:, :, :, :, :, None, :] * spread).reshape(
        SSM_BLOCKS, SSM_STEPS * LANES, 2 * STATE_COLS)
    gr = c_re[None] * pr[1:, :, None, :] - c_im[None] * pi[1:, :, None, :]
    gi = c_re[None] * pi[1:, :, None, :] + c_im[None] * pr[1:, :, None, :]
    wi = jnp.stack([gr, -gi]).reshape(2, SSM_STEPS, SSM_BLOCKS, GROUPS_PER_BLOCK, SSM_GROUP, SSM_STATE)
    win = (wi.transpose(2, 0, 3, 5, 1, 4)[:, :, :, :, :, None, :] * spread).reshape(
        SSM_BLOCKS, 2 * STATE_COLS, SSM_STEPS * LANES)
    lr, li = [pr[SSM_STEPS]], [pi[SSM_STEPS]]
    for _ in range(7):
        lr, li = lr + [lr[-1] * lr[-1] - li[-1] * li[-1]], li + [2.0 * lr[-1] * li[-1]]
    lev = jnp.stack([jnp.stack(lr), jnp.stack(li)], axis=1)
    lev = lev.reshape(8, 2, SSM_BLOCKS, STATE_COLS).transpose(2, 0, 1, 3)
    return toep.astype(bf16), wout.astype(bf16), win.astype(bf16), lev


def _ssm_chunk_rows(u_ref, nk):
    xs = [u_ref[0, pl.ds(s, nk, stride=SSM_STEPS), :] for s in range(SSM_STEPS)]
    pairs = [jnp.concatenate([xs[2 * p], xs[2 * p + 1]], axis=1).astype(bf16) for p in range(SSM_STEPS // 2)]
    return xs, pairs


def _ssm_intra(pairs, toep_ref, nk):
    nd = len(pairs)
    y = [None] * nd
    for d in range(nd):
        lhs = jnp.concatenate(pairs[:nd - d], axis=0) if nd - d > 1 else pairs[0]
        r = jnp.dot(lhs, toep_ref[0, d], preferred_element_type=f32)
        for p in range(nd - d):
            blk = r[p * nk:(p + 1) * nk]
            y[p + d] = blk if y[p + d] is None else y[p + d] + blk
    return y


def _shift_rows(x, sh):
    rows = lax.broadcasted_iota(i32, (x.shape[0], 1), 0)
    return jnp.where(rows >= sh, pltpu.roll(x, sh, axis=0), 0.0)


def _ssm_prompt_body(u_ref, toep_ref, wout_ref, win_ref, lev_ref, d_ref, y_ref, hr_ref, hi_ref):
    nk = u_ref.shape[1] // SSM_STEPS
    xs, pairs = _ssm_chunk_rows(u_ref, nk)
    y = _ssm_intra(pairs, toep_ref, nk)
    s = jnp.dot(jnp.concatenate(pairs, axis=1), wout_ref[0], preferred_element_type=f32)
    sr, si = s[:, :STATE_COLS], s[:, STATE_COLS:]
    level = 0
    while (1 << level) < nk:
        ar, ai = lev_ref[0, level, 0:1, :], lev_ref[0, level, 1:2, :]
        tr, ti = _shift_rows(sr, 1 << level), _shift_rows(si, 1 << level)
        sr, si = sr + ar * tr - ai * ti, si + ar * ti + ai * tr
        level += 1
    hr_ref[0, 0] = sr[nk - 1:nk]
    hi_ref[0, 0] = si[nk - 1:nk]
    hprev = jnp.concatenate([_shift_rows(sr, 1), _shift_rows(si, 1)], axis=1)
    y2 = _bdot(hprev, win_ref[0])
    for st in range(SSM_STEPS):
        piece = (y[st // 2][:, (st % 2) * LANES:(st % 2 + 1) * LANES] + y2[:, st * LANES:(st + 1) * LANES]
                 + d_ref[...] * xs[st])
        y_ref[0, pl.ds(st, nk, stride=SSM_STEPS), :] = piece


def _ssm_prompt(u, toep, wout, win, lev, d):
    b, l, _ = u.shape
    wspec = lambda a: pl.BlockSpec((1,) + a.shape[1:], lambda j, bi: (j,) + (0,) * (a.ndim - 1))
    st = pl.BlockSpec((1, 1, 1, STATE_COLS), lambda j, bi: (bi, j, 0, 0))
    return pl.pallas_call(
        _ssm_prompt_body,
        grid=(SSM_BLOCKS, b),
        in_specs=[pl.BlockSpec((1, l, LANES), lambda j, bi: (bi, 0, j)), wspec(toep), wspec(wout), wspec(win), wspec(lev),
                  pl.BlockSpec((1, LANES), lambda j, bi: (0, j))],
        out_specs=[pl.BlockSpec((1, l, LANES), lambda j, bi: (bi, 0, j)), st, st],
        out_shape=[jax.ShapeDtypeStruct((b, l, SSM_WIDTH), f32),
                   jax.ShapeDtypeStruct((b, SSM_BLOCKS, 1, STATE_COLS), f32),
                   jax.ShapeDtypeStruct((b, SSM_BLOCKS, 1, STATE_COLS), f32)],
        compiler_params=_cparams(2),
        name="ssm_prompt",
    )(u, toep, wout, win, lev, d)


def _ssm_sample_body(u_ref, h0r_ref, h0i_ref, toep_ref, wout_ref, win_ref, lev_ref, d_ref, y_ref, hr_ref, hi_ref):
    nk = u_ref.shape[1] // SSM_STEPS
    xs, pairs = _ssm_chunk_rows(u_ref, nk)
    y = _ssm_intra(pairs, toep_ref, nk)
    s = jnp.dot(jnp.concatenate(pairs, axis=1), wout_ref[0], preferred_element_type=f32)
    h0r, h0i = h0r_ref[0], h0i_ref[0]
    ar, ai = lev_ref[0, 0, 0:1, :], lev_ref[0, 0, 1:2, :]
    hr_ref[0] = s[:, :STATE_COLS] + ar * h0r - ai * h0i
    hi_ref[0] = s[:, STATE_COLS:] + ar * h0i + ai * h0r
    y2 = _bdot(jnp.concatenate([h0r, h0i], axis=1), win_ref[0])
    for st in range(SSM_STEPS):
        piece = (y[st // 2][:, (st % 2) * LANES:(st % 2 + 1) * LANES] + y2[:, st * LANES:(st + 1) * LANES]
                 + d_ref[...] * xs[st])
        y_ref[0, pl.ds(st, nk, stride=SSM_STEPS), :] = piece


def _ssm_sample(u, h0r, h0i, toep, wout, win, lev, d):
    _, rows, _ = u.shape
    nb = rows // SSM_STEPS
    wspec = lambda a: pl.BlockSpec((1,) + a.shape[1:], lambda j: (j,) + (0,) * (a.ndim - 1))
    st = pl.BlockSpec((1, nb, STATE_COLS), lambda j: (j, 0, 0))
    return pl.pallas_call(
        _ssm_sample_body,
        grid=(SSM_BLOCKS,),
        in_specs=[pl.BlockSpec((1, rows, LANES), lambda j: (0, 0, j)), st, st, wspec(toep), wspec(wout), wspec(win),
                  wspec(lev), pl.BlockSpec((1, LANES), lambda j: (0, j))],
        out_specs=[pl.BlockSpec((1, rows, LANES), lambda j: (0, 0, j)), st, st],
        out_shape=[jax.ShapeDtypeStruct((1, rows, SSM_WIDTH), f32),
                   jax.ShapeDtypeStruct((SSM_BLOCKS, nb, STATE_COLS), f32),
                   jax.ShapeDtypeStruct((SSM_BLOCKS, nb, STATE_COLS), f32)],
        compiler_params=_cparams(),
        name="ssm_sample",
    )(u, h0r, h0i, toep, wout, win, lev, d)


ROUTER_COLS = LANES


def _mix_body(a_ref, ys_ref, x_ref, carry_ref, wglu_ref, bglu_ref, gs_ref, wout_ref, gf_ref, wr_ref, br_ref, tri_ref,
              x1_ref, hf_ref, wts_ref, idx_ref, cnt_ref, cnt_scr):
    @pl.when(pl.program_id(0) == 0)
    def _():
        cnt_scr[...] = carry_ref[...]

    y = ys_ref[...]
    y = 0.5 * y * (1.0 + jnp.tanh(math.sqrt(2.0 / math.pi) * (y + 0.044715 * (y * y * y))))
    y = y * jax.nn.sigmoid(_bdot(y, wglu_ref[...]) + bglu_ref[...])
    cat = jnp.concatenate([a_ref[...], _rms(y, gs_ref[...]).astype(bf16)], axis=1)
    x1 = x_ref[...] + jnp.dot(cat, wout_ref[...], preferred_element_type=f32)
    x1_ref[...] = x1
    hf = _rms(x1, gf_ref[...])
    hf_ref[...] = hf

    logits = _bdot(hf, wr_ref[...]) + br_ref[...]
    le = logits[:, :N_EXPERTS]
    lg = logits[:, N_EXPERTS:N_EXPERTS + N_EXPERT_GROUPS]
    tm = le.shape[0]
    gmax = jnp.max(lg, axis=-1, keepdims=True)
    gi = lax.broadcasted_iota(i32, (tm, N_EXPERT_GROUPS), 1).astype(f32)
    gsel = jnp.min(jnp.where(lg == gmax, gi, float(N_EXPERT_GROUPS)), axis=-1, keepdims=True)
    pg = 1.0 / jnp.sum(jnp.exp(lg - gmax), axis=-1, keepdims=True)
    ei_int = lax.broadcasted_iota(i32, (tm, N_EXPERTS), 1)
    ei = ei_int.astype(f32)
    egroup = jnp.right_shift(ei_int, int(math.log2(EXPERTS_PER_GROUP))).astype(f32)
    lm = jnp.where(egroup == gsel, le, NEG)
    v1 = jnp.max(lm, axis=-1, keepdims=True)
    i1 = jnp.min(jnp.where(lm == v1, ei, float(N_EXPERTS)), axis=-1, keepdims=True)
    lm2 = jnp.where(ei == i1, NEG, lm)
    v2 = jnp.max(lm2, axis=-1, keepdims=True)
    i2 = jnp.min(jnp.where(lm2 == v2, ei, float(N_EXPERTS)), axis=-1, keepdims=True)
    ex = jnp.exp(v2 - v1)
    wts_ref[...] = jnp.concatenate([pg / (1.0 + ex), pg * ex / (1.0 + ex)], axis=1)

    oh1 = (ei == i1).astype(f32)
    oh2 = (ei == i2).astype(f32)
    both = oh1 + oh2
    before = jnp.dot(tri_ref[...], both.astype(bf16), preferred_element_type=f32) + cnt_scr[...]
    r1 = jnp.sum(oh1 * before, axis=-1, keepdims=True)
    r2 = jnp.sum(oh2 * before, axis=-1, keepdims=True)
    idx_ref[...] = jnp.concatenate([i1, i2, r1, r2], axis=1).astype(i32)
    cnt_scr[...] = cnt_scr[...] + jnp.sum(both, axis=0, keepdims=True)
    cnt_ref[...] = cnt_scr[...]


def _mix(a, ys, x, carry, wglu, bglu, gs, wout, gf, wr, br, tri, tm):
    t = x.shape[0]
    row = lambda n: pl.BlockSpec((tm, n), lambda i: (i, 0))
    full = lambda arr: pl.BlockSpec(arr.shape, lambda i: (0,) * arr.ndim)
    return pl.pallas_call(
        _mix_body,
        grid=(t // tm,),
        in_specs=[row(ATTN_WIDTH), row(SSM_WIDTH), row(D_MODEL), full(carry), full(wglu), full(bglu), full(gs),
                  full(wout), full(gf), full(wr), full(br), full(tri)],
        out_specs=[row(D_MODEL), row(D_MODEL), row(2), row(4), pl.BlockSpec((1, N_EXPERTS), lambda i: (0, 0))],
        out_shape=[jax.ShapeDtypeStruct((t, D_MODEL), f32), jax.ShapeDtypeStruct((t, D_MODEL), f32),
                   jax.ShapeDtypeStruct((t, 2), f32), jax.ShapeDtypeStruct((t, 4), i32),
                   jax.ShapeDtypeStruct((1, N_EXPERTS), f32)],
        scratch_shapes=[pltpu.VMEM((1, N_EXPERTS), f32)],
        compiler_params=_cparams(),
        name="mix",
    )(a, ys, x, carry, wglu, bglu, gs, wout, gf, wr, br, tri)


def _row_copy(src, s_row, dst, d_row, sem):
    return pltpu.make_async_copy(src.at[pl.ds(s_row, 1), :], dst.at[pl.ds(d_row, 1), :], sem)


SUBLANES = 8
PAD_PIECES = tuple(1 << b for b in reversed(range(int(math.log2(SUBLANES)), int(math.log2(ROW_TILE)))))


def _zero_fill(zs_ref, zn_ref, zbuf, xs_hbm, sem, issue):
    def go(cp):
        cp.start() if issue else cp.wait()

    def segment(e, c):
        n, start = zn_ref[e], zs_ref[e]
        head = n & (SUBLANES - 1)
        for r in range(SUBLANES - 1):
            @pl.when(r < head)
            def _(r=r):
                go(_row_copy(zbuf, 0, xs_hbm, start + r, sem))
        off = start + head
        for piece in PAD_PIECES:
            @pl.when((n & piece) != 0)
            def _(off=off, piece=piece):
                go(pltpu.make_async_copy(zbuf.at[pl.ds(0, piece), :],
                                         xs_hbm.at[pl.ds(pl.multiple_of(off, SUBLANES), piece), :], sem))
            off = off + (n & piece)
        return c
    lax.fori_loop(0, N_EXPERTS, segment, 0)

    def tail(t, c):
        first = pl.multiple_of(zs_ref[N_EXPERTS] + t * ROW_TILE, ROW_TILE)
        go(pltpu.make_async_copy(zbuf, xs_hbm.at[pl.ds(first, ROW_TILE), :], sem))
        return c
    lax.fori_loop(0, zn_ref[N_EXPERTS], tail, 0)


def _scatter_body(zs_ref, zn_ref, p1_ref, p2_ref, hp_ref, hs_ref, xs_hbm, zbuf, sem, *, prompt_tiles):
    i = pl.program_id(0)

    @pl.when(i == pl.num_programs(0) - 1)
    def _():
        zbuf[...] = jnp.zeros_like(zbuf)
        _zero_fill(zs_ref, zn_ref, zbuf, xs_hbm, sem, True)
        _zero_fill(zs_ref, zn_ref, zbuf, xs_hbm, sem, False)

    def scatter_from(src):
        def issue(r, c):
            _row_copy(src, r, xs_hbm, p1_ref[0, 0, r], sem).start()
            _row_copy(src, r, xs_hbm, p2_ref[0, 0, r], sem).start()
            return c
        lax.fori_loop(0, ROW_TILE, issue, 0)

        def drain(r, c):
            _row_copy(src, 0, xs_hbm, 0, sem).wait()
            _row_copy(src, 0, xs_hbm, 0, sem).wait()
            return c
        lax.fori_loop(0, ROW_TILE, drain, 0)

    @pl.when(i < prompt_tiles)
    def _():
        scatter_from(hp_ref)

    @pl.when(i >= prompt_tiles)
    def _():
        scatter_from(hs_ref)


def _scatter(zero_start, zero_count, pos1, pos2, hf_p, hf_s, total_rows):
    prompt_tiles = hf_p.shape[0] // ROW_TILE
    tiles = prompt_tiles + hf_s.shape[0] // ROW_TILE
    smem = pl.BlockSpec((1, 1, ROW_TILE), lambda i, zs, zn: (i, 0, 0), memory_space=pltpu.SMEM)
    return pl.pallas_call(
        functools.partial(_scatter_body, prompt_tiles=prompt_tiles),
        grid_spec=pltpu.PrefetchScalarGridSpec(
            num_scalar_prefetch=2,
            grid=(tiles,),
            in_specs=[smem, smem,
                      pl.BlockSpec((ROW_TILE, D_MODEL), lambda i, zs, zn: (jnp.minimum(i, prompt_tiles - 1), 0)),
                      pl.BlockSpec((ROW_TILE, D_MODEL), lambda i, zs, zn: (jnp.maximum(i - prompt_tiles, 0), 0))],
            out_specs=pl.BlockSpec(memory_space=pl.ANY),
            scratch_shapes=[pltpu.VMEM((ROW_TILE, D_MODEL), f32), pltpu.SemaphoreType.DMA]),
        out_shape=jax.ShapeDtypeStruct((total_rows, D_MODEL), f32),
        compiler_params=_cparams(),
        name="moe_scatter",
    )(zero_start, zero_count, pos1, pos2, hf_p, hf_s)


def _experts_body(te_ref, tr_ref, ts_ref, x_ref, wg_ref, wu_ref, wd_ref, o_ref):
    n_valid = tr_ref[pl.program_id(0)]

    @pl.when(n_valid > 0)
    def _():
        x = x_ref[...].astype(bf16)
        hg = jnp.dot(x, wg_ref[0].astype(bf16), preferred_element_type=f32)
        hu = jnp.dot(x, wu_ref[0].astype(bf16), preferred_element_type=f32)
        o_ref[...] = _bdot(hg * jax.nn.sigmoid(hg) * hu, wd_ref[0])

    @pl.when(n_valid == 0)
    def _():
        o_ref[...] = jnp.zeros_like(o_ref)


def _experts(tile_expert, tile_rows, tile_src, xs, wg, wu, wd):
    tiles = xs.shape[0] // ROW_TILE
    wspec = lambda a: pl.BlockSpec((1,) + a.shape[1:], lambda i, te, tr, ts: (te[i], 0, 0))
    return pl.pallas_call(
        _experts_body,
        grid_spec=pltpu.PrefetchScalarGridSpec(
            num_scalar_prefetch=3,
            grid=(tiles,),
            in_specs=[pl.BlockSpec((ROW_TILE, D_MODEL), lambda i, te, tr, ts: (ts[i], 0)), wspec(wg), wspec(wu), wspec(wd)],
            out_specs=pl.BlockSpec((ROW_TILE, D_MODEL), lambda i, te, tr, ts: (i, 0))),
        out_shape=jax.ShapeDtypeStruct(xs.shape, f32),
        compiler_params=_cparams(),
        name="moe_experts",
    )(tile_expert, tile_rows, tile_src, xs, wg, wu, wd)


def _combine_body(p1_ref, p2_ref, ys_hbm, xp_ref, xs_ref, wp_ref, ws_ref, g_ref, op_ref, os_ref, buf1, buf2, sem,
                  *, prompt_tiles):
    i = pl.program_id(0)

    def issue(r, c):
        _row_copy(ys_hbm, p1_ref[0, 0, r], buf1, r, sem).start()
        _row_copy(ys_hbm, p2_ref[0, 0, r], buf2, r, sem).start()
        return c
    lax.fori_loop(0, ROW_TILE, issue, 0)

    def drain(r, c):
        _row_copy(ys_hbm, 0, buf1, 0, sem).wait()
        _row_copy(ys_hbm, 0, buf2, 0, sem).wait()
        return c
    lax.fori_loop(0, ROW_TILE, drain, 0)

    def finish(x_ref, w_ref, o_ref):
        w = w_ref[...]
        o_ref[...] = _rms(x_ref[...] + (w[:, 0:1] * buf1[...] + w[:, 1:2] * buf2[...]), g_ref[...])

    @pl.when(i < prompt_tiles)
    def _():
        finish(xp_ref, wp_ref, op_ref)

    @pl.when(i >= prompt_tiles)
    def _():
        finish(xs_ref, ws_ref, os_ref)


def _combine(pos1, pos2, ys, x1_p, x1_s, wts_p, wts_s, g):
    prompt_tiles = x1_p.shape[0] // ROW_TILE
    tiles = prompt_tiles + x1_s.shape[0] // ROW_TILE
    smem = pl.BlockSpec((1, 1, ROW_TILE), lambda i: (i, 0, 0), memory_space=pltpu.SMEM)
    p_rows = lambda n: pl.BlockSpec((ROW_TILE, n), lambda i: (jnp.minimum(i, prompt_tiles - 1), 0))
    s_rows = lambda n: pl.BlockSpec((ROW_TILE, n), lambda i: (jnp.maximum(i - prompt_tiles, 0), 0))
    return pl.pallas_call(
        functools.partial(_combine_body, prompt_tiles=prompt_tiles),
        grid=(tiles,),
        in_specs=[smem, smem, pl.BlockSpec(memory_space=pl.ANY), p_rows(D_MODEL), s_rows(D_MODEL), p_rows(2), s_rows(2),
                  pl.BlockSpec((1, D_MODEL), lambda i: (0, 0))],
        out_specs=[p_rows(D_MODEL), s_rows(D_MODEL)],
        out_shape=[jax.ShapeDtypeStruct(x1_p.shape, f32), jax.ShapeDtypeStruct(x1_s.shape, f32)],
        scratch_shapes=[pltpu.VMEM((ROW_TILE, D_MODEL), f32), pltpu.VMEM((ROW_TILE, D_MODEL), f32),
                        pltpu.SemaphoreType.DMA],
        compiler_params=_cparams(),
        name="moe_combine",
    )(pos1, pos2, ys, x1_p, x1_s, wts_p, wts_s, g)


def _expert_layout(counts, idx):
    padded = (counts + ROW_TILE - 1) // ROW_TILE * ROW_TILE
    e = jnp.arange(N_EXPERTS)
    ends = jnp.sum(jnp.where(e[None, :] <= e[:, None], padded[None, :], 0), axis=1)
    starts = ends - padded
    onehot = lambda e: (e[:, None] == jnp.arange(N_EXPERTS)[None, :]).astype(i32)
    pos1 = jnp.sum(onehot(idx[:, 0]) * starts[None, :], axis=1) + idx[:, 2]
    pos2 = jnp.sum(onehot(idx[:, 1]) * starts[None, :], axis=1) + idx[:, 3]
    return starts, ends, pos1, pos2


def _tile_tables(counts, starts, ends, tiles):
    first = jnp.arange(tiles, dtype=i32) * ROW_TILE
    used = first < ends[-1]
    expert = jnp.minimum(jnp.sum((first[:, None] >= ends[None, :]).astype(i32), axis=1), N_EXPERTS - 1)
    rows = jnp.clip(jnp.sum(jnp.where(expert[:, None] == jnp.arange(N_EXPERTS)[None, :], (starts + counts)[None, :], 0),
                            axis=1) - first, 0, ROW_TILE)
    rows = jnp.where(used, rows, 0).astype(i32)
    last = ends[-1] // ROW_TILE - 1
    src = jnp.minimum(jnp.arange(tiles, dtype=i32), last).astype(i32)
    return expert.astype(i32), rows, src


def kernel(x_prompt, x_sample, cache_k, cache_v, state_ssm_re, state_ssm_im, g_norm_mix, w_in, attn_sinks, ssm_a_re,
           ssm_a_im, ssm_log_dt, ssm_b_re, ssm_b_im, ssm_c_re, ssm_c_im, ssm_d, w_glu, b_glu, g_attn_out, g_ssm_out,
           w_out, g_norm_ffn, w_router_group, b_router_group, w_router_expert, b_router_expert, w_exp_gate, w_exp_up,
           w_exp_down, g_final):
    bp, lp, _ = x_prompt.shape
    bs, ls, _ = x_sample.shape
    depth = w_in.shape[0]
    assert depth == 1 and ls == SSM_STEPS and lp % ATTN_TILE == 0 and (bs * ls) % ROW_TILE == 0
    tp, ts = bp * lp, bs * ls
    wc = cache_k.shape[2]
    row2 = lambda v: v.reshape(1, -1)

    xp = x_prompt.reshape(tp, D_MODEL)
    xs = x_sample.reshape(ts, D_MODEL)
    w_in_bf = w_in[0].astype(bf16)
    qp, kp, vp, up = _proj(xp, row2(g_norm_mix[0]), w_in_bf, 512)
    qs, kq, vq, us = _proj(xs, row2(g_norm_mix[0]), w_in_bf, ts)

    sinks = attn_sinks[0]
    g_att = row2(g_attn_out[0])
    ap = _attn_prompt(qp.reshape(bp, lp, -1), kp.reshape(bp, lp, -1), vp.reshape(bp, lp, -1), sinks, g_att)
    a_s, k_roll, v_roll = _attn_sample(qs.reshape(bs, ls, -1), kq.reshape(bs, ls, -1), vq.reshape(bs, ls, -1),
                                       cache_k[0].reshape(bs, wc, KV_WIDTH), cache_v[0].reshape(bs, wc, KV_WIDTH),
                                       sinks, g_att)

    toep, wso, wsi, lev = _ssm_weights(ssm_a_re[0], ssm_a_im[0], ssm_log_dt[0], ssm_b_re[0], ssm_b_im[0],
                                       ssm_c_re[0], ssm_c_im[0])
    d_row = row2(ssm_d[0])
    yp, hrp, hip = _ssm_prompt(up.reshape(bp, lp, -1), toep, wso, wsi, lev, d_row)
    to_blocks = lambda h: h.reshape(bs, SSM_BLOCKS, STATE_COLS).transpose(1, 0, 2)
    from_blocks = lambda h: h.transpose(1, 0, 2).reshape(bs, SSM_GROUPS, SSM_STATE)
    ysm, hrs, his = _ssm_sample(us.reshape(1, ts, -1), to_blocks(state_ssm_re[0]), to_blocks(state_ssm_im[0]),
                                toep, wso, wsi, lev, d_row)

    wr = jnp.zeros((D_MODEL, ROUTER_COLS), f32)
    wr = wr.at[:, :N_EXPERTS].set(w_router_expert[0]).at[:, N_EXPERTS:N_EXPERTS + N_EXPERT_GROUPS].set(w_router_group[0])
    br = jnp.zeros((1, ROUTER_COLS), f32)
    br = br.at[0, :N_EXPERTS].set(b_router_expert[0]).at[0, N_EXPERTS:N_EXPERTS + N_EXPERT_GROUPS].set(b_router_group[0])
    mix_w = (w_glu[0].astype(bf16), row2(b_glu[0]), row2(g_ssm_out[0]), w_out[0].astype(bf16), row2(g_norm_ffn[0]),
             wr.astype(bf16), br)
    tri = jnp.tril(jnp.ones((512, 512), bf16), -1)
    x1p, hfp, wtp, idxp, cnt = _mix(ap.reshape(tp, -1), yp.reshape(tp, -1), xp, jnp.zeros((1, N_EXPERTS), f32),
                                    *mix_w, tri, 512)
    x1s, hfs, wts, idxs, cnt = _mix(a_s.reshape(ts, -1), ysm.reshape(ts, -1), xs, cnt, *mix_w, tri[:ts, :ts], ts)

    counts = cnt[0].astype(i32)
    idx = jnp.concatenate([idxp, idxs], axis=0)
    starts, ends, pos1, pos2 = _expert_layout(counts, idx)
    tiles = (2 * (tp + ts)) // ROW_TILE + N_EXPERTS
    tile_expert, tile_rows, tile_src = _tile_tables(counts, starts, ends, tiles)
    pos1 = pos1.reshape(-1, 1, ROW_TILE)
    pos2 = pos2.reshape(-1, 1, ROW_TILE)
    zero_start = jnp.concatenate([starts + counts, ends[-1:]]).astype(i32)
    zero_count = jnp.concatenate([ends - starts - counts, tiles - ends[-1:] // ROW_TILE]).astype(i32)
    sorted_rows = _scatter(zero_start, zero_count, pos1, pos2, hfp, hfs, tiles * ROW_TILE)
    expert_out = _experts(tile_expert, tile_rows, tile_src, sorted_rows, w_exp_gate[0], w_exp_up[0], w_exp_down[0])
    y_p, y_s = _combine(pos1, pos2, expert_out, x1p, x1s, wtp, wts, row2(g_final))

    kvshape = lambda a, b: a.reshape(1, b, -1, N_KV_HEADS, HEAD_DIM)
    block_state = lambda h: h.reshape(bp, SSM_GROUPS, SSM_STATE)[None]
    wcp = min(WINDOW, lp)
    return (y_p.reshape(bp, lp, D_MODEL), y_s.reshape(bs, ls, D_MODEL),
            kvshape(kp.reshape(bp, lp, -1)[:, lp - wcp:], bp), kvshape(vp.reshape(bp, lp, -1)[:, lp - wcp:], bp),
            block_state(hrp), block_state(hip),
            kvshape(k_roll, bs), kvshape(v_roll, bs),
            from_blocks(hrs)[None], from_blocks(his)[None])
```

```python
import functools
import math

import jax
import jax.numpy as jnp
from jax import lax
from jax.experimental import pallas as pl
from jax.experimental.pallas import tpu as pltpu

f32, bf16, i32 = jnp.float32, jnp.bfloat16, jnp.int32

D_MODEL = 1024
CHUNK = 64
N_BACK = 2
WINDOW = 128
ATTN_WIDTH = 512
HEAD_DIM = 64
N_KV_HEADS = 2
Q_PER_KV = 4
KV_WIDTH = 128
SSM_WIDTH = 512
SSM_GROUP = 16
SSM_GROUPS = 32
SSM_STATE = 64
PROJ_WIDTH = 1280
N_EXPERT_GROUPS = 4
EXPERTS_PER_GROUP = 8
N_EXPERTS = 32
D_EXPERT = 512
EPS = 1e-6
NEG = -1e30

LANES = 128
SSM_STEPS = 16
SSM_BLOCKS = SSM_WIDTH // LANES
GROUPS_PER_BLOCK = LANES // SSM_GROUP
STATE_COLS = GROUPS_PER_BLOCK * SSM_STATE
ROW_TILE = 256
VMEM_LIMIT = 56 * 1024 * 1024


def _cparams(n_axes=1, limit=VMEM_LIMIT):
    return pltpu.CompilerParams(dimension_semantics=("arbitrary",) * n_axes, vmem_limit_bytes=limit)


def _rms(x, g):
    return x * lax.rsqrt(jnp.mean(x * x, axis=-1, keepdims=True) + EPS) * g


def _bdot(a, b):
    return jnp.dot(a.astype(bf16), b.astype(bf16), preferred_element_type=f32)


def _proj_body(x_ref, g_ref, w_ref, q_ref, k_ref, v_ref, u_ref):
    h = _rms(x_ref[...], g_ref[...])
    z = _bdot(h, w_ref[...])
    q_ref[...] = z[:, :ATTN_WIDTH] * (HEAD_DIM ** -0.5)
    k_ref[...] = z[:, ATTN_WIDTH:ATTN_WIDTH + KV_WIDTH]
    v_ref[...] = z[:, ATTN_WIDTH + KV_WIDTH:ATTN_WIDTH + 2 * KV_WIDTH]
    u_ref[...] = z[:, ATTN_WIDTH + 2 * KV_WIDTH:]


def _proj(x2d, g, w_bf, tm):
    t = x2d.shape[0]
    row = lambda n: pl.BlockSpec((tm, n), lambda i: (i, 0))
    full = lambda a: pl.BlockSpec(a.shape, lambda i: (0,) * a.ndim)
    return pl.pallas_call(
        _proj_body,
        grid=(t // tm,),
        in_specs=[row(D_MODEL), full(g), full(w_bf)],
        out_specs=[row(ATTN_WIDTH), row(KV_WIDTH), row(KV_WIDTH), row(SSM_WIDTH)],
        out_shape=[jax.ShapeDtypeStruct((t, n), f32) for n in (ATTN_WIDTH, KV_WIDTH, KV_WIDTH, SSM_WIDTH)],
        compiler_params=_cparams(),
        name="proj",
    )(x2d, g, w_bf)


def _sink_column(sink_ref, kv, rows_per_head):
    r = lax.broadcasted_iota(i32, (Q_PER_KV * rows_per_head, 1), 0)
    col = jnp.full((Q_PER_KV * rows_per_head, 1), sink_ref[kv * Q_PER_KV], f32)
    for j in range(1, Q_PER_KV):
        col = jnp.where(r >= j * rows_per_head, sink_ref[kv * Q_PER_KV + j], col)
    return col


def _attend(qs, kc, vc, sink_col, valid):
    s = lax.dot_general(qs.astype(bf16), kc.astype(bf16), (((1,), (1,)), ((), ())), preferred_element_type=f32)
    if valid is not None:
        s = jnp.where(valid, s, NEG)
    m = jnp.maximum(jnp.max(s, axis=-1, keepdims=True), sink_col)
    p = jnp.exp(s - m)
    denom = jnp.sum(p, axis=-1, keepdims=True) + jnp.exp(sink_col - m)
    return _bdot(p, vc) / denom


def _heads_attend(q, k, v, sink_ref, valid):
    rows = q.shape[0]
    pieces = []
    for kv in range(N_KV_HEADS):
        qs = jnp.concatenate(
            [q[:, (kv * Q_PER_KV + j) * HEAD_DIM:(kv * Q_PER_KV + j + 1) * HEAD_DIM] for j in range(Q_PER_KV)], axis=0)
        o = _attend(qs, k[:, kv * HEAD_DIM:(kv + 1) * HEAD_DIM], v[:, kv * HEAD_DIM:(kv + 1) * HEAD_DIM],
                    _sink_column(sink_ref, kv, rows), valid)
        pieces += [o[j * rows:(j + 1) * rows] for j in range(Q_PER_KV)]
    return jnp.concatenate(pieces, axis=1)


ATTN_TILE = 256
CHUNKS_PER_TILE = ATTN_TILE // CHUNK
KEY_SPAN = (N_BACK + 1) * CHUNK


def _attn_prompt_body(sink_ref, q_ref, kp_ref, kc_ref, vp_ref, vc_ref, g_ref, o_ref):
    i = pl.program_id(1)
    kwin = jnp.concatenate([kp_ref[0], kc_ref[0]], axis=0)
    vwin = jnp.concatenate([vp_ref[0], vc_ref[0]], axis=0)
    key_chunk = lax.broadcasted_iota(i32, (1, KEY_SPAN), 1) // CHUNK
    for c in range(CHUNKS_PER_TILE):
        valid = (i * CHUNKS_PER_TILE + c - N_BACK + key_chunk) >= 0
        o = _heads_attend(q_ref[0, c * CHUNK:(c + 1) * CHUNK, :], kwin[c * CHUNK:c * CHUNK + KEY_SPAN],
                          vwin[c * CHUNK:c * CHUNK + KEY_SPAN], sink_ref, valid)
        o_ref[0, c * CHUNK:(c + 1) * CHUNK, :] = _rms(o, g_ref[...]).astype(bf16)


def _attn_prompt(q, k, v, sinks, g):
    b, l, _ = q.shape
    back = N_BACK * CHUNK
    per = ATTN_TILE // back
    prev = pl.BlockSpec((1, back, KV_WIDTH), lambda bi, i: (bi, jnp.maximum(i * per - 1, 0), 0))
    cur = pl.BlockSpec((1, ATTN_TILE, KV_WIDTH), lambda bi, i: (bi, i, 0))
    return pl.pallas_call(
        _attn_prompt_body,
        grid=(b, l // ATTN_TILE),
        in_specs=[pl.BlockSpec(memory_space=pltpu.SMEM),
                  pl.BlockSpec((1, ATTN_TILE, ATTN_WIDTH), lambda bi, i: (bi, i, 0)),
                  prev, cur, prev, cur,
                  pl.BlockSpec((1, ATTN_WIDTH), lambda bi, i: (0, 0))],
        out_specs=pl.BlockSpec((1, ATTN_TILE, ATTN_WIDTH), lambda bi, i: (bi, i, 0)),
        out_shape=jax.ShapeDtypeStruct((b, l, ATTN_WIDTH), bf16),
        compiler_params=_cparams(2),
        name="attn_prompt",
    )(sinks, q, k, k, v, v, g)


def _attn_sample_body(sink_ref, q_ref, kn_ref, vn_ref, ck_ref, cv_ref, g_ref, o_ref, nk_ref, nv_ref):
    kall = jnp.concatenate([ck_ref[0], kn_ref[0]], axis=0)
    vall = jnp.concatenate([cv_ref[0], vn_ref[0]], axis=0)
    o = _heads_attend(q_ref[0], kall, vall, sink_ref, None)
    o_ref[0] = _rms(o, g_ref[...]).astype(bf16)
    n_new = kn_ref.shape[1]
    nk_ref[0] = kall[n_new:]
    nv_ref[0] = vall[n_new:]


def _attn_sample(q, k_new, v_new, cache_k, cache_v, sinks, g):
    b, l, _ = q.shape
    wc = cache_k.shape[1]
    blk = lambda r, n: pl.BlockSpec((1, r, n), lambda bi: (bi, 0, 0))
    return pl.pallas_call(
        _attn_sample_body,
        grid=(b,),
        in_specs=[pl.BlockSpec(memory_space=pltpu.SMEM), blk(l, ATTN_WIDTH), blk(l, KV_WIDTH), blk(l, KV_WIDTH),
                  blk(wc, KV_WIDTH), blk(wc, KV_WIDTH), pl.BlockSpec((1, ATTN_WIDTH), lambda bi: (0, 0))],
        out_specs=[blk(l, ATTN_WIDTH), blk(wc, KV_WIDTH), blk(wc, KV_WIDTH)],
        out_shape=[jax.ShapeDtypeStruct((b, l, ATTN_WIDTH), bf16),
                   jax.ShapeDtypeStruct((b, wc, KV_WIDTH), f32), jax.ShapeDtypeStruct((b, wc, KV_WIDTH), f32)],
        compiler_params=_cparams(),
        name="attn_sample",
    )(sinks, q, k_new, v_new, cache_k, cache_v, g)


def _ssm_weights(a_re, a_im, log_dt, b_re, b_im, c_re, c_im):
    hp = lax.Precision.HIGHEST
    dt = jnp.exp(log_dt)[:, None]
    mag = jnp.exp(a_re * dt)
    ar, ai = mag * jnp.cos(a_im * dt), mag * jnp.sin(a_im * dt)
    den = a_re * a_re + a_im * a_im
    nr, ni = ar - 1.0, ai
    fr = ((nr * a_re + ni * a_im) / den)[..., None]
    fi = ((ni * a_re - nr * a_im) / den)[..., None]
    bbr, bbi = fr * b_re - fi * b_im, fr * b_im + fi * b_re
    pr, pi = [jnp.ones_like(ar)], [jnp.zeros_like(ar)]
    for _ in range(SSM_STEPS):
        pr, pi = pr + [pr[-1] * ar - pi[-1] * ai], pi + [pr[-1] * ai + pi[-1] * ar]
    rev_r, rev_i = jnp.stack(pr[SSM_STEPS - 1::-1]), jnp.stack(pi[SSM_STEPS - 1::-1])
    pr, pi = jnp.stack(pr), jnp.stack(pi)
    cpr = c_re[None] * pr[:SSM_STEPS, :, None, :] - c_im[None] * pi[:SSM_STEPS, :, None, :]
    cpi = c_re[None] * pi[:SSM_STEPS, :, None, :] + c_im[None] * pr[:SSM_STEPS, :, None, :]
    kl = (jnp.einsum("lgdp,gpc->lgcd", cpr, bbr, precision=hp, preferred_element_type=f32)
          - jnp.einsum("lgdp,gpc->lgcd", cpi, bbi, precision=hp, preferred_element_type=f32))
    kl = jnp.concatenate([jnp.zeros_like(kl[:1]), kl], axis=0)
    nd = SSM_STEPS // 2
    lag = 2 * jnp.arange(nd)[:, None, None] + jnp.arange(2)[None, None, :] - jnp.arange(2)[None, :, None]
    ksel = kl[lag + 1]
    ksel = ksel.reshape(nd, 2, 2, SSM_BLOCKS, GROUPS_PER_BLOCK, SSM_GROUP, SSM_GROUP)
    toep = ksel.transpose(3, 0, 1, 4, 5, 2, 6).reshape(SSM_BLOCKS, nd, 2 * LANES, 2 * SSM_GROUP)
    bbr_t, bbi_t = bbr.transpose(0, 2, 1), bbi.transpose(0, 2, 1)
    wo_r = rev_r[:, :, None, :] * bbr_t[None] - rev_i[:, :, None, :] * bbi_t[None]
    wo_i = rev_r[:, :, None, :] * bbi_t[None] + rev_i[:, :, None, :] * bbr_t[None]
    wo = jnp.stack([wo_r, wo_i]).reshape(2, SSM_STEPS, SSM_BLOCKS, GROUPS_PER_BLOCK, SSM_GROUP, SSM_STATE)
    wout = wo.transpose(2, 1, 3, 4, 0, 5).reshape(SSM_BLOCKS, SSM_STEPS * LANES, 2 * SSM_STATE)
    gr = c_re[None] * pr[1:, :, None, :] - c_im[None] * pi[1:, :, None, :]
    gi = c_re[None] * pi[1:, :, None, :] + c_im[None] * pr[1:, :, None, :]
    wi = jnp.stack([gr, -gi]).reshape(2, SSM_STEPS, SSM_BLOCKS, GROUPS_PER_BLOCK, SSM_GROUP, SSM_STATE)
    win = wi.transpose(2, 0, 3, 5, 1, 4).reshape(SSM_BLOCKS, 2 * STATE_COLS, SSM_STEPS * SSM_GROUP)
    lr, li = [pr[SSM_STEPS]], [pi[SSM_STEPS]]
    for _ in range(7):
        lr, li = lr + [lr[-1] * lr[-1] - li[-1] * li[-1]], li + [2.0 * lr[-1] * li[-1]]
    lev = jnp.stack([jnp.stack(lr), jnp.stack(li)], axis=1)
    lev = lev.reshape(8, 2, SSM_BLOCKS, STATE_COLS).transpose(2, 0, 1, 3)
    return toep.astype(bf16), wout.astype(bf16), win.astype(bf16), lev


def _ssm_chunk_rows(u_ref, nk):
    xs = [u_ref[0, pl.ds(s, nk, stride=SSM_STEPS), :] for s in range(SSM_STEPS)]
    pairs = [jnp.concatenate([xs[2 * p], xs[2 * p + 1]], axis=1).astype(bf16) for p in range(SSM_STEPS // 2)]
    return xs, pairs


def _ssm_intra(pairs, toep_ref, nk):
    nd = len(pairs)
    y = [None] * nd
    for d in range(nd):
        lhs = jnp.concatenate(pairs[:nd - d], axis=0) if nd - d > 1 else pairs[0]
        r = jnp.dot(lhs, toep_ref[d], preferred_element_type=f32)
        for p in range(nd - d):
            blk = r[p * nk:(p + 1) * nk]
            y[p + d] = blk if y[p + d] is None else y[p + d] + blk
    return y


def _shift_rows(x, sh):
    rows = lax.broadcasted_iota(i32, (x.shape[0], 1), 0)
    return jnp.where(rows >= sh, pltpu.roll(x, sh, axis=0), 0.0)


def _spread_matrix(k_in, sub):
    k = lax.broadcasted_iota(i32, (k_in, k_in * GROUPS_PER_BLOCK), 0)
    n = lax.broadcasted_iota(i32, (k_in, k_in * GROUPS_PER_BLOCK), 1)
    shift = int(math.log2(sub))
    same_a = jnp.right_shift(n, shift + int(math.log2(GROUPS_PER_BLOCK))) == jnp.right_shift(k, shift)
    return (same_a & ((n & (sub - 1)) == (k & (sub - 1)))).astype(bf16)


def _group_index(shape, axis, sub, offset=0):
    idx = lax.broadcasted_iota(i32, shape, axis) + offset
    return jnp.right_shift(idx, int(math.log2(sub))) & (GROUPS_PER_BLOCK - 1)


def _expand_block_diag(compact_ref, out_ref, row_sub, col_sub):
    rows, k_in = compact_ref.shape
    spread = _spread_matrix(k_in, col_sub)
    col_group = _group_index((2 * LANES, k_in * GROUPS_PER_BLOCK), 1, col_sub)
    for r0 in range(0, rows, 2 * LANES):
        full = jnp.dot(compact_ref[r0:r0 + 2 * LANES, :], spread, preferred_element_type=f32)
        row_group = _group_index(full.shape, 0, row_sub, r0)
        out_ref[r0:r0 + 2 * LANES, :] = jnp.where(row_group == col_group, full, 0.0).astype(bf16)


def _ssm_chunks(u_ref, d_ref, y_ref, toep_s, wout_s, win_s, entry_state):
    nk = u_ref.shape[1] // SSM_STEPS
    xs, pairs = _ssm_chunk_rows(u_ref, nk)
    y = _ssm_intra(pairs, toep_s, nk)
    s = jnp.dot(jnp.concatenate(pairs, axis=1), wout_s[...], preferred_element_type=f32)
    hprev, hr, hi = entry_state(s[:, :STATE_COLS], s[:, STATE_COLS:])
    y2 = _bdot(hprev, win_s[...])
    for st in range(SSM_STEPS):
        piece = (y[st // 2][:, (st % 2) * LANES:(st % 2 + 1) * LANES] + y2[:, st * LANES:(st + 1) * LANES]
                 + d_ref[...] * xs[st])
        y_ref[0, pl.ds(st, nk, stride=SSM_STEPS), :] = piece
    return hr, hi


def _ssm_body(u_ref, us_ref, h0r_ref, h0i_ref, toep_ref, wout_ref, win_ref, lev_ref, d_ref,
              y_ref, hr_ref, hi_ref, ys_ref, hrs_ref, his_ref, toep_s, wout_s, win_s):
    @pl.when(pl.program_id(1) == 0)
    def _():
        for d in range(SSM_STEPS // 2):
            _expand_block_diag(toep_ref.at[0, d], toep_s.at[d], SSM_GROUP, SSM_GROUP)
        _expand_block_diag(wout_ref.at[0], wout_s, SSM_GROUP, SSM_STATE)
        _expand_block_diag(win_ref.at[0], win_s, SSM_STATE, SSM_GROUP)

        def one_chunk(sr, si):
            h0r, h0i = h0r_ref[0], h0i_ref[0]
            ar, ai = lev_ref[0, 0, 0:1, :], lev_ref[0, 0, 1:2, :]
            return (jnp.concatenate([h0r, h0i], axis=1), sr + ar * h0r - ai * h0i, si + ar * h0i + ai * h0r)
        hrs_ref[0], his_ref[0] = _ssm_chunks(us_ref, d_ref, ys_ref, toep_s, wout_s, win_s, one_chunk)

    def scan_chunks(sr, si):
        nk = sr.shape[0]
        level = 0
        while (1 << level) < nk:
            ar, ai = lev_ref[0, level, 0:1, :], lev_ref[0, level, 1:2, :]
            tr, ti = _shift_rows(sr, 1 << level), _shift_rows(si, 1 << level)
            sr, si = sr + ar * tr - ai * ti, si + ar * ti + ai * tr
            level += 1
        return (jnp.concatenate([_shift_rows(sr, 1), _shift_rows(si, 1)], axis=1), sr[nk - 1:nk], si[nk - 1:nk])
    hr_ref[0, 0], hi_ref[0, 0] = _ssm_chunks(u_ref, d_ref, y_ref, toep_s, wout_s, win_s, scan_chunks)


def _ssm(u, us, h0r, h0i, toep, wout, win, lev, d):
    b, l, _ = u.shape
    rows = us.shape[1]
    nb = rows // SSM_STEPS
    wspec = lambda a: pl.BlockSpec((1,) + a.shape[1:], lambda j, bi: (j,) + (0,) * (a.ndim - 1))
    st = pl.BlockSpec((1, 1, 1, STATE_COLS), lambda j, bi: (bi, j, 0, 0))
    sst = pl.BlockSpec((1, nb, STATE_COLS), lambda j, bi: (j, 0, 0))
    seq = pl.BlockSpec((1, l, LANES), lambda j, bi: (bi, 0, j))
    sseq = pl.BlockSpec((1, rows, LANES), lambda j, bi: (0, 0, j))
    return pl.pallas_call(
        _ssm_body,
        grid=(SSM_BLOCKS, b),
        in_specs=[seq, sseq, sst, sst, wspec(toep), wspec(wout), wspec(win), wspec(lev),
                  pl.BlockSpec((1, LANES), lambda j, bi: (0, j))],
        out_specs=[seq, st, st, sseq, sst, sst],
        out_shape=[jax.ShapeDtypeStruct((b, l, SSM_WIDTH), f32),
                   jax.ShapeDtypeStruct((b, SSM_BLOCKS, 1, STATE_COLS), f32),
                   jax.ShapeDtypeStruct((b, SSM_BLOCKS, 1, STATE_COLS), f32),
                   jax.ShapeDtypeStruct((1, rows, SSM_WIDTH), f32),
                   jax.ShapeDtypeStruct((SSM_BLOCKS, nb, STATE_COLS), f32),
                   jax.ShapeDtypeStruct((SSM_BLOCKS, nb, STATE_COLS), f32)],
        scratch_shapes=[pltpu.VMEM((SSM_STEPS // 2, 2 * LANES, 2 * LANES), bf16),
                        pltpu.VMEM((SSM_STEPS * LANES, 2 * STATE_COLS), bf16),
                        pltpu.VMEM((2 * STATE_COLS, SSM_STEPS * LANES), bf16)],
        compiler_params=_cparams(2),
        name="ssm",
    )(u, us, h0r, h0i, toep, wout, win, lev, d)


ROUTER_COLS = LANES


def _mix_body(a_ref, ys_ref, x_ref, carry_ref, wglu_ref, bglu_ref, gs_ref, wout_ref, gf_ref, wr_ref, br_ref, tri_ref,
              x1_ref, hf_ref, wts_ref, idx_ref, cnt_ref, cnt_scr):
    @pl.when(pl.program_id(0) == 0)
    def _():
        cnt_scr[...] = carry_ref[...]

    y = ys_ref[...]
    y = 0.5 * y * (1.0 + jnp.tanh(math.sqrt(2.0 / math.pi) * (y + 0.044715 * (y * y * y))))
    y = y * jax.nn.sigmoid(_bdot(y, wglu_ref[...]) + bglu_ref[...])
    cat = jnp.concatenate([a_ref[...], _rms(y, gs_ref[...]).astype(bf16)], axis=1)
    x1 = x_ref[...] + jnp.dot(cat, wout_ref[...], preferred_element_type=f32)
    x1_ref[...] = x1
    hf = _rms(x1, gf_ref[...])
    hf_ref[...] = hf

    logits = _bdot(hf, wr_ref[...]) + br_ref[...]
    le = logits[:, :N_EXPERTS]
    lg = logits[:, N_EXPERTS:N_EXPERTS + N_EXPERT_GROUPS]
    tm = le.shape[0]
    gmax = jnp.max(lg, axis=-1, keepdims=True)
    gi = lax.broadcasted_iota(i32, (tm, N_EXPERT_GROUPS), 1).astype(f32)
    gsel = jnp.min(jnp.where(lg == gmax, gi, float(N_EXPERT_GROUPS)), axis=-1, keepdims=True)
    pg = 1.0 / jnp.sum(jnp.exp(lg - gmax), axis=-1, keepdims=True)
    ei_int = lax.broadcasted_iota(i32, (tm, N_EXPERTS), 1)
    ei = ei_int.astype(f32)
    egroup = jnp.right_shift(ei_int, int(math.log2(EXPERTS_PER_GROUP))).astype(f32)
    lm = jnp.where(egroup == gsel, le, NEG)
    v1 = jnp.max(lm, axis=-1, keepdims=True)
    i1 = jnp.min(jnp.where(lm == v1, ei, float(N_EXPERTS)), axis=-1, keepdims=True)
    lm2 = jnp.where(ei == i1, NEG, lm)
    v2 = jnp.max(lm2, axis=-1, keepdims=True)
    i2 = jnp.min(jnp.where(lm2 == v2, ei, float(N_EXPERTS)), axis=-1, keepdims=True)
    ex = jnp.exp(v2 - v1)
    wts_ref[...] = jnp.concatenate([pg / (1.0 + ex), pg * ex / (1.0 + ex)], axis=1)

    oh1 = (ei == i1).astype(f32)
    oh2 = (ei == i2).astype(f32)
    both = oh1 + oh2
    before = jnp.dot(tri_ref[...], both.astype(bf16), preferred_element_type=f32) + cnt_scr[...]
    r1 = jnp.sum(oh1 * before, axis=-1, keepdims=True)
    r2 = jnp.sum(oh2 * before, axis=-1, keepdims=True)
    idx_ref[...] = jnp.concatenate([i1, i2, r1, r2], axis=1).astype(i32)
    cnt_scr[...] = cnt_scr[...] + jnp.sum(both, axis=0, keepdims=True)
    cnt_ref[...] = cnt_scr[...]


def _mix(a, ys, x, carry, wglu, bglu, gs, wout, gf, wr, br, tri, tm):
    t = x.shape[0]
    row = lambda n: pl.BlockSpec((tm, n), lambda i: (i, 0))
    full = lambda arr: pl.BlockSpec(arr.shape, lambda i: (0,) * arr.ndim)
    return pl.pallas_call(
        _mix_body,
        grid=(t // tm,),
        in_specs=[row(ATTN_WIDTH), row(SSM_WIDTH), row(D_MODEL), full(carry), full(wglu), full(bglu), full(gs),
                  full(wout), full(gf), full(wr), full(br), full(tri)],
        out_specs=[row(D_MODEL), row(D_MODEL), row(2), row(4), pl.BlockSpec((1, N_EXPERTS), lambda i: (0, 0))],
        out_shape=[jax.ShapeDtypeStruct((t, D_MODEL), f32), jax.ShapeDtypeStruct((t, D_MODEL), f32),
                   jax.ShapeDtypeStruct((t, 2), f32), jax.ShapeDtypeStruct((t, 4), i32),
                   jax.ShapeDtypeStruct((1, N_EXPERTS), f32)],
        scratch_shapes=[pltpu.VMEM((1, N_EXPERTS), f32)],
        compiler_params=_cparams(),
        name="mix",
    )(a, ys, x, carry, wglu, bglu, gs, wout, gf, wr, br, tri)


def _row_copy(src, s_row, dst, d_row, sem):
    return pltpu.make_async_copy(src.at[pl.ds(s_row, 1), :], dst.at[pl.ds(d_row, 1), :], sem)


SUBLANES = 8
PAD_PIECES = tuple(1 << b for b in reversed(range(int(math.log2(SUBLANES)), int(math.log2(ROW_TILE)))))


def _zero_fill(zs_ref, zn_ref, zbuf, xs_hbm, sem, issue):
    def go(cp):
        cp.start() if issue else cp.wait()

    def segment(e, c):
        n, start = zn_ref[e], zs_ref[e]
        head = n & (SUBLANES - 1)
        for r in range(SUBLANES - 1):
            @pl.when(r < head)
            def _(r=r):
                go(_row_copy(zbuf, 0, xs_hbm, start + r, sem))
        off = start + head
        for piece in PAD_PIECES:
            @pl.when((n & piece) != 0)
            def _(off=off, piece=piece):
                go(pltpu.make_async_copy(zbuf.at[pl.ds(0, piece), :],
                                         xs_hbm.at[pl.ds(pl.multiple_of(off, SUBLANES), piece), :], sem))
            off = off + (n & piece)
        return c
    lax.fori_loop(0, N_EXPERTS, segment, 0)

    def tail(t, c):
        first = pl.multiple_of(zs_ref[N_EXPERTS] + t * ROW_TILE, ROW_TILE)
        go(pltpu.make_async_copy(zbuf, xs_hbm.at[pl.ds(first, ROW_TILE), :], sem))
        return c
    lax.fori_loop(0, zn_ref[N_EXPERTS], tail, 0)


def _scatter_body(zs_ref, zn_ref, p1_ref, p2_ref, hp_ref, hs_ref, xs_hbm, zbuf, sem, *, prompt_tiles):
    i = pl.program_id(0)

    @pl.when(i == pl.num_programs(0) - 1)
    def _():
        zbuf[...] = jnp.zeros_like(zbuf)
        _zero_fill(zs_ref, zn_ref, zbuf, xs_hbm, sem, True)
        _zero_fill(zs_ref, zn_ref, zbuf, xs_hbm, sem, False)

    def scatter_from(src):
        def issue(r, c):
            _row_copy(src, r, xs_hbm, p1_ref[0, 0, r], sem).start()
            _row_copy(src, r, xs_hbm, p2_ref[0, 0, r], sem).start()
            return c
        lax.fori_loop(0, ROW_TILE, issue, 0)

        def drain(r, c):
            _row_copy(src, 0, xs_hbm, 0, sem).wait()
            _row_copy(src, 0, xs_hbm, 0, sem).wait()
            return c
        lax.fori_loop(0, ROW_TILE, drain, 0)

    @pl.when(i < prompt_tiles)
    def _():
        scatter_from(hp_ref)

    @pl.when(i >= prompt_tiles)
    def _():
        scatter_from(hs_ref)


def _scatter(zero_start, zero_count, pos1, pos2, hf_p, hf_s, total_rows):
    prompt_tiles = hf_p.shape[0] // ROW_TILE
    tiles = prompt_tiles + hf_s.shape[0] // ROW_TILE
    smem = pl.BlockSpec((1, 1, ROW_TILE), lambda i, zs, zn: (i, 0, 0), memory_space=pltpu.SMEM)
    return pl.pallas_call(
        functools.partial(_scatter_body, prompt_tiles=prompt_tiles),
        grid_spec=pltpu.PrefetchScalarGridSpec(
            num_scalar_prefetch=2,
            grid=(tiles,),
            in_specs=[smem, smem,
                      pl.BlockSpec((ROW_TILE, D_MODEL), lambda i, zs, zn: (jnp.minimum(i, prompt_tiles - 1), 0)),
                      pl.BlockSpec((ROW_TILE, D_MODEL), lambda i, zs, zn: (jnp.maximum(i - prompt_tiles, 0), 0))],
            out_specs=pl.BlockSpec(memory_space=pl.ANY),
            scratch_shapes=[pltpu.VMEM((ROW_TILE, D_MODEL), f32), pltpu.SemaphoreType.DMA]),
        out_shape=jax.ShapeDtypeStruct((total_rows, D_MODEL), f32),
        compiler_params=_cparams(),
        name="moe_scatter",
    )(zero_start, zero_count, pos1, pos2, hf_p, hf_s)


def _experts_body(te_ref, tr_ref, ts_ref, x_ref, wg_ref, wu_ref, wd_ref, o_ref):
    n_valid = tr_ref[pl.program_id(0)]

    @pl.when(n_valid > 0)
    def _():
        x = x_ref[...].astype(bf16)
        hg = jnp.dot(x, wg_ref[0].astype(bf16), preferred_element_type=f32)
        hu = jnp.dot(x, wu_ref[0].astype(bf16), preferred_element_type=f32)
        o_ref[...] = _bdot(hg * jax.nn.sigmoid(hg) * hu, wd_ref[0])

    @pl.when(n_valid == 0)
    def _():
        o_ref[...] = jnp.zeros_like(o_ref)


def _experts(tile_expert, tile_rows, tile_src, xs, wg, wu, wd):
    tiles = xs.shape[0] // ROW_TILE
    wspec = lambda a: pl.BlockSpec((1,) + a.shape[1:], lambda i, te, tr, ts: (te[i], 0, 0))
    return pl.pallas_call(
        _experts_body,
        grid_spec=pltpu.PrefetchScalarGridSpec(
            num_scalar_prefetch=3,
            grid=(tiles,),
            in_specs=[pl.BlockSpec((ROW_TILE, D_MODEL), lambda i, te, tr, ts: (ts[i], 0)), wspec(wg), wspec(wu), wspec(wd)],
            out_specs=pl.BlockSpec((ROW_TILE, D_MODEL), lambda i, te, tr, ts: (i, 0))),
        out_shape=jax.ShapeDtypeStruct(xs.shape, f32),
        compiler_params=_cparams(),
        name="moe_experts",
    )(tile_expert, tile_rows, tile_src, xs, wg, wu, wd)


def _combine_body(p1_ref, p2_ref, ys_hbm, xp_ref, xs_ref, wp_ref, ws_ref, g_ref, op_ref, os_ref, buf1, buf2, sem,
                  *, prompt_tiles):
    i = pl.program_id(0)

    def issue(r, c):
        _row_copy(ys_hbm, p1_ref[0, 0, r], buf1, r, sem).start()
        _row_copy(ys_hbm, p2_ref[0, 0, r], buf2, r, sem).start()
        return c
    lax.fori_loop(0, ROW_TILE, issue, 0)

    def drain(r, c):
        _row_copy(ys_hbm, 0, buf1, 0, sem).wait()
        _row_copy(ys_hbm, 0, buf2, 0, sem).wait()
        return c
    lax.fori_loop(0, ROW_TILE, drain, 0)

    def finish(x_ref, w_ref, o_ref):
        w = w_ref[...]
        o_ref[...] = _rms(x_ref[...] + (w[:, 0:1] * buf1[...] + w[:, 1:2] * buf2[...]), g_ref[...])

    @pl.when(i < prompt_tiles)
    def _():
        finish(xp_ref, wp_ref, op_ref)

    @pl.when(i >= prompt_tiles)
    def _():
        finish(xs_ref, ws_ref, os_ref)


def _combine(pos1, pos2, ys, x1_p, x1_s, wts_p, wts_s, g):
    prompt_tiles = x1_p.shape[0] // ROW_TILE
    tiles = prompt_tiles + x1_s.shape[0] // ROW_TILE
    smem = pl.BlockSpec((1, 1, ROW_TILE), lambda i: (i, 0, 0), memory_space=pltpu.SMEM)
    p_rows = lambda n: pl.BlockSpec((ROW_TILE, n), lambda i: (jnp.minimum(i, prompt_tiles - 1), 0))
    s_rows = lambda n: pl.BlockSpec((ROW_TILE, n), lambda i: (jnp.maximum(i - prompt_tiles, 0), 0))
    return pl.pallas_call(
        functools.partial(_combine_body, prompt_tiles=prompt_tiles),
        grid=(tiles,),
        in_specs=[smem, smem, pl.BlockSpec(memory_space=pl.ANY), p_rows(D_MODEL), s_rows(D_MODEL), p_rows(2), s_rows(2),
                  pl.BlockSpec((1, D_MODEL), lambda i: (0, 0))],
        out_specs=[p_rows(D_MODEL), s_rows(D_MODEL)],
        out_shape=[jax.ShapeDtypeStruct(x1_p.shape, f32), jax.ShapeDtypeStruct(x1_s.shape, f32)],
        scratch_shapes=[pltpu.VMEM((ROW_TILE, D_MODEL), f32), pltpu.VMEM((ROW_TILE, D_MODEL), f32),
                        pltpu.SemaphoreType.DMA],
        compiler_params=_cparams(),
        name="moe_combine",
    )(pos1, pos2, ys, x1_p, x1_s, wts_p, wts_s, g)


def _expert_layout(counts, idx):
    padded = (counts + ROW_TILE - 1) // ROW_TILE * ROW_TILE
    e = jnp.arange(N_EXPERTS)
    ends = jnp.sum(jnp.where(e[None, :] <= e[:, None], padded[None, :], 0), axis=1)
    starts = ends - padded
    onehot = lambda e: (e[:, None] == jnp.arange(N_EXPERTS)[None, :]).astype(i32)
    pos1 = jnp.sum(onehot(idx[:, 0]) * starts[None, :], axis=1) + idx[:, 2]
    pos2 = jnp.sum(onehot(idx[:, 1]) * starts[None, :], axis=1) + idx[:, 3]
    return starts, ends, pos1, pos2


def _tile_tables(counts, starts, ends, tiles):
    first = jnp.arange(tiles, dtype=i32) * ROW_TILE
    used = first < ends[-1]
    expert = jnp.minimum(jnp.sum((first[:, None] >= ends[None, :]).astype(i32), axis=1), N_EXPERTS - 1)
    rows = jnp.clip(jnp.sum(jnp.where(expert[:, None] == jnp.arange(N_EXPERTS)[None, :], (starts + counts)[None, :], 0),
                            axis=1) - first, 0, ROW_TILE)
    rows = jnp.where(used, rows, 0).astype(i32)
    last = ends[-1] // ROW_TILE - 1
    src = jnp.minimum(jnp.arange(tiles, dtype=i32), last).astype(i32)
    return expert.astype(i32), rows, src


def kernel(x_prompt, x_sample, cache_k, cache_v, state_ssm_re, state_ssm_im, g_norm_mix, w_in, attn_sinks, ssm_a_re,
           ssm_a_im, ssm_log_dt, ssm_b_re, ssm_b_im, ssm_c_re, ssm_c_im, ssm_d, w_glu, b_glu, g_attn_out, g_ssm_out,
           w_out, g_norm_ffn, w_router_group, b_router_group, w_router_expert, b_router_expert, w_exp_gate, w_exp_up,
           w_exp_down, g_final):
    bp, lp, _ = x_prompt.shape
    bs, ls, _ = x_sample.shape
    depth = w_in.shape[0]
    assert depth == 1 and ls == SSM_STEPS and lp % ATTN_TILE == 0 and (bs * ls) % ROW_TILE == 0
    tp, ts = bp * lp, bs * ls
    wc = cache_k.shape[2]
    row2 = lambda v: v.reshape(1, -1)

    xp = x_prompt.reshape(tp, D_MODEL)
    xs = x_sample.reshape(ts, D_MODEL)
    w_in_bf = w_in[0].astype(bf16)
    qp, kp, vp, up = _proj(xp, row2(g_norm_mix[0]), w_in_bf, 512)
    qs, kq, vq, us = _proj(xs, row2(g_norm_mix[0]), w_in_bf, ts)

    sinks = attn_sinks[0]
    g_att = row2(g_attn_out[0])
    ap = _attn_prompt(qp.reshape(bp, lp, -1), kp.reshape(bp, lp, -1), vp.reshape(bp, lp, -1), sinks, g_att)
    a_s, k_roll, v_roll = _attn_sample(qs.reshape(bs, ls, -1), kq.reshape(bs, ls, -1), vq.reshape(bs, ls, -1),
                                       cache_k[0].reshape(bs, wc, KV_WIDTH), cache_v[0].reshape(bs, wc, KV_WIDTH),
                                       sinks, g_att)

    toep, wso, wsi, lev = _ssm_weights(ssm_a_re[0], ssm_a_im[0], ssm_log_dt[0], ssm_b_re[0], ssm_b_im[0],
                                       ssm_c_re[0], ssm_c_im[0])
    d_row = row2(ssm_d[0])
    to_blocks = lambda h: h.reshape(bs, SSM_BLOCKS, STATE_COLS).transpose(1, 0, 2)
    from_blocks = lambda h: h.transpose(1, 0, 2).reshape(bs, SSM_GROUPS, SSM_STATE)
    yp, hrp, hip, ysm, hrs, his = _ssm(up.reshape(bp, lp, -1), us.reshape(1, ts, -1), to_blocks(state_ssm_re[0]),
                                       to_blocks(state_ssm_im[0]), toep, wso, wsi, lev, d_row)

    wr = jnp.zeros((D_MODEL, ROUTER_COLS), f32)
    wr = wr.at[:, :N_EXPERTS].set(w_router_expert[0]).at[:, N_EXPERTS:N_EXPERTS + N_EXPERT_GROUPS].set(w_router_group[0])
    br = jnp.zeros((1, ROUTER_COLS), f32)
    br = br.at[0, :N_EXPERTS].set(b_router_expert[0]).at[0, N_EXPERTS:N_EXPERTS + N_EXPERT_GROUPS].set(b_router_group[0])
    mix_w = (w_glu[0].astype(bf16), row2(b_glu[0]), row2(g_ssm_out[0]), w_out[0].astype(bf16), row2(g_norm_ffn[0]),
             wr.astype(bf16), br)
    tri = jnp.tril(jnp.ones((512, 512), bf16), -1)
    x1p, hfp, wtp, idxp, cnt = _mix(ap.reshape(tp, -1), yp.reshape(tp, -1), xp, jnp.zeros((1, N_EXPERTS), f32),
                                    *mix_w, tri, 512)
    x1s, hfs, wts, idxs, cnt = _mix(a_s.reshape(ts, -1), ysm.reshape(ts, -1), xs, cnt, *mix_w, tri[:ts, :ts], ts)

    counts = cnt[0].astype(i32)
    idx = jnp.concatenate([idxp, idxs], axis=0)
    starts, ends, pos1, pos2 = _expert_layout(counts, idx)
    tiles = (2 * (tp + ts)) // ROW_TILE + N_EXPERTS
    tile_expert, tile_rows, tile_src = _tile_tables(counts, starts, ends, tiles)
    pos1 = pos1.reshape(-1, 1, ROW_TILE)
    pos2 = pos2.reshape(-1, 1, ROW_TILE)
    zero_start = jnp.concatenate([starts + counts, ends[-1:]]).astype(i32)
    zero_count = jnp.concatenate([ends - starts - counts, tiles - ends[-1:] // ROW_TILE]).astype(i32)
    sorted_rows = _scatter(zero_start, zero_count, pos1, pos2, hfp, hfs, tiles * ROW_TILE)
    expert_out = _experts(tile_expert, tile_rows, tile_src, sorted_rows, w_exp_gate[0], w_exp_up[0], w_exp_down[0])
    y_p, y_s = _combine(pos1, pos2, expert_out, x1p, x1s, wtp, wts, row2(g_final))

    kvshape = lambda a, b: a.reshape(1, b, -1, N_KV_HEADS, HEAD_DIM)
    block_state = lambda h: h.reshape(bp, SSM_GROUPS, SSM_STATE)[None]
    wcp = min(WINDOW, lp)
    return (y_p.reshape(bp, lp, D_MODEL), y_s.reshape(bs, ls, D_MODEL),
            kvshape(kp.reshape(bp, lp, -1)[:, lp - wcp:], bp), kvshape(vp.reshape(bp, lp, -1)[:, lp - wcp:], bp),
            block_state(hrp), block_state(hip),
            kvshape(k_roll, bs), kvshape(v_roll, bs),
            from_blocks(hrs)[None], from_blocks(his)[None])
```

```python
import functools
import math

import jax
import jax.numpy as jnp
from jax import lax
from jax.experimental import pallas as pl
from jax.experimental.pallas import tpu as pltpu

f32, bf16, i32 = jnp.float32, jnp.bfloat16, jnp.int32

D_MODEL = 1024
CHUNK = 64
N_BACK = 2
WINDOW = 128
ATTN_WIDTH = 512
HEAD_DIM = 64
N_KV_HEADS = 2
Q_PER_KV = 4
KV_WIDTH = 128
SSM_WIDTH = 512
SSM_GROUP = 16
SSM_GROUPS = 32
SSM_STATE = 64
PROJ_WIDTH = 1280
N_EXPERT_GROUPS = 4
EXPERTS_PER_GROUP = 8
N_EXPERTS = 32
D_EXPERT = 512
EPS = 1e-6
NEG = -1e30

LANES = 128
SSM_STEPS = 16
SSM_BLOCKS = SSM_WIDTH // LANES
GROUPS_PER_BLOCK = LANES // SSM_GROUP
STATE_COLS = GROUPS_PER_BLOCK * SSM_STATE
ROW_TILE = 256
SLOTS = 2
PIECES = D_MODEL // LANES
RUN_PIECES = tuple(1 << b for b in reversed(range(int(math.log2(ROW_TILE)) + 1)))
VMEM_LIMIT = 56 * 1024 * 1024


def _cparams(n_axes=1, limit=VMEM_LIMIT):
    return pltpu.CompilerParams(dimension_semantics=("arbitrary",) * n_axes, vmem_limit_bytes=limit)


def _rms(x, g):
    return x * lax.rsqrt(jnp.mean(x * x, axis=-1, keepdims=True) + EPS) * g


def _bdot(a, b):
    return jnp.dot(a.astype(bf16), b.astype(bf16), preferred_element_type=f32)


def _proj_body(x_ref, g_ref, w_ref, q_ref, k_ref, v_ref, u_ref):
    h = _rms(x_ref[...], g_ref[...])
    z = _bdot(h, w_ref[...])
    q_ref[...] = z[:, :ATTN_WIDTH] * (HEAD_DIM ** -0.5)
    k_ref[...] = z[:, ATTN_WIDTH:ATTN_WIDTH + KV_WIDTH]
    v_ref[...] = z[:, ATTN_WIDTH + KV_WIDTH:ATTN_WIDTH + 2 * KV_WIDTH]
    u_ref[...] = z[:, ATTN_WIDTH + 2 * KV_WIDTH:]


def _proj(x2d, g, w_bf, tm):
    t = x2d.shape[0]
    row = lambda n: pl.BlockSpec((tm, n), lambda i: (i, 0))
    full = lambda a: pl.BlockSpec(a.shape, lambda i: (0,) * a.ndim)
    return pl.pallas_call(
        _proj_body,
        grid=(t // tm,),
        in_specs=[row(D_MODEL), full(g), full(w_bf)],
        out_specs=[row(ATTN_WIDTH), row(KV_WIDTH), row(KV_WIDTH), row(SSM_WIDTH)],
        out_shape=[jax.ShapeDtypeStruct((t, n), f32) for n in (ATTN_WIDTH, KV_WIDTH, KV_WIDTH, SSM_WIDTH)],
        compiler_params=_cparams(),
        name="proj",
    )(x2d, g, w_bf)


def _sink_column(sink_ref, kv, rows_per_head):
    r = lax.broadcasted_iota(i32, (Q_PER_KV * rows_per_head, 1), 0)
    col = jnp.full((Q_PER_KV * rows_per_head, 1), sink_ref[kv * Q_PER_KV], f32)
    for j in range(1, Q_PER_KV):
        col = jnp.where(r >= j * rows_per_head, sink_ref[kv * Q_PER_KV + j], col)
    return col


def _attend(qs, kc, vc, sink_col, valid):
    s = lax.dot_general(qs.astype(bf16), kc.astype(bf16), (((1,), (1,)), ((), ())), preferred_element_type=f32)
    if valid is not None:
        s = jnp.where(valid, s, NEG)
    m = jnp.maximum(jnp.max(s, axis=-1, keepdims=True), sink_col)
    p = jnp.exp(s - m)
    denom = jnp.sum(p, axis=-1, keepdims=True) + jnp.exp(sink_col - m)
    return _bdot(p, vc) / denom


def _heads_attend(q, k, v, sink_ref, valid):
    rows = q.shape[0]
    pieces = []
    for kv in range(N_KV_HEADS):
        qs = jnp.concatenate(
            [q[:, (kv * Q_PER_KV + j) * HEAD_DIM:(kv * Q_PER_KV + j + 1) * HEAD_DIM] for j in range(Q_PER_KV)], axis=0)
        o = _attend(qs, k[:, kv * HEAD_DIM:(kv + 1) * HEAD_DIM], v[:, kv * HEAD_DIM:(kv + 1) * HEAD_DIM],
                    _sink_column(sink_ref, kv, rows), valid)
        pieces += [o[j * rows:(j + 1) * rows] for j in range(Q_PER_KV)]
    return jnp.concatenate(pieces, axis=1)


ATTN_TILE = 256
CHUNKS_PER_TILE = ATTN_TILE // CHUNK
KEY_SPAN = (N_BACK + 1) * CHUNK


def _attn_prompt_body(sink_ref, q_ref, kp_ref, kc_ref, vp_ref, vc_ref, g_ref, o_ref):
    i = pl.program_id(1)
    kwin = jnp.concatenate([kp_ref[0], kc_ref[0]], axis=0)
    vwin = jnp.concatenate([vp_ref[0], vc_ref[0]], axis=0)
    key_chunk = lax.broadcasted_iota(i32, (1, KEY_SPAN), 1) // CHUNK
    for c in range(CHUNKS_PER_TILE):
        valid = (i * CHUNKS_PER_TILE + c - N_BACK + key_chunk) >= 0
        o = _heads_attend(q_ref[0, c * CHUNK:(c + 1) * CHUNK, :], kwin[c * CHUNK:c * CHUNK + KEY_SPAN],
                          vwin[c * CHUNK:c * CHUNK + KEY_SPAN], sink_ref, valid)
        o_ref[0, c * CHUNK:(c + 1) * CHUNK, :] = _rms(o, g_ref[...]).astype(bf16)


def _attn_prompt(q, k, v, sinks, g):
    b, l, _ = q.shape
    back = N_BACK * CHUNK
    per = ATTN_TILE // back
    prev = pl.BlockSpec((1, back, KV_WIDTH), lambda bi, i: (bi, jnp.maximum(i * per - 1, 0), 0))
    cur = pl.BlockSpec((1, ATTN_TILE, KV_WIDTH), lambda bi, i: (bi, i, 0))
    return pl.pallas_call(
        _attn_prompt_body,
        grid=(b, l // ATTN_TILE),
        in_specs=[pl.BlockSpec(memory_space=pltpu.SMEM),
                  pl.BlockSpec((1, ATTN_TILE, ATTN_WIDTH), lambda bi, i: (bi, i, 0)),
                  prev, cur, prev, cur,
                  pl.BlockSpec((1, ATTN_WIDTH), lambda bi, i: (0, 0))],
        out_specs=pl.BlockSpec((1, ATTN_TILE, ATTN_WIDTH), lambda bi, i: (bi, i, 0)),
        out_shape=jax.ShapeDtypeStruct((b, l, ATTN_WIDTH), bf16),
        compiler_params=_cparams(2),
        name="attn_prompt",
    )(sinks, q, k, k, v, v, g)


def _attn_sample_body(sink_ref, q_ref, kn_ref, vn_ref, ck_ref, cv_ref, g_ref, o_ref, nk_ref, nv_ref):
    kall = jnp.concatenate([ck_ref[0], kn_ref[0]], axis=0)
    vall = jnp.concatenate([cv_ref[0], vn_ref[0]], axis=0)
    o = _heads_attend(q_ref[0], kall, vall, sink_ref, None)
    o_ref[0] = _rms(o, g_ref[...]).astype(bf16)
    n_new = kn_ref.shape[1]
    nk_ref[0] = kall[n_new:]
    nv_ref[0] = vall[n_new:]


def _attn_sample(q, k_new, v_new, cache_k, cache_v, sinks, g):
    b, l, _ = q.shape
    wc = cache_k.shape[1]
    blk = lambda r, n: pl.BlockSpec((1, r, n), lambda bi: (bi, 0, 0))
    return pl.pallas_call(
        _attn_sample_body,
        grid=(b,),
        in_specs=[pl.BlockSpec(memory_space=pltpu.SMEM), blk(l, ATTN_WIDTH), blk(l, KV_WIDTH), blk(l, KV_WIDTH),
                  blk(wc, KV_WIDTH), blk(wc, KV_WIDTH), pl.BlockSpec((1, ATTN_WIDTH), lambda bi: (0, 0))],
        out_specs=[blk(l, ATTN_WIDTH), blk(wc, KV_WIDTH), blk(wc, KV_WIDTH)],
        out_shape=[jax.ShapeDtypeStruct((b, l, ATTN_WIDTH), bf16),
                   jax.ShapeDtypeStruct((b, wc, KV_WIDTH), f32), jax.ShapeDtypeStruct((b, wc, KV_WIDTH), f32)],
        compiler_params=_cparams(),
        name="attn_sample",
    )(sinks, q, k_new, v_new, cache_k, cache_v, g)


def _ssm_weights(a_re, a_im, log_dt, b_re, b_im, c_re, c_im):
    hp = lax.Precision.HIGHEST
    dt = jnp.exp(log_dt)[:, None]
    mag = jnp.exp(a_re * dt)
    ar, ai = mag * jnp.cos(a_im * dt), mag * jnp.sin(a_im * dt)
    den = a_re * a_re + a_im * a_im
    nr, ni = ar - 1.0, ai
    fr = ((nr * a_re + ni * a_im) / den)[..., None]
    fi = ((ni * a_re - nr * a_im) / den)[..., None]
    bbr, bbi = fr * b_re - fi * b_im, fr * b_im + fi * b_re
    pr, pi = [jnp.ones_like(ar)], [jnp.zeros_like(ar)]
    for _ in range(SSM_STEPS):
        pr, pi = pr + [pr[-1] * ar - pi[-1] * ai], pi + [pr[-1] * ai + pi[-1] * ar]
    rev_r, rev_i = jnp.stack(pr[SSM_STEPS - 1::-1]), jnp.stack(pi[SSM_STEPS - 1::-1])
    pr, pi = jnp.stack(pr), jnp.stack(pi)
    cpr = c_re[None] * pr[:SSM_STEPS, :, None, :] - c_im[None] * pi[:SSM_STEPS, :, None, :]
    cpi = c_re[None] * pi[:SSM_STEPS, :, None, :] + c_im[None] * pr[:SSM_STEPS, :, None, :]
    kl = (jnp.einsum("lgdp,gpc->lgcd", cpr, bbr, precision=hp, preferred_element_type=f32)
          - jnp.einsum("lgdp,gpc->lgcd", cpi, bbi, precision=hp, preferred_element_type=f32))
    kl = jnp.concatenate([jnp.zeros_like(kl[:1]), kl], axis=0)
    nd = SSM_STEPS // 2
    lag = 2 * jnp.arange(nd)[:, None, None] + jnp.arange(2)[None, None, :] - jnp.arange(2)[None, :, None]
    ksel = kl[lag + 1]
    ksel = ksel.reshape(nd, 2, 2, SSM_BLOCKS, GROUPS_PER_BLOCK, SSM_GROUP, SSM_GROUP)
    toep = ksel.transpose(3, 0, 1, 4, 5, 2, 6).reshape(SSM_BLOCKS, nd, 2 * LANES, 2 * SSM_GROUP)
    bbr_t, bbi_t = bbr.transpose(0, 2, 1), bbi.transpose(0, 2, 1)
    wo_r = rev_r[:, :, None, :] * bbr_t[None] - rev_i[:, :, None, :] * bbi_t[None]
    wo_i = rev_r[:, :, None, :] * bbi_t[None] + rev_i[:, :, None, :] * bbr_t[None]
    wo = jnp.stack([wo_r, wo_i]).reshape(2, SSM_STEPS, SSM_BLOCKS, GROUPS_PER_BLOCK, SSM_GROUP, SSM_STATE)
    wout = wo.transpose(2, 1, 3, 4, 0, 5).reshape(SSM_BLOCKS, SSM_STEPS * LANES, 2 * SSM_STATE)
    gr = c_re[None] * pr[1:, :, None, :] - c_im[None] * pi[1:, :, None, :]
    gi = c_re[None] * pi[1:, :, None, :] + c_im[None] * pr[1:, :, None, :]
    wi = jnp.stack([gr, -gi]).reshape(2, SSM_STEPS, SSM_BLOCKS, GROUPS_PER_BLOCK, SSM_GROUP, SSM_STATE)
    win = wi.transpose(2, 0, 3, 5, 1, 4).reshape(SSM_BLOCKS, 2 * STATE_COLS, SSM_STEPS * SSM_GROUP)
    lr, li = [pr[SSM_STEPS]], [pi[SSM_STEPS]]
    for _ in range(7):
        lr, li = lr + [lr[-1] * lr[-1] - li[-1] * li[-1]], li + [2.0 * lr[-1] * li[-1]]
    lev = jnp.stack([jnp.stack(lr), jnp.stack(li)], axis=1)
    lev = lev.reshape(8, 2, SSM_BLOCKS, STATE_COLS).transpose(2, 0, 1, 3)
    return toep.astype(bf16), wout.astype(bf16), win.astype(bf16), lev


def _ssm_chunk_rows(u_ref, nk):
    xs = [u_ref[0, pl.ds(s, nk, stride=SSM_STEPS), :] for s in range(SSM_STEPS)]
    pairs = [jnp.concatenate([xs[2 * p], xs[2 * p + 1]], axis=1).astype(bf16) for p in range(SSM_STEPS // 2)]
    return xs, pairs


def _ssm_intra(pairs, toep_ref, nk):
    nd = len(pairs)
    y = [None] * nd
    for d in range(nd):
        lhs = jnp.concatenate(pairs[:nd - d], axis=0) if nd - d > 1 else pairs[0]
        r = jnp.dot(lhs, toep_ref[d], preferred_element_type=f32)
        for p in range(nd - d):
            blk = r[p * nk:(p + 1) * nk]
            y[p + d] = blk if y[p + d] is None else y[p + d] + blk
    return y


def _shift_rows(x, sh):
    rows = lax.broadcasted_iota(i32, (x.shape[0], 1), 0)
    return jnp.where(rows >= sh, pltpu.roll(x, sh, axis=0), 0.0)


def _spread_matrix(k_in, sub):
    k = lax.broadcasted_iota(i32, (k_in, k_in * GROUPS_PER_BLOCK), 0)
    n = lax.broadcasted_iota(i32, (k_in, k_in * GROUPS_PER_BLOCK), 1)
    shift = int(math.log2(sub))
    same_a = jnp.right_shift(n, shift + int(math.log2(GROUPS_PER_BLOCK))) == jnp.right_shift(k, shift)
    return (same_a & ((n & (sub - 1)) == (k & (sub - 1)))).astype(bf16)


def _group_index(shape, axis, sub, offset=0):
    idx = lax.broadcasted_iota(i32, shape, axis) + offset
    return jnp.right_shift(idx, int(math.log2(sub))) & (GROUPS_PER_BLOCK - 1)


def _expand_block_diag(compact_ref, out_ref, row_sub, col_sub):
    rows, k_in = compact_ref.shape
    spread = _spread_matrix(k_in, col_sub)
    col_group = _group_index((2 * LANES, k_in * GROUPS_PER_BLOCK), 1, col_sub)
    for r0 in range(0, rows, 2 * LANES):
        full = jnp.dot(compact_ref[r0:r0 + 2 * LANES, :], spread, preferred_element_type=f32)
        row_group = _group_index(full.shape, 0, row_sub, r0)
        out_ref[r0:r0 + 2 * LANES, :] = jnp.where(row_group == col_group, full, 0.0).astype(bf16)


def _ssm_chunks(u_ref, d_ref, y_ref, toep_s, wout_s, win_s, entry_state):
    nk = u_ref.shape[1] // SSM_STEPS
    xs, pairs = _ssm_chunk_rows(u_ref, nk)
    y = _ssm_intra(pairs, toep_s, nk)
    s = jnp.dot(jnp.concatenate(pairs, axis=1), wout_s[...], preferred_element_type=f32)
    hprev, hr, hi = entry_state(s[:, :STATE_COLS], s[:, STATE_COLS:])
    y2 = _bdot(hprev, win_s[...])
    for st in range(SSM_STEPS):
        piece = (y[st // 2][:, (st % 2) * LANES:(st % 2 + 1) * LANES] + y2[:, st * LANES:(st + 1) * LANES]
                 + d_ref[...] * xs[st])
        y_ref[0, pl.ds(st, nk, stride=SSM_STEPS), :] = piece
    return hr, hi


def _ssm_body(u_ref, us_ref, h0r_ref, h0i_ref, toep_ref, wout_ref, win_ref, lev_ref, d_ref,
              y_ref, hr_ref, hi_ref, ys_ref, hrs_ref, his_ref, toep_s, wout_s, win_s):
    @pl.when(pl.program_id(1) == 0)
    def _():
        for d in range(SSM_STEPS // 2):
            _expand_block_diag(toep_ref.at[0, d], toep_s.at[d], SSM_GROUP, SSM_GROUP)
        _expand_block_diag(wout_ref.at[0], wout_s, SSM_GROUP, SSM_STATE)
        _expand_block_diag(win_ref.at[0], win_s, SSM_STATE, SSM_GROUP)

        def one_chunk(sr, si):
            h0r, h0i = h0r_ref[0], h0i_ref[0]
            ar, ai = lev_ref[0, 0, 0:1, :], lev_ref[0, 0, 1:2, :]
            return (jnp.concatenate([h0r, h0i], axis=1), sr + ar * h0r - ai * h0i, si + ar * h0i + ai * h0r)
        hrs_ref[0], his_ref[0] = _ssm_chunks(us_ref, d_ref, ys_ref, toep_s, wout_s, win_s, one_chunk)

    def scan_chunks(sr, si):
        nk = sr.shape[0]
        level = 0
        while (1 << level) < nk:
            ar, ai = lev_ref[0, level, 0:1, :], lev_ref[0, level, 1:2, :]
            tr, ti = _shift_rows(sr, 1 << level), _shift_rows(si, 1 << level)
            sr, si = sr + ar * tr - ai * ti, si + ar * ti + ai * tr
            level += 1
        return (jnp.concatenate([_shift_rows(sr, 1), _shift_rows(si, 1)], axis=1), sr[nk - 1:nk], si[nk - 1:nk])
    hr_ref[0, 0], hi_ref[0, 0] = _ssm_chunks(u_ref, d_ref, y_ref, toep_s, wout_s, win_s, scan_chunks)


def _ssm(u, us, h0r, h0i, toep, wout, win, lev, d):
    b, l, _ = u.shape
    rows = us.shape[1]
    nb = rows // SSM_STEPS
    wspec = lambda a: pl.BlockSpec((1,) + a.shape[1:], lambda j, bi: (j,) + (0,) * (a.ndim - 1))
    st = pl.BlockSpec((1, 1, 1, STATE_COLS), lambda j, bi: (bi, j, 0, 0))
    sst = pl.BlockSpec((1, nb, STATE_COLS), lambda j, bi: (j, 0, 0))
    seq = pl.BlockSpec((1, l, LANES), lambda j, bi: (bi, 0, j))
    sseq = pl.BlockSpec((1, rows, LANES), lambda j, bi: (0, 0, j))
    return pl.pallas_call(
        _ssm_body,
        grid=(SSM_BLOCKS, b),
        in_specs=[seq, sseq, sst, sst, wspec(toep), wspec(wout), wspec(win), wspec(lev),
                  pl.BlockSpec((1, LANES), lambda j, bi: (0, j))],
        out_specs=[seq, st, st, sseq, sst, sst],
        out_shape=[jax.ShapeDtypeStruct((b, l, SSM_WIDTH), f32),
                   jax.ShapeDtypeStruct((b, SSM_BLOCKS, 1, STATE_COLS), f32),
                   jax.ShapeDtypeStruct((b, SSM_BLOCKS, 1, STATE_COLS), f32),
                   jax.ShapeDtypeStruct((1, rows, SSM_WIDTH), f32),
                   jax.ShapeDtypeStruct((SSM_BLOCKS, nb, STATE_COLS), f32),
                   jax.ShapeDtypeStruct((SSM_BLOCKS, nb, STATE_COLS), f32)],
        scratch_shapes=[pltpu.VMEM((SSM_STEPS // 2, 2 * LANES, 2 * LANES), bf16),
                        pltpu.VMEM((SSM_STEPS * LANES, 2 * STATE_COLS), bf16),
                        pltpu.VMEM((2 * STATE_COLS, SSM_STEPS * LANES), bf16)],
        compiler_params=_cparams(2),
        name="ssm",
    )(u, us, h0r, h0i, toep, wout, win, lev, d)


ROUTER_COLS = LANES


def _mix_body(a_ref, ys_ref, x_ref, wglu_ref, bglu_ref, gs_ref, wout_ref, gf_ref, wr_ref, br_ref, tri_ref, upper_ref,
              x1_ref, xl_ref, wts_ref, lpos_ref, n_ref, loff_ref):
    y = ys_ref[...]
    y = 0.5 * y * (1.0 + jnp.tanh(math.sqrt(2.0 / math.pi) * (y + 0.044715 * (y * y * y))))
    y = y * jax.nn.sigmoid(_bdot(y, wglu_ref[...]) + bglu_ref[...])
    cat = jnp.concatenate([a_ref[...], _rms(y, gs_ref[...]).astype(bf16)], axis=1)
    x1 = x_ref[...] + jnp.dot(cat, wout_ref[...], preferred_element_type=f32)
    x1_ref[...] = x1
    hf = _rms(x1, gf_ref[...])

    logits = _bdot(hf, wr_ref[...]) + br_ref[...]
    le = logits[:, :N_EXPERTS]
    lg = logits[:, N_EXPERTS:N_EXPERTS + N_EXPERT_GROUPS]
    tm = le.shape[0]
    gmax = jnp.max(lg, axis=-1, keepdims=True)
    gi = lax.broadcasted_iota(i32, (tm, N_EXPERT_GROUPS), 1).astype(f32)
    gsel = jnp.min(jnp.where(lg == gmax, gi, float(N_EXPERT_GROUPS)), axis=-1, keepdims=True)
    pg = 1.0 / jnp.sum(jnp.exp(lg - gmax), axis=-1, keepdims=True)
    ei_int = lax.broadcasted_iota(i32, (tm, N_EXPERTS), 1)
    ei = ei_int.astype(f32)
    egroup = jnp.right_shift(ei_int, int(math.log2(EXPERTS_PER_GROUP))).astype(f32)
    lm = jnp.where(egroup == gsel, le, NEG)
    v1 = jnp.max(lm, axis=-1, keepdims=True)
    i1 = jnp.min(jnp.where(lm == v1, ei, float(N_EXPERTS)), axis=-1, keepdims=True)
    lm2 = jnp.where(ei == i1, NEG, lm)
    v2 = jnp.max(lm2, axis=-1, keepdims=True)
    i2 = jnp.min(jnp.where(lm2 == v2, ei, float(N_EXPERTS)), axis=-1, keepdims=True)
    ex = jnp.exp(v2 - v1)
    wts_ref[...] = jnp.concatenate([pg / (1.0 + ex), pg * ex / (1.0 + ex)], axis=1)

    oh1 = (ei == i1).astype(f32)
    oh2 = (ei == i2).astype(f32)
    hf_bf = hf.astype(bf16)
    sorted_row = lax.broadcasted_iota(i32, (ROW_TILE, SLOTS * ROW_TILE), 1).astype(f32)
    for h in range(tm // ROW_TILE):
        rows = slice(h * ROW_TILE, (h + 1) * ROW_TILE)
        both = (oh1[rows] + oh2[rows]).astype(bf16)
        before = jnp.dot(tri_ref[...], both, preferred_element_type=f32)
        count = jnp.sum(oh1[rows] + oh2[rows], axis=0, keepdims=True)
        lower = jnp.sum(jnp.dot(both, upper_ref[...], preferred_element_type=f32), axis=0, keepdims=True)
        lp1 = jnp.sum(oh1[rows] * (before + lower), axis=-1, keepdims=True)
        lp2 = jnp.sum(oh2[rows] * (before + lower), axis=-1, keepdims=True)
        lpos_ref[rows, :] = jnp.concatenate([lp1, lp2], axis=1)
        n_ref[0, h:h + 1, :] = count
        loff_ref[0, h:h + 1, :] = lower
        pick = ((sorted_row == lp1) | (sorted_row == lp2)).astype(bf16)
        xl = lax.dot_general(pick, hf_bf[rows], (((0,), (0,)), ((), ())), preferred_element_type=f32)
        _store_row_major(xl_ref, h * SLOTS * ROW_TILE, xl)


def _mix(a, ys, x, wglu, bglu, gs, wout, gf, wr, br, tri, upper, tm):
    t = x.shape[0]
    halves = tm // ROW_TILE
    row = lambda n: pl.BlockSpec((tm, n), lambda i: (i, 0))
    full = lambda arr: pl.BlockSpec(arr.shape, lambda i: (0,) * arr.ndim)
    per_tile = pl.BlockSpec((1, halves, N_EXPERTS), lambda i: (i, 0, 0))
    return pl.pallas_call(
        _mix_body,
        grid=(t // tm,),
        in_specs=[row(ATTN_WIDTH), row(SSM_WIDTH), row(D_MODEL), full(wglu), full(bglu), full(gs),
                  full(wout), full(gf), full(wr), full(br), full(tri), full(upper)],
        out_specs=[row(D_MODEL), pl.BlockSpec((SLOTS * tm * PIECES, LANES), lambda i: (i, 0)), row(2), row(2),
                   per_tile, per_tile],
        out_shape=[jax.ShapeDtypeStruct((t, D_MODEL), f32),
                   jax.ShapeDtypeStruct((SLOTS * t * PIECES, LANES), f32),
                   jax.ShapeDtypeStruct((t, 2), f32), jax.ShapeDtypeStruct((t, 2), f32),
                   jax.ShapeDtypeStruct((t // tm, halves, N_EXPERTS), f32),
                   jax.ShapeDtypeStruct((t // tm, halves, N_EXPERTS), f32)],
        compiler_params=_cparams(),
        name="mix",
    )(a, ys, x, wglu, bglu, gs, wout, gf, wr, br, tri, upper)


def _store_row_major(ref, first_row, x):
    for c in range(PIECES):
        ref[pl.ds(first_row * PIECES + c, x.shape[0], stride=PIECES), :] = x[:, c * LANES:(c + 1) * LANES]


def _load_row_major(ref, n_rows):
    return jnp.concatenate([ref[pl.ds(c, n_rows, stride=PIECES), :] for c in range(PIECES)], axis=1)


def _copy_rows(src, s_row, dst, d_row, n_rows, sem):
    return pltpu.make_async_copy(src.at[pl.ds(pl.multiple_of(s_row * PIECES, PIECES), n_rows * PIECES), :],
                                 dst.at[pl.ds(pl.multiple_of(d_row * PIECES, PIECES), n_rows * PIECES), :], sem)


def _for_each_piece(n, fn):
    off = 0
    for piece in RUN_PIECES:
        @pl.when((n & piece) != 0)
        def _(off=off, piece=piece):
            fn(off, piece)
        off = off + (n & piece)


def _experts_body(te_ref, tpos_ref, tvalid_ref, tlo_ref, thi_ref, n_ref, cum_ref, loff_ref,
                  xlp_hbm, xls_hbm, wg_ref, wu_ref, wd_ref, o_ref, xbuf, wg_s, wu_s, wd_s, sem, *, prompt_tiles):
    i = pl.program_id(0)
    last = pl.num_programs(0) - 1

    def fetch(t):
        slot = t % 2
        e, lo, valid = te_ref[t], tpos_ref[t], tvalid_ref[t]

        @pl.when(valid < ROW_TILE)
        def _():
            xbuf[slot] = jnp.zeros((ROW_TILE * PIECES, LANES), f32)

        def run(tau, c):
            k = tau * N_EXPERTS + e
            s, n = cum_ref[k], n_ref[k]
            a = jnp.maximum(s, lo)
            length = jnp.maximum(jnp.minimum(s + n, lo + ROW_TILE) - a, 0)
            local = loff_ref[k] + (a - s)

            def start_from(src_hbm, tile):
                _for_each_piece(length, lambda off, piece: _copy_rows(
                    src_hbm, tile * (SLOTS * ROW_TILE) + local + off, xbuf.at[slot], a - lo + off, piece,
                    sem.at[slot]).start())

            @pl.when(tau < prompt_tiles)
            def _():
                start_from(xlp_hbm, tau)

            @pl.when(tau >= prompt_tiles)
            def _():
                start_from(xls_hbm, tau - prompt_tiles)
            return c
        lax.fori_loop(tlo_ref[t], thi_ref[t] + 1, run, 0)

    @pl.when((i == 0) & (tvalid_ref[0] > 0))
    def _():
        fetch(0)

    nxt = jnp.minimum(i + 1, last)

    @pl.when((i < last) & (tvalid_ref[nxt] > 0))
    def _():
        fetch(nxt)

    @pl.when((i == 0) | (te_ref[i] != te_ref[jnp.maximum(i - 1, 0)]))
    def _():
        wg_s[...] = wg_ref[0].astype(bf16)
        wu_s[...] = wu_ref[0].astype(bf16)
        wd_s[...] = wd_ref[0].astype(bf16)

    valid = tvalid_ref[i]

    @pl.when(valid > 0)
    def _():
        slot = i % 2
        _for_each_piece(valid, lambda off, piece: _copy_rows(
            xlp_hbm, 0, xbuf.at[slot], 0, piece, sem.at[slot]).wait())
        x = _load_row_major(xbuf.at[slot], ROW_TILE).astype(bf16)
        hg = jnp.dot(x, wg_s[...], preferred_element_type=f32)
        hu = jnp.dot(x, wu_s[...], preferred_element_type=f32)
        y = jnp.dot((hg * jax.nn.sigmoid(hg) * hu).astype(bf16), wd_s[...], preferred_element_type=f32)
        _store_row_major(o_ref, 0, y)

    @pl.when(valid == 0)
    def _():
        o_ref[...] = jnp.zeros_like(o_ref)


def _experts(tables, xl_p, xl_s, wg, wu, wd, tiles):
    n_pre = len(tables)
    prompt_tiles = xl_p.shape[0] // (SLOTS * ROW_TILE * PIECES)
    wspec = lambda a: pl.BlockSpec((1,) + a.shape[1:], lambda i, te, *_: (te[i], 0, 0))
    return pl.pallas_call(
        functools.partial(_experts_body, prompt_tiles=prompt_tiles),
        grid_spec=pltpu.PrefetchScalarGridSpec(
            num_scalar_prefetch=n_pre,
            grid=(tiles,),
            in_specs=[pl.BlockSpec(memory_space=pl.ANY), pl.BlockSpec(memory_space=pl.ANY), wspec(wg), wspec(wu), wspec(wd)],
            out_specs=pl.BlockSpec((ROW_TILE * PIECES, LANES), lambda i, *_: (i, 0)),
            scratch_shapes=[pltpu.VMEM((2, ROW_TILE * PIECES, LANES), f32),
                            pltpu.VMEM(wg.shape[1:], bf16), pltpu.VMEM(wu.shape[1:], bf16), pltpu.VMEM(wd.shape[1:], bf16),
                            pltpu.SemaphoreType.DMA((2,))]),
        out_shape=jax.ShapeDtypeStruct((tiles * ROW_TILE * PIECES, LANES), f32),
        compiler_params=_cparams(),
        name="moe_experts",
    )(*tables, xl_p, xl_s, wg, wu, wd)


def _combine_body(n_ref, gpos_ref, loff_ref, ys_hbm, xp_ref, xs_ref, wp_ref, ws_ref, lp_ref, ls_ref, g_ref,
                  op_ref, os_ref, ybuf, sem, *, prompt_tiles):
    i = pl.program_id(0)
    last = pl.num_programs(0) - 1
    tile_rows = SLOTS * ROW_TILE

    def fetch(t):
        slot = t % 2

        def run(e, c):
            k = t * N_EXPERTS + e
            _for_each_piece(n_ref[k], lambda off, piece: _copy_rows(
                ys_hbm, gpos_ref[k] + off, ybuf.at[slot], loff_ref[k] + off, piece, sem.at[slot]).start())
            return c
        lax.fori_loop(0, N_EXPERTS, run, 0)

    @pl.when(i == 0)
    def _():
        fetch(0)

    @pl.when(i < last)
    def _():
        fetch(jnp.minimum(i + 1, last))

    slot = i % 2
    _copy_rows(ys_hbm, 0, ybuf.at[slot], 0, tile_rows, sem.at[slot]).wait()
    yl = _load_row_major(ybuf.at[slot], tile_rows).astype(bf16)
    sorted_row = lax.broadcasted_iota(i32, (ROW_TILE, tile_rows), 1).astype(f32)

    def finish(x_ref, w_ref, l_ref, o_ref):
        w, lp = w_ref[...], l_ref[...]
        y1 = jnp.dot((sorted_row == lp[:, 0:1]).astype(bf16), yl, preferred_element_type=f32)
        y2 = jnp.dot((sorted_row == lp[:, 1:2]).astype(bf16), yl, preferred_element_type=f32)
        o_ref[...] = _rms(x_ref[...] + (w[:, 0:1] * y1 + w[:, 1:2] * y2), g_ref[...])

    @pl.when(i < prompt_tiles)
    def _():
        finish(xp_ref, wp_ref, lp_ref, op_ref)

    @pl.when(i >= prompt_tiles)
    def _():
        finish(xs_ref, ws_ref, ls_ref, os_ref)


def _combine(tables, ys, x1_p, x1_s, wts_p, wts_s, lpos_p, lpos_s, g):
    prompt_tiles = x1_p.shape[0] // ROW_TILE
    tiles = prompt_tiles + x1_s.shape[0] // ROW_TILE
    p_rows = lambda n: pl.BlockSpec((ROW_TILE, n), lambda i, *_: (jnp.minimum(i, prompt_tiles - 1), 0))
    s_rows = lambda n: pl.BlockSpec((ROW_TILE, n), lambda i, *_: (jnp.maximum(i - prompt_tiles, 0), 0))
    return pl.pallas_call(
        functools.partial(_combine_body, prompt_tiles=prompt_tiles),
        grid_spec=pltpu.PrefetchScalarGridSpec(
            num_scalar_prefetch=len(tables),
            grid=(tiles,),
            in_specs=[pl.BlockSpec(memory_space=pl.ANY), p_rows(D_MODEL), s_rows(D_MODEL), p_rows(2), s_rows(2),
                      p_rows(2), s_rows(2), pl.BlockSpec((1, D_MODEL), lambda i, *_: (0, 0))],
            out_specs=[p_rows(D_MODEL), s_rows(D_MODEL)],
            scratch_shapes=[pltpu.VMEM((2, SLOTS * ROW_TILE * PIECES, LANES), f32), pltpu.SemaphoreType.DMA((2,))]),
        out_shape=[jax.ShapeDtypeStruct(x1_p.shape, f32), jax.ShapeDtypeStruct(x1_s.shape, f32)],
        compiler_params=_cparams(),
        name="moe_combine",
    )(*tables, ys, x1_p, x1_s, wts_p, wts_s, lpos_p, lpos_s, g)


def _moe_tables(n, loff, tiles):
    cum = jnp.cumsum(n, axis=0) - n
    counts = jnp.sum(n, axis=0)
    padded = (counts + ROW_TILE - 1) // ROW_TILE * ROW_TILE
    ends = jnp.cumsum(padded)
    starts = ends - padded
    first = jnp.arange(tiles, dtype=i32) * ROW_TILE
    expert = jnp.minimum(jnp.sum((first[:, None] >= ends[None, :]).astype(i32), axis=1), N_EXPERTS - 1)
    sel = expert[:, None] == jnp.arange(N_EXPERTS)[None, :]
    pick = lambda v: jnp.sum(jnp.where(sel, v[None, :], 0), axis=1)
    pos = first - pick(starts)
    valid = jnp.where(first < ends[-1], jnp.clip(pick(counts) - pos, 0, ROW_TILE), 0)
    cum_t, n_t = cum.T[expert], n.T[expert]
    touches = (cum_t + n_t > pos[:, None]) & (cum_t < (pos + ROW_TILE)[:, None]) & (n_t > 0)
    tau = jnp.arange(n.shape[0], dtype=i32)[None, :]
    lo = jnp.min(jnp.where(touches, tau, n.shape[0]), axis=1)
    hi = jnp.max(jnp.where(touches, tau, -1), axis=1)
    as_i32 = lambda v: v.astype(i32)
    flat = lambda v: v.reshape(-1).astype(i32)
    expert_tables = tuple(map(as_i32, (expert, pos, valid, lo, hi))) + (flat(n), flat(cum), flat(loff))
    combine_tables = (flat(n), flat(starts[None, :] + cum), flat(loff))
    return expert_tables, combine_tables


def kernel(x_prompt, x_sample, cache_k, cache_v, state_ssm_re, state_ssm_im, g_norm_mix, w_in, attn_sinks, ssm_a_re,
           ssm_a_im, ssm_log_dt, ssm_b_re, ssm_b_im, ssm_c_re, ssm_c_im, ssm_d, w_glu, b_glu, g_attn_out, g_ssm_out,
           w_out, g_norm_ffn, w_router_group, b_router_group, w_router_expert, b_router_expert, w_exp_gate, w_exp_up,
           w_exp_down, g_final):
    bp, lp, _ = x_prompt.shape
    bs, ls, _ = x_sample.shape
    depth = w_in.shape[0]
    assert depth == 1 and ls == SSM_STEPS and lp % ATTN_TILE == 0 and (bs * ls) % ROW_TILE == 0
    tp, ts = bp * lp, bs * ls
    wc = cache_k.shape[2]
    row2 = lambda v: v.reshape(1, -1)

    xp = x_prompt.reshape(tp, D_MODEL)
    xs = x_sample.reshape(ts, D_MODEL)
    w_in_bf = w_in[0].astype(bf16)
    qp, kp, vp, up = _proj(xp, row2(g_norm_mix[0]), w_in_bf, 512)
    qs, kq, vq, us = _proj(xs, row2(g_norm_mix[0]), w_in_bf, ts)

    sinks = attn_sinks[0]
    g_att = row2(g_attn_out[0])
    ap = _attn_prompt(qp.reshape(bp, lp, -1), kp.reshape(bp, lp, -1), vp.reshape(bp, lp, -1), sinks, g_att)
    a_s, k_roll, v_roll = _attn_sample(qs.reshape(bs, ls, -1), kq.reshape(bs, ls, -1), vq.reshape(bs, ls, -1),
                                       cache_k[0].reshape(bs, wc, KV_WIDTH), cache_v[0].reshape(bs, wc, KV_WIDTH),
                                       sinks, g_att)

    toep, wso, wsi, lev = _ssm_weights(ssm_a_re[0], ssm_a_im[0], ssm_log_dt[0], ssm_b_re[0], ssm_b_im[0],
                                       ssm_c_re[0], ssm_c_im[0])
    d_row = row2(ssm_d[0])
    to_blocks = lambda h: h.reshape(bs, SSM_BLOCKS, STATE_COLS).transpose(1, 0, 2)
    from_blocks = lambda h: h.transpose(1, 0, 2).reshape(bs, SSM_GROUPS, SSM_STATE)
    yp, hrp, hip, ysm, hrs, his = _ssm(up.reshape(bp, lp, -1), us.reshape(1, ts, -1), to_blocks(state_ssm_re[0]),
                                       to_blocks(state_ssm_im[0]), toep, wso, wsi, lev, d_row)

    wr = jnp.zeros((D_MODEL, ROUTER_COLS), f32)
    wr = wr.at[:, :N_EXPERTS].set(w_router_expert[0]).at[:, N_EXPERTS:N_EXPERTS + N_EXPERT_GROUPS].set(w_router_group[0])
    br = jnp.zeros((1, ROUTER_COLS), f32)
    br = br.at[0, :N_EXPERTS].set(b_router_expert[0]).at[0, N_EXPERTS:N_EXPERTS + N_EXPERT_GROUPS].set(b_router_group[0])
    tri = jnp.tril(jnp.ones((ROW_TILE, ROW_TILE), bf16), -1)
    upper = jnp.triu(jnp.ones((N_EXPERTS, N_EXPERTS), bf16), 1)
    mix_w = (w_glu[0].astype(bf16), row2(b_glu[0]), row2(g_ssm_out[0]), w_out[0].astype(bf16), row2(g_norm_ffn[0]),
             wr.astype(bf16), br, tri, upper)
    x1p, xlp, wtp, lpp, n_p, off_p = _mix(ap.reshape(tp, -1), yp.reshape(tp, -1), xp, *mix_w, 2 * ROW_TILE)
    x1s, xls, wts, lps, n_s, off_s = _mix(a_s.reshape(ts, -1), ysm.reshape(ts, -1), xs, *mix_w, ROW_TILE)

    per_tile = lambda p, s: jnp.concatenate([p.reshape(-1, N_EXPERTS), s.reshape(-1, N_EXPERTS)], axis=0).astype(i32)
    tiles = (SLOTS * (tp + ts)) // ROW_TILE + N_EXPERTS
    expert_tables, combine_tables = _moe_tables(per_tile(n_p, n_s), per_tile(off_p, off_s), tiles)
    expert_out = _experts(expert_tables, xlp, xls, w_exp_gate[0], w_exp_up[0], w_exp_down[0], tiles)
    y_p, y_s = _combine(combine_tables, expert_out, x1p, x1s, wtp, wts, lpp, lps, row2(g_final))

    kvshape = lambda a, b: a.reshape(1, b, -1, N_KV_HEADS, HEAD_DIM)
    block_state = lambda h: h.reshape(bp, SSM_GROUPS, SSM_STATE)[None]
    wcp = min(WINDOW, lp)
    return (y_p.reshape(bp, lp, D_MODEL), y_s.reshape(bs, ls, D_MODEL),
            kvshape(kp.reshape(bp, lp, -1)[:, lp - wcp:], bp), kvshape(vp.reshape(bp, lp, -1)[:, lp - wcp:], bp),
            block_state(hrp), block_state(hip),
            kvshape(k_roll, bs), kvshape(v_roll, bs),
            from_blocks(hrs)[None], from_blocks(his)[None])
```

```python
import functools
import math

import jax
import jax.numpy as jnp
from jax import lax
from jax.experimental import pallas as pl
from jax.experimental.pallas import tpu as pltpu

f32, bf16, i32 = jnp.float32, jnp.bfloat16, jnp.int32

D_MODEL = 1024
CHUNK = 64
N_BACK = 2
WINDOW = 128
ATTN_WIDTH = 512
HEAD_DIM = 64
N_KV_HEADS = 2
Q_PER_KV = 4
KV_WIDTH = 128
SSM_WIDTH = 512
SSM_GROUP = 16
SSM_GROUPS = 32
SSM_STATE = 64
PROJ_WIDTH = 1280
N_EXPERT_GROUPS = 4
EXPERTS_PER_GROUP = 8
N_EXPERTS = 32
D_EXPERT = 512
EPS = 1e-6
NEG = -1e30

LANES = 128
SSM_STEPS = 16
SSM_BLOCKS = SSM_WIDTH // LANES
GROUPS_PER_BLOCK = LANES // SSM_GROUP
STATE_COLS = GROUPS_PER_BLOCK * SSM_STATE
ROW_TILE = 256
SLOTS = 2
PIECES = D_MODEL // LANES
RUN_PIECES = tuple(1 << b for b in reversed(range(int(math.log2(ROW_TILE)) + 1)))
SHORT_RUN = 32
VMEM_LIMIT = 56 * 1024 * 1024


def _cparams(n_axes=1, limit=VMEM_LIMIT):
    return pltpu.CompilerParams(dimension_semantics=("arbitrary",) * n_axes, vmem_limit_bytes=limit)


def _rms(x, g):
    return x * lax.rsqrt(jnp.mean(x * x, axis=-1, keepdims=True) + EPS) * g


def _bdot(a, b):
    return jnp.dot(a.astype(bf16), b.astype(bf16), preferred_element_type=f32)


def _proj_body(x_ref, g_ref, w_ref, q_ref, k_ref, v_ref, u_ref):
    h = _rms(x_ref[...], g_ref[...])
    z = _bdot(h, w_ref[...])
    q_ref[...] = z[:, :ATTN_WIDTH] * (HEAD_DIM ** -0.5)
    k_ref[...] = z[:, ATTN_WIDTH:ATTN_WIDTH + KV_WIDTH]
    v_ref[...] = z[:, ATTN_WIDTH + KV_WIDTH:ATTN_WIDTH + 2 * KV_WIDTH]
    u_ref[...] = z[:, ATTN_WIDTH + 2 * KV_WIDTH:]


def _proj(x2d, g, w_bf, tm):
    t = x2d.shape[0]
    row = lambda n: pl.BlockSpec((tm, n), lambda i: (i, 0))
    full = lambda a: pl.BlockSpec(a.shape, lambda i: (0,) * a.ndim)
    return pl.pallas_call(
        _proj_body,
        grid=(t // tm,),
        in_specs=[row(D_MODEL), full(g), full(w_bf)],
        out_specs=[row(ATTN_WIDTH), row(KV_WIDTH), row(KV_WIDTH), row(SSM_WIDTH)],
        out_shape=[jax.ShapeDtypeStruct((t, n), f32) for n in (ATTN_WIDTH, KV_WIDTH, KV_WIDTH, SSM_WIDTH)],
        compiler_params=_cparams(),
        name="proj",
    )(x2d, g, w_bf)


def _sink_column(sink_ref, kv, rows_per_head):
    r = lax.broadcasted_iota(i32, (Q_PER_KV * rows_per_head, 1), 0)
    col = jnp.full((Q_PER_KV * rows_per_head, 1), sink_ref[kv * Q_PER_KV], f32)
    for j in range(1, Q_PER_KV):
        col = jnp.where(r >= j * rows_per_head, sink_ref[kv * Q_PER_KV + j], col)
    return col


def _attend(qs, kc, vc, sink_col, valid):
    s = lax.dot_general(qs.astype(bf16), kc.astype(bf16), (((1,), (1,)), ((), ())), preferred_element_type=f32)
    if valid is not None:
        s = jnp.where(valid, s, NEG)
    m = jnp.maximum(jnp.max(s, axis=-1, keepdims=True), sink_col)
    p = jnp.exp(s - m)
    denom = jnp.sum(p, axis=-1, keepdims=True) + jnp.exp(sink_col - m)
    return _bdot(p, vc) / denom


def _heads_attend(q, k, v, sink_ref, valid):
    rows = q.shape[0]
    pieces = []
    for kv in range(N_KV_HEADS):
        qs = jnp.concatenate(
            [q[:, (kv * Q_PER_KV + j) * HEAD_DIM:(kv * Q_PER_KV + j + 1) * HEAD_DIM] for j in range(Q_PER_KV)], axis=0)
        o = _attend(qs, k[:, kv * HEAD_DIM:(kv + 1) * HEAD_DIM], v[:, kv * HEAD_DIM:(kv + 1) * HEAD_DIM],
                    _sink_column(sink_ref, kv, rows), valid)
        pieces += [o[j * rows:(j + 1) * rows] for j in range(Q_PER_KV)]
    return jnp.concatenate(pieces, axis=1)


ATTN_TILE = 256
CHUNKS_PER_TILE = ATTN_TILE // CHUNK
KEY_SPAN = (N_BACK + 1) * CHUNK


def _attn_prompt_body(sink_ref, q_ref, kp_ref, kc_ref, vp_ref, vc_ref, g_ref, o_ref):
    i = pl.program_id(1)
    kwin = jnp.concatenate([kp_ref[0], kc_ref[0]], axis=0)
    vwin = jnp.concatenate([vp_ref[0], vc_ref[0]], axis=0)
    key_chunk = lax.broadcasted_iota(i32, (1, KEY_SPAN), 1) // CHUNK
    for c in range(CHUNKS_PER_TILE):
        valid = (i * CHUNKS_PER_TILE + c - N_BACK + key_chunk) >= 0
        o = _heads_attend(q_ref[0, c * CHUNK:(c + 1) * CHUNK, :], kwin[c * CHUNK:c * CHUNK + KEY_SPAN],
                          vwin[c * CHUNK:c * CHUNK + KEY_SPAN], sink_ref, valid)
        o_ref[0, c * CHUNK:(c + 1) * CHUNK, :] = _rms(o, g_ref[...]).astype(bf16)


def _attn_prompt(q, k, v, sinks, g):
    b, l, _ = q.shape
    back = N_BACK * CHUNK
    per = ATTN_TILE // back
    prev = pl.BlockSpec((1, back, KV_WIDTH), lambda bi, i: (bi, jnp.maximum(i * per - 1, 0), 0))
    cur = pl.BlockSpec((1, ATTN_TILE, KV_WIDTH), lambda bi, i: (bi, i, 0))
    return pl.pallas_call(
        _attn_prompt_body,
        grid=(b, l // ATTN_TILE),
        in_specs=[pl.BlockSpec(memory_space=pltpu.SMEM),
                  pl.BlockSpec((1, ATTN_TILE, ATTN_WIDTH), lambda bi, i: (bi, i, 0)),
                  prev, cur, prev, cur,
                  pl.BlockSpec((1, ATTN_WIDTH), lambda bi, i: (0, 0))],
        out_specs=pl.BlockSpec((1, ATTN_TILE, ATTN_WIDTH), lambda bi, i: (bi, i, 0)),
        out_shape=jax.ShapeDtypeStruct((b, l, ATTN_WIDTH), bf16),
        compiler_params=_cparams(2),
        name="attn_prompt",
    )(sinks, q, k, k, v, v, g)


def _attn_sample_body(sink_ref, q_ref, kn_ref, vn_ref, ck_ref, cv_ref, g_ref, o_ref, nk_ref, nv_ref):
    kall = jnp.concatenate([ck_ref[0], kn_ref[0]], axis=0)
    vall = jnp.concatenate([cv_ref[0], vn_ref[0]], axis=0)
    o = _heads_attend(q_ref[0], kall, vall, sink_ref, None)
    o_ref[0] = _rms(o, g_ref[...]).astype(bf16)
    n_new = kn_ref.shape[1]
    nk_ref[0] = kall[n_new:]
    nv_ref[0] = vall[n_new:]


def _attn_sample(q, k_new, v_new, cache_k, cache_v, sinks, g):
    b, l, _ = q.shape
    wc = cache_k.shape[1]
    blk = lambda r, n: pl.BlockSpec((1, r, n), lambda bi: (bi, 0, 0))
    return pl.pallas_call(
        _attn_sample_body,
        grid=(b,),
        in_specs=[pl.BlockSpec(memory_space=pltpu.SMEM), blk(l, ATTN_WIDTH), blk(l, KV_WIDTH), blk(l, KV_WIDTH),
                  blk(wc, KV_WIDTH), blk(wc, KV_WIDTH), pl.BlockSpec((1, ATTN_WIDTH), lambda bi: (0, 0))],
        out_specs=[blk(l, ATTN_WIDTH), blk(wc, KV_WIDTH), blk(wc, KV_WIDTH)],
        out_shape=[jax.ShapeDtypeStruct((b, l, ATTN_WIDTH), bf16),
                   jax.ShapeDtypeStruct((b, wc, KV_WIDTH), f32), jax.ShapeDtypeStruct((b, wc, KV_WIDTH), f32)],
        compiler_params=_cparams(),
        name="attn_sample",
    )(sinks, q, k_new, v_new, cache_k, cache_v, g)


def _ssm_weights(a_re, a_im, log_dt, b_re, b_im, c_re, c_im):
    hp = lax.Precision.HIGHEST
    dt = jnp.exp(log_dt)[:, None]
    mag = jnp.exp(a_re * dt)
    ar, ai = mag * jnp.cos(a_im * dt), mag * jnp.sin(a_im * dt)
    den = a_re * a_re + a_im * a_im
    nr, ni = ar - 1.0, ai
    fr = ((nr * a_re + ni * a_im) / den)[..., None]
    fi = ((ni * a_re - nr * a_im) / den)[..., None]
    bbr, bbi = fr * b_re - fi * b_im, fr * b_im + fi * b_re
    pr, pi = [jnp.ones_like(ar)], [jnp.zeros_like(ar)]
    for _ in range(SSM_STEPS):
        pr, pi = pr + [pr[-1] * ar - pi[-1] * ai], pi + [pr[-1] * ai + pi[-1] * ar]
    rev_r, rev_i = jnp.stack(pr[SSM_STEPS - 1::-1]), jnp.stack(pi[SSM_STEPS - 1::-1])
    pr, pi = jnp.stack(pr), jnp.stack(pi)
    c_re_t, c_im_t = c_re.transpose(0, 2, 1), c_im.transpose(0, 2, 1)
    pr_l, pi_l = pr[:SSM_STEPS, :, :, None], pi[:SSM_STEPS, :, :, None]
    cpr = c_re_t[None] * pr_l - c_im_t[None] * pi_l
    cpi = c_re_t[None] * pi_l + c_im_t[None] * pr_l
    bbr_c, bbi_c = bbr.transpose(0, 2, 1)[None, :, :, :, None], bbi.transpose(0, 2, 1)[None, :, :, :, None]
    kl = jnp.sum(cpr[:, :, None] * bbr_c - cpi[:, :, None] * bbi_c, axis=3)
    kl = jnp.concatenate([jnp.zeros_like(kl[:1]), kl], axis=0)
    nd = SSM_STEPS // 2
    lag = 2 * jnp.arange(nd)[:, None, None] + jnp.arange(2)[None, None, :] - jnp.arange(2)[None, :, None]
    ksel = kl[lag + 1]
    ksel = ksel.reshape(nd, 2, 2, SSM_BLOCKS, GROUPS_PER_BLOCK, SSM_GROUP, SSM_GROUP)
    toep = ksel.transpose(3, 0, 1, 4, 5, 2, 6).reshape(SSM_BLOCKS, nd, 2 * LANES, 2 * SSM_GROUP)
    bbr_t, bbi_t = bbr.transpose(0, 2, 1), bbi.transpose(0, 2, 1)
    wo_r = rev_r[:, :, None, :] * bbr_t[None] - rev_i[:, :, None, :] * bbi_t[None]
    wo_i = rev_r[:, :, None, :] * bbi_t[None] + rev_i[:, :, None, :] * bbr_t[None]
    wo = jnp.stack([wo_r, wo_i]).reshape(2, SSM_STEPS, SSM_BLOCKS, GROUPS_PER_BLOCK, SSM_GROUP, SSM_STATE)
    wout = wo.transpose(2, 1, 3, 4, 0, 5).reshape(SSM_BLOCKS, SSM_STEPS * LANES, 2 * SSM_STATE)
    gr = c_re[None] * pr[1:, :, None, :] - c_im[None] * pi[1:, :, None, :]
    gi = c_re[None] * pi[1:, :, None, :] + c_im[None] * pr[1:, :, None, :]
    wi = jnp.stack([gr, -gi]).reshape(2, SSM_STEPS, SSM_BLOCKS, GROUPS_PER_BLOCK, SSM_GROUP, SSM_STATE)
    win = wi.transpose(2, 0, 3, 5, 1, 4).reshape(SSM_BLOCKS, 2 * STATE_COLS, SSM_STEPS * SSM_GROUP)
    lr, li = [pr[SSM_STEPS]], [pi[SSM_STEPS]]
    for _ in range(7):
        lr, li = lr + [lr[-1] * lr[-1] - li[-1] * li[-1]], li + [2.0 * lr[-1] * li[-1]]
    lev = jnp.stack([jnp.stack(lr), jnp.stack(li)], axis=1)
    lev = lev.reshape(8, 2, SSM_BLOCKS, STATE_COLS).transpose(2, 0, 1, 3)
    return toep.astype(bf16), wout.astype(bf16), win.astype(bf16), lev


def _ssm_chunk_rows(u_ref, nk):
    xs = [u_ref[0, pl.ds(s, nk, stride=SSM_STEPS), :] for s in range(SSM_STEPS)]
    pairs = [jnp.concatenate([xs[2 * p], xs[2 * p + 1]], axis=1).astype(bf16) for p in range(SSM_STEPS // 2)]
    return xs, pairs


def _ssm_intra(pairs, toep_ref, nk):
    nd = len(pairs)
    y = [None] * nd
    for d in range(nd):
        lhs = jnp.concatenate(pairs[:nd - d], axis=0) if nd - d > 1 else pairs[0]
        r = jnp.dot(lhs, toep_ref[d], preferred_element_type=f32)
        for p in range(nd - d):
            blk = r[p * nk:(p + 1) * nk]
            y[p + d] = blk if y[p + d] is None else y[p + d] + blk
    return y


def _shift_rows(x, sh):
    rows = lax.broadcasted_iota(i32, (x.shape[0], 1), 0)
    return jnp.where(rows >= sh, pltpu.roll(x, sh, axis=0), 0.0)


def _spread_matrix(k_in, sub):
    k = lax.broadcasted_iota(i32, (k_in, k_in * GROUPS_PER_BLOCK), 0)
    n = lax.broadcasted_iota(i32, (k_in, k_in * GROUPS_PER_BLOCK), 1)
    shift = int(math.log2(sub))
    same_a = jnp.right_shift(n, shift + int(math.log2(GROUPS_PER_BLOCK))) == jnp.right_shift(k, shift)
    return (same_a & ((n & (sub - 1)) == (k & (sub - 1)))).astype(bf16)


def _group_index(shape, axis, sub, offset=0):
    idx = lax.broadcasted_iota(i32, shape, axis) + offset
    return jnp.right_shift(idx, int(math.log2(sub))) & (GROUPS_PER_BLOCK - 1)


def _expand_block_diag(compact_ref, out_ref, row_sub, col_sub):
    rows, k_in = compact_ref.shape
    spread = _spread_matrix(k_in, col_sub)
    col_group = _group_index((2 * LANES, k_in * GROUPS_PER_BLOCK), 1, col_sub)
    for r0 in range(0, rows, 2 * LANES):
        full = jnp.dot(compact_ref[r0:r0 + 2 * LANES, :], spread, preferred_element_type=f32)
        row_group = _group_index(full.shape, 0, row_sub, r0)
        out_ref[r0:r0 + 2 * LANES, :] = jnp.where(row_group == col_group, full, 0.0).astype(bf16)


def _ssm_chunks(u_ref, d_ref, y_ref, toep_s, wout_s, win_s, entry_state):
    nk = u_ref.shape[1] // SSM_STEPS
    xs, pairs = _ssm_chunk_rows(u_ref, nk)
    y = _ssm_intra(pairs, toep_s, nk)
    s = jnp.dot(jnp.concatenate(pairs, axis=1), wout_s[...], preferred_element_type=f32)
    hprev, hr, hi = entry_state(s[:, :STATE_COLS], s[:, STATE_COLS:])
    y2 = _bdot(hprev, win_s[...])
    for st in range(SSM_STEPS):
        piece = (y[st // 2][:, (st % 2) * LANES:(st % 2 + 1) * LANES] + y2[:, st * LANES:(st + 1) * LANES]
                 + d_ref[...] * xs[st])
        y_ref[0, pl.ds(st, nk, stride=SSM_STEPS), :] = piece
    return hr, hi


def _ssm_body(u_ref, us_ref, h0r_ref, h0i_ref, toep_ref, wout_ref, win_ref, lev_ref, d_ref,
              y_ref, hr_ref, hi_ref, ys_ref, hrs_ref, his_ref, toep_s, wout_s, win_s):
    @pl.when(pl.program_id(1) == 0)
    def _():
        for d in range(SSM_STEPS // 2):
            _expand_block_diag(toep_ref.at[0, d], toep_s.at[d], SSM_GROUP, SSM_GROUP)
        _expand_block_diag(wout_ref.at[0], wout_s, SSM_GROUP, SSM_STATE)
        _expand_block_diag(win_ref.at[0], win_s, SSM_STATE, SSM_GROUP)

        def one_chunk(sr, si):
            h0r, h0i = h0r_ref[0], h0i_ref[0]
            ar, ai = lev_ref[0, 0, 0:1, :], lev_ref[0, 0, 1:2, :]
            return (jnp.concatenate([h0r, h0i], axis=1), sr + ar * h0r - ai * h0i, si + ar * h0i + ai * h0r)
        hrs_ref[0], his_ref[0] = _ssm_chunks(us_ref, d_ref, ys_ref, toep_s, wout_s, win_s, one_chunk)

    def scan_chunks(sr, si):
        nk = sr.shape[0]
        level = 0
        while (1 << level) < nk:
            ar, ai = lev_ref[0, level, 0:1, :], lev_ref[0, level, 1:2, :]
            tr, ti = _shift_rows(sr, 1 << level), _shift_rows(si, 1 << level)
            sr, si = sr + ar * tr - ai * ti, si + ar * ti + ai * tr
            level += 1
        return (jnp.concatenate([_shift_rows(sr, 1), _shift_rows(si, 1)], axis=1), sr[nk - 1:nk], si[nk - 1:nk])
    hr_ref[0, 0], hi_ref[0, 0] = _ssm_chunks(u_ref, d_ref, y_ref, toep_s, wout_s, win_s, scan_chunks)


def _ssm(u, us, h0r, h0i, toep, wout, win, lev, d):
    b, l, _ = u.shape
    rows = us.shape[1]
    nb = rows // SSM_STEPS
    wspec = lambda a: pl.BlockSpec((1,) + a.shape[1:], lambda j, bi: (j,) + (0,) * (a.ndim - 1))
    st = pl.BlockSpec((1, 1, 1, STATE_COLS), lambda j, bi: (bi, j, 0, 0))
    sst = pl.BlockSpec((1, nb, STATE_COLS), lambda j, bi: (j, 0, 0))
    seq = pl.BlockSpec((1, l, LANES), lambda j, bi: (bi, 0, j))
    sseq = pl.BlockSpec((1, rows, LANES), lambda j, bi: (0, 0, j))
    return pl.pallas_call(
        _ssm_body,
        grid=(SSM_BLOCKS, b),
        in_specs=[seq, sseq, sst, sst, wspec(toep), wspec(wout), wspec(win), wspec(lev),
                  pl.BlockSpec((1, LANES), lambda j, bi: (0, j))],
        out_specs=[seq, st, st, sseq, sst, sst],
        out_shape=[jax.ShapeDtypeStruct((b, l, SSM_WIDTH), f32),
                   jax.ShapeDtypeStruct((b, SSM_BLOCKS, 1, STATE_COLS), f32),
                   jax.ShapeDtypeStruct((b, SSM_BLOCKS, 1, STATE_COLS), f32),
                   jax.ShapeDtypeStruct((1, rows, SSM_WIDTH), f32),
                   jax.ShapeDtypeStruct((SSM_BLOCKS, nb, STATE_COLS), f32),
                   jax.ShapeDtypeStruct((SSM_BLOCKS, nb, STATE_COLS), f32)],
        scratch_shapes=[pltpu.VMEM((SSM_STEPS // 2, 2 * LANES, 2 * LANES), bf16),
                        pltpu.VMEM((SSM_STEPS * LANES, 2 * STATE_COLS), bf16),
                        pltpu.VMEM((2 * STATE_COLS, SSM_STEPS * LANES), bf16)],
        compiler_params=_cparams(2),
        name="ssm",
    )(u, us, h0r, h0i, toep, wout, win, lev, d)


ROUTER_COLS = LANES


def _mix_body(a_ref, ys_ref, x_ref, wglu_ref, bglu_ref, gs_ref, wout_ref, gf_ref, wr_ref, br_ref, tri_ref, upper_ref,
              x1_ref, xl_ref, wts_ref, lpos_ref, n_ref, loff_ref):
    y = ys_ref[...]
    y = 0.5 * y * (1.0 + jnp.tanh(math.sqrt(2.0 / math.pi) * (y + 0.044715 * (y * y * y))))
    y = y * jax.nn.sigmoid(_bdot(y, wglu_ref[...]) + bglu_ref[...])
    cat = jnp.concatenate([a_ref[...], _rms(y, gs_ref[...]).astype(bf16)], axis=1)
    x1 = x_ref[...] + jnp.dot(cat, wout_ref[...], preferred_element_type=f32)
    x1_ref[...] = x1
    hf = _rms(x1, gf_ref[...])

    logits = _bdot(hf, wr_ref[...]) + br_ref[...]
    le = logits[:, :N_EXPERTS]
    lg = logits[:, N_EXPERTS:N_EXPERTS + N_EXPERT_GROUPS]
    tm = le.shape[0]
    gmax = jnp.max(lg, axis=-1, keepdims=True)
    gi = lax.broadcasted_iota(i32, (tm, N_EXPERT_GROUPS), 1).astype(f32)
    gsel = jnp.min(jnp.where(lg == gmax, gi, float(N_EXPERT_GROUPS)), axis=-1, keepdims=True)
    pg = 1.0 / jnp.sum(jnp.exp(lg - gmax), axis=-1, keepdims=True)
    ei_int = lax.broadcasted_iota(i32, (tm, N_EXPERTS), 1)
    ei = ei_int.astype(f32)
    egroup = jnp.right_shift(ei_int, int(math.log2(EXPERTS_PER_GROUP))).astype(f32)
    lm = jnp.where(egroup == gsel, le, NEG)
    v1 = jnp.max(lm, axis=-1, keepdims=True)
    i1 = jnp.min(jnp.where(lm == v1, ei, float(N_EXPERTS)), axis=-1, keepdims=True)
    lm2 = jnp.where(ei == i1, NEG, lm)
    v2 = jnp.max(lm2, axis=-1, keepdims=True)
    i2 = jnp.min(jnp.where(lm2 == v2, ei, float(N_EXPERTS)), axis=-1, keepdims=True)
    ex = jnp.exp(v2 - v1)
    wts_ref[...] = jnp.concatenate([pg / (1.0 + ex), pg * ex / (1.0 + ex)], axis=1)

    oh1 = (ei == i1).astype(f32)
    oh2 = (ei == i2).astype(f32)
    hf_bf = hf.astype(bf16)
    sorted_row = lax.broadcasted_iota(i32, (ROW_TILE, SLOTS * ROW_TILE), 1).astype(f32)
    for h in range(tm // ROW_TILE):
        rows = slice(h * ROW_TILE, (h + 1) * ROW_TILE)
        both = (oh1[rows] + oh2[rows]).astype(bf16)
        before = jnp.dot(tri_ref[...], both, preferred_element_type=f32)
        count = jnp.sum(oh1[rows] + oh2[rows], axis=0, keepdims=True)
        lower = jnp.sum(jnp.dot(both, upper_ref[...], preferred_element_type=f32), axis=0, keepdims=True)
        lp1 = jnp.sum(oh1[rows] * (before + lower), axis=-1, keepdims=True)
        lp2 = jnp.sum(oh2[rows] * (before + lower), axis=-1, keepdims=True)
        lpos_ref[rows, :] = jnp.concatenate([lp1, lp2], axis=1)
        n_ref[0, h:h + 1, :] = count
        loff_ref[0, h:h + 1, :] = lower
        pick = ((sorted_row == lp1) | (sorted_row == lp2)).astype(bf16)
        xl = lax.dot_general(pick, hf_bf[rows], (((0,), (0,)), ((), ())), preferred_element_type=f32)
        _store_row_major(xl_ref, h * SLOTS * ROW_TILE, xl)


def _mix(a, ys, x, wglu, bglu, gs, wout, gf, wr, br, tri, upper, tm):
    t = x.shape[0]
    halves = tm // ROW_TILE
    row = lambda n: pl.BlockSpec((tm, n), lambda i: (i, 0))
    full = lambda arr: pl.BlockSpec(arr.shape, lambda i: (0,) * arr.ndim)
    per_tile = pl.BlockSpec((1, halves, N_EXPERTS), lambda i: (i, 0, 0))
    return pl.pallas_call(
        _mix_body,
        grid=(t // tm,),
        in_specs=[row(ATTN_WIDTH), row(SSM_WIDTH), row(D_MODEL), full(wglu), full(bglu), full(gs),
                  full(wout), full(gf), full(wr), full(br), full(tri), full(upper)],
        out_specs=[row(D_MODEL), pl.BlockSpec((SLOTS * tm * PIECES, LANES), lambda i: (i, 0)), row(2), row(2),
                   per_tile, per_tile],
        out_shape=[jax.ShapeDtypeStruct((t, D_MODEL), f32),
                   jax.ShapeDtypeStruct((SLOTS * t * PIECES, LANES), f32),
                   jax.ShapeDtypeStruct((t, 2), f32), jax.ShapeDtypeStruct((t, 2), f32),
                   jax.ShapeDtypeStruct((t // tm, halves, N_EXPERTS), f32),
                   jax.ShapeDtypeStruct((t // tm, halves, N_EXPERTS), f32)],
        compiler_params=_cparams(),
        name="mix",
    )(a, ys, x, wglu, bglu, gs, wout, gf, wr, br, tri, upper)


def _store_row_major(ref, first_row, x):
    for c in range(PIECES):
        ref[pl.ds(first_row * PIECES + c, x.shape[0], stride=PIECES), :] = x[:, c * LANES:(c + 1) * LANES]


def _load_row_major(ref, n_rows):
    return jnp.concatenate([ref[pl.ds(c, n_rows, stride=PIECES), :] for c in range(PIECES)], axis=1)


def _copy_rows(src, s_row, dst, d_row, n_rows, sem):
    return pltpu.make_async_copy(src.at[pl.ds(pl.multiple_of(s_row * PIECES, PIECES), n_rows * PIECES), :],
                                 dst.at[pl.ds(pl.multiple_of(d_row * PIECES, PIECES), n_rows * PIECES), :], sem)


def _for_each_piece(n, fn):
    off = 0
    for piece in RUN_PIECES:
        @pl.when((n & piece) != 0)
        def _(off=off, piece=piece):
            fn(off, piece)
        off = off + (n & piece)


def _copy_run(n, fn):
    def pick(lo, hi):
        if lo == hi:
            if lo > 0:
                fn(0, lo)
            return
        mid = (lo + hi) // 2
        pl.when(n <= mid)(functools.partial(pick, lo, mid))
        pl.when(n > mid)(functools.partial(pick, mid + 1, hi))

    pl.when(n <= SHORT_RUN)(functools.partial(pick, 0, SHORT_RUN))
    pl.when(n > SHORT_RUN)(functools.partial(_for_each_piece, n, fn))


def _experts_body(te_ref, tpos_ref, tvalid_ref, tlo_ref, thi_ref, n_ref, cum_ref, loff_ref,
                  xlp_hbm, xls_hbm, wg_ref, wu_ref, wd_ref, o_ref, xbuf, wg_s, wu_s, wd_s, sem, *, prompt_tiles):
    i = pl.program_id(0)
    last = pl.num_programs(0) - 1

    def fetch(t):
        slot = t % 2
        e, lo, valid = te_ref[t], tpos_ref[t], tvalid_ref[t]

        @pl.when(valid < ROW_TILE)
        def _():
            xbuf[slot] = jnp.zeros((ROW_TILE * PIECES, LANES), f32)

        def run(tau, c):
            k = tau * N_EXPERTS + e
            s, n = cum_ref[k], n_ref[k]
            a = jnp.maximum(s, lo)
            length = jnp.maximum(jnp.minimum(s + n, lo + ROW_TILE) - a, 0)
            local = loff_ref[k] + (a - s)

            def start_from(src_hbm, tile):
                _copy_run(length, lambda off, piece: _copy_rows(
                    src_hbm, tile * (SLOTS * ROW_TILE) + local + off, xbuf.at[slot], a - lo + off, piece,
                    sem.at[slot]).start())

            @pl.when(tau < prompt_tiles)
            def _():
                start_from(xlp_hbm, tau)

            @pl.when(tau >= prompt_tiles)
            def _():
                start_from(xls_hbm, tau - prompt_tiles)
            return c
        lax.fori_loop(tlo_ref[t], thi_ref[t] + 1, run, 0)

    @pl.when((i == 0) & (tvalid_ref[0] > 0))
    def _():
        fetch(0)

    nxt = jnp.minimum(i + 1, last)

    @pl.when((i < last) & (tvalid_ref[nxt] > 0))
    def _():
        fetch(nxt)

    @pl.when((i == 0) | (te_ref[i] != te_ref[jnp.maximum(i - 1, 0)]))
    def _():
        wg_s[...] = wg_ref[0].astype(bf16)
        wu_s[...] = wu_ref[0].astype(bf16)
        wd_s[...] = wd_ref[0].astype(bf16)

    valid = tvalid_ref[i]

    @pl.when(valid > 0)
    def _():
        slot = i % 2
        _for_each_piece(valid, lambda off, piece: _copy_rows(
            xlp_hbm, 0, xbuf.at[slot], 0, piece, sem.at[slot]).wait())
        x = _load_row_major(xbuf.at[slot], ROW_TILE).astype(bf16)
        hg = jnp.dot(x, wg_s[...], preferred_element_type=f32)
        hu = jnp.dot(x, wu_s[...], preferred_element_type=f32)
        y = jnp.dot((hg * jax.nn.sigmoid(hg) * hu).astype(bf16), wd_s[...], preferred_element_type=f32)
        _store_row_major(o_ref, 0, y)

    @pl.when(valid == 0)
    def _():
        o_ref[...] = jnp.zeros_like(o_ref)


def _experts(tables, xl_p, xl_s, wg, wu, wd, tiles):
    n_pre = len(tables)
    prompt_tiles = xl_p.shape[0] // (SLOTS * ROW_TILE * PIECES)
    wspec = lambda a: pl.BlockSpec((1,) + a.shape[1:], lambda i, te, *_: (te[i], 0, 0))
    return pl.pallas_call(
        functools.partial(_experts_body, prompt_tiles=prompt_tiles),
        grid_spec=pltpu.PrefetchScalarGridSpec(
            num_scalar_prefetch=n_pre,
            grid=(tiles,),
            in_specs=[pl.BlockSpec(memory_space=pl.ANY), pl.BlockSpec(memory_space=pl.ANY), wspec(wg), wspec(wu), wspec(wd)],
            out_specs=pl.BlockSpec((ROW_TILE * PIECES, LANES), lambda i, *_: (i, 0)),
            scratch_shapes=[pltpu.VMEM((2, ROW_TILE * PIECES, LANES), f32),
                            pltpu.VMEM(wg.shape[1:], bf16), pltpu.VMEM(wu.shape[1:], bf16), pltpu.VMEM(wd.shape[1:], bf16),
                            pltpu.SemaphoreType.DMA((2,))]),
        out_shape=jax.ShapeDtypeStruct((tiles * ROW_TILE * PIECES, LANES), f32),
        compiler_params=_cparams(),
        name="moe_experts",
    )(*tables, xl_p, xl_s, wg, wu, wd)


def _combine_body(n_ref, gpos_ref, loff_ref, ys_hbm, xp_ref, xs_ref, wp_ref, ws_ref, lp_ref, ls_ref, g_ref,
                  op_ref, os_ref, ybuf, sem, *, prompt_tiles):
    i = pl.program_id(0)
    last = pl.num_programs(0) - 1
    tile_rows = SLOTS * ROW_TILE

    def fetch(t):
        slot = t % 2

        def run(e, c):
            k = t * N_EXPERTS + e
            _copy_run(n_ref[k], lambda off, piece: _copy_rows(
                ys_hbm, gpos_ref[k] + off, ybuf.at[slot], loff_ref[k] + off, piece, sem.at[slot]).start())
            return c
        lax.fori_loop(0, N_EXPERTS, run, 0)

    @pl.when(i == 0)
    def _():
        fetch(0)

    @pl.when(i < last)
    def _():
        fetch(jnp.minimum(i + 1, last))

    slot = i % 2
    _copy_rows(ys_hbm, 0, ybuf.at[slot], 0, tile_rows, sem.at[slot]).wait()
    yl = _load_row_major(ybuf.at[slot], tile_rows).astype(bf16)
    sorted_row = lax.broadcasted_iota(i32, (ROW_TILE, tile_rows), 1).astype(f32)

    def finish(x_ref, w_ref, l_ref, o_ref):
        w, lp = w_ref[...], l_ref[...]
        y1 = jnp.dot((sorted_row == lp[:, 0:1]).astype(bf16), yl, preferred_element_type=f32)
        y2 = jnp.dot((sorted_row == lp[:, 1:2]).astype(bf16), yl, preferred_element_type=f32)
        o_ref[...] = _rms(x_ref[...] + (w[:, 0:1] * y1 + w[:, 1:2] * y2), g_ref[...])

    @pl.when(i < prompt_tiles)
    def _():
        finish(xp_ref, wp_ref, lp_ref, op_ref)

    @pl.when(i >= prompt_tiles)
    def _():
        finish(xs_ref, ws_ref, ls_ref, os_ref)


def _combine(tables, ys, x1_p, x1_s, wts_p, wts_s, lpos_p, lpos_s, g):
    prompt_tiles = x1_p.shape[0] // ROW_TILE
    tiles = prompt_tiles + x1_s.shape[0] // ROW_TILE
    p_rows = lambda n: pl.BlockSpec((ROW_TILE, n), lambda i, *_: (jnp.minimum(i, prompt_tiles - 1), 0))
    s_rows = lambda n: pl.BlockSpec((ROW_TILE, n), lambda i, *_: (jnp.maximum(i - prompt_tiles, 0), 0))
    return pl.pallas_call(
        functools.partial(_combine_body, prompt_tiles=prompt_tiles),
        grid_spec=pltpu.PrefetchScalarGridSpec(
            num_scalar_prefetch=len(tables),
            grid=(tiles,),
            in_specs=[pl.BlockSpec(memory_space=pl.ANY), p_rows(D_MODEL), s_rows(D_MODEL), p_rows(2), s_rows(2),
                      p_rows(2), s_rows(2), pl.BlockSpec((1, D_MODEL), lambda i, *_: (0, 0))],
            out_specs=[p_rows(D_MODEL), s_rows(D_MODEL)],
            scratch_shapes=[pltpu.VMEM((2, SLOTS * ROW_TILE * PIECES, LANES), f32), pltpu.SemaphoreType.DMA((2,))]),
        out_shape=[jax.ShapeDtypeStruct(x1_p.shape, f32), jax.ShapeDtypeStruct(x1_s.shape, f32)],
        compiler_params=_cparams(),
        name="moe_combine",
    )(*tables, ys, x1_p, x1_s, wts_p, wts_s, lpos_p, lpos_s, g)


def _moe_tables(n, loff, tiles):
    cum = jnp.cumsum(n, axis=0) - n
    counts = jnp.sum(n, axis=0)
    padded = (counts + ROW_TILE - 1) // ROW_TILE * ROW_TILE
    ends = jnp.cumsum(padded)
    starts = ends - padded
    first = jnp.arange(tiles, dtype=i32) * ROW_TILE
    expert = jnp.minimum(jnp.sum((first[:, None] >= ends[None, :]).astype(i32), axis=1), N_EXPERTS - 1)
    sel = expert[:, None] == jnp.arange(N_EXPERTS)[None, :]
    pick = lambda v: jnp.sum(jnp.where(sel, v[None, :], 0), axis=1)
    pos = first - pick(starts)
    valid = jnp.where(first < ends[-1], jnp.clip(pick(counts) - pos, 0, ROW_TILE), 0)
    cum_t, n_t = cum.T[expert], n.T[expert]
    touches = (cum_t + n_t > pos[:, None]) & (cum_t < (pos + ROW_TILE)[:, None]) & (n_t > 0)
    tau = jnp.arange(n.shape[0], dtype=i32)[None, :]
    lo = jnp.min(jnp.where(touches, tau, n.shape[0]), axis=1)
    hi = jnp.max(jnp.where(touches, tau, -1), axis=1)
    as_i32 = lambda v: v.astype(i32)
    flat = lambda v: v.reshape(-1).astype(i32)
    expert_tables = tuple(map(as_i32, (expert, pos, valid, lo, hi))) + (flat(n), flat(cum), flat(loff))
    combine_tables = (flat(n), flat(starts[None, :] + cum), flat(loff))
    return expert_tables, combine_tables


def kernel(x_prompt, x_sample, cache_k, cache_v, state_ssm_re, state_ssm_im, g_norm_mix, w_in, attn_sinks, ssm_a_re,
           ssm_a_im, ssm_log_dt, ssm_b_re, ssm_b_im, ssm_c_re, ssm_c_im, ssm_d, w_glu, b_glu, g_attn_out, g_ssm_out,
           w_out, g_norm_ffn, w_router_group, b_router_group, w_router_expert, b_router_expert, w_exp_gate, w_exp_up,
           w_exp_down, g_final):
    bp, lp, _ = x_prompt.shape
    bs, ls, _ = x_sample.shape
    depth = w_in.shape[0]
    assert depth == 1 and ls == SSM_STEPS and lp % ATTN_TILE == 0 and (bs * ls) % ROW_TILE == 0
    tp, ts = bp * lp, bs * ls
    wc = cache_k.shape[2]
    row2 = lambda v: v.reshape(1, -1)

    xp = x_prompt.reshape(tp, D_MODEL)
    xs = x_sample.reshape(ts, D_MODEL)
    w_in_bf = w_in[0].astype(bf16)
    qp, kp, vp, up = _proj(xp, row2(g_norm_mix[0]), w_in_bf, 512)
    qs, kq, vq, us = _proj(xs, row2(g_norm_mix[0]), w_in_bf, ts)

    sinks = attn_sinks[0]
    g_att = row2(g_attn_out[0])
    ap = _attn_prompt(qp.reshape(bp, lp, -1), kp.reshape(bp, lp, -1), vp.reshape(bp, lp, -1), sinks, g_att)
    a_s, k_roll, v_roll = _attn_sample(qs.reshape(bs, ls, -1), kq.reshape(bs, ls, -1), vq.reshape(bs, ls, -1),
                                       cache_k[0].reshape(bs, wc, KV_WIDTH), cache_v[0].reshape(bs, wc, KV_WIDTH),
                                       sinks, g_att)

    toep, wso, wsi, lev = _ssm_weights(ssm_a_re[0], ssm_a_im[0], ssm_log_dt[0], ssm_b_re[0], ssm_b_im[0],
                                       ssm_c_re[0], ssm_c_im[0])
    d_row = row2(ssm_d[0])
    to_blocks = lambda h: h.reshape(bs, SSM_BLOCKS, STATE_COLS).transpose(1, 0, 2)
    from_blocks = lambda h: h.transpose(1, 0, 2).reshape(bs, SSM_GROUPS, SSM_STATE)
    yp, hrp, hip, ysm, hrs, his = _ssm(up.reshape(bp, lp, -1), us.reshape(1, ts, -1), to_blocks(state_ssm_re[0]),
                                       to_blocks(state_ssm_im[0]), toep, wso, wsi, lev, d_row)

    wr = jnp.zeros((D_MODEL, ROUTER_COLS), f32)
    wr = wr.at[:, :N_EXPERTS].set(w_router_expert[0]).at[:, N_EXPERTS:N_EXPERTS + N_EXPERT_GROUPS].set(w_router_group[0])
    br = jnp.zeros((1, ROUTER_COLS), f32)
    br = br.at[0, :N_EXPERTS].set(b_router_expert[0]).at[0, N_EXPERTS:N_EXPERTS + N_EXPERT_GROUPS].set(b_router_group[0])
    tri = jnp.tril(jnp.ones((ROW_TILE, ROW_TILE), bf16), -1)
    upper = jnp.triu(jnp.ones((N_EXPERTS, N_EXPERTS), bf16), 1)
    mix_w = (w_glu[0].astype(bf16), row2(b_glu[0]), row2(g_ssm_out[0]), w_out[0].astype(bf16), row2(g_norm_ffn[0]),
             wr.astype(bf16), br, tri, upper)
    x1p, xlp, wtp, lpp, n_p, off_p = _mix(ap.reshape(tp, -1), yp.reshape(tp, -1), xp, *mix_w, 2 * ROW_TILE)
    x1s, xls, wts, lps, n_s, off_s = _mix(a_s.reshape(ts, -1), ysm.reshape(ts, -1), xs, *mix_w, ROW_TILE)

    per_tile = lambda p, s: jnp.concatenate([p.reshape(-1, N_EXPERTS), s.reshape(-1, N_EXPERTS)], axis=0).astype(i32)
    tiles = (SLOTS * (tp + ts)) // ROW_TILE + N_EXPERTS
    expert_tables, combine_tables = _moe_tables(per_tile(n_p, n_s), per_tile(off_p, off_s), tiles)
    expert_out = _experts(expert_tables, xlp, xls, w_exp_gate[0], w_exp_up[0], w_exp_down[0], tiles)
    y_p, y_s = _combine(combine_tables, expert_out, x1p, x1s, wtp, wts, lpp, lps, row2(g_final))

    kvshape = lambda a, b: a.reshape(1, b, -1, N_KV_HEADS, HEAD_DIM)
    block_state = lambda h: h.reshape(bp, SSM_GROUPS, SSM_STATE)[None]
    wcp = min(WINDOW, lp)
    return (y_p.reshape(bp, lp, D_MODEL), y_s.reshape(bs, ls, D_MODEL),
            kvshape(kp.reshape(bp, lp, -1)[:, lp - wcp:], bp), kvshape(vp.reshape(bp, lp, -1)[:, lp - wcp:], bp),
            block_state(hrp), block_state(hip),
            kvshape(k_roll, bs), kvshape(v_roll, bs),
            from_blocks(hrs)[None], from_blocks(his)[None])
```

```python
import functools
import math

import jax
import jax.numpy as jnp
from jax import lax
from jax.experimental import pallas as pl
from jax.experimental.pallas import tpu as pltpu

f32, bf16, i32 = jnp.float32, jnp.bfloat16, jnp.int32

D_MODEL = 1024
CHUNK = 64
N_BACK = 2
WINDOW = 128
ATTN_WIDTH = 512
HEAD_DIM = 64
N_KV_HEADS = 2
Q_PER_KV = 4
KV_WIDTH = 128
SSM_WIDTH = 512
SSM_GROUP = 16
SSM_GROUPS = 32
SSM_STATE = 64
PROJ_WIDTH = 1280
N_EXPERT_GROUPS = 4
EXPERTS_PER_GROUP = 8
N_EXPERTS = 32
D_EXPERT = 512
EPS = 1e-6
NEG = -1e30

LANES = 128
SSM_STEPS = 16
SSM_BLOCKS = SSM_WIDTH // LANES
GROUPS_PER_BLOCK = LANES // SSM_GROUP
STATE_COLS = GROUPS_PER_BLOCK * SSM_STATE
ROW_TILE = 256
SLOTS = 2
PIECES = D_MODEL // LANES
RUN_PIECES = tuple(1 << b for b in reversed(range(int(math.log2(ROW_TILE)) + 1)))
SHORT_RUN = 32
VMEM_LIMIT = 56 * 1024 * 1024


def _cparams(n_axes=1, limit=VMEM_LIMIT):
    return pltpu.CompilerParams(dimension_semantics=("arbitrary",) * n_axes, vmem_limit_bytes=limit)


def _rms(x, g):
    return x * lax.rsqrt(jnp.mean(x * x, axis=-1, keepdims=True) + EPS) * g


def _bdot(a, b):
    return jnp.dot(a.astype(bf16), b.astype(bf16), preferred_element_type=f32)


def _proj_body(x_ref, g_ref, w_ref, q_ref, k_ref, v_ref, u_ref):
    h = _rms(x_ref[...], g_ref[...])
    z = _bdot(h, w_ref[...])
    q_ref[...] = z[:, :ATTN_WIDTH] * (HEAD_DIM ** -0.5)
    k_ref[...] = z[:, ATTN_WIDTH:ATTN_WIDTH + KV_WIDTH]
    v_ref[...] = z[:, ATTN_WIDTH + KV_WIDTH:ATTN_WIDTH + 2 * KV_WIDTH]
    u_ref[...] = z[:, ATTN_WIDTH + 2 * KV_WIDTH:]


def _proj(x2d, g, w_bf, tm):
    t = x2d.shape[0]
    row = lambda n: pl.BlockSpec((tm, n), lambda i: (i, 0))
    full = lambda a: pl.BlockSpec(a.shape, lambda i: (0,) * a.ndim)
    return pl.pallas_call(
        _proj_body,
        grid=(t // tm,),
        in_specs=[row(D_MODEL), full(g), full(w_bf)],
        out_specs=[row(ATTN_WIDTH), row(KV_WIDTH), row(KV_WIDTH), row(SSM_WIDTH)],
        out_shape=[jax.ShapeDtypeStruct((t, n), f32) for n in (ATTN_WIDTH, KV_WIDTH, KV_WIDTH, SSM_WIDTH)],
        compiler_params=_cparams(),
        name="proj",
    )(x2d, g, w_bf)


def _sink_column(sink_ref, kv, rows_per_head):
    r = lax.broadcasted_iota(i32, (Q_PER_KV * rows_per_head, 1), 0)
    col = jnp.full((Q_PER_KV * rows_per_head, 1), sink_ref[kv * Q_PER_KV], f32)
    for j in range(1, Q_PER_KV):
        col = jnp.where(r >= j * rows_per_head, sink_ref[kv * Q_PER_KV + j], col)
    return col


def _attend(qs, kc, vc, sink_col, valid):
    s = lax.dot_general(qs.astype(bf16), kc.astype(bf16), (((1,), (1,)), ((), ())), preferred_element_type=f32)
    if valid is not None:
        s = jnp.where(valid, s, NEG)
    m = jnp.maximum(jnp.max(s, axis=-1, keepdims=True), sink_col)
    p = jnp.exp(s - m)
    denom = jnp.sum(p, axis=-1, keepdims=True) + jnp.exp(sink_col - m)
    return _bdot(p, vc) / denom


def _heads_attend(q, k, v, sink_ref, valid):
    rows = q.shape[0]
    pieces = []
    for kv in range(N_KV_HEADS):
        qs = jnp.concatenate(
            [q[:, (kv * Q_PER_KV + j) * HEAD_DIM:(kv * Q_PER_KV + j + 1) * HEAD_DIM] for j in range(Q_PER_KV)], axis=0)
        o = _attend(qs, k[:, kv * HEAD_DIM:(kv + 1) * HEAD_DIM], v[:, kv * HEAD_DIM:(kv + 1) * HEAD_DIM],
                    _sink_column(sink_ref, kv, rows), valid)
        pieces += [o[j * rows:(j + 1) * rows] for j in range(Q_PER_KV)]
    return jnp.concatenate(pieces, axis=1)


ATTN_TILE = 256
CHUNKS_PER_TILE = ATTN_TILE // CHUNK
KEY_SPAN = (N_BACK + 1) * CHUNK


def _attn_prompt_body(sink_ref, q_ref, kp_ref, kc_ref, vp_ref, vc_ref, g_ref, o_ref):
    i = pl.program_id(1)
    kwin = jnp.concatenate([kp_ref[0], kc_ref[0]], axis=0)
    vwin = jnp.concatenate([vp_ref[0], vc_ref[0]], axis=0)
    key_chunk = lax.broadcasted_iota(i32, (1, KEY_SPAN), 1) // CHUNK
    for c in range(CHUNKS_PER_TILE):
        valid = (i * CHUNKS_PER_TILE + c - N_BACK + key_chunk) >= 0
        o = _heads_attend(q_ref[0, c * CHUNK:(c + 1) * CHUNK, :], kwin[c * CHUNK:c * CHUNK + KEY_SPAN],
                          vwin[c * CHUNK:c * CHUNK + KEY_SPAN], sink_ref, valid)
        o_ref[0, c * CHUNK:(c + 1) * CHUNK, :] = _rms(o, g_ref[...]).astype(bf16)


def _attn_prompt(q, k, v, sinks, g):
    b, l, _ = q.shape
    back = N_BACK * CHUNK
    per = ATTN_TILE // back
    prev = pl.BlockSpec((1, back, KV_WIDTH), lambda bi, i: (bi, jnp.maximum(i * per - 1, 0), 0))
    cur = pl.BlockSpec((1, ATTN_TILE, KV_WIDTH), lambda bi, i: (bi, i, 0))
    return pl.pallas_call(
        _attn_prompt_body,
        grid=(b, l // ATTN_TILE),
        in_specs=[pl.BlockSpec(memory_space=pltpu.SMEM),
                  pl.BlockSpec((1, ATTN_TILE, ATTN_WIDTH), lambda bi, i: (bi, i, 0)),
                  prev, cur, prev, cur,
                  pl.BlockSpec((1, ATTN_WIDTH), lambda bi, i: (0, 0))],
        out_specs=pl.BlockSpec((1, ATTN_TILE, ATTN_WIDTH), lambda bi, i: (bi, i, 0)),
        out_shape=jax.ShapeDtypeStruct((b, l, ATTN_WIDTH), bf16),
        compiler_params=_cparams(2),
        name="attn_prompt",
    )(sinks, q, k, k, v, v, g)


def _attn_sample_body(sink_ref, q_ref, kn_ref, vn_ref, ck_ref, cv_ref, g_ref, o_ref, nk_ref, nv_ref):
    kall = jnp.concatenate([ck_ref[0], kn_ref[0]], axis=0)
    vall = jnp.concatenate([cv_ref[0], vn_ref[0]], axis=0)
    o = _heads_attend(q_ref[0], kall, vall, sink_ref, None)
    o_ref[0] = _rms(o, g_ref[...]).astype(bf16)
    n_new = kn_ref.shape[1]
    nk_ref[0] = kall[n_new:]
    nv_ref[0] = vall[n_new:]


def _attn_sample(q, k_new, v_new, cache_k, cache_v, sinks, g):
    b, l, _ = q.shape
    wc = cache_k.shape[1]
    blk = lambda r, n: pl.BlockSpec((1, r, n), lambda bi: (bi, 0, 0))
    return pl.pallas_call(
        _attn_sample_body,
        grid=(b,),
        in_specs=[pl.BlockSpec(memory_space=pltpu.SMEM), blk(l, ATTN_WIDTH), blk(l, KV_WIDTH), blk(l, KV_WIDTH),
                  blk(wc, KV_WIDTH), blk(wc, KV_WIDTH), pl.BlockSpec((1, ATTN_WIDTH), lambda bi: (0, 0))],
        out_specs=[blk(l, ATTN_WIDTH), blk(wc, KV_WIDTH), blk(wc, KV_WIDTH)],
        out_shape=[jax.ShapeDtypeStruct((b, l, ATTN_WIDTH), bf16),
                   jax.ShapeDtypeStruct((b, wc, KV_WIDTH), f32), jax.ShapeDtypeStruct((b, wc, KV_WIDTH), f32)],
        compiler_params=_cparams(),
        name="attn_sample",
    )(sinks, q, k_new, v_new, cache_k, cache_v, g)


def _ssm_weights(a_re, a_im, log_dt, b_re, b_im, c_re, c_im):
    hp = lax.Precision.HIGHEST
    dt = jnp.exp(log_dt)[:, None]
    mag = jnp.exp(a_re * dt)
    ar, ai = mag * jnp.cos(a_im * dt), mag * jnp.sin(a_im * dt)
    den = a_re * a_re + a_im * a_im
    nr, ni = ar - 1.0, ai
    fr = ((nr * a_re + ni * a_im) / den)[..., None]
    fi = ((ni * a_re - nr * a_im) / den)[..., None]
    bbr, bbi = fr * b_re - fi * b_im, fr * b_im + fi * b_re
    pr, pi = [jnp.ones_like(ar)], [jnp.zeros_like(ar)]
    for _ in range(SSM_STEPS):
        pr, pi = pr + [pr[-1] * ar - pi[-1] * ai], pi + [pr[-1] * ai + pi[-1] * ar]
    rev_r, rev_i = jnp.stack(pr[SSM_STEPS - 1::-1]), jnp.stack(pi[SSM_STEPS - 1::-1])
    pr, pi = jnp.stack(pr), jnp.stack(pi)
    c_re_t, c_im_t = c_re.transpose(0, 2, 1), c_im.transpose(0, 2, 1)
    pr_l, pi_l = pr[:SSM_STEPS, :, :, None], pi[:SSM_STEPS, :, :, None]
    cpr = c_re_t[None] * pr_l - c_im_t[None] * pi_l
    cpi = c_re_t[None] * pi_l + c_im_t[None] * pr_l
    bbr_c, bbi_c = bbr.transpose(0, 2, 1)[None, :, :, :, None], bbi.transpose(0, 2, 1)[None, :, :, :, None]
    kl = jnp.sum(cpr[:, :, None] * bbr_c - cpi[:, :, None] * bbi_c, axis=3)
    kl = jnp.concatenate([jnp.zeros_like(kl[:1]), kl], axis=0)
    nd = SSM_STEPS // 2
    lag = 2 * jnp.arange(nd)[:, None, None] + jnp.arange(2)[None, None, :] - jnp.arange(2)[None, :, None]
    ksel = kl[lag + 1]
    ksel = ksel.reshape(nd, 2, 2, SSM_BLOCKS, GROUPS_PER_BLOCK, SSM_GROUP, SSM_GROUP)
    toep = ksel.transpose(3, 0, 1, 4, 5, 2, 6).reshape(SSM_BLOCKS, nd, 2 * LANES, 2 * SSM_GROUP)
    bbr_t, bbi_t = bbr.transpose(0, 2, 1), bbi.transpose(0, 2, 1)
    wo_r = rev_r[:, :, None, :] * bbr_t[None] - rev_i[:, :, None, :] * bbi_t[None]
    wo_i = rev_r[:, :, None, :] * bbi_t[None] + rev_i[:, :, None, :] * bbr_t[None]
    wo = jnp.stack([wo_r, wo_i]).reshape(2, SSM_STEPS, SSM_BLOCKS, GROUPS_PER_BLOCK, SSM_GROUP, SSM_STATE)
    wout = wo.transpose(2, 1, 3, 4, 0, 5).reshape(SSM_BLOCKS, SSM_STEPS * LANES, 2 * SSM_STATE)
    gr = c_re[None] * pr[1:, :, None, :] - c_im[None] * pi[1:, :, None, :]
    gi = c_re[None] * pi[1:, :, None, :] + c_im[None] * pr[1:, :, None, :]
    wi = jnp.stack([gr, -gi]).reshape(2, SSM_STEPS, SSM_BLOCKS, GROUPS_PER_BLOCK, SSM_GROUP, SSM_STATE)
    win = wi.transpose(2, 0, 3, 5, 1, 4).reshape(SSM_BLOCKS, 2 * STATE_COLS, SSM_STEPS * SSM_GROUP)
    lr, li = [pr[SSM_STEPS]], [pi[SSM_STEPS]]
    for _ in range(7):
        lr, li = lr + [lr[-1] * lr[-1] - li[-1] * li[-1]], li + [2.0 * lr[-1] * li[-1]]
    lev = jnp.stack([jnp.stack(lr), jnp.stack(li)], axis=1)
    lev = lev.reshape(8, 2, SSM_BLOCKS, STATE_COLS).transpose(2, 0, 1, 3)
    return toep.astype(bf16), wout.astype(bf16), win.astype(bf16), lev


def _ssm_chunk_rows(u_ref, nk):
    xs = [u_ref[0, pl.ds(s, nk, stride=SSM_STEPS), :] for s in range(SSM_STEPS)]
    pairs = [jnp.concatenate([xs[2 * p], xs[2 * p + 1]], axis=1).astype(bf16) for p in range(SSM_STEPS // 2)]
    return xs, pairs


def _ssm_intra(pairs, toep_ref, nk):
    nd = len(pairs)
    y = [None] * nd
    for d in range(nd):
        lhs = jnp.concatenate(pairs[:nd - d], axis=0) if nd - d > 1 else pairs[0]
        r = jnp.dot(lhs, toep_ref[d], preferred_element_type=f32)
        for p in range(nd - d):
            blk = r[p * nk:(p + 1) * nk]
            y[p + d] = blk if y[p + d] is None else y[p + d] + blk
    return y


def _shift_rows(x, sh):
    rows = lax.broadcasted_iota(i32, (x.shape[0], 1), 0)
    return jnp.where(rows >= sh, pltpu.roll(x, sh, axis=0), 0.0)


def _spread_matrix(k_in, sub):
    k = lax.broadcasted_iota(i32, (k_in, k_in * GROUPS_PER_BLOCK), 0)
    n = lax.broadcasted_iota(i32, (k_in, k_in * GROUPS_PER_BLOCK), 1)
    shift = int(math.log2(sub))
    same_a = jnp.right_shift(n, shift + int(math.log2(GROUPS_PER_BLOCK))) == jnp.right_shift(k, shift)
    return (same_a & ((n & (sub - 1)) == (k & (sub - 1)))).astype(bf16)


def _group_index(shape, axis, sub, offset=0):
    idx = lax.broadcasted_iota(i32, shape, axis) + offset
    return jnp.right_shift(idx, int(math.log2(sub))) & (GROUPS_PER_BLOCK - 1)


def _expand_block_diag(compact_ref, out_ref, row_sub, col_sub):
    rows, k_in = compact_ref.shape
    spread = _spread_matrix(k_in, col_sub)
    col_group = _group_index((2 * LANES, k_in * GROUPS_PER_BLOCK), 1, col_sub)
    for r0 in range(0, rows, 2 * LANES):
        full = jnp.dot(compact_ref[r0:r0 + 2 * LANES, :], spread, preferred_element_type=f32)
        row_group = _group_index(full.shape, 0, row_sub, r0)
        out_ref[r0:r0 + 2 * LANES, :] = jnp.where(row_group == col_group, full, 0.0).astype(bf16)


def _ssm_chunks(u_ref, d_ref, y_ref, toep_s, wout_s, win_s, entry_state):
    nk = u_ref.shape[1] // SSM_STEPS
    xs, pairs = _ssm_chunk_rows(u_ref, nk)
    y = _ssm_intra(pairs, toep_s, nk)
    s = jnp.dot(jnp.concatenate(pairs, axis=1), wout_s[...], preferred_element_type=f32)
    hprev, hr, hi = entry_state(s[:, :STATE_COLS], s[:, STATE_COLS:])
    y2 = _bdot(hprev, win_s[...])
    for st in range(SSM_STEPS):
        piece = (y[st // 2][:, (st % 2) * LANES:(st % 2 + 1) * LANES] + y2[:, st * LANES:(st + 1) * LANES]
                 + d_ref[...] * xs[st])
        y_ref[0, pl.ds(st, nk, stride=SSM_STEPS), :] = piece
    return hr, hi


def _ssm_body(u_ref, us_ref, h0r_ref, h0i_ref, toep_ref, wout_ref, win_ref, lev_ref, d_ref,
              y_ref, hr_ref, hi_ref, ys_ref, hrs_ref, his_ref, toep_s, wout_s, win_s):
    @pl.when(pl.program_id(1) == 0)
    def _():
        for d in range(SSM_STEPS // 2):
            _expand_block_diag(toep_ref.at[0, d], toep_s.at[d], SSM_GROUP, SSM_GROUP)
        _expand_block_diag(wout_ref.at[0], wout_s, SSM_GROUP, SSM_STATE)
        _expand_block_diag(win_ref.at[0], win_s, SSM_STATE, SSM_GROUP)

        def one_chunk(sr, si):
            h0r, h0i = h0r_ref[0], h0i_ref[0]
            ar, ai = lev_ref[0, 0, 0:1, :], lev_ref[0, 0, 1:2, :]
            return (jnp.concatenate([h0r, h0i], axis=1), sr + ar * h0r - ai * h0i, si + ar * h0i + ai * h0r)
        hrs_ref[0], his_ref[0] = _ssm_chunks(us_ref, d_ref, ys_ref, toep_s, wout_s, win_s, one_chunk)

    def scan_chunks(sr, si):
        nk = sr.shape[0]
        level = 0
        while (1 << level) < nk:
            ar, ai = lev_ref[0, level, 0:1, :], lev_ref[0, level, 1:2, :]
            tr, ti = _shift_rows(sr, 1 << level), _shift_rows(si, 1 << level)
            sr, si = sr + ar * tr - ai * ti, si + ar * ti + ai * tr
            level += 1
        return (jnp.concatenate([_shift_rows(sr, 1), _shift_rows(si, 1)], axis=1), sr[nk - 1:nk], si[nk - 1:nk])
    hr_ref[0, 0], hi_ref[0, 0] = _ssm_chunks(u_ref, d_ref, y_ref, toep_s, wout_s, win_s, scan_chunks)


def _ssm(u, us, h0r, h0i, toep, wout, win, lev, d):
    b, l, _ = u.shape
    rows = us.shape[1]
    nb = rows // SSM_STEPS
    wspec = lambda a: pl.BlockSpec((1,) + a.shape[1:], lambda j, bi: (j,) + (0,) * (a.ndim - 1))
    st = pl.BlockSpec((1, 1, 1, STATE_COLS), lambda j, bi: (bi, j, 0, 0))
    sst = pl.BlockSpec((1, nb, STATE_COLS), lambda j, bi: (j, 0, 0))
    seq = pl.BlockSpec((1, l, LANES), lambda j, bi: (bi, 0, j))
    sseq = pl.BlockSpec((1, rows, LANES), lambda j, bi: (0, 0, j))
    return pl.pallas_call(
        _ssm_body,
        grid=(SSM_BLOCKS, b),
        in_specs=[seq, sseq, sst, sst, wspec(toep), wspec(wout), wspec(win), wspec(lev),
                  pl.BlockSpec((1, LANES), lambda j, bi: (0, j))],
        out_specs=[seq, st, st, sseq, sst, sst],
        out_shape=[jax.ShapeDtypeStruct((b, l, SSM_WIDTH), f32),
                   jax.ShapeDtypeStruct((b, SSM_BLOCKS, 1, STATE_COLS), f32),
                   jax.ShapeDtypeStruct((b, SSM_BLOCKS, 1, STATE_COLS), f32),
                   jax.ShapeDtypeStruct((1, rows, SSM_WIDTH), f32),
                   jax.ShapeDtypeStruct((SSM_BLOCKS, nb, STATE_COLS), f32),
                   jax.ShapeDtypeStruct((SSM_BLOCKS, nb, STATE_COLS), f32)],
        scratch_shapes=[pltpu.VMEM((SSM_STEPS // 2, 2 * LANES, 2 * LANES), bf16),
                        pltpu.VMEM((SSM_STEPS * LANES, 2 * STATE_COLS), bf16),
                        pltpu.VMEM((2 * STATE_COLS, SSM_STEPS * LANES), bf16)],
        compiler_params=_cparams(2),
        name="ssm",
    )(u, us, h0r, h0i, toep, wout, win, lev, d)


ROUTER_COLS = LANES


def _mix_body(ap_ref, as_ref, yp_ref, ys_ref, xp_ref, xs_ref, wglu_ref, bglu_ref, gs_ref, wout_ref, gf_ref, wr_ref,
              br_ref, tri_ref, upper_ref, x1_ref, xl_ref, wts_ref, lpos_ref, n_ref, loff_ref, *, prompt_tiles):
    is_prompt = pl.program_id(0) < prompt_tiles
    y = jnp.where(is_prompt, yp_ref[...], ys_ref[...])
    y = 0.5 * y * (1.0 + jnp.tanh(math.sqrt(2.0 / math.pi) * (y + 0.044715 * (y * y * y))))
    y = y * jax.nn.sigmoid(_bdot(y, wglu_ref[...]) + bglu_ref[...])
    attn = jnp.where(is_prompt, ap_ref[...].astype(f32), as_ref[...].astype(f32)).astype(bf16)
    cat = jnp.concatenate([attn, _rms(y, gs_ref[...]).astype(bf16)], axis=1)
    x1 = jnp.where(is_prompt, xp_ref[...], xs_ref[...]) + jnp.dot(cat, wout_ref[...], preferred_element_type=f32)
    x1_ref[...] = x1
    hf = _rms(x1, gf_ref[...])

    logits = _bdot(hf, wr_ref[...]) + br_ref[...]
    le = logits[:, :N_EXPERTS]
    lg = logits[:, N_EXPERTS:N_EXPERTS + N_EXPERT_GROUPS]
    tm = le.shape[0]
    gmax = jnp.max(lg, axis=-1, keepdims=True)
    gi = lax.broadcasted_iota(i32, (tm, N_EXPERT_GROUPS), 1).astype(f32)
    gsel = jnp.min(jnp.where(lg == gmax, gi, float(N_EXPERT_GROUPS)), axis=-1, keepdims=True)
    pg = 1.0 / jnp.sum(jnp.exp(lg - gmax), axis=-1, keepdims=True)
    ei_int = lax.broadcasted_iota(i32, (tm, N_EXPERTS), 1)
    ei = ei_int.astype(f32)
    egroup = jnp.right_shift(ei_int, int(math.log2(EXPERTS_PER_GROUP))).astype(f32)
    lm = jnp.where(egroup == gsel, le, NEG)
    v1 = jnp.max(lm, axis=-1, keepdims=True)
    i1 = jnp.min(jnp.where(lm == v1, ei, float(N_EXPERTS)), axis=-1, keepdims=True)
    lm2 = jnp.where(ei == i1, NEG, lm)
    v2 = jnp.max(lm2, axis=-1, keepdims=True)
    i2 = jnp.min(jnp.where(lm2 == v2, ei, float(N_EXPERTS)), axis=-1, keepdims=True)
    ex = jnp.exp(v2 - v1)
    wts_ref[...] = jnp.concatenate([pg / (1.0 + ex), pg * ex / (1.0 + ex)], axis=1)

    oh1 = (ei == i1).astype(f32)
    oh2 = (ei == i2).astype(f32)
    hf_bf = hf.astype(bf16)
    sorted_row = lax.broadcasted_iota(i32, (ROW_TILE, SLOTS * ROW_TILE), 1).astype(f32)
    both = (oh1 + oh2).astype(bf16)
    before = jnp.dot(tri_ref[...], both, preferred_element_type=f32)
    count = jnp.sum(oh1 + oh2, axis=0, keepdims=True)
    lower = jnp.sum(jnp.dot(both, upper_ref[...], preferred_element_type=f32), axis=0, keepdims=True)
    lp1 = jnp.sum(oh1 * (before + lower), axis=-1, keepdims=True)
    lp2 = jnp.sum(oh2 * (before + lower), axis=-1, keepdims=True)
    lpos_ref[...] = jnp.concatenate([lp1, lp2], axis=1)
    n_ref[0] = count
    loff_ref[0] = lower
    pick = ((sorted_row == lp1) | (sorted_row == lp2)).astype(bf16)
    xl = lax.dot_general(pick, hf_bf, (((0,), (0,)), ((), ())), preferred_element_type=f32)
    _store_row_major(xl_ref, 0, xl)


def _mix(a_p, a_s, ys_p, ys_s, x_p, x_s, wglu, bglu, gs, wout, gf, wr, br, tri, upper):
    prompt_tiles = x_p.shape[0] // ROW_TILE
    t = x_p.shape[0] + x_s.shape[0]
    p_rows = lambda n: pl.BlockSpec((ROW_TILE, n), lambda i: (jnp.minimum(i, prompt_tiles - 1), 0))
    s_rows = lambda n: pl.BlockSpec((ROW_TILE, n), lambda i: (jnp.maximum(i - prompt_tiles, 0), 0))
    row = lambda n: pl.BlockSpec((ROW_TILE, n), lambda i: (i, 0))
    full = lambda arr: pl.BlockSpec(arr.shape, lambda i: (0,) * arr.ndim)
    per_tile = pl.BlockSpec((1, 1, N_EXPERTS), lambda i: (i, 0, 0))
    return pl.pallas_call(
        functools.partial(_mix_body, prompt_tiles=prompt_tiles),
        grid=(t // ROW_TILE,),
        in_specs=[p_rows(ATTN_WIDTH), s_rows(ATTN_WIDTH), p_rows(SSM_WIDTH), s_rows(SSM_WIDTH), p_rows(D_MODEL),
                  s_rows(D_MODEL), full(wglu), full(bglu), full(gs), full(wout), full(gf), full(wr), full(br),
                  full(tri), full(upper)],
        out_specs=[row(D_MODEL), pl.BlockSpec((SLOTS * ROW_TILE * PIECES, LANES), lambda i: (i, 0)), row(2), row(2),
                   per_tile, per_tile],
        out_shape=[jax.ShapeDtypeStruct((t, D_MODEL), f32),
                   jax.ShapeDtypeStruct((SLOTS * t * PIECES, LANES), f32),
                   jax.ShapeDtypeStruct((t, 2), f32), jax.ShapeDtypeStruct((t, 2), f32),
                   jax.ShapeDtypeStruct((t // ROW_TILE, 1, N_EXPERTS), f32),
                   jax.ShapeDtypeStruct((t // ROW_TILE, 1, N_EXPERTS), f32)],
        compiler_params=_cparams(),
        name="mix",
    )(a_p, a_s, ys_p, ys_s, x_p, x_s, wglu, bglu, gs, wout, gf, wr, br, tri, upper)


def _store_row_major(ref, first_row, x):
    for c in range(PIECES):
        ref[pl.ds(first_row * PIECES + c, x.shape[0], stride=PIECES), :] = x[:, c * LANES:(c + 1) * LANES]


def _load_row_major(ref, n_rows):
    return jnp.concatenate([ref[pl.ds(c, n_rows, stride=PIECES), :] for c in range(PIECES)], axis=1)


def _copy_rows(src, s_row, dst, d_row, n_rows, sem):
    return pltpu.make_async_copy(src.at[pl.ds(pl.multiple_of(s_row * PIECES, PIECES), n_rows * PIECES), :],
                                 dst.at[pl.ds(pl.multiple_of(d_row * PIECES, PIECES), n_rows * PIECES), :], sem)


def _for_each_piece(n, fn):
    off = 0
    for piece in RUN_PIECES:
        @pl.when((n & piece) != 0)
        def _(off=off, piece=piece):
            fn(off, piece)
        off = off + (n & piece)


def _copy_run(n, fn):
    long_part = n & ~(2 * SHORT_RUN - 1)

    @pl.when(long_part != 0)
    def _():
        _for_each_piece(long_part, fn)
    off = long_part
    for piece in RUN_PIECES:
        if piece <= SHORT_RUN:
            @pl.when((n & piece) != 0)
            def _(off=off, piece=piece):
                fn(off, piece)
            off = off + (n & piece)


def _experts_body(te_ref, tpos_ref, tvalid_ref, tlo_ref, thi_ref, n_ref, cum_ref, loff_ref,
                  xl_hbm, wg_ref, wu_ref, wd_ref, o_ref, xbuf, wg_s, wu_s, wd_s, sem):
    i = pl.program_id(0)
    last = pl.num_programs(0) - 1

    def fetch(t):
        slot = t % 2
        e, lo, valid = te_ref[t], tpos_ref[t], tvalid_ref[t]

        @pl.when(valid < ROW_TILE)
        def _():
            xbuf[slot] = jnp.zeros((ROW_TILE * PIECES, LANES), f32)

        def run(tau, c):
            k = tau * N_EXPERTS + e
            s, n = cum_ref[k], n_ref[k]
            a = jnp.maximum(s, lo)
            length = jnp.maximum(jnp.minimum(s + n, lo + ROW_TILE) - a, 0)
            local = loff_ref[k] + (a - s)

            _copy_run(length, lambda off, piece: _copy_rows(
                xl_hbm, tau * (SLOTS * ROW_TILE) + local + off, xbuf.at[slot], a - lo + off, piece,
                sem.at[slot]).start())
            return c
        lax.fori_loop(tlo_ref[t], thi_ref[t] + 1, run, 0)

    @pl.when((i == 0) & (tvalid_ref[0] > 0))
    def _():
        fetch(0)

    nxt = jnp.minimum(i + 1, last)

    @pl.when((i < last) & (tvalid_ref[nxt] > 0))
    def _():
        fetch(nxt)

    @pl.when((i == 0) | (te_ref[i] != te_ref[jnp.maximum(i - 1, 0)]))
    def _():
        wg_s[...] = wg_ref[0].astype(bf16)
        wu_s[...] = wu_ref[0].astype(bf16)
        wd_s[...] = wd_ref[0].astype(bf16)

    valid = tvalid_ref[i]

    @pl.when(valid > 0)
    def _():
        slot = i % 2
        _for_each_piece(valid, lambda off, piece: _copy_rows(
            xl_hbm, 0, xbuf.at[slot], 0, piece, sem.at[slot]).wait())
        x = _load_row_major(xbuf.at[slot], ROW_TILE).astype(bf16)
        hg = jnp.dot(x, wg_s[...], preferred_element_type=f32)
        hu = jnp.dot(x, wu_s[...], preferred_element_type=f32)
        y = jnp.dot((hg * jax.nn.sigmoid(hg) * hu).astype(bf16), wd_s[...], preferred_element_type=f32)
        _store_row_major(o_ref, 0, y)

    @pl.when(valid == 0)
    def _():
        o_ref[...] = jnp.zeros_like(o_ref)


def _experts(tables, xl, wg, wu, wd, tiles):
    wspec = lambda a: pl.BlockSpec((1,) + a.shape[1:], lambda i, te, *_: (te[i], 0, 0))
    return pl.pallas_call(
        _experts_body,
        grid_spec=pltpu.PrefetchScalarGridSpec(
            num_scalar_prefetch=len(tables),
            grid=(tiles,),
            in_specs=[pl.BlockSpec(memory_space=pl.ANY), wspec(wg), wspec(wu), wspec(wd)],
            out_specs=pl.BlockSpec((ROW_TILE * PIECES, LANES), lambda i, *_: (i, 0)),
            scratch_shapes=[pltpu.VMEM((2, ROW_TILE * PIECES, LANES), f32),
                            pltpu.VMEM(wg.shape[1:], bf16), pltpu.VMEM(wu.shape[1:], bf16), pltpu.VMEM(wd.shape[1:], bf16),
                            pltpu.SemaphoreType.DMA((2,))]),
        out_shape=jax.ShapeDtypeStruct((tiles * ROW_TILE * PIECES, LANES), f32),
        compiler_params=_cparams(),
        name="moe_experts",
    )(*tables, xl, wg, wu, wd)


def _combine_body(n_ref, gpos_ref, loff_ref, ys_hbm, x_ref, w_ref, l_ref, g_ref, op_ref, os_ref, ybuf, sem,
                  *, prompt_tiles):
    i = pl.program_id(0)
    last = pl.num_programs(0) - 1
    tile_rows = SLOTS * ROW_TILE

    def fetch(t):
        slot = t % 2

        def run(e, c):
            k = t * N_EXPERTS + e
            _copy_run(n_ref[k], lambda off, piece: _copy_rows(
                ys_hbm, gpos_ref[k] + off, ybuf.at[slot], loff_ref[k] + off, piece, sem.at[slot]).start())
            return c
        lax.fori_loop(0, N_EXPERTS, run, 0)

    @pl.when(i == 0)
    def _():
        fetch(0)

    @pl.when(i < last)
    def _():
        fetch(jnp.minimum(i + 1, last))

    slot = i % 2
    _copy_rows(ys_hbm, 0, ybuf.at[slot], 0, tile_rows, sem.at[slot]).wait()
    yl = _load_row_major(ybuf.at[slot], tile_rows).astype(bf16)
    sorted_row = lax.broadcasted_iota(i32, (ROW_TILE, tile_rows), 1).astype(f32)

    w, lp = w_ref[...], l_ref[...]
    y1 = jnp.dot((sorted_row == lp[:, 0:1]).astype(bf16), yl, preferred_element_type=f32)
    y2 = jnp.dot((sorted_row == lp[:, 1:2]).astype(bf16), yl, preferred_element_type=f32)
    out = _rms(x_ref[...] + (w[:, 0:1] * y1 + w[:, 1:2] * y2), g_ref[...])

    @pl.when(i < prompt_tiles)
    def _():
        op_ref[...] = out

    @pl.when(i >= prompt_tiles)
    def _():
        os_ref[...] = out


def _combine(tables, ys, x1, wts, lpos, g, prompt_rows):
    prompt_tiles = prompt_rows // ROW_TILE
    tiles = x1.shape[0] // ROW_TILE
    row = lambda n: pl.BlockSpec((ROW_TILE, n), lambda i, *_: (i, 0))
    p_rows = lambda n: pl.BlockSpec((ROW_TILE, n), lambda i, *_: (jnp.minimum(i, prompt_tiles - 1), 0))
    s_rows = lambda n: pl.BlockSpec((ROW_TILE, n), lambda i, *_: (jnp.maximum(i - prompt_tiles, 0), 0))
    return pl.pallas_call(
        functools.partial(_combine_body, prompt_tiles=prompt_tiles),
        grid_spec=pltpu.PrefetchScalarGridSpec(
            num_scalar_prefetch=len(tables),
            grid=(tiles,),
            in_specs=[pl.BlockSpec(memory_space=pl.ANY), row(D_MODEL), row(2), row(2),
                      pl.BlockSpec((1, D_MODEL), lambda i, *_: (0, 0))],
            out_specs=[p_rows(D_MODEL), s_rows(D_MODEL)],
            scratch_shapes=[pltpu.VMEM((2, SLOTS * ROW_TILE * PIECES, LANES), f32), pltpu.SemaphoreType.DMA((2,))]),
        out_shape=[jax.ShapeDtypeStruct((prompt_rows, D_MODEL), f32),
                   jax.ShapeDtypeStruct((x1.shape[0] - prompt_rows, D_MODEL), f32)],
        compiler_params=_cparams(),
        name="moe_combine",
    )(*tables, ys, x1, wts, lpos, g)


def _moe_tables(n, loff, tiles):
    cum = jnp.cumsum(n, axis=0) - n
    counts = jnp.sum(n, axis=0)
    padded = (counts + ROW_TILE - 1) // ROW_TILE * ROW_TILE
    ends = jnp.cumsum(padded)
    starts = ends - padded
    first = jnp.arange(tiles, dtype=i32) * ROW_TILE
    expert = jnp.minimum(jnp.sum((first[:, None] >= ends[None, :]).astype(i32), axis=1), N_EXPERTS - 1)
    sel = expert[:, None] == jnp.arange(N_EXPERTS)[None, :]
    pick = lambda v: jnp.sum(jnp.where(sel, v[None, :], 0), axis=1)
    pos = first - pick(starts)
    valid = jnp.where(first < ends[-1], jnp.clip(pick(counts) - pos, 0, ROW_TILE), 0)
    cum_t, n_t = cum.T[expert], n.T[expert]
    touches = (cum_t + n_t > pos[:, None]) & (cum_t < (pos + ROW_TILE)[:, None]) & (n_t > 0)
    tau = jnp.arange(n.shape[0], dtype=i32)[None, :]
    lo = jnp.min(jnp.where(touches, tau, n.shape[0]), axis=1)
    hi = jnp.max(jnp.where(touches, tau, -1), axis=1)
    as_i32 = lambda v: v.astype(i32)
    flat = lambda v: v.reshape(-1).astype(i32)
    expert_tables = tuple(map(as_i32, (expert, pos, valid, lo, hi))) + (flat(n), flat(cum), flat(loff))
    combine_tables = (flat(n), flat(starts[None, :] + cum), flat(loff))
    return expert_tables, combine_tables


def kernel(x_prompt, x_sample, cache_k, cache_v, state_ssm_re, state_ssm_im, g_norm_mix, w_in, attn_sinks, ssm_a_re,
           ssm_a_im, ssm_log_dt, ssm_b_re, ssm_b_im, ssm_c_re, ssm_c_im, ssm_d, w_glu, b_glu, g_attn_out, g_ssm_out,
           w_out, g_norm_ffn, w_router_group, b_router_group, w_router_expert, b_router_expert, w_exp_gate, w_exp_up,
           w_exp_down, g_final):
    bp, lp, _ = x_prompt.shape
    bs, ls, _ = x_sample.shape
    depth = w_in.shape[0]
    assert depth == 1 and ls == SSM_STEPS and lp % ATTN_TILE == 0 and (bs * ls) % ROW_TILE == 0
    tp, ts = bp * lp, bs * ls
    wc = cache_k.shape[2]
    row2 = lambda v: v.reshape(1, -1)

    xp = x_prompt.reshape(tp, D_MODEL)
    xs = x_sample.reshape(ts, D_MODEL)
    w_in_bf = w_in[0].astype(bf16)
    qp, kp, vp, up = _proj(xp, row2(g_norm_mix[0]), w_in_bf, 512)
    qs, kq, vq, us = _proj(xs, row2(g_norm_mix[0]), w_in_bf, ts)

    sinks = attn_sinks[0]
    g_att = row2(g_attn_out[0])
    ap = _attn_prompt(qp.reshape(bp, lp, -1), kp.reshape(bp, lp, -1), vp.reshape(bp, lp, -1), sinks, g_att)
    a_s, k_roll, v_roll = _attn_sample(qs.reshape(bs, ls, -1), kq.reshape(bs, ls, -1), vq.reshape(bs, ls, -1),
                                       cache_k[0].reshape(bs, wc, KV_WIDTH), cache_v[0].reshape(bs, wc, KV_WIDTH),
                                       sinks, g_att)

    toep, wso, wsi, lev = _ssm_weights(ssm_a_re[0], ssm_a_im[0], ssm_log_dt[0], ssm_b_re[0], ssm_b_im[0],
                                       ssm_c_re[0], ssm_c_im[0])
    d_row = row2(ssm_d[0])
    to_blocks = lambda h: h.reshape(bs, SSM_BLOCKS, STATE_COLS).transpose(1, 0, 2)
    from_blocks = lambda h: h.transpose(1, 0, 2).reshape(bs, SSM_GROUPS, SSM_STATE)
    yp, hrp, hip, ysm, hrs, his = _ssm(up.reshape(bp, lp, -1), us.reshape(1, ts, -1), to_blocks(state_ssm_re[0]),
                                       to_blocks(state_ssm_im[0]), toep, wso, wsi, lev, d_row)

    wr = jnp.zeros((D_MODEL, ROUTER_COLS), f32)
    wr = wr.at[:, :N_EXPERTS].set(w_router_expert[0]).at[:, N_EXPERTS:N_EXPERTS + N_EXPERT_GROUPS].set(w_router_group[0])
    br = jnp.zeros((1, ROUTER_COLS), f32)
    br = br.at[0, :N_EXPERTS].set(b_router_expert[0]).at[0, N_EXPERTS:N_EXPERTS + N_EXPERT_GROUPS].set(b_router_group[0])
    tri = jnp.tril(jnp.ones((ROW_TILE, ROW_TILE), bf16), -1)
    upper = jnp.triu(jnp.ones((N_EXPERTS, N_EXPERTS), bf16), 1)
    mix_w = (w_glu[0].astype(bf16), row2(b_glu[0]), row2(g_ssm_out[0]), w_out[0].astype(bf16), row2(g_norm_ffn[0]),
             wr.astype(bf16), br, tri, upper)
    x1, xl, wts, lpos, n_rows, n_off = _mix(ap.reshape(tp, -1), a_s.reshape(ts, -1), yp.reshape(tp, -1),
                                            ysm.reshape(ts, -1), xp, xs, *mix_w)

    per_tile = lambda v: v.reshape(-1, N_EXPERTS).astype(i32)
    tiles = (SLOTS * (tp + ts)) // ROW_TILE + N_EXPERTS
    expert_tables, combine_tables = _moe_tables(per_tile(n_rows), per_tile(n_off), tiles)
    expert_out = _experts(expert_tables, xl, w_exp_gate[0], w_exp_up[0], w_exp_down[0], tiles)
    y_p, y_s = _combine(combine_tables, expert_out, x1, wts, lpos, row2(g_final), tp)

    kvshape = lambda a, b: a.reshape(1, b, -1, N_KV_HEADS, HEAD_DIM)
    block_state = lambda h: h.reshape(bp, SSM_GROUPS, SSM_STATE)[None]
    wcp = min(WINDOW, lp)
    return (y_p.reshape(bp, lp, D_MODEL), y_s.reshape(bs, ls, D_MODEL),
            kvshape(kp.reshape(bp, lp, -1)[:, lp - wcp:], bp), kvshape(vp.reshape(bp, lp, -1)[:, lp - wcp:], bp),
            block_state(hrp), block_state(hip),
            kvshape(k_roll, bs), kvshape(v_roll, bs),
            from_blocks(hrs)[None], from_blocks(his)[None])
```

```python
import functools
import math

import jax
import jax.numpy as jnp
from jax import lax
from jax.experimental import pallas as pl
from jax.experimental.pallas import tpu as pltpu

f32, bf16, i32 = jnp.float32, jnp.bfloat16, jnp.int32

D_MODEL = 1024
CHUNK = 64
N_BACK = 2
WINDOW = 128
ATTN_WIDTH = 512
HEAD_DIM = 64
N_KV_HEADS = 2
Q_PER_KV = 4
KV_WIDTH = 128
SSM_WIDTH = 512
SSM_GROUP = 16
SSM_GROUPS = 32
SSM_STATE = 64
PROJ_WIDTH = 1280
N_EXPERT_GROUPS = 4
EXPERTS_PER_GROUP = 8
N_EXPERTS = 32
D_EXPERT = 512
EPS = 1e-6
NEG = -1e30

LANES = 128
SSM_STEPS = 16
SSM_BLOCKS = SSM_WIDTH // LANES
GROUPS_PER_BLOCK = LANES // SSM_GROUP
STATE_COLS = GROUPS_PER_BLOCK * SSM_STATE
ROW_TILE = 256
SLOTS = 2
PIECES = D_MODEL // LANES
RUN_PIECES = tuple(1 << b for b in reversed(range(int(math.log2(ROW_TILE)) + 1)))
SHORT_RUN = 32
MIX_TILES = 2
UNROLLED_RUNS = 20
VMEM_LIMIT = 56 * 1024 * 1024


def _cparams(n_axes=1, limit=VMEM_LIMIT):
    return pltpu.CompilerParams(dimension_semantics=("arbitrary",) * n_axes, vmem_limit_bytes=limit)


def _rms(x, g):
    return x * lax.rsqrt(jnp.mean(x * x, axis=-1, keepdims=True) + EPS) * g


def _bdot(a, b):
    return jnp.dot(a.astype(bf16), b.astype(bf16), preferred_element_type=f32)


def _proj_body(x_ref, g_ref, w_ref, q_ref, k_ref, v_ref, u_ref):
    h = _rms(x_ref[...], g_ref[...])
    z = _bdot(h, w_ref[...])
    q_ref[...] = z[:, :ATTN_WIDTH] * (HEAD_DIM ** -0.5)
    k_ref[...] = z[:, ATTN_WIDTH:ATTN_WIDTH + KV_WIDTH]
    v_ref[...] = z[:, ATTN_WIDTH + KV_WIDTH:ATTN_WIDTH + 2 * KV_WIDTH]
    u_ref[...] = z[:, ATTN_WIDTH + 2 * KV_WIDTH:]


def _proj(x2d, g, w_bf, tm):
    t = x2d.shape[0]
    row = lambda n: pl.BlockSpec((tm, n), lambda i: (i, 0))
    full = lambda a: pl.BlockSpec(a.shape, lambda i: (0,) * a.ndim)
    return pl.pallas_call(
        _proj_body,
        grid=(t // tm,),
        in_specs=[row(D_MODEL), full(g), full(w_bf)],
        out_specs=[row(ATTN_WIDTH), row(KV_WIDTH), row(KV_WIDTH), row(SSM_WIDTH)],
        out_shape=[jax.ShapeDtypeStruct((t, n), f32) for n in (ATTN_WIDTH, KV_WIDTH, KV_WIDTH, SSM_WIDTH)],
        compiler_params=_cparams(),
        name="proj",
    )(x2d, g, w_bf)


def _sink_column(sink_ref, kv, rows_per_head):
    r = lax.broadcasted_iota(i32, (Q_PER_KV * rows_per_head, 1), 0)
    col = jnp.full((Q_PER_KV * rows_per_head, 1), sink_ref[kv * Q_PER_KV], f32)
    for j in range(1, Q_PER_KV):
        col = jnp.where(r >= j * rows_per_head, sink_ref[kv * Q_PER_KV + j], col)
    return col


def _attend(qs, kc, vc, sink_col, valid):
    s = lax.dot_general(qs.astype(bf16), kc.astype(bf16), (((1,), (1,)), ((), ())), preferred_element_type=f32)
    if valid is not None:
        s = jnp.where(valid, s, NEG)
    m = jnp.maximum(jnp.max(s, axis=-1, keepdims=True), sink_col)
    p = jnp.exp(s - m)
    denom = jnp.sum(p, axis=-1, keepdims=True) + jnp.exp(sink_col - m)
    return _bdot(p, vc) / denom


def _heads_attend(q, k, v, sink_ref, valid):
    rows = q.shape[0]
    pieces = []
    for kv in range(N_KV_HEADS):
        qs = jnp.concatenate(
            [q[:, (kv * Q_PER_KV + j) * HEAD_DIM:(kv * Q_PER_KV + j + 1) * HEAD_DIM] for j in range(Q_PER_KV)], axis=0)
        o = _attend(qs, k[:, kv * HEAD_DIM:(kv + 1) * HEAD_DIM], v[:, kv * HEAD_DIM:(kv + 1) * HEAD_DIM],
                    _sink_column(sink_ref, kv, rows), valid)
        pieces += [o[j * rows:(j + 1) * rows] for j in range(Q_PER_KV)]
    return jnp.concatenate(pieces, axis=1)


ATTN_TILE = 256
CHUNKS_PER_TILE = ATTN_TILE // CHUNK
KEY_SPAN = (N_BACK + 1) * CHUNK


def _attn_prompt_body(sink_ref, q_ref, kp_ref, kc_ref, vp_ref, vc_ref, g_ref, o_ref):
    i = pl.program_id(1)
    kwin = jnp.concatenate([kp_ref[0], kc_ref[0]], axis=0)
    vwin = jnp.concatenate([vp_ref[0], vc_ref[0]], axis=0)
    key_chunk = lax.broadcasted_iota(i32, (1, KEY_SPAN), 1) // CHUNK
    for c in range(CHUNKS_PER_TILE):
        valid = (i * CHUNKS_PER_TILE + c - N_BACK + key_chunk) >= 0
        o = _heads_attend(q_ref[0, c * CHUNK:(c + 1) * CHUNK, :], kwin[c * CHUNK:c * CHUNK + KEY_SPAN],
                          vwin[c * CHUNK:c * CHUNK + KEY_SPAN], sink_ref, valid)
        o_ref[0, c * CHUNK:(c + 1) * CHUNK, :] = _rms(o, g_ref[...]).astype(bf16)


def _attn_prompt(q, k, v, sinks, g):
    b, l, _ = q.shape
    back = N_BACK * CHUNK
    per = ATTN_TILE // back
    prev = pl.BlockSpec((1, back, KV_WIDTH), lambda bi, i: (bi, jnp.maximum(i * per - 1, 0), 0))
    cur = pl.BlockSpec((1, ATTN_TILE, KV_WIDTH), lambda bi, i: (bi, i, 0))
    return pl.pallas_call(
        _attn_prompt_body,
        grid=(b, l // ATTN_TILE),
        in_specs=[pl.BlockSpec(memory_space=pltpu.SMEM),
                  pl.BlockSpec((1, ATTN_TILE, ATTN_WIDTH), lambda bi, i: (bi, i, 0)),
                  prev, cur, prev, cur,
                  pl.BlockSpec((1, ATTN_WIDTH), lambda bi, i: (0, 0))],
        out_specs=pl.BlockSpec((1, ATTN_TILE, ATTN_WIDTH), lambda bi, i: (bi, i, 0)),
        out_shape=jax.ShapeDtypeStruct((b, l, ATTN_WIDTH), bf16),
        compiler_params=_cparams(2),
        name="attn_prompt",
    )(sinks, q, k, k, v, v, g)


def _attn_sample_body(sink_ref, q_ref, kn_ref, vn_ref, ck_ref, cv_ref, g_ref, o_ref, nk_ref, nv_ref):
    kall = jnp.concatenate([ck_ref[0], kn_ref[0]], axis=0)
    vall = jnp.concatenate([cv_ref[0], vn_ref[0]], axis=0)
    o = _heads_attend(q_ref[0], kall, vall, sink_ref, None)
    o_ref[0] = _rms(o, g_ref[...]).astype(bf16)
    n_new = kn_ref.shape[1]
    nk_ref[0] = kall[n_new:]
    nv_ref[0] = vall[n_new:]


def _attn_sample(q, k_new, v_new, cache_k, cache_v, sinks, g):
    b, l, _ = q.shape
    wc = cache_k.shape[1]
    blk = lambda r, n: pl.BlockSpec((1, r, n), lambda bi: (bi, 0, 0))
    return pl.pallas_call(
        _attn_sample_body,
        grid=(b,),
        in_specs=[pl.BlockSpec(memory_space=pltpu.SMEM), blk(l, ATTN_WIDTH), blk(l, KV_WIDTH), blk(l, KV_WIDTH),
                  blk(wc, KV_WIDTH), blk(wc, KV_WIDTH), pl.BlockSpec((1, ATTN_WIDTH), lambda bi: (0, 0))],
        out_specs=[blk(l, ATTN_WIDTH), blk(wc, KV_WIDTH), blk(wc, KV_WIDTH)],
        out_shape=[jax.ShapeDtypeStruct((b, l, ATTN_WIDTH), bf16),
                   jax.ShapeDtypeStruct((b, wc, KV_WIDTH), f32), jax.ShapeDtypeStruct((b, wc, KV_WIDTH), f32)],
        compiler_params=_cparams(),
        name="attn_sample",
    )(sinks, q, k_new, v_new, cache_k, cache_v, g)


def _ssm_weights(a_re, a_im, log_dt, b_re, b_im, c_re, c_im):
    hp = lax.Precision.HIGHEST
    dt = jnp.exp(log_dt)[:, None]
    mag = jnp.exp(a_re * dt)
    ar, ai = mag * jnp.cos(a_im * dt), mag * jnp.sin(a_im * dt)
    den = a_re * a_re + a_im * a_im
    nr, ni = ar - 1.0, ai
    fr = ((nr * a_re + ni * a_im) / den)[..., None]
    fi = ((ni * a_re - nr * a_im) / den)[..., None]
    bbr, bbi = fr * b_re - fi * b_im, fr * b_im + fi * b_re
    pr, pi = [jnp.ones_like(ar)], [jnp.zeros_like(ar)]
    for _ in range(SSM_STEPS):
        pr, pi = pr + [pr[-1] * ar - pi[-1] * ai], pi + [pr[-1] * ai + pi[-1] * ar]
    rev_r, rev_i = jnp.stack(pr[SSM_STEPS - 1::-1]), jnp.stack(pi[SSM_STEPS - 1::-1])
    pr, pi = jnp.stack(pr), jnp.stack(pi)
    c_re_t, c_im_t = c_re.transpose(0, 2, 1), c_im.transpose(0, 2, 1)
    pr_l, pi_l = pr[:SSM_STEPS, :, :, None], pi[:SSM_STEPS, :, :, None]
    cpr = c_re_t[None] * pr_l - c_im_t[None] * pi_l
    cpi = c_re_t[None] * pi_l + c_im_t[None] * pr_l
    bbr_c, bbi_c = bbr.transpose(0, 2, 1)[None, :, :, :, None], bbi.transpose(0, 2, 1)[None, :, :, :, None]
    kl = jnp.sum(cpr[:, :, None] * bbr_c - cpi[:, :, None] * bbi_c, axis=3)
    kl = jnp.concatenate([jnp.zeros_like(kl[:1]), kl], axis=0)
    nd = SSM_STEPS // 2
    lag = 2 * jnp.arange(nd)[:, None, None] + jnp.arange(2)[None, None, :] - jnp.arange(2)[None, :, None]
    ksel = kl[lag + 1]
    ksel = ksel.reshape(nd, 2, 2, SSM_BLOCKS, GROUPS_PER_BLOCK, SSM_GROUP, SSM_GROUP)
    toep = ksel.transpose(3, 0, 1, 4, 5, 2, 6).reshape(SSM_BLOCKS, nd, 2 * LANES, 2 * SSM_GROUP)
    bbr_t, bbi_t = bbr.transpose(0, 2, 1), bbi.transpose(0, 2, 1)
    wo_r = rev_r[:, :, None, :] * bbr_t[None] - rev_i[:, :, None, :] * bbi_t[None]
    wo_i = rev_r[:, :, None, :] * bbi_t[None] + rev_i[:, :, None, :] * bbr_t[None]
    wo = jnp.stack([wo_r, wo_i]).reshape(2, SSM_STEPS, SSM_BLOCKS, GROUPS_PER_BLOCK, SSM_GROUP, SSM_STATE)
    wout = wo.transpose(2, 1, 3, 4, 0, 5).reshape(SSM_BLOCKS, SSM_STEPS * LANES, 2 * SSM_STATE)
    gr = c_re[None] * pr[1:, :, None, :] - c_im[None] * pi[1:, :, None, :]
    gi = c_re[None] * pi[1:, :, None, :] + c_im[None] * pr[1:, :, None, :]
    wi = jnp.stack([gr, -gi]).reshape(2, SSM_STEPS, SSM_BLOCKS, GROUPS_PER_BLOCK, SSM_GROUP, SSM_STATE)
    win = wi.transpose(2, 0, 3, 5, 1, 4).reshape(SSM_BLOCKS, 2 * STATE_COLS, SSM_STEPS * SSM_GROUP)
    lr, li = [pr[SSM_STEPS]], [pi[SSM_STEPS]]
    for _ in range(7):
        lr, li = lr + [lr[-1] * lr[-1] - li[-1] * li[-1]], li + [2.0 * lr[-1] * li[-1]]
    lev = jnp.stack([jnp.stack(lr), jnp.stack(li)], axis=1)
    lev = lev.reshape(8, 2, SSM_BLOCKS, STATE_COLS).transpose(2, 0, 1, 3)
    return toep.astype(bf16), wout.astype(bf16), win.astype(bf16), lev


def _ssm_chunk_rows(u_ref, nk):
    xs = [u_ref[0, pl.ds(s, nk, stride=SSM_STEPS), :] for s in range(SSM_STEPS)]
    pairs = [jnp.concatenate([xs[2 * p], xs[2 * p + 1]], axis=1).astype(bf16) for p in range(SSM_STEPS // 2)]
    return xs, pairs


def _ssm_intra(pairs, toep_ref, nk):
    nd = len(pairs)
    y = [None] * nd
    for d in range(nd):
        lhs = jnp.concatenate(pairs[:nd - d], axis=0) if nd - d > 1 else pairs[0]
        r = jnp.dot(lhs, toep_ref[d], preferred_element_type=f32)
        for p in range(nd - d):
            blk = r[p * nk:(p + 1) * nk]
            y[p + d] = blk if y[p + d] is None else y[p + d] + blk
    return y


def _shift_rows(x, sh):
    rows = lax.broadcasted_iota(i32, (x.shape[0], 1), 0)
    return jnp.where(rows >= sh, pltpu.roll(x, sh, axis=0), 0.0)


def _spread_matrix(k_in, sub):
    k = lax.broadcasted_iota(i32, (k_in, k_in * GROUPS_PER_BLOCK), 0)
    n = lax.broadcasted_iota(i32, (k_in, k_in * GROUPS_PER_BLOCK), 1)
    shift = int(math.log2(sub))
    same_a = jnp.right_shift(n, shift + int(math.log2(GROUPS_PER_BLOCK))) == jnp.right_shift(k, shift)
    return (same_a & ((n & (sub - 1)) == (k & (sub - 1)))).astype(bf16)


def _group_index(shape, axis, sub, offset=0):
    idx = lax.broadcasted_iota(i32, shape, axis) + offset
    return jnp.right_shift(idx, int(math.log2(sub))) & (GROUPS_PER_BLOCK - 1)


def _expand_block_diag(compact_ref, out_ref, row_sub, col_sub):
    rows, k_in = compact_ref.shape
    spread = _spread_matrix(k_in, col_sub)
    col_group = _group_index((2 * LANES, k_in * GROUPS_PER_BLOCK), 1, col_sub)
    for r0 in range(0, rows, 2 * LANES):
        full = jnp.dot(compact_ref[r0:r0 + 2 * LANES, :], spread, preferred_element_type=f32)
        row_group = _group_index(full.shape, 0, row_sub, r0)
        out_ref[r0:r0 + 2 * LANES, :] = jnp.where(row_group == col_group, full, 0.0).astype(bf16)


def _ssm_chunks(u_ref, d_ref, y_ref, toep_s, wout_s, win_s, entry_state):
    nk = u_ref.shape[1] // SSM_STEPS
    xs, pairs = _ssm_chunk_rows(u_ref, nk)
    y = _ssm_intra(pairs, toep_s, nk)
    s = jnp.dot(jnp.concatenate(pairs, axis=1), wout_s[...], preferred_element_type=f32)
    hprev, hr, hi = entry_state(s[:, :STATE_COLS], s[:, STATE_COLS:])
    y2 = _bdot(hprev, win_s[...])
    for st in range(SSM_STEPS):
        piece = (y[st // 2][:, (st % 2) * LANES:(st % 2 + 1) * LANES] + y2[:, st * LANES:(st + 1) * LANES]
                 + d_ref[...] * xs[st])
        y_ref[0, pl.ds(st, nk, stride=SSM_STEPS), :] = piece
    return hr, hi


def _ssm_body(u_ref, us_ref, h0r_ref, h0i_ref, toep_ref, wout_ref, win_ref, lev_ref, d_ref,
              y_ref, hr_ref, hi_ref, ys_ref, hrs_ref, his_ref, toep_s, wout_s, win_s):
    @pl.when(pl.program_id(1) == 0)
    def _():
        for d in range(SSM_STEPS // 2):
            _expand_block_diag(toep_ref.at[0, d], toep_s.at[d], SSM_GROUP, SSM_GROUP)
        _expand_block_diag(wout_ref.at[0], wout_s, SSM_GROUP, SSM_STATE)
        _expand_block_diag(win_ref.at[0], win_s, SSM_STATE, SSM_GROUP)

        def one_chunk(sr, si):
            h0r, h0i = h0r_ref[0], h0i_ref[0]
            ar, ai = lev_ref[0, 0, 0:1, :], lev_ref[0, 0, 1:2, :]
            return (jnp.concatenate([h0r, h0i], axis=1), sr + ar * h0r - ai * h0i, si + ar * h0i + ai * h0r)
        hrs_ref[0], his_ref[0] = _ssm_chunks(us_ref, d_ref, ys_ref, toep_s, wout_s, win_s, one_chunk)

    def scan_chunks(sr, si):
        nk = sr.shape[0]
        level = 0
        while (1 << level) < nk:
            ar, ai = lev_ref[0, level, 0:1, :], lev_ref[0, level, 1:2, :]
            tr, ti = _shift_rows(sr, 1 << level), _shift_rows(si, 1 << level)
            sr, si = sr + ar * tr - ai * ti, si + ar * ti + ai * tr
            level += 1
        return (jnp.concatenate([_shift_rows(sr, 1), _shift_rows(si, 1)], axis=1), sr[nk - 1:nk], si[nk - 1:nk])
    hr_ref[0, 0], hi_ref[0, 0] = _ssm_chunks(u_ref, d_ref, y_ref, toep_s, wout_s, win_s, scan_chunks)


def _ssm(u, us, h0r, h0i, toep, wout, win, lev, d):
    b, l, _ = u.shape
    rows = us.shape[1]
    nb = rows // SSM_STEPS
    wspec = lambda a: pl.BlockSpec((1,) + a.shape[1:], lambda j, bi: (j,) + (0,) * (a.ndim - 1))
    st = pl.BlockSpec((1, 1, 1, STATE_COLS), lambda j, bi: (bi, j, 0, 0))
    sst = pl.BlockSpec((1, nb, STATE_COLS), lambda j, bi: (j, 0, 0))
    seq = pl.BlockSpec((1, l, LANES), lambda j, bi: (bi, 0, j))
    sseq = pl.BlockSpec((1, rows, LANES), lambda j, bi: (0, 0, j))
    return pl.pallas_call(
        _ssm_body,
        grid=(SSM_BLOCKS, b),
        in_specs=[seq, sseq, sst, sst, wspec(toep), wspec(wout), wspec(win), wspec(lev),
                  pl.BlockSpec((1, LANES), lambda j, bi: (0, j))],
        out_specs=[seq, st, st, sseq, sst, sst],
        out_shape=[jax.ShapeDtypeStruct((b, l, SSM_WIDTH), f32),
                   jax.ShapeDtypeStruct((b, SSM_BLOCKS, 1, STATE_COLS), f32),
                   jax.ShapeDtypeStruct((b, SSM_BLOCKS, 1, STATE_COLS), f32),
                   jax.ShapeDtypeStruct((1, rows, SSM_WIDTH), f32),
                   jax.ShapeDtypeStruct((SSM_BLOCKS, nb, STATE_COLS), f32),
                   jax.ShapeDtypeStruct((SSM_BLOCKS, nb, STATE_COLS), f32)],
        scratch_shapes=[pltpu.VMEM((SSM_STEPS // 2, 2 * LANES, 2 * LANES), bf16),
                        pltpu.VMEM((SSM_STEPS * LANES, 2 * STATE_COLS), bf16),
                        pltpu.VMEM((2 * STATE_COLS, SSM_STEPS * LANES), bf16)],
        compiler_params=_cparams(2),
        name="ssm",
    )(u, us, h0r, h0i, toep, wout, win, lev, d)


ROUTER_COLS = LANES


def _mix_body(ap_ref, as_ref, yp_ref, ys_ref, xp_ref, xs_ref, wglu_ref, bglu_ref, gs_ref, wout_ref, gf_ref, wr_ref,
              br_ref, tri_ref, upper_ref, x1_ref, xl_ref, wts_ref, lpos_ref, n_ref, loff_ref, *, prompt_steps):
    is_prompt = pl.program_id(0) < prompt_steps
    pick_rows = lambda p_ref, s_ref: jnp.where(is_prompt, p_ref[...], jnp.concatenate([s_ref[...]] * MIX_TILES, axis=0))
    y = pick_rows(yp_ref, ys_ref)
    y = 0.5 * y * (1.0 + jnp.tanh(math.sqrt(2.0 / math.pi) * (y + 0.044715 * (y * y * y))))
    y = y * jax.nn.sigmoid(_bdot(y, wglu_ref[...]) + bglu_ref[...])
    attn = jnp.where(is_prompt, ap_ref[...].astype(f32),
                     jnp.concatenate([as_ref[...].astype(f32)] * MIX_TILES, axis=0)).astype(bf16)
    cat = jnp.concatenate([attn, _rms(y, gs_ref[...]).astype(bf16)], axis=1)
    x1 = pick_rows(xp_ref, xs_ref) + jnp.dot(cat, wout_ref[...], preferred_element_type=f32)
    x1_ref[...] = x1
    hf = _rms(x1, gf_ref[...])

    logits = _bdot(hf, wr_ref[...]) + br_ref[...]
    le = logits[:, :N_EXPERTS]
    lg = logits[:, N_EXPERTS:N_EXPERTS + N_EXPERT_GROUPS]
    tm = le.shape[0]
    gmax = jnp.max(lg, axis=-1, keepdims=True)
    gi = lax.broadcasted_iota(i32, (tm, N_EXPERT_GROUPS), 1).astype(f32)
    gsel = jnp.min(jnp.where(lg == gmax, gi, float(N_EXPERT_GROUPS)), axis=-1, keepdims=True)
    pg = 1.0 / jnp.sum(jnp.exp(lg - gmax), axis=-1, keepdims=True)
    ei_int = lax.broadcasted_iota(i32, (tm, N_EXPERTS), 1)
    ei = ei_int.astype(f32)
    egroup = jnp.right_shift(ei_int, int(math.log2(EXPERTS_PER_GROUP))).astype(f32)
    lm = jnp.where(egroup == gsel, le, NEG)
    v1 = jnp.max(lm, axis=-1, keepdims=True)
    i1 = jnp.min(jnp.where(lm == v1, ei, float(N_EXPERTS)), axis=-1, keepdims=True)
    lm2 = jnp.where(ei == i1, NEG, lm)
    v2 = jnp.max(lm2, axis=-1, keepdims=True)
    i2 = jnp.min(jnp.where(lm2 == v2, ei, float(N_EXPERTS)), axis=-1, keepdims=True)
    ex = jnp.exp(v2 - v1)
    wts_ref[...] = jnp.concatenate([pg / (1.0 + ex), pg * ex / (1.0 + ex)], axis=1)

    oh1 = (ei == i1).astype(f32)
    oh2 = (ei == i2).astype(f32)
    hf_bf = hf.astype(bf16)
    sorted_row = lax.broadcasted_iota(i32, (ROW_TILE, SLOTS * ROW_TILE), 1).astype(f32)
    for h in range(MIX_TILES):
        rows = slice(h * ROW_TILE, (h + 1) * ROW_TILE)
        both = (oh1[rows] + oh2[rows]).astype(bf16)
        before = jnp.dot(tri_ref[...], both, preferred_element_type=f32)
        count = jnp.sum(oh1[rows] + oh2[rows], axis=0, keepdims=True)
        lower = jnp.sum(jnp.dot(both, upper_ref[...], preferred_element_type=f32), axis=0, keepdims=True)
        lp1 = jnp.sum(oh1[rows] * (before + lower), axis=-1, keepdims=True)
        lp2 = jnp.sum(oh2[rows] * (before + lower), axis=-1, keepdims=True)
        lpos_ref[rows, :] = jnp.concatenate([lp1, lp2], axis=1)
        n_ref[h] = count
        loff_ref[h] = lower
        pick = ((sorted_row == lp1) | (sorted_row == lp2)).astype(bf16)
        xl = lax.dot_general(pick, hf_bf[rows], (((0,), (0,)), ((), ())), preferred_element_type=f32)
        _store_row_major(xl_ref, h * SLOTS * ROW_TILE, xl)


def _mix(a_p, a_s, ys_p, ys_s, x_p, x_s, wglu, bglu, gs, wout, gf, wr, br, tri, upper):
    tm = MIX_TILES * ROW_TILE
    assert x_p.shape[0] % tm == 0 and x_s.shape[0] == ROW_TILE
    prompt_steps = x_p.shape[0] // tm
    t = x_p.shape[0] + x_s.shape[0]
    p_rows = lambda n: pl.BlockSpec((tm, n), lambda i: (jnp.minimum(i, prompt_steps - 1), 0))
    s_rows = lambda n: pl.BlockSpec((ROW_TILE, n), lambda i: (0, 0))
    row = lambda n: pl.BlockSpec((tm, n), lambda i: (i, 0))
    full = lambda arr: pl.BlockSpec(arr.shape, lambda i: (0,) * arr.ndim)
    per_tile = pl.BlockSpec((MIX_TILES, 1, N_EXPERTS), lambda i: (i, 0, 0))
    return pl.pallas_call(
        functools.partial(_mix_body, prompt_steps=prompt_steps),
        grid=(prompt_steps + 1,),
        in_specs=[p_rows(ATTN_WIDTH), s_rows(ATTN_WIDTH), p_rows(SSM_WIDTH), s_rows(SSM_WIDTH), p_rows(D_MODEL),
                  s_rows(D_MODEL), full(wglu), full(bglu), full(gs), full(wout), full(gf), full(wr), full(br),
                  full(tri), full(upper)],
        out_specs=[row(D_MODEL), pl.BlockSpec((SLOTS * tm * PIECES, LANES), lambda i: (i, 0)), row(2), row(2),
                   per_tile, per_tile],
        out_shape=[jax.ShapeDtypeStruct((t, D_MODEL), f32),
                   jax.ShapeDtypeStruct((SLOTS * t * PIECES, LANES), f32),
                   jax.ShapeDtypeStruct((t, 2), f32), jax.ShapeDtypeStruct((t, 2), f32),
                   jax.ShapeDtypeStruct((t // ROW_TILE, 1, N_EXPERTS), f32),
                   jax.ShapeDtypeStruct((t // ROW_TILE, 1, N_EXPERTS), f32)],
        compiler_params=_cparams(),
        name="mix",
    )(a_p, a_s, ys_p, ys_s, x_p, x_s, wglu, bglu, gs, wout, gf, wr, br, tri, upper)


def _store_row_major(ref, first_row, x):
    for c in range(PIECES):
        ref[pl.ds(first_row * PIECES + c, x.shape[0], stride=PIECES), :] = x[:, c * LANES:(c + 1) * LANES]


def _load_row_major(ref, n_rows):
    return jnp.concatenate([ref[pl.ds(c, n_rows, stride=PIECES), :] for c in range(PIECES)], axis=1)


def _copy_rows(src, s_row, dst, d_row, n_rows, sem):
    return pltpu.make_async_copy(src.at[pl.ds(pl.multiple_of(s_row * PIECES, PIECES), n_rows * PIECES), :],
                                 dst.at[pl.ds(pl.multiple_of(d_row * PIECES, PIECES), n_rows * PIECES), :], sem)


def _for_each_piece(n, fn):
    off = 0
    for piece in RUN_PIECES:
        @pl.when((n & piece) != 0)
        def _(off=off, piece=piece):
            fn(off, piece)
        off = off + (n & piece)


def _copy_run(n, fn):
    long_part = n & ~(2 * SHORT_RUN - 1)

    @pl.when(long_part != 0)
    def _():
        _for_each_piece(long_part, fn)
    off = long_part
    for piece in RUN_PIECES:
        if piece <= SHORT_RUN:
            @pl.when((n & piece) != 0)
            def _(off=off, piece=piece):
                fn(off, piece)
            off = off + (n & piece)


def _experts_body(te_ref, tpos_ref, tvalid_ref, tlo_ref, thi_ref, n_ref, cum_ref, loff_ref,
                  xl_hbm, wg_ref, wu_ref, wd_ref, o_ref, xbuf_even, xbuf_odd, wg_s, wu_s, wd_s, sem):
    i = pl.program_id(0)
    last = pl.num_programs(0) - 1
    token_tiles = n_ref.shape[0] // N_EXPERTS

    def start_run(t, tau, enabled, buf, buf_sem):
        e, lo = te_ref[t], tpos_ref[t]
        k = tau * N_EXPERTS + e
        s, n = cum_ref[k], n_ref[k]
        a = jnp.maximum(s, lo)
        length = jnp.where(enabled, jnp.maximum(jnp.minimum(s + n, lo + ROW_TILE) - a, 0), 0)
        local = loff_ref[k] + (a - s)
        _for_each_piece(length, lambda off, piece: _copy_rows(
            xl_hbm, tau * (SLOTS * ROW_TILE) + local + off, buf, a - lo + off, piece, buf_sem).start())

    def start_runs_loop(t, first, stop, buf, buf_sem):
        def run(tau, c):
            start_run(t, tau, True, buf, buf_sem)
            return c
        lax.fori_loop(first, stop, run, 0)

    def clear(buf):
        buf[...] = jnp.zeros_like(buf)

    @pl.when((i == 0) & (tvalid_ref[0] > 0))
    def _():
        clear(xbuf_even)
        clear(xbuf_odd)
        start_runs_loop(0, tlo_ref[0], thi_ref[0] + 1, xbuf_even, sem.at[0])

    @pl.when((i == 0) | (te_ref[i] != te_ref[jnp.maximum(i - 1, 0)]))
    def _():
        wg_s[...] = wg_ref[0].astype(bf16)
        wu_s[...] = wu_ref[0].astype(bf16)
        wd_s[...] = wd_ref[0].astype(bf16)

    valid = tvalid_ref[i]

    def tile_step(buf, buf_sem, next_buf, next_sem):
        _for_each_piece(valid, lambda off, piece: _copy_rows(
            xl_hbm, 0, buf, 0, piece, buf_sem).wait())

        nxt = jnp.minimum(i + 1, last)
        go = (i < last) & (tvalid_ref[nxt] > 0)
        first, final = tlo_ref[nxt], thi_ref[nxt]
        for j in range(UNROLLED_RUNS):
            start_run(nxt, jnp.minimum(first + j, token_tiles - 1), go & (first + j <= final), next_buf, next_sem)

        x = _load_row_major(buf, ROW_TILE).astype(bf16)
        clear(buf)
        hg = jnp.dot(x, wg_s[...], preferred_element_type=f32)
        hu = jnp.dot(x, wu_s[...], preferred_element_type=f32)
        y = jnp.dot((hg * jax.nn.sigmoid(hg) * hu).astype(bf16), wd_s[...], preferred_element_type=f32)
        _store_row_major(o_ref, 0, y)

        @pl.when(go & (final - first >= UNROLLED_RUNS))
        def _():
            start_runs_loop(nxt, first + UNROLLED_RUNS, final + 1, next_buf, next_sem)

    pl.when((valid > 0) & (i % 2 == 0))(functools.partial(tile_step, xbuf_even, sem.at[0], xbuf_odd, sem.at[1]))
    pl.when((valid > 0) & (i % 2 == 1))(functools.partial(tile_step, xbuf_odd, sem.at[1], xbuf_even, sem.at[0]))

    @pl.when(valid == 0)
    def _():
        o_ref[...] = jnp.zeros_like(o_ref)


def _experts(tables, xl, wg, wu, wd, tiles):
    wspec = lambda a: pl.BlockSpec((1,) + a.shape[1:], lambda i, te, *_: (te[i], 0, 0))
    return pl.pallas_call(
        _experts_body,
        grid_spec=pltpu.PrefetchScalarGridSpec(
            num_scalar_prefetch=len(tables),
            grid=(tiles,),
            in_specs=[pl.BlockSpec(memory_space=pl.ANY), wspec(wg), wspec(wu), wspec(wd)],
            out_specs=pl.BlockSpec((ROW_TILE * PIECES, LANES), lambda i, *_: (i, 0)),
            scratch_shapes=[pltpu.VMEM((ROW_TILE * PIECES, LANES), f32), pltpu.VMEM((ROW_TILE * PIECES, LANES), f32),
                            pltpu.VMEM(wg.shape[1:], bf16), pltpu.VMEM(wu.shape[1:], bf16), pltpu.VMEM(wd.shape[1:], bf16),
                            pltpu.SemaphoreType.DMA((2,))]),
        out_shape=jax.ShapeDtypeStruct((tiles * ROW_TILE * PIECES, LANES), f32),
        compiler_params=_cparams(),
        name="moe_experts",
    )(*tables, xl, wg, wu, wd)


def _combine_body(n_ref, gpos_ref, loff_ref, ys_hbm, x_ref, w_ref, l_ref, g_ref, op_ref, os_ref, ybuf_even, ybuf_odd,
                  sem, *, prompt_tiles):
    i = pl.program_id(0)
    last = pl.num_programs(0) - 1
    tile_rows = SLOTS * ROW_TILE

    def start_run(t, e, enabled, buf, buf_sem):
        k = t * N_EXPERTS + e
        _for_each_piece(jnp.where(enabled, n_ref[k], 0), lambda off, piece: _copy_rows(
            ys_hbm, gpos_ref[k] + off, buf, loff_ref[k] + off, piece, buf_sem).start())

    @pl.when(i == 0)
    def _():
        def run(e, c):
            start_run(0, e, True, ybuf_even, sem.at[0])
            return c
        lax.fori_loop(0, N_EXPERTS, run, 0)

    def step(buf, buf_sem, next_buf, next_sem):
        _copy_rows(ys_hbm, 0, buf, 0, tile_rows, buf_sem).wait()
        for e in range(N_EXPERTS):
            start_run(jnp.minimum(i + 1, last), e, i < last, next_buf, next_sem)
        yl = _load_row_major(buf, tile_rows).astype(bf16)
        sorted_row = lax.broadcasted_iota(i32, (ROW_TILE, tile_rows), 1).astype(f32)
        w, lp = w_ref[...], l_ref[...]
        y1 = jnp.dot((sorted_row == lp[:, 0:1]).astype(bf16), yl, preferred_element_type=f32)
        y2 = jnp.dot((sorted_row == lp[:, 1:2]).astype(bf16), yl, preferred_element_type=f32)
        out = _rms(x_ref[...] + (w[:, 0:1] * y1 + w[:, 1:2] * y2), g_ref[...])

        @pl.when(i < prompt_tiles)
        def _():
            op_ref[...] = out

        @pl.when(i >= prompt_tiles)
        def _():
            os_ref[...] = out

    pl.when(i % 2 == 0)(functools.partial(step, ybuf_even, sem.at[0], ybuf_odd, sem.at[1]))
    pl.when(i % 2 == 1)(functools.partial(step, ybuf_odd, sem.at[1], ybuf_even, sem.at[0]))


def _combine(tables, ys, x1, wts, lpos, g, prompt_rows):
    prompt_tiles = prompt_rows // ROW_TILE
    tiles = x1.shape[0] // ROW_TILE
    row = lambda n: pl.BlockSpec((ROW_TILE, n), lambda i, *_: (i, 0))
    p_rows = lambda n: pl.BlockSpec((ROW_TILE, n), lambda i, *_: (jnp.minimum(i, prompt_tiles - 1), 0))
    s_rows = lambda n: pl.BlockSpec((ROW_TILE, n), lambda i, *_: (jnp.maximum(i - prompt_tiles, 0), 0))
    return pl.pallas_call(
        functools.partial(_combine_body, prompt_tiles=prompt_tiles),
        grid_spec=pltpu.PrefetchScalarGridSpec(
            num_scalar_prefetch=len(tables),
            grid=(tiles,),
            in_specs=[pl.BlockSpec(memory_space=pl.ANY), row(D_MODEL), row(2), row(2),
                      pl.BlockSpec((1, D_MODEL), lambda i, *_: (0, 0))],
            out_specs=[p_rows(D_MODEL), s_rows(D_MODEL)],
            scratch_shapes=[pltpu.VMEM((SLOTS * ROW_TILE * PIECES, LANES), f32),
                            pltpu.VMEM((SLOTS * ROW_TILE * PIECES, LANES), f32), pltpu.SemaphoreType.DMA((2,))]),
        out_shape=[jax.ShapeDtypeStruct((prompt_rows, D_MODEL), f32),
                   jax.ShapeDtypeStruct((x1.shape[0] - prompt_rows, D_MODEL), f32)],
        compiler_params=_cparams(),
        name="moe_combine",
    )(*tables, ys, x1, wts, lpos, g)


def _moe_tables(n, loff, tiles):
    cum = jnp.cumsum(n, axis=0) - n
    counts = jnp.sum(n, axis=0)
    padded = (counts + ROW_TILE - 1) // ROW_TILE * ROW_TILE
    ends = jnp.cumsum(padded)
    starts = ends - padded
    first = jnp.arange(tiles, dtype=i32) * ROW_TILE
    expert = jnp.minimum(jnp.sum((first[:, None] >= ends[None, :]).astype(i32), axis=1), N_EXPERTS - 1)
    sel = expert[:, None] == jnp.arange(N_EXPERTS)[None, :]
    pick = lambda v: jnp.sum(jnp.where(sel, v[None, :], 0), axis=1)
    pos = first - pick(starts)
    valid = jnp.where(first < ends[-1], jnp.clip(pick(counts) - pos, 0, ROW_TILE), 0)
    cum_t, n_t = cum.T[expert], n.T[expert]
    touches = (cum_t + n_t > pos[:, None]) & (cum_t < (pos + ROW_TILE)[:, None]) & (n_t > 0)
    tau = jnp.arange(n.shape[0], dtype=i32)[None, :]
    lo = jnp.min(jnp.where(touches, tau, n.shape[0]), axis=1)
    hi = jnp.max(jnp.where(touches, tau, -1), axis=1)
    as_i32 = lambda v: v.astype(i32)
    flat = lambda v: v.reshape(-1).astype(i32)
    expert_tables = tuple(map(as_i32, (expert, pos, valid, lo, hi))) + (flat(n), flat(cum), flat(loff))
    combine_tables = (flat(n), flat(starts[None, :] + cum), flat(loff))
    return expert_tables, combine_tables


def kernel(x_prompt, x_sample, cache_k, cache_v, state_ssm_re, state_ssm_im, g_norm_mix, w_in, attn_sinks, ssm_a_re,
           ssm_a_im, ssm_log_dt, ssm_b_re, ssm_b_im, ssm_c_re, ssm_c_im, ssm_d, w_glu, b_glu, g_attn_out, g_ssm_out,
           w_out, g_norm_ffn, w_router_group, b_router_group, w_router_expert, b_router_expert, w_exp_gate, w_exp_up,
           w_exp_down, g_final):
    bp, lp, _ = x_prompt.shape
    bs, ls, _ = x_sample.shape
    depth = w_in.shape[0]
    assert depth == 1 and ls == SSM_STEPS and lp % ATTN_TILE == 0 and (bs * ls) % ROW_TILE == 0
    tp, ts = bp * lp, bs * ls
    wc = cache_k.shape[2]
    row2 = lambda v: v.reshape(1, -1)

    xp = x_prompt.reshape(tp, D_MODEL)
    xs = x_sample.reshape(ts, D_MODEL)
    w_in_bf = w_in[0].astype(bf16)
    qp, kp, vp, up = _proj(xp, row2(g_norm_mix[0]), w_in_bf, 512)
    qs, kq, vq, us = _proj(xs, row2(g_norm_mix[0]), w_in_bf, ts)

    sinks = attn_sinks[0]
    g_att = row2(g_attn_out[0])
    ap = _attn_prompt(qp.reshape(bp, lp, -1), kp.reshape(bp, lp, -1), vp.reshape(bp, lp, -1), sinks, g_att)
    a_s, k_roll, v_roll = _attn_sample(qs.reshape(bs, ls, -1), kq.reshape(bs, ls, -1), vq.reshape(bs, ls, -1),
                                       cache_k[0].reshape(bs, wc, KV_WIDTH), cache_v[0].reshape(bs, wc, KV_WIDTH),
                                       sinks, g_att)

    toep, wso, wsi, lev = _ssm_weights(ssm_a_re[0], ssm_a_im[0], ssm_log_dt[0], ssm_b_re[0], ssm_b_im[0],
                                       ssm_c_re[0], ssm_c_im[0])
    d_row = row2(ssm_d[0])
    to_blocks = lambda h: h.reshape(bs, SSM_BLOCKS, STATE_COLS).transpose(1, 0, 2)
    from_blocks = lambda h: h.transpose(1, 0, 2).reshape(bs, SSM_GROUPS, SSM_STATE)
    yp, hrp, hip, ysm, hrs, his = _ssm(up.reshape(bp, lp, -1), us.reshape(1, ts, -1), to_blocks(state_ssm_re[0]),
                                       to_blocks(state_ssm_im[0]), toep, wso, wsi, lev, d_row)

    wr = jnp.zeros((D_MODEL, ROUTER_COLS), f32)
    wr = wr.at[:, :N_EXPERTS].set(w_router_expert[0]).at[:, N_EXPERTS:N_EXPERTS + N_EXPERT_GROUPS].set(w_router_group[0])
    br = jnp.zeros((1, ROUTER_COLS), f32)
    br = br.at[0, :N_EXPERTS].set(b_router_expert[0]).at[0, N_EXPERTS:N_EXPERTS + N_EXPERT_GROUPS].set(b_router_group[0])
    tri = jnp.tril(jnp.ones((ROW_TILE, ROW_TILE), bf16), -1)
    upper = jnp.triu(jnp.ones((N_EXPERTS, N_EXPERTS), bf16), 1)
    mix_w = (w_glu[0].astype(bf16), row2(b_glu[0]), row2(g_ssm_out[0]), w_out[0].astype(bf16), row2(g_norm_ffn[0]),
             wr.astype(bf16), br, tri, upper)
    x1, xl, wts, lpos, n_rows, n_off = _mix(ap.reshape(tp, -1), a_s.reshape(ts, -1), yp.reshape(tp, -1),
                                            ysm.reshape(ts, -1), xp, xs, *mix_w)

    per_tile = lambda v: v.reshape(-1, N_EXPERTS).astype(i32)
    tiles = (SLOTS * (tp + ts)) // ROW_TILE + N_EXPERTS
    expert_tables, combine_tables = _moe_tables(per_tile(n_rows), per_tile(n_off), tiles)
    expert_out = _experts(expert_tables, xl, w_exp_gate[0], w_exp_up[0], w_exp_down[0], tiles)
    y_p, y_s = _combine(combine_tables, expert_out, x1, wts, lpos, row2(g_final), tp)

    kvshape = lambda a, b: a.reshape(1, b, -1, N_KV_HEADS, HEAD_DIM)
    block_state = lambda h: h.reshape(bp, SSM_GROUPS, SSM_STATE)[None]
    wcp = min(WINDOW, lp)
    return (y_p.reshape(bp, lp, D_MODEL), y_s.reshape(bs, ls, D_MODEL),
            kvshape(kp.reshape(bp, lp, -1)[:, lp - wcp:], bp), kvshape(vp.reshape(bp, lp, -1)[:, lp - wcp:], bp),
            block_state(hrp), block_state(hip),
            kvshape(k_roll, bs), kvshape(v_roll, bs),
            from_blocks(hrs)[None], from_blocks(his)[None])
```

```python
import functools
import math

import jax
import jax.numpy as jnp
from jax import lax
from jax.experimental import pallas as pl
from jax.experimental.pallas import tpu as pltpu

f32, bf16, i32 = jnp.float32, jnp.bfloat16, jnp.int32

D_MODEL = 1024
CHUNK = 64
N_BACK = 2
WINDOW = 128
ATTN_WIDTH = 512
HEAD_DIM = 64
N_KV_HEADS = 2
Q_PER_KV = 4
KV_WIDTH = 128
SSM_WIDTH = 512
SSM_GROUP = 16
SSM_GROUPS = 32
SSM_STATE = 64
PROJ_WIDTH = 1280
N_EXPERT_GROUPS = 4
EXPERTS_PER_GROUP = 8
N_EXPERTS = 32
D_EXPERT = 512
EPS = 1e-6
NEG = -1e30

LANES = 128
SSM_STEPS = 16
SSM_BLOCKS = SSM_WIDTH // LANES
GROUPS_PER_BLOCK = LANES // SSM_GROUP
STATE_COLS = GROUPS_PER_BLOCK * SSM_STATE
ROW_TILE = 256
SLOTS = 2
PIECES = D_MODEL // LANES
RUN_PIECES = tuple(1 << b for b in reversed(range(int(math.log2(ROW_TILE)) + 1)))
SHORT_RUN = 32
MIX_TILES = 2
UNROLLED_RUNS = 20
EXPERT_RING = 3
COMBINE_RING = 3
VMEM_LIMIT = 56 * 1024 * 1024


def _cparams(n_axes=1, limit=VMEM_LIMIT):
    return pltpu.CompilerParams(dimension_semantics=("arbitrary",) * n_axes, vmem_limit_bytes=limit)


def _rms(x, g):
    return x * lax.rsqrt(jnp.mean(x * x, axis=-1, keepdims=True) + EPS) * g


def _bdot(a, b):
    return jnp.dot(a.astype(bf16), b.astype(bf16), preferred_element_type=f32)


def _proj_body(x_ref, g_ref, w_ref, q_ref, k_ref, v_ref, u_ref):
    h = _rms(x_ref[...], g_ref[...])
    z = _bdot(h, w_ref[...])
    q_ref[...] = z[:, :ATTN_WIDTH] * (HEAD_DIM ** -0.5)
    k_ref[...] = z[:, ATTN_WIDTH:ATTN_WIDTH + KV_WIDTH]
    v_ref[...] = z[:, ATTN_WIDTH + KV_WIDTH:ATTN_WIDTH + 2 * KV_WIDTH]
    u_ref[...] = z[:, ATTN_WIDTH + 2 * KV_WIDTH:]


def _proj(x2d, g, w_bf, tm):
    t = x2d.shape[0]
    row = lambda n: pl.BlockSpec((tm, n), lambda i: (i, 0))
    full = lambda a: pl.BlockSpec(a.shape, lambda i: (0,) * a.ndim)
    return pl.pallas_call(
        _proj_body,
        grid=(t // tm,),
        in_specs=[row(D_MODEL), full(g), full(w_bf)],
        out_specs=[row(ATTN_WIDTH), row(KV_WIDTH), row(KV_WIDTH), row(SSM_WIDTH)],
        out_shape=[jax.ShapeDtypeStruct((t, n), f32) for n in (ATTN_WIDTH, KV_WIDTH, KV_WIDTH, SSM_WIDTH)],
        compiler_params=_cparams(),
        name="proj",
    )(x2d, g, w_bf)


def _sink_column(sink_ref, kv, rows_per_head):
    r = lax.broadcasted_iota(i32, (Q_PER_KV * rows_per_head, 1), 0)
    col = jnp.full((Q_PER_KV * rows_per_head, 1), sink_ref[kv * Q_PER_KV], f32)
    for j in range(1, Q_PER_KV):
        col = jnp.where(r >= j * rows_per_head, sink_ref[kv * Q_PER_KV + j], col)
    return col


def _attend(qs, kc, vc, sink_col, valid):
    s = lax.dot_general(qs.astype(bf16), kc.astype(bf16), (((1,), (1,)), ((), ())), preferred_element_type=f32)
    if valid is not None:
        s = jnp.where(valid, s, NEG)
    m = jnp.maximum(jnp.max(s, axis=-1, keepdims=True), sink_col)
    p = jnp.exp(s - m)
    denom = jnp.sum(p, axis=-1, keepdims=True) + jnp.exp(sink_col - m)
    return _bdot(p, vc) / denom


def _heads_attend(q, k, v, sink_ref, valid):
    rows = q.shape[0]
    pieces = []
    for kv in range(N_KV_HEADS):
        qs = jnp.concatenate(
            [q[:, (kv * Q_PER_KV + j) * HEAD_DIM:(kv * Q_PER_KV + j + 1) * HEAD_DIM] for j in range(Q_PER_KV)], axis=0)
        o = _attend(qs, k[:, kv * HEAD_DIM:(kv + 1) * HEAD_DIM], v[:, kv * HEAD_DIM:(kv + 1) * HEAD_DIM],
                    _sink_column(sink_ref, kv, rows), valid)
        pieces += [o[j * rows:(j + 1) * rows] for j in range(Q_PER_KV)]
    return jnp.concatenate(pieces, axis=1)


ATTN_TILE = 256
CHUNKS_PER_TILE = ATTN_TILE // CHUNK
KEY_SPAN = (N_BACK + 1) * CHUNK


def _attn_prompt_body(sink_ref, q_ref, kp_ref, kc_ref, vp_ref, vc_ref, g_ref, o_ref):
    i = pl.program_id(1)
    kwin = jnp.concatenate([kp_ref[0], kc_ref[0]], axis=0)
    vwin = jnp.concatenate([vp_ref[0], vc_ref[0]], axis=0)
    key_chunk = lax.broadcasted_iota(i32, (1, KEY_SPAN), 1) // CHUNK
    for c in range(CHUNKS_PER_TILE):
        valid = (i * CHUNKS_PER_TILE + c - N_BACK + key_chunk) >= 0
        o = _heads_attend(q_ref[0, c * CHUNK:(c + 1) * CHUNK, :], kwin[c * CHUNK:c * CHUNK + KEY_SPAN],
                          vwin[c * CHUNK:c * CHUNK + KEY_SPAN], sink_ref, valid)
        o_ref[0, c * CHUNK:(c + 1) * CHUNK, :] = _rms(o, g_ref[...]).astype(bf16)


def _attn_prompt(q, k, v, sinks, g):
    b, l, _ = q.shape
    back = N_BACK * CHUNK
    per = ATTN_TILE // back
    prev = pl.BlockSpec((1, back, KV_WIDTH), lambda bi, i: (bi, jnp.maximum(i * per - 1, 0), 0))
    cur = pl.BlockSpec((1, ATTN_TILE, KV_WIDTH), lambda bi, i: (bi, i, 0))
    return pl.pallas_call(
        _attn_prompt_body,
        grid=(b, l // ATTN_TILE),
        in_specs=[pl.BlockSpec(memory_space=pltpu.SMEM),
                  pl.BlockSpec((1, ATTN_TILE, ATTN_WIDTH), lambda bi, i: (bi, i, 0)),
                  prev, cur, prev, cur,
                  pl.BlockSpec((1, ATTN_WIDTH), lambda bi, i: (0, 0))],
        out_specs=pl.BlockSpec((1, ATTN_TILE, ATTN_WIDTH), lambda bi, i: (bi, i, 0)),
        out_shape=jax.ShapeDtypeStruct((b, l, ATTN_WIDTH), bf16),
        compiler_params=_cparams(2),
        name="attn_prompt",
    )(sinks, q, k, k, v, v, g)


def _attn_sample_body(sink_ref, q_ref, kn_ref, vn_ref, ck_ref, cv_ref, g_ref, o_ref, nk_ref, nv_ref):
    kall = jnp.concatenate([ck_ref[0], kn_ref[0]], axis=0)
    vall = jnp.concatenate([cv_ref[0], vn_ref[0]], axis=0)
    o = _heads_attend(q_ref[0], kall, vall, sink_ref, None)
    o_ref[0] = _rms(o, g_ref[...]).astype(bf16)
    n_new = kn_ref.shape[1]
    nk_ref[0] = kall[n_new:]
    nv_ref[0] = vall[n_new:]


def _attn_sample(q, k_new, v_new, cache_k, cache_v, sinks, g):
    b, l, _ = q.shape
    wc = cache_k.shape[1]
    blk = lambda r, n: pl.BlockSpec((1, r, n), lambda bi: (bi, 0, 0))
    return pl.pallas_call(
        _attn_sample_body,
        grid=(b,),
        in_specs=[pl.BlockSpec(memory_space=pltpu.SMEM), blk(l, ATTN_WIDTH), blk(l, KV_WIDTH), blk(l, KV_WIDTH),
                  blk(wc, KV_WIDTH), blk(wc, KV_WIDTH), pl.BlockSpec((1, ATTN_WIDTH), lambda bi: (0, 0))],
        out_specs=[blk(l, ATTN_WIDTH), blk(wc, KV_WIDTH), blk(wc, KV_WIDTH)],
        out_shape=[jax.ShapeDtypeStruct((b, l, ATTN_WIDTH), bf16),
                   jax.ShapeDtypeStruct((b, wc, KV_WIDTH), f32), jax.ShapeDtypeStruct((b, wc, KV_WIDTH), f32)],
        compiler_params=_cparams(),
        name="attn_sample",
    )(sinks, q, k_new, v_new, cache_k, cache_v, g)


def _ssm_weights(a_re, a_im, log_dt, b_re, b_im, c_re, c_im):
    hp = lax.Precision.HIGHEST
    dt = jnp.exp(log_dt)[:, None]
    mag = jnp.exp(a_re * dt)
    ar, ai = mag * jnp.cos(a_im * dt), mag * jnp.sin(a_im * dt)
    den = a_re * a_re + a_im * a_im
    nr, ni = ar - 1.0, ai
    fr = ((nr * a_re + ni * a_im) / den)[..., None]
    fi = ((ni * a_re - nr * a_im) / den)[..., None]
    bbr, bbi = fr * b_re - fi * b_im, fr * b_im + fi * b_re
    pr, pi = [jnp.ones_like(ar)], [jnp.zeros_like(ar)]
    for _ in range(SSM_STEPS):
        pr, pi = pr + [pr[-1] * ar - pi[-1] * ai], pi + [pr[-1] * ai + pi[-1] * ar]
    rev_r, rev_i = jnp.stack(pr[SSM_STEPS - 1::-1]), jnp.stack(pi[SSM_STEPS - 1::-1])
    pr, pi = jnp.stack(pr), jnp.stack(pi)
    c_re_t, c_im_t = c_re.transpose(0, 2, 1), c_im.transpose(0, 2, 1)
    pr_l, pi_l = pr[:SSM_STEPS, :, :, None], pi[:SSM_STEPS, :, :, None]
    cpr = c_re_t[None] * pr_l - c_im_t[None] * pi_l
    cpi = c_re_t[None] * pi_l + c_im_t[None] * pr_l
    bbr_c, bbi_c = bbr.transpose(0, 2, 1)[None, :, :, :, None], bbi.transpose(0, 2, 1)[None, :, :, :, None]
    kl = jnp.sum(cpr[:, :, None] * bbr_c - cpi[:, :, None] * bbi_c, axis=3)
    kl = jnp.concatenate([jnp.zeros_like(kl[:1]), kl], axis=0)
    nd = SSM_STEPS // 2
    lag = 2 * jnp.arange(nd)[:, None, None] + jnp.arange(2)[None, None, :] - jnp.arange(2)[None, :, None]
    ksel = kl[lag + 1]
    ksel = ksel.reshape(nd, 2, 2, SSM_BLOCKS, GROUPS_PER_BLOCK, SSM_GROUP, SSM_GROUP)
    toep = ksel.transpose(3, 0, 1, 4, 5, 2, 6).reshape(SSM_BLOCKS, nd, 2 * LANES, 2 * SSM_GROUP)
    bbr_t, bbi_t = bbr.transpose(0, 2, 1), bbi.transpose(0, 2, 1)
    wo_r = rev_r[:, :, None, :] * bbr_t[None] - rev_i[:, :, None, :] * bbi_t[None]
    wo_i = rev_r[:, :, None, :] * bbi_t[None] + rev_i[:, :, None, :] * bbr_t[None]
    wo = jnp.stack([wo_r, wo_i]).reshape(2, SSM_STEPS, SSM_BLOCKS, GROUPS_PER_BLOCK, SSM_GROUP, SSM_STATE)
    wout = wo.transpose(2, 1, 3, 4, 0, 5).reshape(SSM_BLOCKS, SSM_STEPS * LANES, 2 * SSM_STATE)
    gr = c_re[None] * pr[1:, :, None, :] - c_im[None] * pi[1:, :, None, :]
    gi = c_re[None] * pi[1:, :, None, :] + c_im[None] * pr[1:, :, None, :]
    wi = jnp.stack([gr, -gi]).reshape(2, SSM_STEPS, SSM_BLOCKS, GROUPS_PER_BLOCK, SSM_GROUP, SSM_STATE)
    win = wi.transpose(2, 0, 3, 5, 1, 4).reshape(SSM_BLOCKS, 2 * STATE_COLS, SSM_STEPS * SSM_GROUP)
    lr, li = [pr[SSM_STEPS]], [pi[SSM_STEPS]]
    for _ in range(7):
        lr, li = lr + [lr[-1] * lr[-1] - li[-1] * li[-1]], li + [2.0 * lr[-1] * li[-1]]
    lev = jnp.stack([jnp.stack(lr), jnp.stack(li)], axis=1)
    lev = lev.reshape(8, 2, SSM_BLOCKS, STATE_COLS).transpose(2, 0, 1, 3)
    return toep.astype(bf16), wout.astype(bf16), win.astype(bf16), lev


def _ssm_chunk_rows(u_ref, nk):
    xs = [u_ref[0, pl.ds(s, nk, stride=SSM_STEPS), :] for s in range(SSM_STEPS)]
    pairs = [jnp.concatenate([xs[2 * p], xs[2 * p + 1]], axis=1).astype(bf16) for p in range(SSM_STEPS // 2)]
    return xs, pairs


def _ssm_intra(pairs, toep_ref, nk):
    nd = len(pairs)
    y = [None] * nd
    for d in range(nd):
        lhs = jnp.concatenate(pairs[:nd - d], axis=0) if nd - d > 1 else pairs[0]
        r = jnp.dot(lhs, toep_ref[d], preferred_element_type=f32)
        for p in range(nd - d):
            blk = r[p * nk:(p + 1) * nk]
            y[p + d] = blk if y[p + d] is None else y[p + d] + blk
    return y


def _shift_rows(x, sh):
    rows = lax.broadcasted_iota(i32, (x.shape[0], 1), 0)
    return jnp.where(rows >= sh, pltpu.roll(x, sh, axis=0), 0.0)


def _spread_matrix(k_in, sub):
    k = lax.broadcasted_iota(i32, (k_in, k_in * GROUPS_PER_BLOCK), 0)
    n = lax.broadcasted_iota(i32, (k_in, k_in * GROUPS_PER_BLOCK), 1)
    shift = int(math.log2(sub))
    same_a = jnp.right_shift(n, shift + int(math.log2(GROUPS_PER_BLOCK))) == jnp.right_shift(k, shift)
    return (same_a & ((n & (sub - 1)) == (k & (sub - 1)))).astype(bf16)


def _group_index(shape, axis, sub, offset=0):
    idx = lax.broadcasted_iota(i32, shape, axis) + offset
    return jnp.right_shift(idx, int(math.log2(sub))) & (GROUPS_PER_BLOCK - 1)


def _expand_block_diag(compact_ref, out_ref, row_sub, col_sub):
    rows, k_in = compact_ref.shape
    spread = _spread_matrix(k_in, col_sub)
    col_group = _group_index((2 * LANES, k_in * GROUPS_PER_BLOCK), 1, col_sub)
    for r0 in range(0, rows, 2 * LANES):
        full = jnp.dot(compact_ref[r0:r0 + 2 * LANES, :], spread, preferred_element_type=f32)
        row_group = _group_index(full.shape, 0, row_sub, r0)
        out_ref[r0:r0 + 2 * LANES, :] = jnp.where(row_group == col_group, full, 0.0).astype(bf16)


def _ssm_chunks(u_ref, d_ref, y_ref, toep_s, wout_s, win_s, entry_state):
    nk = u_ref.shape[1] // SSM_STEPS
    xs, pairs = _ssm_chunk_rows(u_ref, nk)
    y = _ssm_intra(pairs, toep_s, nk)
    s = jnp.dot(jnp.concatenate(pairs, axis=1), wout_s[...], preferred_element_type=f32)
    hprev, hr, hi = entry_state(s[:, :STATE_COLS], s[:, STATE_COLS:])
    y2 = _bdot(hprev, win_s[...])
    for st in range(SSM_STEPS):
        piece = (y[st // 2][:, (st % 2) * LANES:(st % 2 + 1) * LANES] + y2[:, st * LANES:(st + 1) * LANES]
                 + d_ref[...] * xs[st])
        y_ref[0, pl.ds(st, nk, stride=SSM_STEPS), :] = piece
    return hr, hi


def _ssm_body(u_ref, us_ref, h0r_ref, h0i_ref, toep_ref, wout_ref, win_ref, lev_ref, d_ref,
              y_ref, hr_ref, hi_ref, ys_ref, hrs_ref, his_ref, toep_s, wout_s, win_s):
    @pl.when(pl.program_id(1) == 0)
    def _():
        for d in range(SSM_STEPS // 2):
            _expand_block_diag(toep_ref.at[0, d], toep_s.at[d], SSM_GROUP, SSM_GROUP)
        _expand_block_diag(wout_ref.at[0], wout_s, SSM_GROUP, SSM_STATE)
        _expand_block_diag(win_ref.at[0], win_s, SSM_STATE, SSM_GROUP)

        def one_chunk(sr, si):
            h0r, h0i = h0r_ref[0], h0i_ref[0]
            ar, ai = lev_ref[0, 0, 0:1, :], lev_ref[0, 0, 1:2, :]
            return (jnp.concatenate([h0r, h0i], axis=1), sr + ar * h0r - ai * h0i, si + ar * h0i + ai * h0r)
        hrs_ref[0], his_ref[0] = _ssm_chunks(us_ref, d_ref, ys_ref, toep_s, wout_s, win_s, one_chunk)

    def scan_chunks(sr, si):
        nk = sr.shape[0]
        level = 0
        while (1 << level) < nk:
            ar, ai = lev_ref[0, level, 0:1, :], lev_ref[0, level, 1:2, :]
            tr, ti = _shift_rows(sr, 1 << level), _shift_rows(si, 1 << level)
            sr, si = sr + ar * tr - ai * ti, si + ar * ti + ai * tr
            level += 1
        return (jnp.concatenate([_shift_rows(sr, 1), _shift_rows(si, 1)], axis=1), sr[nk - 1:nk], si[nk - 1:nk])
    hr_ref[0, 0], hi_ref[0, 0] = _ssm_chunks(u_ref, d_ref, y_ref, toep_s, wout_s, win_s, scan_chunks)


def _ssm(u, us, h0r, h0i, toep, wout, win, lev, d):
    b, l, _ = u.shape
    rows = us.shape[1]
    nb = rows // SSM_STEPS
    wspec = lambda a: pl.BlockSpec((1,) + a.shape[1:], lambda j, bi: (j,) + (0,) * (a.ndim - 1))
    st = pl.BlockSpec((1, 1, 1, STATE_COLS), lambda j, bi: (bi, j, 0, 0))
    sst = pl.BlockSpec((1, nb, STATE_COLS), lambda j, bi: (j, 0, 0))
    seq = pl.BlockSpec((1, l, LANES), lambda j, bi: (bi, 0, j))
    sseq = pl.BlockSpec((1, rows, LANES), lambda j, bi: (0, 0, j))
    return pl.pallas_call(
        _ssm_body,
        grid=(SSM_BLOCKS, b),
        in_specs=[seq, sseq, sst, sst, wspec(toep), wspec(wout), wspec(win), wspec(lev),
                  pl.BlockSpec((1, LANES), lambda j, bi: (0, j))],
        out_specs=[seq, st, st, sseq, sst, sst],
        out_shape=[jax.ShapeDtypeStruct((b, l, SSM_WIDTH), f32),
                   jax.ShapeDtypeStruct((b, SSM_BLOCKS, 1, STATE_COLS), f32),
                   jax.ShapeDtypeStruct((b, SSM_BLOCKS, 1, STATE_COLS), f32),
                   jax.ShapeDtypeStruct((1, rows, SSM_WIDTH), f32),
                   jax.ShapeDtypeStruct((SSM_BLOCKS, nb, STATE_COLS), f32),
                   jax.ShapeDtypeStruct((SSM_BLOCKS, nb, STATE_COLS), f32)],
        scratch_shapes=[pltpu.VMEM((SSM_STEPS // 2, 2 * LANES, 2 * LANES), bf16),
                        pltpu.VMEM((SSM_STEPS * LANES, 2 * STATE_COLS), bf16),
                        pltpu.VMEM((2 * STATE_COLS, SSM_STEPS * LANES), bf16)],
        compiler_params=_cparams(2),
        name="ssm",
    )(u, us, h0r, h0i, toep, wout, win, lev, d)


ROUTER_COLS = LANES


def _mix_body(ap_ref, as_ref, yp_ref, ys_ref, xp_ref, xs_ref, wglu_ref, bglu_ref, gs_ref, wout_ref, gf_ref, wr_ref,
              br_ref, tri_ref, upper_ref, x1_ref, xl_ref, wts_ref, lpos_ref, n_ref, loff_ref, *, prompt_steps):
    is_prompt = pl.program_id(0) < prompt_steps
    pick_rows = lambda p_ref, s_ref: jnp.where(is_prompt, p_ref[...], jnp.concatenate([s_ref[...]] * MIX_TILES, axis=0))
    y = pick_rows(yp_ref, ys_ref)
    y = 0.5 * y * (1.0 + jnp.tanh(math.sqrt(2.0 / math.pi) * (y + 0.044715 * (y * y * y))))
    y = y * jax.nn.sigmoid(_bdot(y, wglu_ref[...]) + bglu_ref[...])
    attn = jnp.where(is_prompt, ap_ref[...].astype(f32),
                     jnp.concatenate([as_ref[...].astype(f32)] * MIX_TILES, axis=0)).astype(bf16)
    cat = jnp.concatenate([attn, _rms(y, gs_ref[...]).astype(bf16)], axis=1)
    x1 = pick_rows(xp_ref, xs_ref) + jnp.dot(cat, wout_ref[...], preferred_element_type=f32)
    x1_ref[...] = x1
    hf = _rms(x1, gf_ref[...])

    logits = _bdot(hf, wr_ref[...]) + br_ref[...]
    le = logits[:, :N_EXPERTS]
    lg = logits[:, N_EXPERTS:N_EXPERTS + N_EXPERT_GROUPS]
    tm = le.shape[0]
    gmax = jnp.max(lg, axis=-1, keepdims=True)
    gi = lax.broadcasted_iota(i32, (tm, N_EXPERT_GROUPS), 1).astype(f32)
    gsel = jnp.min(jnp.where(lg == gmax, gi, float(N_EXPERT_GROUPS)), axis=-1, keepdims=True)
    pg = 1.0 / jnp.sum(jnp.exp(lg - gmax), axis=-1, keepdims=True)
    ei_int = lax.broadcasted_iota(i32, (tm, N_EXPERTS), 1)
    ei = ei_int.astype(f32)
    egroup = jnp.right_shift(ei_int, int(math.log2(EXPERTS_PER_GROUP))).astype(f32)
    lm = jnp.where(egroup == gsel, le, NEG)
    v1 = jnp.max(lm, axis=-1, keepdims=True)
    i1 = jnp.min(jnp.where(lm == v1, ei, float(N_EXPERTS)), axis=-1, keepdims=True)
    lm2 = jnp.where(ei == i1, NEG, lm)
    v2 = jnp.max(lm2, axis=-1, keepdims=True)
    i2 = jnp.min(jnp.where(lm2 == v2, ei, float(N_EXPERTS)), axis=-1, keepdims=True)
    ex = jnp.exp(v2 - v1)
    wts_ref[...] = jnp.concatenate([pg / (1.0 + ex), pg * ex / (1.0 + ex)], axis=1)

    oh1 = (ei == i1).astype(f32)
    oh2 = (ei == i2).astype(f32)
    hf_bf = hf.astype(bf16)
    sorted_row = lax.broadcasted_iota(i32, (ROW_TILE, SLOTS * ROW_TILE), 1).astype(f32)
    for h in range(MIX_TILES):
        rows = slice(h * ROW_TILE, (h + 1) * ROW_TILE)
        both = (oh1[rows] + oh2[rows]).astype(bf16)
        before = jnp.dot(tri_ref[...], both, preferred_element_type=f32)
        count = jnp.sum(oh1[rows] + oh2[rows], axis=0, keepdims=True)
        lower = jnp.sum(jnp.dot(both, upper_ref[...], preferred_element_type=f32), axis=0, keepdims=True)
        lp1 = jnp.sum(oh1[rows] * (before + lower), axis=-1, keepdims=True)
        lp2 = jnp.sum(oh2[rows] * (before + lower), axis=-1, keepdims=True)
        lpos_ref[rows, :] = jnp.concatenate([lp1, lp2], axis=1)
        n_ref[h] = count
        loff_ref[h] = lower
        pick = ((sorted_row == lp1) | (sorted_row == lp2)).astype(bf16)
        xl = lax.dot_general(pick, hf_bf[rows], (((0,), (0,)), ((), ())), preferred_element_type=f32)
        _store_row_major(xl_ref, h * SLOTS * ROW_TILE, xl)


def _mix(a_p, a_s, ys_p, ys_s, x_p, x_s, wglu, bglu, gs, wout, gf, wr, br, tri, upper):
    tm = MIX_TILES * ROW_TILE
    assert x_p.shape[0] % tm == 0 and x_s.shape[0] == ROW_TILE
    prompt_steps = x_p.shape[0] // tm
    t = x_p.shape[0] + x_s.shape[0]
    p_rows = lambda n: pl.BlockSpec((tm, n), lambda i: (jnp.minimum(i, prompt_steps - 1), 0))
    s_rows = lambda n: pl.BlockSpec((ROW_TILE, n), lambda i: (0, 0))
    row = lambda n: pl.BlockSpec((tm, n), lambda i: (i, 0))
    full = lambda arr: pl.BlockSpec(arr.shape, lambda i: (0,) * arr.ndim)
    per_tile = pl.BlockSpec((MIX_TILES, 1, N_EXPERTS), lambda i: (i, 0, 0))
    return pl.pallas_call(
        functools.partial(_mix_body, prompt_steps=prompt_steps),
        grid=(prompt_steps + 1,),
        in_specs=[p_rows(ATTN_WIDTH), s_rows(ATTN_WIDTH), p_rows(SSM_WIDTH), s_rows(SSM_WIDTH), p_rows(D_MODEL),
                  s_rows(D_MODEL), full(wglu), full(bglu), full(gs), full(wout), full(gf), full(wr), full(br),
                  full(tri), full(upper)],
        out_specs=[row(D_MODEL), pl.BlockSpec((SLOTS * tm * PIECES, LANES), lambda i: (i, 0)), row(2), row(2),
                   per_tile, per_tile],
        out_shape=[jax.ShapeDtypeStruct((t, D_MODEL), f32),
                   jax.ShapeDtypeStruct((SLOTS * t * PIECES, LANES), f32),
                   jax.ShapeDtypeStruct((t, 2), f32), jax.ShapeDtypeStruct((t, 2), f32),
                   jax.ShapeDtypeStruct((t // ROW_TILE, 1, N_EXPERTS), f32),
                   jax.ShapeDtypeStruct((t // ROW_TILE, 1, N_EXPERTS), f32)],
        compiler_params=_cparams(),
        name="mix",
    )(a_p, a_s, ys_p, ys_s, x_p, x_s, wglu, bglu, gs, wout, gf, wr, br, tri, upper)


def _store_row_major(ref, first_row, x):
    for c in range(PIECES):
        ref[pl.ds(first_row * PIECES + c, x.shape[0], stride=PIECES), :] = x[:, c * LANES:(c + 1) * LANES]


def _load_row_major(ref, n_rows):
    return jnp.concatenate([ref[pl.ds(c, n_rows, stride=PIECES), :] for c in range(PIECES)], axis=1)


def _copy_rows(src, s_row, dst, d_row, n_rows, sem):
    return pltpu.make_async_copy(src.at[pl.ds(pl.multiple_of(s_row * PIECES, PIECES), n_rows * PIECES), :],
                                 dst.at[pl.ds(pl.multiple_of(d_row * PIECES, PIECES), n_rows * PIECES), :], sem)


def _for_each_piece(n, fn):
    off = 0
    for piece in RUN_PIECES:
        @pl.when((n & piece) != 0)
        def _(off=off, piece=piece):
            fn(off, piece)
        off = off + (n & piece)


def _copy_run(n, fn):
    long_part = n & ~(2 * SHORT_RUN - 1)

    @pl.when(long_part != 0)
    def _():
        _for_each_piece(long_part, fn)
    off = long_part
    for piece in RUN_PIECES:
        if piece <= SHORT_RUN:
            @pl.when((n & piece) != 0)
            def _(off=off, piece=piece):
                fn(off, piece)
            off = off + (n & piece)


def _experts_body(te_ref, tpos_ref, tvalid_ref, tlo_ref, thi_ref, n_ref, cum_ref, loff_ref,
                  xl_hbm, wg_ref, wu_ref, wd_ref, o_ref, wg_s, wu_s, wd_s, sem, *xbufs):
    i = pl.program_id(0)
    last = pl.num_programs(0) - 1
    token_tiles = n_ref.shape[0] // N_EXPERTS
    ring = len(xbufs)

    def start_run(t, tau, enabled, buf, buf_sem):
        e, lo = te_ref[t], tpos_ref[t]
        k = tau * N_EXPERTS + e
        s, n = cum_ref[k], n_ref[k]
        a = jnp.maximum(s, lo)
        length = jnp.where(enabled, jnp.maximum(jnp.minimum(s + n, lo + ROW_TILE) - a, 0), 0)
        local = loff_ref[k] + (a - s)
        _for_each_piece(length, lambda off, piece: _copy_rows(
            xl_hbm, tau * (SLOTS * ROW_TILE) + local + off, buf, a - lo + off, piece, buf_sem).start())

    def start_runs_loop(t, first, stop, buf, buf_sem):
        def run(tau, c):
            start_run(t, tau, True, buf, buf_sem)
            return c
        lax.fori_loop(first, stop, run, 0)

    def clear(buf):
        buf[...] = jnp.zeros_like(buf)

    @pl.when(i == 0)
    def _():
        for buf in xbufs:
            clear(buf)
        for t in range(ring - 1):
            @pl.when(tvalid_ref[t] > 0)
            def _(t=t):
                start_runs_loop(t, tlo_ref[t], thi_ref[t] + 1, xbufs[t], sem.at[t])

    @pl.when((i == 0) | (te_ref[i] != te_ref[jnp.maximum(i - 1, 0)]))
    def _():
        wg_s[...] = wg_ref[0].astype(bf16)
        wu_s[...] = wu_ref[0].astype(bf16)
        wd_s[...] = wd_ref[0].astype(bf16)

    valid = tvalid_ref[i]

    def tile_step(cur):
        ahead = (cur + ring - 1) % ring
        buf, buf_sem, next_buf, next_sem = xbufs[cur], sem.at[cur], xbufs[ahead], sem.at[ahead]
        _for_each_piece(valid, lambda off, piece: _copy_rows(
            xl_hbm, 0, buf, 0, piece, buf_sem).wait())

        nxt = jnp.minimum(i + ring - 1, last)
        go = (i + ring - 1 <= last) & (tvalid_ref[nxt] > 0)
        first, final = tlo_ref[nxt], thi_ref[nxt]
        for j in range(UNROLLED_RUNS):
            start_run(nxt, jnp.minimum(first + j, token_tiles - 1), go & (first + j <= final), next_buf, next_sem)

        x = _load_row_major(buf, ROW_TILE).astype(bf16)
        clear(buf)
        hg = jnp.dot(x, wg_s[...], preferred_element_type=f32)
        hu = jnp.dot(x, wu_s[...], preferred_element_type=f32)
        y = jnp.dot((hg * jax.nn.sigmoid(hg) * hu).astype(bf16), wd_s[...], preferred_element_type=f32)
        _store_row_major(o_ref, 0, y)

        @pl.when(go & (final - first >= UNROLLED_RUNS))
        def _():
            start_runs_loop(nxt, first + UNROLLED_RUNS, final + 1, next_buf, next_sem)

    for cur in range(ring):
        pl.when((valid > 0) & (i % ring == cur))(functools.partial(tile_step, cur))

    @pl.when(valid == 0)
    def _():
        o_ref[...] = jnp.zeros_like(o_ref)


def _experts(tables, xl, wg, wu, wd, tiles):
    wspec = lambda a: pl.BlockSpec((1,) + a.shape[1:], lambda i, te, *_: (te[i], 0, 0))
    return pl.pallas_call(
        _experts_body,
        grid_spec=pltpu.PrefetchScalarGridSpec(
            num_scalar_prefetch=len(tables),
            grid=(tiles,),
            in_specs=[pl.BlockSpec(memory_space=pl.ANY), wspec(wg), wspec(wu), wspec(wd)],
            out_specs=pl.BlockSpec((ROW_TILE * PIECES, LANES), lambda i, *_: (i, 0)),
            scratch_shapes=[pltpu.VMEM(wg.shape[1:], bf16), pltpu.VMEM(wu.shape[1:], bf16), pltpu.VMEM(wd.shape[1:], bf16),
                            pltpu.SemaphoreType.DMA((EXPERT_RING,))]
            + [pltpu.VMEM((ROW_TILE * PIECES, LANES), f32)] * EXPERT_RING),
        out_shape=jax.ShapeDtypeStruct((tiles * ROW_TILE * PIECES, LANES), f32),
        compiler_params=_cparams(),
        name="moe_experts",
    )(*tables, xl, wg, wu, wd)


def _combine_body(n_ref, gpos_ref, loff_ref, ys_hbm, x_ref, w_ref, l_ref, g_ref, op_ref, os_ref, sem, *ybufs,
                  prompt_tiles):
    i = pl.program_id(0)
    last = pl.num_programs(0) - 1
    tile_rows = SLOTS * ROW_TILE
    ring = len(ybufs)

    def start_run(t, e, enabled, buf, buf_sem):
        k = t * N_EXPERTS + e
        _for_each_piece(jnp.where(enabled, n_ref[k], 0), lambda off, piece: _copy_rows(
            ys_hbm, gpos_ref[k] + off, buf, loff_ref[k] + off, piece, buf_sem).start())

    @pl.when(i == 0)
    def _():
        for t in range(ring - 1):
            def run(e, c, t=t):
                start_run(t, e, True, ybufs[t], sem.at[t])
                return c
            lax.fori_loop(0, N_EXPERTS, run, 0)

    def step(cur):
        ahead = (cur + ring - 1) % ring
        buf, buf_sem, next_buf, next_sem = ybufs[cur], sem.at[cur], ybufs[ahead], sem.at[ahead]
        _copy_rows(ys_hbm, 0, buf, 0, tile_rows, buf_sem).wait()
        for e in range(N_EXPERTS):
            start_run(jnp.minimum(i + ring - 1, last), e, i + ring - 1 <= last, next_buf, next_sem)
        yl = _load_row_major(buf, tile_rows).astype(bf16)
        sorted_row = lax.broadcasted_iota(i32, (ROW_TILE, tile_rows), 1).astype(f32)
        w, lp = w_ref[...], l_ref[...]
        y1 = jnp.dot((sorted_row == lp[:, 0:1]).astype(bf16), yl, preferred_element_type=f32)
        y2 = jnp.dot((sorted_row == lp[:, 1:2]).astype(bf16), yl, preferred_element_type=f32)
        out = _rms(x_ref[...] + (w[:, 0:1] * y1 + w[:, 1:2] * y2), g_ref[...])

        @pl.when(i < prompt_tiles)
        def _():
            op_ref[...] = out

        @pl.when(i >= prompt_tiles)
        def _():
            os_ref[...] = out

    for cur in range(ring):
        pl.when(i % ring == cur)(functools.partial(step, cur))


def _combine(tables, ys, x1, wts, lpos, g, prompt_rows):
    prompt_tiles = prompt_rows // ROW_TILE
    tiles = x1.shape[0] // ROW_TILE
    row = lambda n: pl.BlockSpec((ROW_TILE, n), lambda i, *_: (i, 0))
    p_rows = lambda n: pl.BlockSpec((ROW_TILE, n), lambda i, *_: (jnp.minimum(i, prompt_tiles - 1), 0))
    s_rows = lambda n: pl.BlockSpec((ROW_TILE, n), lambda i, *_: (jnp.maximum(i - prompt_tiles, 0), 0))
    return pl.pallas_call(
        functools.partial(_combine_body, prompt_tiles=prompt_tiles),
        grid_spec=pltpu.PrefetchScalarGridSpec(
            num_scalar_prefetch=len(tables),
            grid=(tiles,),
            in_specs=[pl.BlockSpec(memory_space=pl.ANY), row(D_MODEL), row(2), row(2),
                      pl.BlockSpec((1, D_MODEL), lambda i, *_: (0, 0))],
            out_specs=[p_rows(D_MODEL), s_rows(D_MODEL)],
            scratch_shapes=[pltpu.SemaphoreType.DMA((COMBINE_RING,))]
            + [pltpu.VMEM((SLOTS * ROW_TILE * PIECES, LANES), f32)] * COMBINE_RING),
        out_shape=[jax.ShapeDtypeStruct((prompt_rows, D_MODEL), f32),
                   jax.ShapeDtypeStruct((x1.shape[0] - prompt_rows, D_MODEL), f32)],
        compiler_params=_cparams(),
        name="moe_combine",
    )(*tables, ys, x1, wts, lpos, g)


def _moe_tables(n, loff, tiles):
    cum = jnp.cumsum(n, axis=0) - n
    counts = jnp.sum(n, axis=0)
    padded = (counts + ROW_TILE - 1) // ROW_TILE * ROW_TILE
    ends = jnp.cumsum(padded)
    starts = ends - padded
    first = jnp.arange(tiles, dtype=i32) * ROW_TILE
    expert = jnp.minimum(jnp.sum((first[:, None] >= ends[None, :]).astype(i32), axis=1), N_EXPERTS - 1)
    sel = expert[:, None] == jnp.arange(N_EXPERTS)[None, :]
    pick = lambda v: jnp.sum(jnp.where(sel, v[None, :], 0), axis=1)
    pos = first - pick(starts)
    valid = jnp.where(first < ends[-1], jnp.clip(pick(counts) - pos, 0, ROW_TILE), 0)
    cum_t, n_t = cum.T[expert], n.T[expert]
    touches = (cum_t + n_t > pos[:, None]) & (cum_t < (pos + ROW_TILE)[:, None]) & (n_t > 0)
    tau = jnp.arange(n.shape[0], dtype=i32)[None, :]
    lo = jnp.min(jnp.where(touches, tau, n.shape[0]), axis=1)
    hi = jnp.max(jnp.where(touches, tau, -1), axis=1)
    as_i32 = lambda v: v.astype(i32)
    flat = lambda v: v.reshape(-1).astype(i32)
    expert_tables = tuple(map(as_i32, (expert, pos, valid, lo, hi))) + (flat(n), flat(cum), flat(loff))
    combine_tables = (flat(n), flat(starts[None, :] + cum), flat(loff))
    return expert_tables, combine_tables


def kernel(x_prompt, x_sample, cache_k, cache_v, state_ssm_re, state_ssm_im, g_norm_mix, w_in, attn_sinks, ssm_a_re,
           ssm_a_im, ssm_log_dt, ssm_b_re, ssm_b_im, ssm_c_re, ssm_c_im, ssm_d, w_glu, b_glu, g_attn_out, g_ssm_out,
           w_out, g_norm_ffn, w_router_group, b_router_group, w_router_expert, b_router_expert, w_exp_gate, w_exp_up,
           w_exp_down, g_final):
    bp, lp, _ = x_prompt.shape
    bs, ls, _ = x_sample.shape
    depth = w_in.shape[0]
    assert depth == 1 and ls == SSM_STEPS and lp % ATTN_TILE == 0 and (bs * ls) % ROW_TILE == 0
    tp, ts = bp * lp, bs * ls
    wc = cache_k.shape[2]
    row2 = lambda v: v.reshape(1, -1)

    xp = x_prompt.reshape(tp, D_MODEL)
    xs = x_sample.reshape(ts, D_MODEL)
    w_in_bf = w_in[0].astype(bf16)
    qp, kp, vp, up = _proj(xp, row2(g_norm_mix[0]), w_in_bf, 512)
    qs, kq, vq, us = _proj(xs, row2(g_norm_mix[0]), w_in_bf, ts)

    sinks = attn_sinks[0]
    g_att = row2(g_attn_out[0])
    ap = _attn_prompt(qp.reshape(bp, lp, -1), kp.reshape(bp, lp, -1), vp.reshape(bp, lp, -1), sinks, g_att)
    a_s, k_roll, v_roll = _attn_sample(qs.reshape(bs, ls, -1), kq.reshape(bs, ls, -1), vq.reshape(bs, ls, -1),
                                       cache_k[0].reshape(bs, wc, KV_WIDTH), cache_v[0].reshape(bs, wc, KV_WIDTH),
                                       sinks, g_att)

    toep, wso, wsi, lev = _ssm_weights(ssm_a_re[0], ssm_a_im[0], ssm_log_dt[0], ssm_b_re[0], ssm_b_im[0],
                                       ssm_c_re[0], ssm_c_im[0])
    d_row = row2(ssm_d[0])
    to_blocks = lambda h: h.reshape(bs, SSM_BLOCKS, STATE_COLS).transpose(1, 0, 2)
    from_blocks = lambda h: h.transpose(1, 0, 2).reshape(bs, SSM_GROUPS, SSM_STATE)
    yp, hrp, hip, ysm, hrs, his = _ssm(up.reshape(bp, lp, -1), us.reshape(1, ts, -1), to_blocks(state_ssm_re[0]),
                                       to_blocks(state_ssm_im[0]), toep, wso, wsi, lev, d_row)

    wr = jnp.zeros((D_MODEL, ROUTER_COLS), f32)
    wr = wr.at[:, :N_EXPERTS].set(w_router_expert[0]).at[:, N_EXPERTS:N_EXPERTS + N_EXPERT_GROUPS].set(w_router_group[0])
    br = jnp.zeros((1, ROUTER_COLS), f32)
    br = br.at[0, :N_EXPERTS].set(b_router_expert[0]).at[0, N_EXPERTS:N_EXPERTS + N_EXPERT_GROUPS].set(b_router_group[0])
    tri = jnp.tril(jnp.ones((ROW_TILE, ROW_TILE), bf16), -1)
    upper = jnp.triu(jnp.ones((N_EXPERTS, N_EXPERTS), bf16), 1)
    mix_w = (w_glu[0].astype(bf16), row2(b_glu[0]), row2(g_ssm_out[0]), w_out[0].astype(bf16), row2(g_norm_ffn[0]),
             wr.astype(bf16), br, tri, upper)
    x1, xl, wts, lpos, n_rows, n_off = _mix(ap.reshape(tp, -1), a_s.reshape(ts, -1), yp.reshape(tp, -1),
                                            ysm.reshape(ts, -1), xp, xs, *mix_w)

    per_tile = lambda v: v.reshape(-1, N_EXPERTS).astype(i32)
    tiles = (SLOTS * (tp + ts)) // ROW_TILE + N_EXPERTS
    expert_tables, combine_tables = _moe_tables(per_tile(n_rows), per_tile(n_off), tiles)
    expert_out = _experts(expert_tables, xl, w_exp_gate[0], w_exp_up[0], w_exp_down[0], tiles)
    y_p, y_s = _combine(combine_tables, expert_out, x1, wts, lpos, row2(g_final), tp)

    kvshape = lambda a, b: a.reshape(1, b, -1, N_KV_HEADS, HEAD_DIM)
    block_state = lambda h: h.reshape(bp, SSM_GROUPS, SSM_STATE)[None]
    wcp = min(WINDOW, lp)
    return (y_p.reshape(bp, lp, D_MODEL), y_s.reshape(bs, ls, D_MODEL),
            kvshape(kp.reshape(bp, lp, -1)[:, lp - wcp:], bp), kvshape(vp.reshape(bp, lp, -1)[:, lp - wcp:], bp),
            block_state(hrp), block_state(hip),
            kvshape(k_roll, bs), kvshape(v_roll, bs),
            from_blocks(hrs)[None], from_blocks(his)[None])
```

```python
import functools
import math

import jax
import jax.numpy as jnp
from jax import lax
from jax.experimental import pallas as pl
from jax.experimental.pallas import tpu as pltpu

f32, bf16, i32 = jnp.float32, jnp.bfloat16, jnp.int32

D_MODEL = 1024
CHUNK = 64
N_BACK = 2
WINDOW = 128
ATTN_WIDTH = 512
HEAD_DIM = 64
N_KV_HEADS = 2
Q_PER_KV = 4
KV_WIDTH = 128
SSM_WIDTH = 512
SSM_GROUP = 16
SSM_GROUPS = 32
SSM_STATE = 64
PROJ_WIDTH = 1280
N_EXPERT_GROUPS = 4
EXPERTS_PER_GROUP = 8
N_EXPERTS = 32
D_EXPERT = 512
EPS = 1e-6
NEG = -1e30

LANES = 128
SSM_STEPS = 16
SSM_BLOCKS = SSM_WIDTH // LANES
GROUPS_PER_BLOCK = LANES // SSM_GROUP
STATE_COLS = GROUPS_PER_BLOCK * SSM_STATE
ROW_TILE = 256
SLOTS = 2
PIECES = D_MODEL // LANES
RUN_PIECES = tuple(1 << b for b in reversed(range(int(math.log2(ROW_TILE)) + 1)))
SHORT_PIECES = tuple(p for p in RUN_PIECES if p <= 32)
LONG_PIECES = tuple(p for p in RUN_PIECES if p > 32)
MIX_TILES = 2
UNROLLED_RUNS = 20
EXPERT_RING = 3
COMBINE_RING = 3
VMEM_LIMIT = 56 * 1024 * 1024


def _cparams(n_axes=1, limit=VMEM_LIMIT):
    return pltpu.CompilerParams(dimension_semantics=("arbitrary",) * n_axes, vmem_limit_bytes=limit)


def _rms(x, g):
    return x * lax.rsqrt(jnp.mean(x * x, axis=-1, keepdims=True) + EPS) * g


def _bdot(a, b):
    return jnp.dot(a.astype(bf16), b.astype(bf16), preferred_element_type=f32)


def _proj_body(x_ref, g_ref, w_ref, q_ref, k_ref, v_ref, u_ref):
    h = _rms(x_ref[...], g_ref[...])
    z = _bdot(h, w_ref[...])
    q_ref[...] = z[:, :ATTN_WIDTH] * (HEAD_DIM ** -0.5)
    k_ref[...] = z[:, ATTN_WIDTH:ATTN_WIDTH + KV_WIDTH]
    v_ref[...] = z[:, ATTN_WIDTH + KV_WIDTH:ATTN_WIDTH + 2 * KV_WIDTH]
    u_ref[...] = z[:, ATTN_WIDTH + 2 * KV_WIDTH:]


def _proj(x2d, g, w_bf, tm):
    t = x2d.shape[0]
    row = lambda n: pl.BlockSpec((tm, n), lambda i: (i, 0))
    full = lambda a: pl.BlockSpec(a.shape, lambda i: (0,) * a.ndim)
    return pl.pallas_call(
        _proj_body,
        grid=(t // tm,),
        in_specs=[row(D_MODEL), full(g), full(w_bf)],
        out_specs=[row(ATTN_WIDTH), row(KV_WIDTH), row(KV_WIDTH), row(SSM_WIDTH)],
        out_shape=[jax.ShapeDtypeStruct((t, n), f32) for n in (ATTN_WIDTH, KV_WIDTH, KV_WIDTH, SSM_WIDTH)],
        compiler_params=_cparams(),
        name="proj",
    )(x2d, g, w_bf)


def _sink_column(sink_ref, kv, rows_per_head):
    r = lax.broadcasted_iota(i32, (Q_PER_KV * rows_per_head, 1), 0)
    col = jnp.full((Q_PER_KV * rows_per_head, 1), sink_ref[kv * Q_PER_KV], f32)
    for j in range(1, Q_PER_KV):
        col = jnp.where(r >= j * rows_per_head, sink_ref[kv * Q_PER_KV + j], col)
    return col


def _attend(qs, kc, vc, sink_col, valid):
    s = lax.dot_general(qs.astype(bf16), kc.astype(bf16), (((1,), (1,)), ((), ())), preferred_element_type=f32)
    if valid is not None:
        s = jnp.where(valid, s, NEG)
    m = jnp.maximum(jnp.max(s, axis=-1, keepdims=True), sink_col)
    p = jnp.exp(s - m)
    denom = jnp.sum(p, axis=-1, keepdims=True) + jnp.exp(sink_col - m)
    return _bdot(p, vc) / denom


def _heads_attend(q, k, v, sink_ref, valid):
    rows = q.shape[0]
    pieces = []
    for kv in range(N_KV_HEADS):
        qs = jnp.concatenate(
            [q[:, (kv * Q_PER_KV + j) * HEAD_DIM:(kv * Q_PER_KV + j + 1) * HEAD_DIM] for j in range(Q_PER_KV)], axis=0)
        o = _attend(qs, k[:, kv * HEAD_DIM:(kv + 1) * HEAD_DIM], v[:, kv * HEAD_DIM:(kv + 1) * HEAD_DIM],
                    _sink_column(sink_ref, kv, rows), valid)
        pieces += [o[j * rows:(j + 1) * rows] for j in range(Q_PER_KV)]
    return jnp.concatenate(pieces, axis=1)


ATTN_TILE = 256
CHUNKS_PER_TILE = ATTN_TILE // CHUNK
KEY_SPAN = (N_BACK + 1) * CHUNK


def _attn_prompt_body(sink_ref, q_ref, kp_ref, kc_ref, vp_ref, vc_ref, g_ref, o_ref):
    i = pl.program_id(1)
    kwin = jnp.concatenate([kp_ref[0], kc_ref[0]], axis=0)
    vwin = jnp.concatenate([vp_ref[0], vc_ref[0]], axis=0)
    key_chunk = lax.broadcasted_iota(i32, (1, KEY_SPAN), 1) // CHUNK
    for c in range(CHUNKS_PER_TILE):
        valid = (i * CHUNKS_PER_TILE + c - N_BACK + key_chunk) >= 0
        o = _heads_attend(q_ref[0, c * CHUNK:(c + 1) * CHUNK, :], kwin[c * CHUNK:c * CHUNK + KEY_SPAN],
                          vwin[c * CHUNK:c * CHUNK + KEY_SPAN], sink_ref, valid)
        o_ref[0, c * CHUNK:(c + 1) * CHUNK, :] = _rms(o, g_ref[...]).astype(bf16)


def _attn_prompt(q, k, v, sinks, g):
    b, l, _ = q.shape
    back = N_BACK * CHUNK
    per = ATTN_TILE // back
    prev = pl.BlockSpec((1, back, KV_WIDTH), lambda bi, i: (bi, jnp.maximum(i * per - 1, 0), 0))
    cur = pl.BlockSpec((1, ATTN_TILE, KV_WIDTH), lambda bi, i: (bi, i, 0))
    return pl.pallas_call(
        _attn_prompt_body,
        grid=(b, l // ATTN_TILE),
        in_specs=[pl.BlockSpec(memory_space=pltpu.SMEM),
                  pl.BlockSpec((1, ATTN_TILE, ATTN_WIDTH), lambda bi, i: (bi, i, 0)),
                  prev, cur, prev, cur,
                  pl.BlockSpec((1, ATTN_WIDTH), lambda bi, i: (0, 0))],
        out_specs=pl.BlockSpec((1, ATTN_TILE, ATTN_WIDTH), lambda bi, i: (bi, i, 0)),
        out_shape=jax.ShapeDtypeStruct((b, l, ATTN_WIDTH), bf16),
        compiler_params=_cparams(2),
        name="attn_prompt",
    )(sinks, q, k, k, v, v, g)


def _attn_sample_body(sink_ref, q_ref, kn_ref, vn_ref, ck_ref, cv_ref, g_ref, o_ref, nk_ref, nv_ref):
    kall = jnp.concatenate([ck_ref[0], kn_ref[0]], axis=0)
    vall = jnp.concatenate([cv_ref[0], vn_ref[0]], axis=0)
    o = _heads_attend(q_ref[0], kall, vall, sink_ref, None)
    o_ref[0] = _rms(o, g_ref[...]).astype(bf16)
    n_new = kn_ref.shape[1]
    nk_ref[0] = kall[n_new:]
    nv_ref[0] = vall[n_new:]


def _attn_sample(q, k_new, v_new, cache_k, cache_v, sinks, g):
    b, l, _ = q.shape
    wc = cache_k.shape[1]
    blk = lambda r, n: pl.BlockSpec((1, r, n), lambda bi: (bi, 0, 0))
    return pl.pallas_call(
        _attn_sample_body,
        grid=(b,),
        in_specs=[pl.BlockSpec(memory_space=pltpu.SMEM), blk(l, ATTN_WIDTH), blk(l, KV_WIDTH), blk(l, KV_WIDTH),
                  blk(wc, KV_WIDTH), blk(wc, KV_WIDTH), pl.BlockSpec((1, ATTN_WIDTH), lambda bi: (0, 0))],
        out_specs=[blk(l, ATTN_WIDTH), blk(wc, KV_WIDTH), blk(wc, KV_WIDTH)],
        out_shape=[jax.ShapeDtypeStruct((b, l, ATTN_WIDTH), bf16),
                   jax.ShapeDtypeStruct((b, wc, KV_WIDTH), f32), jax.ShapeDtypeStruct((b, wc, KV_WIDTH), f32)],
        compiler_params=_cparams(),
        name="attn_sample",
    )(sinks, q, k_new, v_new, cache_k, cache_v, g)


def _ssm_weights(a_re, a_im, log_dt, b_re, b_im, c_re, c_im):
    dt = jnp.exp(log_dt)[:, None]
    lam_r, lam_i = a_re * dt, a_im * dt

    def power(n):
        mag = jnp.exp(n * lam_r)
        return mag * jnp.cos(n * lam_i), mag * jnp.sin(n * lam_i)

    ar, ai = power(1.0)
    den = a_re * a_re + a_im * a_im
    nr, ni = ar - 1.0, ai
    fr = ((nr * a_re + ni * a_im) / den)[..., None]
    fi = ((ni * a_re - nr * a_im) / den)[..., None]
    bbr, bbi = fr * b_re - fi * b_im, fr * b_im + fi * b_re
    steps = jnp.arange(SSM_STEPS + 1, dtype=f32)[:, None, None]
    pr, pi = power(steps)
    rev_r, rev_i = power(SSM_STEPS - 1.0 - steps[:SSM_STEPS])
    c_re_p, c_im_p = c_re.transpose(2, 0, 1), c_im.transpose(2, 0, 1)
    pr_p, pi_p = pr[:SSM_STEPS].transpose(0, 2, 1)[..., None], pi[:SSM_STEPS].transpose(0, 2, 1)[..., None]
    cpr = c_re_p[None] * pr_p - c_im_p[None] * pi_p
    cpi = c_re_p[None] * pi_p + c_im_p[None] * pr_p
    bbr_p, bbi_p = bbr.transpose(1, 0, 2)[None, ..., None], bbi.transpose(1, 0, 2)[None, ..., None]
    kl = jnp.sum(cpr[:, :, :, None, :] * bbr_p - cpi[:, :, :, None, :] * bbi_p, axis=1)
    kl = jnp.concatenate([jnp.zeros_like(kl[:1]), kl], axis=0)
    nd = SSM_STEPS // 2
    lag = 2 * jnp.arange(nd)[:, None, None] + jnp.arange(2)[None, None, :] - jnp.arange(2)[None, :, None]
    ksel = kl[lag + 1]
    ksel = ksel.reshape(nd, 2, 2, SSM_BLOCKS, GROUPS_PER_BLOCK, SSM_GROUP, SSM_GROUP)
    toep = ksel.transpose(3, 0, 1, 4, 5, 2, 6).reshape(SSM_BLOCKS, nd, 2 * LANES, 2 * SSM_GROUP)
    bbr_t, bbi_t = bbr.transpose(0, 2, 1), bbi.transpose(0, 2, 1)
    wo_r = rev_r[:, :, None, :] * bbr_t[None] - rev_i[:, :, None, :] * bbi_t[None]
    wo_i = rev_r[:, :, None, :] * bbi_t[None] + rev_i[:, :, None, :] * bbr_t[None]
    wo = jnp.stack([wo_r, wo_i]).reshape(2, SSM_STEPS, SSM_BLOCKS, GROUPS_PER_BLOCK, SSM_GROUP, SSM_STATE)
    wout = wo.transpose(2, 1, 3, 4, 0, 5).reshape(SSM_BLOCKS, SSM_STEPS * LANES, 2 * SSM_STATE)
    gr = c_re[None] * pr[1:, :, None, :] - c_im[None] * pi[1:, :, None, :]
    gi = c_re[None] * pi[1:, :, None, :] + c_im[None] * pr[1:, :, None, :]
    wi = jnp.stack([gr, -gi]).reshape(2, SSM_STEPS, SSM_BLOCKS, GROUPS_PER_BLOCK, SSM_GROUP, SSM_STATE)
    win = wi.transpose(2, 0, 3, 5, 1, 4).reshape(SSM_BLOCKS, 2 * STATE_COLS, SSM_STEPS * SSM_GROUP)
    lev = jnp.stack(power(SSM_STEPS * 2.0 ** jnp.arange(8, dtype=f32)[:, None, None]), axis=1)
    lev = lev.reshape(8, 2, SSM_BLOCKS, STATE_COLS).transpose(2, 0, 1, 3)
    return toep.astype(bf16), wout.astype(bf16), win.astype(bf16), lev


def _ssm_chunk_rows(u_ref, nk):
    xs = [u_ref[0, pl.ds(s, nk, stride=SSM_STEPS), :] for s in range(SSM_STEPS)]
    pairs = [jnp.concatenate([xs[2 * p], xs[2 * p + 1]], axis=1).astype(bf16) for p in range(SSM_STEPS // 2)]
    return xs, pairs


def _ssm_intra(pairs, toep_ref, nk):
    nd = len(pairs)
    y = [None] * nd
    for d in range(nd):
        lhs = jnp.concatenate(pairs[:nd - d], axis=0) if nd - d > 1 else pairs[0]
        r = jnp.dot(lhs, toep_ref[d], preferred_element_type=f32)
        for p in range(nd - d):
            blk = r[p * nk:(p + 1) * nk]
            y[p + d] = blk if y[p + d] is None else y[p + d] + blk
    return y


def _shift_rows(x, sh):
    rows = lax.broadcasted_iota(i32, (x.shape[0], 1), 0)
    return jnp.where(rows >= sh, pltpu.roll(x, sh, axis=0), 0.0)


def _spread_matrix(k_in, sub):
    k = lax.broadcasted_iota(i32, (k_in, k_in * GROUPS_PER_BLOCK), 0)
    n = lax.broadcasted_iota(i32, (k_in, k_in * GROUPS_PER_BLOCK), 1)
    shift = int(math.log2(sub))
    same_a = jnp.right_shift(n, shift + int(math.log2(GROUPS_PER_BLOCK))) == jnp.right_shift(k, shift)
    return (same_a & ((n & (sub - 1)) == (k & (sub - 1)))).astype(bf16)


def _group_index(shape, axis, sub, offset=0):
    idx = lax.broadcasted_iota(i32, shape, axis) + offset
    return jnp.right_shift(idx, int(math.log2(sub))) & (GROUPS_PER_BLOCK - 1)


def _expand_block_diag(compact_ref, out_ref, row_sub, col_sub):
    rows, k_in = compact_ref.shape
    spread = _spread_matrix(k_in, col_sub)
    col_group = _group_index((2 * LANES, k_in * GROUPS_PER_BLOCK), 1, col_sub)
    for r0 in range(0, rows, 2 * LANES):
        full = jnp.dot(compact_ref[r0:r0 + 2 * LANES, :], spread, preferred_element_type=f32)
        row_group = _group_index(full.shape, 0, row_sub, r0)
        out_ref[r0:r0 + 2 * LANES, :] = jnp.where(row_group == col_group, full, 0.0).astype(bf16)


def _ssm_chunks(u_ref, d_ref, y_ref, toep_s, wout_s, win_s, entry_state):
    nk = u_ref.shape[1] // SSM_STEPS
    xs, pairs = _ssm_chunk_rows(u_ref, nk)
    y = _ssm_intra(pairs, toep_s, nk)
    s = jnp.dot(jnp.concatenate(pairs, axis=1), wout_s[...], preferred_element_type=f32)
    hprev, hr, hi = entry_state(s[:, :STATE_COLS], s[:, STATE_COLS:])
    y2 = _bdot(hprev, win_s[...])
    for st in range(SSM_STEPS):
        piece = (y[st // 2][:, (st % 2) * LANES:(st % 2 + 1) * LANES] + y2[:, st * LANES:(st + 1) * LANES]
                 + d_ref[...] * xs[st])
        y_ref[0, pl.ds(st, nk, stride=SSM_STEPS), :] = piece
    return hr, hi


def _ssm_body(u_ref, us_ref, h0r_ref, h0i_ref, toep_ref, wout_ref, win_ref, lev_ref, d_ref,
              y_ref, hr_ref, hi_ref, ys_ref, hrs_ref, his_ref, toep_s, wout_s, win_s):
    @pl.when(pl.program_id(1) == 0)
    def _():
        for d in range(SSM_STEPS // 2):
            _expand_block_diag(toep_ref.at[0, d], toep_s.at[d], SSM_GROUP, SSM_GROUP)
        _expand_block_diag(wout_ref.at[0], wout_s, SSM_GROUP, SSM_STATE)
        _expand_block_diag(win_ref.at[0], win_s, SSM_STATE, SSM_GROUP)

        def one_chunk(sr, si):
            h0r, h0i = h0r_ref[0], h0i_ref[0]
            ar, ai = lev_ref[0, 0, 0:1, :], lev_ref[0, 0, 1:2, :]
            return (jnp.concatenate([h0r, h0i], axis=1), sr + ar * h0r - ai * h0i, si + ar * h0i + ai * h0r)
        hrs_ref[0], his_ref[0] = _ssm_chunks(us_ref, d_ref, ys_ref, toep_s, wout_s, win_s, one_chunk)

    def scan_chunks(sr, si):
        nk = sr.shape[0]
        level = 0
        while (1 << level) < nk:
            ar, ai = lev_ref[0, level, 0:1, :], lev_ref[0, level, 1:2, :]
            tr, ti = _shift_rows(sr, 1 << level), _shift_rows(si, 1 << level)
            sr, si = sr + ar * tr - ai * ti, si + ar * ti + ai * tr
            level += 1
        return (jnp.concatenate([_shift_rows(sr, 1), _shift_rows(si, 1)], axis=1), sr[nk - 1:nk], si[nk - 1:nk])
    hr_ref[0, 0], hi_ref[0, 0] = _ssm_chunks(u_ref, d_ref, y_ref, toep_s, wout_s, win_s, scan_chunks)


def _ssm(u, us, h0r, h0i, toep, wout, win, lev, d):
    b, l, _ = u.shape
    rows = us.shape[1]
    nb = rows // SSM_STEPS
    wspec = lambda a: pl.BlockSpec((1,) + a.shape[1:], lambda j, bi: (j,) + (0,) * (a.ndim - 1))
    st = pl.BlockSpec((1, 1, 1, STATE_COLS), lambda j, bi: (bi, j, 0, 0))
    sst = pl.BlockSpec((1, nb, STATE_COLS), lambda j, bi: (j, 0, 0))
    seq = pl.BlockSpec((1, l, LANES), lambda j, bi: (bi, 0, j))
    sseq = pl.BlockSpec((1, rows, LANES), lambda j, bi: (0, 0, j))
    return pl.pallas_call(
        _ssm_body,
        grid=(SSM_BLOCKS, b),
        in_specs=[seq, sseq, sst, sst, wspec(toep), wspec(wout), wspec(win), wspec(lev),
                  pl.BlockSpec((1, LANES), lambda j, bi: (0, j))],
        out_specs=[seq, st, st, sseq, sst, sst],
        out_shape=[jax.ShapeDtypeStruct((b, l, SSM_WIDTH), f32),
                   jax.ShapeDtypeStruct((b, SSM_BLOCKS, 1, STATE_COLS), f32),
                   jax.ShapeDtypeStruct((b, SSM_BLOCKS, 1, STATE_COLS), f32),
                   jax.ShapeDtypeStruct((1, rows, SSM_WIDTH), f32),
                   jax.ShapeDtypeStruct((SSM_BLOCKS, nb, STATE_COLS), f32),
                   jax.ShapeDtypeStruct((SSM_BLOCKS, nb, STATE_COLS), f32)],
        scratch_shapes=[pltpu.VMEM((SSM_STEPS // 2, 2 * LANES, 2 * LANES), bf16),
                        pltpu.VMEM((SSM_STEPS * LANES, 2 * STATE_COLS), bf16),
                        pltpu.VMEM((2 * STATE_COLS, SSM_STEPS * LANES), bf16)],
        compiler_params=_cparams(2),
        name="ssm",
    )(u, us, h0r, h0i, toep, wout, win, lev, d)


ROUTER_COLS = LANES


def _mix_body(ap_ref, as_ref, yp_ref, ys_ref, xp_ref, xs_ref, wglu_ref, bglu_ref, gs_ref, wout_ref, gf_ref, wr_ref,
              br_ref, tri_ref, upper_ref, x1_ref, xl_ref, wts_ref, lpos_ref, n_ref, loff_ref, *, prompt_steps):
    is_prompt = pl.program_id(0) < prompt_steps
    pick_rows = lambda p_ref, s_ref: jnp.where(is_prompt, p_ref[...], jnp.concatenate([s_ref[...]] * MIX_TILES, axis=0))
    y = pick_rows(yp_ref, ys_ref)
    y = 0.5 * y * (1.0 + jnp.tanh(math.sqrt(2.0 / math.pi) * (y + 0.044715 * (y * y * y))))
    y = y * jax.nn.sigmoid(_bdot(y, wglu_ref[...]) + bglu_ref[...])
    attn = jnp.where(is_prompt, ap_ref[...].astype(f32),
                     jnp.concatenate([as_ref[...].astype(f32)] * MIX_TILES, axis=0)).astype(bf16)
    cat = jnp.concatenate([attn, _rms(y, gs_ref[...]).astype(bf16)], axis=1)
    x1 = pick_rows(xp_ref, xs_ref) + jnp.dot(cat, wout_ref[...], preferred_element_type=f32)
    x1_ref[...] = x1
    hf = _rms(x1, gf_ref[...])

    logits = _bdot(hf, wr_ref[...]) + br_ref[...]
    le = logits[:, :N_EXPERTS]
    lg = logits[:, N_EXPERTS:N_EXPERTS + N_EXPERT_GROUPS]
    tm = le.shape[0]
    gmax = jnp.max(lg, axis=-1, keepdims=True)
    gi = lax.broadcasted_iota(i32, (tm, N_EXPERT_GROUPS), 1).astype(f32)
    gsel = jnp.min(jnp.where(lg == gmax, gi, float(N_EXPERT_GROUPS)), axis=-1, keepdims=True)
    pg = 1.0 / jnp.sum(jnp.exp(lg - gmax), axis=-1, keepdims=True)
    ei_int = lax.broadcasted_iota(i32, (tm, N_EXPERTS), 1)
    ei = ei_int.astype(f32)
    egroup = jnp.right_shift(ei_int, int(math.log2(EXPERTS_PER_GROUP))).astype(f32)
    lm = jnp.where(egroup == gsel, le, NEG)
    v1 = jnp.max(lm, axis=-1, keepdims=True)
    i1 = jnp.min(jnp.where(lm == v1, ei, float(N_EXPERTS)), axis=-1, keepdims=True)
    lm2 = jnp.where(ei == i1, NEG, lm)
    v2 = jnp.max(lm2, axis=-1, keepdims=True)
    i2 = jnp.min(jnp.where(lm2 == v2, ei, float(N_EXPERTS)), axis=-1, keepdims=True)
    ex = jnp.exp(v2 - v1)
    wts_ref[...] = jnp.concatenate([pg / (1.0 + ex), pg * ex / (1.0 + ex)], axis=1)

    oh1 = (ei == i1).astype(f32)
    oh2 = (ei == i2).astype(f32)
    hf_bf = hf.astype(bf16)
    sorted_row = lax.broadcasted_iota(i32, (ROW_TILE, SLOTS * ROW_TILE), 1).astype(f32)
    for h in range(MIX_TILES):
        rows = slice(h * ROW_TILE, (h + 1) * ROW_TILE)
        both = (oh1[rows] + oh2[rows]).astype(bf16)
        before = jnp.dot(tri_ref[...], both, preferred_element_type=f32)
        count = jnp.sum(oh1[rows] + oh2[rows], axis=0, keepdims=True)
        lower = jnp.sum(jnp.dot(both, upper_ref[...], preferred_element_type=f32), axis=0, keepdims=True)
        lp1 = jnp.sum(oh1[rows] * (before + lower), axis=-1, keepdims=True)
        lp2 = jnp.sum(oh2[rows] * (before + lower), axis=-1, keepdims=True)
        lpos_ref[rows, :] = jnp.concatenate([lp1, lp2], axis=1)
        n_ref[h] = count
        loff_ref[h] = lower
        pick = ((sorted_row == lp1) | (sorted_row == lp2)).astype(bf16)
        xl = lax.dot_general(pick, hf_bf[rows], (((0,), (0,)), ((), ())), preferred_element_type=f32)
        _store_row_major(xl_ref, h * SLOTS * ROW_TILE, xl)


def _mix(a_p, a_s, ys_p, ys_s, x_p, x_s, wglu, bglu, gs, wout, gf, wr, br, tri, upper):
    tm = MIX_TILES * ROW_TILE
    assert x_p.shape[0] % tm == 0 and x_s.shape[0] == ROW_TILE
    prompt_steps = x_p.shape[0] // tm
    t = x_p.shape[0] + x_s.shape[0]
    p_rows = lambda n: pl.BlockSpec((tm, n), lambda i: (jnp.minimum(i, prompt_steps - 1), 0))
    s_rows = lambda n: pl.BlockSpec((ROW_TILE, n), lambda i: (0, 0))
    row = lambda n: pl.BlockSpec((tm, n), lambda i: (i, 0))
    full = lambda arr: pl.BlockSpec(arr.shape, lambda i: (0,) * arr.ndim)
    per_tile = pl.BlockSpec((MIX_TILES, 1, N_EXPERTS), lambda i: (i, 0, 0))
    return pl.pallas_call(
        functools.partial(_mix_body, prompt_steps=prompt_steps),
        grid=(prompt_steps + 1,),
        in_specs=[p_rows(ATTN_WIDTH), s_rows(ATTN_WIDTH), p_rows(SSM_WIDTH), s_rows(SSM_WIDTH), p_rows(D_MODEL),
                  s_rows(D_MODEL), full(wglu), full(bglu), full(gs), full(wout), full(gf), full(wr), full(br),
                  full(tri), full(upper)],
        out_specs=[row(D_MODEL), pl.BlockSpec((SLOTS * tm * PIECES, LANES), lambda i: (i, 0)), row(2), row(2),
                   per_tile, per_tile],
        out_shape=[jax.ShapeDtypeStruct((t, D_MODEL), f32),
                   jax.ShapeDtypeStruct((SLOTS * t * PIECES, LANES), f32),
                   jax.ShapeDtypeStruct((t, 2), f32), jax.ShapeDtypeStruct((t, 2), f32),
                   jax.ShapeDtypeStruct((t // ROW_TILE, 1, N_EXPERTS), f32),
                   jax.ShapeDtypeStruct((t // ROW_TILE, 1, N_EXPERTS), f32)],
        compiler_params=_cparams(),
        name="mix",
    )(a_p, a_s, ys_p, ys_s, x_p, x_s, wglu, bglu, gs, wout, gf, wr, br, tri, upper)


def _store_row_major(ref, first_row, x):
    for c in range(PIECES):
        ref[pl.ds(first_row * PIECES + c, x.shape[0], stride=PIECES), :] = x[:, c * LANES:(c + 1) * LANES]


def _load_row_major(ref, n_rows):
    return jnp.concatenate([ref[pl.ds(c, n_rows, stride=PIECES), :] for c in range(PIECES)], axis=1)


def _copy_rows(src, s_row, dst, d_row, n_rows, sem):
    return pltpu.make_async_copy(src.at[pl.ds(pl.multiple_of(s_row * PIECES, PIECES), n_rows * PIECES), :],
                                 dst.at[pl.ds(pl.multiple_of(d_row * PIECES, PIECES), n_rows * PIECES), :], sem)


def _for_each_piece(n, fn, pieces=RUN_PIECES):
    off = n & ~(2 * pieces[0] - 1)
    for piece in pieces:
        @pl.when((n & piece) != 0)
        def _(off=off, piece=piece):
            fn(off, piece)
        off = off + (n & piece)


def _experts_body(te_ref, tpos_ref, tvalid_ref, tlo_ref, thi_ref, n_ref, cum_ref, loff_ref,
                  xl_hbm, wg_ref, wu_ref, wd_ref, o_ref, wg_s, wu_s, wd_s, sem, *xbufs):
    i = pl.program_id(0)
    last = pl.num_programs(0) - 1
    token_tiles = n_ref.shape[0] // N_EXPERTS
    ring = len(xbufs)

    def start_run(t, tau, enabled, buf, buf_sem):
        e, lo = te_ref[t], tpos_ref[t]
        k = tau * N_EXPERTS + e
        s, n = cum_ref[k], n_ref[k]
        a = jnp.maximum(s, lo)
        length = jnp.where(enabled, jnp.maximum(jnp.minimum(s + n, lo + ROW_TILE) - a, 0), 0)
        local = loff_ref[k] + (a - s)
        _for_each_piece(length, lambda off, piece: _copy_rows(
            xl_hbm, tau * (SLOTS * ROW_TILE) + local + off, buf, a - lo + off, piece, buf_sem).start())

    def start_runs_loop(t, first, stop, buf, buf_sem):
        def run(tau, c):
            start_run(t, tau, True, buf, buf_sem)
            return c
        lax.fori_loop(first, stop, run, 0)

    def clear(buf):
        buf[...] = jnp.zeros_like(buf)

    @pl.when(i == 0)
    def _():
        for buf in xbufs:
            clear(buf)
        for t in range(ring - 1):
            @pl.when(tvalid_ref[t] > 0)
            def _(t=t):
                start_runs_loop(t, tlo_ref[t], thi_ref[t] + 1, xbufs[t], sem.at[t])

    @pl.when((i == 0) | (te_ref[i] != te_ref[jnp.maximum(i - 1, 0)]))
    def _():
        wg_s[...] = wg_ref[0].astype(bf16)
        wu_s[...] = wu_ref[0].astype(bf16)
        wd_s[...] = wd_ref[0].astype(bf16)

    valid = tvalid_ref[i]

    def tile_step(cur):
        ahead = (cur + ring - 1) % ring
        buf, buf_sem, next_buf, next_sem = xbufs[cur], sem.at[cur], xbufs[ahead], sem.at[ahead]
        _for_each_piece(valid, lambda off, piece: _copy_rows(
            xl_hbm, 0, buf, 0, piece, buf_sem).wait())

        nxt = jnp.minimum(i + ring - 1, last)
        go = (i + ring - 1 <= last) & (tvalid_ref[nxt] > 0)
        first, final = tlo_ref[nxt], thi_ref[nxt]
        for j in range(UNROLLED_RUNS):
            start_run(nxt, jnp.minimum(first + j, token_tiles - 1), go & (first + j <= final), next_buf, next_sem)

        x = _load_row_major(buf, ROW_TILE).astype(bf16)
        clear(buf)
        hg = jnp.dot(x, wg_s[...], preferred_element_type=f32)
        hu = jnp.dot(x, wu_s[...], preferred_element_type=f32)
        y = jnp.dot((hg * jax.nn.sigmoid(hg) * hu).astype(bf16), wd_s[...], preferred_element_type=f32)
        _store_row_major(o_ref, 0, y)

        @pl.when(go & (final - first >= UNROLLED_RUNS))
        def _():
            start_runs_loop(nxt, first + UNROLLED_RUNS, final + 1, next_buf, next_sem)

    for cur in range(ring):
        pl.when((valid > 0) & (i % ring == cur))(functools.partial(tile_step, cur))

    @pl.when(valid == 0)
    def _():
        o_ref[...] = jnp.zeros_like(o_ref)


def _experts(tables, xl, wg, wu, wd, tiles):
    wspec = lambda a: pl.BlockSpec((1,) + a.shape[1:], lambda i, te, *_: (te[i], 0, 0))
    return pl.pallas_call(
        _experts_body,
        grid_spec=pltpu.PrefetchScalarGridSpec(
            num_scalar_prefetch=len(tables),
            grid=(tiles,),
            in_specs=[pl.BlockSpec(memory_space=pl.ANY), wspec(wg), wspec(wu), wspec(wd)],
            out_specs=pl.BlockSpec((ROW_TILE * PIECES, LANES), lambda i, *_: (i, 0)),
            scratch_shapes=[pltpu.VMEM(wg.shape[1:], bf16), pltpu.VMEM(wu.shape[1:], bf16), pltpu.VMEM(wd.shape[1:], bf16),
                            pltpu.SemaphoreType.DMA((EXPERT_RING,))]
            + [pltpu.VMEM((ROW_TILE * PIECES, LANES), f32)] * EXPERT_RING),
        out_shape=jax.ShapeDtypeStruct((tiles * ROW_TILE * PIECES, LANES), f32),
        compiler_params=_cparams(),
        name="moe_experts",
    )(*tables, xl, wg, wu, wd)


def _combine_body(n_ref, gpos_ref, loff_ref, ys_hbm, x_ref, w_ref, l_ref, g_ref, op_ref, os_ref, sem, *ybufs,
                  prompt_tiles):
    i = pl.program_id(0)
    last = pl.num_programs(0) - 1
    tile_rows = SLOTS * ROW_TILE
    ring = len(ybufs)

    def start_run(t, e, enabled, buf, buf_sem, pieces=RUN_PIECES):
        k = t * N_EXPERTS + e
        _for_each_piece(jnp.where(enabled, n_ref[k], 0), lambda off, piece: _copy_rows(
            ys_hbm, gpos_ref[k] + off, buf, loff_ref[k] + off, piece, buf_sem).start(), pieces)

    @pl.when(i == 0)
    def _():
        for t in range(ring - 1):
            def run(e, c, t=t):
                start_run(t, e, True, ybufs[t], sem.at[t])
                return c
            lax.fori_loop(0, N_EXPERTS, run, 0)

    def step(cur):
        ahead = (cur + ring - 1) % ring
        buf, buf_sem, next_buf, next_sem = ybufs[cur], sem.at[cur], ybufs[ahead], sem.at[ahead]
        _copy_rows(ys_hbm, 0, buf, 0, tile_rows, buf_sem).wait()
        nxt, go = jnp.minimum(i + ring - 1, last), i + ring - 1 <= last
        longest = 0
        for e in range(N_EXPERTS):
            start_run(nxt, e, go, next_buf, next_sem, SHORT_PIECES)
            longest = longest | n_ref[nxt * N_EXPERTS + e]
        yl = _load_row_major(buf, tile_rows).astype(bf16)
        sorted_row = lax.broadcasted_iota(i32, (ROW_TILE, tile_rows), 1).astype(f32)
        w, lp = w_ref[...], l_ref[...]
        y1 = jnp.dot((sorted_row == lp[:, 0:1]).astype(bf16), yl, preferred_element_type=f32)
        y2 = jnp.dot((sorted_row == lp[:, 1:2]).astype(bf16), yl, preferred_element_type=f32)
        out = _rms(x_ref[...] + (w[:, 0:1] * y1 + w[:, 1:2] * y2), g_ref[...])

        @pl.when(i < prompt_tiles)
        def _():
            op_ref[...] = out

        @pl.when(i >= prompt_tiles)
        def _():
            os_ref[...] = out

        @pl.when(go & (longest >= 2 * SHORT_PIECES[0]))
        def _():
            def long_pieces(e, c):
                start_run(nxt, e, True, next_buf, next_sem, LONG_PIECES)
                return c
            lax.fori_loop(0, N_EXPERTS, long_pieces, 0)

    for cur in range(ring):
        pl.when(i % ring == cur)(functools.partial(step, cur))


def _combine(tables, ys, x1, wts, lpos, g, prompt_rows):
    prompt_tiles = prompt_rows // ROW_TILE
    tiles = x1.shape[0] // ROW_TILE
    row = lambda n: pl.BlockSpec((ROW_TILE, n), lambda i, *_: (i, 0))
    p_rows = lambda n: pl.BlockSpec((ROW_TILE, n), lambda i, *_: (jnp.minimum(i, prompt_tiles - 1), 0))
    s_rows = lambda n: pl.BlockSpec((ROW_TILE, n), lambda i, *_: (jnp.maximum(i - prompt_tiles, 0), 0))
    return pl.pallas_call(
        functools.partial(_combine_body, prompt_tiles=prompt_tiles),
        grid_spec=pltpu.PrefetchScalarGridSpec(
            num_scalar_prefetch=len(tables),
            grid=(tiles,),
            in_specs=[pl.BlockSpec(memory_space=pl.ANY), row(D_MODEL), row(2), row(2),
                      pl.BlockSpec((1, D_MODEL), lambda i, *_: (0, 0))],
            out_specs=[p_rows(D_MODEL), s_rows(D_MODEL)],
            scratch_shapes=[pltpu.SemaphoreType.DMA((COMBINE_RING,))]
            + [pltpu.VMEM((SLOTS * ROW_TILE * PIECES, LANES), f32)] * COMBINE_RING),
        out_shape=[jax.ShapeDtypeStruct((prompt_rows, D_MODEL), f32),
                   jax.ShapeDtypeStruct((x1.shape[0] - prompt_rows, D_MODEL), f32)],
        compiler_params=_cparams(),
        name="moe_combine",
    )(*tables, ys, x1, wts, lpos, g)


def _moe_tables(n, loff, tiles):
    cum = jnp.cumsum(n, axis=0) - n
    counts = jnp.sum(n, axis=0)
    padded = (counts + ROW_TILE - 1) // ROW_TILE * ROW_TILE
    ends = jnp.cumsum(padded)
    starts = ends - padded
    first = jnp.arange(tiles, dtype=i32) * ROW_TILE
    expert = jnp.minimum(jnp.sum((first[:, None] >= ends[None, :]).astype(i32), axis=1), N_EXPERTS - 1)
    sel = expert[:, None] == jnp.arange(N_EXPERTS)[None, :]
    pick = lambda v: jnp.sum(jnp.where(sel, v[None, :], 0), axis=1)
    pos = first - pick(starts)
    valid = jnp.where(first < ends[-1], jnp.clip(pick(counts) - pos, 0, ROW_TILE), 0)
    cum_t, n_t = cum.T[expert], n.T[expert]
    touches = (cum_t + n_t > pos[:, None]) & (cum_t < (pos + ROW_TILE)[:, None]) & (n_t > 0)
    tau = jnp.arange(n.shape[0], dtype=i32)[None, :]
    lo = jnp.min(jnp.where(touches, tau, n.shape[0]), axis=1)
    hi = jnp.max(jnp.where(touches, tau, -1), axis=1)
    as_i32 = lambda v: v.astype(i32)
    flat = lambda v: v.reshape(-1).astype(i32)
    expert_tables = tuple(map(as_i32, (expert, pos, valid, lo, hi))) + (flat(n), flat(cum), flat(loff))
    combine_tables = (flat(n), flat(starts[None, :] + cum), flat(loff))
    return expert_tables, combine_tables


def kernel(x_prompt, x_sample, cache_k, cache_v, state_ssm_re, state_ssm_im, g_norm_mix, w_in, attn_sinks, ssm_a_re,
           ssm_a_im, ssm_log_dt, ssm_b_re, ssm_b_im, ssm_c_re, ssm_c_im, ssm_d, w_glu, b_glu, g_attn_out, g_ssm_out,
           w_out, g_norm_ffn, w_router_group, b_router_group, w_router_expert, b_router_expert, w_exp_gate, w_exp_up,
           w_exp_down, g_final):
    bp, lp, _ = x_prompt.shape
    bs, ls, _ = x_sample.shape
    depth = w_in.shape[0]
    assert depth == 1 and ls == SSM_STEPS and lp % ATTN_TILE == 0 and (bs * ls) % ROW_TILE == 0
    tp, ts = bp * lp, bs * ls
    wc = cache_k.shape[2]
    row2 = lambda v: v.reshape(1, -1)

    xp = x_prompt.reshape(tp, D_MODEL)
    xs = x_sample.reshape(ts, D_MODEL)
    w_in_bf = w_in[0].astype(bf16)
    qp, kp, vp, up = _proj(xp, row2(g_norm_mix[0]), w_in_bf, 512)
    qs, kq, vq, us = _proj(xs, row2(g_norm_mix[0]), w_in_bf, ts)

    sinks = attn_sinks[0]
    g_att = row2(g_attn_out[0])
    ap = _attn_prompt(qp.reshape(bp, lp, -1), kp.reshape(bp, lp, -1), vp.reshape(bp, lp, -1), sinks, g_att)
    a_s, k_roll, v_roll = _attn_sample(qs.reshape(bs, ls, -1), kq.reshape(bs, ls, -1), vq.reshape(bs, ls, -1),
                                       cache_k[0].reshape(bs, wc, KV_WIDTH), cache_v[0].reshape(bs, wc, KV_WIDTH),
                                       sinks, g_att)

    toep, wso, wsi, lev = _ssm_weights(ssm_a_re[0], ssm_a_im[0], ssm_log_dt[0], ssm_b_re[0], ssm_b_im[0],
                                       ssm_c_re[0], ssm_c_im[0])
    d_row = row2(ssm_d[0])
    to_blocks = lambda h: h.reshape(bs, SSM_BLOCKS, STATE_COLS).transpose(1, 0, 2)
    from_blocks = lambda h: h.transpose(1, 0, 2).reshape(bs, SSM_GROUPS, SSM_STATE)
    yp, hrp, hip, ysm, hrs, his = _ssm(up.reshape(bp, lp, -1), us.reshape(1, ts, -1), to_blocks(state_ssm_re[0]),
                                       to_blocks(state_ssm_im[0]), toep, wso, wsi, lev, d_row)

    wr = jnp.zeros((D_MODEL, ROUTER_COLS), f32)
    wr = wr.at[:, :N_EXPERTS].set(w_router_expert[0]).at[:, N_EXPERTS:N_EXPERTS + N_EXPERT_GROUPS].set(w_router_group[0])
    br = jnp.zeros((1, ROUTER_COLS), f32)
    br = br.at[0, :N_EXPERTS].set(b_router_expert[0]).at[0, N_EXPERTS:N_EXPERTS + N_EXPERT_GROUPS].set(b_router_group[0])
    tri = jnp.tril(jnp.ones((ROW_TILE, ROW_TILE), bf16), -1)
    upper = jnp.triu(jnp.ones((N_EXPERTS, N_EXPERTS), bf16), 1)
    mix_w = (w_glu[0].astype(bf16), row2(b_glu[0]), row2(g_ssm_out[0]), w_out[0].astype(bf16), row2(g_norm_ffn[0]),
             wr.astype(bf16), br, tri, upper)
    x1, xl, wts, lpos, n_rows, n_off = _mix(ap.reshape(tp, -1), a_s.reshape(ts, -1), yp.reshape(tp, -1),
                                            ysm.reshape(ts, -1), xp, xs, *mix_w)

    per_tile = lambda v: v.reshape(-1, N_EXPERTS).astype(i32)
    tiles = (SLOTS * (tp + ts)) // ROW_TILE + N_EXPERTS
    expert_tables, combine_tables = _moe_tables(per_tile(n_rows), per_tile(n_off), tiles)
    expert_out = _experts(expert_tables, xl, w_exp_gate[0], w_exp_up[0], w_exp_down[0], tiles)
    y_p, y_s = _combine(combine_tables, expert_out, x1, wts, lpos, row2(g_final), tp)

    kvshape = lambda a, b: a.reshape(1, b, -1, N_KV_HEADS, HEAD_DIM)
    block_state = lambda h: h.reshape(bp, SSM_GROUPS, SSM_STATE)[None]
    wcp = min(WINDOW, lp)
    return (y_p.reshape(bp, lp, D_MODEL), y_s.reshape(bs, ls, D_MODEL),
            kvshape(kp.reshape(bp, lp, -1)[:, lp - wcp:], bp), kvshape(vp.reshape(bp, lp, -1)[:, lp - wcp:], bp),
            block_state(hrp), block_state(hip),
            kvshape(k_roll, bs), kvshape(v_roll, bs),
            from_blocks(hrs)[None], from_blocks(his)[None])
```

```python
import functools
import math

import jax
import jax.numpy as jnp
from jax import lax
from jax.experimental import pallas as pl
from jax.experimental.pallas import tpu as pltpu

f32, bf16, i32 = jnp.float32, jnp.bfloat16, jnp.int32

D_MODEL = 1024
CHUNK = 64
N_BACK = 2
WINDOW = 128
ATTN_WIDTH = 512
HEAD_DIM = 64
N_KV_HEADS = 2
Q_PER_KV = 4
KV_WIDTH = 128
SSM_WIDTH = 512
SSM_GROUP = 16
SSM_GROUPS = 32
SSM_STATE = 64
PROJ_WIDTH = 1280
N_EXPERT_GROUPS = 4
EXPERTS_PER_GROUP = 8
N_EXPERTS = 32
D_EXPERT = 512
EPS = 1e-6
NEG = -1e30

LANES = 128
SSM_STEPS = 16
SSM_BLOCKS = SSM_WIDTH // LANES
GROUPS_PER_BLOCK = LANES // SSM_GROUP
STATE_COLS = GROUPS_PER_BLOCK * SSM_STATE
ROW_TILE = 256
SLOTS = 2
PIECES = D_MODEL // LANES
RUN_PIECES = tuple(1 << b for b in reversed(range(int(math.log2(ROW_TILE)) + 1)))
SHORT_PIECES = tuple(p for p in RUN_PIECES if p <= 32)
LONG_PIECES = tuple(p for p in RUN_PIECES if p > 32)
MIX_TILES = 2
UNROLLED_RUNS = 20
EXPERT_RING = 3
COMBINE_RING = 3
VMEM_LIMIT = 56 * 1024 * 1024


def _cparams(n_axes=1, limit=VMEM_LIMIT):
    return pltpu.CompilerParams(dimension_semantics=("arbitrary",) * n_axes, vmem_limit_bytes=limit)


def _rms(x, g):
    return x * lax.rsqrt(jnp.mean(x * x, axis=-1, keepdims=True) + EPS) * g


def _bdot(a, b):
    return jnp.dot(a.astype(bf16), b.astype(bf16), preferred_element_type=f32)


def _proj_body(x_ref, g_ref, w_ref, q_ref, k_ref, v_ref, u_ref):
    h = _rms(x_ref[...], g_ref[...])
    z = _bdot(h, w_ref[...])
    q_ref[...] = z[:, :ATTN_WIDTH] * (HEAD_DIM ** -0.5)
    k_ref[...] = z[:, ATTN_WIDTH:ATTN_WIDTH + KV_WIDTH]
    v_ref[...] = z[:, ATTN_WIDTH + KV_WIDTH:ATTN_WIDTH + 2 * KV_WIDTH]
    u_ref[...] = z[:, ATTN_WIDTH + 2 * KV_WIDTH:]


def _proj(x2d, g, w_bf, tm):
    t = x2d.shape[0]
    row = lambda n: pl.BlockSpec((tm, n), lambda i: (i, 0))
    full = lambda a: pl.BlockSpec(a.shape, lambda i: (0,) * a.ndim)
    return pl.pallas_call(
        _proj_body,
        grid=(t // tm,),
        in_specs=[row(D_MODEL), full(g), full(w_bf)],
        out_specs=[row(ATTN_WIDTH), row(KV_WIDTH), row(KV_WIDTH), row(SSM_WIDTH)],
        out_shape=[jax.ShapeDtypeStruct((t, n), f32) for n in (ATTN_WIDTH, KV_WIDTH, KV_WIDTH, SSM_WIDTH)],
        compiler_params=_cparams(),
        name="proj",
    )(x2d, g, w_bf)


def _sink_column(sink_ref, kv, rows_per_head):
    r = lax.broadcasted_iota(i32, (Q_PER_KV * rows_per_head, 1), 0)
    col = jnp.full((Q_PER_KV * rows_per_head, 1), sink_ref[kv * Q_PER_KV], f32)
    for j in range(1, Q_PER_KV):
        col = jnp.where(r >= j * rows_per_head, sink_ref[kv * Q_PER_KV + j], col)
    return col


def _attend(qs, kc, vc, sink_col, valid):
    s = lax.dot_general(qs.astype(bf16), kc.astype(bf16), (((1,), (1,)), ((), ())), preferred_element_type=f32)
    if valid is not None:
        s = jnp.where(valid, s, NEG)
    m = jnp.maximum(jnp.max(s, axis=-1, keepdims=True), sink_col)
    p = jnp.exp(s - m)
    denom = jnp.sum(p, axis=-1, keepdims=True) + jnp.exp(sink_col - m)
    return _bdot(p, vc) / denom


def _heads_attend(q, k, v, sink_ref, valid):
    rows = q.shape[0]
    pieces = []
    for kv in range(N_KV_HEADS):
        qs = jnp.concatenate(
            [q[:, (kv * Q_PER_KV + j) * HEAD_DIM:(kv * Q_PER_KV + j + 1) * HEAD_DIM] for j in range(Q_PER_KV)], axis=0)
        o = _attend(qs, k[:, kv * HEAD_DIM:(kv + 1) * HEAD_DIM], v[:, kv * HEAD_DIM:(kv + 1) * HEAD_DIM],
                    _sink_column(sink_ref, kv, rows), valid)
        pieces += [o[j * rows:(j + 1) * rows] for j in range(Q_PER_KV)]
    return jnp.concatenate(pieces, axis=1)


ATTN_TILE = 256
CHUNKS_PER_TILE = ATTN_TILE // CHUNK
KEY_SPAN = (N_BACK + 1) * CHUNK


def _attn_prompt_body(sink_ref, q_ref, kp_ref, kc_ref, vp_ref, vc_ref, g_ref, o_ref):
    i = pl.program_id(1)
    kwin = jnp.concatenate([kp_ref[0], kc_ref[0]], axis=0)
    vwin = jnp.concatenate([vp_ref[0], vc_ref[0]], axis=0)
    key_chunk = lax.broadcasted_iota(i32, (1, KEY_SPAN), 1) // CHUNK
    for c in range(CHUNKS_PER_TILE):
        valid = (i * CHUNKS_PER_TILE + c - N_BACK + key_chunk) >= 0
        o = _heads_attend(q_ref[0, c * CHUNK:(c + 1) * CHUNK, :], kwin[c * CHUNK:c * CHUNK + KEY_SPAN],
                          vwin[c * CHUNK:c * CHUNK + KEY_SPAN], sink_ref, valid)
        o_ref[0, c * CHUNK:(c + 1) * CHUNK, :] = _rms(o, g_ref[...]).astype(bf16)


def _attn_prompt(q, k, v, sinks, g):
    b, l, _ = q.shape
    back = N_BACK * CHUNK
    per = ATTN_TILE // back
    prev = pl.BlockSpec((1, back, KV_WIDTH), lambda bi, i: (bi, jnp.maximum(i * per - 1, 0), 0))
    cur = pl.BlockSpec((1, ATTN_TILE, KV_WIDTH), lambda bi, i: (bi, i, 0))
    return pl.pallas_call(
        _attn_prompt_body,
        grid=(b, l // ATTN_TILE),
        in_specs=[pl.BlockSpec(memory_space=pltpu.SMEM),
                  pl.BlockSpec((1, ATTN_TILE, ATTN_WIDTH), lambda bi, i: (bi, i, 0)),
                  prev, cur, prev, cur,
                  pl.BlockSpec((1, ATTN_WIDTH), lambda bi, i: (0, 0))],
        out_specs=pl.BlockSpec((1, ATTN_TILE, ATTN_WIDTH), lambda bi, i: (bi, i, 0)),
        out_shape=jax.ShapeDtypeStruct((b, l, ATTN_WIDTH), bf16),
        compiler_params=_cparams(2),
        name="attn_prompt",
    )(sinks, q, k, k, v, v, g)


def _attn_sample_body(sink_ref, q_ref, kn_ref, vn_ref, ck_ref, cv_ref, g_ref, o_ref, nk_ref, nv_ref):
    kall = jnp.concatenate([ck_ref[0], kn_ref[0]], axis=0)
    vall = jnp.concatenate([cv_ref[0], vn_ref[0]], axis=0)
    o = _heads_attend(q_ref[0], kall, vall, sink_ref, None)
    o_ref[0] = _rms(o, g_ref[...]).astype(bf16)
    n_new = kn_ref.shape[1]
    nk_ref[0] = kall[n_new:]
    nv_ref[0] = vall[n_new:]


def _attn_sample(q, k_new, v_new, cache_k, cache_v, sinks, g):
    b, l, _ = q.shape
    wc = cache_k.shape[1]
    blk = lambda r, n: pl.BlockSpec((1, r, n), lambda bi: (bi, 0, 0))
    return pl.pallas_call(
        _attn_sample_body,
        grid=(b,),
        in_specs=[pl.BlockSpec(memory_space=pltpu.SMEM), blk(l, ATTN_WIDTH), blk(l, KV_WIDTH), blk(l, KV_WIDTH),
                  blk(wc, KV_WIDTH), blk(wc, KV_WIDTH), pl.BlockSpec((1, ATTN_WIDTH), lambda bi: (0, 0))],
        out_specs=[blk(l, ATTN_WIDTH), blk(wc, KV_WIDTH), blk(wc, KV_WIDTH)],
        out_shape=[jax.ShapeDtypeStruct((b, l, ATTN_WIDTH), bf16),
                   jax.ShapeDtypeStruct((b, wc, KV_WIDTH), f32), jax.ShapeDtypeStruct((b, wc, KV_WIDTH), f32)],
        compiler_params=_cparams(),
        name="attn_sample",
    )(sinks, q, k_new, v_new, cache_k, cache_v, g)


def _ssm_weights(a_re, a_im, log_dt, b_re, b_im, c_re, c_im):
    dt = jnp.exp(log_dt)[:, None]
    lam_r, lam_i = a_re * dt, a_im * dt

    def power(n):
        mag = jnp.exp(n * lam_r)
        return mag * jnp.cos(n * lam_i), mag * jnp.sin(n * lam_i)

    ar, ai = power(1.0)
    den = a_re * a_re + a_im * a_im
    nr, ni = ar - 1.0, ai
    fr = ((nr * a_re + ni * a_im) / den)[..., None]
    fi = ((ni * a_re - nr * a_im) / den)[..., None]
    bbr, bbi = fr * b_re - fi * b_im, fr * b_im + fi * b_re
    steps = jnp.arange(SSM_STEPS + 1, dtype=f32)[:, None, None]
    pr, pi = power(steps)
    rev_r, rev_i = power(SSM_STEPS - 1.0 - steps[:SSM_STEPS])
    c_re_p, c_im_p = c_re.transpose(2, 0, 1), c_im.transpose(2, 0, 1)
    pr_p, pi_p = pr[:SSM_STEPS].transpose(0, 2, 1)[..., None], pi[:SSM_STEPS].transpose(0, 2, 1)[..., None]
    cpr = c_re_p[None] * pr_p - c_im_p[None] * pi_p
    cpi = c_re_p[None] * pi_p + c_im_p[None] * pr_p
    bbr_p, bbi_p = bbr.transpose(1, 0, 2)[None, ..., None], bbi.transpose(1, 0, 2)[None, ..., None]
    kl = jnp.sum(cpr[:, :, :, None, :] * bbr_p - cpi[:, :, :, None, :] * bbi_p, axis=1)
    kl = jnp.concatenate([jnp.zeros_like(kl[:1]), kl], axis=0)
    nd = SSM_STEPS // 2
    lag = 2 * jnp.arange(nd)[:, None, None] + jnp.arange(2)[None, None, :] - jnp.arange(2)[None, :, None]
    ksel = kl[lag + 1]
    ksel = ksel.reshape(nd, 2, 2, SSM_BLOCKS, GROUPS_PER_BLOCK, SSM_GROUP, SSM_GROUP)
    toep = ksel.transpose(3, 0, 1, 4, 5, 2, 6).reshape(SSM_BLOCKS, nd, 2 * LANES, 2 * SSM_GROUP)
    bbr_t, bbi_t = bbr.transpose(0, 2, 1), bbi.transpose(0, 2, 1)
    wo_r = rev_r[:, :, None, :] * bbr_t[None] - rev_i[:, :, None, :] * bbi_t[None]
    wo_i = rev_r[:, :, None, :] * bbi_t[None] + rev_i[:, :, None, :] * bbr_t[None]
    wo = jnp.stack([wo_r, wo_i]).reshape(2, SSM_STEPS, SSM_BLOCKS, GROUPS_PER_BLOCK, SSM_GROUP, SSM_STATE)
    wout = wo.transpose(2, 1, 3, 4, 0, 5).reshape(SSM_BLOCKS, SSM_STEPS * LANES, 2 * SSM_STATE)
    gr = c_re[None] * pr[1:, :, None, :] - c_im[None] * pi[1:, :, None, :]
    gi = c_re[None] * pi[1:, :, None, :] + c_im[None] * pr[1:, :, None, :]
    wi = jnp.stack([gr, -gi]).reshape(2, SSM_STEPS, SSM_BLOCKS, GROUPS_PER_BLOCK, SSM_GROUP, SSM_STATE)
    win = wi.transpose(2, 0, 3, 5, 1, 4).reshape(SSM_BLOCKS, 2 * STATE_COLS, SSM_STEPS * SSM_GROUP)
    lev = jnp.stack(power(SSM_STEPS * 2.0 ** jnp.arange(8, dtype=f32)[:, None, None]), axis=1)
    lev = lev.reshape(8, 2, SSM_BLOCKS, STATE_COLS).transpose(2, 0, 1, 3)
    return toep.astype(bf16), wout.astype(bf16), win.astype(bf16), lev


def _ssm_chunk_rows(u_ref, nk):
    xs = [u_ref[0, pl.ds(s, nk, stride=SSM_STEPS), :] for s in range(SSM_STEPS)]
    pairs = [jnp.concatenate([xs[2 * p], xs[2 * p + 1]], axis=1).astype(bf16) for p in range(SSM_STEPS // 2)]
    return xs, pairs


def _ssm_intra(pairs, toep_ref, nk):
    nd = len(pairs)
    y = [None] * nd
    for d in range(nd):
        lhs = jnp.concatenate(pairs[:nd - d], axis=0) if nd - d > 1 else pairs[0]
        r = jnp.dot(lhs, toep_ref[d], preferred_element_type=f32)
        for p in range(nd - d):
            blk = r[p * nk:(p + 1) * nk]
            y[p + d] = blk if y[p + d] is None else y[p + d] + blk
    return y


def _shift_rows(x, sh):
    rows = lax.broadcasted_iota(i32, (x.shape[0], 1), 0)
    return jnp.where(rows >= sh, pltpu.roll(x, sh, axis=0), 0.0)


def _spread_matrix(k_in, sub):
    k = lax.broadcasted_iota(i32, (k_in, k_in * GROUPS_PER_BLOCK), 0)
    n = lax.broadcasted_iota(i32, (k_in, k_in * GROUPS_PER_BLOCK), 1)
    shift = int(math.log2(sub))
    same_a = jnp.right_shift(n, shift + int(math.log2(GROUPS_PER_BLOCK))) == jnp.right_shift(k, shift)
    return (same_a & ((n & (sub - 1)) == (k & (sub - 1)))).astype(bf16)


def _group_index(shape, axis, sub, offset=0):
    idx = lax.broadcasted_iota(i32, shape, axis) + offset
    return jnp.right_shift(idx, int(math.log2(sub))) & (GROUPS_PER_BLOCK - 1)


def _expand_block_diag(compact_ref, out_ref, row_sub, col_sub):
    rows, k_in = compact_ref.shape
    spread = _spread_matrix(k_in, col_sub)
    col_group = _group_index((2 * LANES, k_in * GROUPS_PER_BLOCK), 1, col_sub)
    for r0 in range(0, rows, 2 * LANES):
        full = jnp.dot(compact_ref[r0:r0 + 2 * LANES, :], spread, preferred_element_type=f32)
        row_group = _group_index(full.shape, 0, row_sub, r0)
        out_ref[r0:r0 + 2 * LANES, :] = jnp.where(row_group == col_group, full, 0.0).astype(bf16)


def _ssm_chunks(u_ref, d_ref, y_ref, toep_s, wout_s, win_s, entry_state):
    nk = u_ref.shape[1] // SSM_STEPS
    xs, pairs = _ssm_chunk_rows(u_ref, nk)
    y = _ssm_intra(pairs, toep_s, nk)
    s = jnp.dot(jnp.concatenate(pairs, axis=1), wout_s[...], preferred_element_type=f32)
    hprev, hr, hi = entry_state(s[:, :STATE_COLS], s[:, STATE_COLS:])
    y2 = _bdot(hprev, win_s[...])
    for st in range(SSM_STEPS):
        piece = (y[st // 2][:, (st % 2) * LANES:(st % 2 + 1) * LANES] + y2[:, st * LANES:(st + 1) * LANES]
                 + d_ref[...] * xs[st])
        y_ref[0, pl.ds(st, nk, stride=SSM_STEPS), :] = piece
    return hr, hi


def _ssm_body(u_ref, us_ref, h0r_ref, h0i_ref, toep_ref, wout_ref, win_ref, lev_ref, d_ref,
              y_ref, hr_ref, hi_ref, ys_ref, hrs_ref, his_ref, toep_s, wout_s, win_s):
    @pl.when(pl.program_id(1) == 0)
    def _():
        for d in range(SSM_STEPS // 2):
            _expand_block_diag(toep_ref.at[0, d], toep_s.at[d], SSM_GROUP, SSM_GROUP)
        _expand_block_diag(wout_ref.at[0], wout_s, SSM_GROUP, SSM_STATE)
        _expand_block_diag(win_ref.at[0], win_s, SSM_STATE, SSM_GROUP)

        def one_chunk(sr, si):
            h0r, h0i = h0r_ref[0], h0i_ref[0]
            ar, ai = lev_ref[0, 0, 0:1, :], lev_ref[0, 0, 1:2, :]
            return (jnp.concatenate([h0r, h0i], axis=1), sr + ar * h0r - ai * h0i, si + ar * h0i + ai * h0r)
        hrs_ref[0], his_ref[0] = _ssm_chunks(us_ref, d_ref, ys_ref, toep_s, wout_s, win_s, one_chunk)

    def scan_chunks(sr, si):
        nk = sr.shape[0]
        level = 0
        while (1 << level) < nk:
            ar, ai = lev_ref[0, level, 0:1, :], lev_ref[0, level, 1:2, :]
            tr, ti = _shift_rows(sr, 1 << level), _shift_rows(si, 1 << level)
            sr, si = sr + ar * tr - ai * ti, si + ar * ti + ai * tr
            level += 1
        return (jnp.concatenate([_shift_rows(sr, 1), _shift_rows(si, 1)], axis=1), sr[nk - 1:nk], si[nk - 1:nk])
    hr_ref[0, 0], hi_ref[0, 0] = _ssm_chunks(u_ref, d_ref, y_ref, toep_s, wout_s, win_s, scan_chunks)


def _ssm(u, us, h0r, h0i, toep, wout, win, lev, d):
    b, l, _ = u.shape
    rows = us.shape[1]
    nb = rows // SSM_STEPS
    wspec = lambda a: pl.BlockSpec((1,) + a.shape[1:], lambda j, bi: (j,) + (0,) * (a.ndim - 1))
    st = pl.BlockSpec((1, 1, 1, STATE_COLS), lambda j, bi: (bi, j, 0, 0))
    sst = pl.BlockSpec((1, nb, STATE_COLS), lambda j, bi: (j, 0, 0))
    seq = pl.BlockSpec((1, l, LANES), lambda j, bi: (bi, 0, j))
    sseq = pl.BlockSpec((1, rows, LANES), lambda j, bi: (0, 0, j))
    return pl.pallas_call(
        _ssm_body,
        grid=(SSM_BLOCKS, b),
        in_specs=[seq, sseq, sst, sst, wspec(toep), wspec(wout), wspec(win), wspec(lev),
                  pl.BlockSpec((1, LANES), lambda j, bi: (0, j))],
        out_specs=[seq, st, st, sseq, sst, sst],
        out_shape=[jax.ShapeDtypeStruct((b, l, SSM_WIDTH), f32),
                   jax.ShapeDtypeStruct((b, SSM_BLOCKS, 1, STATE_COLS), f32),
                   jax.ShapeDtypeStruct((b, SSM_BLOCKS, 1, STATE_COLS), f32),
                   jax.ShapeDtypeStruct((1, rows, SSM_WIDTH), f32),
                   jax.ShapeDtypeStruct((SSM_BLOCKS, nb, STATE_COLS), f32),
                   jax.ShapeDtypeStruct((SSM_BLOCKS, nb, STATE_COLS), f32)],
        scratch_shapes=[pltpu.VMEM((SSM_STEPS // 2, 2 * LANES, 2 * LANES), bf16),
                        pltpu.VMEM((SSM_STEPS * LANES, 2 * STATE_COLS), bf16),
                        pltpu.VMEM((2 * STATE_COLS, SSM_STEPS * LANES), bf16)],
        compiler_params=_cparams(2),
        name="ssm",
    )(u, us, h0r, h0i, toep, wout, win, lev, d)


ROUTER_COLS = LANES


def _mix_body(ap_ref, as_ref, yp_ref, ys_ref, xp_ref, xs_ref, wglu_ref, bglu_ref, gs_ref, wout_ref, gf_ref, wr_ref,
              br_ref, tri_ref, upper_ref, x1_ref, xl_ref, wts_ref, lpos_ref, n_ref, loff_ref, *, prompt_steps):
    is_prompt = pl.program_id(0) < prompt_steps
    pick_rows = lambda p_ref, s_ref: jnp.where(is_prompt, p_ref[...], jnp.concatenate([s_ref[...]] * MIX_TILES, axis=0))
    y = pick_rows(yp_ref, ys_ref)
    y = 0.5 * y * (1.0 + jnp.tanh(math.sqrt(2.0 / math.pi) * (y + 0.044715 * (y * y * y))))
    y = y * jax.nn.sigmoid(_bdot(y, wglu_ref[...]) + bglu_ref[...])
    attn = jnp.where(is_prompt, ap_ref[...].astype(f32),
                     jnp.concatenate([as_ref[...].astype(f32)] * MIX_TILES, axis=0)).astype(bf16)
    cat = jnp.concatenate([attn, _rms(y, gs_ref[...]).astype(bf16)], axis=1)
    x1 = pick_rows(xp_ref, xs_ref) + jnp.dot(cat, wout_ref[...], preferred_element_type=f32)
    x1_ref[...] = x1
    hf = _rms(x1, gf_ref[...])

    logits = _bdot(hf, wr_ref[...]) + br_ref[...]
    le = logits[:, :N_EXPERTS]
    lg = logits[:, N_EXPERTS:N_EXPERTS + N_EXPERT_GROUPS]
    tm = le.shape[0]
    gmax = jnp.max(lg, axis=-1, keepdims=True)
    gi = lax.broadcasted_iota(i32, (tm, N_EXPERT_GROUPS), 1).astype(f32)
    gsel = jnp.min(jnp.where(lg == gmax, gi, float(N_EXPERT_GROUPS)), axis=-1, keepdims=True)
    pg = 1.0 / jnp.sum(jnp.exp(lg - gmax), axis=-1, keepdims=True)
    ei_int = lax.broadcasted_iota(i32, (tm, N_EXPERTS), 1)
    ei = ei_int.astype(f32)
    egroup = jnp.right_shift(ei_int, int(math.log2(EXPERTS_PER_GROUP))).astype(f32)
    lm = jnp.where(egroup == gsel, le, NEG)
    v1 = jnp.max(lm, axis=-1, keepdims=True)
    i1 = jnp.min(jnp.where(lm == v1, ei, float(N_EXPERTS)), axis=-1, keepdims=True)
    lm2 = jnp.where(ei == i1, NEG, lm)
    v2 = jnp.max(lm2, axis=-1, keepdims=True)
    i2 = jnp.min(jnp.where(lm2 == v2, ei, float(N_EXPERTS)), axis=-1, keepdims=True)
    ex = jnp.exp(v2 - v1)
    wts_ref[...] = jnp.concatenate([pg / (1.0 + ex), pg * ex / (1.0 + ex)], axis=1)

    oh1 = (ei == i1).astype(f32)
    oh2 = (ei == i2).astype(f32)
    hf_bf = hf.astype(bf16)
    sorted_row = lax.broadcasted_iota(i32, (ROW_TILE, SLOTS * ROW_TILE), 1).astype(f32)
    for h in range(MIX_TILES):
        rows = slice(h * ROW_TILE, (h + 1) * ROW_TILE)
        both = (oh1[rows] + oh2[rows]).astype(bf16)
        before = jnp.dot(tri_ref[...], both, preferred_element_type=f32)
        count = jnp.sum(oh1[rows] + oh2[rows], axis=0, keepdims=True)
        lower = jnp.sum(jnp.dot(both, upper_ref[...], preferred_element_type=f32), axis=0, keepdims=True)
        lp1 = jnp.sum(oh1[rows] * (before + lower), axis=-1, keepdims=True)
        lp2 = jnp.sum(oh2[rows] * (before + lower), axis=-1, keepdims=True)
        lpos_ref[rows, :] = jnp.concatenate([lp1, lp2], axis=1)
        n_ref[h] = count
        loff_ref[h] = lower
        pick = ((sorted_row == lp1) | (sorted_row == lp2)).astype(bf16)
        xl = lax.dot_general(pick, hf_bf[rows], (((0,), (0,)), ((), ())), preferred_element_type=f32)
        _store_row_major(xl_ref, h * SLOTS * ROW_TILE, xl)


def _mix(a_p, a_s, ys_p, ys_s, x_p, x_s, wglu, bglu, gs, wout, gf, wr, br, tri, upper):
    tm = MIX_TILES * ROW_TILE
    assert x_p.shape[0] % tm == 0 and x_s.shape[0] == ROW_TILE
    prompt_steps = x_p.shape[0] // tm
    t = x_p.shape[0] + x_s.shape[0]
    p_rows = lambda n: pl.BlockSpec((tm, n), lambda i: (jnp.minimum(i, prompt_steps - 1), 0))
    s_rows = lambda n: pl.BlockSpec((ROW_TILE, n), lambda i: (0, 0))
    row = lambda n: pl.BlockSpec((tm, n), lambda i: (i, 0))
    full = lambda arr: pl.BlockSpec(arr.shape, lambda i: (0,) * arr.ndim)
    per_tile = pl.BlockSpec((MIX_TILES, 1, N_EXPERTS), lambda i: (i, 0, 0))
    return pl.pallas_call(
        functools.partial(_mix_body, prompt_steps=prompt_steps),
        grid=(prompt_steps + 1,),
        in_specs=[p_rows(ATTN_WIDTH), s_rows(ATTN_WIDTH), p_rows(SSM_WIDTH), s_rows(SSM_WIDTH), p_rows(D_MODEL),
                  s_rows(D_MODEL), full(wglu), full(bglu), full(gs), full(wout), full(gf), full(wr), full(br),
                  full(tri), full(upper)],
        out_specs=[row(D_MODEL), pl.BlockSpec((SLOTS * tm * PIECES, LANES), lambda i: (i, 0)), row(2), row(2),
                   per_tile, per_tile],
        out_shape=[jax.ShapeDtypeStruct((t, D_MODEL), f32),
                   jax.ShapeDtypeStruct((SLOTS * t * PIECES, LANES), f32),
                   jax.ShapeDtypeStruct((t, 2), f32), jax.ShapeDtypeStruct((t, 2), f32),
                   jax.ShapeDtypeStruct((t // ROW_TILE, 1, N_EXPERTS), f32),
                   jax.ShapeDtypeStruct((t // ROW_TILE, 1, N_EXPERTS), f32)],
        compiler_params=_cparams(),
        name="mix",
    )(a_p, a_s, ys_p, ys_s, x_p, x_s, wglu, bglu, gs, wout, gf, wr, br, tri, upper)


def _store_row_major(ref, first_row, x):
    for c in range(PIECES):
        ref[pl.ds(first_row * PIECES + c, x.shape[0], stride=PIECES), :] = x[:, c * LANES:(c + 1) * LANES]


def _load_row_major(ref, n_rows):
    return jnp.concatenate([ref[pl.ds(c, n_rows, stride=PIECES), :] for c in range(PIECES)], axis=1)


def _copy_rows(src, s_row, dst, d_row, n_rows, sem):
    return pltpu.make_async_copy(src.at[pl.ds(pl.multiple_of(s_row * PIECES, PIECES), n_rows * PIECES), :],
                                 dst.at[pl.ds(pl.multiple_of(d_row * PIECES, PIECES), n_rows * PIECES), :], sem)


def _for_each_piece(n, fn, pieces=RUN_PIECES):
    off = 0 if pieces[0] == RUN_PIECES[0] else n & ~(2 * pieces[0] - 1)
    for piece in pieces:
        @pl.when((n & piece) != 0)
        def _(off=off, piece=piece):
            fn(off, piece)
        off = off + (n & piece)


def _experts_body(te_ref, tpos_ref, tvalid_ref, tlo_ref, thi_ref, n_ref, cum_ref, loff_ref,
                  xl_hbm, wg_ref, wu_ref, wd_ref, o_ref, wg_s, wu_s, wd_s, sem, *xbufs):
    i = pl.program_id(0)
    last = pl.num_programs(0) - 1
    token_tiles = n_ref.shape[0] // N_EXPERTS
    ring = len(xbufs)

    def start_run(t, tau, enabled, buf, buf_sem):
        e, lo = te_ref[t], tpos_ref[t]
        k = tau * N_EXPERTS + e
        s, n = cum_ref[k], n_ref[k]
        a = jnp.maximum(s, lo)
        length = jnp.where(enabled, jnp.maximum(jnp.minimum(s + n, lo + ROW_TILE) - a, 0), 0)
        local = loff_ref[k] + (a - s)
        _for_each_piece(length, lambda off, piece: _copy_rows(
            xl_hbm, tau * (SLOTS * ROW_TILE) + local + off, buf, a - lo + off, piece, buf_sem).start())

    def start_runs_loop(t, first, stop, buf, buf_sem):
        def run(tau, c):
            start_run(t, tau, True, buf, buf_sem)
            return c
        lax.fori_loop(first, stop, run, 0)

    def clear(buf):
        buf[...] = jnp.zeros_like(buf)

    @pl.when(i == 0)
    def _():
        for buf in xbufs:
            clear(buf)
        for t in range(ring - 1):
            @pl.when(tvalid_ref[t] > 0)
            def _(t=t):
                start_runs_loop(t, tlo_ref[t], thi_ref[t] + 1, xbufs[t], sem.at[t])

    @pl.when((i == 0) | (te_ref[i] != te_ref[jnp.maximum(i - 1, 0)]))
    def _():
        wg_s[...] = wg_ref[0].astype(bf16)
        wu_s[...] = wu_ref[0].astype(bf16)
        wd_s[...] = wd_ref[0].astype(bf16)

    valid = tvalid_ref[i]

    def tile_step(cur):
        ahead = (cur + ring - 1) % ring
        buf, buf_sem, next_buf, next_sem = xbufs[cur], sem.at[cur], xbufs[ahead], sem.at[ahead]
        _for_each_piece(valid, lambda off, piece: _copy_rows(
            xl_hbm, 0, buf, 0, piece, buf_sem).wait())

        nxt = jnp.minimum(i + ring - 1, last)
        go = (i + ring - 1 <= last) & (tvalid_ref[nxt] > 0)
        first, final = tlo_ref[nxt], thi_ref[nxt]
        for j in range(UNROLLED_RUNS):
            start_run(nxt, jnp.minimum(first + j, token_tiles - 1), go & (first + j <= final), next_buf, next_sem)

        x = _load_row_major(buf, ROW_TILE).astype(bf16)
        clear(buf)
        hg = jnp.dot(x, wg_s[...], preferred_element_type=f32)
        hu = jnp.dot(x, wu_s[...], preferred_element_type=f32)
        y = jnp.dot((hg * jax.nn.sigmoid(hg) * hu).astype(bf16), wd_s[...], preferred_element_type=f32)
        _store_row_major(o_ref, 0, y)

        @pl.when(go & (final - first >= UNROLLED_RUNS))
        def _():
            start_runs_loop(nxt, first + UNROLLED_RUNS, final + 1, next_buf, next_sem)

    for cur in range(ring):
        pl.when((valid > 0) & (i % ring == cur))(functools.partial(tile_step, cur))

    @pl.when(valid == 0)
    def _():
        o_ref[...] = jnp.zeros_like(o_ref)


def _experts(tables, xl, wg, wu, wd, tiles):
    wspec = lambda a: pl.BlockSpec((1,) + a.shape[1:], lambda i, te, *_: (te[i], 0, 0))
    return pl.pallas_call(
        _experts_body,
        grid_spec=pltpu.PrefetchScalarGridSpec(
            num_scalar_prefetch=len(tables),
            grid=(tiles,),
            in_specs=[pl.BlockSpec(memory_space=pl.ANY), wspec(wg), wspec(wu), wspec(wd)],
            out_specs=pl.BlockSpec((ROW_TILE * PIECES, LANES), lambda i, *_: (i, 0)),
            scratch_shapes=[pltpu.VMEM(wg.shape[1:], bf16), pltpu.VMEM(wu.shape[1:], bf16), pltpu.VMEM(wd.shape[1:], bf16),
                            pltpu.SemaphoreType.DMA((EXPERT_RING,))]
            + [pltpu.VMEM((ROW_TILE * PIECES, LANES), f32)] * EXPERT_RING),
        out_shape=jax.ShapeDtypeStruct((tiles * ROW_TILE * PIECES, LANES), f32),
        compiler_params=_cparams(),
        name="moe_experts",
    )(*tables, xl, wg, wu, wd)


def _combine_body(n_ref, gpos_ref, loff_ref, ys_hbm, x_ref, w_ref, l_ref, g_ref, op_ref, os_ref, sem, *ybufs,
                  prompt_tiles):
    i = pl.program_id(0)
    last = pl.num_programs(0) - 1
    tile_rows = SLOTS * ROW_TILE
    ring = len(ybufs)

    def start_run(t, e, enabled, buf, buf_sem, pieces=RUN_PIECES):
        k = t * N_EXPERTS + e
        _for_each_piece(jnp.where(enabled, n_ref[k], 0), lambda off, piece: _copy_rows(
            ys_hbm, gpos_ref[k] + off, buf, loff_ref[k] + off, piece, buf_sem).start(), pieces)

    @pl.when(i == 0)
    def _():
        for t in range(ring - 1):
            def run(e, c, t=t):
                start_run(t, e, True, ybufs[t], sem.at[t])
                return c
            lax.fori_loop(0, N_EXPERTS, run, 0)

    def step(cur):
        ahead = (cur + ring - 1) % ring
        buf, buf_sem, next_buf, next_sem = ybufs[cur], sem.at[cur], ybufs[ahead], sem.at[ahead]
        _copy_rows(ys_hbm, 0, buf, 0, tile_rows, buf_sem).wait()
        nxt, go = jnp.minimum(i + ring - 1, last), i + ring - 1 <= last
        longest = 0
        for e in range(N_EXPERTS):
            start_run(nxt, e, go, next_buf, next_sem, SHORT_PIECES)
            longest = longest | n_ref[nxt * N_EXPERTS + e]
        yl = _load_row_major(buf, tile_rows).astype(bf16)
        sorted_row = lax.broadcasted_iota(i32, (ROW_TILE, tile_rows), 1).astype(f32)
        w, lp = w_ref[...], l_ref[...]
        y1 = jnp.dot((sorted_row == lp[:, 0:1]).astype(bf16), yl, preferred_element_type=f32)
        y2 = jnp.dot((sorted_row == lp[:, 1:2]).astype(bf16), yl, preferred_element_type=f32)
        out = _rms(x_ref[...] + (w[:, 0:1] * y1 + w[:, 1:2] * y2), g_ref[...])

        @pl.when(i < prompt_tiles)
        def _():
            op_ref[...] = out

        @pl.when(i >= prompt_tiles)
        def _():
            os_ref[...] = out

        @pl.when(go & (longest >= 2 * SHORT_PIECES[0]))
        def _():
            def long_pieces(e, c):
                start_run(nxt, e, True, next_buf, next_sem, LONG_PIECES)
                return c
            lax.fori_loop(0, N_EXPERTS, long_pieces, 0)

    for cur in range(ring):
        pl.when(i % ring == cur)(functools.partial(step, cur))


def _combine(tables, ys, x1, wts, lpos, g, prompt_rows):
    prompt_tiles = prompt_rows // ROW_TILE
    tiles = x1.shape[0] // ROW_TILE
    row = lambda n: pl.BlockSpec((ROW_TILE, n), lambda i, *_: (i, 0))
    p_rows = lambda n: pl.BlockSpec((ROW_TILE, n), lambda i, *_: (jnp.minimum(i, prompt_tiles - 1), 0))
    s_rows = lambda n: pl.BlockSpec((ROW_TILE, n), lambda i, *_: (jnp.maximum(i - prompt_tiles, 0), 0))
    return pl.pallas_call(
        functools.partial(_combine_body, prompt_tiles=prompt_tiles),
        grid_spec=pltpu.PrefetchScalarGridSpec(
            num_scalar_prefetch=len(tables),
            grid=(tiles,),
            in_specs=[pl.BlockSpec(memory_space=pl.ANY), row(D_MODEL), row(2), row(2),
                      pl.BlockSpec((1, D_MODEL), lambda i, *_: (0, 0))],
            out_specs=[p_rows(D_MODEL), s_rows(D_MODEL)],
            scratch_shapes=[pltpu.SemaphoreType.DMA((COMBINE_RING,))]
            + [pltpu.VMEM((SLOTS * ROW_TILE * PIECES, LANES), f32)] * COMBINE_RING),
        out_shape=[jax.ShapeDtypeStruct((prompt_rows, D_MODEL), f32),
                   jax.ShapeDtypeStruct((x1.shape[0] - prompt_rows, D_MODEL), f32)],
        compiler_params=_cparams(),
        name="moe_combine",
    )(*tables, ys, x1, wts, lpos, g)


def _moe_tables(n, loff, tiles):
    cum = jnp.cumsum(n, axis=0) - n
    counts = jnp.sum(n, axis=0)
    padded = (counts + ROW_TILE - 1) // ROW_TILE * ROW_TILE
    ends = jnp.cumsum(padded)
    starts = ends - padded
    first = jnp.arange(tiles, dtype=i32) * ROW_TILE
    expert = jnp.minimum(jnp.sum((first[:, None] >= ends[None, :]).astype(i32), axis=1), N_EXPERTS - 1)
    sel = expert[:, None] == jnp.arange(N_EXPERTS)[None, :]
    pick = lambda v: jnp.sum(jnp.where(sel, v[None, :], 0), axis=1)
    pos = first - pick(starts)
    valid = jnp.where(first < ends[-1], jnp.clip(pick(counts) - pos, 0, ROW_TILE), 0)
    cum_t, n_t = cum.T[expert], n.T[expert]
    touches = (cum_t + n_t > pos[:, None]) & (cum_t < (pos + ROW_TILE)[:, None]) & (n_t > 0)
    tau = jnp.arange(n.shape[0], dtype=i32)[None, :]
    lo = jnp.min(jnp.where(touches, tau, n.shape[0]), axis=1)
    hi = jnp.max(jnp.where(touches, tau, -1), axis=1)
    as_i32 = lambda v: v.astype(i32)
    flat = lambda v: v.reshape(-1).astype(i32)
    expert_tables = tuple(map(as_i32, (expert, pos, valid, lo, hi))) + (flat(n), flat(cum), flat(loff))
    combine_tables = (flat(n), flat(starts[None, :] + cum), flat(loff))
    return expert_tables, combine_tables


def kernel(x_prompt, x_sample, cache_k, cache_v, state_ssm_re, state_ssm_im, g_norm_mix, w_in, attn_sinks, ssm_a_re,
           ssm_a_im, ssm_log_dt, ssm_b_re, ssm_b_im, ssm_c_re, ssm_c_im, ssm_d, w_glu, b_glu, g_attn_out, g_ssm_out,
           w_out, g_norm_ffn, w_router_group, b_router_group, w_router_expert, b_router_expert, w_exp_gate, w_exp_up,
           w_exp_down, g_final):
    bp, lp, _ = x_prompt.shape
    bs, ls, _ = x_sample.shape
    depth = w_in.shape[0]
    assert depth == 1 and ls == SSM_STEPS and lp % ATTN_TILE == 0 and (bs * ls) % ROW_TILE == 0
    tp, ts = bp * lp, bs * ls
    wc = cache_k.shape[2]
    row2 = lambda v: v.reshape(1, -1)

    xp = x_prompt.reshape(tp, D_MODEL)
    xs = x_sample.reshape(ts, D_MODEL)
    w_in_bf = w_in[0].astype(bf16)
    qp, kp, vp, up = _proj(xp, row2(g_norm_mix[0]), w_in_bf, 512)
    qs, kq, vq, us = _proj(xs, row2(g_norm_mix[0]), w_in_bf, ts)

    sinks = attn_sinks[0]
    g_att = row2(g_attn_out[0])
    ap = _attn_prompt(qp.reshape(bp, lp, -1), kp.reshape(bp, lp, -1), vp.reshape(bp, lp, -1), sinks, g_att)
    a_s, k_roll, v_roll = _attn_sample(qs.reshape(bs, ls, -1), kq.reshape(bs, ls, -1), vq.reshape(bs, ls, -1),
                                       cache_k[0].reshape(bs, wc, KV_WIDTH), cache_v[0].reshape(bs, wc, KV_WIDTH),
                                       sinks, g_att)

    toep, wso, wsi, lev = _ssm_weights(ssm_a_re[0], ssm_a_im[0], ssm_log_dt[0], ssm_b_re[0], ssm_b_im[0],
                                       ssm_c_re[0], ssm_c_im[0])
    d_row = row2(ssm_d[0])
    to_blocks = lambda h: h.reshape(bs, SSM_BLOCKS, STATE_COLS).transpose(1, 0, 2)
    from_blocks = lambda h: h.transpose(1, 0, 2).reshape(bs, SSM_GROUPS, SSM_STATE)
    yp, hrp, hip, ysm, hrs, his = _ssm(up.reshape(bp, lp, -1), us.reshape(1, ts, -1), to_blocks(state_ssm_re[0]),
                                       to_blocks(state_ssm_im[0]), toep, wso, wsi, lev, d_row)

    wr = jnp.zeros((D_MODEL, ROUTER_COLS), f32)
    wr = wr.at[:, :N_EXPERTS].set(w_router_expert[0]).at[:, N_EXPERTS:N_EXPERTS + N_EXPERT_GROUPS].set(w_router_group[0])
    br = jnp.zeros((1, ROUTER_COLS), f32)
    br = br.at[0, :N_EXPERTS].set(b_router_expert[0]).at[0, N_EXPERTS:N_EXPERTS + N_EXPERT_GROUPS].set(b_router_group[0])
    tri = jnp.tril(jnp.ones((ROW_TILE, ROW_TILE), bf16), -1)
    upper = jnp.triu(jnp.ones((N_EXPERTS, N_EXPERTS), bf16), 1)
    mix_w = (w_glu[0].astype(bf16), row2(b_glu[0]), row2(g_ssm_out[0]), w_out[0].astype(bf16), row2(g_norm_ffn[0]),
             wr.astype(bf16), br, tri, upper)
    x1, xl, wts, lpos, n_rows, n_off = _mix(ap.reshape(tp, -1), a_s.reshape(ts, -1), yp.reshape(tp, -1),
                                            ysm.reshape(ts, -1), xp, xs, *mix_w)

    per_tile = lambda v: v.reshape(-1, N_EXPERTS).astype(i32)
    tiles = (SLOTS * (tp + ts)) // ROW_TILE + N_EXPERTS
    expert_tables, combine_tables = _moe_tables(per_tile(n_rows), per_tile(n_off), tiles)
    expert_out = _experts(expert_tables, xl, w_exp_gate[0], w_exp_up[0], w_exp_down[0], tiles)
    y_p, y_s = _combine(combine_tables, expert_out, x1, wts, lpos, row2(g_final), tp)

    kvshape = lambda a, b: a.reshape(1, b, -1, N_KV_HEADS, HEAD_DIM)
    block_state = lambda h: h.reshape(bp, SSM_GROUPS, SSM_STATE)[None]
    wcp = min(WINDOW, lp)
    return (y_p.reshape(bp, lp, D_MODEL), y_s.reshape(bs, ls, D_MODEL),
            kvshape(kp.reshape(bp, lp, -1)[:, lp - wcp:], bp), kvshape(vp.reshape(bp, lp, -1)[:, lp - wcp:], bp),
            block_state(hrp), block_state(hip),
            kvshape(k_roll, bs), kvshape(v_roll, bs),
            from_blocks(hrs)[None], from_blocks(his)[None])
```

```python
import functools
import math

import jax
import jax.numpy as jnp
from jax import lax
from jax.experimental import pallas as pl
from jax.experimental.pallas import tpu as pltpu

f32, bf16, i32 = jnp.float32, jnp.bfloat16, jnp.int32

D_MODEL = 1024
CHUNK = 64
N_BACK = 2
WINDOW = 128
ATTN_WIDTH = 512
HEAD_DIM = 64
N_KV_HEADS = 2
Q_PER_KV = 4
KV_WIDTH = 128
SSM_WIDTH = 512
SSM_GROUP = 16
SSM_GROUPS = 32
SSM_STATE = 64
PROJ_WIDTH = 1280
N_EXPERT_GROUPS = 4
EXPERTS_PER_GROUP = 8
N_EXPERTS = 32
D_EXPERT = 512
EPS = 1e-6
NEG = -1e30

LANES = 128
SSM_STEPS = 16
SSM_BLOCKS = SSM_WIDTH // LANES
GROUPS_PER_BLOCK = LANES // SSM_GROUP
STATE_COLS = GROUPS_PER_BLOCK * SSM_STATE
ROW_TILE = 256
SLOTS = 2
PIECES = D_MODEL // LANES
RUN_PIECES = tuple(1 << b for b in reversed(range(int(math.log2(ROW_TILE)) + 1)))
SHORT_PIECES = tuple(p for p in RUN_PIECES if p <= 32)
LONG_PIECES = tuple(p for p in RUN_PIECES if p > 32)
MIX_TILES = 2
UNROLLED_RUNS = 20
EXPERT_RING = 3
COMBINE_RING = 3
VMEM_LIMIT = 56 * 1024 * 1024


def _cparams(n_axes=1, limit=VMEM_LIMIT):
    return pltpu.CompilerParams(dimension_semantics=("arbitrary",) * n_axes, vmem_limit_bytes=limit)


def _rms(x, g):
    return x * lax.rsqrt(jnp.mean(x * x, axis=-1, keepdims=True) + EPS) * g


def _bdot(a, b):
    return jnp.dot(a.astype(bf16), b.astype(bf16), preferred_element_type=f32)


def _proj_body(x_ref, g_ref, w_ref, q_ref, k_ref, v_ref, u_ref):
    h = _rms(x_ref[...], g_ref[...])
    z = _bdot(h, w_ref[...])
    q_ref[...] = z[:, :ATTN_WIDTH] * (HEAD_DIM ** -0.5)
    k_ref[...] = z[:, ATTN_WIDTH:ATTN_WIDTH + KV_WIDTH]
    v_ref[...] = z[:, ATTN_WIDTH + KV_WIDTH:ATTN_WIDTH + 2 * KV_WIDTH]
    u_ref[...] = z[:, ATTN_WIDTH + 2 * KV_WIDTH:]


def _proj(x2d, g, w_bf, tm):
    t = x2d.shape[0]
    row = lambda n: pl.BlockSpec((tm, n), lambda i: (i, 0))
    full = lambda a: pl.BlockSpec(a.shape, lambda i: (0,) * a.ndim)
    return pl.pallas_call(
        _proj_body,
        grid=(t // tm,),
        in_specs=[row(D_MODEL), full(g), full(w_bf)],
        out_specs=[row(ATTN_WIDTH), row(KV_WIDTH), row(KV_WIDTH), row(SSM_WIDTH)],
        out_shape=[jax.ShapeDtypeStruct((t, n), f32) for n in (ATTN_WIDTH, KV_WIDTH, KV_WIDTH, SSM_WIDTH)],
        compiler_params=_cparams(),
        name="proj",
    )(x2d, g, w_bf)


def _sink_column(sink_ref, kv, rows_per_head):
    r = lax.broadcasted_iota(i32, (Q_PER_KV * rows_per_head, 1), 0)
    col = jnp.full((Q_PER_KV * rows_per_head, 1), sink_ref[kv * Q_PER_KV], f32)
    for j in range(1, Q_PER_KV):
        col = jnp.where(r >= j * rows_per_head, sink_ref[kv * Q_PER_KV + j], col)
    return col


def _attend(qs, kc, vc, sink_col, valid):
    s = lax.dot_general(qs.astype(bf16), kc.astype(bf16), (((1,), (1,)), ((), ())), preferred_element_type=f32)
    if valid is not None:
        s = jnp.where(valid, s, NEG)
    m = jnp.maximum(jnp.max(s, axis=-1, keepdims=True), sink_col)
    p = jnp.exp(s - m)
    denom = jnp.sum(p, axis=-1, keepdims=True) + jnp.exp(sink_col - m)
    return _bdot(p, vc) / denom


def _heads_attend(q, k, v, sink_ref, valid):
    rows = q.shape[0]
    pieces = []
    for kv in range(N_KV_HEADS):
        qs = jnp.concatenate(
            [q[:, (kv * Q_PER_KV + j) * HEAD_DIM:(kv * Q_PER_KV + j + 1) * HEAD_DIM] for j in range(Q_PER_KV)], axis=0)
        o = _attend(qs, k[:, kv * HEAD_DIM:(kv + 1) * HEAD_DIM], v[:, kv * HEAD_DIM:(kv + 1) * HEAD_DIM],
                    _sink_column(sink_ref, kv, rows), valid)
        pieces += [o[j * rows:(j + 1) * rows] for j in range(Q_PER_KV)]
    return jnp.concatenate(pieces, axis=1)


ATTN_TILE = 256
CHUNKS_PER_TILE = ATTN_TILE // CHUNK
KEY_SPAN = (N_BACK + 1) * CHUNK


def _attn_prompt_body(sink_ref, q_ref, kp_ref, kc_ref, vp_ref, vc_ref, g_ref, o_ref):
    i = pl.program_id(1)
    kwin = jnp.concatenate([kp_ref[0], kc_ref[0]], axis=0)
    vwin = jnp.concatenate([vp_ref[0], vc_ref[0]], axis=0)
    key_chunk = lax.broadcasted_iota(i32, (1, KEY_SPAN), 1) // CHUNK
    for c in range(CHUNKS_PER_TILE):
        valid = (i * CHUNKS_PER_TILE + c - N_BACK + key_chunk) >= 0
        o = _heads_attend(q_ref[0, c * CHUNK:(c + 1) * CHUNK, :], kwin[c * CHUNK:c * CHUNK + KEY_SPAN],
                          vwin[c * CHUNK:c * CHUNK + KEY_SPAN], sink_ref, valid)
        o_ref[0, c * CHUNK:(c + 1) * CHUNK, :] = _rms(o, g_ref[...]).astype(bf16)


def _attn_prompt(q, k, v, sinks, g):
    b, l, _ = q.shape
    back = N_BACK * CHUNK
    per = ATTN_TILE // back
    prev = pl.BlockSpec((1, back, KV_WIDTH), lambda bi, i: (bi, jnp.maximum(i * per - 1, 0), 0))
    cur = pl.BlockSpec((1, ATTN_TILE, KV_WIDTH), lambda bi, i: (bi, i, 0))
    return pl.pallas_call(
        _attn_prompt_body,
        grid=(b, l // ATTN_TILE),
        in_specs=[pl.BlockSpec(memory_space=pltpu.SMEM),
                  pl.BlockSpec((1, ATTN_TILE, ATTN_WIDTH), lambda bi, i: (bi, i, 0)),
                  prev, cur, prev, cur,
                  pl.BlockSpec((1, ATTN_WIDTH), lambda bi, i: (0, 0))],
        out_specs=pl.BlockSpec((1, ATTN_TILE, ATTN_WIDTH), lambda bi, i: (bi, i, 0)),
        out_shape=jax.ShapeDtypeStruct((b, l, ATTN_WIDTH), bf16),
        compiler_params=_cparams(2),
        name="attn_prompt",
    )(sinks, q, k, k, v, v, g)


def _attn_sample_body(sink_ref, q_ref, kn_ref, vn_ref, ck_ref, cv_ref, g_ref, o_ref, nk_ref, nv_ref):
    kall = jnp.concatenate([ck_ref[0], kn_ref[0]], axis=0)
    vall = jnp.concatenate([cv_ref[0], vn_ref[0]], axis=0)
    o = _heads_attend(q_ref[0], kall, vall, sink_ref, None)
    o_ref[0] = _rms(o, g_ref[...]).astype(bf16)
    n_new = kn_ref.shape[1]
    nk_ref[0] = kall[n_new:]
    nv_ref[0] = vall[n_new:]


def _attn_sample(q, k_new, v_new, cache_k, cache_v, sinks, g):
    b, l, _ = q.shape
    wc = cache_k.shape[1]
    blk = lambda r, n: pl.BlockSpec((1, r, n), lambda bi: (bi, 0, 0))
    return pl.pallas_call(
        _attn_sample_body,
        grid=(b,),
        in_specs=[pl.BlockSpec(memory_space=pltpu.SMEM), blk(l, ATTN_WIDTH), blk(l, KV_WIDTH), blk(l, KV_WIDTH),
                  blk(wc, KV_WIDTH), blk(wc, KV_WIDTH), pl.BlockSpec((1, ATTN_WIDTH), lambda bi: (0, 0))],
        out_specs=[blk(l, ATTN_WIDTH), blk(wc, KV_WIDTH), blk(wc, KV_WIDTH)],
        out_shape=[jax.ShapeDtypeStruct((b, l, ATTN_WIDTH), bf16),
                   jax.ShapeDtypeStruct((b, wc, KV_WIDTH), f32), jax.ShapeDtypeStruct((b, wc, KV_WIDTH), f32)],
        compiler_params=_cparams(),
        name="attn_sample",
    )(sinks, q, k_new, v_new, cache_k, cache_v, g)


def _ssm_tables(a_re, a_im, log_dt, b_re, b_im):
    dt = jnp.exp(log_dt)[:, None]
    lam_r, lam_i = a_re * dt, a_im * dt

    def power(n):
        mag = jnp.exp(n * lam_r)
        return jnp.stack([mag * jnp.cos(n * lam_i), mag * jnp.sin(n * lam_i)])

    ar, ai = power(1.0)
    den = a_re * a_re + a_im * a_im
    nr, ni = ar - 1.0, ai
    fr = ((nr * a_re + ni * a_im) / den)[..., None]
    fi = ((ni * a_re - nr * a_im) / den)[..., None]
    bbar = jnp.stack([fr * b_re - fi * b_im, fr * b_im + fi * b_re])
    bbar = bbar.transpose(0, 1, 3, 2).reshape(2, SSM_BLOCKS, LANES, SSM_STATE)
    by_block = lambda t: t.reshape(t.shape[:-2] + (SSM_BLOCKS, STATE_COLS))
    steps = by_block(power(jnp.arange(SSM_STEPS + 1, dtype=f32)[:, None, None]))
    levels = by_block(power(SSM_STEPS * 2.0 ** jnp.arange(8, dtype=f32)[:, None, None]))
    return bbar, steps.transpose(2, 0, 1, 3), levels.transpose(2, 1, 0, 3)


def _ssm_chunk_rows(u_ref, nk):
    xs = [u_ref[0, pl.ds(s, nk, stride=SSM_STEPS), :] for s in range(SSM_STEPS)]
    pairs = [jnp.concatenate([xs[2 * p], xs[2 * p + 1]], axis=1).astype(bf16) for p in range(SSM_STEPS // 2)]
    return xs, pairs


def _ssm_intra(pairs, toep_ref, nk):
    nd = len(pairs)
    y = [None] * nd
    for d in range(nd):
        lhs = jnp.concatenate(pairs[:nd - d], axis=0) if nd - d > 1 else pairs[0]
        r = jnp.dot(lhs, toep_ref[d], preferred_element_type=f32)
        for p in range(nd - d):
            blk = r[p * nk:(p + 1) * nk]
            y[p + d] = blk if y[p + d] is None else y[p + d] + blk
    return y


def _shift_rows(x, sh):
    rows = lax.broadcasted_iota(i32, (x.shape[0], 1), 0)
    return jnp.where(rows >= sh, pltpu.roll(x, sh, axis=0), 0.0)


def _group_index(shape, axis, sub):
    idx = lax.broadcasted_iota(i32, shape, axis)
    return jnp.right_shift(idx, int(math.log2(sub))) & (GROUPS_PER_BLOCK - 1)


def _build_ssm_weights(bb_ref, cr_ref, ci_ref, pw_ref, toep_s, wout_s, win_s):
    hp = lax.Precision.HIGHEST
    k = lax.broadcasted_iota(i32, (SSM_STATE, STATE_COLS), 0)
    n = lax.broadcasted_iota(i32, (SSM_STATE, STATE_COLS), 1)
    spread = ((n & (SSM_STATE - 1)) == k).astype(f32)
    own = _group_index((LANES, STATE_COLS), 0, SSM_GROUP) == _group_index((LANES, STATE_COLS), 1, SSM_STATE)

    def block_diag(compact):
        return jnp.where(own, jnp.dot(compact, spread, precision=hp, preferred_element_type=f32), 0.0)

    bb_r, bb_i = block_diag(bb_ref[0, 0]), block_diag(bb_ref[1, 0])
    c_r, c_i = block_diag(cr_ref[0]), block_diag(ci_ref[0])
    c_stack = jnp.concatenate([c_r.T, -c_i.T], axis=0)
    power = lambda e: (pw_ref[0, 0, e:e + 1, :], pw_ref[0, 1, e:e + 1, :])
    lag_kernel = [None] * SSM_STEPS
    for s in range(SSM_STEPS):
        pr, pi = power(SSM_STEPS - 1 - s)
        w = jnp.concatenate([bb_r * pr - bb_i * pi, bb_r * pi + bb_i * pr], axis=1)
        wout_s[s * LANES:(s + 1) * LANES, :] = w.astype(bf16)
        lag_kernel[SSM_STEPS - 1 - s] = jnp.dot(w, c_stack, precision=hp, preferred_element_type=f32)
        pr, pi = power(s + 1)
        g = jnp.concatenate([c_r * pr - c_i * pi, -(c_r * pi + c_i * pr)], axis=1)
        win_s[:, s * LANES:(s + 1) * LANES] = g.T.astype(bf16)
    zero = jnp.zeros((LANES, LANES), f32)
    for d in range(SSM_STEPS // 2):
        top = jnp.concatenate([lag_kernel[2 * d], lag_kernel[2 * d + 1]], axis=1)
        bottom = jnp.concatenate([lag_kernel[2 * d - 1] if d > 0 else zero, lag_kernel[2 * d]], axis=1)
        toep_s[d] = jnp.concatenate([top, bottom], axis=0).astype(bf16)


def _ssm_chunks(u_ref, d_ref, y_ref, toep_s, wout_s, win_s, entry_state):
    nk = u_ref.shape[1] // SSM_STEPS
    xs, pairs = _ssm_chunk_rows(u_ref, nk)
    y = _ssm_intra(pairs, toep_s, nk)
    s = jnp.dot(jnp.concatenate(pairs, axis=1), wout_s[...], preferred_element_type=f32)
    hprev, hr, hi = entry_state(s[:, :STATE_COLS], s[:, STATE_COLS:])
    y2 = _bdot(hprev, win_s[...])
    for st in range(SSM_STEPS):
        piece = (y[st // 2][:, (st % 2) * LANES:(st % 2 + 1) * LANES] + y2[:, st * LANES:(st + 1) * LANES]
                 + d_ref[...] * xs[st])
        y_ref[0, pl.ds(st, nk, stride=SSM_STEPS), :] = piece
    return hr, hi


def _ssm_body(u_ref, us_ref, h0r_ref, h0i_ref, bb_ref, cr_ref, ci_ref, pw_ref, lev_ref, d_ref,
              y_ref, hr_ref, hi_ref, ys_ref, hrs_ref, his_ref, toep_s, wout_s, win_s):
    @pl.when(pl.program_id(1) == 0)
    def _():
        _build_ssm_weights(bb_ref, cr_ref, ci_ref, pw_ref, toep_s, wout_s, win_s)

        def one_chunk(sr, si):
            h0r, h0i = h0r_ref[0], h0i_ref[0]
            ar, ai = lev_ref[0, 0, 0:1, :], lev_ref[0, 0, 1:2, :]
            return (jnp.concatenate([h0r, h0i], axis=1), sr + ar * h0r - ai * h0i, si + ar * h0i + ai * h0r)
        hrs_ref[0], his_ref[0] = _ssm_chunks(us_ref, d_ref, ys_ref, toep_s, wout_s, win_s, one_chunk)

    def scan_chunks(sr, si):
        nk = sr.shape[0]
        level = 0
        while (1 << level) < nk:
            ar, ai = lev_ref[0, level, 0:1, :], lev_ref[0, level, 1:2, :]
            tr, ti = _shift_rows(sr, 1 << level), _shift_rows(si, 1 << level)
            sr, si = sr + ar * tr - ai * ti, si + ar * ti + ai * tr
            level += 1
        return (jnp.concatenate([_shift_rows(sr, 1), _shift_rows(si, 1)], axis=1), sr[nk - 1:nk], si[nk - 1:nk])
    hr_ref[0, 0], hi_ref[0, 0] = _ssm_chunks(u_ref, d_ref, y_ref, toep_s, wout_s, win_s, scan_chunks)


def _ssm(u, us, h0r, h0i, bbar, c_re, c_im, steps, lev, d):
    b, l, _ = u.shape
    rows = us.shape[1]
    nb = rows // SSM_STEPS
    wspec = lambda a: pl.BlockSpec((1,) + a.shape[1:], lambda j, bi: (j,) + (0,) * (a.ndim - 1))
    st = pl.BlockSpec((1, 1, 1, STATE_COLS), lambda j, bi: (bi, j, 0, 0))
    sst = pl.BlockSpec((1, nb, STATE_COLS), lambda j, bi: (j, 0, 0))
    seq = pl.BlockSpec((1, l, LANES), lambda j, bi: (bi, 0, j))
    sseq = pl.BlockSpec((1, rows, LANES), lambda j, bi: (0, 0, j))
    return pl.pallas_call(
        _ssm_body,
        grid=(SSM_BLOCKS, b),
        in_specs=[seq, sseq, sst, sst, pl.BlockSpec((2, 1, LANES, SSM_STATE), lambda j, bi: (0, j, 0, 0)),
                  wspec(c_re), wspec(c_im), wspec(steps), wspec(lev), pl.BlockSpec((1, LANES), lambda j, bi: (0, j))],
        out_specs=[seq, st, st, sseq, sst, sst],
        out_shape=[jax.ShapeDtypeStruct((b, l, SSM_WIDTH), f32),
                   jax.ShapeDtypeStruct((b, SSM_BLOCKS, 1, STATE_COLS), f32),
                   jax.ShapeDtypeStruct((b, SSM_BLOCKS, 1, STATE_COLS), f32),
                   jax.ShapeDtypeStruct((1, rows, SSM_WIDTH), f32),
                   jax.ShapeDtypeStruct((SSM_BLOCKS, nb, STATE_COLS), f32),
                   jax.ShapeDtypeStruct((SSM_BLOCKS, nb, STATE_COLS), f32)],
        scratch_shapes=[pltpu.VMEM((SSM_STEPS // 2, 2 * LANES, 2 * LANES), bf16),
                        pltpu.VMEM((SSM_STEPS * LANES, 2 * STATE_COLS), bf16),
                        pltpu.VMEM((2 * STATE_COLS, SSM_STEPS * LANES), bf16)],
        compiler_params=_cparams(2),
        name="ssm",
    )(u, us, h0r, h0i, bbar, c_re, c_im, steps, lev, d)


ROUTER_COLS = LANES


def _mix_body(ap_ref, as_ref, yp_ref, ys_ref, xp_ref, xs_ref, wglu_ref, bglu_ref, gs_ref, wout_ref, gf_ref, wr_ref,
              br_ref, tri_ref, upper_ref, x1_ref, xl_ref, wts_ref, lpos_ref, n_ref, loff_ref, *, prompt_steps):
    is_prompt = pl.program_id(0) < prompt_steps
    pick_rows = lambda p_ref, s_ref: jnp.where(is_prompt, p_ref[...], jnp.concatenate([s_ref[...]] * MIX_TILES, axis=0))
    y = pick_rows(yp_ref, ys_ref)
    y = 0.5 * y * (1.0 + jnp.tanh(math.sqrt(2.0 / math.pi) * (y + 0.044715 * (y * y * y))))
    y = y * jax.nn.sigmoid(_bdot(y, wglu_ref[...]) + bglu_ref[...])
    attn = jnp.where(is_prompt, ap_ref[...].astype(f32),
                     jnp.concatenate([as_ref[...].astype(f32)] * MIX_TILES, axis=0)).astype(bf16)
    cat = jnp.concatenate([attn, _rms(y, gs_ref[...]).astype(bf16)], axis=1)
    x1 = pick_rows(xp_ref, xs_ref) + jnp.dot(cat, wout_ref[...], preferred_element_type=f32)
    x1_ref[...] = x1
    hf = _rms(x1, gf_ref[...])

    logits = _bdot(hf, wr_ref[...]) + br_ref[...]
    le = logits[:, :N_EXPERTS]
    lg = logits[:, N_EXPERTS:N_EXPERTS + N_EXPERT_GROUPS]
    tm = le.shape[0]
    gmax = jnp.max(lg, axis=-1, keepdims=True)
    gi = lax.broadcasted_iota(i32, (tm, N_EXPERT_GROUPS), 1).astype(f32)
    gsel = jnp.min(jnp.where(lg == gmax, gi, float(N_EXPERT_GROUPS)), axis=-1, keepdims=True)
    pg = 1.0 / jnp.sum(jnp.exp(lg - gmax), axis=-1, keepdims=True)
    ei_int = lax.broadcasted_iota(i32, (tm, N_EXPERTS), 1)
    ei = ei_int.astype(f32)
    egroup = jnp.right_shift(ei_int, int(math.log2(EXPERTS_PER_GROUP))).astype(f32)
    lm = jnp.where(egroup == gsel, le, NEG)
    v1 = jnp.max(lm, axis=-1, keepdims=True)
    i1 = jnp.min(jnp.where(lm == v1, ei, float(N_EXPERTS)), axis=-1, keepdims=True)
    lm2 = jnp.where(ei == i1, NEG, lm)
    v2 = jnp.max(lm2, axis=-1, keepdims=True)
    i2 = jnp.min(jnp.where(lm2 == v2, ei, float(N_EXPERTS)), axis=-1, keepdims=True)
    ex = jnp.exp(v2 - v1)
    wts_ref[...] = jnp.concatenate([pg / (1.0 + ex), pg * ex / (1.0 + ex)], axis=1)

    oh1 = (ei == i1).astype(f32)
    oh2 = (ei == i2).astype(f32)
    hf_bf = hf.astype(bf16)
    sorted_row = lax.broadcasted_iota(i32, (ROW_TILE, SLOTS * ROW_TILE), 1).astype(f32)
    for h in range(MIX_TILES):
        rows = slice(h * ROW_TILE, (h + 1) * ROW_TILE)
        both = (oh1[rows] + oh2[rows]).astype(bf16)
        before = jnp.dot(tri_ref[...], both, preferred_element_type=f32)
        count = jnp.sum(oh1[rows] + oh2[rows], axis=0, keepdims=True)
        lower = jnp.sum(jnp.dot(both, upper_ref[...], preferred_element_type=f32), axis=0, keepdims=True)
        lp1 = jnp.sum(oh1[rows] * (before + lower), axis=-1, keepdims=True)
        lp2 = jnp.sum(oh2[rows] * (before + lower), axis=-1, keepdims=True)
        lpos_ref[rows, :] = jnp.concatenate([lp1, lp2], axis=1)
        n_ref[h] = count
        loff_ref[h] = lower
        pick = ((sorted_row == lp1) | (sorted_row == lp2)).astype(bf16)
        xl = lax.dot_general(pick, hf_bf[rows], (((0,), (0,)), ((), ())), preferred_element_type=f32)
        _store_row_major(xl_ref, h * SLOTS * ROW_TILE, xl)


def _mix(a_p, a_s, ys_p, ys_s, x_p, x_s, wglu, bglu, gs, wout, gf, wr, br, tri, upper):
    tm = MIX_TILES * ROW_TILE
    assert x_p.shape[0] % tm == 0 and x_s.shape[0] == ROW_TILE
    prompt_steps = x_p.shape[0] // tm
    t = x_p.shape[0] + x_s.shape[0]
    p_rows = lambda n: pl.BlockSpec((tm, n), lambda i: (jnp.minimum(i, prompt_steps - 1), 0))
    s_rows = lambda n: pl.BlockSpec((ROW_TILE, n), lambda i: (0, 0))
    row = lambda n: pl.BlockSpec((tm, n), lambda i: (i, 0))
    full = lambda arr: pl.BlockSpec(arr.shape, lambda i: (0,) * arr.ndim)
    per_tile = pl.BlockSpec((MIX_TILES, 1, N_EXPERTS), lambda i: (i, 0, 0))
    return pl.pallas_call(
        functools.partial(_mix_body, prompt_steps=prompt_steps),
        grid=(prompt_steps + 1,),
        in_specs=[p_rows(ATTN_WIDTH), s_rows(ATTN_WIDTH), p_rows(SSM_WIDTH), s_rows(SSM_WIDTH), p_rows(D_MODEL),
                  s_rows(D_MODEL), full(wglu), full(bglu), full(gs), full(wout), full(gf), full(wr), full(br),
                  full(tri), full(upper)],
        out_specs=[row(D_MODEL), pl.BlockSpec((SLOTS * tm * PIECES, LANES), lambda i: (i, 0)), row(2), row(2),
                   per_tile, per_tile],
        out_shape=[jax.ShapeDtypeStruct((t, D_MODEL), f32),
                   jax.ShapeDtypeStruct((SLOTS * t * PIECES, LANES), f32),
                   jax.ShapeDtypeStruct((t, 2), f32), jax.ShapeDtypeStruct((t, 2), f32),
                   jax.ShapeDtypeStruct((t // ROW_TILE, 1, N_EXPERTS), f32),
                   jax.ShapeDtypeStruct((t // ROW_TILE, 1, N_EXPERTS), f32)],
        compiler_params=_cparams(),
        name="mix",
    )(a_p, a_s, ys_p, ys_s, x_p, x_s, wglu, bglu, gs, wout, gf, wr, br, tri, upper)


def _store_row_major(ref, first_row, x):
    for c in range(PIECES):
        ref[pl.ds(first_row * PIECES + c, x.shape[0], stride=PIECES), :] = x[:, c * LANES:(c + 1) * LANES]


def _load_row_major(ref, n_rows):
    return jnp.concatenate([ref[pl.ds(c, n_rows, stride=PIECES), :] for c in range(PIECES)], axis=1)


def _copy_rows(src, s_row, dst, d_row, n_rows, sem):
    return pltpu.make_async_copy(src.at[pl.ds(pl.multiple_of(s_row * PIECES, PIECES), n_rows * PIECES), :],
                                 dst.at[pl.ds(pl.multiple_of(d_row * PIECES, PIECES), n_rows * PIECES), :], sem)


def _for_each_piece(n, fn, pieces=RUN_PIECES):
    off = 0 if pieces[0] == RUN_PIECES[0] else n & ~(2 * pieces[0] - 1)
    for piece in pieces:
        @pl.when((n & piece) != 0)
        def _(off=off, piece=piece):
            fn(off, piece)
        off = off + (n & piece)


def _experts_body(te_ref, tpos_ref, tvalid_ref, tlo_ref, thi_ref, n_ref, cum_ref, loff_ref,
                  xl_hbm, wg_ref, wu_ref, wd_ref, o_ref, wg_s, wu_s, wd_s, sem, *xbufs):
    i = pl.program_id(0)
    last = pl.num_programs(0) - 1
    token_tiles = n_ref.shape[0] // N_EXPERTS
    ring = len(xbufs)

    def start_run(t, tau, enabled, buf, buf_sem):
        e, lo = te_ref[t], tpos_ref[t]
        k = tau * N_EXPERTS + e
        s, n = cum_ref[k], n_ref[k]
        a = jnp.maximum(s, lo)
        length = jnp.where(enabled, jnp.maximum(jnp.minimum(s + n, lo + ROW_TILE) - a, 0), 0)
        local = loff_ref[k] + (a - s)
        _for_each_piece(length, lambda off, piece: _copy_rows(
            xl_hbm, tau * (SLOTS * ROW_TILE) + local + off, buf, a - lo + off, piece, buf_sem).start())

    def start_runs_loop(t, first, stop, buf, buf_sem):
        def run(tau, c):
            start_run(t, tau, True, buf, buf_sem)
            return c
        lax.fori_loop(first, stop, run, 0)

    def clear(buf):
        buf[...] = jnp.zeros_like(buf)

    @pl.when(i == 0)
    def _():
        for buf in xbufs:
            clear(buf)
        for t in range(ring - 1):
            @pl.when(tvalid_ref[t] > 0)
            def _(t=t):
                start_runs_loop(t, tlo_ref[t], thi_ref[t] + 1, xbufs[t], sem.at[t])

    @pl.when((i == 0) | (te_ref[i] != te_ref[jnp.maximum(i - 1, 0)]))
    def _():
        wg_s[...] = wg_ref[0].astype(bf16)
        wu_s[...] = wu_ref[0].astype(bf16)
        wd_s[...] = wd_ref[0].astype(bf16)

    valid = tvalid_ref[i]

    def tile_step(cur):
        ahead = (cur + ring - 1) % ring
        buf, buf_sem, next_buf, next_sem = xbufs[cur], sem.at[cur], xbufs[ahead], sem.at[ahead]
        _for_each_piece(valid, lambda off, piece: _copy_rows(
            xl_hbm, 0, buf, 0, piece, buf_sem).wait())

        nxt = jnp.minimum(i + ring - 1, last)
        go = (i + ring - 1 <= last) & (tvalid_ref[nxt] > 0)
        first, final = tlo_ref[nxt], thi_ref[nxt]
        for j in range(UNROLLED_RUNS):
            start_run(nxt, jnp.minimum(first + j, token_tiles - 1), go & (first + j <= final), next_buf, next_sem)

        x = _load_row_major(buf, ROW_TILE).astype(bf16)
        clear(buf)
        hg = jnp.dot(x, wg_s[...], preferred_element_type=f32)
        hu = jnp.dot(x, wu_s[...], preferred_element_type=f32)
        y = jnp.dot((hg * jax.nn.sigmoid(hg) * hu).astype(bf16), wd_s[...], preferred_element_type=f32)
        _store_row_major(o_ref, 0, y)

        @pl.when(go & (final - first >= UNROLLED_RUNS))
        def _():
            start_runs_loop(nxt, first + UNROLLED_RUNS, final + 1, next_buf, next_sem)

    for cur in range(ring):
        pl.when((valid > 0) & (i % ring == cur))(functools.partial(tile_step, cur))

    @pl.when(valid == 0)
    def _():
        o_ref[...] = jnp.zeros_like(o_ref)


def _experts(tables, xl, wg, wu, wd, tiles):
    wspec = lambda a: pl.BlockSpec((1,) + a.shape[1:], lambda i, te, *_: (te[i], 0, 0))
    return pl.pallas_call(
        _experts_body,
        grid_spec=pltpu.PrefetchScalarGridSpec(
            num_scalar_prefetch=len(tables),
            grid=(tiles,),
            in_specs=[pl.BlockSpec(memory_space=pl.ANY), wspec(wg), wspec(wu), wspec(wd)],
            out_specs=pl.BlockSpec((ROW_TILE * PIECES, LANES), lambda i, *_: (i, 0)),
            scratch_shapes=[pltpu.VMEM(wg.shape[1:], bf16), pltpu.VMEM(wu.shape[1:], bf16), pltpu.VMEM(wd.shape[1:], bf16),
                            pltpu.SemaphoreType.DMA((EXPERT_RING,))]
            + [pltpu.VMEM((ROW_TILE * PIECES, LANES), f32)] * EXPERT_RING),
        out_shape=jax.ShapeDtypeStruct((tiles * ROW_TILE * PIECES, LANES), f32),
        compiler_params=_cparams(),
        name="moe_experts",
    )(*tables, xl, wg, wu, wd)


def _combine_body(n_ref, gpos_ref, loff_ref, ys_hbm, x_ref, w_ref, l_ref, g_ref, op_ref, os_ref, sem, *ybufs,
                  prompt_tiles):
    i = pl.program_id(0)
    last = pl.num_programs(0) - 1
    tile_rows = SLOTS * ROW_TILE
    ring = len(ybufs)

    def start_run(t, e, enabled, buf, buf_sem, pieces=RUN_PIECES):
        k = t * N_EXPERTS + e
        _for_each_piece(jnp.where(enabled, n_ref[k], 0), lambda off, piece: _copy_rows(
            ys_hbm, gpos_ref[k] + off, buf, loff_ref[k] + off, piece, buf_sem).start(), pieces)

    @pl.when(i == 0)
    def _():
        for t in range(ring - 1):
            def run(e, c, t=t):
                start_run(t, e, True, ybufs[t], sem.at[t])
                return c
            lax.fori_loop(0, N_EXPERTS, run, 0)

    def step(cur):
        ahead = (cur + ring - 1) % ring
        buf, buf_sem, next_buf, next_sem = ybufs[cur], sem.at[cur], ybufs[ahead], sem.at[ahead]
        _copy_rows(ys_hbm, 0, buf, 0, tile_rows, buf_sem).wait()
        nxt, go = jnp.minimum(i + ring - 1, last), i + ring - 1 <= last
        longest = 0
        for e in range(N_EXPERTS):
            start_run(nxt, e, go, next_buf, next_sem, SHORT_PIECES)
            longest = longest | n_ref[nxt * N_EXPERTS + e]
        yl = _load_row_major(buf, tile_rows).astype(bf16)
        sorted_row = lax.broadcasted_iota(i32, (ROW_TILE, tile_rows), 1).astype(f32)
        w, lp = w_ref[...], l_ref[...]
        y1 = jnp.dot((sorted_row == lp[:, 0:1]).astype(bf16), yl, preferred_element_type=f32)
        y2 = jnp.dot((sorted_row == lp[:, 1:2]).astype(bf16), yl, preferred_element_type=f32)
        out = _rms(x_ref[...] + (w[:, 0:1] * y1 + w[:, 1:2] * y2), g_ref[...])

        @pl.when(i < prompt_tiles)
        def _():
            op_ref[...] = out

        @pl.when(i >= prompt_tiles)
        def _():
            os_ref[...] = out

        @pl.when(go & (longest >= 2 * SHORT_PIECES[0]))
        def _():
            def long_pieces(e, c):
                start_run(nxt, e, True, next_buf, next_sem, LONG_PIECES)
                return c
            lax.fori_loop(0, N_EXPERTS, long_pieces, 0)

    for cur in range(ring):
        pl.when(i % ring == cur)(functools.partial(step, cur))


def _combine(tables, ys, x1, wts, lpos, g, prompt_rows):
    prompt_tiles = prompt_rows // ROW_TILE
    tiles = x1.shape[0] // ROW_TILE
    row = lambda n: pl.BlockSpec((ROW_TILE, n), lambda i, *_: (i, 0))
    p_rows = lambda n: pl.BlockSpec((ROW_TILE, n), lambda i, *_: (jnp.minimum(i, prompt_tiles - 1), 0))
    s_rows = lambda n: pl.BlockSpec((ROW_TILE, n), lambda i, *_: (jnp.maximum(i - prompt_tiles, 0), 0))
    return pl.pallas_call(
        functools.partial(_combine_body, prompt_tiles=prompt_tiles),
        grid_spec=pltpu.PrefetchScalarGridSpec(
            num_scalar_prefetch=len(tables),
            grid=(tiles,),
            in_specs=[pl.BlockSpec(memory_space=pl.ANY), row(D_MODEL), row(2), row(2),
                      pl.BlockSpec((1, D_MODEL), lambda i, *_: (0, 0))],
            out_specs=[p_rows(D_MODEL), s_rows(D_MODEL)],
            scratch_shapes=[pltpu.SemaphoreType.DMA((COMBINE_RING,))]
            + [pltpu.VMEM((SLOTS * ROW_TILE * PIECES, LANES), f32)] * COMBINE_RING),
        out_shape=[jax.ShapeDtypeStruct((prompt_rows, D_MODEL), f32),
                   jax.ShapeDtypeStruct((x1.shape[0] - prompt_rows, D_MODEL), f32)],
        compiler_params=_cparams(),
        name="moe_combine",
    )(*tables, ys, x1, wts, lpos, g)


def _moe_tables(n, loff, tiles):
    cum = jnp.cumsum(n, axis=0) - n
    counts = jnp.sum(n, axis=0)
    padded = (counts + ROW_TILE - 1) // ROW_TILE * ROW_TILE
    ends = jnp.cumsum(padded)
    starts = ends - padded
    first = jnp.arange(tiles, dtype=i32) * ROW_TILE
    expert = jnp.minimum(jnp.sum((first[:, None] >= ends[None, :]).astype(i32), axis=1), N_EXPERTS - 1)
    sel = expert[:, None] == jnp.arange(N_EXPERTS)[None, :]
    pick = lambda v: jnp.sum(jnp.where(sel, v[None, :], 0), axis=1)
    pos = first - pick(starts)
    valid = jnp.where(first < ends[-1], jnp.clip(pick(counts) - pos, 0, ROW_TILE), 0)
    cum_t, n_t = cum.T[expert], n.T[expert]
    touches = (cum_t + n_t > pos[:, None]) & (cum_t < (pos + ROW_TILE)[:, None]) & (n_t > 0)
    tau = jnp.arange(n.shape[0], dtype=i32)[None, :]
    lo = jnp.min(jnp.where(touches, tau, n.shape[0]), axis=1)
    hi = jnp.max(jnp.where(touches, tau, -1), axis=1)
    as_i32 = lambda v: v.astype(i32)
    flat = lambda v: v.reshape(-1).astype(i32)
    expert_tables = tuple(map(as_i32, (expert, pos, valid, lo, hi))) + (flat(n), flat(cum), flat(loff))
    combine_tables = (flat(n), flat(starts[None, :] + cum), flat(loff))
    return expert_tables, combine_tables


def kernel(x_prompt, x_sample, cache_k, cache_v, state_ssm_re, state_ssm_im, g_norm_mix, w_in, attn_sinks, ssm_a_re,
           ssm_a_im, ssm_log_dt, ssm_b_re, ssm_b_im, ssm_c_re, ssm_c_im, ssm_d, w_glu, b_glu, g_attn_out, g_ssm_out,
           w_out, g_norm_ffn, w_router_group, b_router_group, w_router_expert, b_router_expert, w_exp_gate, w_exp_up,
           w_exp_down, g_final):
    bp, lp, _ = x_prompt.shape
    bs, ls, _ = x_sample.shape
    depth = w_in.shape[0]
    assert depth == 1 and ls == SSM_STEPS and lp % ATTN_TILE == 0 and (bs * ls) % ROW_TILE == 0
    tp, ts = bp * lp, bs * ls
    wc = cache_k.shape[2]
    row2 = lambda v: v.reshape(1, -1)

    xp = x_prompt.reshape(tp, D_MODEL)
    xs = x_sample.reshape(ts, D_MODEL)
    w_in_bf = w_in[0].astype(bf16)
    qp, kp, vp, up = _proj(xp, row2(g_norm_mix[0]), w_in_bf, 512)
    qs, kq, vq, us = _proj(xs, row2(g_norm_mix[0]), w_in_bf, ts)

    sinks = attn_sinks[0]
    g_att = row2(g_attn_out[0])
    ap = _attn_prompt(qp.reshape(bp, lp, -1), kp.reshape(bp, lp, -1), vp.reshape(bp, lp, -1), sinks, g_att)
    a_s, k_roll, v_roll = _attn_sample(qs.reshape(bs, ls, -1), kq.reshape(bs, ls, -1), vq.reshape(bs, ls, -1),
                                       cache_k[0].reshape(bs, wc, KV_WIDTH), cache_v[0].reshape(bs, wc, KV_WIDTH),
                                       sinks, g_att)

    bbar, steps, lev = _ssm_tables(ssm_a_re[0], ssm_a_im[0], ssm_log_dt[0], ssm_b_re[0], ssm_b_im[0])
    c_blocks = lambda c: c.reshape(SSM_BLOCKS, LANES, SSM_STATE)
    d_row = row2(ssm_d[0])
    to_blocks = lambda h: h.reshape(bs, SSM_BLOCKS, STATE_COLS).transpose(1, 0, 2)
    from_blocks = lambda h: h.transpose(1, 0, 2).reshape(bs, SSM_GROUPS, SSM_STATE)
    yp, hrp, hip, ysm, hrs, his = _ssm(up.reshape(bp, lp, -1), us.reshape(1, ts, -1), to_blocks(state_ssm_re[0]),
                                       to_blocks(state_ssm_im[0]), bbar, c_blocks(ssm_c_re[0]),
                                       c_blocks(ssm_c_im[0]), steps, lev, d_row)

    wr = jnp.zeros((D_MODEL, ROUTER_COLS), f32)
    wr = wr.at[:, :N_EXPERTS].set(w_router_expert[0]).at[:, N_EXPERTS:N_EXPERTS + N_EXPERT_GROUPS].set(w_router_group[0])
    br = jnp.zeros((1, ROUTER_COLS), f32)
    br = br.at[0, :N_EXPERTS].set(b_router_expert[0]).at[0, N_EXPERTS:N_EXPERTS + N_EXPERT_GROUPS].set(b_router_group[0])
    tri = jnp.tril(jnp.ones((ROW_TILE, ROW_TILE), bf16), -1)
    upper = jnp.triu(jnp.ones((N_EXPERTS, N_EXPERTS), bf16), 1)
    mix_w = (w_glu[0].astype(bf16), row2(b_glu[0]), row2(g_ssm_out[0]), w_out[0].astype(bf16), row2(g_norm_ffn[0]),
             wr.astype(bf16), br, tri, upper)
    x1, xl, wts, lpos, n_rows, n_off = _mix(ap.reshape(tp, -1), a_s.reshape(ts, -1), yp.reshape(tp, -1),
                                            ysm.reshape(ts, -1), xp, xs, *mix_w)

    per_tile = lambda v: v.reshape(-1, N_EXPERTS).astype(i32)
    tiles = (SLOTS * (tp + ts)) // ROW_TILE + N_EXPERTS
    expert_tables, combine_tables = _moe_tables(per_tile(n_rows), per_tile(n_off), tiles)
    expert_out = _experts(expert_tables, xl, w_exp_gate[0], w_exp_up[0], w_exp_down[0], tiles)
    y_p, y_s = _combine(combine_tables, expert_out, x1, wts, lpos, row2(g_final), tp)

    kvshape = lambda a, b: a.reshape(1, b, -1, N_KV_HEADS, HEAD_DIM)
    block_state = lambda h: h.reshape(bp, SSM_GROUPS, SSM_STATE)[None]
    wcp = min(WINDOW, lp)
    return (y_p.reshape(bp, lp, D_MODEL), y_s.reshape(bs, ls, D_MODEL),
            kvshape(kp.reshape(bp, lp, -1)[:, lp - wcp:], bp), kvshape(vp.reshape(bp, lp, -1)[:, lp - wcp:], bp),
            block_state(hrp), block_state(hip),
            kvshape(k_roll, bs), kvshape(v_roll, bs),
            from_blocks(hrs)[None], from_blocks(his)[None])
```

```python
import functools
import math

import jax
import jax.numpy as jnp
from jax import lax
from jax.experimental import pallas as pl
from jax.experimental.pallas import tpu as pltpu

f32, bf16, i32 = jnp.float32, jnp.bfloat16, jnp.int32

D_MODEL = 1024
CHUNK = 64
N_BACK = 2
WINDOW = 128
ATTN_WIDTH = 512
HEAD_DIM = 64
N_KV_HEADS = 2
Q_PER_KV = 4
KV_WIDTH = 128
SSM_WIDTH = 512
SSM_GROUP = 16
SSM_GROUPS = 32
SSM_STATE = 64
PROJ_WIDTH = 1280
N_EXPERT_GROUPS = 4
EXPERTS_PER_GROUP = 8
N_EXPERTS = 32
D_EXPERT = 512
EPS = 1e-6
NEG = -1e30

LANES = 128
SSM_STEPS = 16
SSM_BLOCKS = SSM_WIDTH // LANES
GROUPS_PER_BLOCK = LANES // SSM_GROUP
STATE_COLS = GROUPS_PER_BLOCK * SSM_STATE
ROW_TILE = 256
SLOTS = 2
PIECES = D_MODEL // LANES
RUN_PIECES = tuple(1 << b for b in reversed(range(int(math.log2(ROW_TILE)) + 1)))
SHORT_PIECES = tuple(p for p in RUN_PIECES if p <= 32)
LONG_PIECES = tuple(p for p in RUN_PIECES if p > 32)
MIX_TILES = 2
UNROLLED_RUNS = 20
EXPERT_RING = 3
COMBINE_RING = 3
VMEM_LIMIT = 56 * 1024 * 1024


def _cparams(n_axes=1, limit=VMEM_LIMIT):
    return pltpu.CompilerParams(dimension_semantics=("arbitrary",) * n_axes, vmem_limit_bytes=limit)


def _rms(x, g):
    return x * lax.rsqrt(jnp.mean(x * x, axis=-1, keepdims=True) + EPS) * g


def _bdot(a, b):
    return jnp.dot(a.astype(bf16), b.astype(bf16), preferred_element_type=f32)


def _proj_body(x_ref, g_ref, w_ref, q_ref, k_ref, v_ref, u_ref):
    h = _rms(x_ref[...], g_ref[...])
    z = _bdot(h, w_ref[...])
    q_ref[...] = z[:, :ATTN_WIDTH] * (HEAD_DIM ** -0.5)
    k_ref[...] = z[:, ATTN_WIDTH:ATTN_WIDTH + KV_WIDTH]
    v_ref[...] = z[:, ATTN_WIDTH + KV_WIDTH:ATTN_WIDTH + 2 * KV_WIDTH]
    u_ref[...] = z[:, ATTN_WIDTH + 2 * KV_WIDTH:]


def _proj(x2d, g, w_bf, tm):
    t = x2d.shape[0]
    row = lambda n: pl.BlockSpec((tm, n), lambda i: (i, 0))
    full = lambda a: pl.BlockSpec(a.shape, lambda i: (0,) * a.ndim)
    return pl.pallas_call(
        _proj_body,
        grid=(t // tm,),
        in_specs=[row(D_MODEL), full(g), full(w_bf)],
        out_specs=[row(ATTN_WIDTH), row(KV_WIDTH), row(KV_WIDTH), row(SSM_WIDTH)],
        out_shape=[jax.ShapeDtypeStruct((t, n), f32) for n in (ATTN_WIDTH, KV_WIDTH, KV_WIDTH, SSM_WIDTH)],
        compiler_params=_cparams(),
        name="proj",
    )(x2d, g, w_bf)


def _sink_column(sink_ref, kv, rows_per_head):
    r = lax.broadcasted_iota(i32, (Q_PER_KV * rows_per_head, 1), 0)
    col = jnp.full((Q_PER_KV * rows_per_head, 1), sink_ref[kv * Q_PER_KV], f32)
    for j in range(1, Q_PER_KV):
        col = jnp.where(r >= j * rows_per_head, sink_ref[kv * Q_PER_KV + j], col)
    return col


def _attend(qs, kc, vc, sink_col, valid):
    s = lax.dot_general(qs.astype(bf16), kc.astype(bf16), (((1,), (1,)), ((), ())), preferred_element_type=f32)
    if valid is not None:
        s = jnp.where(valid, s, NEG)
    m = jnp.maximum(jnp.max(s, axis=-1, keepdims=True), sink_col)
    p = jnp.exp(s - m)
    denom = jnp.sum(p, axis=-1, keepdims=True) + jnp.exp(sink_col - m)
    return _bdot(p, vc) / denom


def _heads_attend(q, k, v, sink_ref, valid):
    rows = q.shape[0]
    pieces = []
    for kv in range(N_KV_HEADS):
        qs = jnp.concatenate(
            [q[:, (kv * Q_PER_KV + j) * HEAD_DIM:(kv * Q_PER_KV + j + 1) * HEAD_DIM] for j in range(Q_PER_KV)], axis=0)
        o = _attend(qs, k[:, kv * HEAD_DIM:(kv + 1) * HEAD_DIM], v[:, kv * HEAD_DIM:(kv + 1) * HEAD_DIM],
                    _sink_column(sink_ref, kv, rows), valid)
        pieces += [o[j * rows:(j + 1) * rows] for j in range(Q_PER_KV)]
    return jnp.concatenate(pieces, axis=1)


ATTN_TILE = 256
CHUNKS_PER_TILE = ATTN_TILE // CHUNK
KEY_SPAN = (N_BACK + 1) * CHUNK


def _attn_prompt_body(sink_ref, q_ref, kp_ref, kc_ref, vp_ref, vc_ref, g_ref, o_ref):
    i = pl.program_id(1)
    kwin = jnp.concatenate([kp_ref[0], kc_ref[0]], axis=0)
    vwin = jnp.concatenate([vp_ref[0], vc_ref[0]], axis=0)
    key_chunk = lax.broadcasted_iota(i32, (1, KEY_SPAN), 1) // CHUNK
    for c in range(CHUNKS_PER_TILE):
        valid = (i * CHUNKS_PER_TILE + c - N_BACK + key_chunk) >= 0
        o = _heads_attend(q_ref[0, c * CHUNK:(c + 1) * CHUNK, :], kwin[c * CHUNK:c * CHUNK + KEY_SPAN],
                          vwin[c * CHUNK:c * CHUNK + KEY_SPAN], sink_ref, valid)
        o_ref[0, c * CHUNK:(c + 1) * CHUNK, :] = _rms(o, g_ref[...]).astype(bf16)


def _attn_prompt(q, k, v, sinks, g):
    b, l, _ = q.shape
    back = N_BACK * CHUNK
    per = ATTN_TILE // back
    prev = pl.BlockSpec((1, back, KV_WIDTH), lambda bi, i: (bi, jnp.maximum(i * per - 1, 0), 0))
    cur = pl.BlockSpec((1, ATTN_TILE, KV_WIDTH), lambda bi, i: (bi, i, 0))
    return pl.pallas_call(
        _attn_prompt_body,
        grid=(b, l // ATTN_TILE),
        in_specs=[pl.BlockSpec(memory_space=pltpu.SMEM),
                  pl.BlockSpec((1, ATTN_TILE, ATTN_WIDTH), lambda bi, i: (bi, i, 0)),
                  prev, cur, prev, cur,
                  pl.BlockSpec((1, ATTN_WIDTH), lambda bi, i: (0, 0))],
        out_specs=pl.BlockSpec((1, ATTN_TILE, ATTN_WIDTH), lambda bi, i: (bi, i, 0)),
        out_shape=jax.ShapeDtypeStruct((b, l, ATTN_WIDTH), bf16),
        compiler_params=_cparams(2),
        name="attn_prompt",
    )(sinks, q, k, k, v, v, g)


def _attn_sample_body(sink_ref, q_ref, kn_ref, vn_ref, ck_ref, cv_ref, g_ref, o_ref, nk_ref, nv_ref):
    kall = jnp.concatenate([ck_ref[0], kn_ref[0]], axis=0)
    vall = jnp.concatenate([cv_ref[0], vn_ref[0]], axis=0)
    o = _heads_attend(q_ref[0], kall, vall, sink_ref, None)
    o_ref[0] = _rms(o, g_ref[...]).astype(bf16)
    n_new = kn_ref.shape[1]
    nk_ref[0] = kall[n_new:]
    nv_ref[0] = vall[n_new:]


def _attn_sample(q, k_new, v_new, cache_k, cache_v, sinks, g):
    b, l, _ = q.shape
    wc = cache_k.shape[1]
    blk = lambda r, n: pl.BlockSpec((1, r, n), lambda bi: (bi, 0, 0))
    return pl.pallas_call(
        _attn_sample_body,
        grid=(b,),
        in_specs=[pl.BlockSpec(memory_space=pltpu.SMEM), blk(l, ATTN_WIDTH), blk(l, KV_WIDTH), blk(l, KV_WIDTH),
                  blk(wc, KV_WIDTH), blk(wc, KV_WIDTH), pl.BlockSpec((1, ATTN_WIDTH), lambda bi: (0, 0))],
        out_specs=[blk(l, ATTN_WIDTH), blk(wc, KV_WIDTH), blk(wc, KV_WIDTH)],
        out_shape=[jax.ShapeDtypeStruct((b, l, ATTN_WIDTH), bf16),
                   jax.ShapeDtypeStruct((b, wc, KV_WIDTH), f32), jax.ShapeDtypeStruct((b, wc, KV_WIDTH), f32)],
        compiler_params=_cparams(),
        name="attn_sample",
    )(sinks, q, k_new, v_new, cache_k, cache_v, g)


def _ssm_tables(a_re, a_im, log_dt, b_re, b_im):
    dt = jnp.exp(log_dt)[:, None]
    lam_r, lam_i = a_re * dt, a_im * dt

    def power(n):
        mag = jnp.exp(n * lam_r)
        return jnp.stack([mag * jnp.cos(n * lam_i), mag * jnp.sin(n * lam_i)])

    ar, ai = power(1.0)
    den = a_re * a_re + a_im * a_im
    nr, ni = ar - 1.0, ai
    fr = ((nr * a_re + ni * a_im) / den)[..., None]
    fi = ((ni * a_re - nr * a_im) / den)[..., None]
    bbar = jnp.stack([fr * b_re - fi * b_im, fr * b_im + fi * b_re])
    bbar = bbar.transpose(0, 1, 3, 2).reshape(2, SSM_BLOCKS, LANES, SSM_STATE)
    by_block = lambda t: t.reshape(t.shape[:-2] + (SSM_BLOCKS, STATE_COLS))
    steps = by_block(power(jnp.arange(SSM_STEPS + 1, dtype=f32)[:, None, None]))
    levels = by_block(power(SSM_STEPS * 2.0 ** jnp.arange(8, dtype=f32)[:, None, None]))
    return bbar, steps.transpose(2, 0, 1, 3), levels.transpose(2, 1, 0, 3)


def _ssm_chunk_rows(u_ref, nk):
    xs = [u_ref[0, pl.ds(s, nk, stride=SSM_STEPS), :] for s in range(SSM_STEPS)]
    pairs = [jnp.concatenate([xs[2 * p], xs[2 * p + 1]], axis=1).astype(bf16) for p in range(SSM_STEPS // 2)]
    return xs, pairs


def _ssm_intra(pairs, toep_ref, nk):
    nd = len(pairs)
    y = [None] * nd
    for d in range(nd):
        lhs = jnp.concatenate(pairs[:nd - d], axis=0) if nd - d > 1 else pairs[0]
        r = jnp.dot(lhs, toep_ref[d], preferred_element_type=f32)
        for p in range(nd - d):
            blk = r[p * nk:(p + 1) * nk]
            y[p + d] = blk if y[p + d] is None else y[p + d] + blk
    return y


def _shift_rows(x, sh):
    rows = lax.broadcasted_iota(i32, (x.shape[0], 1), 0)
    return jnp.where(rows >= sh, pltpu.roll(x, sh, axis=0), 0.0)


def _group_index(shape, axis, sub):
    idx = lax.broadcasted_iota(i32, shape, axis)
    return jnp.right_shift(idx, int(math.log2(sub))) & (GROUPS_PER_BLOCK - 1)


def _build_ssm_weights(bb_ref, cr_ref, ci_ref, pw_ref, toep_s, wout_s, win_s, w_low):
    hp = lax.Precision.HIGHEST
    k = lax.broadcasted_iota(i32, (SSM_STATE, STATE_COLS), 0)
    n = lax.broadcasted_iota(i32, (SSM_STATE, STATE_COLS), 1)
    spread = ((n & (SSM_STATE - 1)) == k).astype(f32)
    own = _group_index((LANES, STATE_COLS), 0, SSM_GROUP) == _group_index((LANES, STATE_COLS), 1, SSM_STATE)

    def block_diag(compact):
        return jnp.where(own, jnp.dot(compact, spread, precision=hp, preferred_element_type=f32), 0.0)

    bb_r, bb_i = block_diag(bb_ref[0, 0]), block_diag(bb_ref[1, 0])
    c_r, c_i = block_diag(cr_ref[0]), block_diag(ci_ref[0])
    c_stack = jnp.concatenate([c_r.T, -c_i.T], axis=0)
    power = lambda e: (pw_ref[0, 0, e:e + 1, :], pw_ref[0, 1, e:e + 1, :])
    for s in range(SSM_STEPS):
        rows = slice(s * LANES, (s + 1) * LANES)
        pr, pi = power(SSM_STEPS - 1 - s)
        w = jnp.concatenate([bb_r * pr - bb_i * pi, bb_r * pi + bb_i * pr], axis=1)
        wout_s[rows, :] = w.astype(bf16)
        w_low[rows, :] = (w - w.astype(bf16).astype(f32)).astype(bf16)
        pr, pi = power(s + 1)
        g = jnp.concatenate([c_r * pr - c_i * pi, -(c_r * pi + c_i * pr)], axis=1)
        win_s[:, rows] = g.T.astype(bf16)
    c_high = c_stack.astype(bf16)
    c_low = (c_stack - c_high.astype(f32)).astype(bf16)
    lags = (jnp.dot(wout_s[...], c_high, preferred_element_type=f32)
            + (jnp.dot(wout_s[...], c_low, preferred_element_type=f32)
               + jnp.dot(w_low[...], c_high, preferred_element_type=f32)))
    lag_kernel = [lags[(SSM_STEPS - 1 - lag) * LANES:(SSM_STEPS - lag) * LANES] for lag in range(SSM_STEPS)]
    zero = jnp.zeros((LANES, LANES), f32)
    for d in range(SSM_STEPS // 2):
        top = jnp.concatenate([lag_kernel[2 * d], lag_kernel[2 * d + 1]], axis=1)
        bottom = jnp.concatenate([lag_kernel[2 * d - 1] if d > 0 else zero, lag_kernel[2 * d]], axis=1)
        toep_s[d] = jnp.concatenate([top, bottom], axis=0).astype(bf16)


def _ssm_chunks(u_ref, d_ref, y_ref, toep_s, wout_s, win_s, entry_state):
    nk = u_ref.shape[1] // SSM_STEPS
    xs, pairs = _ssm_chunk_rows(u_ref, nk)
    y = _ssm_intra(pairs, toep_s, nk)
    s = jnp.dot(jnp.concatenate(pairs, axis=1), wout_s[...], preferred_element_type=f32)
    hprev, hr, hi = entry_state(s[:, :STATE_COLS], s[:, STATE_COLS:])
    y2 = _bdot(hprev, win_s[...])
    for st in range(SSM_STEPS):
        piece = (y[st // 2][:, (st % 2) * LANES:(st % 2 + 1) * LANES] + y2[:, st * LANES:(st + 1) * LANES]
                 + d_ref[...] * xs[st])
        y_ref[0, pl.ds(st, nk, stride=SSM_STEPS), :] = piece
    return hr, hi


def _ssm_body(u_ref, us_ref, h0r_ref, h0i_ref, bb_ref, cr_ref, ci_ref, pw_ref, lev_ref, d_ref,
              y_ref, hr_ref, hi_ref, ys_ref, hrs_ref, his_ref, toep_s, wout_s, win_s, w_low):
    @pl.when(pl.program_id(1) == 0)
    def _():
        _build_ssm_weights(bb_ref, cr_ref, ci_ref, pw_ref, toep_s, wout_s, win_s, w_low)

        def one_chunk(sr, si):
            h0r, h0i = h0r_ref[0], h0i_ref[0]
            ar, ai = lev_ref[0, 0, 0:1, :], lev_ref[0, 0, 1:2, :]
            return (jnp.concatenate([h0r, h0i], axis=1), sr + ar * h0r - ai * h0i, si + ar * h0i + ai * h0r)
        hrs_ref[0], his_ref[0] = _ssm_chunks(us_ref, d_ref, ys_ref, toep_s, wout_s, win_s, one_chunk)

    def scan_chunks(sr, si):
        nk = sr.shape[0]
        level = 0
        while (1 << level) < nk:
            ar, ai = lev_ref[0, level, 0:1, :], lev_ref[0, level, 1:2, :]
            tr, ti = _shift_rows(sr, 1 << level), _shift_rows(si, 1 << level)
            sr, si = sr + ar * tr - ai * ti, si + ar * ti + ai * tr
            level += 1
        return (jnp.concatenate([_shift_rows(sr, 1), _shift_rows(si, 1)], axis=1), sr[nk - 1:nk], si[nk - 1:nk])
    hr_ref[0, 0], hi_ref[0, 0] = _ssm_chunks(u_ref, d_ref, y_ref, toep_s, wout_s, win_s, scan_chunks)


def _ssm(u, us, h0r, h0i, bbar, c_re, c_im, steps, lev, d):
    b, l, _ = u.shape
    rows = us.shape[1]
    nb = rows // SSM_STEPS
    wspec = lambda a: pl.BlockSpec((1,) + a.shape[1:], lambda j, bi: (j,) + (0,) * (a.ndim - 1))
    st = pl.BlockSpec((1, 1, 1, STATE_COLS), lambda j, bi: (bi, j, 0, 0))
    sst = pl.BlockSpec((1, nb, STATE_COLS), lambda j, bi: (j, 0, 0))
    seq = pl.BlockSpec((1, l, LANES), lambda j, bi: (bi, 0, j))
    sseq = pl.BlockSpec((1, rows, LANES), lambda j, bi: (0, 0, j))
    return pl.pallas_call(
        _ssm_body,
        grid=(SSM_BLOCKS, b),
        in_specs=[seq, sseq, sst, sst, pl.BlockSpec((2, 1, LANES, SSM_STATE), lambda j, bi: (0, j, 0, 0)),
                  wspec(c_re), wspec(c_im), wspec(steps), wspec(lev), pl.BlockSpec((1, LANES), lambda j, bi: (0, j))],
        out_specs=[seq, st, st, sseq, sst, sst],
        out_shape=[jax.ShapeDtypeStruct((b, l, SSM_WIDTH), f32),
                   jax.ShapeDtypeStruct((b, SSM_BLOCKS, 1, STATE_COLS), f32),
                   jax.ShapeDtypeStruct((b, SSM_BLOCKS, 1, STATE_COLS), f32),
                   jax.ShapeDtypeStruct((1, rows, SSM_WIDTH), f32),
                   jax.ShapeDtypeStruct((SSM_BLOCKS, nb, STATE_COLS), f32),
                   jax.ShapeDtypeStruct((SSM_BLOCKS, nb, STATE_COLS), f32)],
        scratch_shapes=[pltpu.VMEM((SSM_STEPS // 2, 2 * LANES, 2 * LANES), bf16),
                        pltpu.VMEM((SSM_STEPS * LANES, 2 * STATE_COLS), bf16),
                        pltpu.VMEM((2 * STATE_COLS, SSM_STEPS * LANES), bf16),
                        pltpu.VMEM((SSM_STEPS * LANES, 2 * STATE_COLS), bf16)],
        compiler_params=_cparams(2),
        name="ssm",
    )(u, us, h0r, h0i, bbar, c_re, c_im, steps, lev, d)


ROUTER_COLS = LANES


def _mix_body(ap_ref, as_ref, yp_ref, ys_ref, xp_ref, xs_ref, wglu_ref, bglu_ref, gs_ref, wout_ref, gf_ref, wr_ref,
              br_ref, tri_ref, upper_ref, x1_ref, xl_ref, wts_ref, lpos_ref, n_ref, loff_ref, *, prompt_steps):
    is_prompt = pl.program_id(0) < prompt_steps
    pick_rows = lambda p_ref, s_ref: jnp.where(is_prompt, p_ref[...], jnp.concatenate([s_ref[...]] * MIX_TILES, axis=0))
    y = pick_rows(yp_ref, ys_ref)
    y = 0.5 * y * (1.0 + jnp.tanh(math.sqrt(2.0 / math.pi) * (y + 0.044715 * (y * y * y))))
    y = y * jax.nn.sigmoid(_bdot(y, wglu_ref[...]) + bglu_ref[...])
    attn = jnp.where(is_prompt, ap_ref[...].astype(f32),
                     jnp.concatenate([as_ref[...].astype(f32)] * MIX_TILES, axis=0)).astype(bf16)
    cat = jnp.concatenate([attn, _rms(y, gs_ref[...]).astype(bf16)], axis=1)
    x1 = pick_rows(xp_ref, xs_ref) + jnp.dot(cat, wout_ref[...], preferred_element_type=f32)
    x1_ref[...] = x1
    hf = _rms(x1, gf_ref[...])

    logits = _bdot(hf, wr_ref[...]) + br_ref[...]
    le = logits[:, :N_EXPERTS]
    lg = logits[:, N_EXPERTS:N_EXPERTS + N_EXPERT_GROUPS]
    tm = le.shape[0]
    gmax = jnp.max(lg, axis=-1, keepdims=True)
    gi = lax.broadcasted_iota(i32, (tm, N_EXPERT_GROUPS), 1).astype(f32)
    gsel = jnp.min(jnp.where(lg == gmax, gi, float(N_EXPERT_GROUPS)), axis=-1, keepdims=True)
    pg = 1.0 / jnp.sum(jnp.exp(lg - gmax), axis=-1, keepdims=True)
    ei_int = lax.broadcasted_iota(i32, (tm, N_EXPERTS), 1)
    ei = ei_int.astype(f32)
    egroup = jnp.right_shift(ei_int, int(math.log2(EXPERTS_PER_GROUP))).astype(f32)
    lm = jnp.where(egroup == gsel, le, NEG)
    v1 = jnp.max(lm, axis=-1, keepdims=True)
    i1 = jnp.min(jnp.where(lm == v1, ei, float(N_EXPERTS)), axis=-1, keepdims=True)
    lm2 = jnp.where(ei == i1, NEG, lm)
    v2 = jnp.max(lm2, axis=-1, keepdims=True)
    i2 = jnp.min(jnp.where(lm2 == v2, ei, float(N_EXPERTS)), axis=-1, keepdims=True)
    ex = jnp.exp(v2 - v1)
    wts_ref[...] = jnp.concatenate([pg / (1.0 + ex), pg * ex / (1.0 + ex)], axis=1)

    oh1 = (ei == i1).astype(f32)
    oh2 = (ei == i2).astype(f32)
    hf_bf = hf.astype(bf16)
    sorted_row = lax.broadcasted_iota(i32, (ROW_TILE, SLOTS * ROW_TILE), 1).astype(f32)
    for h in range(MIX_TILES):
        rows = slice(h * ROW_TILE, (h + 1) * ROW_TILE)
        both = (oh1[rows] + oh2[rows]).astype(bf16)
        before = jnp.dot(tri_ref[...], both, preferred_element_type=f32)
        count = jnp.sum(oh1[rows] + oh2[rows], axis=0, keepdims=True)
        lower = jnp.sum(jnp.dot(both, upper_ref[...], preferred_element_type=f32), axis=0, keepdims=True)
        lp1 = jnp.sum(oh1[rows] * (before + lower), axis=-1, keepdims=True)
        lp2 = jnp.sum(oh2[rows] * (before + lower), axis=-1, keepdims=True)
        lpos_ref[rows, :] = jnp.concatenate([lp1, lp2], axis=1)
        n_ref[h] = count
        loff_ref[h] = lower
        pick = ((sorted_row == lp1) | (sorted_row == lp2)).astype(bf16)
        xl = lax.dot_general(pick, hf_bf[rows], (((0,), (0,)), ((), ())), preferred_element_type=f32)
        _store_row_major(xl_ref, h * SLOTS * ROW_TILE, xl)


def _mix(a_p, a_s, ys_p, ys_s, x_p, x_s, wglu, bglu, gs, wout, gf, wr, br, tri, upper):
    tm = MIX_TILES * ROW_TILE
    assert x_p.shape[0] % tm == 0 and x_s.shape[0] == ROW_TILE
    prompt_steps = x_p.shape[0] // tm
    t = x_p.shape[0] + x_s.shape[0]
    p_rows = lambda n: pl.BlockSpec((tm, n), lambda i: (jnp.minimum(i, prompt_steps - 1), 0))
    s_rows = lambda n: pl.BlockSpec((ROW_TILE, n), lambda i: (0, 0))
    row = lambda n: pl.BlockSpec((tm, n), lambda i: (i, 0))
    full = lambda arr: pl.BlockSpec(arr.shape, lambda i: (0,) * arr.ndim)
    per_tile = pl.BlockSpec((MIX_TILES, 1, N_EXPERTS), lambda i: (i, 0, 0))
    return pl.pallas_call(
        functools.partial(_mix_body, prompt_steps=prompt_steps),
        grid=(prompt_steps + 1,),
        in_specs=[p_rows(ATTN_WIDTH), s_rows(ATTN_WIDTH), p_rows(SSM_WIDTH), s_rows(SSM_WIDTH), p_rows(D_MODEL),
                  s_rows(D_MODEL), full(wglu), full(bglu), full(gs), full(wout), full(gf), full(wr), full(br),
                  full(tri), full(upper)],
        out_specs=[row(D_MODEL), pl.BlockSpec((SLOTS * tm * PIECES, LANES), lambda i: (i, 0)), row(2), row(2),
                   per_tile, per_tile],
        out_shape=[jax.ShapeDtypeStruct((t, D_MODEL), f32),
                   jax.ShapeDtypeStruct((SLOTS * t * PIECES, LANES), f32),
                   jax.ShapeDtypeStruct((t, 2), f32), jax.ShapeDtypeStruct((t, 2), f32),
                   jax.ShapeDtypeStruct((t // ROW_TILE, 1, N_EXPERTS), f32),
                   jax.ShapeDtypeStruct((t // ROW_TILE, 1, N_EXPERTS), f32)],
        compiler_params=_cparams(),
        name="mix",
    )(a_p, a_s, ys_p, ys_s, x_p, x_s, wglu, bglu, gs, wout, gf, wr, br, tri, upper)


def _store_row_major(ref, first_row, x):
    for c in range(PIECES):
        ref[pl.ds(first_row * PIECES + c, x.shape[0], stride=PIECES), :] = x[:, c * LANES:(c + 1) * LANES]


def _load_row_major(ref, n_rows):
    return jnp.concatenate([ref[pl.ds(c, n_rows, stride=PIECES), :] for c in range(PIECES)], axis=1)


def _copy_rows(src, s_row, dst, d_row, n_rows, sem):
    return pltpu.make_async_copy(src.at[pl.ds(pl.multiple_of(s_row * PIECES, PIECES), n_rows * PIECES), :],
                                 dst.at[pl.ds(pl.multiple_of(d_row * PIECES, PIECES), n_rows * PIECES), :], sem)


def _for_each_piece(n, fn, pieces=RUN_PIECES):
    off = 0 if pieces[0] == RUN_PIECES[0] else n & ~(2 * pieces[0] - 1)
    for piece in pieces:
        @pl.when((n & piece) != 0)
        def _(off=off, piece=piece):
            fn(off, piece)
        off = off + (n & piece)


def _experts_body(te_ref, tpos_ref, tvalid_ref, tlo_ref, thi_ref, n_ref, cum_ref, loff_ref,
                  xl_hbm, wg_ref, wu_ref, wd_ref, o_ref, wg_s, wu_s, wd_s, sem, *xbufs):
    i = pl.program_id(0)
    last = pl.num_programs(0) - 1
    token_tiles = n_ref.shape[0] // N_EXPERTS
    ring = len(xbufs)

    def start_run(t, tau, enabled, buf, buf_sem):
        e, lo = te_ref[t], tpos_ref[t]
        k = tau * N_EXPERTS + e
        s, n = cum_ref[k], n_ref[k]
        a = jnp.maximum(s, lo)
        length = jnp.where(enabled, jnp.maximum(jnp.minimum(s + n, lo + ROW_TILE) - a, 0), 0)
        local = loff_ref[k] + (a - s)
        _for_each_piece(length, lambda off, piece: _copy_rows(
            xl_hbm, tau * (SLOTS * ROW_TILE) + local + off, buf, a - lo + off, piece, buf_sem).start())

    def start_runs_loop(t, first, stop, buf, buf_sem):
        def run(tau, c):
            start_run(t, tau, True, buf, buf_sem)
            return c
        lax.fori_loop(first, stop, run, 0)

    def clear(buf):
        buf[...] = jnp.zeros_like(buf)

    @pl.when(i == 0)
    def _():
        for buf in xbufs:
            clear(buf)
        for t in range(ring - 1):
            @pl.when(tvalid_ref[t] > 0)
            def _(t=t):
                start_runs_loop(t, tlo_ref[t], thi_ref[t] + 1, xbufs[t], sem.at[t])

    @pl.when((i == 0) | (te_ref[i] != te_ref[jnp.maximum(i - 1, 0)]))
    def _():
        wg_s[...] = wg_ref[0].astype(bf16)
        wu_s[...] = wu_ref[0].astype(bf16)
        wd_s[...] = wd_ref[0].astype(bf16)

    valid = tvalid_ref[i]

    def tile_step(cur):
        ahead = (cur + ring - 1) % ring
        buf, buf_sem, next_buf, next_sem = xbufs[cur], sem.at[cur], xbufs[ahead], sem.at[ahead]
        _for_each_piece(valid, lambda off, piece: _copy_rows(
            xl_hbm, 0, buf, 0, piece, buf_sem).wait())

        nxt = jnp.minimum(i + ring - 1, last)
        go = (i + ring - 1 <= last) & (tvalid_ref[nxt] > 0)
        first, final = tlo_ref[nxt], thi_ref[nxt]
        for j in range(UNROLLED_RUNS):
            start_run(nxt, jnp.minimum(first + j, token_tiles - 1), go & (first + j <= final), next_buf, next_sem)

        x = _load_row_major(buf, ROW_TILE).astype(bf16)
        clear(buf)
        hg = jnp.dot(x, wg_s[...], preferred_element_type=f32)
        hu = jnp.dot(x, wu_s[...], preferred_element_type=f32)
        y = jnp.dot((hg * jax.nn.sigmoid(hg) * hu).astype(bf16), wd_s[...], preferred_element_type=f32)
        _store_row_major(o_ref, 0, y)

        @pl.when(go & (final - first >= UNROLLED_RUNS))
        def _():
            start_runs_loop(nxt, first + UNROLLED_RUNS, final + 1, next_buf, next_sem)

    for cur in range(ring):
        pl.when((valid > 0) & (i % ring == cur))(functools.partial(tile_step, cur))

    @pl.when(valid == 0)
    def _():
        o_ref[...] = jnp.zeros_like(o_ref)


def _experts(tables, xl, wg, wu, wd, tiles):
    wspec = lambda a: pl.BlockSpec((1,) + a.shape[1:], lambda i, te, *_: (te[i], 0, 0))
    return pl.pallas_call(
        _experts_body,
        grid_spec=pltpu.PrefetchScalarGridSpec(
            num_scalar_prefetch=len(tables),
            grid=(tiles,),
            in_specs=[pl.BlockSpec(memory_space=pl.ANY), wspec(wg), wspec(wu), wspec(wd)],
            out_specs=pl.BlockSpec((ROW_TILE * PIECES, LANES), lambda i, *_: (i, 0)),
            scratch_shapes=[pltpu.VMEM(wg.shape[1:], bf16), pltpu.VMEM(wu.shape[1:], bf16), pltpu.VMEM(wd.shape[1:], bf16),
                            pltpu.SemaphoreType.DMA((EXPERT_RING,))]
            + [pltpu.VMEM((ROW_TILE * PIECES, LANES), f32)] * EXPERT_RING),
        out_shape=jax.ShapeDtypeStruct((tiles * ROW_TILE * PIECES, LANES), f32),
        compiler_params=_cparams(),
        name="moe_experts",
    )(*tables, xl, wg, wu, wd)


def _combine_body(n_ref, gpos_ref, loff_ref, ys_hbm, x_ref, w_ref, l_ref, g_ref, op_ref, os_ref, sem, *ybufs,
                  prompt_tiles):
    i = pl.program_id(0)
    last = pl.num_programs(0) - 1
    tile_rows = SLOTS * ROW_TILE
    ring = len(ybufs)

    def start_run(t, e, enabled, buf, buf_sem, pieces=RUN_PIECES):
        k = t * N_EXPERTS + e
        _for_each_piece(jnp.where(enabled, n_ref[k], 0), lambda off, piece: _copy_rows(
            ys_hbm, gpos_ref[k] + off, buf, loff_ref[k] + off, piece, buf_sem).start(), pieces)

    @pl.when(i == 0)
    def _():
        for t in range(ring - 1):
            def run(e, c, t=t):
                start_run(t, e, True, ybufs[t], sem.at[t])
                return c
            lax.fori_loop(0, N_EXPERTS, run, 0)

    def step(cur):
        ahead = (cur + ring - 1) % ring
        buf, buf_sem, next_buf, next_sem = ybufs[cur], sem.at[cur], ybufs[ahead], sem.at[ahead]
        _copy_rows(ys_hbm, 0, buf, 0, tile_rows, buf_sem).wait()
        nxt, go = jnp.minimum(i + ring - 1, last), i + ring - 1 <= last
        longest = 0
        for e in range(N_EXPERTS):
            start_run(nxt, e, go, next_buf, next_sem, SHORT_PIECES)
            longest = longest | n_ref[nxt * N_EXPERTS + e]
        yl = _load_row_major(buf, tile_rows).astype(bf16)
        sorted_row = lax.broadcasted_iota(i32, (ROW_TILE, tile_rows), 1).astype(f32)
        w, lp = w_ref[...], l_ref[...]
        y1 = jnp.dot((sorted_row == lp[:, 0:1]).astype(bf16), yl, preferred_element_type=f32)
        y2 = jnp.dot((sorted_row == lp[:, 1:2]).astype(bf16), yl, preferred_element_type=f32)
        out = _rms(x_ref[...] + (w[:, 0:1] * y1 + w[:, 1:2] * y2), g_ref[...])

        @pl.when(i < prompt_tiles)
        def _():
            op_ref[...] = out

        @pl.when(i >= prompt_tiles)
        def _():
            os_ref[...] = out

        @pl.when(go & (longest >= 2 * SHORT_PIECES[0]))
        def _():
            def long_pieces(e, c):
                start_run(nxt, e, True, next_buf, next_sem, LONG_PIECES)
                return c
            lax.fori_loop(0, N_EXPERTS, long_pieces, 0)

    for cur in range(ring):
        pl.when(i % ring == cur)(functools.partial(step, cur))


def _combine(tables, ys, x1, wts, lpos, g, prompt_rows):
    prompt_tiles = prompt_rows // ROW_TILE
    tiles = x1.shape[0] // ROW_TILE
    row = lambda n: pl.BlockSpec((ROW_TILE, n), lambda i, *_: (i, 0))
    p_rows = lambda n: pl.BlockSpec((ROW_TILE, n), lambda i, *_: (jnp.minimum(i, prompt_tiles - 1), 0))
    s_rows = lambda n: pl.BlockSpec((ROW_TILE, n), lambda i, *_: (jnp.maximum(i - prompt_tiles, 0), 0))
    return pl.pallas_call(
        functools.partial(_combine_body, prompt_tiles=prompt_tiles),
        grid_spec=pltpu.PrefetchScalarGridSpec(
            num_scalar_prefetch=len(tables),
            grid=(tiles,),
            in_specs=[pl.BlockSpec(memory_space=pl.ANY), row(D_MODEL), row(2), row(2),
                      pl.BlockSpec((1, D_MODEL), lambda i, *_: (0, 0))],
            out_specs=[p_rows(D_MODEL), s_rows(D_MODEL)],
            scratch_shapes=[pltpu.SemaphoreType.DMA((COMBINE_RING,))]
            + [pltpu.VMEM((SLOTS * ROW_TILE * PIECES, LANES), f32)] * COMBINE_RING),
        out_shape=[jax.ShapeDtypeStruct((prompt_rows, D_MODEL), f32),
                   jax.ShapeDtypeStruct((x1.shape[0] - prompt_rows, D_MODEL), f32)],
        compiler_params=_cparams(),
        name="moe_combine",
    )(*tables, ys, x1, wts, lpos, g)


def _moe_tables(n, loff, tiles):
    cum = jnp.cumsum(n, axis=0) - n
    counts = jnp.sum(n, axis=0)
    padded = (counts + ROW_TILE - 1) // ROW_TILE * ROW_TILE
    ends = jnp.cumsum(padded)
    starts = ends - padded
    first = jnp.arange(tiles, dtype=i32) * ROW_TILE
    expert = jnp.minimum(jnp.sum((first[:, None] >= ends[None, :]).astype(i32), axis=1), N_EXPERTS - 1)
    sel = expert[:, None] == jnp.arange(N_EXPERTS)[None, :]
    pick = lambda v: jnp.sum(jnp.where(sel, v[None, :], 0), axis=1)
    pos = first - pick(starts)
    valid = jnp.where(first < ends[-1], jnp.clip(pick(counts) - pos, 0, ROW_TILE), 0)
    cum_t, n_t = cum.T[expert], n.T[expert]
    touches = (cum_t + n_t > pos[:, None]) & (cum_t < (pos + ROW_TILE)[:, None]) & (n_t > 0)
    tau = jnp.arange(n.shape[0], dtype=i32)[None, :]
    lo = jnp.min(jnp.where(touches, tau, n.shape[0]), axis=1)
    hi = jnp.max(jnp.where(touches, tau, -1), axis=1)
    as_i32 = lambda v: v.astype(i32)
    flat = lambda v: v.reshape(-1).astype(i32)
    expert_tables = tuple(map(as_i32, (expert, pos, valid, lo, hi))) + (flat(n), flat(cum), flat(loff))
    combine_tables = (flat(n), flat(starts[None, :] + cum), flat(loff))
    return expert_tables, combine_tables


def kernel(x_prompt, x_sample, cache_k, cache_v, state_ssm_re, state_ssm_im, g_norm_mix, w_in, attn_sinks, ssm_a_re,
           ssm_a_im, ssm_log_dt, ssm_b_re, ssm_b_im, ssm_c_re, ssm_c_im, ssm_d, w_glu, b_glu, g_attn_out, g_ssm_out,
           w_out, g_norm_ffn, w_router_group, b_router_group, w_router_expert, b_router_expert, w_exp_gate, w_exp_up,
           w_exp_down, g_final):
    bp, lp, _ = x_prompt.shape
    bs, ls, _ = x_sample.shape
    depth = w_in.shape[0]
    assert depth == 1 and ls == SSM_STEPS and lp % ATTN_TILE == 0 and (bs * ls) % ROW_TILE == 0
    tp, ts = bp * lp, bs * ls
    wc = cache_k.shape[2]
    row2 = lambda v: v.reshape(1, -1)

    xp = x_prompt.reshape(tp, D_MODEL)
    xs = x_sample.reshape(ts, D_MODEL)
    w_in_bf = w_in[0].astype(bf16)
    qp, kp, vp, up = _proj(xp, row2(g_norm_mix[0]), w_in_bf, 512)
    qs, kq, vq, us = _proj(xs, row2(g_norm_mix[0]), w_in_bf, ts)

    sinks = attn_sinks[0]
    g_att = row2(g_attn_out[0])
    ap = _attn_prompt(qp.reshape(bp, lp, -1), kp.reshape(bp, lp, -1), vp.reshape(bp, lp, -1), sinks, g_att)
    a_s, k_roll, v_roll = _attn_sample(qs.reshape(bs, ls, -1), kq.reshape(bs, ls, -1), vq.reshape(bs, ls, -1),
                                       cache_k[0].reshape(bs, wc, KV_WIDTH), cache_v[0].reshape(bs, wc, KV_WIDTH),
                                       sinks, g_att)

    bbar, steps, lev = _ssm_tables(ssm_a_re[0], ssm_a_im[0], ssm_log_dt[0], ssm_b_re[0], ssm_b_im[0])
    c_blocks = lambda c: c.reshape(SSM_BLOCKS, LANES, SSM_STATE)
    d_row = row2(ssm_d[0])
    to_blocks = lambda h: h.reshape(bs, SSM_BLOCKS, STATE_COLS).transpose(1, 0, 2)
    from_blocks = lambda h: h.transpose(1, 0, 2).reshape(bs, SSM_GROUPS, SSM_STATE)
    yp, hrp, hip, ysm, hrs, his = _ssm(up.reshape(bp, lp, -1), us.reshape(1, ts, -1), to_blocks(state_ssm_re[0]),
                                       to_blocks(state_ssm_im[0]), bbar, c_blocks(ssm_c_re[0]),
                                       c_blocks(ssm_c_im[0]), steps, lev, d_row)

    wr = jnp.zeros((D_MODEL, ROUTER_COLS), f32)
    wr = wr.at[:, :N_EXPERTS].set(w_router_expert[0]).at[:, N_EXPERTS:N_EXPERTS + N_EXPERT_GROUPS].set(w_router_group[0])
    br = jnp.zeros((1, ROUTER_COLS), f32)
    br = br.at[0, :N_EXPERTS].set(b_router_expert[0]).at[0, N_EXPERTS:N_EXPERTS + N_EXPERT_GROUPS].set(b_router_group[0])
    tri = jnp.tril(jnp.ones((ROW_TILE, ROW_TILE), bf16), -1)
    upper = jnp.triu(jnp.ones((N_EXPERTS, N_EXPERTS), bf16), 1)
    mix_w = (w_glu[0].astype(bf16), row2(b_glu[0]), row2(g_ssm_out[0]), w_out[0].astype(bf16), row2(g_norm_ffn[0]),
             wr.astype(bf16), br, tri, upper)
    x1, xl, wts, lpos, n_rows, n_off = _mix(ap.reshape(tp, -1), a_s.reshape(ts, -1), yp.reshape(tp, -1),
                                            ysm.reshape(ts, -1), xp, xs, *mix_w)

    per_tile = lambda v: v.reshape(-1, N_EXPERTS).astype(i32)
    tiles = (SLOTS * (tp + ts)) // ROW_TILE + N_EXPERTS
    expert_tables, combine_tables = _moe_tables(per_tile(n_rows), per_tile(n_off), tiles)
    expert_out = _experts(expert_tables, xl, w_exp_gate[0], w_exp_up[0], w_exp_down[0], tiles)
    y_p, y_s = _combine(combine_tables, expert_out, x1, wts, lpos, row2(g_final), tp)

    kvshape = lambda a, b: a.reshape(1, b, -1, N_KV_HEADS, HEAD_DIM)
    block_state = lambda h: h.reshape(bp, SSM_GROUPS, SSM_STATE)[None]
    wcp = min(WINDOW, lp)
    return (y_p.reshape(bp, lp, D_MODEL), y_s.reshape(bs, ls, D_MODEL),
            kvshape(kp.reshape(bp, lp, -1)[:, lp - wcp:], bp), kvshape(vp.reshape(bp, lp, -1)[:, lp - wcp:], bp),
            block_state(hrp), block_state(hip),
            kvshape(k_roll, bs), kvshape(v_roll, bs),
            from_blocks(hrs)[None], from_blocks(his)[None])
```

```python
import functools
import math

import jax
import jax.numpy as jnp
from jax import lax
from jax.experimental import pallas as pl
from jax.experimental.pallas import tpu as pltpu

f32, bf16, i32 = jnp.float32, jnp.bfloat16, jnp.int32

D_MODEL = 1024
CHUNK = 64
N_BACK = 2
WINDOW = 128
ATTN_WIDTH = 512
HEAD_DIM = 64
N_KV_HEADS = 2
Q_PER_KV = 4
KV_WIDTH = 128
SSM_WIDTH = 512
SSM_GROUP = 16
SSM_GROUPS = 32
SSM_STATE = 64
PROJ_WIDTH = 1280
N_EXPERT_GROUPS = 4
EXPERTS_PER_GROUP = 8
N_EXPERTS = 32
D_EXPERT = 512
EPS = 1e-6
NEG = -1e30

LANES = 128
SSM_STEPS = 16
SSM_BLOCKS = SSM_WIDTH // LANES
GROUPS_PER_BLOCK = LANES // SSM_GROUP
STATE_COLS = GROUPS_PER_BLOCK * SSM_STATE
ROW_TILE = 256
SLOTS = 2
PIECES = D_MODEL // LANES
RUN_PIECES = tuple(1 << b for b in reversed(range(int(math.log2(ROW_TILE)) + 1)))
SHORT_PIECES = tuple(p for p in RUN_PIECES if p <= 32)
LONG_PIECES = tuple(p for p in RUN_PIECES if p > 32)
MIX_TILES = 2
UNROLLED_RUNS = 20
EXPERT_RING = 3
COMBINE_RING = 3
VMEM_LIMIT = 56 * 1024 * 1024


def _cparams(n_axes=1, limit=VMEM_LIMIT):
    return pltpu.CompilerParams(dimension_semantics=("arbitrary",) * n_axes, vmem_limit_bytes=limit)


def _rms(x, g):
    return x * lax.rsqrt(jnp.mean(x * x, axis=-1, keepdims=True) + EPS) * g


def _bdot(a, b):
    return jnp.dot(a.astype(bf16), b.astype(bf16), preferred_element_type=f32)


def _proj_body(x_ref, g_ref, w_ref, q_ref, k_ref, v_ref, u_ref):
    h = _rms(x_ref[...], g_ref[...])
    z = _bdot(h, w_ref[...])
    q_ref[...] = z[:, :ATTN_WIDTH] * (HEAD_DIM ** -0.5)
    k_ref[...] = z[:, ATTN_WIDTH:ATTN_WIDTH + KV_WIDTH]
    v_ref[...] = z[:, ATTN_WIDTH + KV_WIDTH:ATTN_WIDTH + 2 * KV_WIDTH]
    u_ref[...] = z[:, ATTN_WIDTH + 2 * KV_WIDTH:]


def _proj(x2d, g, w_bf, tm):
    t = x2d.shape[0]
    row = lambda n: pl.BlockSpec((tm, n), lambda i: (i, 0))
    full = lambda a: pl.BlockSpec(a.shape, lambda i: (0,) * a.ndim)
    return pl.pallas_call(
        _proj_body,
        grid=(t // tm,),
        in_specs=[row(D_MODEL), full(g), full(w_bf)],
        out_specs=[row(ATTN_WIDTH), row(KV_WIDTH), row(KV_WIDTH), row(SSM_WIDTH)],
        out_shape=[jax.ShapeDtypeStruct((t, n), f32) for n in (ATTN_WIDTH, KV_WIDTH, KV_WIDTH, SSM_WIDTH)],
        compiler_params=_cparams(),
        name="proj",
    )(x2d, g, w_bf)


def _sink_column(sink_ref, kv, rows_per_head):
    r = lax.broadcasted_iota(i32, (Q_PER_KV * rows_per_head, 1), 0)
    col = jnp.full((Q_PER_KV * rows_per_head, 1), sink_ref[kv * Q_PER_KV], f32)
    for j in range(1, Q_PER_KV):
        col = jnp.where(r >= j * rows_per_head, sink_ref[kv * Q_PER_KV + j], col)
    return col


def _attend(qs, kc, vc, sink_col, valid):
    s = lax.dot_general(qs.astype(bf16), kc.astype(bf16), (((1,), (1,)), ((), ())), preferred_element_type=f32)
    if valid is not None:
        s = jnp.where(valid, s, NEG)
    m = jnp.maximum(jnp.max(s, axis=-1, keepdims=True), sink_col)
    p = jnp.exp(s - m)
    denom = jnp.sum(p, axis=-1, keepdims=True) + jnp.exp(sink_col - m)
    return _bdot(p, vc) / denom


def _heads_attend(q, k, v, sink_ref, valid):
    rows = q.shape[0]
    pieces = []
    for kv in range(N_KV_HEADS):
        qs = jnp.concatenate(
            [q[:, (kv * Q_PER_KV + j) * HEAD_DIM:(kv * Q_PER_KV + j + 1) * HEAD_DIM] for j in range(Q_PER_KV)], axis=0)
        o = _attend(qs, k[:, kv * HEAD_DIM:(kv + 1) * HEAD_DIM], v[:, kv * HEAD_DIM:(kv + 1) * HEAD_DIM],
                    _sink_column(sink_ref, kv, rows), valid)
        pieces += [o[j * rows:(j + 1) * rows] for j in range(Q_PER_KV)]
    return jnp.concatenate(pieces, axis=1)


ATTN_TILE = 256
CHUNKS_PER_TILE = ATTN_TILE // CHUNK
KEY_SPAN = (N_BACK + 1) * CHUNK


def _attn_prompt_body(sink_ref, q_ref, kp_ref, kc_ref, vp_ref, vc_ref, g_ref, o_ref):
    i = pl.program_id(1)
    kwin = jnp.concatenate([kp_ref[0], kc_ref[0]], axis=0)
    vwin = jnp.concatenate([vp_ref[0], vc_ref[0]], axis=0)
    key_chunk = lax.broadcasted_iota(i32, (1, KEY_SPAN), 1) // CHUNK
    for c in range(CHUNKS_PER_TILE):
        valid = (i * CHUNKS_PER_TILE + c - N_BACK + key_chunk) >= 0
        o = _heads_attend(q_ref[0, c * CHUNK:(c + 1) * CHUNK, :], kwin[c * CHUNK:c * CHUNK + KEY_SPAN],
                          vwin[c * CHUNK:c * CHUNK + KEY_SPAN], sink_ref, valid)
        o_ref[0, c * CHUNK:(c + 1) * CHUNK, :] = _rms(o, g_ref[...]).astype(bf16)


def _attn_prompt(q, k, v, sinks, g):
    b, l, _ = q.shape
    back = N_BACK * CHUNK
    per = ATTN_TILE // back
    prev = pl.BlockSpec((1, back, KV_WIDTH), lambda bi, i: (bi, jnp.maximum(i * per - 1, 0), 0))
    cur = pl.BlockSpec((1, ATTN_TILE, KV_WIDTH), lambda bi, i: (bi, i, 0))
    return pl.pallas_call(
        _attn_prompt_body,
        grid=(b, l // ATTN_TILE),
        in_specs=[pl.BlockSpec(memory_space=pltpu.SMEM),
                  pl.BlockSpec((1, ATTN_TILE, ATTN_WIDTH), lambda bi, i: (bi, i, 0)),
                  prev, cur, prev, cur,
                  pl.BlockSpec((1, ATTN_WIDTH), lambda bi, i: (0, 0))],
        out_specs=pl.BlockSpec((1, ATTN_TILE, ATTN_WIDTH), lambda bi, i: (bi, i, 0)),
        out_shape=jax.ShapeDtypeStruct((b, l, ATTN_WIDTH), bf16),
        compiler_params=_cparams(2),
        name="attn_prompt",
    )(sinks, q, k, k, v, v, g)


def _attn_sample_body(sink_ref, q_ref, kn_ref, vn_ref, ck_ref, cv_ref, g_ref, o_ref, nk_ref, nv_ref):
    kall = jnp.concatenate([ck_ref[0], kn_ref[0]], axis=0)
    vall = jnp.concatenate([cv_ref[0], vn_ref[0]], axis=0)
    o = _heads_attend(q_ref[0], kall, vall, sink_ref, None)
    o_ref[0] = _rms(o, g_ref[...]).astype(bf16)
    n_new = kn_ref.shape[1]
    nk_ref[0] = kall[n_new:]
    nv_ref[0] = vall[n_new:]


def _attn_sample(q, k_new, v_new, cache_k, cache_v, sinks, g):
    b, l, _ = q.shape
    wc = cache_k.shape[1]
    blk = lambda r, n: pl.BlockSpec((1, r, n), lambda bi: (bi, 0, 0))
    return pl.pallas_call(
        _attn_sample_body,
        grid=(b,),
        in_specs=[pl.BlockSpec(memory_space=pltpu.SMEM), blk(l, ATTN_WIDTH), blk(l, KV_WIDTH), blk(l, KV_WIDTH),
                  blk(wc, KV_WIDTH), blk(wc, KV_WIDTH), pl.BlockSpec((1, ATTN_WIDTH), lambda bi: (0, 0))],
        out_specs=[blk(l, ATTN_WIDTH), blk(wc, KV_WIDTH), blk(wc, KV_WIDTH)],
        out_shape=[jax.ShapeDtypeStruct((b, l, ATTN_WIDTH), bf16),
                   jax.ShapeDtypeStruct((b, wc, KV_WIDTH), f32), jax.ShapeDtypeStruct((b, wc, KV_WIDTH), f32)],
        compiler_params=_cparams(),
        name="attn_sample",
    )(sinks, q, k_new, v_new, cache_k, cache_v, g)


def _ssm_tables(a_re, a_im, log_dt, b_re, b_im):
    dt = jnp.exp(log_dt)[:, None]
    lam_r, lam_i = a_re * dt, a_im * dt

    def power(n):
        mag = jnp.exp(n * lam_r)
        return jnp.stack([mag * jnp.cos(n * lam_i), mag * jnp.sin(n * lam_i)])

    ar, ai = power(1.0)
    den = a_re * a_re + a_im * a_im
    nr, ni = ar - 1.0, ai
    fr = ((nr * a_re + ni * a_im) / den)[..., None]
    fi = ((ni * a_re - nr * a_im) / den)[..., None]
    bbar = jnp.stack([fr * b_re - fi * b_im, fr * b_im + fi * b_re])
    bbar = bbar.transpose(0, 1, 3, 2).reshape(2, SSM_BLOCKS, LANES, SSM_STATE)
    by_block = lambda t: t.reshape(t.shape[:-2] + (SSM_BLOCKS, STATE_COLS))
    steps = by_block(power(jnp.arange(SSM_STEPS + 1, dtype=f32)[:, None, None]))
    levels = by_block(power(SSM_STEPS * 2.0 ** jnp.arange(8, dtype=f32)[:, None, None]))
    return bbar, steps.transpose(2, 0, 1, 3), levels.transpose(2, 1, 0, 3)


def _ssm_chunk_rows(u_ref, nk):
    xs = [u_ref[0, pl.ds(s, nk, stride=SSM_STEPS), :] for s in range(SSM_STEPS)]
    pairs = [jnp.concatenate([xs[2 * p], xs[2 * p + 1]], axis=1).astype(bf16) for p in range(SSM_STEPS // 2)]
    return xs, pairs


def _ssm_intra(pairs, toep_ref, nk):
    nd = len(pairs)
    y = [None] * nd
    for d in range(nd):
        lhs = jnp.concatenate(pairs[:nd - d], axis=0) if nd - d > 1 else pairs[0]
        r = jnp.dot(lhs, toep_ref[d], preferred_element_type=f32)
        for p in range(nd - d):
            blk = r[p * nk:(p + 1) * nk]
            y[p + d] = blk if y[p + d] is None else y[p + d] + blk
    return y


def _shift_rows(x, sh):
    rows = lax.broadcasted_iota(i32, (x.shape[0], 1), 0)
    return jnp.where(rows >= sh, pltpu.roll(x, sh, axis=0), 0.0)


def _group_index(shape, axis, sub):
    idx = lax.broadcasted_iota(i32, shape, axis)
    return jnp.right_shift(idx, int(math.log2(sub))) & (GROUPS_PER_BLOCK - 1)


def _build_ssm_weights(bb_ref, cr_ref, ci_ref, pw_ref, toep_s, wout_s, win_s, w_low):
    hp = lax.Precision.HIGHEST
    k = lax.broadcasted_iota(i32, (SSM_STATE, STATE_COLS), 0)
    n = lax.broadcasted_iota(i32, (SSM_STATE, STATE_COLS), 1)
    spread = ((n & (SSM_STATE - 1)) == k).astype(f32)
    own = _group_index((LANES, STATE_COLS), 0, SSM_GROUP) == _group_index((LANES, STATE_COLS), 1, SSM_STATE)

    def block_diag(compact):
        return jnp.where(own, jnp.dot(compact, spread, precision=hp, preferred_element_type=f32), 0.0)

    bb_r, bb_i = block_diag(bb_ref[0, 0]), block_diag(bb_ref[1, 0])
    c_r, c_i = block_diag(cr_ref[0]), block_diag(ci_ref[0])
    c_stack = jnp.concatenate([c_r.T, -c_i.T], axis=0)
    power = lambda e: (pw_ref[0, 0, e:e + 1, :], pw_ref[0, 1, e:e + 1, :])
    for s in range(SSM_STEPS):
        rows = slice(s * LANES, (s + 1) * LANES)
        pr, pi = power(SSM_STEPS - 1 - s)
        w = jnp.concatenate([bb_r * pr - bb_i * pi, bb_r * pi + bb_i * pr], axis=1)
        wout_s[rows, :] = w.astype(bf16)
        w_low[rows, :] = (w - w.astype(bf16).astype(f32)).astype(bf16)
        pr, pi = power(s + 1)
        g = jnp.concatenate([c_r * pr - c_i * pi, -(c_r * pi + c_i * pr)], axis=1)
        win_s[:, rows] = g.T.astype(bf16)
    c_high = c_stack.astype(bf16)
    c_low = (c_stack - c_high.astype(f32)).astype(bf16)
    lags = (jnp.dot(wout_s[...], c_high, preferred_element_type=f32)
            + (jnp.dot(wout_s[...], c_low, preferred_element_type=f32)
               + jnp.dot(w_low[...], c_high, preferred_element_type=f32)))
    lag_kernel = [lags[(SSM_STEPS - 1 - lag) * LANES:(SSM_STEPS - lag) * LANES] for lag in range(SSM_STEPS)]
    zero = jnp.zeros((LANES, LANES), f32)
    for d in range(SSM_STEPS // 2):
        top = jnp.concatenate([lag_kernel[2 * d], lag_kernel[2 * d + 1]], axis=1)
        bottom = jnp.concatenate([lag_kernel[2 * d - 1] if d > 0 else zero, lag_kernel[2 * d]], axis=1)
        toep_s[d] = jnp.concatenate([top, bottom], axis=0).astype(bf16)


def _ssm_chunks(u_ref, d_ref, y_ref, toep_s, wout_s, win_s, entry_state):
    nk = u_ref.shape[1] // SSM_STEPS
    xs, pairs = _ssm_chunk_rows(u_ref, nk)
    y = _ssm_intra(pairs, toep_s, nk)
    s = jnp.dot(jnp.concatenate(pairs, axis=1), wout_s[...], preferred_element_type=f32)
    hprev, hr, hi = entry_state(s[:, :STATE_COLS], s[:, STATE_COLS:])
    y2 = _bdot(hprev, win_s[...])
    for st in range(SSM_STEPS):
        piece = (y[st // 2][:, (st % 2) * LANES:(st % 2 + 1) * LANES] + y2[:, st * LANES:(st + 1) * LANES]
                 + d_ref[...] * xs[st])
        y_ref[0, pl.ds(st, nk, stride=SSM_STEPS), :] = piece
    return hr, hi


def _ssm_body(u_ref, us_ref, h0r_ref, h0i_ref, bb_ref, cr_ref, ci_ref, pw_ref, lev_ref, d_ref,
              y_ref, hr_ref, hi_ref, ys_ref, hrs_ref, his_ref, toep_s, wout_s, win_s, w_low):
    @pl.when(pl.program_id(1) == 0)
    def _():
        _build_ssm_weights(bb_ref, cr_ref, ci_ref, pw_ref, toep_s, wout_s, win_s, w_low)

        def one_chunk(sr, si):
            h0r, h0i = h0r_ref[0], h0i_ref[0]
            ar, ai = lev_ref[0, 0, 0:1, :], lev_ref[0, 0, 1:2, :]
            return (jnp.concatenate([h0r, h0i], axis=1), sr + ar * h0r - ai * h0i, si + ar * h0i + ai * h0r)
        hrs_ref[0], his_ref[0] = _ssm_chunks(us_ref, d_ref, ys_ref, toep_s, wout_s, win_s, one_chunk)

    def scan_chunks(sr, si):
        nk = sr.shape[0]
        level = 0
        while (1 << level) < nk:
            ar, ai = lev_ref[0, level, 0:1, :], lev_ref[0, level, 1:2, :]
            tr, ti = _shift_rows(sr, 1 << level), _shift_rows(si, 1 << level)
            sr, si = sr + ar * tr - ai * ti, si + ar * ti + ai * tr
            level += 1
        return (jnp.concatenate([_shift_rows(sr, 1), _shift_rows(si, 1)], axis=1), sr[nk - 1:nk], si[nk - 1:nk])
    hr_ref[0, 0], hi_ref[0, 0] = _ssm_chunks(u_ref, d_ref, y_ref, toep_s, wout_s, win_s, scan_chunks)


def _ssm(u, us, h0r, h0i, bbar, c_re, c_im, steps, lev, d):
    b, l, _ = u.shape
    rows = us.shape[1]
    nb = rows // SSM_STEPS
    wspec = lambda a: pl.BlockSpec((1,) + a.shape[1:], lambda j, bi: (j,) + (0,) * (a.ndim - 1))
    st = pl.BlockSpec((1, 1, 1, STATE_COLS), lambda j, bi: (bi, j, 0, 0))
    sst = pl.BlockSpec((1, nb, STATE_COLS), lambda j, bi: (j, 0, 0))
    seq = pl.BlockSpec((1, l, LANES), lambda j, bi: (bi, 0, j))
    sseq = pl.BlockSpec((1, rows, LANES), lambda j, bi: (0, 0, j))
    return pl.pallas_call(
        _ssm_body,
        grid=(SSM_BLOCKS, b),
        in_specs=[seq, sseq, sst, sst, pl.BlockSpec((2, 1, LANES, SSM_STATE), lambda j, bi: (0, j, 0, 0)),
                  wspec(c_re), wspec(c_im), wspec(steps), wspec(lev), pl.BlockSpec((1, LANES), lambda j, bi: (0, j))],
        out_specs=[seq, st, st, sseq, sst, sst],
        out_shape=[jax.ShapeDtypeStruct((b, l, SSM_WIDTH), f32),
                   jax.ShapeDtypeStruct((b, SSM_BLOCKS, 1, STATE_COLS), f32),
                   jax.ShapeDtypeStruct((b, SSM_BLOCKS, 1, STATE_COLS), f32),
                   jax.ShapeDtypeStruct((1, rows, SSM_WIDTH), f32),
                   jax.ShapeDtypeStruct((SSM_BLOCKS, nb, STATE_COLS), f32),
                   jax.ShapeDtypeStruct((SSM_BLOCKS, nb, STATE_COLS), f32)],
        scratch_shapes=[pltpu.VMEM((SSM_STEPS // 2, 2 * LANES, 2 * LANES), bf16),
                        pltpu.VMEM((SSM_STEPS * LANES, 2 * STATE_COLS), bf16),
                        pltpu.VMEM((2 * STATE_COLS, SSM_STEPS * LANES), bf16),
                        pltpu.VMEM((SSM_STEPS * LANES, 2 * STATE_COLS), bf16)],
        compiler_params=_cparams(2),
        name="ssm",
    )(u, us, h0r, h0i, bbar, c_re, c_im, steps, lev, d)


ROUTER_COLS = LANES


def _mix_body(ap_ref, as_ref, yp_ref, ys_ref, xp_ref, xs_ref, wglu_ref, bglu_ref, gs_ref, wout_ref, gf_ref, wr_ref,
              br_ref, tri_ref, upper_ref, x1_ref, xl_ref, wts_ref, lpos_ref, n_ref, loff_ref, *, prompt_steps):
    is_prompt = pl.program_id(0) < prompt_steps
    pick_rows = lambda p_ref, s_ref: jnp.where(is_prompt, p_ref[...], jnp.concatenate([s_ref[...]] * MIX_TILES, axis=0))
    y = pick_rows(yp_ref, ys_ref)
    y = 0.5 * y * (1.0 + jnp.tanh(math.sqrt(2.0 / math.pi) * (y + 0.044715 * (y * y * y))))
    y = y * jax.nn.sigmoid(_bdot(y, wglu_ref[...]) + bglu_ref[...])
    attn = jnp.where(is_prompt, ap_ref[...].astype(f32),
                     jnp.concatenate([as_ref[...].astype(f32)] * MIX_TILES, axis=0)).astype(bf16)
    cat = jnp.concatenate([attn, _rms(y, gs_ref[...]).astype(bf16)], axis=1)
    x1 = pick_rows(xp_ref, xs_ref) + jnp.dot(cat, wout_ref[...], preferred_element_type=f32)
    x1_ref[...] = x1
    hf = _rms(x1, gf_ref[...])

    logits = _bdot(hf, wr_ref[...]) + br_ref[...]
    le = logits[:, :N_EXPERTS]
    lg = logits[:, N_EXPERTS:N_EXPERTS + N_EXPERT_GROUPS]
    tm = le.shape[0]
    gmax = jnp.max(lg, axis=-1, keepdims=True)
    gi = lax.broadcasted_iota(i32, (tm, N_EXPERT_GROUPS), 1).astype(f32)
    gsel = jnp.min(jnp.where(lg == gmax, gi, float(N_EXPERT_GROUPS)), axis=-1, keepdims=True)
    pg = 1.0 / jnp.sum(jnp.exp(lg - gmax), axis=-1, keepdims=True)
    ei_int = lax.broadcasted_iota(i32, (tm, N_EXPERTS), 1)
    ei = ei_int.astype(f32)
    egroup = jnp.right_shift(ei_int, int(math.log2(EXPERTS_PER_GROUP))).astype(f32)
    lm = jnp.where(egroup == gsel, le, NEG)
    v1 = jnp.max(lm, axis=-1, keepdims=True)
    i1 = jnp.min(jnp.where(lm == v1, ei, float(N_EXPERTS)), axis=-1, keepdims=True)
    lm2 = jnp.where(ei == i1, NEG, lm)
    v2 = jnp.max(lm2, axis=-1, keepdims=True)
    i2 = jnp.min(jnp.where(lm2 == v2, ei, float(N_EXPERTS)), axis=-1, keepdims=True)
    ex = jnp.exp(v2 - v1)
    wts_ref[...] = jnp.concatenate([pg / (1.0 + ex), pg * ex / (1.0 + ex)], axis=1)

    oh1 = (ei == i1).astype(f32)
    oh2 = (ei == i2).astype(f32)
    hf_bf = hf.astype(bf16)
    sorted_row = lax.broadcasted_iota(i32, (ROW_TILE, SLOTS * ROW_TILE), 1).astype(f32)
    for h in range(MIX_TILES):
        rows = slice(h * ROW_TILE, (h + 1) * ROW_TILE)
        both = (oh1[rows] + oh2[rows]).astype(bf16)
        before = jnp.dot(tri_ref[...], both, preferred_element_type=f32)
        count = jnp.sum(oh1[rows] + oh2[rows], axis=0, keepdims=True)
        lower = jnp.sum(jnp.dot(both, upper_ref[...], preferred_element_type=f32), axis=0, keepdims=True)
        lp1 = jnp.sum(oh1[rows] * (before + lower), axis=-1, keepdims=True)
        lp2 = jnp.sum(oh2[rows] * (before + lower), axis=-1, keepdims=True)
        lpos_ref[rows, :] = jnp.concatenate([lp1, lp2], axis=1)
        n_ref[h] = count
        loff_ref[h] = lower
        pick = ((sorted_row == lp1) | (sorted_row == lp2)).astype(bf16)
        xl = lax.dot_general(pick, hf_bf[rows], (((0,), (0,)), ((), ())), preferred_element_type=f32)
        _store_row_major(xl_ref, h * SLOTS * ROW_TILE, xl)


def _mix(a_p, a_s, ys_p, ys_s, x_p, x_s, wglu, bglu, gs, wout, gf, wr, br, tri, upper):
    tm = MIX_TILES * ROW_TILE
    assert x_p.shape[0] % tm == 0 and x_s.shape[0] == ROW_TILE
    prompt_steps = x_p.shape[0] // tm
    t = x_p.shape[0] + x_s.shape[0]
    p_rows = lambda n: pl.BlockSpec((tm, n), lambda i: (jnp.minimum(i, prompt_steps - 1), 0))
    s_rows = lambda n: pl.BlockSpec((ROW_TILE, n), lambda i: (0, 0))
    row = lambda n: pl.BlockSpec((tm, n), lambda i: (i, 0))
    full = lambda arr: pl.BlockSpec(arr.shape, lambda i: (0,) * arr.ndim)
    per_tile = pl.BlockSpec((MIX_TILES, 1, N_EXPERTS), lambda i: (i, 0, 0))
    return pl.pallas_call(
        functools.partial(_mix_body, prompt_steps=prompt_steps),
        grid=(prompt_steps + 1,),
        in_specs=[p_rows(ATTN_WIDTH), s_rows(ATTN_WIDTH), p_rows(SSM_WIDTH), s_rows(SSM_WIDTH), p_rows(D_MODEL),
                  s_rows(D_MODEL), full(wglu), full(bglu), full(gs), full(wout), full(gf), full(wr), full(br),
                  full(tri), full(upper)],
        out_specs=[row(D_MODEL), pl.BlockSpec((SLOTS * tm * PIECES, LANES), lambda i: (i, 0)), row(2), row(2),
                   per_tile, per_tile],
        out_shape=[jax.ShapeDtypeStruct((t, D_MODEL), f32),
                   jax.ShapeDtypeStruct((SLOTS * t * PIECES, LANES), f32),
                   jax.ShapeDtypeStruct((t, 2), f32), jax.ShapeDtypeStruct((t, 2), f32),
                   jax.ShapeDtypeStruct((t // ROW_TILE, 1, N_EXPERTS), f32),
                   jax.ShapeDtypeStruct((t // ROW_TILE, 1, N_EXPERTS), f32)],
        compiler_params=_cparams(),
        name="mix",
    )(a_p, a_s, ys_p, ys_s, x_p, x_s, wglu, bglu, gs, wout, gf, wr, br, tri, upper)


def _store_row_major(ref, first_row, x):
    for c in range(PIECES):
        ref[pl.ds(first_row * PIECES + c, x.shape[0], stride=PIECES), :] = x[:, c * LANES:(c + 1) * LANES]


def _load_row_major(ref, n_rows):
    return jnp.concatenate([ref[pl.ds(c, n_rows, stride=PIECES), :] for c in range(PIECES)], axis=1)


def _copy_rows(src, s_row, dst, d_row, n_rows, sem):
    return pltpu.make_async_copy(src.at[pl.ds(pl.multiple_of(s_row * PIECES, PIECES), n_rows * PIECES), :],
                                 dst.at[pl.ds(pl.multiple_of(d_row * PIECES, PIECES), n_rows * PIECES), :], sem)


def _for_each_piece(n, fn, pieces=RUN_PIECES):
    off = 0 if pieces[0] == RUN_PIECES[0] else n & ~(2 * pieces[0] - 1)
    for piece in pieces:
        @pl.when((n & piece) != 0)
        def _(off=off, piece=piece):
            fn(off, piece)
        off = off + (n & piece)


def _experts_body(te_ref, tpos_ref, tvalid_ref, tlo_ref, thi_ref, wslot_ref, wnext_ref, n_ref, cum_ref, loff_ref,
                  xl_hbm, wg_hbm, wu_hbm, wd_hbm, o_ref, wg_s, wu_s, wd_s, wg_f, wu_f, wd_f, wsem, sem, *xbufs):
    i = pl.program_id(0)
    last = pl.num_programs(0) - 1
    token_tiles = n_ref.shape[0] // N_EXPERTS
    ring = len(xbufs)

    def start_run(t, tau, enabled, buf, buf_sem):
        e, lo = te_ref[t], tpos_ref[t]
        k = tau * N_EXPERTS + e
        s, n = cum_ref[k], n_ref[k]
        a = jnp.maximum(s, lo)
        length = jnp.where(enabled, jnp.maximum(jnp.minimum(s + n, lo + ROW_TILE) - a, 0), 0)
        local = loff_ref[k] + (a - s)
        _for_each_piece(length, lambda off, piece: _copy_rows(
            xl_hbm, tau * (SLOTS * ROW_TILE) + local + off, buf, a - lo + off, piece, buf_sem).start())

    def start_runs_loop(t, first, stop, buf, buf_sem):
        def run(tau, c):
            start_run(t, tau, True, buf, buf_sem)
            return c
        lax.fori_loop(first, stop, run, 0)

    def clear(buf):
        buf[...] = jnp.zeros_like(buf)

    @pl.when(i == 0)
    def _():
        for buf in xbufs:
            clear(buf)
        for t in range(ring - 1):
            @pl.when(tvalid_ref[t] > 0)
            def _(t=t):
                start_runs_loop(t, tlo_ref[t], thi_ref[t] + 1, xbufs[t], sem.at[t])

    valid = tvalid_ref[i]

    def weight_copies(e, slot):
        return [pltpu.make_async_copy(w_hbm.at[e], w_f.at[slot], wsem.at[slot])
                for w_hbm, w_f in ((wg_hbm, wg_f), (wu_hbm, wu_f), (wd_hbm, wd_f))]

    @pl.when((i == 0) & (valid > 0))
    def _():
        for cp in weight_copies(te_ref[0], wslot_ref[0]):
            cp.start()

    @pl.when((valid > 0) & ((i == 0) | (te_ref[i] != te_ref[jnp.maximum(i - 1, 0)])))
    def _():
        slot = wslot_ref[i]
        for cp in weight_copies(te_ref[i], slot):
            cp.wait()
        wg_s[...] = wg_f[slot].astype(bf16)
        wu_s[...] = wu_f[slot].astype(bf16)
        wd_s[...] = wd_f[slot].astype(bf16)

        @pl.when(wnext_ref[i] >= 0)
        def _():
            for cp in weight_copies(wnext_ref[i], 1 - slot):
                cp.start()

    def tile_step(cur):
        ahead = (cur + ring - 1) % ring
        buf, buf_sem, next_buf, next_sem = xbufs[cur], sem.at[cur], xbufs[ahead], sem.at[ahead]
        _for_each_piece(valid, lambda off, piece: _copy_rows(
            xl_hbm, 0, buf, 0, piece, buf_sem).wait())

        nxt = jnp.minimum(i + ring - 1, last)
        go = (i + ring - 1 <= last) & (tvalid_ref[nxt] > 0)
        first, final = tlo_ref[nxt], thi_ref[nxt]
        for j in range(UNROLLED_RUNS):
            start_run(nxt, jnp.minimum(first + j, token_tiles - 1), go & (first + j <= final), next_buf, next_sem)

        x = _load_row_major(buf, ROW_TILE).astype(bf16)
        clear(buf)
        hg = jnp.dot(x, wg_s[...], preferred_element_type=f32)
        hu = jnp.dot(x, wu_s[...], preferred_element_type=f32)
        y = jnp.dot((hg * jax.nn.sigmoid(hg) * hu).astype(bf16), wd_s[...], preferred_element_type=f32)
        _store_row_major(o_ref, 0, y)

        @pl.when(go & (final - first >= UNROLLED_RUNS))
        def _():
            start_runs_loop(nxt, first + UNROLLED_RUNS, final + 1, next_buf, next_sem)

    for cur in range(ring):
        pl.when((valid > 0) & (i % ring == cur))(functools.partial(tile_step, cur))

    @pl.when(valid == 0)
    def _():
        o_ref[...] = jnp.zeros_like(o_ref)


def _experts(tables, xl, wg, wu, wd, tiles):
    hbm = pl.BlockSpec(memory_space=pl.ANY)
    return pl.pallas_call(
        _experts_body,
        grid_spec=pltpu.PrefetchScalarGridSpec(
            num_scalar_prefetch=len(tables),
            grid=(tiles,),
            in_specs=[hbm, hbm, hbm, hbm],
            out_specs=pl.BlockSpec((ROW_TILE * PIECES, LANES), lambda i, *_: (i, 0)),
            scratch_shapes=[pltpu.VMEM(w.shape[1:], bf16) for w in (wg, wu, wd)]
            + [pltpu.VMEM((2,) + w.shape[1:], f32) for w in (wg, wu, wd)]
            + [pltpu.SemaphoreType.DMA((2,)), pltpu.SemaphoreType.DMA((EXPERT_RING,))]
            + [pltpu.VMEM((ROW_TILE * PIECES, LANES), f32)] * EXPERT_RING),
        out_shape=jax.ShapeDtypeStruct((tiles * ROW_TILE * PIECES, LANES), f32),
        compiler_params=_cparams(),
        name="moe_experts",
    )(*tables, xl, wg, wu, wd)


def _combine_body(n_ref, gpos_ref, loff_ref, ys_hbm, x_ref, w_ref, l_ref, g_ref, op_ref, os_ref, sem, *ybufs,
                  prompt_tiles):
    i = pl.program_id(0)
    last = pl.num_programs(0) - 1
    tile_rows = SLOTS * ROW_TILE
    ring = len(ybufs)

    def start_run(t, e, enabled, buf, buf_sem, pieces=RUN_PIECES):
        k = t * N_EXPERTS + e
        _for_each_piece(jnp.where(enabled, n_ref[k], 0), lambda off, piece: _copy_rows(
            ys_hbm, gpos_ref[k] + off, buf, loff_ref[k] + off, piece, buf_sem).start(), pieces)

    @pl.when(i == 0)
    def _():
        for t in range(ring - 1):
            def run(e, c, t=t):
                start_run(t, e, True, ybufs[t], sem.at[t])
                return c
            lax.fori_loop(0, N_EXPERTS, run, 0)

    def step(cur):
        ahead = (cur + ring - 1) % ring
        buf, buf_sem, next_buf, next_sem = ybufs[cur], sem.at[cur], ybufs[ahead], sem.at[ahead]
        _copy_rows(ys_hbm, 0, buf, 0, tile_rows, buf_sem).wait()
        nxt, go = jnp.minimum(i + ring - 1, last), i + ring - 1 <= last
        longest = 0
        for e in range(N_EXPERTS):
            start_run(nxt, e, go, next_buf, next_sem, SHORT_PIECES)
            longest = longest | n_ref[nxt * N_EXPERTS + e]
        yl = _load_row_major(buf, tile_rows).astype(bf16)
        sorted_row = lax.broadcasted_iota(i32, (ROW_TILE, tile_rows), 1).astype(f32)
        w, lp = w_ref[...], l_ref[...]
        y1 = jnp.dot((sorted_row == lp[:, 0:1]).astype(bf16), yl, preferred_element_type=f32)
        y2 = jnp.dot((sorted_row == lp[:, 1:2]).astype(bf16), yl, preferred_element_type=f32)
        out = _rms(x_ref[...] + (w[:, 0:1] * y1 + w[:, 1:2] * y2), g_ref[...])

        @pl.when(i < prompt_tiles)
        def _():
            op_ref[...] = out

        @pl.when(i >= prompt_tiles)
        def _():
            os_ref[...] = out

        @pl.when(go & (longest >= 2 * SHORT_PIECES[0]))
        def _():
            def long_pieces(e, c):
                start_run(nxt, e, True, next_buf, next_sem, LONG_PIECES)
                return c
            lax.fori_loop(0, N_EXPERTS, long_pieces, 0)

    for cur in range(ring):
        pl.when(i % ring == cur)(functools.partial(step, cur))


def _combine(tables, ys, x1, wts, lpos, g, prompt_rows):
    prompt_tiles = prompt_rows // ROW_TILE
    tiles = x1.shape[0] // ROW_TILE
    row = lambda n: pl.BlockSpec((ROW_TILE, n), lambda i, *_: (i, 0))
    p_rows = lambda n: pl.BlockSpec((ROW_TILE, n), lambda i, *_: (jnp.minimum(i, prompt_tiles - 1), 0))
    s_rows = lambda n: pl.BlockSpec((ROW_TILE, n), lambda i, *_: (jnp.maximum(i - prompt_tiles, 0), 0))
    return pl.pallas_call(
        functools.partial(_combine_body, prompt_tiles=prompt_tiles),
        grid_spec=pltpu.PrefetchScalarGridSpec(
            num_scalar_prefetch=len(tables),
            grid=(tiles,),
            in_specs=[pl.BlockSpec(memory_space=pl.ANY), row(D_MODEL), row(2), row(2),
                      pl.BlockSpec((1, D_MODEL), lambda i, *_: (0, 0))],
            out_specs=[p_rows(D_MODEL), s_rows(D_MODEL)],
            scratch_shapes=[pltpu.SemaphoreType.DMA((COMBINE_RING,))]
            + [pltpu.VMEM((SLOTS * ROW_TILE * PIECES, LANES), f32)] * COMBINE_RING),
        out_shape=[jax.ShapeDtypeStruct((prompt_rows, D_MODEL), f32),
                   jax.ShapeDtypeStruct((x1.shape[0] - prompt_rows, D_MODEL), f32)],
        compiler_params=_cparams(),
        name="moe_combine",
    )(*tables, ys, x1, wts, lpos, g)


def _moe_tables(n, loff, tiles):
    cum = jnp.cumsum(n, axis=0) - n
    counts = jnp.sum(n, axis=0)
    padded = (counts + ROW_TILE - 1) // ROW_TILE * ROW_TILE
    ends = jnp.cumsum(padded)
    starts = ends - padded
    first = jnp.arange(tiles, dtype=i32) * ROW_TILE
    expert = jnp.minimum(jnp.sum((first[:, None] >= ends[None, :]).astype(i32), axis=1), N_EXPERTS - 1)
    sel = expert[:, None] == jnp.arange(N_EXPERTS)[None, :]
    pick = lambda v: jnp.sum(jnp.where(sel, v[None, :], 0), axis=1)
    pos = first - pick(starts)
    valid = jnp.where(first < ends[-1], jnp.clip(pick(counts) - pos, 0, ROW_TILE), 0)
    cum_t, n_t = cum.T[expert], n.T[expert]
    touches = (cum_t + n_t > pos[:, None]) & (cum_t < (pos + ROW_TILE)[:, None]) & (n_t > 0)
    tau = jnp.arange(n.shape[0], dtype=i32)[None, :]
    lo = jnp.min(jnp.where(touches, tau, n.shape[0]), axis=1)
    hi = jnp.max(jnp.where(touches, tau, -1), axis=1)
    ids = jnp.arange(N_EXPERTS)
    busy = counts > 0
    ordinal = jnp.sum(busy[None, :] & (ids[None, :] < ids[:, None]), axis=1)
    following = jnp.min(jnp.where(busy[None, :] & (ids[None, :] > ids[:, None]), ids[None, :], N_EXPERTS), axis=1)
    following = jnp.where(following < N_EXPERTS, following, -1)
    as_i32 = lambda v: v.astype(i32)
    flat = lambda v: v.reshape(-1).astype(i32)
    expert_tables = (tuple(map(as_i32, (expert, pos, valid, lo, hi, pick(ordinal) % 2, pick(following))))
                     + (flat(n), flat(cum), flat(loff)))
    combine_tables = (flat(n), flat(starts[None, :] + cum), flat(loff))
    return expert_tables, combine_tables


def kernel(x_prompt, x_sample, cache_k, cache_v, state_ssm_re, state_ssm_im, g_norm_mix, w_in, attn_sinks, ssm_a_re,
           ssm_a_im, ssm_log_dt, ssm_b_re, ssm_b_im, ssm_c_re, ssm_c_im, ssm_d, w_glu, b_glu, g_attn_out, g_ssm_out,
           w_out, g_norm_ffn, w_router_group, b_router_group, w_router_expert, b_router_expert, w_exp_gate, w_exp_up,
           w_exp_down, g_final):
    bp, lp, _ = x_prompt.shape
    bs, ls, _ = x_sample.shape
    depth = w_in.shape[0]
    assert depth == 1 and ls == SSM_STEPS and lp % ATTN_TILE == 0 and (bs * ls) % ROW_TILE == 0
    tp, ts = bp * lp, bs * ls
    wc = cache_k.shape[2]
    row2 = lambda v: v.reshape(1, -1)

    xp = x_prompt.reshape(tp, D_MODEL)
    xs = x_sample.reshape(ts, D_MODEL)
    w_in_bf = w_in[0].astype(bf16)
    qp, kp, vp, up = _proj(xp, row2(g_norm_mix[0]), w_in_bf, 512)
    qs, kq, vq, us = _proj(xs, row2(g_norm_mix[0]), w_in_bf, ts)

    sinks = attn_sinks[0]
    g_att = row2(g_attn_out[0])
    ap = _attn_prompt(qp.reshape(bp, lp, -1), kp.reshape(bp, lp, -1), vp.reshape(bp, lp, -1), sinks, g_att)
    a_s, k_roll, v_roll = _attn_sample(qs.reshape(bs, ls, -1), kq.reshape(bs, ls, -1), vq.reshape(bs, ls, -1),
                                       cache_k[0].reshape(bs, wc, KV_WIDTH), cache_v[0].reshape(bs, wc, KV_WIDTH),
                                       sinks, g_att)

    bbar, steps, lev = _ssm_tables(ssm_a_re[0], ssm_a_im[0], ssm_log_dt[0], ssm_b_re[0], ssm_b_im[0])
    c_blocks = lambda c: c.reshape(SSM_BLOCKS, LANES, SSM_STATE)
    d_row = row2(ssm_d[0])
    to_blocks = lambda h: h.reshape(bs, SSM_BLOCKS, STATE_COLS).transpose(1, 0, 2)
    from_blocks = lambda h: h.transpose(1, 0, 2).reshape(bs, SSM_GROUPS, SSM_STATE)
    yp, hrp, hip, ysm, hrs, his = _ssm(up.reshape(bp, lp, -1), us.reshape(1, ts, -1), to_blocks(state_ssm_re[0]),
                                       to_blocks(state_ssm_im[0]), bbar, c_blocks(ssm_c_re[0]),
                                       c_blocks(ssm_c_im[0]), steps, lev, d_row)

    wr = jnp.zeros((D_MODEL, ROUTER_COLS), f32)
    wr = wr.at[:, :N_EXPERTS].set(w_router_expert[0]).at[:, N_EXPERTS:N_EXPERTS + N_EXPERT_GROUPS].set(w_router_group[0])
    br = jnp.zeros((1, ROUTER_COLS), f32)
    br = br.at[0, :N_EXPERTS].set(b_router_expert[0]).at[0, N_EXPERTS:N_EXPERTS + N_EXPERT_GROUPS].set(b_router_group[0])
    tri = jnp.tril(jnp.ones((ROW_TILE, ROW_TILE), bf16), -1)
    upper = jnp.triu(jnp.ones((N_EXPERTS, N_EXPERTS), bf16), 1)
    mix_w = (w_glu[0].astype(bf16), row2(b_glu[0]), row2(g_ssm_out[0]), w_out[0].astype(bf16), row2(g_norm_ffn[0]),
             wr.astype(bf16), br, tri, upper)
    x1, xl, wts, lpos, n_rows, n_off = _mix(ap.reshape(tp, -1), a_s.reshape(ts, -1), yp.reshape(tp, -1),
                                            ysm.reshape(ts, -1), xp, xs, *mix_w)

    per_tile = lambda v: v.reshape(-1, N_EXPERTS).astype(i32)
    tiles = (SLOTS * (tp + ts)) // ROW_TILE + N_EXPERTS
    expert_tables, combine_tables = _moe_tables(per_tile(n_rows), per_tile(n_off), tiles)
    expert_out = _experts(expert_tables, xl, w_exp_gate[0], w_exp_up[0], w_exp_down[0], tiles)
    y_p, y_s = _combine(combine_tables, expert_out, x1, wts, lpos, row2(g_final), tp)

    kvshape = lambda a, b: a.reshape(1, b, -1, N_KV_HEADS, HEAD_DIM)
    block_state = lambda h: h.reshape(bp, SSM_GROUPS, SSM_STATE)[None]
    wcp = min(WINDOW, lp)
    return (y_p.reshape(bp, lp, D_MODEL), y_s.reshape(bs, ls, D_MODEL),
            kvshape(kp.reshape(bp, lp, -1)[:, lp - wcp:], bp), kvshape(vp.reshape(bp, lp, -1)[:, lp - wcp:], bp),
            block_state(hrp), block_state(hip),
            kvshape(k_roll, bs), kvshape(v_roll, bs),
            from_blocks(hrs)[None], from_blocks(his)[None])
```

```python
import functools
import math

import jax
import jax.numpy as jnp
from jax import lax
from jax.experimental import pallas as pl
from jax.experimental.pallas import tpu as pltpu

f32, bf16, i32 = jnp.float32, jnp.bfloat16, jnp.int32

D_MODEL = 1024
CHUNK = 64
N_BACK = 2
WINDOW = 128
ATTN_WIDTH = 512
HEAD_DIM = 64
N_KV_HEADS = 2
Q_PER_KV = 4
KV_WIDTH = 128
SSM_WIDTH = 512
SSM_GROUP = 16
SSM_GROUPS = 32
SSM_STATE = 64
PROJ_WIDTH = 1280
N_EXPERT_GROUPS = 4
EXPERTS_PER_GROUP = 8
N_EXPERTS = 32
D_EXPERT = 512
EPS = 1e-6
NEG = -1e30
LOG2E = math.log2(math.e)

LANES = 128
SSM_STEPS = 16
SSM_BLOCKS = SSM_WIDTH // LANES
GROUPS_PER_BLOCK = LANES // SSM_GROUP
STATE_COLS = GROUPS_PER_BLOCK * SSM_STATE
ROW_TILE = 256
SLOTS = 2
PIECES = D_MODEL // LANES
RUN_PIECES = tuple(1 << b for b in reversed(range(int(math.log2(ROW_TILE)) + 1)))
SHORT_PIECES = tuple(p for p in RUN_PIECES if p <= 32)
LONG_PIECES = tuple(p for p in RUN_PIECES if p > 32)
MIX_TILES = 2
UNROLLED_RUNS = 20
EXPERT_RING = 3
COMBINE_RING = 3
VMEM_LIMIT = 56 * 1024 * 1024


def _cparams(n_axes=1, limit=VMEM_LIMIT):
    return pltpu.CompilerParams(dimension_semantics=("arbitrary",) * n_axes, vmem_limit_bytes=limit)


def _rms(x, g):
    return x * lax.rsqrt(jnp.mean(x * x, axis=-1, keepdims=True) + EPS) * g


def _bdot(a, b):
    return jnp.dot(a.astype(bf16), b.astype(bf16), preferred_element_type=f32)


def _proj_body(x_ref, g_ref, w_ref, q_ref, k_ref, v_ref, u_ref):
    h = _rms(x_ref[...], g_ref[...])
    z = _bdot(h, w_ref[...])
    q_ref[...] = z[:, :ATTN_WIDTH] * (HEAD_DIM ** -0.5 * LOG2E)
    k_ref[...] = z[:, ATTN_WIDTH:ATTN_WIDTH + KV_WIDTH]
    v_ref[...] = z[:, ATTN_WIDTH + KV_WIDTH:ATTN_WIDTH + 2 * KV_WIDTH]
    u_ref[...] = z[:, ATTN_WIDTH + 2 * KV_WIDTH:]


def _proj(x2d, g, w_bf, tm):
    t = x2d.shape[0]
    row = lambda n: pl.BlockSpec((tm, n), lambda i: (i, 0))
    full = lambda a: pl.BlockSpec(a.shape, lambda i: (0,) * a.ndim)
    return pl.pallas_call(
        _proj_body,
        grid=(t // tm,),
        in_specs=[row(D_MODEL), full(g), full(w_bf)],
        out_specs=[row(ATTN_WIDTH), row(KV_WIDTH), row(KV_WIDTH), row(SSM_WIDTH)],
        out_shape=[jax.ShapeDtypeStruct((t, n), f32) for n in (ATTN_WIDTH, KV_WIDTH, KV_WIDTH, SSM_WIDTH)],
        compiler_params=_cparams(),
        name="proj",
    )(x2d, g, w_bf)


def _sink_column(sink_ref, kv, rows_per_head):
    r = lax.broadcasted_iota(i32, (Q_PER_KV * rows_per_head, 1), 0)
    col = jnp.full((Q_PER_KV * rows_per_head, 1), sink_ref[kv * Q_PER_KV], f32)
    for j in range(1, Q_PER_KV):
        col = jnp.where(r >= j * rows_per_head, sink_ref[kv * Q_PER_KV + j], col)
    return col * LOG2E


def _attend(qs, kc, vc, sink_col, valid):
    s = lax.dot_general(qs.astype(bf16), kc.astype(bf16), (((1,), (1,)), ((), ())), preferred_element_type=f32)
    if valid is not None:
        s = jnp.where(valid, s, NEG)
    m = jnp.maximum(jnp.max(s, axis=-1, keepdims=True), sink_col)
    p = jnp.exp2(s - m)
    denom = jnp.sum(p, axis=-1, keepdims=True) + jnp.exp2(sink_col - m)
    return _bdot(p, vc) / denom


def _heads_attend(q, k, v, sink_ref, valid):
    rows = q.shape[0]
    pieces = []
    for kv in range(N_KV_HEADS):
        qs = jnp.concatenate(
            [q[:, (kv * Q_PER_KV + j) * HEAD_DIM:(kv * Q_PER_KV + j + 1) * HEAD_DIM] for j in range(Q_PER_KV)], axis=0)
        o = _attend(qs, k[:, kv * HEAD_DIM:(kv + 1) * HEAD_DIM], v[:, kv * HEAD_DIM:(kv + 1) * HEAD_DIM],
                    _sink_column(sink_ref, kv, rows), valid)
        pieces += [o[j * rows:(j + 1) * rows] for j in range(Q_PER_KV)]
    return jnp.concatenate(pieces, axis=1)


ATTN_TILE = 256
CHUNKS_PER_TILE = ATTN_TILE // CHUNK
KEY_SPAN = (N_BACK + 1) * CHUNK


def _attn_prompt_body(sink_ref, q_ref, kp_ref, kc_ref, vp_ref, vc_ref, g_ref, o_ref):
    i = pl.program_id(1)
    kwin = jnp.concatenate([kp_ref[0], kc_ref[0]], axis=0)
    vwin = jnp.concatenate([vp_ref[0], vc_ref[0]], axis=0)
    key_chunk = lax.broadcasted_iota(i32, (1, KEY_SPAN), 1) // CHUNK
    for c in range(CHUNKS_PER_TILE):
        valid = (i * CHUNKS_PER_TILE + c - N_BACK + key_chunk) >= 0 if c < N_BACK else None
        o = _heads_attend(q_ref[0, c * CHUNK:(c + 1) * CHUNK, :], kwin[c * CHUNK:c * CHUNK + KEY_SPAN],
                          vwin[c * CHUNK:c * CHUNK + KEY_SPAN], sink_ref, valid)
        o_ref[0, c * CHUNK:(c + 1) * CHUNK, :] = _rms(o, g_ref[...]).astype(bf16)


def _attn_prompt(q, k, v, sinks, g):
    b, l, _ = q.shape
    back = N_BACK * CHUNK
    per = ATTN_TILE // back
    prev = pl.BlockSpec((1, back, KV_WIDTH), lambda bi, i: (bi, jnp.maximum(i * per - 1, 0), 0))
    cur = pl.BlockSpec((1, ATTN_TILE, KV_WIDTH), lambda bi, i: (bi, i, 0))
    return pl.pallas_call(
        _attn_prompt_body,
        grid=(b, l // ATTN_TILE),
        in_specs=[pl.BlockSpec(memory_space=pltpu.SMEM),
                  pl.BlockSpec((1, ATTN_TILE, ATTN_WIDTH), lambda bi, i: (bi, i, 0)),
                  prev, cur, prev, cur,
                  pl.BlockSpec((1, ATTN_WIDTH), lambda bi, i: (0, 0))],
        out_specs=pl.BlockSpec((1, ATTN_TILE, ATTN_WIDTH), lambda bi, i: (bi, i, 0)),
        out_shape=jax.ShapeDtypeStruct((b, l, ATTN_WIDTH), bf16),
        compiler_params=_cparams(2),
        name="attn_prompt",
    )(sinks, q, k, k, v, v, g)


def _attn_sample_body(sink_ref, q_ref, kn_ref, vn_ref, ck_ref, cv_ref, g_ref, o_ref, nk_ref, nv_ref):
    kall = jnp.concatenate([ck_ref[0], kn_ref[0]], axis=0)
    vall = jnp.concatenate([cv_ref[0], vn_ref[0]], axis=0)
    o = _heads_attend(q_ref[0], kall, vall, sink_ref, None)
    o_ref[0] = _rms(o, g_ref[...]).astype(bf16)
    n_new = kn_ref.shape[1]
    nk_ref[0] = kall[n_new:]
    nv_ref[0] = vall[n_new:]


def _attn_sample(q, k_new, v_new, cache_k, cache_v, sinks, g):
    b, l, _ = q.shape
    wc = cache_k.shape[1]
    blk = lambda r, n: pl.BlockSpec((1, r, n), lambda bi: (bi, 0, 0))
    return pl.pallas_call(
        _attn_sample_body,
        grid=(b,),
        in_specs=[pl.BlockSpec(memory_space=pltpu.SMEM), blk(l, ATTN_WIDTH), blk(l, KV_WIDTH), blk(l, KV_WIDTH),
                  blk(wc, KV_WIDTH), blk(wc, KV_WIDTH), pl.BlockSpec((1, ATTN_WIDTH), lambda bi: (0, 0))],
        out_specs=[blk(l, ATTN_WIDTH), blk(wc, KV_WIDTH), blk(wc, KV_WIDTH)],
        out_shape=[jax.ShapeDtypeStruct((b, l, ATTN_WIDTH), bf16),
                   jax.ShapeDtypeStruct((b, wc, KV_WIDTH), f32), jax.ShapeDtypeStruct((b, wc, KV_WIDTH), f32)],
        compiler_params=_cparams(),
        name="attn_sample",
    )(sinks, q, k_new, v_new, cache_k, cache_v, g)


def _ssm_tables(a_re, a_im, log_dt, b_re, b_im):
    dt = jnp.exp(log_dt)[:, None]
    lam_r, lam_i = a_re * dt, a_im * dt

    def power(n):
        mag = jnp.exp(n * lam_r)
        return jnp.stack([mag * jnp.cos(n * lam_i), mag * jnp.sin(n * lam_i)])

    ar, ai = power(1.0)
    den = a_re * a_re + a_im * a_im
    nr, ni = ar - 1.0, ai
    fr = ((nr * a_re + ni * a_im) / den)[..., None]
    fi = ((ni * a_re - nr * a_im) / den)[..., None]
    bbar = jnp.stack([fr * b_re - fi * b_im, fr * b_im + fi * b_re])
    bbar = bbar.transpose(0, 1, 3, 2).reshape(2, SSM_BLOCKS, LANES, SSM_STATE)
    by_block = lambda t: t.reshape(t.shape[:-2] + (SSM_BLOCKS, STATE_COLS))
    steps = by_block(power(jnp.arange(SSM_STEPS + 1, dtype=f32)[:, None, None]))
    levels = by_block(power(SSM_STEPS * 2.0 ** jnp.arange(8, dtype=f32)[:, None, None]))
    return bbar, steps.transpose(2, 0, 1, 3), levels.transpose(2, 1, 0, 3)


def _ssm_chunk_rows(u_ref, nk):
    xs = [u_ref[0, pl.ds(s, nk, stride=SSM_STEPS), :] for s in range(SSM_STEPS)]
    pairs = [jnp.concatenate([xs[2 * p], xs[2 * p + 1]], axis=1).astype(bf16) for p in range(SSM_STEPS // 2)]
    return xs, pairs


def _ssm_intra(pairs, toep_ref, nk):
    nd = len(pairs)
    y = [None] * nd
    for d in range(nd):
        lhs = jnp.concatenate(pairs[:nd - d], axis=0) if nd - d > 1 else pairs[0]
        r = jnp.dot(lhs, toep_ref[d], preferred_element_type=f32)
        for p in range(nd - d):
            blk = r[p * nk:(p + 1) * nk]
            y[p + d] = blk if y[p + d] is None else y[p + d] + blk
    return y


def _shift_rows(x, sh):
    rows = lax.broadcasted_iota(i32, (x.shape[0], 1), 0)
    return jnp.where(rows >= sh, pltpu.roll(x, sh, axis=0), 0.0)


def _group_index(shape, axis, sub):
    idx = lax.broadcasted_iota(i32, shape, axis)
    return jnp.right_shift(idx, int(math.log2(sub))) & (GROUPS_PER_BLOCK - 1)


def _build_ssm_weights(bb_ref, cr_ref, ci_ref, pw_ref, toep_s, wout_s, win_s, w_low):
    hp = lax.Precision.HIGHEST
    k = lax.broadcasted_iota(i32, (SSM_STATE, STATE_COLS), 0)
    n = lax.broadcasted_iota(i32, (SSM_STATE, STATE_COLS), 1)
    spread = ((n & (SSM_STATE - 1)) == k).astype(f32)
    own = _group_index((LANES, STATE_COLS), 0, SSM_GROUP) == _group_index((LANES, STATE_COLS), 1, SSM_STATE)

    def block_diag(compact):
        return jnp.where(own, jnp.dot(compact, spread, precision=hp, preferred_element_type=f32), 0.0)

    bb_r, bb_i = block_diag(bb_ref[0, 0]), block_diag(bb_ref[1, 0])
    c_r, c_i = block_diag(cr_ref[0]), block_diag(ci_ref[0])
    c_stack = jnp.concatenate([c_r.T, -c_i.T], axis=0)
    power = lambda e: (pw_ref[0, 0, e:e + 1, :], pw_ref[0, 1, e:e + 1, :])
    for s in range(SSM_STEPS):
        rows = slice(s * LANES, (s + 1) * LANES)
        pr, pi = power(SSM_STEPS - 1 - s)
        w = jnp.concatenate([bb_r * pr - bb_i * pi, bb_r * pi + bb_i * pr], axis=1)
        wout_s[rows, :] = w.astype(bf16)
        w_low[rows, :] = (w - w.astype(bf16).astype(f32)).astype(bf16)
        pr, pi = power(s + 1)
        g = jnp.concatenate([c_r * pr - c_i * pi, -(c_r * pi + c_i * pr)], axis=1)
        win_s[:, rows] = g.T.astype(bf16)
    c_high = c_stack.astype(bf16)
    c_low = (c_stack - c_high.astype(f32)).astype(bf16)
    lags = (jnp.dot(wout_s[...], c_high, preferred_element_type=f32)
            + (jnp.dot(wout_s[...], c_low, preferred_element_type=f32)
               + jnp.dot(w_low[...], c_high, preferred_element_type=f32)))
    lag_kernel = [lags[(SSM_STEPS - 1 - lag) * LANES:(SSM_STEPS - lag) * LANES] for lag in range(SSM_STEPS)]
    zero = jnp.zeros((LANES, LANES), f32)
    for d in range(SSM_STEPS // 2):
        top = jnp.concatenate([lag_kernel[2 * d], lag_kernel[2 * d + 1]], axis=1)
        bottom = jnp.concatenate([lag_kernel[2 * d - 1] if d > 0 else zero, lag_kernel[2 * d]], axis=1)
        toep_s[d] = jnp.concatenate([top, bottom], axis=0).astype(bf16)


def _ssm_chunks(u_ref, d_ref, y_ref, toep_s, wout_s, win_s, entry_state):
    nk = u_ref.shape[1] // SSM_STEPS
    xs, pairs = _ssm_chunk_rows(u_ref, nk)
    y = _ssm_intra(pairs, toep_s, nk)
    s = jnp.dot(jnp.concatenate(pairs, axis=1), wout_s[...], preferred_element_type=f32)
    hprev, hr, hi = entry_state(s[:, :STATE_COLS], s[:, STATE_COLS:])
    y2 = _bdot(hprev, win_s[...])
    for st in range(SSM_STEPS):
        piece = (y[st // 2][:, (st % 2) * LANES:(st % 2 + 1) * LANES] + y2[:, st * LANES:(st + 1) * LANES]
                 + d_ref[...] * xs[st])
        y_ref[0, pl.ds(st, nk, stride=SSM_STEPS), :] = piece
    return hr, hi


def _ssm_body(u_ref, us_ref, h0r_ref, h0i_ref, bb_ref, cr_ref, ci_ref, pw_ref, lev_ref, d_ref,
              y_ref, hr_ref, hi_ref, ys_ref, hrs_ref, his_ref, toep_s, wout_s, win_s, w_low):
    @pl.when(pl.program_id(1) == 0)
    def _():
        _build_ssm_weights(bb_ref, cr_ref, ci_ref, pw_ref, toep_s, wout_s, win_s, w_low)

        def one_chunk(sr, si):
            h0r, h0i = h0r_ref[0], h0i_ref[0]
            ar, ai = lev_ref[0, 0, 0:1, :], lev_ref[0, 0, 1:2, :]
            return (jnp.concatenate([h0r, h0i], axis=1), sr + ar * h0r - ai * h0i, si + ar * h0i + ai * h0r)
        hrs_ref[0], his_ref[0] = _ssm_chunks(us_ref, d_ref, ys_ref, toep_s, wout_s, win_s, one_chunk)

    def scan_chunks(sr, si):
        nk = sr.shape[0]
        level = 0
        while (1 << level) < nk:
            ar, ai = lev_ref[0, level, 0:1, :], lev_ref[0, level, 1:2, :]
            tr, ti = _shift_rows(sr, 1 << level), _shift_rows(si, 1 << level)
            sr, si = sr + ar * tr - ai * ti, si + ar * ti + ai * tr
            level += 1
        return (jnp.concatenate([_shift_rows(sr, 1), _shift_rows(si, 1)], axis=1), sr[nk - 1:nk], si[nk - 1:nk])
    hr_ref[0, 0], hi_ref[0, 0] = _ssm_chunks(u_ref, d_ref, y_ref, toep_s, wout_s, win_s, scan_chunks)


def _ssm(u, us, h0r, h0i, bbar, c_re, c_im, steps, lev, d):
    b, l, _ = u.shape
    rows = us.shape[1]
    nb = rows // SSM_STEPS
    wspec = lambda a: pl.BlockSpec((1,) + a.shape[1:], lambda j, bi: (j,) + (0,) * (a.ndim - 1))
    st = pl.BlockSpec((1, 1, 1, STATE_COLS), lambda j, bi: (bi, j, 0, 0))
    sst = pl.BlockSpec((1, nb, STATE_COLS), lambda j, bi: (j, 0, 0))
    seq = pl.BlockSpec((1, l, LANES), lambda j, bi: (bi, 0, j))
    sseq = pl.BlockSpec((1, rows, LANES), lambda j, bi: (0, 0, j))
    return pl.pallas_call(
        _ssm_body,
        grid=(SSM_BLOCKS, b),
        in_specs=[seq, sseq, sst, sst, pl.BlockSpec((2, 1, LANES, SSM_STATE), lambda j, bi: (0, j, 0, 0)),
                  wspec(c_re), wspec(c_im), wspec(steps), wspec(lev), pl.BlockSpec((1, LANES), lambda j, bi: (0, j))],
        out_specs=[seq, st, st, sseq, sst, sst],
        out_shape=[jax.ShapeDtypeStruct((b, l, SSM_WIDTH), f32),
                   jax.ShapeDtypeStruct((b, SSM_BLOCKS, 1, STATE_COLS), f32),
                   jax.ShapeDtypeStruct((b, SSM_BLOCKS, 1, STATE_COLS), f32),
                   jax.ShapeDtypeStruct((1, rows, SSM_WIDTH), f32),
                   jax.ShapeDtypeStruct((SSM_BLOCKS, nb, STATE_COLS), f32),
                   jax.ShapeDtypeStruct((SSM_BLOCKS, nb, STATE_COLS), f32)],
        scratch_shapes=[pltpu.VMEM((SSM_STEPS // 2, 2 * LANES, 2 * LANES), bf16),
                        pltpu.VMEM((SSM_STEPS * LANES, 2 * STATE_COLS), bf16),
                        pltpu.VMEM((2 * STATE_COLS, SSM_STEPS * LANES), bf16),
                        pltpu.VMEM((SSM_STEPS * LANES, 2 * STATE_COLS), bf16)],
        compiler_params=_cparams(2),
        name="ssm",
    )(u, us, h0r, h0i, bbar, c_re, c_im, steps, lev, d)


ROUTER_COLS = LANES


def _mix_body(ap_ref, as_ref, yp_ref, ys_ref, xp_ref, xs_ref, wglu_ref, bglu_ref, gs_ref, wout_ref, gf_ref, wr_ref,
              br_ref, tri_ref, upper_ref, x1_ref, xl_ref, gate_ref, lpos_ref, n_ref, loff_ref, *, prompt_steps):
    is_prompt = pl.program_id(0) < prompt_steps
    pick_rows = lambda p_ref, s_ref: jnp.where(is_prompt, p_ref[...], jnp.concatenate([s_ref[...]] * MIX_TILES, axis=0))
    y = pick_rows(yp_ref, ys_ref)
    y = 0.5 * y * (1.0 + jnp.tanh(math.sqrt(2.0 / math.pi) * (y + 0.044715 * (y * y * y))))
    y = y * jax.nn.sigmoid(_bdot(y, wglu_ref[...]) + bglu_ref[...])
    attn = jnp.where(is_prompt, ap_ref[...].astype(f32),
                     jnp.concatenate([as_ref[...].astype(f32)] * MIX_TILES, axis=0)).astype(bf16)
    cat = jnp.concatenate([attn, _rms(y, gs_ref[...]).astype(bf16)], axis=1)
    x1 = pick_rows(xp_ref, xs_ref) + jnp.dot(cat, wout_ref[...], preferred_element_type=f32)
    x1_ref[...] = x1
    hf = _rms(x1, gf_ref[...])

    logits = _bdot(hf, wr_ref[...]) + br_ref[...]
    le = logits[:, :N_EXPERTS]
    lg = logits[:, N_EXPERTS:N_EXPERTS + N_EXPERT_GROUPS]
    tm = le.shape[0]
    gmax = jnp.max(lg, axis=-1, keepdims=True)
    gi = lax.broadcasted_iota(i32, (tm, N_EXPERT_GROUPS), 1).astype(f32)
    gsel = jnp.min(jnp.where(lg == gmax, gi, float(N_EXPERT_GROUPS)), axis=-1, keepdims=True)
    pg = 1.0 / jnp.sum(jnp.exp(lg - gmax), axis=-1, keepdims=True)
    ei_int = lax.broadcasted_iota(i32, (tm, N_EXPERTS), 1)
    ei = ei_int.astype(f32)
    egroup = jnp.right_shift(ei_int, int(math.log2(EXPERTS_PER_GROUP))).astype(f32)
    lm = jnp.where(egroup == gsel, le, NEG)
    v1 = jnp.max(lm, axis=-1, keepdims=True)
    i1 = jnp.min(jnp.where(lm == v1, ei, float(N_EXPERTS)), axis=-1, keepdims=True)
    lm2 = jnp.where(ei == i1, NEG, lm)
    v2 = jnp.max(lm2, axis=-1, keepdims=True)
    i2 = jnp.min(jnp.where(lm2 == v2, ei, float(N_EXPERTS)), axis=-1, keepdims=True)
    ex = jnp.exp(v2 - v1)
    w1, w2 = pg / (1.0 + ex), pg * ex / (1.0 + ex)

    oh1 = (ei == i1).astype(f32)
    oh2 = (ei == i2).astype(f32)
    hf_bf = hf.astype(bf16)
    sorted_row = lax.broadcasted_iota(i32, (ROW_TILE, SLOTS * ROW_TILE), 1).astype(f32)
    for h in range(MIX_TILES):
        rows = slice(h * ROW_TILE, (h + 1) * ROW_TILE)
        both = (oh1[rows] + oh2[rows]).astype(bf16)
        before = jnp.dot(tri_ref[...], both, preferred_element_type=f32)
        count = jnp.sum(oh1[rows] + oh2[rows], axis=0, keepdims=True)
        lower = jnp.sum(jnp.dot(both, upper_ref[...], preferred_element_type=f32), axis=0, keepdims=True)
        lp1 = jnp.sum(oh1[rows] * (before + lower), axis=-1, keepdims=True)
        lp2 = jnp.sum(oh2[rows] * (before + lower), axis=-1, keepdims=True)
        lpos_ref[rows, :] = jnp.concatenate([lp1, lp2], axis=1)
        n_ref[h] = count
        loff_ref[h] = lower
        first, second = sorted_row == lp1, sorted_row == lp2
        gate_ref[h] = jnp.sum(jnp.where(first, w1[rows], 0.0) + jnp.where(second, w2[rows], 0.0),
                              axis=0, keepdims=True)
        pick = (first | second).astype(bf16)
        xl = lax.dot_general(pick, hf_bf[rows], (((0,), (0,)), ((), ())), preferred_element_type=f32)
        _store_row_major(xl_ref, h * SLOTS * ROW_TILE, xl)


def _mix(a_p, a_s, ys_p, ys_s, x_p, x_s, wglu, bglu, gs, wout, gf, wr, br, tri, upper):
    tm = MIX_TILES * ROW_TILE
    assert x_p.shape[0] % tm == 0 and x_s.shape[0] == ROW_TILE
    prompt_steps = x_p.shape[0] // tm
    t = x_p.shape[0] + x_s.shape[0]
    p_rows = lambda n: pl.BlockSpec((tm, n), lambda i: (jnp.minimum(i, prompt_steps - 1), 0))
    s_rows = lambda n: pl.BlockSpec((ROW_TILE, n), lambda i: (0, 0))
    row = lambda n: pl.BlockSpec((tm, n), lambda i: (i, 0))
    full = lambda arr: pl.BlockSpec(arr.shape, lambda i: (0,) * arr.ndim)
    per_tile = pl.BlockSpec((MIX_TILES, 1, N_EXPERTS), lambda i: (i, 0, 0))
    return pl.pallas_call(
        functools.partial(_mix_body, prompt_steps=prompt_steps),
        grid=(prompt_steps + 1,),
        in_specs=[p_rows(ATTN_WIDTH), s_rows(ATTN_WIDTH), p_rows(SSM_WIDTH), s_rows(SSM_WIDTH), p_rows(D_MODEL),
                  s_rows(D_MODEL), full(wglu), full(bglu), full(gs), full(wout), full(gf), full(wr), full(br),
                  full(tri), full(upper)],
        out_specs=[row(D_MODEL), pl.BlockSpec((SLOTS * tm * PIECES, LANES), lambda i: (i, 0)),
                   pl.BlockSpec((MIX_TILES, 1, SLOTS * ROW_TILE), lambda i: (i, 0, 0)), row(2), per_tile, per_tile],
        out_shape=[jax.ShapeDtypeStruct((t, D_MODEL), f32),
                   jax.ShapeDtypeStruct((SLOTS * t * PIECES, LANES), f32),
                   jax.ShapeDtypeStruct((t // ROW_TILE, 1, SLOTS * ROW_TILE), f32), jax.ShapeDtypeStruct((t, 2), f32),
                   jax.ShapeDtypeStruct((t // ROW_TILE, 1, N_EXPERTS), f32),
                   jax.ShapeDtypeStruct((t // ROW_TILE, 1, N_EXPERTS), f32)],
        compiler_params=_cparams(),
        name="mix",
    )(a_p, a_s, ys_p, ys_s, x_p, x_s, wglu, bglu, gs, wout, gf, wr, br, tri, upper)


def _store_row_major(ref, first_row, x):
    for c in range(PIECES):
        ref[pl.ds(first_row * PIECES + c, x.shape[0], stride=PIECES), :] = x[:, c * LANES:(c + 1) * LANES]


def _load_row_major(ref, n_rows):
    return jnp.concatenate([ref[pl.ds(c, n_rows, stride=PIECES), :] for c in range(PIECES)], axis=1)


def _copy_rows(src, s_row, dst, d_row, n_rows, sem):
    return pltpu.make_async_copy(src.at[pl.ds(pl.multiple_of(s_row * PIECES, PIECES), n_rows * PIECES), :],
                                 dst.at[pl.ds(pl.multiple_of(d_row * PIECES, PIECES), n_rows * PIECES), :], sem)


def _for_each_piece(n, fn, pieces=RUN_PIECES):
    off = 0 if pieces[0] == RUN_PIECES[0] else n & ~(2 * pieces[0] - 1)
    for piece in pieces:
        @pl.when((n & piece) != 0)
        def _(off=off, piece=piece):
            fn(off, piece)
        off = off + (n & piece)


def _experts_body(te_ref, tpos_ref, tvalid_ref, tlo_ref, thi_ref, wslot_ref, wnext_ref, n_ref, cum_ref, loff_ref,
                  xl_hbm, wg_hbm, wu_hbm, wd_hbm, o_ref, wg_s, wu_s, wd_s, wg_f, wu_f, wd_f, wsem, sem, *xbufs):
    i = pl.program_id(0)
    last = pl.num_programs(0) - 1
    token_tiles = n_ref.shape[0] // N_EXPERTS
    ring = len(xbufs)

    def start_run(t, tau, enabled, buf, buf_sem):
        e, lo = te_ref[t], tpos_ref[t]
        k = tau * N_EXPERTS + e
        s, n = cum_ref[k], n_ref[k]
        a = jnp.maximum(s, lo)
        length = jnp.where(enabled, jnp.maximum(jnp.minimum(s + n, lo + ROW_TILE) - a, 0), 0)
        local = loff_ref[k] + (a - s)
        _for_each_piece(length, lambda off, piece: _copy_rows(
            xl_hbm, tau * (SLOTS * ROW_TILE) + local + off, buf, a - lo + off, piece, buf_sem).start())

    def start_runs_loop(t, first, stop, buf, buf_sem):
        def run(tau, c):
            start_run(t, tau, True, buf, buf_sem)
            return c
        lax.fori_loop(first, stop, run, 0)

    def clear(buf):
        buf[...] = jnp.zeros_like(buf)

    @pl.when(i == 0)
    def _():
        for buf in xbufs:
            clear(buf)
        for t in range(ring - 1):
            @pl.when(tvalid_ref[t] > 0)
            def _(t=t):
                start_runs_loop(t, tlo_ref[t], thi_ref[t] + 1, xbufs[t], sem.at[t])

    valid = tvalid_ref[i]

    def weight_copies(e, slot):
        return [pltpu.make_async_copy(w_hbm.at[e], w_f.at[slot], wsem.at[slot])
                for w_hbm, w_f in ((wg_hbm, wg_f), (wu_hbm, wu_f), (wd_hbm, wd_f))]

    @pl.when((i == 0) & (valid > 0))
    def _():
        for cp in weight_copies(te_ref[0], wslot_ref[0]):
            cp.start()

    @pl.when((valid > 0) & ((i == 0) | (te_ref[i] != te_ref[jnp.maximum(i - 1, 0)])))
    def _():
        slot = wslot_ref[i]
        for cp in weight_copies(te_ref[i], slot):
            cp.wait()
        wg_s[...] = wg_f[slot].astype(bf16)
        wu_s[...] = wu_f[slot].astype(bf16)
        wd_s[...] = wd_f[slot].astype(bf16)

        @pl.when(wnext_ref[i] >= 0)
        def _():
            for cp in weight_copies(wnext_ref[i], 1 - slot):
                cp.start()

    def tile_step(cur):
        ahead = (cur + ring - 1) % ring
        buf, buf_sem, next_buf, next_sem = xbufs[cur], sem.at[cur], xbufs[ahead], sem.at[ahead]
        _for_each_piece(valid, lambda off, piece: _copy_rows(
            xl_hbm, 0, buf, 0, piece, buf_sem).wait())

        nxt = jnp.minimum(i + ring - 1, last)
        go = (i + ring - 1 <= last) & (tvalid_ref[nxt] > 0)
        first, final = tlo_ref[nxt], thi_ref[nxt]
        for j in range(UNROLLED_RUNS):
            start_run(nxt, jnp.minimum(first + j, token_tiles - 1), go & (first + j <= final), next_buf, next_sem)

        x = _load_row_major(buf, ROW_TILE).astype(bf16)
        clear(buf)
        hg = jnp.dot(x, wg_s[...], preferred_element_type=f32)
        hu = jnp.dot(x, wu_s[...], preferred_element_type=f32)
        y = jnp.dot((hg * jax.nn.sigmoid(hg) * hu).astype(bf16), wd_s[...], preferred_element_type=f32)
        _store_row_major(o_ref, 0, y)

        @pl.when(go & (final - first >= UNROLLED_RUNS))
        def _():
            start_runs_loop(nxt, first + UNROLLED_RUNS, final + 1, next_buf, next_sem)

    for cur in range(ring):
        pl.when((valid > 0) & (i % ring == cur))(functools.partial(tile_step, cur))

    @pl.when(valid == 0)
    def _():
        o_ref[...] = jnp.zeros_like(o_ref)


def _experts(tables, xl, wg, wu, wd, tiles):
    hbm = pl.BlockSpec(memory_space=pl.ANY)
    return pl.pallas_call(
        _experts_body,
        grid_spec=pltpu.PrefetchScalarGridSpec(
            num_scalar_prefetch=len(tables),
            grid=(tiles,),
            in_specs=[hbm, hbm, hbm, hbm],
            out_specs=pl.BlockSpec((ROW_TILE * PIECES, LANES), lambda i, *_: (i, 0)),
            scratch_shapes=[pltpu.VMEM(w.shape[1:], bf16) for w in (wg, wu, wd)]
            + [pltpu.VMEM((2,) + w.shape[1:], f32) for w in (wg, wu, wd)]
            + [pltpu.SemaphoreType.DMA((2,)), pltpu.SemaphoreType.DMA((EXPERT_RING,))]
            + [pltpu.VMEM((ROW_TILE * PIECES, LANES), f32)] * EXPERT_RING),
        out_shape=jax.ShapeDtypeStruct((tiles * ROW_TILE * PIECES, LANES), f32),
        compiler_params=_cparams(),
        name="moe_experts",
    )(*tables, xl, wg, wu, wd)


def _combine_body(n_ref, gpos_ref, loff_ref, ys_hbm, x_ref, gate_ref, l_ref, g_ref, op_ref, os_ref, sem, *ybufs,
                  prompt_tiles):
    i = pl.program_id(0)
    last = pl.num_programs(0) - 1
    tile_rows = SLOTS * ROW_TILE
    ring = len(ybufs)

    def start_run(t, e, enabled, buf, buf_sem, pieces=RUN_PIECES):
        k = t * N_EXPERTS + e
        _for_each_piece(jnp.where(enabled, n_ref[k], 0), lambda off, piece: _copy_rows(
            ys_hbm, gpos_ref[k] + off, buf, loff_ref[k] + off, piece, buf_sem).start(), pieces)

    @pl.when(i == 0)
    def _():
        for t in range(ring - 1):
            def run(e, c, t=t):
                start_run(t, e, True, ybufs[t], sem.at[t])
                return c
            lax.fori_loop(0, N_EXPERTS, run, 0)

    def step(cur):
        ahead = (cur + ring - 1) % ring
        buf, buf_sem, next_buf, next_sem = ybufs[cur], sem.at[cur], ybufs[ahead], sem.at[ahead]
        _copy_rows(ys_hbm, 0, buf, 0, tile_rows, buf_sem).wait()
        nxt, go = jnp.minimum(i + ring - 1, last), i + ring - 1 <= last
        longest = 0
        for e in range(N_EXPERTS):
            start_run(nxt, e, go, next_buf, next_sem, SHORT_PIECES)
            longest = longest | n_ref[nxt * N_EXPERTS + e]
        gate_col = jnp.transpose(jnp.broadcast_to(gate_ref[0], (8, tile_rows)))[:, 0:1]
        yl = (_load_row_major(buf, tile_rows) * gate_col).astype(bf16)
        sorted_row = lax.broadcasted_iota(i32, (ROW_TILE, tile_rows), 1).astype(f32)
        lp = l_ref[...]
        both = ((sorted_row == lp[:, 0:1]) | (sorted_row == lp[:, 1:2])).astype(bf16)
        out = _rms(x_ref[...] + jnp.dot(both, yl, preferred_element_type=f32), g_ref[...])

        @pl.when(i < prompt_tiles)
        def _():
            op_ref[...] = out

        @pl.when(i >= prompt_tiles)
        def _():
            os_ref[...] = out

        @pl.when(go & (longest >= 2 * SHORT_PIECES[0]))
        def _():
            def long_pieces(e, c):
                start_run(nxt, e, True, next_buf, next_sem, LONG_PIECES)
                return c
            lax.fori_loop(0, N_EXPERTS, long_pieces, 0)

    for cur in range(ring):
        pl.when(i % ring == cur)(functools.partial(step, cur))


def _combine(tables, ys, x1, gates, lpos, g, prompt_rows):
    prompt_tiles = prompt_rows // ROW_TILE
    tiles = x1.shape[0] // ROW_TILE
    row = lambda n: pl.BlockSpec((ROW_TILE, n), lambda i, *_: (i, 0))
    p_rows = lambda n: pl.BlockSpec((ROW_TILE, n), lambda i, *_: (jnp.minimum(i, prompt_tiles - 1), 0))
    s_rows = lambda n: pl.BlockSpec((ROW_TILE, n), lambda i, *_: (jnp.maximum(i - prompt_tiles, 0), 0))
    return pl.pallas_call(
        functools.partial(_combine_body, prompt_tiles=prompt_tiles),
        grid_spec=pltpu.PrefetchScalarGridSpec(
            num_scalar_prefetch=len(tables),
            grid=(tiles,),
            in_specs=[pl.BlockSpec(memory_space=pl.ANY), row(D_MODEL),
                      pl.BlockSpec((1, 1, SLOTS * ROW_TILE), lambda i, *_: (i, 0, 0)), row(2),
                      pl.BlockSpec((1, D_MODEL), lambda i, *_: (0, 0))],
            out_specs=[p_rows(D_MODEL), s_rows(D_MODEL)],
            scratch_shapes=[pltpu.SemaphoreType.DMA((COMBINE_RING,))]
            + [pltpu.VMEM((SLOTS * ROW_TILE * PIECES, LANES), f32)] * COMBINE_RING),
        out_shape=[jax.ShapeDtypeStruct((prompt_rows, D_MODEL), f32),
                   jax.ShapeDtypeStruct((x1.shape[0] - prompt_rows, D_MODEL), f32)],
        compiler_params=_cparams(),
        name="moe_combine",
    )(*tables, ys, x1, gates, lpos, g)


def _moe_tables(n, loff, tiles):
    cum = jnp.cumsum(n, axis=0) - n
    counts = jnp.sum(n, axis=0)
    padded = (counts + ROW_TILE - 1) // ROW_TILE * ROW_TILE
    ends = jnp.cumsum(padded)
    starts = ends - padded
    first = jnp.arange(tiles, dtype=i32) * ROW_TILE
    expert = jnp.minimum(jnp.sum((first[:, None] >= ends[None, :]).astype(i32), axis=1), N_EXPERTS - 1)
    sel = expert[:, None] == jnp.arange(N_EXPERTS)[None, :]
    pick = lambda v: jnp.sum(jnp.where(sel, v[None, :], 0), axis=1)
    pos = first - pick(starts)
    valid = jnp.where(first < ends[-1], jnp.clip(pick(counts) - pos, 0, ROW_TILE), 0)
    cum_t, n_t = cum.T[expert], n.T[expert]
    touches = (cum_t + n_t > pos[:, None]) & (cum_t < (pos + ROW_TILE)[:, None]) & (n_t > 0)
    tau = jnp.arange(n.shape[0], dtype=i32)[None, :]
    lo = jnp.min(jnp.where(touches, tau, n.shape[0]), axis=1)
    hi = jnp.max(jnp.where(touches, tau, -1), axis=1)
    ids = jnp.arange(N_EXPERTS)
    busy = counts > 0
    ordinal = jnp.sum(busy[None, :] & (ids[None, :] < ids[:, None]), axis=1)
    following = jnp.min(jnp.where(busy[None, :] & (ids[None, :] > ids[:, None]), ids[None, :], N_EXPERTS), axis=1)
    following = jnp.where(following < N_EXPERTS, following, -1)
    as_i32 = lambda v: v.astype(i32)
    flat = lambda v: v.reshape(-1).astype(i32)
    expert_tables = (tuple(map(as_i32, (expert, pos, valid, lo, hi, pick(ordinal) % 2, pick(following))))
                     + (flat(n), flat(cum), flat(loff)))
    combine_tables = (flat(n), flat(starts[None, :] + cum), flat(loff))
    return expert_tables, combine_tables


def kernel(x_prompt, x_sample, cache_k, cache_v, state_ssm_re, state_ssm_im, g_norm_mix, w_in, attn_sinks, ssm_a_re,
           ssm_a_im, ssm_log_dt, ssm_b_re, ssm_b_im, ssm_c_re, ssm_c_im, ssm_d, w_glu, b_glu, g_attn_out, g_ssm_out,
           w_out, g_norm_ffn, w_router_group, b_router_group, w_router_expert, b_router_expert, w_exp_gate, w_exp_up,
           w_exp_down, g_final):
    bp, lp, _ = x_prompt.shape
    bs, ls, _ = x_sample.shape
    depth = w_in.shape[0]
    assert depth == 1 and ls == SSM_STEPS and lp % ATTN_TILE == 0 and (bs * ls) % ROW_TILE == 0
    tp, ts = bp * lp, bs * ls
    wc = cache_k.shape[2]
    row2 = lambda v: v.reshape(1, -1)

    xp = x_prompt.reshape(tp, D_MODEL)
    xs = x_sample.reshape(ts, D_MODEL)
    w_in_bf = w_in[0].astype(bf16)
    qp, kp, vp, up = _proj(xp, row2(g_norm_mix[0]), w_in_bf, 512)
    qs, kq, vq, us = _proj(xs, row2(g_norm_mix[0]), w_in_bf, ts)

    sinks = attn_sinks[0]
    g_att = row2(g_attn_out[0])
    ap = _attn_prompt(qp.reshape(bp, lp, -1), kp.reshape(bp, lp, -1), vp.reshape(bp, lp, -1), sinks, g_att)
    a_s, k_roll, v_roll = _attn_sample(qs.reshape(bs, ls, -1), kq.reshape(bs, ls, -1), vq.reshape(bs, ls, -1),
                                       cache_k[0].reshape(bs, wc, KV_WIDTH), cache_v[0].reshape(bs, wc, KV_WIDTH),
                                       sinks, g_att)

    bbar, steps, lev = _ssm_tables(ssm_a_re[0], ssm_a_im[0], ssm_log_dt[0], ssm_b_re[0], ssm_b_im[0])
    c_blocks = lambda c: c.reshape(SSM_BLOCKS, LANES, SSM_STATE)
    d_row = row2(ssm_d[0])
    to_blocks = lambda h: h.reshape(bs, SSM_BLOCKS, STATE_COLS).transpose(1, 0, 2)
    from_blocks = lambda h: h.transpose(1, 0, 2).reshape(bs, SSM_GROUPS, SSM_STATE)
    yp, hrp, hip, ysm, hrs, his = _ssm(up.reshape(bp, lp, -1), us.reshape(1, ts, -1), to_blocks(state_ssm_re[0]),
                                       to_blocks(state_ssm_im[0]), bbar, c_blocks(ssm_c_re[0]),
                                       c_blocks(ssm_c_im[0]), steps, lev, d_row)

    wr = jnp.zeros((D_MODEL, ROUTER_COLS), f32)
    wr = wr.at[:, :N_EXPERTS].set(w_router_expert[0]).at[:, N_EXPERTS:N_EXPERTS + N_EXPERT_GROUPS].set(w_router_group[0])
    br = jnp.zeros((1, ROUTER_COLS), f32)
    br = br.at[0, :N_EXPERTS].set(b_router_expert[0]).at[0, N_EXPERTS:N_EXPERTS + N_EXPERT_GROUPS].set(b_router_group[0])
    tri = jnp.tril(jnp.ones((ROW_TILE, ROW_TILE), bf16), -1)
    upper = jnp.triu(jnp.ones((N_EXPERTS, N_EXPERTS), bf16), 1)
    mix_w = (w_glu[0].astype(bf16), row2(b_glu[0]), row2(g_ssm_out[0]), w_out[0].astype(bf16), row2(g_norm_ffn[0]),
             wr.astype(bf16), br, tri, upper)
    x1, xl, gates, lpos, n_rows, n_off = _mix(ap.reshape(tp, -1), a_s.reshape(ts, -1), yp.reshape(tp, -1),
                                            ysm.reshape(ts, -1), xp, xs, *mix_w)

    per_tile = lambda v: v.reshape(-1, N_EXPERTS).astype(i32)
    tiles = (SLOTS * (tp + ts)) // ROW_TILE + N_EXPERTS
    expert_tables, combine_tables = _moe_tables(per_tile(n_rows), per_tile(n_off), tiles)
    expert_out = _experts(expert_tables, xl, w_exp_gate[0], w_exp_up[0], w_exp_down[0], tiles)
    y_p, y_s = _combine(combine_tables, expert_out, x1, gates, lpos, row2(g_final), tp)

    kvshape = lambda a, b: a.reshape(1, b, -1, N_KV_HEADS, HEAD_DIM)
    block_state = lambda h: h.reshape(bp, SSM_GROUPS, SSM_STATE)[None]
    wcp = min(WINDOW, lp)
    return (y_p.reshape(bp, lp, D_MODEL), y_s.reshape(bs, ls, D_MODEL),
            kvshape(kp.reshape(bp, lp, -1)[:, lp - wcp:], bp), kvshape(vp.reshape(bp, lp, -1)[:, lp - wcp:], bp),
            block_state(hrp), block_state(hip),
            kvshape(k_roll, bs), kvshape(v_roll, bs),
            from_blocks(hrs)[None], from_blocks(his)[None])
```

```python
import functools
import math

import jax
import jax.numpy as jnp
from jax import lax
from jax.experimental import pallas as pl
from jax.experimental.pallas import tpu as pltpu

f32, bf16, i32 = jnp.float32, jnp.bfloat16, jnp.int32

D_MODEL = 1024
CHUNK = 64
N_BACK = 2
WINDOW = 128
ATTN_WIDTH = 512
HEAD_DIM = 64
N_KV_HEADS = 2
Q_PER_KV = 4
KV_WIDTH = 128
SSM_WIDTH = 512
SSM_GROUP = 16
SSM_GROUPS = 32
SSM_STATE = 64
PROJ_WIDTH = 1280
N_EXPERT_GROUPS = 4
EXPERTS_PER_GROUP = 8
N_EXPERTS = 32
D_EXPERT = 512
EPS = 1e-6
NEG = -1e30

LANES = 128
SSM_STEPS = 16
SSM_SEQS = 2
SSM_BLOCKS = SSM_WIDTH // LANES
GROUPS_PER_BLOCK = LANES // SSM_GROUP
STATE_COLS = GROUPS_PER_BLOCK * SSM_STATE
ROW_TILE = 256
SLOTS = 2
PIECES = D_MODEL // LANES
RUN_PIECES = tuple(1 << b for b in reversed(range(int(math.log2(ROW_TILE)) + 1)))
SHORT_PIECES = tuple(p for p in RUN_PIECES if p <= 32)
LONG_PIECES = tuple(p for p in RUN_PIECES if p > 32)
MIX_TILES = 2
UNROLLED_RUNS = 20
EXPERT_RING = 3
COMBINE_RING = 3
VMEM_LIMIT = 56 * 1024 * 1024


def _cparams(n_axes=1, limit=VMEM_LIMIT):
    return pltpu.CompilerParams(dimension_semantics=("arbitrary",) * n_axes, vmem_limit_bytes=limit)


def _rms(x, g):
    return x * lax.rsqrt(jnp.mean(x * x, axis=-1, keepdims=True) + EPS) * g


def _bdot(a, b):
    return jnp.dot(a.astype(bf16), b.astype(bf16), preferred_element_type=f32)


def _proj_body(x_ref, g_ref, w_ref, q_ref, k_ref, v_ref, u_ref):
    h = _rms(x_ref[...], g_ref[...])
    z = _bdot(h, w_ref[...])
    q_ref[...] = z[:, :ATTN_WIDTH] * (HEAD_DIM ** -0.5)
    k_ref[...] = z[:, ATTN_WIDTH:ATTN_WIDTH + KV_WIDTH]
    v_ref[...] = z[:, ATTN_WIDTH + KV_WIDTH:ATTN_WIDTH + 2 * KV_WIDTH]
    u_ref[...] = z[:, ATTN_WIDTH + 2 * KV_WIDTH:]


def _proj(x2d, g, w_bf, tm):
    t = x2d.shape[0]
    row = lambda n: pl.BlockSpec((tm, n), lambda i: (i, 0))
    full = lambda a: pl.BlockSpec(a.shape, lambda i: (0,) * a.ndim)
    return pl.pallas_call(
        _proj_body,
        grid=(t // tm,),
        in_specs=[row(D_MODEL), full(g), full(w_bf)],
        out_specs=[row(ATTN_WIDTH), row(KV_WIDTH), row(KV_WIDTH), row(SSM_WIDTH)],
        out_shape=[jax.ShapeDtypeStruct((t, n), f32) for n in (ATTN_WIDTH, KV_WIDTH, KV_WIDTH, SSM_WIDTH)],
        compiler_params=_cparams(),
        name="proj",
    )(x2d, g, w_bf)


def _sink_column(sink_ref, kv, rows_per_head):
    r = lax.broadcasted_iota(i32, (Q_PER_KV * rows_per_head, 1), 0)
    col = jnp.full((Q_PER_KV * rows_per_head, 1), sink_ref[kv * Q_PER_KV], f32)
    for j in range(1, Q_PER_KV):
        col = jnp.where(r >= j * rows_per_head, sink_ref[kv * Q_PER_KV + j], col)
    return col


def _attend(qs, kc, vc, sink_col, valid):
    s = lax.dot_general(qs.astype(bf16), kc.astype(bf16), (((1,), (1,)), ((), ())), preferred_element_type=f32)
    if valid is not None:
        s = jnp.where(valid, s, NEG)
    m = jnp.maximum(jnp.max(s, axis=-1, keepdims=True), sink_col)
    p = jnp.exp(s - m)
    denom = jnp.sum(p, axis=-1, keepdims=True) + jnp.exp(sink_col - m)
    return _bdot(p, vc) / denom


def _heads_attend(q, k, v, sink_ref, valid):
    rows = q.shape[0]
    pieces = []
    for kv in range(N_KV_HEADS):
        qs = jnp.concatenate(
            [q[:, (kv * Q_PER_KV + j) * HEAD_DIM:(kv * Q_PER_KV + j + 1) * HEAD_DIM] for j in range(Q_PER_KV)], axis=0)
        o = _attend(qs, k[:, kv * HEAD_DIM:(kv + 1) * HEAD_DIM], v[:, kv * HEAD_DIM:(kv + 1) * HEAD_DIM],
                    _sink_column(sink_ref, kv, rows), valid)
        pieces += [o[j * rows:(j + 1) * rows] for j in range(Q_PER_KV)]
    return jnp.concatenate(pieces, axis=1)


ATTN_TILE = 256
CHUNKS_PER_TILE = ATTN_TILE // CHUNK
KEY_SPAN = (N_BACK + 1) * CHUNK


def _attn_prompt_body(sink_ref, q_ref, kp_ref, kc_ref, vp_ref, vc_ref, g_ref, o_ref):
    i = pl.program_id(1)
    kwin = jnp.concatenate([kp_ref[0], kc_ref[0]], axis=0)
    vwin = jnp.concatenate([vp_ref[0], vc_ref[0]], axis=0)
    key_chunk = lax.broadcasted_iota(i32, (1, KEY_SPAN), 1) // CHUNK
    for c in range(CHUNKS_PER_TILE):
        valid = (i * CHUNKS_PER_TILE + c - N_BACK + key_chunk) >= 0
        o = _heads_attend(q_ref[0, c * CHUNK:(c + 1) * CHUNK, :], kwin[c * CHUNK:c * CHUNK + KEY_SPAN],
                          vwin[c * CHUNK:c * CHUNK + KEY_SPAN], sink_ref, valid)
        o_ref[0, c * CHUNK:(c + 1) * CHUNK, :] = _rms(o, g_ref[...]).astype(bf16)


def _attn_prompt(q, k, v, sinks, g):
    b, l, _ = q.shape
    back = N_BACK * CHUNK
    per = ATTN_TILE // back
    prev = pl.BlockSpec((1, back, KV_WIDTH), lambda bi, i: (bi, jnp.maximum(i * per - 1, 0), 0))
    cur = pl.BlockSpec((1, ATTN_TILE, KV_WIDTH), lambda bi, i: (bi, i, 0))
    return pl.pallas_call(
        _attn_prompt_body,
        grid=(b, l // ATTN_TILE),
        in_specs=[pl.BlockSpec(memory_space=pltpu.SMEM),
                  pl.BlockSpec((1, ATTN_TILE, ATTN_WIDTH), lambda bi, i: (bi, i, 0)),
                  prev, cur, prev, cur,
                  pl.BlockSpec((1, ATTN_WIDTH), lambda bi, i: (0, 0))],
        out_specs=pl.BlockSpec((1, ATTN_TILE, ATTN_WIDTH), lambda bi, i: (bi, i, 0)),
        out_shape=jax.ShapeDtypeStruct((b, l, ATTN_WIDTH), bf16),
        compiler_params=_cparams(2),
        name="attn_prompt",
    )(sinks, q, k, k, v, v, g)


def _attn_sample_body(sink_ref, q_ref, kn_ref, vn_ref, ck_ref, cv_ref, g_ref, o_ref, nk_ref, nv_ref):
    kall = jnp.concatenate([ck_ref[0], kn_ref[0]], axis=0)
    vall = jnp.concatenate([cv_ref[0], vn_ref[0]], axis=0)
    o = _heads_attend(q_ref[0], kall, vall, sink_ref, None)
    o_ref[0] = _rms(o, g_ref[...]).astype(bf16)
    n_new = kn_ref.shape[1]
    nk_ref[0] = kall[n_new:]
    nv_ref[0] = vall[n_new:]


def _attn_sample(q, k_new, v_new, cache_k, cache_v, sinks, g):
    b, l, _ = q.shape
    wc = cache_k.shape[1]
    blk = lambda r, n: pl.BlockSpec((1, r, n), lambda bi: (bi, 0, 0))
    return pl.pallas_call(
        _attn_sample_body,
        grid=(b,),
        in_specs=[pl.BlockSpec(memory_space=pltpu.SMEM), blk(l, ATTN_WIDTH), blk(l, KV_WIDTH), blk(l, KV_WIDTH),
                  blk(wc, KV_WIDTH), blk(wc, KV_WIDTH), pl.BlockSpec((1, ATTN_WIDTH), lambda bi: (0, 0))],
        out_specs=[blk(l, ATTN_WIDTH), blk(wc, KV_WIDTH), blk(wc, KV_WIDTH)],
        out_shape=[jax.ShapeDtypeStruct((b, l, ATTN_WIDTH), bf16),
                   jax.ShapeDtypeStruct((b, wc, KV_WIDTH), f32), jax.ShapeDtypeStruct((b, wc, KV_WIDTH), f32)],
        compiler_params=_cparams(),
        name="attn_sample",
    )(sinks, q, k_new, v_new, cache_k, cache_v, g)


def _ssm_tables(a_re, a_im, log_dt, b_re, b_im):
    dt = jnp.exp(log_dt)[:, None]
    lam_r, lam_i = a_re * dt, a_im * dt

    def power(n):
        mag = jnp.exp(n * lam_r)
        return jnp.stack([mag * jnp.cos(n * lam_i), mag * jnp.sin(n * lam_i)])

    ar, ai = power(1.0)
    den = a_re * a_re + a_im * a_im
    nr, ni = ar - 1.0, ai
    fr = ((nr * a_re + ni * a_im) / den)[..., None]
    fi = ((ni * a_re - nr * a_im) / den)[..., None]
    bbar = jnp.stack([fr * b_re - fi * b_im, fr * b_im + fi * b_re])
    bbar = bbar.transpose(0, 1, 3, 2).reshape(2, SSM_BLOCKS, LANES, SSM_STATE)
    by_block = lambda t: t.reshape(t.shape[:-2] + (SSM_BLOCKS, STATE_COLS))
    steps = by_block(power(jnp.arange(SSM_STEPS + 1, dtype=f32)[:, None, None]))
    levels = by_block(power(SSM_STEPS * 2.0 ** jnp.arange(8, dtype=f32)[:, None, None]))
    return bbar, steps.transpose(2, 0, 1, 3), levels.transpose(2, 1, 0, 3)


def _ssm_chunk_rows(u_ref, nk):
    xs = [u_ref[0, pl.ds(s, nk, stride=SSM_STEPS), :] for s in range(SSM_STEPS)]
    pairs = [jnp.concatenate([xs[2 * p], xs[2 * p + 1]], axis=1).astype(bf16) for p in range(SSM_STEPS // 2)]
    return xs, pairs


def _ssm_intra(pairs, toep_ref, nk):
    nd = len(pairs)
    y = [None] * nd
    for d in range(nd):
        lhs = jnp.concatenate(pairs[:nd - d], axis=0) if nd - d > 1 else pairs[0]
        r = jnp.dot(lhs, toep_ref[d], preferred_element_type=f32)
        for p in range(nd - d):
            blk = r[p * nk:(p + 1) * nk]
            y[p + d] = blk if y[p + d] is None else y[p + d] + blk
    return y


def _shift_rows(x, sh):
    rows = lax.broadcasted_iota(i32, (x.shape[0], 1), 0)
    return jnp.where(rows >= sh, pltpu.roll(x, sh, axis=0), 0.0)


def _group_index(shape, axis, sub):
    idx = lax.broadcasted_iota(i32, shape, axis)
    return jnp.right_shift(idx, int(math.log2(sub))) & (GROUPS_PER_BLOCK - 1)


def _build_ssm_weights(bb_ref, cr_ref, ci_ref, pw_ref, toep_s, wout_s, win_s, w_low):
    hp = lax.Precision.HIGHEST
    k = lax.broadcasted_iota(i32, (SSM_STATE, STATE_COLS), 0)
    n = lax.broadcasted_iota(i32, (SSM_STATE, STATE_COLS), 1)
    spread = ((n & (SSM_STATE - 1)) == k).astype(f32)
    own = _group_index((LANES, STATE_COLS), 0, SSM_GROUP) == _group_index((LANES, STATE_COLS), 1, SSM_STATE)

    def block_diag(compact):
        return jnp.where(own, jnp.dot(compact, spread, precision=hp, preferred_element_type=f32), 0.0)

    bb_r, bb_i = block_diag(bb_ref[0, 0]), block_diag(bb_ref[1, 0])
    c_r, c_i = block_diag(cr_ref[0]), block_diag(ci_ref[0])
    c_stack = jnp.concatenate([c_r.T, -c_i.T], axis=0)
    power = lambda e: (pw_ref[0, 0, e:e + 1, :], pw_ref[0, 1, e:e + 1, :])
    for s in range(SSM_STEPS):
        rows = slice(s * LANES, (s + 1) * LANES)
        pr, pi = power(SSM_STEPS - 1 - s)
        w = jnp.concatenate([bb_r * pr - bb_i * pi, bb_r * pi + bb_i * pr], axis=1)
        wout_s[rows, :] = w.astype(bf16)
        w_low[rows, :] = (w - w.astype(bf16).astype(f32)).astype(bf16)
        pr, pi = power(s + 1)
        g = jnp.concatenate([c_r * pr - c_i * pi, -(c_r * pi + c_i * pr)], axis=1)
        win_s[:, rows] = g.T.astype(bf16)
    c_high = c_stack.astype(bf16)
    c_low = (c_stack - c_high.astype(f32)).astype(bf16)
    lags = (jnp.dot(wout_s[...], c_high, preferred_element_type=f32)
            + (jnp.dot(wout_s[...], c_low, preferred_element_type=f32)
               + jnp.dot(w_low[...], c_high, preferred_element_type=f32)))
    lag_kernel = [lags[(SSM_STEPS - 1 - lag) * LANES:(SSM_STEPS - lag) * LANES] for lag in range(SSM_STEPS)]
    zero = jnp.zeros((LANES, LANES), f32)
    for d in range(SSM_STEPS // 2):
        top = jnp.concatenate([lag_kernel[2 * d], lag_kernel[2 * d + 1]], axis=1)
        bottom = jnp.concatenate([lag_kernel[2 * d - 1] if d > 0 else zero, lag_kernel[2 * d]], axis=1)
        toep_s[d] = jnp.concatenate([top, bottom], axis=0).astype(bf16)


def _ssm_chunks(u_ref, d_ref, y_ref, toep_s, wout_s, win_s, entry_state):
    nk = u_ref.shape[1] // SSM_STEPS
    xs, pairs = _ssm_chunk_rows(u_ref, nk)
    y = _ssm_intra(pairs, toep_s, nk)
    s = jnp.dot(jnp.concatenate(pairs, axis=1), wout_s[...], preferred_element_type=f32)
    hprev, hr, hi = entry_state(s[:, :STATE_COLS], s[:, STATE_COLS:])
    y2 = _bdot(hprev, win_s[...])
    for st in range(SSM_STEPS):
        piece = (y[st // 2][:, (st % 2) * LANES:(st % 2 + 1) * LANES] + y2[:, st * LANES:(st + 1) * LANES]
                 + d_ref[...] * xs[st])
        y_ref[0, pl.ds(st, nk, stride=SSM_STEPS), :] = piece
    return hr, hi


def _ssm_body(u_ref, us_ref, h0r_ref, h0i_ref, bb_ref, cr_ref, ci_ref, pw_ref, lev_ref, d_ref,
              y_ref, hr_ref, hi_ref, ys_ref, hrs_ref, his_ref, toep_s, wout_s, win_s, w_low):
    @pl.when(pl.program_id(1) == 0)
    def _():
        _build_ssm_weights(bb_ref, cr_ref, ci_ref, pw_ref, toep_s, wout_s, win_s, w_low)

        def one_chunk(sr, si):
            h0r, h0i = h0r_ref[0], h0i_ref[0]
            ar, ai = lev_ref[0, 0, 0:1, :], lev_ref[0, 0, 1:2, :]
            return (jnp.concatenate([h0r, h0i], axis=1), sr + ar * h0r - ai * h0i, si + ar * h0i + ai * h0r)
        hrs_ref[0], his_ref[0] = _ssm_chunks(us_ref, d_ref, ys_ref, toep_s, wout_s, win_s, one_chunk)

    def scan_chunks(sr, si):
        nk = sr.shape[0]
        level = 0
        while (1 << level) < nk:
            ar, ai = lev_ref[0, level, 0:1, :], lev_ref[0, level, 1:2, :]
            tr, ti = _shift_rows(sr, 1 << level), _shift_rows(si, 1 << level)
            sr, si = sr + ar * tr - ai * ti, si + ar * ti + ai * tr
            level += 1
        return (jnp.concatenate([_shift_rows(sr, 1), _shift_rows(si, 1)], axis=1), sr[nk - 1:nk], si[nk - 1:nk])
    for b in range(u_ref.shape[0]):
        hr_ref[b, 0], hi_ref[b, 0] = _ssm_chunks(u_ref.at[pl.ds(b, 1)], d_ref, y_ref.at[pl.ds(b, 1)], toep_s, wout_s,
                                                 win_s, scan_chunks)


def _ssm(u, us, h0r, h0i, bbar, c_re, c_im, steps, lev, d):
    b, l, _ = u.shape
    rows = us.shape[1]
    nb = rows // SSM_STEPS
    wspec = lambda a: pl.BlockSpec((1,) + a.shape[1:], lambda j, bi: (j,) + (0,) * (a.ndim - 1))
    assert b % SSM_SEQS == 0
    st = pl.BlockSpec((SSM_SEQS, 1, 1, STATE_COLS), lambda j, bi: (bi, j, 0, 0))
    sst = pl.BlockSpec((1, nb, STATE_COLS), lambda j, bi: (j, 0, 0))
    seq = pl.BlockSpec((SSM_SEQS, l, LANES), lambda j, bi: (bi, 0, j))
    sseq = pl.BlockSpec((1, rows, LANES), lambda j, bi: (0, 0, j))
    return pl.pallas_call(
        _ssm_body,
        grid=(SSM_BLOCKS, b // SSM_SEQS),
        in_specs=[seq, sseq, sst, sst, pl.BlockSpec((2, 1, LANES, SSM_STATE), lambda j, bi: (0, j, 0, 0)),
                  wspec(c_re), wspec(c_im), wspec(steps), wspec(lev), pl.BlockSpec((1, LANES), lambda j, bi: (0, j))],
        out_specs=[seq, st, st, sseq, sst, sst],
        out_shape=[jax.ShapeDtypeStruct((b, l, SSM_WIDTH), f32),
                   jax.ShapeDtypeStruct((b, SSM_BLOCKS, 1, STATE_COLS), f32),
                   jax.ShapeDtypeStruct((b, SSM_BLOCKS, 1, STATE_COLS), f32),
                   jax.ShapeDtypeStruct((1, rows, SSM_WIDTH), f32),
                   jax.ShapeDtypeStruct((SSM_BLOCKS, nb, STATE_COLS), f32),
                   jax.ShapeDtypeStruct((SSM_BLOCKS, nb, STATE_COLS), f32)],
        scratch_shapes=[pltpu.VMEM((SSM_STEPS // 2, 2 * LANES, 2 * LANES), bf16),
                        pltpu.VMEM((SSM_STEPS * LANES, 2 * STATE_COLS), bf16),
                        pltpu.VMEM((2 * STATE_COLS, SSM_STEPS * LANES), bf16),
                        pltpu.VMEM((SSM_STEPS * LANES, 2 * STATE_COLS), bf16)],
        compiler_params=_cparams(2),
        name="ssm",
    )(u, us, h0r, h0i, bbar, c_re, c_im, steps, lev, d)


ROUTER_COLS = LANES


def _mix_body(ap_ref, as_ref, yp_ref, ys_ref, xp_ref, xs_ref, wglu_ref, bglu_ref, gs_ref, wout_ref, gf_ref, wr_ref,
              br_ref, tri_ref, upper_ref, x1_ref, xl_ref, wts_ref, lpos_ref, n_ref, loff_ref, *, prompt_steps):
    is_prompt = pl.program_id(0) < prompt_steps
    pick_rows = lambda p_ref, s_ref: jnp.where(is_prompt, p_ref[...], jnp.concatenate([s_ref[...]] * MIX_TILES, axis=0))
    y = pick_rows(yp_ref, ys_ref)
    y = 0.5 * y * (1.0 + jnp.tanh(math.sqrt(2.0 / math.pi) * (y + 0.044715 * (y * y * y))))
    y = y * jax.nn.sigmoid(_bdot(y, wglu_ref[...]) + bglu_ref[...])
    attn = jnp.where(is_prompt, ap_ref[...].astype(f32),
                     jnp.concatenate([as_ref[...].astype(f32)] * MIX_TILES, axis=0)).astype(bf16)
    cat = jnp.concatenate([attn, _rms(y, gs_ref[...]).astype(bf16)], axis=1)
    x1 = pick_rows(xp_ref, xs_ref) + jnp.dot(cat, wout_ref[...], preferred_element_type=f32)
    x1_ref[...] = x1
    hf = _rms(x1, gf_ref[...])

    logits = _bdot(hf, wr_ref[...]) + br_ref[...]
    le = logits[:, :N_EXPERTS]
    lg = logits[:, N_EXPERTS:N_EXPERTS + N_EXPERT_GROUPS]
    tm = le.shape[0]
    gmax = jnp.max(lg, axis=-1, keepdims=True)
    gi = lax.broadcasted_iota(i32, (tm, N_EXPERT_GROUPS), 1).astype(f32)
    gsel = jnp.min(jnp.where(lg == gmax, gi, float(N_EXPERT_GROUPS)), axis=-1, keepdims=True)
    pg = 1.0 / jnp.sum(jnp.exp(lg - gmax), axis=-1, keepdims=True)
    ei_int = lax.broadcasted_iota(i32, (tm, N_EXPERTS), 1)
    ei = ei_int.astype(f32)
    egroup = jnp.right_shift(ei_int, int(math.log2(EXPERTS_PER_GROUP))).astype(f32)
    lm = jnp.where(egroup == gsel, le, NEG)
    v1 = jnp.max(lm, axis=-1, keepdims=True)
    i1 = jnp.min(jnp.where(lm == v1, ei, float(N_EXPERTS)), axis=-1, keepdims=True)
    lm2 = jnp.where(ei == i1, NEG, lm)
    v2 = jnp.max(lm2, axis=-1, keepdims=True)
    i2 = jnp.min(jnp.where(lm2 == v2, ei, float(N_EXPERTS)), axis=-1, keepdims=True)
    ex = jnp.exp(v2 - v1)
    wts_ref[...] = jnp.concatenate([pg / (1.0 + ex), pg * ex / (1.0 + ex)], axis=1)

    oh1 = (ei == i1).astype(f32)
    oh2 = (ei == i2).astype(f32)
    hf_bf = hf.astype(bf16)
    sorted_row = lax.broadcasted_iota(i32, (ROW_TILE, SLOTS * ROW_TILE), 1).astype(f32)
    for h in range(MIX_TILES):
        rows = slice(h * ROW_TILE, (h + 1) * ROW_TILE)
        both = (oh1[rows] + oh2[rows]).astype(bf16)
        before = jnp.dot(tri_ref[...], both, preferred_element_type=f32)
        count = jnp.sum(oh1[rows] + oh2[rows], axis=0, keepdims=True)
        lower = jnp.sum(jnp.dot(both, upper_ref[...], preferred_element_type=f32), axis=0, keepdims=True)
        lp1 = jnp.sum(oh1[rows] * (before + lower), axis=-1, keepdims=True)
        lp2 = jnp.sum(oh2[rows] * (before + lower), axis=-1, keepdims=True)
        lpos_ref[rows, :] = jnp.concatenate([lp1, lp2], axis=1)
        n_ref[h] = count
        loff_ref[h] = lower
        pick = ((sorted_row == lp1) | (sorted_row == lp2)).astype(bf16)
        xl = lax.dot_general(pick, hf_bf[rows], (((0,), (0,)), ((), ())), preferred_element_type=f32)
        _store_row_major(xl_ref, h * SLOTS * ROW_TILE, xl)


def _mix(a_p, a_s, ys_p, ys_s, x_p, x_s, wglu, bglu, gs, wout, gf, wr, br, tri, upper):
    tm = MIX_TILES * ROW_TILE
    assert x_p.shape[0] % tm == 0 and x_s.shape[0] == ROW_TILE
    prompt_steps = x_p.shape[0] // tm
    t = x_p.shape[0] + x_s.shape[0]
    p_rows = lambda n: pl.BlockSpec((tm, n), lambda i: (jnp.minimum(i, prompt_steps - 1), 0))
    s_rows = lambda n: pl.BlockSpec((ROW_TILE, n), lambda i: (0, 0))
    row = lambda n: pl.BlockSpec((tm, n), lambda i: (i, 0))
    full = lambda arr: pl.BlockSpec(arr.shape, lambda i: (0,) * arr.ndim)
    per_tile = pl.BlockSpec((MIX_TILES, 1, N_EXPERTS), lambda i: (i, 0, 0))
    return pl.pallas_call(
        functools.partial(_mix_body, prompt_steps=prompt_steps),
        grid=(prompt_steps + 1,),
        in_specs=[p_rows(ATTN_WIDTH), s_rows(ATTN_WIDTH), p_rows(SSM_WIDTH), s_rows(SSM_WIDTH), p_rows(D_MODEL),
                  s_rows(D_MODEL), full(wglu), full(bglu), full(gs), full(wout), full(gf), full(wr), full(br),
                  full(tri), full(upper)],
        out_specs=[row(D_MODEL), pl.BlockSpec((SLOTS * tm * PIECES, LANES), lambda i: (i, 0)), row(2), row(2),
                   per_tile, per_tile],
        out_shape=[jax.ShapeDtypeStruct((t, D_MODEL), f32),
                   jax.ShapeDtypeStruct((SLOTS * t * PIECES, LANES), f32),
                   jax.ShapeDtypeStruct((t, 2), f32), jax.ShapeDtypeStruct((t, 2), f32),
                   jax.ShapeDtypeStruct((t // ROW_TILE, 1, N_EXPERTS), f32),
                   jax.ShapeDtypeStruct((t // ROW_TILE, 1, N_EXPERTS), f32)],
        compiler_params=_cparams(),
        name="mix",
    )(a_p, a_s, ys_p, ys_s, x_p, x_s, wglu, bglu, gs, wout, gf, wr, br, tri, upper)


def _store_row_major(ref, first_row, x):
    for c in range(PIECES):
        ref[pl.ds(first_row * PIECES + c, x.shape[0], stride=PIECES), :] = x[:, c * LANES:(c + 1) * LANES]


def _load_row_major(ref, n_rows):
    return jnp.concatenate([ref[pl.ds(c, n_rows, stride=PIECES), :] for c in range(PIECES)], axis=1)


def _copy_rows(src, s_row, dst, d_row, n_rows, sem):
    return pltpu.make_async_copy(src.at[pl.ds(pl.multiple_of(s_row * PIECES, PIECES), n_rows * PIECES), :],
                                 dst.at[pl.ds(pl.multiple_of(d_row * PIECES, PIECES), n_rows * PIECES), :], sem)


def _for_each_piece(n, fn, pieces=RUN_PIECES):
    off = 0 if pieces[0] == RUN_PIECES[0] else n & ~(2 * pieces[0] - 1)
    for piece in pieces:
        @pl.when((n & piece) != 0)
        def _(off=off, piece=piece):
            fn(off, piece)
        off = off + (n & piece)


def _experts_body(te_ref, tpos_ref, tvalid_ref, tlo_ref, thi_ref, wslot_ref, wnext_ref, n_ref, cum_ref, loff_ref,
                  xl_hbm, wg_hbm, wu_hbm, wd_hbm, o_ref, wg_s, wu_s, wd_s, wg_f, wu_f, wd_f, wsem, sem, *xbufs):
    i = pl.program_id(0)
    last = pl.num_programs(0) - 1
    token_tiles = n_ref.shape[0] // N_EXPERTS
    ring = len(xbufs)

    def start_run(t, tau, enabled, buf, buf_sem):
        e, lo = te_ref[t], tpos_ref[t]
        k = tau * N_EXPERTS + e
        s, n = cum_ref[k], n_ref[k]
        a = jnp.maximum(s, lo)
        length = jnp.where(enabled, jnp.maximum(jnp.minimum(s + n, lo + ROW_TILE) - a, 0), 0)
        local = loff_ref[k] + (a - s)
        _for_each_piece(length, lambda off, piece: _copy_rows(
            xl_hbm, tau * (SLOTS * ROW_TILE) + local + off, buf, a - lo + off, piece, buf_sem).start())

    def start_runs_loop(t, first, stop, buf, buf_sem):
        def run(tau, c):
            start_run(t, tau, True, buf, buf_sem)
            return c
        lax.fori_loop(first, stop, run, 0)

    def clear(buf):
        buf[...] = jnp.zeros_like(buf)

    @pl.when(i == 0)
    def _():
        for buf in xbufs:
            clear(buf)
        for t in range(ring - 1):
            @pl.when(tvalid_ref[t] > 0)
            def _(t=t):
                start_runs_loop(t, tlo_ref[t], thi_ref[t] + 1, xbufs[t], sem.at[t])

    valid = tvalid_ref[i]

    def weight_copies(e, slot):
        return [pltpu.make_async_copy(w_hbm.at[e], w_f.at[slot], wsem.at[slot])
                for w_hbm, w_f in ((wg_hbm, wg_f), (wu_hbm, wu_f), (wd_hbm, wd_f))]

    @pl.when((i == 0) & (valid > 0))
    def _():
        for cp in weight_copies(te_ref[0], wslot_ref[0]):
            cp.start()

    @pl.when((valid > 0) & ((i == 0) | (te_ref[i] != te_ref[jnp.maximum(i - 1, 0)])))
    def _():
        slot = wslot_ref[i]
        for cp in weight_copies(te_ref[i], slot):
            cp.wait()
        wg_s[...] = wg_f[slot].astype(bf16)
        wu_s[...] = wu_f[slot].astype(bf16)
        wd_s[...] = wd_f[slot].astype(bf16)

        @pl.when(wnext_ref[i] >= 0)
        def _():
            for cp in weight_copies(wnext_ref[i], 1 - slot):
                cp.start()

    def tile_step(cur):
        ahead = (cur + ring - 1) % ring
        buf, buf_sem, next_buf, next_sem = xbufs[cur], sem.at[cur], xbufs[ahead], sem.at[ahead]
        _for_each_piece(valid, lambda off, piece: _copy_rows(
            xl_hbm, 0, buf, 0, piece, buf_sem).wait())

        nxt = jnp.minimum(i + ring - 1, last)
        go = (i + ring - 1 <= last) & (tvalid_ref[nxt] > 0)
        first, final = tlo_ref[nxt], thi_ref[nxt]
        for j in range(UNROLLED_RUNS):
            start_run(nxt, jnp.minimum(first + j, token_tiles - 1), go & (first + j <= final), next_buf, next_sem)

        x = _load_row_major(buf, ROW_TILE).astype(bf16)
        clear(buf)
        hg = jnp.dot(x, wg_s[...], preferred_element_type=f32)
        hu = jnp.dot(x, wu_s[...], preferred_element_type=f32)
        y = jnp.dot((hg * jax.nn.sigmoid(hg) * hu).astype(bf16), wd_s[...], preferred_element_type=f32)
        _store_row_major(o_ref, 0, y)

        @pl.when(go & (final - first >= UNROLLED_RUNS))
        def _():
            start_runs_loop(nxt, first + UNROLLED_RUNS, final + 1, next_buf, next_sem)

    for cur in range(ring):
        pl.when((valid > 0) & (i % ring == cur))(functools.partial(tile_step, cur))

    @pl.when(valid == 0)
    def _():
        o_ref[...] = jnp.zeros_like(o_ref)


def _experts(tables, xl, wg, wu, wd, tiles):
    hbm = pl.BlockSpec(memory_space=pl.ANY)
    return pl.pallas_call(
        _experts_body,
        grid_spec=pltpu.PrefetchScalarGridSpec(
            num_scalar_prefetch=len(tables),
            grid=(tiles,),
            in_specs=[hbm, hbm, hbm, hbm],
            out_specs=pl.BlockSpec((ROW_TILE * PIECES, LANES), lambda i, *_: (i, 0)),
            scratch_shapes=[pltpu.VMEM(w.shape[1:], bf16) for w in (wg, wu, wd)]
            + [pltpu.VMEM((2,) + w.shape[1:], f32) for w in (wg, wu, wd)]
            + [pltpu.SemaphoreType.DMA((2,)), pltpu.SemaphoreType.DMA((EXPERT_RING,))]
            + [pltpu.VMEM((ROW_TILE * PIECES, LANES), f32)] * EXPERT_RING),
        out_shape=jax.ShapeDtypeStruct((tiles * ROW_TILE * PIECES, LANES), f32),
        compiler_params=_cparams(),
        name="moe_experts",
    )(*tables, xl, wg, wu, wd)


def _combine_body(n_ref, gpos_ref, loff_ref, ys_hbm, x_ref, w_ref, l_ref, g_ref, op_ref, os_ref, sem, *ybufs,
                  prompt_tiles):
    i = pl.program_id(0)
    last = pl.num_programs(0) - 1
    tile_rows = SLOTS * ROW_TILE
    ring = len(ybufs)

    def start_run(t, e, enabled, buf, buf_sem, pieces=RUN_PIECES):
        k = t * N_EXPERTS + e
        _for_each_piece(jnp.where(enabled, n_ref[k], 0), lambda off, piece: _copy_rows(
            ys_hbm, gpos_ref[k] + off, buf, loff_ref[k] + off, piece, buf_sem).start(), pieces)

    @pl.when(i == 0)
    def _():
        for t in range(ring - 1):
            def run(e, c, t=t):
                start_run(t, e, True, ybufs[t], sem.at[t])
                return c
            lax.fori_loop(0, N_EXPERTS, run, 0)

    def step(cur):
        ahead = (cur + ring - 1) % ring
        buf, buf_sem, next_buf, next_sem = ybufs[cur], sem.at[cur], ybufs[ahead], sem.at[ahead]
        _copy_rows(ys_hbm, 0, buf, 0, tile_rows, buf_sem).wait()
        nxt, go = jnp.minimum(i + ring - 1, last), i + ring - 1 <= last
        longest = 0
        for e in range(N_EXPERTS):
            start_run(nxt, e, go, next_buf, next_sem, SHORT_PIECES)
            longest = longest | n_ref[nxt * N_EXPERTS + e]
        yl = _load_row_major(buf, tile_rows).astype(bf16)
        sorted_row = lax.broadcasted_iota(i32, (ROW_TILE, tile_rows), 1).astype(f32)
        w, lp = w_ref[...], l_ref[...]
        y1 = jnp.dot((sorted_row == lp[:, 0:1]).astype(bf16), yl, preferred_element_type=f32)
        y2 = jnp.dot((sorted_row == lp[:, 1:2]).astype(bf16), yl, preferred_element_type=f32)
        out = _rms(x_ref[...] + (w[:, 0:1] * y1 + w[:, 1:2] * y2), g_ref[...])

        @pl.when(i < prompt_tiles)
        def _():
            op_ref[...] = out

        @pl.when(i >= prompt_tiles)
        def _():
            os_ref[...] = out

        @pl.when(go & (longest >= 2 * SHORT_PIECES[0]))
        def _():
            def long_pieces(e, c):
                start_run(nxt, e, True, next_buf, next_sem, LONG_PIECES)
                return c
            lax.fori_loop(0, N_EXPERTS, long_pieces, 0)

    for cur in range(ring):
        pl.when(i % ring == cur)(functools.partial(step, cur))


def _combine(tables, ys, x1, wts, lpos, g, prompt_rows):
    prompt_tiles = prompt_rows // ROW_TILE
    tiles = x1.shape[0] // ROW_TILE
    row = lambda n: pl.BlockSpec((ROW_TILE, n), lambda i, *_: (i, 0))
    p_rows = lambda n: pl.BlockSpec((ROW_TILE, n), lambda i, *_: (jnp.minimum(i, prompt_tiles - 1), 0))
    s_rows = lambda n: pl.BlockSpec((ROW_TILE, n), lambda i, *_: (jnp.maximum(i - prompt_tiles, 0), 0))
    return pl.pallas_call(
        functools.partial(_combine_body, prompt_tiles=prompt_tiles),
        grid_spec=pltpu.PrefetchScalarGridSpec(
            num_scalar_prefetch=len(tables),
            grid=(tiles,),
            in_specs=[pl.BlockSpec(memory_space=pl.ANY), row(D_MODEL), row(2), row(2),
                      pl.BlockSpec((1, D_MODEL), lambda i, *_: (0, 0))],
            out_specs=[p_rows(D_MODEL), s_rows(D_MODEL)],
            scratch_shapes=[pltpu.SemaphoreType.DMA((COMBINE_RING,))]
            + [pltpu.VMEM((SLOTS * ROW_TILE * PIECES, LANES), f32)] * COMBINE_RING),
        out_shape=[jax.ShapeDtypeStruct((prompt_rows, D_MODEL), f32),
                   jax.ShapeDtypeStruct((x1.shape[0] - prompt_rows, D_MODEL), f32)],
        compiler_params=_cparams(),
        name="moe_combine",
    )(*tables, ys, x1, wts, lpos, g)


def _moe_tables(n, loff, tiles):
    cum = jnp.cumsum(n, axis=0) - n
    counts = jnp.sum(n, axis=0)
    padded = (counts + ROW_TILE - 1) // ROW_TILE * ROW_TILE
    ends = jnp.cumsum(padded)
    starts = ends - padded
    first = jnp.arange(tiles, dtype=i32) * ROW_TILE
    expert = jnp.minimum(jnp.sum((first[:, None] >= ends[None, :]).astype(i32), axis=1), N_EXPERTS - 1)
    sel = expert[:, None] == jnp.arange(N_EXPERTS)[None, :]
    pick = lambda v: jnp.sum(jnp.where(sel, v[None, :], 0), axis=1)
    pos = first - pick(starts)
    valid = jnp.where(first < ends[-1], jnp.clip(pick(counts) - pos, 0, ROW_TILE), 0)
    cum_t, n_t = cum.T[expert], n.T[expert]
    touches = (cum_t + n_t > pos[:, None]) & (cum_t < (pos + ROW_TILE)[:, None]) & (n_t > 0)
    tau = jnp.arange(n.shape[0], dtype=i32)[None, :]
    lo = jnp.min(jnp.where(touches, tau, n.shape[0]), axis=1)
    hi = jnp.max(jnp.where(touches, tau, -1), axis=1)
    ids = jnp.arange(N_EXPERTS)
    busy = counts > 0
    ordinal = jnp.sum(busy[None, :] & (ids[None, :] < ids[:, None]), axis=1)
    following = jnp.min(jnp.where(busy[None, :] & (ids[None, :] > ids[:, None]), ids[None, :], N_EXPERTS), axis=1)
    following = jnp.where(following < N_EXPERTS, following, -1)
    as_i32 = lambda v: v.astype(i32)
    flat = lambda v: v.reshape(-1).astype(i32)
    expert_tables = (tuple(map(as_i32, (expert, pos, valid, lo, hi, pick(ordinal) % 2, pick(following))))
                     + (flat(n), flat(cum), flat(loff)))
    combine_tables = (flat(n), flat(starts[None, :] + cum), flat(loff))
    return expert_tables, combine_tables


def kernel(x_prompt, x_sample, cache_k, cache_v, state_ssm_re, state_ssm_im, g_norm_mix, w_in, attn_sinks, ssm_a_re,
           ssm_a_im, ssm_log_dt, ssm_b_re, ssm_b_im, ssm_c_re, ssm_c_im, ssm_d, w_glu, b_glu, g_attn_out, g_ssm_out,
           w_out, g_norm_ffn, w_router_group, b_router_group, w_router_expert, b_router_expert, w_exp_gate, w_exp_up,
           w_exp_down, g_final):
    bp, lp, _ = x_prompt.shape
    bs, ls, _ = x_sample.shape
    depth = w_in.shape[0]
    assert depth == 1 and ls == SSM_STEPS and lp % ATTN_TILE == 0 and (bs * ls) % ROW_TILE == 0
    tp, ts = bp * lp, bs * ls
    wc = cache_k.shape[2]
    row2 = lambda v: v.reshape(1, -1)

    xp = x_prompt.reshape(tp, D_MODEL)
    xs = x_sample.reshape(ts, D_MODEL)
    w_in_bf = w_in[0].astype(bf16)
    qp, kp, vp, up = _proj(xp, row2(g_norm_mix[0]), w_in_bf, 512)
    qs, kq, vq, us = _proj(xs, row2(g_norm_mix[0]), w_in_bf, ts)

    sinks = attn_sinks[0]
    g_att = row2(g_attn_out[0])
    ap = _attn_prompt(qp.reshape(bp, lp, -1), kp.reshape(bp, lp, -1), vp.reshape(bp, lp, -1), sinks, g_att)
    a_s, k_roll, v_roll = _attn_sample(qs.reshape(bs, ls, -1), kq.reshape(bs, ls, -1), vq.reshape(bs, ls, -1),
                                       cache_k[0].reshape(bs, wc, KV_WIDTH), cache_v[0].reshape(bs, wc, KV_WIDTH),
                                       sinks, g_att)

    bbar, steps, lev = _ssm_tables(ssm_a_re[0], ssm_a_im[0], ssm_log_dt[0], ssm_b_re[0], ssm_b_im[0])
    c_blocks = lambda c: c.reshape(SSM_BLOCKS, LANES, SSM_STATE)
    d_row = row2(ssm_d[0])
    to_blocks = lambda h: h.reshape(bs, SSM_BLOCKS, STATE_COLS).transpose(1, 0, 2)
    from_blocks = lambda h: h.transpose(1, 0, 2).reshape(bs, SSM_GROUPS, SSM_STATE)
    yp, hrp, hip, ysm, hrs, his = _ssm(up.reshape(bp, lp, -1), us.reshape(1, ts, -1), to_blocks(state_ssm_re[0]),
                                       to_blocks(state_ssm_im[0]), bbar, c_blocks(ssm_c_re[0]),
                                       c_blocks(ssm_c_im[0]), steps, lev, d_row)

    wr = jnp.zeros((D_MODEL, ROUTER_COLS), f32)
    wr = wr.at[:, :N_EXPERTS].set(w_router_expert[0]).at[:, N_EXPERTS:N_EXPERTS + N_EXPERT_GROUPS].set(w_router_group[0])
    br = jnp.zeros((1, ROUTER_COLS), f32)
    br = br.at[0, :N_EXPERTS].set(b_router_expert[0]).at[0, N_EXPERTS:N_EXPERTS + N_EXPERT_GROUPS].set(b_router_group[0])
    tri = jnp.tril(jnp.ones((ROW_TILE, ROW_TILE), bf16), -1)
    upper = jnp.triu(jnp.ones((N_EXPERTS, N_EXPERTS), bf16), 1)
    mix_w = (w_glu[0].astype(bf16), row2(b_glu[0]), row2(g_ssm_out[0]), w_out[0].astype(bf16), row2(g_norm_ffn[0]),
             wr.astype(bf16), br, tri, upper)
    x1, xl, wts, lpos, n_rows, n_off = _mix(ap.reshape(tp, -1), a_s.reshape(ts, -1), yp.reshape(tp, -1),
                                            ysm.reshape(ts, -1), xp, xs, *mix_w)

    per_tile = lambda v: v.reshape(-1, N_EXPERTS).astype(i32)
    tiles = (SLOTS * (tp + ts)) // ROW_TILE + N_EXPERTS
    expert_tables, combine_tables = _moe_tables(per_tile(n_rows), per_tile(n_off), tiles)
    expert_out = _experts(expert_tables, xl, w_exp_gate[0], w_exp_up[0], w_exp_down[0], tiles)
    y_p, y_s = _combine(combine_tables, expert_out, x1, wts, lpos, row2(g_final), tp)

    kvshape = lambda a, b: a.reshape(1, b, -1, N_KV_HEADS, HEAD_DIM)
    block_state = lambda h: h.reshape(bp, SSM_GROUPS, SSM_STATE)[None]
    wcp = min(WINDOW, lp)
    return (y_p.reshape(bp, lp, D_MODEL), y_s.reshape(bs, ls, D_MODEL),
            kvshape(kp.reshape(bp, lp, -1)[:, lp - wcp:], bp), kvshape(vp.reshape(bp, lp, -1)[:, lp - wcp:], bp),
            block_state(hrp), block_state(hip),
            kvshape(k_roll, bs), kvshape(v_roll, bs),
            from_blocks(hrs)[None], from_blocks(his)[None])
```

```python
import functools
import math

import jax
import jax.numpy as jnp
from jax import lax
from jax.experimental import pallas as pl
from jax.experimental.pallas import tpu as pltpu

f32, bf16, i32 = jnp.float32, jnp.bfloat16, jnp.int32

D_MODEL = 1024
CHUNK = 64
N_BACK = 2
WINDOW = 128
ATTN_WIDTH = 512
HEAD_DIM = 64
N_KV_HEADS = 2
Q_PER_KV = 4
KV_WIDTH = 128
SSM_WIDTH = 512
SSM_GROUP = 16
SSM_GROUPS = 32
SSM_STATE = 64
PROJ_WIDTH = 1280
N_EXPERT_GROUPS = 4
EXPERTS_PER_GROUP = 8
N_EXPERTS = 32
D_EXPERT = 512
EPS = 1e-6
NEG = -1e30

LANES = 128
SSM_STEPS = 16
SSM_SEQS = 2
SSM_BLOCKS = SSM_WIDTH // LANES
GROUPS_PER_BLOCK = LANES // SSM_GROUP
STATE_COLS = GROUPS_PER_BLOCK * SSM_STATE
ROW_TILE = 256
SLOTS = 2
PIECES = D_MODEL // LANES
RUN_PIECES = tuple(1 << b for b in reversed(range(int(math.log2(ROW_TILE)) + 1)))
SHORT_PIECES = tuple(p for p in RUN_PIECES if p <= 32)
LONG_PIECES = tuple(p for p in RUN_PIECES if p > 32)
MIX_TILES = 2
UNROLLED_RUNS = 20
EXPERT_RING = 3
COMBINE_RING = 3
VMEM_LIMIT = 56 * 1024 * 1024


def _cparams(n_axes=1, limit=VMEM_LIMIT):
    return pltpu.CompilerParams(dimension_semantics=("arbitrary",) * n_axes, vmem_limit_bytes=limit)


def _rms(x, g):
    return x * lax.rsqrt(jnp.mean(x * x, axis=-1, keepdims=True) + EPS) * g


def _bdot(a, b):
    return jnp.dot(a.astype(bf16), b.astype(bf16), preferred_element_type=f32)


def _proj_body(x_ref, g_ref, w_ref, q_ref, k_ref, v_ref, u_ref):
    h = _rms(x_ref[...], g_ref[...])
    z = _bdot(h, w_ref[...])
    q_ref[...] = z[:, :ATTN_WIDTH] * (HEAD_DIM ** -0.5)
    k_ref[...] = z[:, ATTN_WIDTH:ATTN_WIDTH + KV_WIDTH]
    v_ref[...] = z[:, ATTN_WIDTH + KV_WIDTH:ATTN_WIDTH + 2 * KV_WIDTH]
    u_ref[...] = z[:, ATTN_WIDTH + 2 * KV_WIDTH:]


def _proj(x2d, g, w_bf, tm):
    t = x2d.shape[0]
    row = lambda n: pl.BlockSpec((tm, n), lambda i: (i, 0))
    full = lambda a: pl.BlockSpec(a.shape, lambda i: (0,) * a.ndim)
    return pl.pallas_call(
        _proj_body,
        grid=(t // tm,),
        in_specs=[row(D_MODEL), full(g), full(w_bf)],
        out_specs=[row(ATTN_WIDTH), row(KV_WIDTH), row(KV_WIDTH), row(SSM_WIDTH)],
        out_shape=[jax.ShapeDtypeStruct((t, n), f32) for n in (ATTN_WIDTH, KV_WIDTH, KV_WIDTH, SSM_WIDTH)],
        compiler_params=_cparams(),
        name="proj",
    )(x2d, g, w_bf)


def _sink_column(sink_ref, kv, rows_per_head):
    r = lax.broadcasted_iota(i32, (Q_PER_KV * rows_per_head, 1), 0)
    col = jnp.full((Q_PER_KV * rows_per_head, 1), sink_ref[kv * Q_PER_KV], f32)
    for j in range(1, Q_PER_KV):
        col = jnp.where(r >= j * rows_per_head, sink_ref[kv * Q_PER_KV + j], col)
    return col


def _attend(qs, kc, vc, sink_col, valid):
    s = lax.dot_general(qs.astype(bf16), kc.astype(bf16), (((1,), (1,)), ((), ())), preferred_element_type=f32)
    if valid is not None:
        s = jnp.where(valid, s, NEG)
    m = jnp.maximum(jnp.max(s, axis=-1, keepdims=True), sink_col)
    p = jnp.exp(s - m)
    denom = jnp.sum(p, axis=-1, keepdims=True) + jnp.exp(sink_col - m)
    return _bdot(p, vc) / denom


def _heads_attend(q, k, v, sink_ref, valid):
    rows = q.shape[0]
    pieces = []
    for kv in range(N_KV_HEADS):
        qs = jnp.concatenate(
            [q[:, (kv * Q_PER_KV + j) * HEAD_DIM:(kv * Q_PER_KV + j + 1) * HEAD_DIM] for j in range(Q_PER_KV)], axis=0)
        o = _attend(qs, k[:, kv * HEAD_DIM:(kv + 1) * HEAD_DIM], v[:, kv * HEAD_DIM:(kv + 1) * HEAD_DIM],
                    _sink_column(sink_ref, kv, rows), valid)
        pieces += [o[j * rows:(j + 1) * rows] for j in range(Q_PER_KV)]
    return jnp.concatenate(pieces, axis=1)


ATTN_TILE = 512
CHUNKS_PER_TILE = ATTN_TILE // CHUNK
KEY_SPAN = (N_BACK + 1) * CHUNK


def _attn_prompt_body(sink_ref, q_ref, kp_ref, kc_ref, vp_ref, vc_ref, g_ref, o_ref):
    i = pl.program_id(1)
    kwin = jnp.concatenate([kp_ref[0], kc_ref[0]], axis=0)
    vwin = jnp.concatenate([vp_ref[0], vc_ref[0]], axis=0)
    key_chunk = lax.broadcasted_iota(i32, (1, KEY_SPAN), 1) // CHUNK
    for c in range(CHUNKS_PER_TILE):
        valid = (i * CHUNKS_PER_TILE + c - N_BACK + key_chunk) >= 0
        o = _heads_attend(q_ref[0, c * CHUNK:(c + 1) * CHUNK, :], kwin[c * CHUNK:c * CHUNK + KEY_SPAN],
                          vwin[c * CHUNK:c * CHUNK + KEY_SPAN], sink_ref, valid)
        o_ref[0, c * CHUNK:(c + 1) * CHUNK, :] = _rms(o, g_ref[...]).astype(bf16)


def _attn_prompt(q, k, v, sinks, g):
    b, l, _ = q.shape
    back = N_BACK * CHUNK
    per = ATTN_TILE // back
    prev = pl.BlockSpec((1, back, KV_WIDTH), lambda bi, i: (bi, jnp.maximum(i * per - 1, 0), 0))
    cur = pl.BlockSpec((1, ATTN_TILE, KV_WIDTH), lambda bi, i: (bi, i, 0))
    return pl.pallas_call(
        _attn_prompt_body,
        grid=(b, l // ATTN_TILE),
        in_specs=[pl.BlockSpec(memory_space=pltpu.SMEM),
                  pl.BlockSpec((1, ATTN_TILE, ATTN_WIDTH), lambda bi, i: (bi, i, 0)),
                  prev, cur, prev, cur,
                  pl.BlockSpec((1, ATTN_WIDTH), lambda bi, i: (0, 0))],
        out_specs=pl.BlockSpec((1, ATTN_TILE, ATTN_WIDTH), lambda bi, i: (bi, i, 0)),
        out_shape=jax.ShapeDtypeStruct((b, l, ATTN_WIDTH), bf16),
        compiler_params=_cparams(2),
        name="attn_prompt",
    )(sinks, q, k, k, v, v, g)


def _attn_sample_body(sink_ref, q_ref, kn_ref, vn_ref, ck_ref, cv_ref, g_ref, o_ref, nk_ref, nv_ref):
    kall = jnp.concatenate([ck_ref[0], kn_ref[0]], axis=0)
    vall = jnp.concatenate([cv_ref[0], vn_ref[0]], axis=0)
    o = _heads_attend(q_ref[0], kall, vall, sink_ref, None)
    o_ref[0] = _rms(o, g_ref[...]).astype(bf16)
    n_new = kn_ref.shape[1]
    nk_ref[0] = kall[n_new:]
    nv_ref[0] = vall[n_new:]


def _attn_sample(q, k_new, v_new, cache_k, cache_v, sinks, g):
    b, l, _ = q.shape
    wc = cache_k.shape[1]
    blk = lambda r, n: pl.BlockSpec((1, r, n), lambda bi: (bi, 0, 0))
    return pl.pallas_call(
        _attn_sample_body,
        grid=(b,),
        in_specs=[pl.BlockSpec(memory_space=pltpu.SMEM), blk(l, ATTN_WIDTH), blk(l, KV_WIDTH), blk(l, KV_WIDTH),
                  blk(wc, KV_WIDTH), blk(wc, KV_WIDTH), pl.BlockSpec((1, ATTN_WIDTH), lambda bi: (0, 0))],
        out_specs=[blk(l, ATTN_WIDTH), blk(wc, KV_WIDTH), blk(wc, KV_WIDTH)],
        out_shape=[jax.ShapeDtypeStruct((b, l, ATTN_WIDTH), bf16),
                   jax.ShapeDtypeStruct((b, wc, KV_WIDTH), f32), jax.ShapeDtypeStruct((b, wc, KV_WIDTH), f32)],
        compiler_params=_cparams(),
        name="attn_sample",
    )(sinks, q, k_new, v_new, cache_k, cache_v, g)


def _ssm_tables(a_re, a_im, log_dt, b_re, b_im):
    dt = jnp.exp(log_dt)[:, None]
    lam_r, lam_i = a_re * dt, a_im * dt

    def power(n):
        mag = jnp.exp(n * lam_r)
        return jnp.stack([mag * jnp.cos(n * lam_i), mag * jnp.sin(n * lam_i)])

    ar, ai = power(1.0)
    den = a_re * a_re + a_im * a_im
    nr, ni = ar - 1.0, ai
    fr = ((nr * a_re + ni * a_im) / den)[..., None]
    fi = ((ni * a_re - nr * a_im) / den)[..., None]
    bbar = jnp.stack([fr * b_re - fi * b_im, fr * b_im + fi * b_re])
    bbar = bbar.transpose(0, 1, 3, 2).reshape(2, SSM_BLOCKS, LANES, SSM_STATE)
    by_block = lambda t: t.reshape(t.shape[:-2] + (SSM_BLOCKS, STATE_COLS))
    steps = by_block(power(jnp.arange(SSM_STEPS + 1, dtype=f32)[:, None, None]))
    levels = by_block(power(SSM_STEPS * 2.0 ** jnp.arange(8, dtype=f32)[:, None, None]))
    return bbar, steps.transpose(2, 0, 1, 3), levels.transpose(2, 1, 0, 3)


def _ssm_chunk_rows(u_ref, nk):
    xs = [u_ref[0, pl.ds(s, nk, stride=SSM_STEPS), :] for s in range(SSM_STEPS)]
    pairs = [jnp.concatenate([xs[2 * p], xs[2 * p + 1]], axis=1).astype(bf16) for p in range(SSM_STEPS // 2)]
    return xs, pairs


def _ssm_intra(pairs, toep_ref, nk):
    nd = len(pairs)
    y = [None] * nd
    for d in range(nd):
        lhs = jnp.concatenate(pairs[:nd - d], axis=0) if nd - d > 1 else pairs[0]
        r = jnp.dot(lhs, toep_ref[d], preferred_element_type=f32)
        for p in range(nd - d):
            blk = r[p * nk:(p + 1) * nk]
            y[p + d] = blk if y[p + d] is None else y[p + d] + blk
    return y


def _shift_rows(x, sh):
    rows = lax.broadcasted_iota(i32, (x.shape[0], 1), 0)
    return jnp.where(rows >= sh, pltpu.roll(x, sh, axis=0), 0.0)


def _group_index(shape, axis, sub):
    idx = lax.broadcasted_iota(i32, shape, axis)
    return jnp.right_shift(idx, int(math.log2(sub))) & (GROUPS_PER_BLOCK - 1)


def _build_ssm_weights(bb_ref, cr_ref, ci_ref, pw_ref, toep_s, wout_s, win_s, w_low):
    hp = lax.Precision.HIGHEST
    k = lax.broadcasted_iota(i32, (SSM_STATE, STATE_COLS), 0)
    n = lax.broadcasted_iota(i32, (SSM_STATE, STATE_COLS), 1)
    spread = ((n & (SSM_STATE - 1)) == k).astype(f32)
    own = _group_index((LANES, STATE_COLS), 0, SSM_GROUP) == _group_index((LANES, STATE_COLS), 1, SSM_STATE)

    def block_diag(compact):
        return jnp.where(own, jnp.dot(compact, spread, precision=hp, preferred_element_type=f32), 0.0)

    bb_r, bb_i = block_diag(bb_ref[0, 0]), block_diag(bb_ref[1, 0])
    c_r, c_i = block_diag(cr_ref[0]), block_diag(ci_ref[0])
    c_stack = jnp.concatenate([c_r.T, -c_i.T], axis=0)
    power = lambda e: (pw_ref[0, 0, e:e + 1, :], pw_ref[0, 1, e:e + 1, :])
    for s in range(SSM_STEPS):
        rows = slice(s * LANES, (s + 1) * LANES)
        pr, pi = power(SSM_STEPS - 1 - s)
        w = jnp.concatenate([bb_r * pr - bb_i * pi, bb_r * pi + bb_i * pr], axis=1)
        wout_s[rows, :] = w.astype(bf16)
        w_low[rows, :] = (w - w.astype(bf16).astype(f32)).astype(bf16)
        pr, pi = power(s + 1)
        g = jnp.concatenate([c_r * pr - c_i * pi, -(c_r * pi + c_i * pr)], axis=1)
        win_s[:, rows] = g.T.astype(bf16)
    c_high = c_stack.astype(bf16)
    c_low = (c_stack - c_high.astype(f32)).astype(bf16)
    lags = (jnp.dot(wout_s[...], c_high, preferred_element_type=f32)
            + (jnp.dot(wout_s[...], c_low, preferred_element_type=f32)
               + jnp.dot(w_low[...], c_high, preferred_element_type=f32)))
    lag_kernel = [lags[(SSM_STEPS - 1 - lag) * LANES:(SSM_STEPS - lag) * LANES] for lag in range(SSM_STEPS)]
    zero = jnp.zeros((LANES, LANES), f32)
    for d in range(SSM_STEPS // 2):
        top = jnp.concatenate([lag_kernel[2 * d], lag_kernel[2 * d + 1]], axis=1)
        bottom = jnp.concatenate([lag_kernel[2 * d - 1] if d > 0 else zero, lag_kernel[2 * d]], axis=1)
        toep_s[d] = jnp.concatenate([top, bottom], axis=0).astype(bf16)


def _ssm_chunks(u_ref, d_ref, y_ref, toep_s, wout_s, win_s, entry_state):
    nk = u_ref.shape[1] // SSM_STEPS
    xs, pairs = _ssm_chunk_rows(u_ref, nk)
    y = _ssm_intra(pairs, toep_s, nk)
    s = jnp.dot(jnp.concatenate(pairs, axis=1), wout_s[...], preferred_element_type=f32)
    hprev, hr, hi = entry_state(s[:, :STATE_COLS], s[:, STATE_COLS:])
    y2 = _bdot(hprev, win_s[...])
    for st in range(SSM_STEPS):
        piece = (y[st // 2][:, (st % 2) * LANES:(st % 2 + 1) * LANES] + y2[:, st * LANES:(st + 1) * LANES]
                 + d_ref[...] * xs[st])
        y_ref[0, pl.ds(st, nk, stride=SSM_STEPS), :] = piece
    return hr, hi


def _ssm_body(u_ref, us_ref, h0r_ref, h0i_ref, bb_ref, cr_ref, ci_ref, pw_ref, lev_ref, d_ref,
              y_ref, hr_ref, hi_ref, ys_ref, hrs_ref, his_ref, toep_s, wout_s, win_s, w_low):
    @pl.when(pl.program_id(1) == 0)
    def _():
        _build_ssm_weights(bb_ref, cr_ref, ci_ref, pw_ref, toep_s, wout_s, win_s, w_low)

        def one_chunk(sr, si):
            h0r, h0i = h0r_ref[0], h0i_ref[0]
            ar, ai = lev_ref[0, 0, 0:1, :], lev_ref[0, 0, 1:2, :]
            return (jnp.concatenate([h0r, h0i], axis=1), sr + ar * h0r - ai * h0i, si + ar * h0i + ai * h0r)
        hrs_ref[0], his_ref[0] = _ssm_chunks(us_ref, d_ref, ys_ref, toep_s, wout_s, win_s, one_chunk)

    def scan_chunks(sr, si):
        nk = sr.shape[0]
        level = 0
        while (1 << level) < nk:
            ar, ai = lev_ref[0, level, 0:1, :], lev_ref[0, level, 1:2, :]
            tr, ti = _shift_rows(sr, 1 << level), _shift_rows(si, 1 << level)
            sr, si = sr + ar * tr - ai * ti, si + ar * ti + ai * tr
            level += 1
        return (jnp.concatenate([_shift_rows(sr, 1), _shift_rows(si, 1)], axis=1), sr[nk - 1:nk], si[nk - 1:nk])
    for b in range(u_ref.shape[0]):
        hr_ref[b, 0], hi_ref[b, 0] = _ssm_chunks(u_ref.at[pl.ds(b, 1)], d_ref, y_ref.at[pl.ds(b, 1)], toep_s, wout_s,
                                                 win_s, scan_chunks)


def _ssm(u, us, h0r, h0i, bbar, c_re, c_im, steps, lev, d):
    b, l, _ = u.shape
    rows = us.shape[1]
    nb = rows // SSM_STEPS
    wspec = lambda a: pl.BlockSpec((1,) + a.shape[1:], lambda j, bi: (j,) + (0,) * (a.ndim - 1))
    assert b % SSM_SEQS == 0
    st = pl.BlockSpec((SSM_SEQS, 1, 1, STATE_COLS), lambda j, bi: (bi, j, 0, 0))
    sst = pl.BlockSpec((1, nb, STATE_COLS), lambda j, bi: (j, 0, 0))
    seq = pl.BlockSpec((SSM_SEQS, l, LANES), lambda j, bi: (bi, 0, j))
    sseq = pl.BlockSpec((1, rows, LANES), lambda j, bi: (0, 0, j))
    return pl.pallas_call(
        _ssm_body,
        grid=(SSM_BLOCKS, b // SSM_SEQS),
        in_specs=[seq, sseq, sst, sst, pl.BlockSpec((2, 1, LANES, SSM_STATE), lambda j, bi: (0, j, 0, 0)),
                  wspec(c_re), wspec(c_im), wspec(steps), wspec(lev), pl.BlockSpec((1, LANES), lambda j, bi: (0, j))],
        out_specs=[seq, st, st, sseq, sst, sst],
        out_shape=[jax.ShapeDtypeStruct((b, l, SSM_WIDTH), f32),
                   jax.ShapeDtypeStruct((b, SSM_BLOCKS, 1, STATE_COLS), f32),
                   jax.ShapeDtypeStruct((b, SSM_BLOCKS, 1, STATE_COLS), f32),
                   jax.ShapeDtypeStruct((1, rows, SSM_WIDTH), f32),
                   jax.ShapeDtypeStruct((SSM_BLOCKS, nb, STATE_COLS), f32),
                   jax.ShapeDtypeStruct((SSM_BLOCKS, nb, STATE_COLS), f32)],
        scratch_shapes=[pltpu.VMEM((SSM_STEPS // 2, 2 * LANES, 2 * LANES), bf16),
                        pltpu.VMEM((SSM_STEPS * LANES, 2 * STATE_COLS), bf16),
                        pltpu.VMEM((2 * STATE_COLS, SSM_STEPS * LANES), bf16),
                        pltpu.VMEM((SSM_STEPS * LANES, 2 * STATE_COLS), bf16)],
        compiler_params=_cparams(2),
        name="ssm",
    )(u, us, h0r, h0i, bbar, c_re, c_im, steps, lev, d)


ROUTER_COLS = LANES


def _mix_body(ap_ref, as_ref, yp_ref, ys_ref, xp_ref, xs_ref, wglu_ref, bglu_ref, gs_ref, wout_ref, gf_ref, wr_ref,
              br_ref, tri_ref, upper_ref, x1_ref, xl_ref, wts_ref, lpos_ref, n_ref, loff_ref, *, prompt_steps):
    is_prompt = pl.program_id(0) < prompt_steps
    pick_rows = lambda p_ref, s_ref: jnp.where(is_prompt, p_ref[...], jnp.concatenate([s_ref[...]] * MIX_TILES, axis=0))
    y = pick_rows(yp_ref, ys_ref)
    y = 0.5 * y * (1.0 + jnp.tanh(math.sqrt(2.0 / math.pi) * (y + 0.044715 * (y * y * y))))
    y = y * jax.nn.sigmoid(_bdot(y, wglu_ref[...]) + bglu_ref[...])
    attn = jnp.where(is_prompt, ap_ref[...].astype(f32),
                     jnp.concatenate([as_ref[...].astype(f32)] * MIX_TILES, axis=0)).astype(bf16)
    cat = jnp.concatenate([attn, _rms(y, gs_ref[...]).astype(bf16)], axis=1)
    x1 = pick_rows(xp_ref, xs_ref) + jnp.dot(cat, wout_ref[...], preferred_element_type=f32)
    x1_ref[...] = x1
    hf = _rms(x1, gf_ref[...])

    logits = _bdot(hf, wr_ref[...]) + br_ref[...]
    le = logits[:, :N_EXPERTS]
    lg = logits[:, N_EXPERTS:N_EXPERTS + N_EXPERT_GROUPS]
    tm = le.shape[0]
    gmax = jnp.max(lg, axis=-1, keepdims=True)
    gi = lax.broadcasted_iota(i32, (tm, N_EXPERT_GROUPS), 1).astype(f32)
    gsel = jnp.min(jnp.where(lg == gmax, gi, float(N_EXPERT_GROUPS)), axis=-1, keepdims=True)
    pg = 1.0 / jnp.sum(jnp.exp(lg - gmax), axis=-1, keepdims=True)
    ei_int = lax.broadcasted_iota(i32, (tm, N_EXPERTS), 1)
    ei = ei_int.astype(f32)
    egroup = jnp.right_shift(ei_int, int(math.log2(EXPERTS_PER_GROUP))).astype(f32)
    lm = jnp.where(egroup == gsel, le, NEG)
    v1 = jnp.max(lm, axis=-1, keepdims=True)
    i1 = jnp.min(jnp.where(lm == v1, ei, float(N_EXPERTS)), axis=-1, keepdims=True)
    lm2 = jnp.where(ei == i1, NEG, lm)
    v2 = jnp.max(lm2, axis=-1, keepdims=True)
    i2 = jnp.min(jnp.where(lm2 == v2, ei, float(N_EXPERTS)), axis=-1, keepdims=True)
    ex = jnp.exp(v2 - v1)
    wts_ref[...] = jnp.concatenate([pg / (1.0 + ex), pg * ex / (1.0 + ex)], axis=1)

    oh1 = (ei == i1).astype(f32)
    oh2 = (ei == i2).astype(f32)
    hf_bf = hf.astype(bf16)
    sorted_row = lax.broadcasted_iota(i32, (ROW_TILE, SLOTS * ROW_TILE), 1).astype(f32)
    for h in range(MIX_TILES):
        rows = slice(h * ROW_TILE, (h + 1) * ROW_TILE)
        both = (oh1[rows] + oh2[rows]).astype(bf16)
        before = jnp.dot(tri_ref[...], both, preferred_element_type=f32)
        count = jnp.sum(oh1[rows] + oh2[rows], axis=0, keepdims=True)
        lower = jnp.sum(jnp.dot(both, upper_ref[...], preferred_element_type=f32), axis=0, keepdims=True)
        lp1 = jnp.sum(oh1[rows] * (before + lower), axis=-1, keepdims=True)
        lp2 = jnp.sum(oh2[rows] * (before + lower), axis=-1, keepdims=True)
        lpos_ref[rows, :] = jnp.concatenate([lp1, lp2], axis=1)
        n_ref[h] = count
        loff_ref[h] = lower
        pick = ((sorted_row == lp1) | (sorted_row == lp2)).astype(bf16)
        xl = lax.dot_general(pick, hf_bf[rows], (((0,), (0,)), ((), ())), preferred_element_type=f32)
        _store_row_major(xl_ref, h * SLOTS * ROW_TILE, xl)


def _mix(a_p, a_s, ys_p, ys_s, x_p, x_s, wglu, bglu, gs, wout, gf, wr, br, tri, upper):
    tm = MIX_TILES * ROW_TILE
    assert x_p.shape[0] % tm == 0 and x_s.shape[0] == ROW_TILE
    prompt_steps = x_p.shape[0] // tm
    t = x_p.shape[0] + x_s.shape[0]
    p_rows = lambda n: pl.BlockSpec((tm, n), lambda i: (jnp.minimum(i, prompt_steps - 1), 0))
    s_rows = lambda n: pl.BlockSpec((ROW_TILE, n), lambda i: (0, 0))
    row = lambda n: pl.BlockSpec((tm, n), lambda i: (i, 0))
    full = lambda arr: pl.BlockSpec(arr.shape, lambda i: (0,) * arr.ndim)
    per_tile = pl.BlockSpec((MIX_TILES, 1, N_EXPERTS), lambda i: (i, 0, 0))
    return pl.pallas_call(
        functools.partial(_mix_body, prompt_steps=prompt_steps),
        grid=(prompt_steps + 1,),
        in_specs=[p_rows(ATTN_WIDTH), s_rows(ATTN_WIDTH), p_rows(SSM_WIDTH), s_rows(SSM_WIDTH), p_rows(D_MODEL),
                  s_rows(D_MODEL), full(wglu), full(bglu), full(gs), full(wout), full(gf), full(wr), full(br),
                  full(tri), full(upper)],
        out_specs=[row(D_MODEL), pl.BlockSpec((SLOTS * tm * PIECES, LANES), lambda i: (i, 0)), row(2), row(2),
                   per_tile, per_tile],
        out_shape=[jax.ShapeDtypeStruct((t, D_MODEL), f32),
                   jax.ShapeDtypeStruct((SLOTS * t * PIECES, LANES), f32),
                   jax.ShapeDtypeStruct((t, 2), f32), jax.ShapeDtypeStruct((t, 2), f32),
                   jax.ShapeDtypeStruct((t // ROW_TILE, 1, N_EXPERTS), f32),
                   jax.ShapeDtypeStruct((t // ROW_TILE, 1, N_EXPERTS), f32)],
        compiler_params=_cparams(),
        name="mix",
    )(a_p, a_s, ys_p, ys_s, x_p, x_s, wglu, bglu, gs, wout, gf, wr, br, tri, upper)


def _store_row_major(ref, first_row, x):
    for c in range(PIECES):
        ref[pl.ds(first_row * PIECES + c, x.shape[0], stride=PIECES), :] = x[:, c * LANES:(c + 1) * LANES]


def _load_row_major(ref, n_rows):
    return jnp.concatenate([ref[pl.ds(c, n_rows, stride=PIECES), :] for c in range(PIECES)], axis=1)


def _copy_rows(src, s_row, dst, d_row, n_rows, sem):
    return pltpu.make_async_copy(src.at[pl.ds(pl.multiple_of(s_row * PIECES, PIECES), n_rows * PIECES), :],
                                 dst.at[pl.ds(pl.multiple_of(d_row * PIECES, PIECES), n_rows * PIECES), :], sem)


def _for_each_piece(n, fn, pieces=RUN_PIECES):
    off = 0 if pieces[0] == RUN_PIECES[0] else n & ~(2 * pieces[0] - 1)
    for piece in pieces:
        @pl.when((n & piece) != 0)
        def _(off=off, piece=piece):
            fn(off, piece)
        off = off + (n & piece)


def _experts_body(te_ref, tpos_ref, tvalid_ref, tlo_ref, thi_ref, wslot_ref, wnext_ref, n_ref, cum_ref, loff_ref,
                  xl_hbm, wg_hbm, wu_hbm, wd_hbm, o_ref, wg_s, wu_s, wd_s, wg_f, wu_f, wd_f, wsem, sem, *xbufs):
    i = pl.program_id(0)
    last = pl.num_programs(0) - 1
    token_tiles = n_ref.shape[0] // N_EXPERTS
    ring = len(xbufs)

    def start_run(t, tau, enabled, buf, buf_sem):
        e, lo = te_ref[t], tpos_ref[t]
        k = tau * N_EXPERTS + e
        s, n = cum_ref[k], n_ref[k]
        a = jnp.maximum(s, lo)
        length = jnp.where(enabled, jnp.maximum(jnp.minimum(s + n, lo + ROW_TILE) - a, 0), 0)
        local = loff_ref[k] + (a - s)
        _for_each_piece(length, lambda off, piece: _copy_rows(
            xl_hbm, tau * (SLOTS * ROW_TILE) + local + off, buf, a - lo + off, piece, buf_sem).start())

    def start_runs_loop(t, first, stop, buf, buf_sem):
        def run(tau, c):
            start_run(t, tau, True, buf, buf_sem)
            return c
        lax.fori_loop(first, stop, run, 0)

    def clear(buf):
        buf[...] = jnp.zeros_like(buf)

    @pl.when(i == 0)
    def _():
        for buf in xbufs:
            clear(buf)
        for t in range(ring - 1):
            @pl.when(tvalid_ref[t] > 0)
            def _(t=t):
                start_runs_loop(t, tlo_ref[t], thi_ref[t] + 1, xbufs[t], sem.at[t])

    valid = tvalid_ref[i]

    def weight_copies(e, slot):
        return [pltpu.make_async_copy(w_hbm.at[e], w_f.at[slot], wsem.at[slot])
                for w_hbm, w_f in ((wg_hbm, wg_f), (wu_hbm, wu_f), (wd_hbm, wd_f))]

    @pl.when((i == 0) & (valid > 0))
    def _():
        for cp in weight_copies(te_ref[0], wslot_ref[0]):
            cp.start()

    @pl.when((valid > 0) & ((i == 0) | (te_ref[i] != te_ref[jnp.maximum(i - 1, 0)])))
    def _():
        slot = wslot_ref[i]
        for cp in weight_copies(te_ref[i], slot):
            cp.wait()
        wg_s[...] = wg_f[slot].astype(bf16)
        wu_s[...] = wu_f[slot].astype(bf16)
        wd_s[...] = wd_f[slot].astype(bf16)

        @pl.when(wnext_ref[i] >= 0)
        def _():
            for cp in weight_copies(wnext_ref[i], 1 - slot):
                cp.start()

    def tile_step(cur):
        ahead = (cur + ring - 1) % ring
        buf, buf_sem, next_buf, next_sem = xbufs[cur], sem.at[cur], xbufs[ahead], sem.at[ahead]
        _for_each_piece(valid, lambda off, piece: _copy_rows(
            xl_hbm, 0, buf, 0, piece, buf_sem).wait())

        nxt = jnp.minimum(i + ring - 1, last)
        go = (i + ring - 1 <= last) & (tvalid_ref[nxt] > 0)
        first, final = tlo_ref[nxt], thi_ref[nxt]
        for j in range(UNROLLED_RUNS):
            start_run(nxt, jnp.minimum(first + j, token_tiles - 1), go & (first + j <= final), next_buf, next_sem)

        x = _load_row_major(buf, ROW_TILE).astype(bf16)
        clear(buf)
        hg = jnp.dot(x, wg_s[...], preferred_element_type=f32)
        hu = jnp.dot(x, wu_s[...], preferred_element_type=f32)
        y = jnp.dot((hg * jax.nn.sigmoid(hg) * hu).astype(bf16), wd_s[...], preferred_element_type=f32)
        _store_row_major(o_ref, 0, y)

        @pl.when(go & (final - first >= UNROLLED_RUNS))
        def _():
            start_runs_loop(nxt, first + UNROLLED_RUNS, final + 1, next_buf, next_sem)

    for cur in range(ring):
        pl.when((valid > 0) & (i % ring == cur))(functools.partial(tile_step, cur))

    @pl.when(valid == 0)
    def _():
        o_ref[...] = jnp.zeros_like(o_ref)


def _experts(tables, xl, wg, wu, wd, tiles):
    hbm = pl.BlockSpec(memory_space=pl.ANY)
    return pl.pallas_call(
        _experts_body,
        grid_spec=pltpu.PrefetchScalarGridSpec(
            num_scalar_prefetch=len(tables),
            grid=(tiles,),
            in_specs=[hbm, hbm, hbm, hbm],
            out_specs=pl.BlockSpec((ROW_TILE * PIECES, LANES), lambda i, *_: (i, 0)),
            scratch_shapes=[pltpu.VMEM(w.shape[1:], bf16) for w in (wg, wu, wd)]
            + [pltpu.VMEM((2,) + w.shape[1:], f32) for w in (wg, wu, wd)]
            + [pltpu.SemaphoreType.DMA((2,)), pltpu.SemaphoreType.DMA((EXPERT_RING,))]
            + [pltpu.VMEM((ROW_TILE * PIECES, LANES), f32)] * EXPERT_RING),
        out_shape=jax.ShapeDtypeStruct((tiles * ROW_TILE * PIECES, LANES), f32),
        compiler_params=_cparams(),
        name="moe_experts",
    )(*tables, xl, wg, wu, wd)


def _combine_body(n_ref, gpos_ref, loff_ref, ys_hbm, x_ref, w_ref, l_ref, g_ref, op_ref, os_ref, sem, *ybufs,
                  prompt_tiles):
    i = pl.program_id(0)
    last = pl.num_programs(0) - 1
    tile_rows = SLOTS * ROW_TILE
    ring = len(ybufs)

    def start_run(t, e, enabled, buf, buf_sem, pieces=RUN_PIECES):
        k = t * N_EXPERTS + e
        _for_each_piece(jnp.where(enabled, n_ref[k], 0), lambda off, piece: _copy_rows(
            ys_hbm, gpos_ref[k] + off, buf, loff_ref[k] + off, piece, buf_sem).start(), pieces)

    @pl.when(i == 0)
    def _():
        for t in range(ring - 1):
            def run(e, c, t=t):
                start_run(t, e, True, ybufs[t], sem.at[t])
                return c
            lax.fori_loop(0, N_EXPERTS, run, 0)

    def step(cur):
        ahead = (cur + ring - 1) % ring
        buf, buf_sem, next_buf, next_sem = ybufs[cur], sem.at[cur], ybufs[ahead], sem.at[ahead]
        _copy_rows(ys_hbm, 0, buf, 0, tile_rows, buf_sem).wait()
        nxt, go = jnp.minimum(i + ring - 1, last), i + ring - 1 <= last
        longest = 0
        for e in range(N_EXPERTS):
            start_run(nxt, e, go, next_buf, next_sem, SHORT_PIECES)
            longest = longest | n_ref[nxt * N_EXPERTS + e]
        yl = _load_row_major(buf, tile_rows).astype(bf16)
        sorted_row = lax.broadcasted_iota(i32, (ROW_TILE, tile_rows), 1).astype(f32)
        w, lp = w_ref[...], l_ref[...]
        y1 = jnp.dot((sorted_row == lp[:, 0:1]).astype(bf16), yl, preferred_element_type=f32)
        y2 = jnp.dot((sorted_row == lp[:, 1:2]).astype(bf16), yl, preferred_element_type=f32)
        out = _rms(x_ref[...] + (w[:, 0:1] * y1 + w[:, 1:2] * y2), g_ref[...])

        @pl.when(i < prompt_tiles)
        def _():
            op_ref[...] = out

        @pl.when(i >= prompt_tiles)
        def _():
            os_ref[...] = out

        @pl.when(go & (longest >= 2 * SHORT_PIECES[0]))
        def _():
            def long_pieces(e, c):
                start_run(nxt, e, True, next_buf, next_sem, LONG_PIECES)
                return c
            lax.fori_loop(0, N_EXPERTS, long_pieces, 0)

    for cur in range(ring):
        pl.when(i % ring == cur)(functools.partial(step, cur))


def _combine(tables, ys, x1, wts, lpos, g, prompt_rows):
    prompt_tiles = prompt_rows // ROW_TILE
    tiles = x1.shape[0] // ROW_TILE
    row = lambda n: pl.BlockSpec((ROW_TILE, n), lambda i, *_: (i, 0))
    p_rows = lambda n: pl.BlockSpec((ROW_TILE, n), lambda i, *_: (jnp.minimum(i, prompt_tiles - 1), 0))
    s_rows = lambda n: pl.BlockSpec((ROW_TILE, n), lambda i, *_: (jnp.maximum(i - prompt_tiles, 0), 0))
    return pl.pallas_call(
        functools.partial(_combine_body, prompt_tiles=prompt_tiles),
        grid_spec=pltpu.PrefetchScalarGridSpec(
            num_scalar_prefetch=len(tables),
            grid=(tiles,),
            in_specs=[pl.BlockSpec(memory_space=pl.ANY), row(D_MODEL), row(2), row(2),
                      pl.BlockSpec((1, D_MODEL), lambda i, *_: (0, 0))],
            out_specs=[p_rows(D_MODEL), s_rows(D_MODEL)],
            scratch_shapes=[pltpu.SemaphoreType.DMA((COMBINE_RING,))]
            + [pltpu.VMEM((SLOTS * ROW_TILE * PIECES, LANES), f32)] * COMBINE_RING),
        out_shape=[jax.ShapeDtypeStruct((prompt_rows, D_MODEL), f32),
                   jax.ShapeDtypeStruct((x1.shape[0] - prompt_rows, D_MODEL), f32)],
        compiler_params=_cparams(),
        name="moe_combine",
    )(*tables, ys, x1, wts, lpos, g)


def _moe_tables(n, loff, tiles):
    cum = jnp.cumsum(n, axis=0) - n
    counts = jnp.sum(n, axis=0)
    padded = (counts + ROW_TILE - 1) // ROW_TILE * ROW_TILE
    ends = jnp.cumsum(padded)
    starts = ends - padded
    first = jnp.arange(tiles, dtype=i32) * ROW_TILE
    expert = jnp.minimum(jnp.sum((first[:, None] >= ends[None, :]).astype(i32), axis=1), N_EXPERTS - 1)
    sel = expert[:, None] == jnp.arange(N_EXPERTS)[None, :]
    pick = lambda v: jnp.sum(jnp.where(sel, v[None, :], 0), axis=1)
    pos = first - pick(starts)
    valid = jnp.where(first < ends[-1], jnp.clip(pick(counts) - pos, 0, ROW_TILE), 0)
    cum_t, n_t = cum.T[expert], n.T[expert]
    touches = (cum_t + n_t > pos[:, None]) & (cum_t < (pos + ROW_TILE)[:, None]) & (n_t > 0)
    tau = jnp.arange(n.shape[0], dtype=i32)[None, :]
    lo = jnp.min(jnp.where(touches, tau, n.shape[0]), axis=1)
    hi = jnp.max(jnp.where(touches, tau, -1), axis=1)
    ids = jnp.arange(N_EXPERTS)
    busy = counts > 0
    ordinal = jnp.sum(busy[None, :] & (ids[None, :] < ids[:, None]), axis=1)
    following = jnp.min(jnp.where(busy[None, :] & (ids[None, :] > ids[:, None]), ids[None, :], N_EXPERTS), axis=1)
    following = jnp.where(following < N_EXPERTS, following, -1)
    as_i32 = lambda v: v.astype(i32)
    flat = lambda v: v.reshape(-1).astype(i32)
    expert_tables = (tuple(map(as_i32, (expert, pos, valid, lo, hi, pick(ordinal) % 2, pick(following))))
                     + (flat(n), flat(cum), flat(loff)))
    combine_tables = (flat(n), flat(starts[None, :] + cum), flat(loff))
    return expert_tables, combine_tables


def kernel(x_prompt, x_sample, cache_k, cache_v, state_ssm_re, state_ssm_im, g_norm_mix, w_in, attn_sinks, ssm_a_re,
           ssm_a_im, ssm_log_dt, ssm_b_re, ssm_b_im, ssm_c_re, ssm_c_im, ssm_d, w_glu, b_glu, g_attn_out, g_ssm_out,
           w_out, g_norm_ffn, w_router_group, b_router_group, w_router_expert, b_router_expert, w_exp_gate, w_exp_up,
           w_exp_down, g_final):
    bp, lp, _ = x_prompt.shape
    bs, ls, _ = x_sample.shape
    depth = w_in.shape[0]
    assert depth == 1 and ls == SSM_STEPS and lp % ATTN_TILE == 0 and (bs * ls) % ROW_TILE == 0
    tp, ts = bp * lp, bs * ls
    wc = cache_k.shape[2]
    row2 = lambda v: v.reshape(1, -1)

    xp = x_prompt.reshape(tp, D_MODEL)
    xs = x_sample.reshape(ts, D_MODEL)
    w_in_bf = w_in[0].astype(bf16)
    qp, kp, vp, up = _proj(xp, row2(g_norm_mix[0]), w_in_bf, 1024)
    qs, kq, vq, us = _proj(xs, row2(g_norm_mix[0]), w_in_bf, ts)

    sinks = attn_sinks[0]
    g_att = row2(g_attn_out[0])
    ap = _attn_prompt(qp.reshape(bp, lp, -1), kp.reshape(bp, lp, -1), vp.reshape(bp, lp, -1), sinks, g_att)
    a_s, k_roll, v_roll = _attn_sample(qs.reshape(bs, ls, -1), kq.reshape(bs, ls, -1), vq.reshape(bs, ls, -1),
                                       cache_k[0].reshape(bs, wc, KV_WIDTH), cache_v[0].reshape(bs, wc, KV_WIDTH),
                                       sinks, g_att)

    bbar, steps, lev = _ssm_tables(ssm_a_re[0], ssm_a_im[0], ssm_log_dt[0], ssm_b_re[0], ssm_b_im[0])
    c_blocks = lambda c: c.reshape(SSM_BLOCKS, LANES, SSM_STATE)
    d_row = row2(ssm_d[0])
    to_blocks = lambda h: h.reshape(bs, SSM_BLOCKS, STATE_COLS).transpose(1, 0, 2)
    from_blocks = lambda h: h.transpose(1, 0, 2).reshape(bs, SSM_GROUPS, SSM_STATE)
    yp, hrp, hip, ysm, hrs, his = _ssm(up.reshape(bp, lp, -1), us.reshape(1, ts, -1), to_blocks(state_ssm_re[0]),
                                       to_blocks(state_ssm_im[0]), bbar, c_blocks(ssm_c_re[0]),
                                       c_blocks(ssm_c_im[0]), steps, lev, d_row)

    wr = jnp.zeros((D_MODEL, ROUTER_COLS), f32)
    wr = wr.at[:, :N_EXPERTS].set(w_router_expert[0]).at[:, N_EXPERTS:N_EXPERTS + N_EXPERT_GROUPS].set(w_router_group[0])
    br = jnp.zeros((1, ROUTER_COLS), f32)
    br = br.at[0, :N_EXPERTS].set(b_router_expert[0]).at[0, N_EXPERTS:N_EXPERTS + N_EXPERT_GROUPS].set(b_router_group[0])
    tri = jnp.tril(jnp.ones((ROW_TILE, ROW_TILE), bf16), -1)
    upper = jnp.triu(jnp.ones((N_EXPERTS, N_EXPERTS), bf16), 1)
    mix_w = (w_glu[0].astype(bf16), row2(b_glu[0]), row2(g_ssm_out[0]), w_out[0].astype(bf16), row2(g_norm_ffn[0]),
             wr.astype(bf16), br, tri, upper)
    x1, xl, wts, lpos, n_rows, n_off = _mix(ap.reshape(tp, -1), a_s.reshape(ts, -1), yp.reshape(tp, -1),
                                            ysm.reshape(ts, -1), xp, xs, *mix_w)

    per_tile = lambda v: v.reshape(-1, N_EXPERTS).astype(i32)
    tiles = (SLOTS * (tp + ts)) // ROW_TILE + N_EXPERTS
    expert_tables, combine_tables = _moe_tables(per_tile(n_rows), per_tile(n_off), tiles)
    expert_out = _experts(expert_tables, xl, w_exp_gate[0], w_exp_up[0], w_exp_down[0], tiles)
    y_p, y_s = _combine(combine_tables, expert_out, x1, wts, lpos, row2(g_final), tp)

    kvshape = lambda a, b: a.reshape(1, b, -1, N_KV_HEADS, HEAD_DIM)
    block_state = lambda h: h.reshape(bp, SSM_GROUPS, SSM_STATE)[None]
    wcp = min(WINDOW, lp)
    return (y_p.reshape(bp, lp, D_MODEL), y_s.reshape(bs, ls, D_MODEL),
            kvshape(kp.reshape(bp, lp, -1)[:, lp - wcp:], bp), kvshape(vp.reshape(bp, lp, -1)[:, lp - wcp:], bp),
            block_state(hrp), block_state(hip),
            kvshape(k_roll, bs), kvshape(v_roll, bs),
            from_blocks(hrs)[None], from_blocks(his)[None])
```

```python
import functools
import math

import jax
import jax.numpy as jnp
from jax import lax
from jax.experimental import pallas as pl
from jax.experimental.pallas import tpu as pltpu

f32, bf16, i32 = jnp.float32, jnp.bfloat16, jnp.int32

D_MODEL = 1024
CHUNK = 64
N_BACK = 2
WINDOW = 128
ATTN_WIDTH = 512
HEAD_DIM = 64
N_KV_HEADS = 2
Q_PER_KV = 4
KV_WIDTH = 128
SSM_WIDTH = 512
SSM_GROUP = 16
SSM_GROUPS = 32
SSM_STATE = 64
PROJ_WIDTH = 1280
N_EXPERT_GROUPS = 4
EXPERTS_PER_GROUP = 8
N_EXPERTS = 32
D_EXPERT = 512
EPS = 1e-6
NEG = -1e30

LANES = 128
SSM_STEPS = 16
SSM_SEQS = 2
SSM_BLOCKS = SSM_WIDTH // LANES
GROUPS_PER_BLOCK = LANES // SSM_GROUP
STATE_COLS = GROUPS_PER_BLOCK * SSM_STATE
ROW_TILE = 256
SLOTS = 2
PIECES = D_MODEL // LANES
RUN_PIECES = tuple(1 << b for b in reversed(range(int(math.log2(ROW_TILE)) + 1)))
SHORT_PIECES = tuple(p for p in RUN_PIECES if p <= 32)
LONG_PIECES = tuple(p for p in RUN_PIECES if p > 32)
MIX_TILES = 2
UNROLLED_RUNS = 20
EXPERT_RING = 3
COMBINE_RING = 3
VMEM_LIMIT = 56 * 1024 * 1024


def _cparams(n_axes=1, limit=VMEM_LIMIT):
    return pltpu.CompilerParams(dimension_semantics=("arbitrary",) * n_axes, vmem_limit_bytes=limit)


def _rms(x, g):
    return x * lax.rsqrt(jnp.mean(x * x, axis=-1, keepdims=True) + EPS) * g


def _bdot(a, b):
    return jnp.dot(a.astype(bf16), b.astype(bf16), preferred_element_type=f32)


def _proj_body(x_ref, g_ref, w_ref, q_ref, k_ref, v_ref, u_ref):
    h = _rms(x_ref[...], g_ref[...])
    z = _bdot(h, w_ref[...])
    q_ref[...] = z[:, :ATTN_WIDTH] * (HEAD_DIM ** -0.5)
    k_ref[...] = z[:, ATTN_WIDTH:ATTN_WIDTH + KV_WIDTH]
    v_ref[...] = z[:, ATTN_WIDTH + KV_WIDTH:ATTN_WIDTH + 2 * KV_WIDTH]
    u_ref[...] = z[:, ATTN_WIDTH + 2 * KV_WIDTH:]


def _proj(x2d, g, w_bf, tm):
    t = x2d.shape[0]
    row = lambda n: pl.BlockSpec((tm, n), lambda i: (i, 0))
    full = lambda a: pl.BlockSpec(a.shape, lambda i: (0,) * a.ndim)
    return pl.pallas_call(
        _proj_body,
        grid=(t // tm,),
        in_specs=[row(D_MODEL), full(g), full(w_bf)],
        out_specs=[row(ATTN_WIDTH), row(KV_WIDTH), row(KV_WIDTH), row(SSM_WIDTH)],
        out_shape=[jax.ShapeDtypeStruct((t, n), f32) for n in (ATTN_WIDTH, KV_WIDTH, KV_WIDTH, SSM_WIDTH)],
        compiler_params=_cparams(),
        name="proj",
    )(x2d, g, w_bf)


def _sink_column(sink_ref, kv, rows_per_head):
    r = lax.broadcasted_iota(i32, (Q_PER_KV * rows_per_head, 1), 0)
    col = jnp.full((Q_PER_KV * rows_per_head, 1), sink_ref[kv * Q_PER_KV], f32)
    for j in range(1, Q_PER_KV):
        col = jnp.where(r >= j * rows_per_head, sink_ref[kv * Q_PER_KV + j], col)
    return col


def _attend(qs, kc, vc, sink_col, valid):
    s = lax.dot_general(qs.astype(bf16), kc.astype(bf16), (((1,), (1,)), ((), ())), preferred_element_type=f32)
    if valid is not None:
        s = jnp.where(valid, s, NEG)
    m = jnp.maximum(jnp.max(s, axis=-1, keepdims=True), sink_col)
    p = jnp.exp(s - m)
    denom = jnp.sum(p, axis=-1, keepdims=True) + jnp.exp(sink_col - m)
    return _bdot(p, vc) / denom


def _heads_attend(q, k, v, sink_ref, valid):
    rows = q.shape[0]
    pieces = []
    for kv in range(N_KV_HEADS):
        qs = jnp.concatenate(
            [q[:, (kv * Q_PER_KV + j) * HEAD_DIM:(kv * Q_PER_KV + j + 1) * HEAD_DIM] for j in range(Q_PER_KV)], axis=0)
        o = _attend(qs, k[:, kv * HEAD_DIM:(kv + 1) * HEAD_DIM], v[:, kv * HEAD_DIM:(kv + 1) * HEAD_DIM],
                    _sink_column(sink_ref, kv, rows), valid)
        pieces += [o[j * rows:(j + 1) * rows] for j in range(Q_PER_KV)]
    return jnp.concatenate(pieces, axis=1)


ATTN_TILE = 256
CHUNKS_PER_TILE = ATTN_TILE // CHUNK
KEY_SPAN = (N_BACK + 1) * CHUNK


def _attn_prompt_body(sink_ref, q_ref, kp_ref, kc_ref, vp_ref, vc_ref, g_ref, o_ref):
    i = pl.program_id(1)
    kwin = jnp.concatenate([kp_ref[0], kc_ref[0]], axis=0)
    vwin = jnp.concatenate([vp_ref[0], vc_ref[0]], axis=0)
    key_chunk = lax.broadcasted_iota(i32, (1, KEY_SPAN), 1) // CHUNK
    for c in range(CHUNKS_PER_TILE):
        valid = (i * CHUNKS_PER_TILE + c - N_BACK + key_chunk) >= 0
        o = _heads_attend(q_ref[0, c * CHUNK:(c + 1) * CHUNK, :], kwin[c * CHUNK:c * CHUNK + KEY_SPAN],
                          vwin[c * CHUNK:c * CHUNK + KEY_SPAN], sink_ref, valid)
        o_ref[0, c * CHUNK:(c + 1) * CHUNK, :] = _rms(o, g_ref[...]).astype(bf16)


def _attn_prompt(q, k, v, sinks, g):
    b, l, _ = q.shape
    back = N_BACK * CHUNK
    per = ATTN_TILE // back
    prev = pl.BlockSpec((1, back, KV_WIDTH), lambda bi, i: (bi, jnp.maximum(i * per - 1, 0), 0))
    cur = pl.BlockSpec((1, ATTN_TILE, KV_WIDTH), lambda bi, i: (bi, i, 0))
    return pl.pallas_call(
        _attn_prompt_body,
        grid=(b, l // ATTN_TILE),
        in_specs=[pl.BlockSpec(memory_space=pltpu.SMEM),
                  pl.BlockSpec((1, ATTN_TILE, ATTN_WIDTH), lambda bi, i: (bi, i, 0)),
                  prev, cur, prev, cur,
                  pl.BlockSpec((1, ATTN_WIDTH), lambda bi, i: (0, 0))],
        out_specs=pl.BlockSpec((1, ATTN_TILE, ATTN_WIDTH), lambda bi, i: (bi, i, 0)),
        out_shape=jax.ShapeDtypeStruct((b, l, ATTN_WIDTH), bf16),
        compiler_params=_cparams(2),
        name="attn_prompt",
    )(sinks, q, k, k, v, v, g)


def _attn_sample_body(sink_ref, q_ref, kn_ref, vn_ref, ck_ref, cv_ref, g_ref, o_ref, nk_ref, nv_ref):
    kall = jnp.concatenate([ck_ref[0], kn_ref[0]], axis=0)
    vall = jnp.concatenate([cv_ref[0], vn_ref[0]], axis=0)
    o = _heads_attend(q_ref[0], kall, vall, sink_ref, None)
    o_ref[0] = _rms(o, g_ref[...]).astype(bf16)
    n_new = kn_ref.shape[1]
    nk_ref[0] = kall[n_new:]
    nv_ref[0] = vall[n_new:]


def _attn_sample(q, k_new, v_new, cache_k, cache_v, sinks, g):
    b, l, _ = q.shape
    wc = cache_k.shape[1]
    blk = lambda r, n: pl.BlockSpec((1, r, n), lambda bi: (bi, 0, 0))
    return pl.pallas_call(
        _attn_sample_body,
        grid=(b,),
        in_specs=[pl.BlockSpec(memory_space=pltpu.SMEM), blk(l, ATTN_WIDTH), blk(l, KV_WIDTH), blk(l, KV_WIDTH),
                  blk(wc, KV_WIDTH), blk(wc, KV_WIDTH), pl.BlockSpec((1, ATTN_WIDTH), lambda bi: (0, 0))],
        out_specs=[blk(l, ATTN_WIDTH), blk(wc, KV_WIDTH), blk(wc, KV_WIDTH)],
        out_shape=[jax.ShapeDtypeStruct((b, l, ATTN_WIDTH), bf16),
                   jax.ShapeDtypeStruct((b, wc, KV_WIDTH), f32), jax.ShapeDtypeStruct((b, wc, KV_WIDTH), f32)],
        compiler_params=_cparams(),
        name="attn_sample",
    )(sinks, q, k_new, v_new, cache_k, cache_v, g)


def _ssm_tables(a_re, a_im, log_dt, b_re, b_im):
    dt = jnp.exp(log_dt)[:, None]
    lam_r, lam_i = a_re * dt, a_im * dt

    def power(n):
        mag = jnp.exp(n * lam_r)
        return jnp.stack([mag * jnp.cos(n * lam_i), mag * jnp.sin(n * lam_i)])

    ar, ai = power(1.0)
    den = a_re * a_re + a_im * a_im
    nr, ni = ar - 1.0, ai
    fr = ((nr * a_re + ni * a_im) / den)[..., None]
    fi = ((ni * a_re - nr * a_im) / den)[..., None]
    bbar = jnp.stack([fr * b_re - fi * b_im, fr * b_im + fi * b_re])
    bbar = bbar.transpose(0, 1, 3, 2).reshape(2, SSM_BLOCKS, LANES, SSM_STATE)
    by_block = lambda t: t.reshape(t.shape[:-2] + (SSM_BLOCKS, STATE_COLS))
    steps = by_block(power(jnp.arange(SSM_STEPS + 1, dtype=f32)[:, None, None]))
    levels = by_block(power(SSM_STEPS * 2.0 ** jnp.arange(8, dtype=f32)[:, None, None]))
    return bbar, steps.transpose(2, 0, 1, 3), levels.transpose(2, 1, 0, 3)


def _ssm_chunk_rows(u_ref, nk):
    xs = [u_ref[0, pl.ds(s, nk, stride=SSM_STEPS), :] for s in range(SSM_STEPS)]
    pairs = [jnp.concatenate([xs[2 * p], xs[2 * p + 1]], axis=1).astype(bf16) for p in range(SSM_STEPS // 2)]
    return xs, pairs


def _ssm_intra(pairs, toep_ref, nk):
    nd = len(pairs)
    y = [None] * nd
    for d in range(nd):
        lhs = jnp.concatenate(pairs[:nd - d], axis=0) if nd - d > 1 else pairs[0]
        r = jnp.dot(lhs, toep_ref[d], preferred_element_type=f32)
        for p in range(nd - d):
            blk = r[p * nk:(p + 1) * nk]
            y[p + d] = blk if y[p + d] is None else y[p + d] + blk
    return y


def _shift_rows(x, sh):
    rows = lax.broadcasted_iota(i32, (x.shape[0], 1), 0)
    return jnp.where(rows >= sh, pltpu.roll(x, sh, axis=0), 0.0)


def _group_index(shape, axis, sub):
    idx = lax.broadcasted_iota(i32, shape, axis)
    return jnp.right_shift(idx, int(math.log2(sub))) & (GROUPS_PER_BLOCK - 1)


def _build_ssm_weights(bb_ref, cr_ref, ci_ref, pw_ref, toep_s, wout_s, win_s, w_low):
    hp = lax.Precision.HIGHEST
    k = lax.broadcasted_iota(i32, (SSM_STATE, STATE_COLS), 0)
    n = lax.broadcasted_iota(i32, (SSM_STATE, STATE_COLS), 1)
    spread = ((n & (SSM_STATE - 1)) == k).astype(f32)
    own = _group_index((LANES, STATE_COLS), 0, SSM_GROUP) == _group_index((LANES, STATE_COLS), 1, SSM_STATE)

    def block_diag(compact):
        return jnp.where(own, jnp.dot(compact, spread, precision=hp, preferred_element_type=f32), 0.0)

    bb_r, bb_i = block_diag(bb_ref[0, 0]), block_diag(bb_ref[1, 0])
    c_r, c_i = block_diag(cr_ref[0]), block_diag(ci_ref[0])
    c_stack = jnp.concatenate([c_r.T, -c_i.T], axis=0)
    power = lambda e: (pw_ref[0, 0, e:e + 1, :], pw_ref[0, 1, e:e + 1, :])
    for s in range(SSM_STEPS):
        rows = slice(s * LANES, (s + 1) * LANES)
        pr, pi = power(SSM_STEPS - 1 - s)
        w = jnp.concatenate([bb_r * pr - bb_i * pi, bb_r * pi + bb_i * pr], axis=1)
        wout_s[rows, :] = w.astype(bf16)
        w_low[rows, :] = (w - w.astype(bf16).astype(f32)).astype(bf16)
        pr, pi = power(s + 1)
        g = jnp.concatenate([c_r * pr - c_i * pi, -(c_r * pi + c_i * pr)], axis=1)
        win_s[:, rows] = g.T.astype(bf16)
    c_high = c_stack.astype(bf16)
    c_low = (c_stack - c_high.astype(f32)).astype(bf16)
    lags = (jnp.dot(wout_s[...], c_high, preferred_element_type=f32)
            + (jnp.dot(wout_s[...], c_low, preferred_element_type=f32)
               + jnp.dot(w_low[...], c_high, preferred_element_type=f32)))
    lag_kernel = [lags[(SSM_STEPS - 1 - lag) * LANES:(SSM_STEPS - lag) * LANES] for lag in range(SSM_STEPS)]
    zero = jnp.zeros((LANES, LANES), f32)
    for d in range(SSM_STEPS // 2):
        top = jnp.concatenate([lag_kernel[2 * d], lag_kernel[2 * d + 1]], axis=1)
        bottom = jnp.concatenate([lag_kernel[2 * d - 1] if d > 0 else zero, lag_kernel[2 * d]], axis=1)
        toep_s[d] = jnp.concatenate([top, bottom], axis=0).astype(bf16)


def _ssm_chunks(u_ref, d_ref, y_ref, toep_s, wout_s, win_s, entry_state):
    nk = u_ref.shape[1] // SSM_STEPS
    xs, pairs = _ssm_chunk_rows(u_ref, nk)
    y = _ssm_intra(pairs, toep_s, nk)
    s = jnp.dot(jnp.concatenate(pairs, axis=1), wout_s[...], preferred_element_type=f32)
    hprev, hr, hi = entry_state(s[:, :STATE_COLS], s[:, STATE_COLS:])
    y2 = _bdot(hprev, win_s[...])
    for st in range(SSM_STEPS):
        piece = (y[st // 2][:, (st % 2) * LANES:(st % 2 + 1) * LANES] + y2[:, st * LANES:(st + 1) * LANES]
                 + d_ref[...] * xs[st])
        y_ref[0, pl.ds(st, nk, stride=SSM_STEPS), :] = piece
    return hr, hi


def _ssm_body(u_ref, us_ref, h0r_ref, h0i_ref, bb_ref, cr_ref, ci_ref, pw_ref, lev_ref, d_ref,
              y_ref, hr_ref, hi_ref, ys_ref, hrs_ref, his_ref, toep_s, wout_s, win_s, w_low):
    @pl.when(pl.program_id(1) == 0)
    def _():
        _build_ssm_weights(bb_ref, cr_ref, ci_ref, pw_ref, toep_s, wout_s, win_s, w_low)

        def one_chunk(sr, si):
            h0r, h0i = h0r_ref[0], h0i_ref[0]
            ar, ai = lev_ref[0, 0, 0:1, :], lev_ref[0, 0, 1:2, :]
            return (jnp.concatenate([h0r, h0i], axis=1), sr + ar * h0r - ai * h0i, si + ar * h0i + ai * h0r)
        hrs_ref[0], his_ref[0] = _ssm_chunks(us_ref, d_ref, ys_ref, toep_s, wout_s, win_s, one_chunk)

    def scan_chunks(sr, si):
        nk = sr.shape[0]
        level = 0
        while (1 << level) < nk:
            ar, ai = lev_ref[0, level, 0:1, :], lev_ref[0, level, 1:2, :]
            tr, ti = _shift_rows(sr, 1 << level), _shift_rows(si, 1 << level)
            sr, si = sr + ar * tr - ai * ti, si + ar * ti + ai * tr
            level += 1
        return (jnp.concatenate([_shift_rows(sr, 1), _shift_rows(si, 1)], axis=1), sr[nk - 1:nk], si[nk - 1:nk])
    for b in range(u_ref.shape[0]):
        hr_ref[b, 0], hi_ref[b, 0] = _ssm_chunks(u_ref.at[pl.ds(b, 1)], d_ref, y_ref.at[pl.ds(b, 1)], toep_s, wout_s,
                                                 win_s, scan_chunks)


def _ssm(u, us, h0r, h0i, bbar, c_re, c_im, steps, lev, d):
    b, l, _ = u.shape
    rows = us.shape[1]
    nb = rows // SSM_STEPS
    wspec = lambda a: pl.BlockSpec((1,) + a.shape[1:], lambda j, bi: (j,) + (0,) * (a.ndim - 1))
    assert b % SSM_SEQS == 0
    st = pl.BlockSpec((SSM_SEQS, 1, 1, STATE_COLS), lambda j, bi: (bi, j, 0, 0))
    sst = pl.BlockSpec((1, nb, STATE_COLS), lambda j, bi: (j, 0, 0))
    seq = pl.BlockSpec((SSM_SEQS, l, LANES), lambda j, bi: (bi, 0, j))
    sseq = pl.BlockSpec((1, rows, LANES), lambda j, bi: (0, 0, j))
    return pl.pallas_call(
        _ssm_body,
        grid=(SSM_BLOCKS, b // SSM_SEQS),
        in_specs=[seq, sseq, sst, sst, pl.BlockSpec((2, 1, LANES, SSM_STATE), lambda j, bi: (0, j, 0, 0)),
                  wspec(c_re), wspec(c_im), wspec(steps), wspec(lev), pl.BlockSpec((1, LANES), lambda j, bi: (0, j))],
        out_specs=[seq, st, st, sseq, sst, sst],
        out_shape=[jax.ShapeDtypeStruct((b, l, SSM_WIDTH), f32),
                   jax.ShapeDtypeStruct((b, SSM_BLOCKS, 1, STATE_COLS), f32),
                   jax.ShapeDtypeStruct((b, SSM_BLOCKS, 1, STATE_COLS), f32),
                   jax.ShapeDtypeStruct((1, rows, SSM_WIDTH), f32),
                   jax.ShapeDtypeStruct((SSM_BLOCKS, nb, STATE_COLS), f32),
                   jax.ShapeDtypeStruct((SSM_BLOCKS, nb, STATE_COLS), f32)],
        scratch_shapes=[pltpu.VMEM((SSM_STEPS // 2, 2 * LANES, 2 * LANES), bf16),
                        pltpu.VMEM((SSM_STEPS * LANES, 2 * STATE_COLS), bf16),
                        pltpu.VMEM((2 * STATE_COLS, SSM_STEPS * LANES), bf16),
                        pltpu.VMEM((SSM_STEPS * LANES, 2 * STATE_COLS), bf16)],
        compiler_params=_cparams(2),
        name="ssm",
    )(u, us, h0r, h0i, bbar, c_re, c_im, steps, lev, d)


ROUTER_COLS = LANES


def _mix_body(ap_ref, as_ref, yp_ref, ys_ref, xp_ref, xs_ref, wglu_ref, bglu_ref, gs_ref, wout_ref, gf_ref, wr_ref,
              br_ref, tri_ref, upper_ref, x1_ref, xl_ref, wts_ref, lpos_ref, n_ref, loff_ref, *, prompt_steps):
    is_prompt = pl.program_id(0) < prompt_steps
    pick_rows = lambda p_ref, s_ref: jnp.where(is_prompt, p_ref[...], jnp.concatenate([s_ref[...]] * MIX_TILES, axis=0))
    y = pick_rows(yp_ref, ys_ref)
    y = 0.5 * y * (1.0 + jnp.tanh(math.sqrt(2.0 / math.pi) * (y + 0.044715 * (y * y * y))))
    y = y * jax.nn.sigmoid(_bdot(y, wglu_ref[...]) + bglu_ref[...])
    attn = jnp.where(is_prompt, ap_ref[...].astype(f32),
                     jnp.concatenate([as_ref[...].astype(f32)] * MIX_TILES, axis=0)).astype(bf16)
    cat = jnp.concatenate([attn, _rms(y, gs_ref[...]).astype(bf16)], axis=1)
    x1 = pick_rows(xp_ref, xs_ref) + jnp.dot(cat, wout_ref[...], preferred_element_type=f32)
    x1_ref[...] = x1
    hf = _rms(x1, gf_ref[...])

    logits = _bdot(hf, wr_ref[...]) + br_ref[...]
    le = logits[:, :N_EXPERTS]
    lg = logits[:, N_EXPERTS:N_EXPERTS + N_EXPERT_GROUPS]
    tm = le.shape[0]
    gmax = jnp.max(lg, axis=-1, keepdims=True)
    gi = lax.broadcasted_iota(i32, (tm, N_EXPERT_GROUPS), 1).astype(f32)
    gsel = jnp.min(jnp.where(lg == gmax, gi, float(N_EXPERT_GROUPS)), axis=-1, keepdims=True)
    pg = 1.0 / jnp.sum(jnp.exp(lg - gmax), axis=-1, keepdims=True)
    ei_int = lax.broadcasted_iota(i32, (tm, N_EXPERTS), 1)
    ei = ei_int.astype(f32)
    egroup = jnp.right_shift(ei_int, int(math.log2(EXPERTS_PER_GROUP))).astype(f32)
    lm = jnp.where(egroup == gsel, le, NEG)
    v1 = jnp.max(lm, axis=-1, keepdims=True)
    i1 = jnp.min(jnp.where(lm == v1, ei, float(N_EXPERTS)), axis=-1, keepdims=True)
    lm2 = jnp.where(ei == i1, NEG, lm)
    v2 = jnp.max(lm2, axis=-1, keepdims=True)
    i2 = jnp.min(jnp.where(lm2 == v2, ei, float(N_EXPERTS)), axis=-1, keepdims=True)
    ex = jnp.exp(v2 - v1)
    wts_ref[...] = jnp.concatenate([pg / (1.0 + ex), pg * ex / (1.0 + ex)], axis=1)

    oh1 = (ei == i1).astype(f32)
    oh2 = (ei == i2).astype(f32)
    hf_bf = hf.astype(bf16)
    sorted_row = lax.broadcasted_iota(i32, (ROW_TILE, SLOTS * ROW_TILE), 1).astype(f32)
    for h in range(MIX_TILES):
        rows = slice(h * ROW_TILE, (h + 1) * ROW_TILE)
        both = (oh1[rows] + oh2[rows]).astype(bf16)
        before = jnp.dot(tri_ref[...], both, preferred_element_type=f32)
        count = jnp.sum(oh1[rows] + oh2[rows], axis=0, keepdims=True)
        lower = jnp.sum(jnp.dot(both, upper_ref[...], preferred_element_type=f32), axis=0, keepdims=True)
        lp1 = jnp.sum(oh1[rows] * (before + lower), axis=-1, keepdims=True)
        lp2 = jnp.sum(oh2[rows] * (before + lower), axis=-1, keepdims=True)
        lpos_ref[rows, :] = jnp.concatenate([lp1, lp2], axis=1)
        n_ref[h] = count
        loff_ref[h] = lower
        pick = ((sorted_row == lp1) | (sorted_row == lp2)).astype(bf16)
        xl = lax.dot_general(pick, hf_bf[rows], (((0,), (0,)), ((), ())), preferred_element_type=f32)
        _store_row_major(xl_ref, h * SLOTS * ROW_TILE, xl)


def _mix(a_p, a_s, ys_p, ys_s, x_p, x_s, wglu, bglu, gs, wout, gf, wr, br, tri, upper):
    tm = MIX_TILES * ROW_TILE
    assert x_p.shape[0] % tm == 0 and x_s.shape[0] == ROW_TILE
    prompt_steps = x_p.shape[0] // tm
    t = x_p.shape[0] + x_s.shape[0]
    p_rows = lambda n: pl.BlockSpec((tm, n), lambda i: (jnp.minimum(i, prompt_steps - 1), 0))
    s_rows = lambda n: pl.BlockSpec((ROW_TILE, n), lambda i: (0, 0))
    row = lambda n: pl.BlockSpec((tm, n), lambda i: (i, 0))
    full = lambda arr: pl.BlockSpec(arr.shape, lambda i: (0,) * arr.ndim)
    per_tile = pl.BlockSpec((MIX_TILES, 1, N_EXPERTS), lambda i: (i, 0, 0))
    return pl.pallas_call(
        functools.partial(_mix_body, prompt_steps=prompt_steps),
        grid=(prompt_steps + 1,),
        in_specs=[p_rows(ATTN_WIDTH), s_rows(ATTN_WIDTH), p_rows(SSM_WIDTH), s_rows(SSM_WIDTH), p_rows(D_MODEL),
                  s_rows(D_MODEL), full(wglu), full(bglu), full(gs), full(wout), full(gf), full(wr), full(br),
                  full(tri), full(upper)],
        out_specs=[row(D_MODEL), pl.BlockSpec((SLOTS * tm * PIECES, LANES), lambda i: (i, 0)), row(2), row(2),
                   per_tile, per_tile],
        out_shape=[jax.ShapeDtypeStruct((t, D_MODEL), f32),
                   jax.ShapeDtypeStruct((SLOTS * t * PIECES, LANES), f32),
                   jax.ShapeDtypeStruct((t, 2), f32), jax.ShapeDtypeStruct((t, 2), f32),
                   jax.ShapeDtypeStruct((t // ROW_TILE, 1, N_EXPERTS), f32),
                   jax.ShapeDtypeStruct((t // ROW_TILE, 1, N_EXPERTS), f32)],
        compiler_params=_cparams(),
        name="mix",
    )(a_p, a_s, ys_p, ys_s, x_p, x_s, wglu, bglu, gs, wout, gf, wr, br, tri, upper)


def _store_row_major(ref, first_row, x):
    for c in range(PIECES):
        ref[pl.ds(first_row * PIECES + c, x.shape[0], stride=PIECES), :] = x[:, c * LANES:(c + 1) * LANES]


def _load_row_major(ref, n_rows):
    return jnp.concatenate([ref[pl.ds(c, n_rows, stride=PIECES), :] for c in range(PIECES)], axis=1)


def _copy_rows(src, s_row, dst, d_row, n_rows, sem):
    return pltpu.make_async_copy(src.at[pl.ds(pl.multiple_of(s_row * PIECES, PIECES), n_rows * PIECES), :],
                                 dst.at[pl.ds(pl.multiple_of(d_row * PIECES, PIECES), n_rows * PIECES), :], sem)


def _for_each_piece(n, fn, pieces=RUN_PIECES):
    off = 0 if pieces[0] == RUN_PIECES[0] else n & ~(2 * pieces[0] - 1)
    for piece in pieces:
        @pl.when((n & piece) != 0)
        def _(off=off, piece=piece):
            fn(off, piece)
        off = off + (n & piece)


def _experts_body(te_ref, tpos_ref, tvalid_ref, tlo_ref, thi_ref, wslot_ref, wnext_ref, n_ref, cum_ref, loff_ref,
                  xl_hbm, wg_hbm, wu_hbm, wd_hbm, o_ref, wg_s, wu_s, wd_s, wg_f, wu_f, wd_f, wsem, sem, *xbufs):
    i = pl.program_id(0)
    last = pl.num_programs(0) - 1
    token_tiles = n_ref.shape[0] // N_EXPERTS
    ring = len(xbufs)

    def start_run(t, tau, enabled, buf, buf_sem):
        e, lo = te_ref[t], tpos_ref[t]
        k = tau * N_EXPERTS + e
        s, n = cum_ref[k], n_ref[k]
        a = jnp.maximum(s, lo)
        length = jnp.where(enabled, jnp.maximum(jnp.minimum(s + n, lo + ROW_TILE) - a, 0), 0)
        local = loff_ref[k] + (a - s)
        _for_each_piece(length, lambda off, piece: _copy_rows(
            xl_hbm, tau * (SLOTS * ROW_TILE) + local + off, buf, a - lo + off, piece, buf_sem).start())

    def start_runs_loop(t, first, stop, buf, buf_sem):
        def run(tau, c):
            start_run(t, tau, True, buf, buf_sem)
            return c
        lax.fori_loop(first, stop, run, 0)

    def clear(buf):
        buf[...] = jnp.zeros_like(buf)

    @pl.when(i == 0)
    def _():
        for buf in xbufs:
            clear(buf)
        for t in range(ring - 1):
            @pl.when(tvalid_ref[t] > 0)
            def _(t=t):
                start_runs_loop(t, tlo_ref[t], thi_ref[t] + 1, xbufs[t], sem.at[t])

    valid = tvalid_ref[i]

    def weight_copies(e, slot):
        return [pltpu.make_async_copy(w_hbm.at[e], w_f.at[slot], wsem.at[slot])
                for w_hbm, w_f in ((wg_hbm, wg_f), (wu_hbm, wu_f), (wd_hbm, wd_f))]

    @pl.when((i == 0) & (valid > 0))
    def _():
        for cp in weight_copies(te_ref[0], wslot_ref[0]):
            cp.start()

    @pl.when((valid > 0) & ((i == 0) | (te_ref[i] != te_ref[jnp.maximum(i - 1, 0)])))
    def _():
        slot = wslot_ref[i]
        for cp in weight_copies(te_ref[i], slot):
            cp.wait()
        wg_s[...] = wg_f[slot].astype(bf16)
        wu_s[...] = wu_f[slot].astype(bf16)
        wd_s[...] = wd_f[slot].astype(bf16)

        @pl.when(wnext_ref[i] >= 0)
        def _():
            for cp in weight_copies(wnext_ref[i], 1 - slot):
                cp.start()

    def tile_step(cur):
        ahead = (cur + ring - 1) % ring
        buf, buf_sem, next_buf, next_sem = xbufs[cur], sem.at[cur], xbufs[ahead], sem.at[ahead]
        _for_each_piece(valid, lambda off, piece: _copy_rows(
            xl_hbm, 0, buf, 0, piece, buf_sem).wait())

        nxt = jnp.minimum(i + ring - 1, last)
        go = (i + ring - 1 <= last) & (tvalid_ref[nxt] > 0)
        first, final = tlo_ref[nxt], thi_ref[nxt]
        for j in range(UNROLLED_RUNS):
            start_run(nxt, jnp.minimum(first + j, token_tiles - 1), go & (first + j <= final), next_buf, next_sem)

        x = _load_row_major(buf, ROW_TILE).astype(bf16)
        clear(buf)
        hg = jnp.dot(x, wg_s[...], preferred_element_type=f32)
        hu = jnp.dot(x, wu_s[...], preferred_element_type=f32)
        y = jnp.dot((hg * jax.nn.sigmoid(hg) * hu).astype(bf16), wd_s[...], preferred_element_type=f32)
        _store_row_major(o_ref, 0, y)

        @pl.when(go & (final - first >= UNROLLED_RUNS))
        def _():
            start_runs_loop(nxt, first + UNROLLED_RUNS, final + 1, next_buf, next_sem)

    for cur in range(ring):
        pl.when((valid > 0) & (i % ring == cur))(functools.partial(tile_step, cur))

    @pl.when(valid == 0)
    def _():
        o_ref[...] = jnp.zeros_like(o_ref)


def _experts(tables, xl, wg, wu, wd, tiles):
    hbm = pl.BlockSpec(memory_space=pl.ANY)
    return pl.pallas_call(
        _experts_body,
        grid_spec=pltpu.PrefetchScalarGridSpec(
            num_scalar_prefetch=len(tables),
            grid=(tiles,),
            in_specs=[hbm, hbm, hbm, hbm],
            out_specs=pl.BlockSpec((ROW_TILE * PIECES, LANES), lambda i, *_: (i, 0)),
            scratch_shapes=[pltpu.VMEM(w.shape[1:], bf16) for w in (wg, wu, wd)]
            + [pltpu.VMEM((2,) + w.shape[1:], f32) for w in (wg, wu, wd)]
            + [pltpu.SemaphoreType.DMA((2,)), pltpu.SemaphoreType.DMA((EXPERT_RING,))]
            + [pltpu.VMEM((ROW_TILE * PIECES, LANES), f32)] * EXPERT_RING),
        out_shape=jax.ShapeDtypeStruct((tiles * ROW_TILE * PIECES, LANES), f32),
        compiler_params=_cparams(),
        name="moe_experts",
    )(*tables, xl, wg, wu, wd)


def _combine_body(n_ref, gpos_ref, loff_ref, ys_hbm, x_ref, w_ref, l_ref, g_ref, op_ref, os_ref, sem, *ybufs,
                  prompt_tiles):
    i = pl.program_id(0)
    last = pl.num_programs(0) - 1
    tile_rows = SLOTS * ROW_TILE
    ring = len(ybufs)

    def start_run(t, e, enabled, buf, buf_sem, pieces=RUN_PIECES):
        k = t * N_EXPERTS + e
        _for_each_piece(jnp.where(enabled, n_ref[k], 0), lambda off, piece: _copy_rows(
            ys_hbm, gpos_ref[k] + off, buf, loff_ref[k] + off, piece, buf_sem).start(), pieces)

    @pl.when(i == 0)
    def _():
        for t in range(ring - 1):
            def run(e, c, t=t):
                start_run(t, e, True, ybufs[t], sem.at[t])
                return c
            lax.fori_loop(0, N_EXPERTS, run, 0)

    def step(cur):
        ahead = (cur + ring - 1) % ring
        buf, buf_sem, next_buf, next_sem = ybufs[cur], sem.at[cur], ybufs[ahead], sem.at[ahead]
        _copy_rows(ys_hbm, 0, buf, 0, tile_rows, buf_sem).wait()
        nxt, go = jnp.minimum(i + ring - 1, last), i + ring - 1 <= last
        longest = 0
        for e in range(N_EXPERTS):
            start_run(nxt, e, go, next_buf, next_sem, SHORT_PIECES)
            longest = longest | n_ref[nxt * N_EXPERTS + e]
        yl = _load_row_major(buf, tile_rows).astype(bf16)
        sorted_row = lax.broadcasted_iota(i32, (ROW_TILE, tile_rows), 1).astype(f32)
        w, lp = w_ref[...], l_ref[...]
        y1 = jnp.dot((sorted_row == lp[:, 0:1]).astype(bf16), yl, preferred_element_type=f32)
        y2 = jnp.dot((sorted_row == lp[:, 1:2]).astype(bf16), yl, preferred_element_type=f32)
        out = _rms(x_ref[...] + (w[:, 0:1] * y1 + w[:, 1:2] * y2), g_ref[...])

        @pl.when(i < prompt_tiles)
        def _():
            op_ref[...] = out

        @pl.when(i >= prompt_tiles)
        def _():
            os_ref[...] = out

        @pl.when(go & (longest >= 2 * SHORT_PIECES[0]))
        def _():
            def long_pieces(e, c):
                start_run(nxt, e, True, next_buf, next_sem, LONG_PIECES)
                return c
            lax.fori_loop(0, N_EXPERTS, long_pieces, 0)

    for cur in range(ring):
        pl.when(i % ring == cur)(functools.partial(step, cur))


def _combine(tables, ys, x1, wts, lpos, g, prompt_rows):
    prompt_tiles = prompt_rows // ROW_TILE
    tiles = x1.shape[0] // ROW_TILE
    row = lambda n: pl.BlockSpec((ROW_TILE, n), lambda i, *_: (i, 0))
    p_rows = lambda n: pl.BlockSpec((ROW_TILE, n), lambda i, *_: (jnp.minimum(i, prompt_tiles - 1), 0))
    s_rows = lambda n: pl.BlockSpec((ROW_TILE, n), lambda i, *_: (jnp.maximum(i - prompt_tiles, 0), 0))
    return pl.pallas_call(
        functools.partial(_combine_body, prompt_tiles=prompt_tiles),
        grid_spec=pltpu.PrefetchScalarGridSpec(
            num_scalar_prefetch=len(tables),
            grid=(tiles,),
            in_specs=[pl.BlockSpec(memory_space=pl.ANY), row(D_MODEL), row(2), row(2),
                      pl.BlockSpec((1, D_MODEL), lambda i, *_: (0, 0))],
            out_specs=[p_rows(D_MODEL), s_rows(D_MODEL)],
            scratch_shapes=[pltpu.SemaphoreType.DMA((COMBINE_RING,))]
            + [pltpu.VMEM((SLOTS * ROW_TILE * PIECES, LANES), f32)] * COMBINE_RING),
        out_shape=[jax.ShapeDtypeStruct((prompt_rows, D_MODEL), f32),
                   jax.ShapeDtypeStruct((x1.shape[0] - prompt_rows, D_MODEL), f32)],
        compiler_params=_cparams(),
        name="moe_combine",
    )(*tables, ys, x1, wts, lpos, g)


def _moe_tables(n, loff, tiles):
    cum = jnp.cumsum(n, axis=0) - n
    counts = jnp.sum(n, axis=0)
    padded = (counts + ROW_TILE - 1) // ROW_TILE * ROW_TILE
    ends = jnp.cumsum(padded)
    starts = ends - padded
    first = jnp.arange(tiles, dtype=i32) * ROW_TILE
    expert = jnp.minimum(jnp.sum((first[:, None] >= ends[None, :]).astype(i32), axis=1), N_EXPERTS - 1)
    sel = expert[:, None] == jnp.arange(N_EXPERTS)[None, :]
    pick = lambda v: jnp.sum(jnp.where(sel, v[None, :], 0), axis=1)
    pos = first - pick(starts)
    valid = jnp.where(first < ends[-1], jnp.clip(pick(counts) - pos, 0, ROW_TILE), 0)
    cum_t, n_t = cum.T[expert], n.T[expert]
    touches = (cum_t + n_t > pos[:, None]) & (cum_t < (pos + ROW_TILE)[:, None]) & (n_t > 0)
    tau = jnp.arange(n.shape[0], dtype=i32)[None, :]
    lo = jnp.min(jnp.where(touches, tau, n.shape[0]), axis=1)
    hi = jnp.max(jnp.where(touches, tau, -1), axis=1)
    ids = jnp.arange(N_EXPERTS)
    busy = counts > 0
    ordinal = jnp.sum(busy[None, :] & (ids[None, :] < ids[:, None]), axis=1)
    following = jnp.min(jnp.where(busy[None, :] & (ids[None, :] > ids[:, None]), ids[None, :], N_EXPERTS), axis=1)
    following = jnp.where(following < N_EXPERTS, following, -1)
    as_i32 = lambda v: v.astype(i32)
    flat = lambda v: v.reshape(-1).astype(i32)
    expert_tables = (tuple(map(as_i32, (expert, pos, valid, lo, hi, pick(ordinal) % 2, pick(following))))
                     + (flat(n), flat(cum), flat(loff)))
    combine_tables = (flat(n), flat(starts[None, :] + cum), flat(loff))
    return expert_tables, combine_tables


def kernel(x_prompt, x_sample, cache_k, cache_v, state_ssm_re, state_ssm_im, g_norm_mix, w_in, attn_sinks, ssm_a_re,
           ssm_a_im, ssm_log_dt, ssm_b_re, ssm_b_im, ssm_c_re, ssm_c_im, ssm_d, w_glu, b_glu, g_attn_out, g_ssm_out,
           w_out, g_norm_ffn, w_router_group, b_router_group, w_router_expert, b_router_expert, w_exp_gate, w_exp_up,
           w_exp_down, g_final):
    bp, lp, _ = x_prompt.shape
    bs, ls, _ = x_sample.shape
    depth = w_in.shape[0]
    assert depth == 1 and ls == SSM_STEPS and lp % ATTN_TILE == 0 and (bs * ls) % ROW_TILE == 0
    tp, ts = bp * lp, bs * ls
    wc = cache_k.shape[2]
    row2 = lambda v: v.reshape(1, -1)

    xp = x_prompt.reshape(tp, D_MODEL)
    xs = x_sample.reshape(ts, D_MODEL)
    w_in_bf = w_in[0].astype(bf16)
    qp, kp, vp, up = _proj(xp, row2(g_norm_mix[0]), w_in_bf, 1024)
    qs, kq, vq, us = _proj(xs, row2(g_norm_mix[0]), w_in_bf, ts)

    sinks = attn_sinks[0]
    g_att = row2(g_attn_out[0])
    ap = _attn_prompt(qp.reshape(bp, lp, -1), kp.reshape(bp, lp, -1), vp.reshape(bp, lp, -1), sinks, g_att)
    a_s, k_roll, v_roll = _attn_sample(qs.reshape(bs, ls, -1), kq.reshape(bs, ls, -1), vq.reshape(bs, ls, -1),
                                       cache_k[0].reshape(bs, wc, KV_WIDTH), cache_v[0].reshape(bs, wc, KV_WIDTH),
                                       sinks, g_att)

    bbar, steps, lev = _ssm_tables(ssm_a_re[0], ssm_a_im[0], ssm_log_dt[0], ssm_b_re[0], ssm_b_im[0])
    c_blocks = lambda c: c.reshape(SSM_BLOCKS, LANES, SSM_STATE)
    d_row = row2(ssm_d[0])
    to_blocks = lambda h: h.reshape(bs, SSM_BLOCKS, STATE_COLS).transpose(1, 0, 2)
    from_blocks = lambda h: h.transpose(1, 0, 2).reshape(bs, SSM_GROUPS, SSM_STATE)
    yp, hrp, hip, ysm, hrs, his = _ssm(up.reshape(bp, lp, -1), us.reshape(1, ts, -1), to_blocks(state_ssm_re[0]),
                                       to_blocks(state_ssm_im[0]), bbar, c_blocks(ssm_c_re[0]),
                                       c_blocks(ssm_c_im[0]), steps, lev, d_row)

    wr = jnp.zeros((D_MODEL, ROUTER_COLS), f32)
    wr = wr.at[:, :N_EXPERTS].set(w_router_expert[0]).at[:, N_EXPERTS:N_EXPERTS + N_EXPERT_GROUPS].set(w_router_group[0])
    br = jnp.zeros((1, ROUTER_COLS), f32)
    br = br.at[0, :N_EXPERTS].set(b_router_expert[0]).at[0, N_EXPERTS:N_EXPERTS + N_EXPERT_GROUPS].set(b_router_group[0])
    tri = jnp.tril(jnp.ones((ROW_TILE, ROW_TILE), bf16), -1)
    upper = jnp.triu(jnp.ones((N_EXPERTS, N_EXPERTS), bf16), 1)
    mix_w = (w_glu[0].astype(bf16), row2(b_glu[0]), row2(g_ssm_out[0]), w_out[0].astype(bf16), row2(g_norm_ffn[0]),
             wr.astype(bf16), br, tri, upper)
    x1, xl, wts, lpos, n_rows, n_off = _mix(ap.reshape(tp, -1), a_s.reshape(ts, -1), yp.reshape(tp, -1),
                                            ysm.reshape(ts, -1), xp, xs, *mix_w)

    per_tile = lambda v: v.reshape(-1, N_EXPERTS).astype(i32)
    tiles = (SLOTS * (tp + ts)) // ROW_TILE + N_EXPERTS
    expert_tables, combine_tables = _moe_tables(per_tile(n_rows), per_tile(n_off), tiles)
    expert_out = _experts(expert_tables, xl, w_exp_gate[0], w_exp_up[0], w_exp_down[0], tiles)
    y_p, y_s = _combine(combine_tables, expert_out, x1, wts, lpos, row2(g_final), tp)

    kvshape = lambda a, b: a.reshape(1, b, -1, N_KV_HEADS, HEAD_DIM)
    block_state = lambda h: h.reshape(bp, SSM_GROUPS, SSM_STATE)[None]
    wcp = min(WINDOW, lp)
    return (y_p.reshape(bp, lp, D_MODEL), y_s.reshape(bs, ls, D_MODEL),
            kvshape(kp.reshape(bp, lp, -1)[:, lp - wcp:], bp), kvshape(vp.reshape(bp, lp, -1)[:, lp - wcp:], bp),
            block_state(hrp), block_state(hip),
            kvshape(k_roll, bs), kvshape(v_roll, bs),
            from_blocks(hrs)[None], from_blocks(his)[None])
```

```python
import functools
import math

import jax
import jax.numpy as jnp
from jax import lax
from jax.experimental import pallas as pl
from jax.experimental.pallas import tpu as pltpu

f32, bf16, i32 = jnp.float32, jnp.bfloat16, jnp.int32

D_MODEL = 1024
CHUNK = 64
N_BACK = 2
WINDOW = 128
ATTN_WIDTH = 512
HEAD_DIM = 64
N_KV_HEADS = 2
Q_PER_KV = 4
KV_WIDTH = 128
SSM_WIDTH = 512
SSM_GROUP = 16
SSM_GROUPS = 32
SSM_STATE = 64
PROJ_WIDTH = 1280
N_EXPERT_GROUPS = 4
EXPERTS_PER_GROUP = 8
N_EXPERTS = 32
D_EXPERT = 512
EPS = 1e-6
NEG = -1e30

LANES = 128
SSM_STEPS = 16
SSM_SEQS = 2
SSM_BLOCKS = SSM_WIDTH // LANES
GROUPS_PER_BLOCK = LANES // SSM_GROUP
STATE_COLS = GROUPS_PER_BLOCK * SSM_STATE
ROW_TILE = 256
SLOTS = 2
PIECES = D_MODEL // LANES
RUN_PIECES = tuple(1 << b for b in reversed(range(int(math.log2(ROW_TILE)) + 1)))
SHORT_PIECES = tuple(p for p in RUN_PIECES if p <= 32)
LONG_PIECES = tuple(p for p in RUN_PIECES if p > 32)
MIX_TILES = 2
UNROLLED_RUNS = 20
EXPERT_RING = 3
COMBINE_RING = 3
VMEM_LIMIT = 56 * 1024 * 1024


def _cparams(n_axes=1, limit=VMEM_LIMIT):
    return pltpu.CompilerParams(dimension_semantics=("arbitrary",) * n_axes, vmem_limit_bytes=limit)


def _rms(x, g):
    return x * lax.rsqrt(jnp.mean(x * x, axis=-1, keepdims=True) + EPS) * g


def _bdot(a, b):
    return jnp.dot(a.astype(bf16), b.astype(bf16), preferred_element_type=f32)


def _proj_body(x_ref, g_ref, w_ref, q_ref, k_ref, v_ref, u_ref):
    h = _rms(x_ref[...], g_ref[...])
    z = _bdot(h, w_ref[...])
    q_ref[...] = z[:, :ATTN_WIDTH] * (HEAD_DIM ** -0.5)
    k_ref[...] = z[:, ATTN_WIDTH:ATTN_WIDTH + KV_WIDTH]
    v_ref[...] = z[:, ATTN_WIDTH + KV_WIDTH:ATTN_WIDTH + 2 * KV_WIDTH]
    u_ref[...] = z[:, ATTN_WIDTH + 2 * KV_WIDTH:]


def _proj(x2d, g, w_bf, tm):
    t = x2d.shape[0]
    row = lambda n: pl.BlockSpec((tm, n), lambda i: (i, 0))
    full = lambda a: pl.BlockSpec(a.shape, lambda i: (0,) * a.ndim)
    return pl.pallas_call(
        _proj_body,
        grid=(t // tm,),
        in_specs=[row(D_MODEL), full(g), full(w_bf)],
        out_specs=[row(ATTN_WIDTH), row(KV_WIDTH), row(KV_WIDTH), row(SSM_WIDTH)],
        out_shape=[jax.ShapeDtypeStruct((t, n), f32) for n in (ATTN_WIDTH, KV_WIDTH, KV_WIDTH, SSM_WIDTH)],
        compiler_params=_cparams(),
        name="proj",
    )(x2d, g, w_bf)


def _sink_column(sink_ref, kv, rows_per_head):
    r = lax.broadcasted_iota(i32, (Q_PER_KV * rows_per_head, 1), 0)
    col = jnp.full((Q_PER_KV * rows_per_head, 1), sink_ref[kv * Q_PER_KV], f32)
    for j in range(1, Q_PER_KV):
        col = jnp.where(r >= j * rows_per_head, sink_ref[kv * Q_PER_KV + j], col)
    return col


def _attend(qs, kc, vc, sink_col, valid):
    s = lax.dot_general(qs.astype(bf16), kc.astype(bf16), (((1,), (1,)), ((), ())), preferred_element_type=f32)
    if valid is not None:
        s = jnp.where(valid, s, NEG)
    m = jnp.maximum(jnp.max(s, axis=-1, keepdims=True), sink_col)
    p = jnp.exp(s - m)
    denom = jnp.sum(p, axis=-1, keepdims=True) + jnp.exp(sink_col - m)
    return _bdot(p, vc) / denom


def _heads_attend(q, k, v, sink_ref, valid):
    rows = q.shape[0]
    pieces = []
    for kv in range(N_KV_HEADS):
        qs = jnp.concatenate(
            [q[:, (kv * Q_PER_KV + j) * HEAD_DIM:(kv * Q_PER_KV + j + 1) * HEAD_DIM] for j in range(Q_PER_KV)], axis=0)
        o = _attend(qs, k[:, kv * HEAD_DIM:(kv + 1) * HEAD_DIM], v[:, kv * HEAD_DIM:(kv + 1) * HEAD_DIM],
                    _sink_column(sink_ref, kv, rows), valid)
        pieces += [o[j * rows:(j + 1) * rows] for j in range(Q_PER_KV)]
    return jnp.concatenate(pieces, axis=1)


ATTN_TILE = 256
CHUNKS_PER_TILE = ATTN_TILE // CHUNK
KEY_SPAN = (N_BACK + 1) * CHUNK


def _attn_prompt_body(sink_ref, q_ref, kp_ref, kc_ref, vp_ref, vc_ref, g_ref, o_ref):
    i = pl.program_id(1)
    kwin = jnp.concatenate([kp_ref[0], kc_ref[0]], axis=0)
    vwin = jnp.concatenate([vp_ref[0], vc_ref[0]], axis=0)
    key_chunk = lax.broadcasted_iota(i32, (1, KEY_SPAN), 1) // CHUNK
    for c in range(CHUNKS_PER_TILE):
        valid = (i * CHUNKS_PER_TILE + c - N_BACK + key_chunk) >= 0
        o = _heads_attend(q_ref[0, c * CHUNK:(c + 1) * CHUNK, :], kwin[c * CHUNK:c * CHUNK + KEY_SPAN],
                          vwin[c * CHUNK:c * CHUNK + KEY_SPAN], sink_ref, valid)
        o_ref[0, c * CHUNK:(c + 1) * CHUNK, :] = _rms(o, g_ref[...]).astype(bf16)


def _attn_prompt(q, k, v, sinks, g):
    b, l, _ = q.shape
    back = N_BACK * CHUNK
    per = ATTN_TILE // back
    prev = pl.BlockSpec((1, back, KV_WIDTH), lambda bi, i: (bi, jnp.maximum(i * per - 1, 0), 0))
    cur = pl.BlockSpec((1, ATTN_TILE, KV_WIDTH), lambda bi, i: (bi, i, 0))
    return pl.pallas_call(
        _attn_prompt_body,
        grid=(b, l // ATTN_TILE),
        in_specs=[pl.BlockSpec(memory_space=pltpu.SMEM),
                  pl.BlockSpec((1, ATTN_TILE, ATTN_WIDTH), lambda bi, i: (bi, i, 0)),
                  prev, cur, prev, cur,
                  pl.BlockSpec((1, ATTN_WIDTH), lambda bi, i: (0, 0))],
        out_specs=pl.BlockSpec((1, ATTN_TILE, ATTN_WIDTH), lambda bi, i: (bi, i, 0)),
        out_shape=jax.ShapeDtypeStruct((b, l, ATTN_WIDTH), bf16),
        compiler_params=_cparams(2),
        name="attn_prompt",
    )(sinks, q, k, k, v, v, g)


def _attn_sample_body(sink_ref, q_ref, kn_ref, vn_ref, ck_ref, cv_ref, g_ref, o_ref, nk_ref, nv_ref):
    kall = jnp.concatenate([ck_ref[0], kn_ref[0]], axis=0)
    vall = jnp.concatenate([cv_ref[0], vn_ref[0]], axis=0)
    o = _heads_attend(q_ref[0], kall, vall, sink_ref, None)
    o_ref[0] = _rms(o, g_ref[...]).astype(bf16)
    n_new = kn_ref.shape[1]
    nk_ref[0] = kall[n_new:]
    nv_ref[0] = vall[n_new:]


def _attn_sample(q, k_new, v_new, cache_k, cache_v, sinks, g):
    b, l, _ = q.shape
    wc = cache_k.shape[1]
    blk = lambda r, n: pl.BlockSpec((1, r, n), lambda bi: (bi, 0, 0))
    return pl.pallas_call(
        _attn_sample_body,
        grid=(b,),
        in_specs=[pl.BlockSpec(memory_space=pltpu.SMEM), blk(l, ATTN_WIDTH), blk(l, KV_WIDTH), blk(l, KV_WIDTH),
                  blk(wc, KV_WIDTH), blk(wc, KV_WIDTH), pl.BlockSpec((1, ATTN_WIDTH), lambda bi: (0, 0))],
        out_specs=[blk(l, ATTN_WIDTH), blk(wc, KV_WIDTH), blk(wc, KV_WIDTH)],
        out_shape=[jax.ShapeDtypeStruct((b, l, ATTN_WIDTH), bf16),
                   jax.ShapeDtypeStruct((b, wc, KV_WIDTH), f32), jax.ShapeDtypeStruct((b, wc, KV_WIDTH), f32)],
        compiler_params=_cparams(),
        name="attn_sample",
    )(sinks, q, k_new, v_new, cache_k, cache_v, g)


def _ssm_tables(a_re, a_im, log_dt, b_re, b_im):
    dt = jnp.exp(log_dt)[:, None]
    lam_r, lam_i = a_re * dt, a_im * dt

    def power(n):
        mag = jnp.exp(n * lam_r)
        return jnp.stack([mag * jnp.cos(n * lam_i), mag * jnp.sin(n * lam_i)])

    ar, ai = power(1.0)
    den = a_re * a_re + a_im * a_im
    nr, ni = ar - 1.0, ai
    fr = ((nr * a_re + ni * a_im) / den)[..., None]
    fi = ((ni * a_re - nr * a_im) / den)[..., None]
    bbar = jnp.stack([fr * b_re - fi * b_im, fr * b_im + fi * b_re])
    bbar = bbar.transpose(0, 1, 3, 2).reshape(2, SSM_BLOCKS, LANES, SSM_STATE)
    by_block = lambda t: t.reshape(t.shape[:-2] + (SSM_BLOCKS, STATE_COLS))
    steps = by_block(power(jnp.arange(SSM_STEPS + 1, dtype=f32)[:, None, None]))
    levels = by_block(power(SSM_STEPS * 2.0 ** jnp.arange(8, dtype=f32)[:, None, None]))
    return bbar, steps.transpose(2, 0, 1, 3), levels.transpose(2, 1, 0, 3)


def _ssm_chunk_rows(u_ref, nk):
    xs = [u_ref[0, pl.ds(s, nk, stride=SSM_STEPS), :] for s in range(SSM_STEPS)]
    pairs = [jnp.concatenate([xs[2 * p], xs[2 * p + 1]], axis=1).astype(bf16) for p in range(SSM_STEPS // 2)]
    return xs, pairs


def _ssm_intra(pairs, toep_ref, nk):
    nd = len(pairs)
    y = [None] * nd
    stack = nk < LANES
    for d in range(nd):
        if stack:
            r = jnp.dot(jnp.concatenate(pairs[:nd - d], axis=0), toep_ref[d], preferred_element_type=f32)
        for p in range(nd - d):
            blk = r[p * nk:(p + 1) * nk] if stack else jnp.dot(pairs[p], toep_ref[d], preferred_element_type=f32)
            y[p + d] = blk if y[p + d] is None else y[p + d] + blk
    return y


def _shift_rows(x, sh):
    rows = lax.broadcasted_iota(i32, (x.shape[0], 1), 0)
    return jnp.where(rows >= sh, pltpu.roll(x, sh, axis=0), 0.0)


def _group_index(shape, axis, sub):
    idx = lax.broadcasted_iota(i32, shape, axis)
    return jnp.right_shift(idx, int(math.log2(sub))) & (GROUPS_PER_BLOCK - 1)


def _build_ssm_weights(bb_ref, cr_ref, ci_ref, pw_ref, toep_s, wout_s, win_s, w_low):
    hp = lax.Precision.HIGHEST
    k = lax.broadcasted_iota(i32, (SSM_STATE, STATE_COLS), 0)
    n = lax.broadcasted_iota(i32, (SSM_STATE, STATE_COLS), 1)
    spread = ((n & (SSM_STATE - 1)) == k).astype(f32)
    own = _group_index((LANES, STATE_COLS), 0, SSM_GROUP) == _group_index((LANES, STATE_COLS), 1, SSM_STATE)

    def block_diag(compact):
        return jnp.where(own, jnp.dot(compact, spread, precision=hp, preferred_element_type=f32), 0.0)

    bb_r, bb_i = block_diag(bb_ref[0, 0]), block_diag(bb_ref[1, 0])
    c_r, c_i = block_diag(cr_ref[0]), block_diag(ci_ref[0])
    c_stack = jnp.concatenate([c_r.T, -c_i.T], axis=0)
    power = lambda e: (pw_ref[0, 0, e:e + 1, :], pw_ref[0, 1, e:e + 1, :])
    for s in range(SSM_STEPS):
        rows = slice(s * LANES, (s + 1) * LANES)
        pr, pi = power(SSM_STEPS - 1 - s)
        w = jnp.concatenate([bb_r * pr - bb_i * pi, bb_r * pi + bb_i * pr], axis=1)
        wout_s[rows, :] = w.astype(bf16)
        w_low[rows, :] = (w - w.astype(bf16).astype(f32)).astype(bf16)
        pr, pi = power(s + 1)
        g = jnp.concatenate([c_r * pr - c_i * pi, -(c_r * pi + c_i * pr)], axis=1)
        win_s[:, rows] = g.T.astype(bf16)
    c_high = c_stack.astype(bf16)
    c_low = (c_stack - c_high.astype(f32)).astype(bf16)
    lags = (jnp.dot(wout_s[...], c_high, preferred_element_type=f32)
            + (jnp.dot(wout_s[...], c_low, preferred_element_type=f32)
               + jnp.dot(w_low[...], c_high, preferred_element_type=f32)))
    lag_kernel = [lags[(SSM_STEPS - 1 - lag) * LANES:(SSM_STEPS - lag) * LANES] for lag in range(SSM_STEPS)]
    zero = jnp.zeros((LANES, LANES), f32)
    for d in range(SSM_STEPS // 2):
        top = jnp.concatenate([lag_kernel[2 * d], lag_kernel[2 * d + 1]], axis=1)
        bottom = jnp.concatenate([lag_kernel[2 * d - 1] if d > 0 else zero, lag_kernel[2 * d]], axis=1)
        toep_s[d] = jnp.concatenate([top, bottom], axis=0).astype(bf16)


def _ssm_chunks(u_ref, d_ref, y_ref, toep_s, wout_s, win_s, entry_state):
    nk = u_ref.shape[1] // SSM_STEPS
    xs, pairs = _ssm_chunk_rows(u_ref, nk)
    y = _ssm_intra(pairs, toep_s, nk)
    s = jnp.dot(jnp.concatenate(pairs, axis=1), wout_s[...], preferred_element_type=f32)
    hprev, hr, hi = entry_state(s[:, :STATE_COLS], s[:, STATE_COLS:])
    y2 = _bdot(hprev, win_s[...])
    for st in range(SSM_STEPS):
        piece = (y[st // 2][:, (st % 2) * LANES:(st % 2 + 1) * LANES] + y2[:, st * LANES:(st + 1) * LANES]
                 + d_ref[...] * xs[st])
        y_ref[0, pl.ds(st, nk, stride=SSM_STEPS), :] = piece
    return hr, hi


def _ssm_body(u_ref, us_ref, h0r_ref, h0i_ref, bb_ref, cr_ref, ci_ref, pw_ref, lev_ref, d_ref,
              y_ref, hr_ref, hi_ref, ys_ref, hrs_ref, his_ref, toep_s, wout_s, win_s, w_low):
    @pl.when(pl.program_id(1) == 0)
    def _():
        _build_ssm_weights(bb_ref, cr_ref, ci_ref, pw_ref, toep_s, wout_s, win_s, w_low)

        def one_chunk(sr, si):
            h0r, h0i = h0r_ref[0], h0i_ref[0]
            ar, ai = lev_ref[0, 0, 0:1, :], lev_ref[0, 0, 1:2, :]
            return (jnp.concatenate([h0r, h0i], axis=1), sr + ar * h0r - ai * h0i, si + ar * h0i + ai * h0r)
        hrs_ref[0], his_ref[0] = _ssm_chunks(us_ref, d_ref, ys_ref, toep_s, wout_s, win_s, one_chunk)

    def scan_chunks(sr, si):
        nk = sr.shape[0]
        level = 0
        while (1 << level) < nk:
            ar, ai = lev_ref[0, level, 0:1, :], lev_ref[0, level, 1:2, :]
            tr, ti = _shift_rows(sr, 1 << level), _shift_rows(si, 1 << level)
            sr, si = sr + ar * tr - ai * ti, si + ar * ti + ai * tr
            level += 1
        return (jnp.concatenate([_shift_rows(sr, 1), _shift_rows(si, 1)], axis=1), sr[nk - 1:nk], si[nk - 1:nk])
    for b in range(u_ref.shape[0]):
        hr_ref[b, 0], hi_ref[b, 0] = _ssm_chunks(u_ref.at[pl.ds(b, 1)], d_ref, y_ref.at[pl.ds(b, 1)], toep_s, wout_s,
                                                 win_s, scan_chunks)


def _ssm(u, us, h0r, h0i, bbar, c_re, c_im, steps, lev, d):
    b, l, _ = u.shape
    rows = us.shape[1]
    nb = rows // SSM_STEPS
    wspec = lambda a: pl.BlockSpec((1,) + a.shape[1:], lambda j, bi: (j,) + (0,) * (a.ndim - 1))
    assert b % SSM_SEQS == 0
    st = pl.BlockSpec((SSM_SEQS, 1, 1, STATE_COLS), lambda j, bi: (bi, j, 0, 0))
    sst = pl.BlockSpec((1, nb, STATE_COLS), lambda j, bi: (j, 0, 0))
    seq = pl.BlockSpec((SSM_SEQS, l, LANES), lambda j, bi: (bi, 0, j))
    sseq = pl.BlockSpec((1, rows, LANES), lambda j, bi: (0, 0, j))
    return pl.pallas_call(
        _ssm_body,
        grid=(SSM_BLOCKS, b // SSM_SEQS),
        in_specs=[seq, sseq, sst, sst, pl.BlockSpec((2, 1, LANES, SSM_STATE), lambda j, bi: (0, j, 0, 0)),
                  wspec(c_re), wspec(c_im), wspec(steps), wspec(lev), pl.BlockSpec((1, LANES), lambda j, bi: (0, j))],
        out_specs=[seq, st, st, sseq, sst, sst],
        out_shape=[jax.ShapeDtypeStruct((b, l, SSM_WIDTH), f32),
                   jax.ShapeDtypeStruct((b, SSM_BLOCKS, 1, STATE_COLS), f32),
                   jax.ShapeDtypeStruct((b, SSM_BLOCKS, 1, STATE_COLS), f32),
                   jax.ShapeDtypeStruct((1, rows, SSM_WIDTH), f32),
                   jax.ShapeDtypeStruct((SSM_BLOCKS, nb, STATE_COLS), f32),
                   jax.ShapeDtypeStruct((SSM_BLOCKS, nb, STATE_COLS), f32)],
        scratch_shapes=[pltpu.VMEM((SSM_STEPS // 2, 2 * LANES, 2 * LANES), bf16),
                        pltpu.VMEM((SSM_STEPS * LANES, 2 * STATE_COLS), bf16),
                        pltpu.VMEM((2 * STATE_COLS, SSM_STEPS * LANES), bf16),
                        pltpu.VMEM((SSM_STEPS * LANES, 2 * STATE_COLS), bf16)],
        compiler_params=_cparams(2),
        name="ssm",
    )(u, us, h0r, h0i, bbar, c_re, c_im, steps, lev, d)


ROUTER_COLS = LANES


def _mix_body(ap_ref, as_ref, yp_ref, ys_ref, xp_ref, xs_ref, wglu_ref, bglu_ref, gs_ref, wout_ref, gf_ref, wr_ref,
              br_ref, tri_ref, upper_ref, x1_ref, xl_ref, wts_ref, lpos_ref, n_ref, loff_ref, *, prompt_steps):
    is_prompt = pl.program_id(0) < prompt_steps
    pick_rows = lambda p_ref, s_ref: jnp.where(is_prompt, p_ref[...], jnp.concatenate([s_ref[...]] * MIX_TILES, axis=0))
    y = pick_rows(yp_ref, ys_ref)
    y = 0.5 * y * (1.0 + jnp.tanh(math.sqrt(2.0 / math.pi) * (y + 0.044715 * (y * y * y))))
    y = y * jax.nn.sigmoid(_bdot(y, wglu_ref[...]) + bglu_ref[...])
    attn = jnp.where(is_prompt, ap_ref[...].astype(f32),
                     jnp.concatenate([as_ref[...].astype(f32)] * MIX_TILES, axis=0)).astype(bf16)
    cat = jnp.concatenate([attn, _rms(y, gs_ref[...]).astype(bf16)], axis=1)
    x1 = pick_rows(xp_ref, xs_ref) + jnp.dot(cat, wout_ref[...], preferred_element_type=f32)
    x1_ref[...] = x1
    hf = _rms(x1, gf_ref[...])

    logits = _bdot(hf, wr_ref[...]) + br_ref[...]
    le = logits[:, :N_EXPERTS]
    lg = logits[:, N_EXPERTS:N_EXPERTS + N_EXPERT_GROUPS]
    tm = le.shape[0]
    gmax = jnp.max(lg, axis=-1, keepdims=True)
    gi = lax.broadcasted_iota(i32, (tm, N_EXPERT_GROUPS), 1).astype(f32)
    gsel = jnp.min(jnp.where(lg == gmax, gi, float(N_EXPERT_GROUPS)), axis=-1, keepdims=True)
    pg = 1.0 / jnp.sum(jnp.exp(lg - gmax), axis=-1, keepdims=True)
    ei_int = lax.broadcasted_iota(i32, (tm, N_EXPERTS), 1)
    ei = ei_int.astype(f32)
    egroup = jnp.right_shift(ei_int, int(math.log2(EXPERTS_PER_GROUP))).astype(f32)
    lm = jnp.where(egroup == gsel, le, NEG)
    v1 = jnp.max(lm, axis=-1, keepdims=True)
    i1 = jnp.min(jnp.where(lm == v1, ei, float(N_EXPERTS)), axis=-1, keepdims=True)
    lm2 = jnp.where(ei == i1, NEG, lm)
    v2 = jnp.max(lm2, axis=-1, keepdims=True)
    i2 = jnp.min(jnp.where(lm2 == v2, ei, float(N_EXPERTS)), axis=-1, keepdims=True)
    ex = jnp.exp(v2 - v1)
    wts_ref[...] = jnp.concatenate([pg / (1.0 + ex), pg * ex / (1.0 + ex)], axis=1)

    oh1 = (ei == i1).astype(f32)
    oh2 = (ei == i2).astype(f32)
    hf_bf = hf.astype(bf16)
    sorted_row = lax.broadcasted_iota(i32, (ROW_TILE, SLOTS * ROW_TILE), 1).astype(f32)
    for h in range(MIX_TILES):
        rows = slice(h * ROW_TILE, (h + 1) * ROW_TILE)
        both = (oh1[rows] + oh2[rows]).astype(bf16)
        before = jnp.dot(tri_ref[...], both, preferred_element_type=f32)
        count = jnp.sum(oh1[rows] + oh2[rows], axis=0, keepdims=True)
        lower = jnp.sum(jnp.dot(both, upper_ref[...], preferred_element_type=f32), axis=0, keepdims=True)
        lp1 = jnp.sum(oh1[rows] * (before + lower), axis=-1, keepdims=True)
        lp2 = jnp.sum(oh2[rows] * (before + lower), axis=-1, keepdims=True)
        lpos_ref[rows, :] = jnp.concatenate([lp1, lp2], axis=1)
        n_ref[h] = count
        loff_ref[h] = lower
        pick = ((sorted_row == lp1) | (sorted_row == lp2)).astype(bf16)
        xl = lax.dot_general(pick, hf_bf[rows], (((0,), (0,)), ((), ())), preferred_element_type=f32)
        _store_row_major(xl_ref, h * SLOTS * ROW_TILE, xl)


def _mix(a_p, a_s, ys_p, ys_s, x_p, x_s, wglu, bglu, gs, wout, gf, wr, br, tri, upper):
    tm = MIX_TILES * ROW_TILE
    assert x_p.shape[0] % tm == 0 and x_s.shape[0] == ROW_TILE
    prompt_steps = x_p.shape[0] // tm
    t = x_p.shape[0] + x_s.shape[0]
    p_rows = lambda n: pl.BlockSpec((tm, n), lambda i: (jnp.minimum(i, prompt_steps - 1), 0))
    s_rows = lambda n: pl.BlockSpec((ROW_TILE, n), lambda i: (0, 0))
    row = lambda n: pl.BlockSpec((tm, n), lambda i: (i, 0))
    full = lambda arr: pl.BlockSpec(arr.shape, lambda i: (0,) * arr.ndim)
    per_tile = pl.BlockSpec((MIX_TILES, 1, N_EXPERTS), lambda i: (i, 0, 0))
    return pl.pallas_call(
        functools.partial(_mix_body, prompt_steps=prompt_steps),
        grid=(prompt_steps + 1,),
        in_specs=[p_rows(ATTN_WIDTH), s_rows(ATTN_WIDTH), p_rows(SSM_WIDTH), s_rows(SSM_WIDTH), p_rows(D_MODEL),
                  s_rows(D_MODEL), full(wglu), full(bglu), full(gs), full(wout), full(gf), full(wr), full(br),
                  full(tri), full(upper)],
        out_specs=[row(D_MODEL), pl.BlockSpec((SLOTS * tm * PIECES, LANES), lambda i: (i, 0)), row(2), row(2),
                   per_tile, per_tile],
        out_shape=[jax.ShapeDtypeStruct((t, D_MODEL), f32),
                   jax.ShapeDtypeStruct((SLOTS * t * PIECES, LANES), f32),
                   jax.ShapeDtypeStruct((t, 2), f32), jax.ShapeDtypeStruct((t, 2), f32),
                   jax.ShapeDtypeStruct((t // ROW_TILE, 1, N_EXPERTS), f32),
                   jax.ShapeDtypeStruct((t // ROW_TILE, 1, N_EXPERTS), f32)],
        compiler_params=_cparams(),
        name="mix",
    )(a_p, a_s, ys_p, ys_s, x_p, x_s, wglu, bglu, gs, wout, gf, wr, br, tri, upper)


def _store_row_major(ref, first_row, x):
    for c in range(PIECES):
        ref[pl.ds(first_row * PIECES + c, x.shape[0], stride=PIECES), :] = x[:, c * LANES:(c + 1) * LANES]


def _load_row_major(ref, n_rows):
    return jnp.concatenate([ref[pl.ds(c, n_rows, stride=PIECES), :] for c in range(PIECES)], axis=1)


def _copy_rows(src, s_row, dst, d_row, n_rows, sem):
    return pltpu.make_async_copy(src.at[pl.ds(pl.multiple_of(s_row * PIECES, PIECES), n_rows * PIECES), :],
                                 dst.at[pl.ds(pl.multiple_of(d_row * PIECES, PIECES), n_rows * PIECES), :], sem)


def _for_each_piece(n, fn, pieces=RUN_PIECES):
    off = 0 if pieces[0] == RUN_PIECES[0] else n & ~(2 * pieces[0] - 1)
    for piece in pieces:
        @pl.when((n & piece) != 0)
        def _(off=off, piece=piece):
            fn(off, piece)
        off = off + (n & piece)


def _experts_body(te_ref, tpos_ref, tvalid_ref, tlo_ref, thi_ref, wslot_ref, wnext_ref, n_ref, cum_ref, loff_ref,
                  xl_hbm, wg_hbm, wu_hbm, wd_hbm, o_ref, wg_s, wu_s, wd_s, wg_f, wu_f, wd_f, wsem, sem, *xbufs):
    i = pl.program_id(0)
    last = pl.num_programs(0) - 1
    token_tiles = n_ref.shape[0] // N_EXPERTS
    ring = len(xbufs)

    def start_run(t, tau, enabled, buf, buf_sem):
        e, lo = te_ref[t], tpos_ref[t]
        k = tau * N_EXPERTS + e
        s, n = cum_ref[k], n_ref[k]
        a = jnp.maximum(s, lo)
        length = jnp.where(enabled, jnp.maximum(jnp.minimum(s + n, lo + ROW_TILE) - a, 0), 0)
        local = loff_ref[k] + (a - s)
        _for_each_piece(length, lambda off, piece: _copy_rows(
            xl_hbm, tau * (SLOTS * ROW_TILE) + local + off, buf, a - lo + off, piece, buf_sem).start())

    def start_runs_loop(t, first, stop, buf, buf_sem):
        def run(tau, c):
            start_run(t, tau, True, buf, buf_sem)
            return c
        lax.fori_loop(first, stop, run, 0)

    def clear(buf):
        buf[...] = jnp.zeros_like(buf)

    @pl.when(i == 0)
    def _():
        for buf in xbufs:
            clear(buf)
        for t in range(ring - 1):
            @pl.when(tvalid_ref[t] > 0)
            def _(t=t):
                start_runs_loop(t, tlo_ref[t], thi_ref[t] + 1, xbufs[t], sem.at[t])

    valid = tvalid_ref[i]

    def weight_copies(e, slot):
        return [pltpu.make_async_copy(w_hbm.at[e], w_f.at[slot], wsem.at[slot])
                for w_hbm, w_f in ((wg_hbm, wg_f), (wu_hbm, wu_f), (wd_hbm, wd_f))]

    @pl.when((i == 0) & (valid > 0))
    def _():
        for cp in weight_copies(te_ref[0], wslot_ref[0]):
            cp.start()

    @pl.when((valid > 0) & ((i == 0) | (te_ref[i] != te_ref[jnp.maximum(i - 1, 0)])))
    def _():
        slot = wslot_ref[i]
        for cp in weight_copies(te_ref[i], slot):
            cp.wait()
        wg_s[...] = wg_f[slot].astype(bf16)
        wu_s[...] = wu_f[slot].astype(bf16)
        wd_s[...] = wd_f[slot].astype(bf16)

        @pl.when(wnext_ref[i] >= 0)
        def _():
            for cp in weight_copies(wnext_ref[i], 1 - slot):
                cp.start()

    def tile_step(cur):
        ahead = (cur + ring - 1) % ring
        buf, buf_sem, next_buf, next_sem = xbufs[cur], sem.at[cur], xbufs[ahead], sem.at[ahead]
        _for_each_piece(valid, lambda off, piece: _copy_rows(
            xl_hbm, 0, buf, 0, piece, buf_sem).wait())

        nxt = jnp.minimum(i + ring - 1, last)
        go = (i + ring - 1 <= last) & (tvalid_ref[nxt] > 0)
        first, final = tlo_ref[nxt], thi_ref[nxt]
        for j in range(UNROLLED_RUNS):
            start_run(nxt, jnp.minimum(first + j, token_tiles - 1), go & (first + j <= final), next_buf, next_sem)

        x = _load_row_major(buf, ROW_TILE).astype(bf16)
        clear(buf)
        hg = jnp.dot(x, wg_s[...], preferred_element_type=f32)
        hu = jnp.dot(x, wu_s[...], preferred_element_type=f32)
        y = jnp.dot((hg * jax.nn.sigmoid(hg) * hu).astype(bf16), wd_s[...], preferred_element_type=f32)
        _store_row_major(o_ref, 0, y)

        @pl.when(go & (final - first >= UNROLLED_RUNS))
        def _():
            start_runs_loop(nxt, first + UNROLLED_RUNS, final + 1, next_buf, next_sem)

    for cur in range(ring):
        pl.when((valid > 0) & (i % ring == cur))(functools.partial(tile_step, cur))

    @pl.when(valid == 0)
    def _():
        o_ref[...] = jnp.zeros_like(o_ref)


def _experts(tables, xl, wg, wu, wd, tiles):
    hbm = pl.BlockSpec(memory_space=pl.ANY)
    return pl.pallas_call(
        _experts_body,
        grid_spec=pltpu.PrefetchScalarGridSpec(
            num_scalar_prefetch=len(tables),
            grid=(tiles,),
            in_specs=[hbm, hbm, hbm, hbm],
            out_specs=pl.BlockSpec((ROW_TILE * PIECES, LANES), lambda i, *_: (i, 0)),
            scratch_shapes=[pltpu.VMEM(w.shape[1:], bf16) for w in (wg, wu, wd)]
            + [pltpu.VMEM((2,) + w.shape[1:], f32) for w in (wg, wu, wd)]
            + [pltpu.SemaphoreType.DMA((2,)), pltpu.SemaphoreType.DMA((EXPERT_RING,))]
            + [pltpu.VMEM((ROW_TILE * PIECES, LANES), f32)] * EXPERT_RING),
        out_shape=jax.ShapeDtypeStruct((tiles * ROW_TILE * PIECES, LANES), f32),
        compiler_params=_cparams(),
        name="moe_experts",
    )(*tables, xl, wg, wu, wd)


def _combine_body(n_ref, gpos_ref, loff_ref, ys_hbm, x_ref, w_ref, l_ref, g_ref, op_ref, os_ref, sem, *ybufs,
                  prompt_tiles):
    i = pl.program_id(0)
    last = pl.num_programs(0) - 1
    tile_rows = SLOTS * ROW_TILE
    ring = len(ybufs)

    def start_run(t, e, enabled, buf, buf_sem, pieces=RUN_PIECES):
        k = t * N_EXPERTS + e
        _for_each_piece(jnp.where(enabled, n_ref[k], 0), lambda off, piece: _copy_rows(
            ys_hbm, gpos_ref[k] + off, buf, loff_ref[k] + off, piece, buf_sem).start(), pieces)

    @pl.when(i == 0)
    def _():
        for t in range(ring - 1):
            def run(e, c, t=t):
                start_run(t, e, True, ybufs[t], sem.at[t])
                return c
            lax.fori_loop(0, N_EXPERTS, run, 0)

    def step(cur):
        ahead = (cur + ring - 1) % ring
        buf, buf_sem, next_buf, next_sem = ybufs[cur], sem.at[cur], ybufs[ahead], sem.at[ahead]
        _copy_rows(ys_hbm, 0, buf, 0, tile_rows, buf_sem).wait()
        nxt, go = jnp.minimum(i + ring - 1, last), i + ring - 1 <= last
        longest = 0
        for e in range(N_EXPERTS):
            start_run(nxt, e, go, next_buf, next_sem, SHORT_PIECES)
            longest = longest | n_ref[nxt * N_EXPERTS + e]
        yl = _load_row_major(buf, tile_rows).astype(bf16)
        sorted_row = lax.broadcasted_iota(i32, (ROW_TILE, tile_rows), 1).astype(f32)
        w, lp = w_ref[...], l_ref[...]
        y1 = jnp.dot((sorted_row == lp[:, 0:1]).astype(bf16), yl, preferred_element_type=f32)
        y2 = jnp.dot((sorted_row == lp[:, 1:2]).astype(bf16), yl, preferred_element_type=f32)
        out = _rms(x_ref[...] + (w[:, 0:1] * y1 + w[:, 1:2] * y2), g_ref[...])

        @pl.when(i < prompt_tiles)
        def _():
            op_ref[...] = out

        @pl.when(i >= prompt_tiles)
        def _():
            os_ref[...] = out

        @pl.when(go & (longest >= 2 * SHORT_PIECES[0]))
        def _():
            def long_pieces(e, c):
                start_run(nxt, e, True, next_buf, next_sem, LONG_PIECES)
                return c
            lax.fori_loop(0, N_EXPERTS, long_pieces, 0)

    for cur in range(ring):
        pl.when(i % ring == cur)(functools.partial(step, cur))


def _combine(tables, ys, x1, wts, lpos, g, prompt_rows):
    prompt_tiles = prompt_rows // ROW_TILE
    tiles = x1.shape[0] // ROW_TILE
    row = lambda n: pl.BlockSpec((ROW_TILE, n), lambda i, *_: (i, 0))
    p_rows = lambda n: pl.BlockSpec((ROW_TILE, n), lambda i, *_: (jnp.minimum(i, prompt_tiles - 1), 0))
    s_rows = lambda n: pl.BlockSpec((ROW_TILE, n), lambda i, *_: (jnp.maximum(i - prompt_tiles, 0), 0))
    return pl.pallas_call(
        functools.partial(_combine_body, prompt_tiles=prompt_tiles),
        grid_spec=pltpu.PrefetchScalarGridSpec(
            num_scalar_prefetch=len(tables),
            grid=(tiles,),
            in_specs=[pl.BlockSpec(memory_space=pl.ANY), row(D_MODEL), row(2), row(2),
                      pl.BlockSpec((1, D_MODEL), lambda i, *_: (0, 0))],
            out_specs=[p_rows(D_MODEL), s_rows(D_MODEL)],
            scratch_shapes=[pltpu.SemaphoreType.DMA((COMBINE_RING,))]
            + [pltpu.VMEM((SLOTS * ROW_TILE * PIECES, LANES), f32)] * COMBINE_RING),
        out_shape=[jax.ShapeDtypeStruct((prompt_rows, D_MODEL), f32),
                   jax.ShapeDtypeStruct((x1.shape[0] - prompt_rows, D_MODEL), f32)],
        compiler_params=_cparams(),
        name="moe_combine",
    )(*tables, ys, x1, wts, lpos, g)


def _moe_tables(n, loff, tiles):
    cum = jnp.cumsum(n, axis=0) - n
    counts = jnp.sum(n, axis=0)
    padded = (counts + ROW_TILE - 1) // ROW_TILE * ROW_TILE
    ends = jnp.cumsum(padded)
    starts = ends - padded
    first = jnp.arange(tiles, dtype=i32) * ROW_TILE
    expert = jnp.minimum(jnp.sum((first[:, None] >= ends[None, :]).astype(i32), axis=1), N_EXPERTS - 1)
    sel = expert[:, None] == jnp.arange(N_EXPERTS)[None, :]
    pick = lambda v: jnp.sum(jnp.where(sel, v[None, :], 0), axis=1)
    pos = first - pick(starts)
    valid = jnp.where(first < ends[-1], jnp.clip(pick(counts) - pos, 0, ROW_TILE), 0)
    cum_t, n_t = cum.T[expert], n.T[expert]
    touches = (cum_t + n_t > pos[:, None]) & (cum_t < (pos + ROW_TILE)[:, None]) & (n_t > 0)
    tau = jnp.arange(n.shape[0], dtype=i32)[None, :]
    lo = jnp.min(jnp.where(touches, tau, n.shape[0]), axis=1)
    hi = jnp.max(jnp.where(touches, tau, -1), axis=1)
    ids = jnp.arange(N_EXPERTS)
    busy = counts > 0
    ordinal = jnp.sum(busy[None, :] & (ids[None, :] < ids[:, None]), axis=1)
    following = jnp.min(jnp.where(busy[None, :] & (ids[None, :] > ids[:, None]), ids[None, :], N_EXPERTS), axis=1)
    following = jnp.where(following < N_EXPERTS, following, -1)
    as_i32 = lambda v: v.astype(i32)
    flat = lambda v: v.reshape(-1).astype(i32)
    expert_tables = (tuple(map(as_i32, (expert, pos, valid, lo, hi, pick(ordinal) % 2, pick(following))))
                     + (flat(n), flat(cum), flat(loff)))
    combine_tables = (flat(n), flat(starts[None, :] + cum), flat(loff))
    return expert_tables, combine_tables


def kernel(x_prompt, x_sample, cache_k, cache_v, state_ssm_re, state_ssm_im, g_norm_mix, w_in, attn_sinks, ssm_a_re,
           ssm_a_im, ssm_log_dt, ssm_b_re, ssm_b_im, ssm_c_re, ssm_c_im, ssm_d, w_glu, b_glu, g_attn_out, g_ssm_out,
           w_out, g_norm_ffn, w_router_group, b_router_group, w_router_expert, b_router_expert, w_exp_gate, w_exp_up,
           w_exp_down, g_final):
    bp, lp, _ = x_prompt.shape
    bs, ls, _ = x_sample.shape
    depth = w_in.shape[0]
    assert depth == 1 and ls == SSM_STEPS and lp % ATTN_TILE == 0 and (bs * ls) % ROW_TILE == 0
    tp, ts = bp * lp, bs * ls
    wc = cache_k.shape[2]
    row2 = lambda v: v.reshape(1, -1)

    xp = x_prompt.reshape(tp, D_MODEL)
    xs = x_sample.reshape(ts, D_MODEL)
    w_in_bf = w_in[0].astype(bf16)
    qp, kp, vp, up = _proj(xp, row2(g_norm_mix[0]), w_in_bf, 1024)
    qs, kq, vq, us = _proj(xs, row2(g_norm_mix[0]), w_in_bf, ts)

    sinks = attn_sinks[0]
    g_att = row2(g_attn_out[0])
    ap = _attn_prompt(qp.reshape(bp, lp, -1), kp.reshape(bp, lp, -1), vp.reshape(bp, lp, -1), sinks, g_att)
    a_s, k_roll, v_roll = _attn_sample(qs.reshape(bs, ls, -1), kq.reshape(bs, ls, -1), vq.reshape(bs, ls, -1),
                                       cache_k[0].reshape(bs, wc, KV_WIDTH), cache_v[0].reshape(bs, wc, KV_WIDTH),
                                       sinks, g_att)

    bbar, steps, lev = _ssm_tables(ssm_a_re[0], ssm_a_im[0], ssm_log_dt[0], ssm_b_re[0], ssm_b_im[0])
    c_blocks = lambda c: c.reshape(SSM_BLOCKS, LANES, SSM_STATE)
    d_row = row2(ssm_d[0])
    to_blocks = lambda h: h.reshape(bs, SSM_BLOCKS, STATE_COLS).transpose(1, 0, 2)
    from_blocks = lambda h: h.transpose(1, 0, 2).reshape(bs, SSM_GROUPS, SSM_STATE)
    yp, hrp, hip, ysm, hrs, his = _ssm(up.reshape(bp, lp, -1), us.reshape(1, ts, -1), to_blocks(state_ssm_re[0]),
                                       to_blocks(state_ssm_im[0]), bbar, c_blocks(ssm_c_re[0]),
                                       c_blocks(ssm_c_im[0]), steps, lev, d_row)

    wr = jnp.zeros((D_MODEL, ROUTER_COLS), f32)
    wr = wr.at[:, :N_EXPERTS].set(w_router_expert[0]).at[:, N_EXPERTS:N_EXPERTS + N_EXPERT_GROUPS].set(w_router_group[0])
    br = jnp.zeros((1, ROUTER_COLS), f32)
    br = br.at[0, :N_EXPERTS].set(b_router_expert[0]).at[0, N_EXPERTS:N_EXPERTS + N_EXPERT_GROUPS].set(b_router_group[0])
    tri = jnp.tril(jnp.ones((ROW_TILE, ROW_TILE), bf16), -1)
    upper = jnp.triu(jnp.ones((N_EXPERTS, N_EXPERTS), bf16), 1)
    mix_w = (w_glu[0].astype(bf16), row2(b_glu[0]), row2(g_ssm_out[0]), w_out[0].astype(bf16), row2(g_norm_ffn[0]),
             wr.astype(bf16), br, tri, upper)
    x1, xl, wts, lpos, n_rows, n_off = _mix(ap.reshape(tp, -1), a_s.reshape(ts, -1), yp.reshape(tp, -1),
                                            ysm.reshape(ts, -1), xp, xs, *mix_w)

    per_tile = lambda v: v.reshape(-1, N_EXPERTS).astype(i32)
    tiles = (SLOTS * (tp + ts)) // ROW_TILE + N_EXPERTS
    expert_tables, combine_tables = _moe_tables(per_tile(n_rows), per_tile(n_off), tiles)
    expert_out = _experts(expert_tables, xl, w_exp_gate[0], w_exp_up[0], w_exp_down[0], tiles)
    y_p, y_s = _combine(combine_tables, expert_out, x1, wts, lpos, row2(g_final), tp)

    kvshape = lambda a, b: a.reshape(1, b, -1, N_KV_HEADS, HEAD_DIM)
    block_state = lambda h: h.reshape(bp, SSM_GROUPS, SSM_STATE)[None]
    wcp = min(WINDOW, lp)
    return (y_p.reshape(bp, lp, D_MODEL), y_s.reshape(bs, ls, D_MODEL),
            kvshape(kp.reshape(bp, lp, -1)[:, lp - wcp:], bp), kvshape(vp.reshape(bp, lp, -1)[:, lp - wcp:], bp),
            block_state(hrp), block_state(hip),
            kvshape(k_roll, bs), kvshape(v_roll, bs),
            from_blocks(hrs)[None], from_blocks(his)[None])
```

```python
import functools
import math

import jax
import jax.numpy as jnp
from jax import lax
from jax.experimental import pallas as pl
from jax.experimental.pallas import tpu as pltpu

f32, bf16, i32 = jnp.float32, jnp.bfloat16, jnp.int32

D_MODEL = 1024
CHUNK = 64
N_BACK = 2
WINDOW = 128
ATTN_WIDTH = 512
HEAD_DIM = 64
N_KV_HEADS = 2
Q_PER_KV = 4
KV_WIDTH = 128
SSM_WIDTH = 512
SSM_GROUP = 16
SSM_GROUPS = 32
SSM_STATE = 64
PROJ_WIDTH = 1280
N_EXPERT_GROUPS = 4
EXPERTS_PER_GROUP = 8
N_EXPERTS = 32
D_EXPERT = 512
EPS = 1e-6
NEG = -1e30

LANES = 128
SSM_STEPS = 16
SSM_SEQS = 2
SSM_BLOCKS = SSM_WIDTH // LANES
GROUPS_PER_BLOCK = LANES // SSM_GROUP
STATE_COLS = GROUPS_PER_BLOCK * SSM_STATE
ROW_TILE = 256
SLOTS = 2
PIECES = D_MODEL // LANES
RUN_PIECES = tuple(1 << b for b in reversed(range(int(math.log2(ROW_TILE)) + 1)))
SHORT_PIECES = tuple(p for p in RUN_PIECES if p <= 32)
LONG_PIECES = tuple(p for p in RUN_PIECES if p > 32)
MIX_TILES = 2
UNROLLED_RUNS = 20
EXPERT_RING = 3
COMBINE_RING = 3
VMEM_LIMIT = 56 * 1024 * 1024


def _cparams(n_axes=1, limit=VMEM_LIMIT):
    return pltpu.CompilerParams(dimension_semantics=("arbitrary",) * n_axes, vmem_limit_bytes=limit)


def _rms(x, g):
    return x * lax.rsqrt(jnp.mean(x * x, axis=-1, keepdims=True) + EPS) * g


def _bdot(a, b):
    return jnp.dot(a.astype(bf16), b.astype(bf16), preferred_element_type=f32)


def _proj_body(x_ref, g_ref, w_ref, q_ref, k_ref, v_ref, u_ref):
    h = _rms(x_ref[...], g_ref[...])
    z = _bdot(h, w_ref[...])
    q_ref[...] = z[:, :ATTN_WIDTH] * (HEAD_DIM ** -0.5)
    k_ref[...] = z[:, ATTN_WIDTH:ATTN_WIDTH + KV_WIDTH]
    v_ref[...] = z[:, ATTN_WIDTH + KV_WIDTH:ATTN_WIDTH + 2 * KV_WIDTH]
    u_ref[...] = z[:, ATTN_WIDTH + 2 * KV_WIDTH:]


def _proj(x2d, g, w_bf, tm):
    t = x2d.shape[0]
    row = lambda n: pl.BlockSpec((tm, n), lambda i: (i, 0))
    full = lambda a: pl.BlockSpec(a.shape, lambda i: (0,) * a.ndim)
    return pl.pallas_call(
        _proj_body,
        grid=(t // tm,),
        in_specs=[row(D_MODEL), full(g), full(w_bf)],
        out_specs=[row(ATTN_WIDTH), row(KV_WIDTH), row(KV_WIDTH), row(SSM_WIDTH)],
        out_shape=[jax.ShapeDtypeStruct((t, n), f32) for n in (ATTN_WIDTH, KV_WIDTH, KV_WIDTH, SSM_WIDTH)],
        compiler_params=_cparams(),
        name="proj",
    )(x2d, g, w_bf)


def _sink_column(sink_ref, kv, rows_per_head):
    r = lax.broadcasted_iota(i32, (Q_PER_KV * rows_per_head, 1), 0)
    col = jnp.full((Q_PER_KV * rows_per_head, 1), sink_ref[kv * Q_PER_KV], f32)
    for j in range(1, Q_PER_KV):
        col = jnp.where(r >= j * rows_per_head, sink_ref[kv * Q_PER_KV + j], col)
    return col


def _attend(qs, kc, vc, sink_col, valid):
    s = lax.dot_general(qs.astype(bf16), kc.astype(bf16), (((1,), (1,)), ((), ())), preferred_element_type=f32)
    if valid is not None:
        s = jnp.where(valid, s, NEG)
    m = jnp.maximum(jnp.max(s, axis=-1, keepdims=True), sink_col)
    p = jnp.exp(s - m)
    denom = jnp.sum(p, axis=-1, keepdims=True) + jnp.exp(sink_col - m)
    return _bdot(p, vc) / denom


def _heads_attend(q, k, v, sink_ref, valid):
    rows = q.shape[0]
    pieces = []
    for kv in range(N_KV_HEADS):
        qs = jnp.concatenate(
            [q[:, (kv * Q_PER_KV + j) * HEAD_DIM:(kv * Q_PER_KV + j + 1) * HEAD_DIM] for j in range(Q_PER_KV)], axis=0)
        o = _attend(qs, k[:, kv * HEAD_DIM:(kv + 1) * HEAD_DIM], v[:, kv * HEAD_DIM:(kv + 1) * HEAD_DIM],
                    _sink_column(sink_ref, kv, rows), valid)
        pieces += [o[j * rows:(j + 1) * rows] for j in range(Q_PER_KV)]
    return jnp.concatenate(pieces, axis=1)


ATTN_TILE = 256
CHUNKS_PER_TILE = ATTN_TILE // CHUNK
KEY_SPAN = (N_BACK + 1) * CHUNK


def _attn_prompt_body(sink_ref, q_ref, kp_ref, kc_ref, vp_ref, vc_ref, g_ref, o_ref):
    i = pl.program_id(1)
    kwin = jnp.concatenate([kp_ref[0], kc_ref[0]], axis=0)
    vwin = jnp.concatenate([vp_ref[0], vc_ref[0]], axis=0)
    key_chunk = lax.broadcasted_iota(i32, (1, KEY_SPAN), 1) // CHUNK
    for c in range(CHUNKS_PER_TILE):
        valid = (i * CHUNKS_PER_TILE + c - N_BACK + key_chunk) >= 0
        o = _heads_attend(q_ref[0, c * CHUNK:(c + 1) * CHUNK, :], kwin[c * CHUNK:c * CHUNK + KEY_SPAN],
                          vwin[c * CHUNK:c * CHUNK + KEY_SPAN], sink_ref, valid)
        o_ref[0, c * CHUNK:(c + 1) * CHUNK, :] = _rms(o, g_ref[...]).astype(bf16)


def _attn_prompt(q, k, v, sinks, g):
    b, l, _ = q.shape
    back = N_BACK * CHUNK
    per = ATTN_TILE // back
    prev = pl.BlockSpec((1, back, KV_WIDTH), lambda bi, i: (bi, jnp.maximum(i * per - 1, 0), 0))
    cur = pl.BlockSpec((1, ATTN_TILE, KV_WIDTH), lambda bi, i: (bi, i, 0))
    return pl.pallas_call(
        _attn_prompt_body,
        grid=(b, l // ATTN_TILE),
        in_specs=[pl.BlockSpec(memory_space=pltpu.SMEM),
                  pl.BlockSpec((1, ATTN_TILE, ATTN_WIDTH), lambda bi, i: (bi, i, 0)),
                  prev, cur, prev, cur,
                  pl.BlockSpec((1, ATTN_WIDTH), lambda bi, i: (0, 0))],
        out_specs=pl.BlockSpec((1, ATTN_TILE, ATTN_WIDTH), lambda bi, i: (bi, i, 0)),
        out_shape=jax.ShapeDtypeStruct((b, l, ATTN_WIDTH), bf16),
        compiler_params=_cparams(2),
        name="attn_prompt",
    )(sinks, q, k, k, v, v, g)


SAMPLE_STREAMS = 8


def _attn_sample_body(sink_ref, q_ref, kn_ref, vn_ref, ck_ref, cv_ref, g_ref, o_ref, nk_ref, nv_ref):
    n_new = kn_ref.shape[1]
    for b in range(q_ref.shape[0]):
        kall = jnp.concatenate([ck_ref[b], kn_ref[b]], axis=0)
        vall = jnp.concatenate([cv_ref[b], vn_ref[b]], axis=0)
        o = _heads_attend(q_ref[b], kall, vall, sink_ref, None)
        o_ref[b] = _rms(o, g_ref[...]).astype(bf16)
        nk_ref[b] = kall[n_new:]
        nv_ref[b] = vall[n_new:]


def _attn_sample(q, k_new, v_new, cache_k, cache_v, sinks, g):
    b, l, _ = q.shape
    wc = cache_k.shape[1]
    assert b % SAMPLE_STREAMS == 0
    blk = lambda r, n: pl.BlockSpec((SAMPLE_STREAMS, r, n), lambda bi: (bi, 0, 0))
    return pl.pallas_call(
        _attn_sample_body,
        grid=(b // SAMPLE_STREAMS,),
        in_specs=[pl.BlockSpec(memory_space=pltpu.SMEM), blk(l, ATTN_WIDTH), blk(l, KV_WIDTH), blk(l, KV_WIDTH),
                  blk(wc, KV_WIDTH), blk(wc, KV_WIDTH), pl.BlockSpec((1, ATTN_WIDTH), lambda bi: (0, 0))],
        out_specs=[blk(l, ATTN_WIDTH), blk(wc, KV_WIDTH), blk(wc, KV_WIDTH)],
        out_shape=[jax.ShapeDtypeStruct((b, l, ATTN_WIDTH), bf16),
                   jax.ShapeDtypeStruct((b, wc, KV_WIDTH), f32), jax.ShapeDtypeStruct((b, wc, KV_WIDTH), f32)],
        compiler_params=_cparams(),
        name="attn_sample",
    )(sinks, q, k_new, v_new, cache_k, cache_v, g)


def _ssm_tables(a_re, a_im, log_dt, b_re, b_im):
    dt = jnp.exp(log_dt)[:, None]
    lam_r, lam_i = a_re * dt, a_im * dt

    def power(n):
        mag = jnp.exp(n * lam_r)
        return jnp.stack([mag * jnp.cos(n * lam_i), mag * jnp.sin(n * lam_i)])

    ar, ai = power(1.0)
    den = a_re * a_re + a_im * a_im
    nr, ni = ar - 1.0, ai
    fr = ((nr * a_re + ni * a_im) / den)[..., None]
    fi = ((ni * a_re - nr * a_im) / den)[..., None]
    bbar = jnp.stack([fr * b_re - fi * b_im, fr * b_im + fi * b_re])
    bbar = bbar.transpose(0, 1, 3, 2).reshape(2, SSM_BLOCKS, LANES, SSM_STATE)
    by_block = lambda t: t.reshape(t.shape[:-2] + (SSM_BLOCKS, STATE_COLS))
    steps = by_block(power(jnp.arange(SSM_STEPS + 1, dtype=f32)[:, None, None]))
    levels = by_block(power(SSM_STEPS * 2.0 ** jnp.arange(8, dtype=f32)[:, None, None]))
    return bbar, steps.transpose(2, 0, 1, 3), levels.transpose(2, 1, 0, 3)


def _ssm_chunk_rows(u_ref, nk):
    xs = [u_ref[0, pl.ds(s, nk, stride=SSM_STEPS), :] for s in range(SSM_STEPS)]
    pairs = [jnp.concatenate([xs[2 * p], xs[2 * p + 1]], axis=1).astype(bf16) for p in range(SSM_STEPS // 2)]
    return xs, pairs


def _ssm_intra(pairs, toep_ref, nk):
    nd = len(pairs)
    y = [None] * nd
    stack = nk < LANES
    for d in range(nd):
        if stack:
            r = jnp.dot(jnp.concatenate(pairs[:nd - d], axis=0), toep_ref[d], preferred_element_type=f32)
        for p in range(nd - d):
            blk = r[p * nk:(p + 1) * nk] if stack else jnp.dot(pairs[p], toep_ref[d], preferred_element_type=f32)
            y[p + d] = blk if y[p + d] is None else y[p + d] + blk
    return y


def _shift_rows(x, sh):
    rows = lax.broadcasted_iota(i32, (x.shape[0], 1), 0)
    return jnp.where(rows >= sh, pltpu.roll(x, sh, axis=0), 0.0)


def _group_index(shape, axis, sub):
    idx = lax.broadcasted_iota(i32, shape, axis)
    return jnp.right_shift(idx, int(math.log2(sub))) & (GROUPS_PER_BLOCK - 1)


def _build_ssm_weights(bb_ref, cr_ref, ci_ref, pw_ref, toep_s, wout_s, win_s, w_low):
    hp = lax.Precision.HIGHEST
    k = lax.broadcasted_iota(i32, (SSM_STATE, STATE_COLS), 0)
    n = lax.broadcasted_iota(i32, (SSM_STATE, STATE_COLS), 1)
    spread = ((n & (SSM_STATE - 1)) == k).astype(f32)
    own = _group_index((LANES, STATE_COLS), 0, SSM_GROUP) == _group_index((LANES, STATE_COLS), 1, SSM_STATE)

    def block_diag(compact):
        return jnp.where(own, jnp.dot(compact, spread, precision=hp, preferred_element_type=f32), 0.0)

    bb_r, bb_i = block_diag(bb_ref[0, 0]), block_diag(bb_ref[1, 0])
    c_r, c_i = block_diag(cr_ref[0]), block_diag(ci_ref[0])
    c_stack = jnp.concatenate([c_r.T, -c_i.T], axis=0)
    power = lambda e: (pw_ref[0, 0, e:e + 1, :], pw_ref[0, 1, e:e + 1, :])
    for s in range(SSM_STEPS):
        rows = slice(s * LANES, (s + 1) * LANES)
        pr, pi = power(SSM_STEPS - 1 - s)
        w = jnp.concatenate([bb_r * pr - bb_i * pi, bb_r * pi + bb_i * pr], axis=1)
        wout_s[rows, :] = w.astype(bf16)
        w_low[rows, :] = (w - w.astype(bf16).astype(f32)).astype(bf16)
        pr, pi = power(s + 1)
        g = jnp.concatenate([c_r * pr - c_i * pi, -(c_r * pi + c_i * pr)], axis=1)
        win_s[:, rows] = g.T.astype(bf16)
    c_high = c_stack.astype(bf16)
    c_low = (c_stack - c_high.astype(f32)).astype(bf16)
    lags = (jnp.dot(wout_s[...], c_high, preferred_element_type=f32)
            + (jnp.dot(wout_s[...], c_low, preferred_element_type=f32)
               + jnp.dot(w_low[...], c_high, preferred_element_type=f32)))
    lag_kernel = [lags[(SSM_STEPS - 1 - lag) * LANES:(SSM_STEPS - lag) * LANES] for lag in range(SSM_STEPS)]
    zero = jnp.zeros((LANES, LANES), f32)
    for d in range(SSM_STEPS // 2):
        top = jnp.concatenate([lag_kernel[2 * d], lag_kernel[2 * d + 1]], axis=1)
        bottom = jnp.concatenate([lag_kernel[2 * d - 1] if d > 0 else zero, lag_kernel[2 * d]], axis=1)
        toep_s[d] = jnp.concatenate([top, bottom], axis=0).astype(bf16)


def _ssm_chunks(u_ref, d_ref, y_ref, toep_s, wout_s, win_s, entry_state):
    nk = u_ref.shape[1] // SSM_STEPS
    xs, pairs = _ssm_chunk_rows(u_ref, nk)
    y = _ssm_intra(pairs, toep_s, nk)
    s = jnp.dot(jnp.concatenate(pairs, axis=1), wout_s[...], preferred_element_type=f32)
    hprev, hr, hi = entry_state(s[:, :STATE_COLS], s[:, STATE_COLS:])
    y2 = _bdot(hprev, win_s[...])
    for st in range(SSM_STEPS):
        piece = (y[st // 2][:, (st % 2) * LANES:(st % 2 + 1) * LANES] + y2[:, st * LANES:(st + 1) * LANES]
                 + d_ref[...] * xs[st])
        y_ref[0, pl.ds(st, nk, stride=SSM_STEPS), :] = piece
    return hr, hi


def _ssm_body(u_ref, us_ref, h0r_ref, h0i_ref, bb_ref, cr_ref, ci_ref, pw_ref, lev_ref, d_ref,
              y_ref, hr_ref, hi_ref, ys_ref, hrs_ref, his_ref, toep_s, wout_s, win_s, w_low):
    @pl.when(pl.program_id(1) == 0)
    def _():
        _build_ssm_weights(bb_ref, cr_ref, ci_ref, pw_ref, toep_s, wout_s, win_s, w_low)

        def one_chunk(sr, si):
            h0r, h0i = h0r_ref[...], h0i_ref[...]
            ar, ai = lev_ref[0, 0, 0:1, :], lev_ref[0, 0, 1:2, :]
            return (jnp.concatenate([h0r, h0i], axis=1), sr + ar * h0r - ai * h0i, si + ar * h0i + ai * h0r)
        hrs_ref[...], his_ref[...] = _ssm_chunks(us_ref, d_ref, ys_ref, toep_s, wout_s, win_s, one_chunk)

    def scan_chunks(sr, si):
        nk = sr.shape[0]
        level = 0
        while (1 << level) < nk:
            ar, ai = lev_ref[0, level, 0:1, :], lev_ref[0, level, 1:2, :]
            tr, ti = _shift_rows(sr, 1 << level), _shift_rows(si, 1 << level)
            sr, si = sr + ar * tr - ai * ti, si + ar * ti + ai * tr
            level += 1
        return (jnp.concatenate([_shift_rows(sr, 1), _shift_rows(si, 1)], axis=1), sr[nk - 1:nk], si[nk - 1:nk])
    for b in range(u_ref.shape[0]):
        hr_ref[b, 0], hi_ref[b, 0] = _ssm_chunks(u_ref.at[pl.ds(b, 1)], d_ref, y_ref.at[pl.ds(b, 1)], toep_s, wout_s,
                                                 win_s, scan_chunks)


def _ssm(u, us, h0r, h0i, bbar, c_re, c_im, steps, lev, d):
    b, l, _ = u.shape
    rows = us.shape[1]
    nb = rows // SSM_STEPS
    wspec = lambda a: pl.BlockSpec((1,) + a.shape[1:], lambda j, bi: (j,) + (0,) * (a.ndim - 1))
    assert b % SSM_SEQS == 0
    st = pl.BlockSpec((SSM_SEQS, 1, 1, STATE_COLS), lambda j, bi: (bi, j, 0, 0))
    sst = pl.BlockSpec((nb, STATE_COLS), lambda j, bi: (0, j))
    seq = pl.BlockSpec((SSM_SEQS, l, LANES), lambda j, bi: (bi, 0, j))
    sseq = pl.BlockSpec((1, rows, LANES), lambda j, bi: (0, 0, j))
    return pl.pallas_call(
        _ssm_body,
        grid=(SSM_BLOCKS, b // SSM_SEQS),
        in_specs=[seq, sseq, sst, sst, pl.BlockSpec((2, 1, LANES, SSM_STATE), lambda j, bi: (0, j, 0, 0)),
                  wspec(c_re), wspec(c_im), wspec(steps), wspec(lev), pl.BlockSpec((1, LANES), lambda j, bi: (0, j))],
        out_specs=[seq, st, st, sseq, sst, sst],
        out_shape=[jax.ShapeDtypeStruct((b, l, SSM_WIDTH), f32),
                   jax.ShapeDtypeStruct((b, SSM_BLOCKS, 1, STATE_COLS), f32),
                   jax.ShapeDtypeStruct((b, SSM_BLOCKS, 1, STATE_COLS), f32),
                   jax.ShapeDtypeStruct((1, rows, SSM_WIDTH), f32),
                   jax.ShapeDtypeStruct((nb, SSM_BLOCKS * STATE_COLS), f32),
                   jax.ShapeDtypeStruct((nb, SSM_BLOCKS * STATE_COLS), f32)],
        scratch_shapes=[pltpu.VMEM((SSM_STEPS // 2, 2 * LANES, 2 * LANES), bf16),
                        pltpu.VMEM((SSM_STEPS * LANES, 2 * STATE_COLS), bf16),
                        pltpu.VMEM((2 * STATE_COLS, SSM_STEPS * LANES), bf16),
                        pltpu.VMEM((SSM_STEPS * LANES, 2 * STATE_COLS), bf16)],
        compiler_params=_cparams(2),
        name="ssm",
    )(u, us, h0r, h0i, bbar, c_re, c_im, steps, lev, d)


ROUTER_COLS = LANES


def _mix_body(ap_ref, as_ref, yp_ref, ys_ref, xp_ref, xs_ref, wglu_ref, bglu_ref, gs_ref, wout_ref, gf_ref, wr_ref,
              br_ref, tri_ref, upper_ref, x1_ref, xl_ref, wts_ref, lpos_ref, n_ref, loff_ref, *, prompt_steps):
    is_prompt = pl.program_id(0) < prompt_steps
    pick_rows = lambda p_ref, s_ref: jnp.where(is_prompt, p_ref[...], jnp.concatenate([s_ref[...]] * MIX_TILES, axis=0))
    y = pick_rows(yp_ref, ys_ref)
    y = 0.5 * y * (1.0 + jnp.tanh(math.sqrt(2.0 / math.pi) * (y + 0.044715 * (y * y * y))))
    y = y * jax.nn.sigmoid(_bdot(y, wglu_ref[...]) + bglu_ref[...])
    attn = jnp.where(is_prompt, ap_ref[...].astype(f32),
                     jnp.concatenate([as_ref[...].astype(f32)] * MIX_TILES, axis=0)).astype(bf16)
    cat = jnp.concatenate([attn, _rms(y, gs_ref[...]).astype(bf16)], axis=1)
    x1 = pick_rows(xp_ref, xs_ref) + jnp.dot(cat, wout_ref[...], preferred_element_type=f32)
    x1_ref[...] = x1
    hf = _rms(x1, gf_ref[...])

    logits = _bdot(hf, wr_ref[...]) + br_ref[...]
    le = logits[:, :N_EXPERTS]
    lg = logits[:, N_EXPERTS:N_EXPERTS + N_EXPERT_GROUPS]
    tm = le.shape[0]
    gmax = jnp.max(lg, axis=-1, keepdims=True)
    gi = lax.broadcasted_iota(i32, (tm, N_EXPERT_GROUPS), 1).astype(f32)
    gsel = jnp.min(jnp.where(lg == gmax, gi, float(N_EXPERT_GROUPS)), axis=-1, keepdims=True)
    pg = 1.0 / jnp.sum(jnp.exp(lg - gmax), axis=-1, keepdims=True)
    ei_int = lax.broadcasted_iota(i32, (tm, N_EXPERTS), 1)
    ei = ei_int.astype(f32)
    egroup = jnp.right_shift(ei_int, int(math.log2(EXPERTS_PER_GROUP))).astype(f32)
    lm = jnp.where(egroup == gsel, le, NEG)
    v1 = jnp.max(lm, axis=-1, keepdims=True)
    i1 = jnp.min(jnp.where(lm == v1, ei, float(N_EXPERTS)), axis=-1, keepdims=True)
    lm2 = jnp.where(ei == i1, NEG, lm)
    v2 = jnp.max(lm2, axis=-1, keepdims=True)
    i2 = jnp.min(jnp.where(lm2 == v2, ei, float(N_EXPERTS)), axis=-1, keepdims=True)
    ex = jnp.exp(v2 - v1)
    wts_ref[...] = jnp.concatenate([pg / (1.0 + ex), pg * ex / (1.0 + ex)], axis=1)

    oh1 = (ei == i1).astype(f32)
    oh2 = (ei == i2).astype(f32)
    hf_bf = hf.astype(bf16)
    sorted_row = lax.broadcasted_iota(i32, (ROW_TILE, SLOTS * ROW_TILE), 1).astype(f32)
    for h in range(MIX_TILES):
        rows = slice(h * ROW_TILE, (h + 1) * ROW_TILE)
        both = (oh1[rows] + oh2[rows]).astype(bf16)
        before = jnp.dot(tri_ref[...], both, preferred_element_type=f32)
        count = jnp.sum(oh1[rows] + oh2[rows], axis=0, keepdims=True)
        lower = jnp.sum(jnp.dot(both, upper_ref[...], preferred_element_type=f32), axis=0, keepdims=True)
        lp1 = jnp.sum(oh1[rows] * (before + lower), axis=-1, keepdims=True)
        lp2 = jnp.sum(oh2[rows] * (before + lower), axis=-1, keepdims=True)
        lpos_ref[rows, :] = jnp.concatenate([lp1, lp2], axis=1)
        n_ref[h] = count
        loff_ref[h] = lower
        pick = ((sorted_row == lp1) | (sorted_row == lp2)).astype(bf16)
        xl = lax.dot_general(pick, hf_bf[rows], (((0,), (0,)), ((), ())), preferred_element_type=f32)
        _store_row_major(xl_ref, h * SLOTS * ROW_TILE, xl)


def _mix(a_p, a_s, ys_p, ys_s, x_p, x_s, wglu, bglu, gs, wout, gf, wr, br, tri, upper):
    tm = MIX_TILES * ROW_TILE
    assert x_p.shape[0] % tm == 0 and x_s.shape[0] == ROW_TILE
    prompt_steps = x_p.shape[0] // tm
    t = x_p.shape[0] + x_s.shape[0]
    p_rows = lambda n: pl.BlockSpec((tm, n), lambda i: (jnp.minimum(i, prompt_steps - 1), 0))
    s_rows = lambda n: pl.BlockSpec((ROW_TILE, n), lambda i: (0, 0))
    row = lambda n: pl.BlockSpec((tm, n), lambda i: (i, 0))
    full = lambda arr: pl.BlockSpec(arr.shape, lambda i: (0,) * arr.ndim)
    per_tile = pl.BlockSpec((MIX_TILES, 1, N_EXPERTS), lambda i: (i, 0, 0))
    return pl.pallas_call(
        functools.partial(_mix_body, prompt_steps=prompt_steps),
        grid=(prompt_steps + 1,),
        in_specs=[p_rows(ATTN_WIDTH), s_rows(ATTN_WIDTH), p_rows(SSM_WIDTH), s_rows(SSM_WIDTH), p_rows(D_MODEL),
                  s_rows(D_MODEL), full(wglu), full(bglu), full(gs), full(wout), full(gf), full(wr), full(br),
                  full(tri), full(upper)],
        out_specs=[row(D_MODEL), pl.BlockSpec((SLOTS * tm * PIECES, LANES), lambda i: (i, 0)), row(2), row(2),
                   per_tile, per_tile],
        out_shape=[jax.ShapeDtypeStruct((t, D_MODEL), f32),
                   jax.ShapeDtypeStruct((SLOTS * t * PIECES, LANES), f32),
                   jax.ShapeDtypeStruct((t, 2), f32), jax.ShapeDtypeStruct((t, 2), f32),
                   jax.ShapeDtypeStruct((t // ROW_TILE, 1, N_EXPERTS), f32),
                   jax.ShapeDtypeStruct((t // ROW_TILE, 1, N_EXPERTS), f32)],
        compiler_params=_cparams(),
        name="mix",
    )(a_p, a_s, ys_p, ys_s, x_p, x_s, wglu, bglu, gs, wout, gf, wr, br, tri, upper)


def _store_row_major(ref, first_row, x):
    for c in range(PIECES):
        ref[pl.ds(first_row * PIECES + c, x.shape[0], stride=PIECES), :] = x[:, c * LANES:(c + 1) * LANES]


def _load_row_major(ref, n_rows):
    return jnp.concatenate([ref[pl.ds(c, n_rows, stride=PIECES), :] for c in range(PIECES)], axis=1)


def _copy_rows(src, s_row, dst, d_row, n_rows, sem):
    return pltpu.make_async_copy(src.at[pl.ds(pl.multiple_of(s_row * PIECES, PIECES), n_rows * PIECES), :],
                                 dst.at[pl.ds(pl.multiple_of(d_row * PIECES, PIECES), n_rows * PIECES), :], sem)


def _for_each_piece(n, fn, pieces=RUN_PIECES):
    off = 0 if pieces[0] == RUN_PIECES[0] else n & ~(2 * pieces[0] - 1)
    for piece in pieces:
        @pl.when((n & piece) != 0)
        def _(off=off, piece=piece):
            fn(off, piece)
        off = off + (n & piece)


def _experts_body(te_ref, tpos_ref, tvalid_ref, tlo_ref, thi_ref, wslot_ref, wnext_ref, n_ref, cum_ref, loff_ref,
                  xl_hbm, wg_hbm, wu_hbm, wd_hbm, o_ref, wg_s, wu_s, wd_s, wg_f, wu_f, wd_f, wsem, sem, *xbufs):
    i = pl.program_id(0)
    last = pl.num_programs(0) - 1
    token_tiles = n_ref.shape[0] // N_EXPERTS
    ring = len(xbufs)

    def start_run(t, tau, enabled, buf, buf_sem):
        e, lo = te_ref[t], tpos_ref[t]
        k = tau * N_EXPERTS + e
        s, n = cum_ref[k], n_ref[k]
        a = jnp.maximum(s, lo)
        length = jnp.where(enabled, jnp.maximum(jnp.minimum(s + n, lo + ROW_TILE) - a, 0), 0)
        local = loff_ref[k] + (a - s)
        _for_each_piece(length, lambda off, piece: _copy_rows(
            xl_hbm, tau * (SLOTS * ROW_TILE) + local + off, buf, a - lo + off, piece, buf_sem).start())

    def start_runs_loop(t, first, stop, buf, buf_sem):
        def run(tau, c):
            start_run(t, tau, True, buf, buf_sem)
            return c
        lax.fori_loop(first, stop, run, 0)

    def clear(buf):
        buf[...] = jnp.zeros_like(buf)

    @pl.when(i == 0)
    def _():
        for buf in xbufs:
            clear(buf)
        for t in range(ring - 1):
            @pl.when(tvalid_ref[t] > 0)
            def _(t=t):
                start_runs_loop(t, tlo_ref[t], thi_ref[t] + 1, xbufs[t], sem.at[t])

    valid = tvalid_ref[i]

    def weight_copies(e, slot):
        return [pltpu.make_async_copy(w_hbm.at[e], w_f.at[slot], wsem.at[slot])
                for w_hbm, w_f in ((wg_hbm, wg_f), (wu_hbm, wu_f), (wd_hbm, wd_f))]

    @pl.when((i == 0) & (valid > 0))
    def _():
        for cp in weight_copies(te_ref[0], wslot_ref[0]):
            cp.start()

    @pl.when((valid > 0) & ((i == 0) | (te_ref[i] != te_ref[jnp.maximum(i - 1, 0)])))
    def _():
        slot = wslot_ref[i]
        for cp in weight_copies(te_ref[i], slot):
            cp.wait()
        wg_s[...] = wg_f[slot].astype(bf16)
        wu_s[...] = wu_f[slot].astype(bf16)
        wd_s[...] = wd_f[slot].astype(bf16)

        @pl.when(wnext_ref[i] >= 0)
        def _():
            for cp in weight_copies(wnext_ref[i], 1 - slot):
                cp.start()

    def tile_step(cur):
        ahead = (cur + ring - 1) % ring
        buf, buf_sem, next_buf, next_sem = xbufs[cur], sem.at[cur], xbufs[ahead], sem.at[ahead]
        _for_each_piece(valid, lambda off, piece: _copy_rows(
            xl_hbm, 0, buf, 0, piece, buf_sem).wait())

        nxt = jnp.minimum(i + ring - 1, last)
        go = (i + ring - 1 <= last) & (tvalid_ref[nxt] > 0)
        first, final = tlo_ref[nxt], thi_ref[nxt]
        for j in range(UNROLLED_RUNS):
            start_run(nxt, jnp.minimum(first + j, token_tiles - 1), go & (first + j <= final), next_buf, next_sem)

        x = _load_row_major(buf, ROW_TILE).astype(bf16)
        clear(buf)
        hg = jnp.dot(x, wg_s[...], preferred_element_type=f32)
        hu = jnp.dot(x, wu_s[...], preferred_element_type=f32)
        y = jnp.dot((hg * jax.nn.sigmoid(hg) * hu).astype(bf16), wd_s[...], preferred_element_type=f32)
        _store_row_major(o_ref, 0, y)

        @pl.when(go & (final - first >= UNROLLED_RUNS))
        def _():
            start_runs_loop(nxt, first + UNROLLED_RUNS, final + 1, next_buf, next_sem)

    for cur in range(ring):
        pl.when((valid > 0) & (i % ring == cur))(functools.partial(tile_step, cur))

    @pl.when(valid == 0)
    def _():
        o_ref[...] = jnp.zeros_like(o_ref)


def _experts(tables, xl, wg, wu, wd, tiles):
    hbm = pl.BlockSpec(memory_space=pl.ANY)
    return pl.pallas_call(
        _experts_body,
        grid_spec=pltpu.PrefetchScalarGridSpec(
            num_scalar_prefetch=len(tables),
            grid=(tiles,),
            in_specs=[hbm, hbm, hbm, hbm],
            out_specs=pl.BlockSpec((ROW_TILE * PIECES, LANES), lambda i, *_: (i, 0)),
            scratch_shapes=[pltpu.VMEM(w.shape[1:], bf16) for w in (wg, wu, wd)]
            + [pltpu.VMEM((2,) + w.shape[1:], f32) for w in (wg, wu, wd)]
            + [pltpu.SemaphoreType.DMA((2,)), pltpu.SemaphoreType.DMA((EXPERT_RING,))]
            + [pltpu.VMEM((ROW_TILE * PIECES, LANES), f32)] * EXPERT_RING),
        out_shape=jax.ShapeDtypeStruct((tiles * ROW_TILE * PIECES, LANES), f32),
        compiler_params=_cparams(),
        name="moe_experts",
    )(*tables, xl, wg, wu, wd)


def _combine_body(n_ref, gpos_ref, loff_ref, ys_hbm, x_ref, w_ref, l_ref, g_ref, op_ref, os_ref, sem, *ybufs,
                  prompt_tiles):
    i = pl.program_id(0)
    last = pl.num_programs(0) - 1
    tile_rows = SLOTS * ROW_TILE
    ring = len(ybufs)

    def start_run(t, e, enabled, buf, buf_sem, pieces=RUN_PIECES):
        k = t * N_EXPERTS + e
        _for_each_piece(jnp.where(enabled, n_ref[k], 0), lambda off, piece: _copy_rows(
            ys_hbm, gpos_ref[k] + off, buf, loff_ref[k] + off, piece, buf_sem).start(), pieces)

    @pl.when(i == 0)
    def _():
        for t in range(ring - 1):
            def run(e, c, t=t):
                start_run(t, e, True, ybufs[t], sem.at[t])
                return c
            lax.fori_loop(0, N_EXPERTS, run, 0)

    def step(cur):
        ahead = (cur + ring - 1) % ring
        buf, buf_sem, next_buf, next_sem = ybufs[cur], sem.at[cur], ybufs[ahead], sem.at[ahead]
        _copy_rows(ys_hbm, 0, buf, 0, tile_rows, buf_sem).wait()
        nxt, go = jnp.minimum(i + ring - 1, last), i + ring - 1 <= last
        longest = 0
        for e in range(N_EXPERTS):
            start_run(nxt, e, go, next_buf, next_sem, SHORT_PIECES)
            longest = longest | n_ref[nxt * N_EXPERTS + e]
        yl = _load_row_major(buf, tile_rows).astype(bf16)
        sorted_row = lax.broadcasted_iota(i32, (ROW_TILE, tile_rows), 1).astype(f32)
        w, lp = w_ref[...], l_ref[...]
        y1 = jnp.dot((sorted_row == lp[:, 0:1]).astype(bf16), yl, preferred_element_type=f32)
        y2 = jnp.dot((sorted_row == lp[:, 1:2]).astype(bf16), yl, preferred_element_type=f32)
        out = _rms(x_ref[...] + (w[:, 0:1] * y1 + w[:, 1:2] * y2), g_ref[...])

        @pl.when(i < prompt_tiles)
        def _():
            op_ref[...] = out

        @pl.when(i >= prompt_tiles)
        def _():
            os_ref[...] = out

        @pl.when(go & (longest >= 2 * SHORT_PIECES[0]))
        def _():
            def long_pieces(e, c):
                start_run(nxt, e, True, next_buf, next_sem, LONG_PIECES)
                return c
            lax.fori_loop(0, N_EXPERTS, long_pieces, 0)

    for cur in range(ring):
        pl.when(i % ring == cur)(functools.partial(step, cur))


def _combine(tables, ys, x1, wts, lpos, g, prompt_rows):
    prompt_tiles = prompt_rows // ROW_TILE
    tiles = x1.shape[0] // ROW_TILE
    row = lambda n: pl.BlockSpec((ROW_TILE, n), lambda i, *_: (i, 0))
    p_rows = lambda n: pl.BlockSpec((ROW_TILE, n), lambda i, *_: (jnp.minimum(i, prompt_tiles - 1), 0))
    s_rows = lambda n: pl.BlockSpec((ROW_TILE, n), lambda i, *_: (jnp.maximum(i - prompt_tiles, 0), 0))
    return pl.pallas_call(
        functools.partial(_combine_body, prompt_tiles=prompt_tiles),
        grid_spec=pltpu.PrefetchScalarGridSpec(
            num_scalar_prefetch=len(tables),
            grid=(tiles,),
            in_specs=[pl.BlockSpec(memory_space=pl.ANY), row(D_MODEL), row(2), row(2),
                      pl.BlockSpec((1, D_MODEL), lambda i, *_: (0, 0))],
            out_specs=[p_rows(D_MODEL), s_rows(D_MODEL)],
            scratch_shapes=[pltpu.SemaphoreType.DMA((COMBINE_RING,))]
            + [pltpu.VMEM((SLOTS * ROW_TILE * PIECES, LANES), f32)] * COMBINE_RING),
        out_shape=[jax.ShapeDtypeStruct((prompt_rows, D_MODEL), f32),
                   jax.ShapeDtypeStruct((x1.shape[0] - prompt_rows, D_MODEL), f32)],
        compiler_params=_cparams(),
        name="moe_combine",
    )(*tables, ys, x1, wts, lpos, g)


def _moe_tables(n, loff, tiles):
    cum = jnp.cumsum(n, axis=0) - n
    counts = jnp.sum(n, axis=0)
    padded = (counts + ROW_TILE - 1) // ROW_TILE * ROW_TILE
    ends = jnp.cumsum(padded)
    starts = ends - padded
    first = jnp.arange(tiles, dtype=i32) * ROW_TILE
    expert = jnp.minimum(jnp.sum((first[:, None] >= ends[None, :]).astype(i32), axis=1), N_EXPERTS - 1)
    sel = expert[:, None] == jnp.arange(N_EXPERTS)[None, :]
    pick = lambda v: jnp.sum(jnp.where(sel, v[None, :], 0), axis=1)
    pos = first - pick(starts)
    valid = jnp.where(first < ends[-1], jnp.clip(pick(counts) - pos, 0, ROW_TILE), 0)
    cum_t, n_t = cum.T[expert], n.T[expert]
    touches = (cum_t + n_t > pos[:, None]) & (cum_t < (pos + ROW_TILE)[:, None]) & (n_t > 0)
    tau = jnp.arange(n.shape[0], dtype=i32)[None, :]
    lo = jnp.min(jnp.where(touches, tau, n.shape[0]), axis=1)
    hi = jnp.max(jnp.where(touches, tau, -1), axis=1)
    ids = jnp.arange(N_EXPERTS)
    busy = counts > 0
    ordinal = jnp.sum(busy[None, :] & (ids[None, :] < ids[:, None]), axis=1)
    following = jnp.min(jnp.where(busy[None, :] & (ids[None, :] > ids[:, None]), ids[None, :], N_EXPERTS), axis=1)
    following = jnp.where(following < N_EXPERTS, following, -1)
    as_i32 = lambda v: v.astype(i32)
    flat = lambda v: v.reshape(-1).astype(i32)
    expert_tables = (tuple(map(as_i32, (expert, pos, valid, lo, hi, pick(ordinal) % 2, pick(following))))
                     + (flat(n), flat(cum), flat(loff)))
    combine_tables = (flat(n), flat(starts[None, :] + cum), flat(loff))
    return expert_tables, combine_tables


def kernel(x_prompt, x_sample, cache_k, cache_v, state_ssm_re, state_ssm_im, g_norm_mix, w_in, attn_sinks, ssm_a_re,
           ssm_a_im, ssm_log_dt, ssm_b_re, ssm_b_im, ssm_c_re, ssm_c_im, ssm_d, w_glu, b_glu, g_attn_out, g_ssm_out,
           w_out, g_norm_ffn, w_router_group, b_router_group, w_router_expert, b_router_expert, w_exp_gate, w_exp_up,
           w_exp_down, g_final):
    bp, lp, _ = x_prompt.shape
    bs, ls, _ = x_sample.shape
    depth = w_in.shape[0]
    assert depth == 1 and ls == SSM_STEPS and lp % ATTN_TILE == 0 and (bs * ls) % ROW_TILE == 0
    tp, ts = bp * lp, bs * ls
    wc = cache_k.shape[2]
    row2 = lambda v: v.reshape(1, -1)

    xp = x_prompt.reshape(tp, D_MODEL)
    xs = x_sample.reshape(ts, D_MODEL)
    w_in_bf = w_in[0].astype(bf16)
    qp, kp, vp, up = _proj(xp, row2(g_norm_mix[0]), w_in_bf, 1024)
    qs, kq, vq, us = _proj(xs, row2(g_norm_mix[0]), w_in_bf, ts)

    sinks = attn_sinks[0]
    g_att = row2(g_attn_out[0])
    ap = _attn_prompt(qp.reshape(bp, lp, -1), kp.reshape(bp, lp, -1), vp.reshape(bp, lp, -1), sinks, g_att)
    a_s, k_roll, v_roll = _attn_sample(qs.reshape(bs, ls, -1), kq.reshape(bs, ls, -1), vq.reshape(bs, ls, -1),
                                       cache_k[0].reshape(bs, wc, KV_WIDTH), cache_v[0].reshape(bs, wc, KV_WIDTH),
                                       sinks, g_att)

    bbar, steps, lev = _ssm_tables(ssm_a_re[0], ssm_a_im[0], ssm_log_dt[0], ssm_b_re[0], ssm_b_im[0])
    c_blocks = lambda c: c.reshape(SSM_BLOCKS, LANES, SSM_STATE)
    d_row = row2(ssm_d[0])
    to_blocks = lambda h: h.reshape(bs, SSM_GROUPS * SSM_STATE)
    from_blocks = lambda h: h.reshape(bs, SSM_GROUPS, SSM_STATE)
    yp, hrp, hip, ysm, hrs, his = _ssm(up.reshape(bp, lp, -1), us.reshape(1, ts, -1), to_blocks(state_ssm_re[0]),
                                       to_blocks(state_ssm_im[0]), bbar, c_blocks(ssm_c_re[0]),
                                       c_blocks(ssm_c_im[0]), steps, lev, d_row)

    wr = jnp.zeros((D_MODEL, ROUTER_COLS), f32)
    wr = wr.at[:, :N_EXPERTS].set(w_router_expert[0]).at[:, N_EXPERTS:N_EXPERTS + N_EXPERT_GROUPS].set(w_router_group[0])
    br = jnp.zeros((1, ROUTER_COLS), f32)
    br = br.at[0, :N_EXPERTS].set(b_router_expert[0]).at[0, N_EXPERTS:N_EXPERTS + N_EXPERT_GROUPS].set(b_router_group[0])
    tri = jnp.tril(jnp.ones((ROW_TILE, ROW_TILE), bf16), -1)
    upper = jnp.triu(jnp.ones((N_EXPERTS, N_EXPERTS), bf16), 1)
    mix_w = (w_glu[0].astype(bf16), row2(b_glu[0]), row2(g_ssm_out[0]), w_out[0].astype(bf16), row2(g_norm_ffn[0]),
             wr.astype(bf16), br, tri, upper)
    x1, xl, wts, lpos, n_rows, n_off = _mix(ap.reshape(tp, -1), a_s.reshape(ts, -1), yp.reshape(tp, -1),
                                            ysm.reshape(ts, -1), xp, xs, *mix_w)

    per_tile = lambda v: v.reshape(-1, N_EXPERTS).astype(i32)
    tiles = (SLOTS * (tp + ts)) // ROW_TILE + N_EXPERTS
    expert_tables, combine_tables = _moe_tables(per_tile(n_rows), per_tile(n_off), tiles)
    expert_out = _experts(expert_tables, xl, w_exp_gate[0], w_exp_up[0], w_exp_down[0], tiles)
    y_p, y_s = _combine(combine_tables, expert_out, x1, wts, lpos, row2(g_final), tp)

    kvshape = lambda a, b: a.reshape(1, b, -1, N_KV_HEADS, HEAD_DIM)
    block_state = lambda h: h.reshape(bp, SSM_GROUPS, SSM_STATE)[None]
    wcp = min(WINDOW, lp)
    return (y_p.reshape(bp, lp, D_MODEL), y_s.reshape(bs, ls, D_MODEL),
            kvshape(kp.reshape(bp, lp, -1)[:, lp - wcp:], bp), kvshape(vp.reshape(bp, lp, -1)[:, lp - wcp:], bp),
            block_state(hrp), block_state(hip),
            kvshape(k_roll, bs), kvshape(v_roll, bs),
            from_blocks(hrs)[None], from_blocks(his)[None])
```

```python
import functools
import math

import jax
import jax.numpy as jnp
from jax import lax
from jax.experimental import pallas as pl
from jax.experimental.pallas import tpu as pltpu

f32, bf16, i32 = jnp.float32, jnp.bfloat16, jnp.int32

D_MODEL = 1024
CHUNK = 64
N_BACK = 2
WINDOW = 128
ATTN_WIDTH = 512
HEAD_DIM = 64
N_KV_HEADS = 2
Q_PER_KV = 4
KV_WIDTH = 128
SSM_WIDTH = 512
SSM_GROUP = 16
SSM_GROUPS = 32
SSM_STATE = 64
N_EXPERT_GROUPS = 4
EXPERTS_PER_GROUP = 8
N_EXPERTS = 32
EPS = 1e-6
NEG = -1e30

LANES = 128
SSM_STEPS = 16
SSM_SEQS = 2
SSM_BLOCKS = SSM_WIDTH // LANES
GROUPS_PER_BLOCK = LANES // SSM_GROUP
STATE_COLS = GROUPS_PER_BLOCK * SSM_STATE
ROW_TILE = 256
SLOTS = 2
PIECES = D_MODEL // LANES
RUN_PIECES = tuple(1 << b for b in reversed(range(int(math.log2(ROW_TILE)) + 1)))
SHORT_PIECES = tuple(p for p in RUN_PIECES if p <= 32)
LONG_PIECES = tuple(p for p in RUN_PIECES if p > 32)
MIX_TILES = 2
UNROLLED_RUNS = 20
EXPERT_RING = 3
COMBINE_RING = 3
VMEM_LIMIT = 56 * 1024 * 1024


def _cparams(n_axes=1, limit=VMEM_LIMIT):
    return pltpu.CompilerParams(dimension_semantics=("arbitrary",) * n_axes, vmem_limit_bytes=limit)


def _rms(x, g):
    return x * lax.rsqrt(jnp.mean(x * x, axis=-1, keepdims=True) + EPS) * g


def _bdot(a, b):
    return jnp.dot(a.astype(bf16), b.astype(bf16), preferred_element_type=f32)


def _proj_body(x_ref, g_ref, w_ref, q_ref, k_ref, v_ref, u_ref):
    h = _rms(x_ref[...], g_ref[...])
    z = _bdot(h, w_ref[...])
    q_ref[...] = z[:, :ATTN_WIDTH] * (HEAD_DIM ** -0.5)
    k_ref[...] = z[:, ATTN_WIDTH:ATTN_WIDTH + KV_WIDTH]
    v_ref[...] = z[:, ATTN_WIDTH + KV_WIDTH:ATTN_WIDTH + 2 * KV_WIDTH]
    u_ref[...] = z[:, ATTN_WIDTH + 2 * KV_WIDTH:]


def _proj(x2d, g, w_bf, tm):
    t = x2d.shape[0]
    row = lambda n: pl.BlockSpec((tm, n), lambda i: (i, 0))
    full = lambda a: pl.BlockSpec(a.shape, lambda i: (0,) * a.ndim)
    return pl.pallas_call(
        _proj_body,
        grid=(t // tm,),
        in_specs=[row(D_MODEL), full(g), full(w_bf)],
        out_specs=[row(ATTN_WIDTH), row(KV_WIDTH), row(KV_WIDTH), row(SSM_WIDTH)],
        out_shape=[jax.ShapeDtypeStruct((t, n), f32) for n in (ATTN_WIDTH, KV_WIDTH, KV_WIDTH, SSM_WIDTH)],
        compiler_params=_cparams(),
        name="proj",
    )(x2d, g, w_bf)


def _sink_column(sink_ref, kv, rows_per_head):
    r = lax.broadcasted_iota(i32, (Q_PER_KV * rows_per_head, 1), 0)
    col = jnp.full((Q_PER_KV * rows_per_head, 1), sink_ref[kv * Q_PER_KV], f32)
    for j in range(1, Q_PER_KV):
        col = jnp.where(r >= j * rows_per_head, sink_ref[kv * Q_PER_KV + j], col)
    return col


def _attend(qs, kc, vc, sink_col, valid):
    s = lax.dot_general(qs.astype(bf16), kc.astype(bf16), (((1,), (1,)), ((), ())), preferred_element_type=f32)
    if valid is not None:
        s = jnp.where(valid, s, NEG)
    m = jnp.maximum(jnp.max(s, axis=-1, keepdims=True), sink_col)
    p = jnp.exp(s - m)
    denom = jnp.sum(p, axis=-1, keepdims=True) + jnp.exp(sink_col - m)
    return _bdot(p, vc) / denom


def _heads_attend(q, k, v, sink_ref, valid):
    rows = q.shape[0]
    pieces = []
    for kv in range(N_KV_HEADS):
        qs = jnp.concatenate(
            [q[:, (kv * Q_PER_KV + j) * HEAD_DIM:(kv * Q_PER_KV + j + 1) * HEAD_DIM] for j in range(Q_PER_KV)], axis=0)
        o = _attend(qs, k[:, kv * HEAD_DIM:(kv + 1) * HEAD_DIM], v[:, kv * HEAD_DIM:(kv + 1) * HEAD_DIM],
                    _sink_column(sink_ref, kv, rows), valid)
        pieces += [o[j * rows:(j + 1) * rows] for j in range(Q_PER_KV)]
    return jnp.concatenate(pieces, axis=1)


ATTN_TILE = 256
CHUNKS_PER_TILE = ATTN_TILE // CHUNK
KEY_SPAN = (N_BACK + 1) * CHUNK


def _attn_prompt_body(sink_ref, q_ref, kp_ref, kc_ref, vp_ref, vc_ref, g_ref, o_ref):
    i = pl.program_id(1)
    kwin = jnp.concatenate([kp_ref[0], kc_ref[0]], axis=0)
    vwin = jnp.concatenate([vp_ref[0], vc_ref[0]], axis=0)
    key_chunk = lax.broadcasted_iota(i32, (1, KEY_SPAN), 1) // CHUNK
    for c in range(CHUNKS_PER_TILE):
        valid = (i * CHUNKS_PER_TILE + c - N_BACK + key_chunk) >= 0
        o = _heads_attend(q_ref[0, c * CHUNK:(c + 1) * CHUNK, :], kwin[c * CHUNK:c * CHUNK + KEY_SPAN],
                          vwin[c * CHUNK:c * CHUNK + KEY_SPAN], sink_ref, valid)
        o_ref[0, c * CHUNK:(c + 1) * CHUNK, :] = _rms(o, g_ref[...]).astype(bf16)


def _attn_prompt(q, k, v, sinks, g):
    b, l, _ = q.shape
    back = N_BACK * CHUNK
    per = ATTN_TILE // back
    prev = pl.BlockSpec((1, back, KV_WIDTH), lambda bi, i: (bi, jnp.maximum(i * per - 1, 0), 0))
    cur = pl.BlockSpec((1, ATTN_TILE, KV_WIDTH), lambda bi, i: (bi, i, 0))
    return pl.pallas_call(
        _attn_prompt_body,
        grid=(b, l // ATTN_TILE),
        in_specs=[pl.BlockSpec(memory_space=pltpu.SMEM),
                  pl.BlockSpec((1, ATTN_TILE, ATTN_WIDTH), lambda bi, i: (bi, i, 0)),
                  prev, cur, prev, cur,
                  pl.BlockSpec((1, ATTN_WIDTH), lambda bi, i: (0, 0))],
        out_specs=pl.BlockSpec((1, ATTN_TILE, ATTN_WIDTH), lambda bi, i: (bi, i, 0)),
        out_shape=jax.ShapeDtypeStruct((b, l, ATTN_WIDTH), bf16),
        compiler_params=_cparams(2),
        name="attn_prompt",
    )(sinks, q, k, k, v, v, g)


SAMPLE_STREAMS = 16


def _attn_sample_body(sink_ref, q_ref, kn_ref, vn_ref, ck_ref, cv_ref, g_ref, o_ref, nk_ref, nv_ref):
    n_new = kn_ref.shape[1]
    for b in range(q_ref.shape[0]):
        kall = jnp.concatenate([ck_ref[b], kn_ref[b]], axis=0)
        vall = jnp.concatenate([cv_ref[b], vn_ref[b]], axis=0)
        o = _heads_attend(q_ref[b], kall, vall, sink_ref, None)
        o_ref[b] = _rms(o, g_ref[...]).astype(bf16)
        nk_ref[b] = kall[n_new:]
        nv_ref[b] = vall[n_new:]


def _attn_sample(q, k_new, v_new, cache_k, cache_v, sinks, g):
    b, l, _ = q.shape
    wc = cache_k.shape[1]
    assert b % SAMPLE_STREAMS == 0
    blk = lambda r, n: pl.BlockSpec((SAMPLE_STREAMS, r, n), lambda bi: (bi, 0, 0))
    return pl.pallas_call(
        _attn_sample_body,
        grid=(b // SAMPLE_STREAMS,),
        in_specs=[pl.BlockSpec(memory_space=pltpu.SMEM), blk(l, ATTN_WIDTH), blk(l, KV_WIDTH), blk(l, KV_WIDTH),
                  blk(wc, KV_WIDTH), blk(wc, KV_WIDTH), pl.BlockSpec((1, ATTN_WIDTH), lambda bi: (0, 0))],
        out_specs=[blk(l, ATTN_WIDTH), blk(wc, KV_WIDTH), blk(wc, KV_WIDTH)],
        out_shape=[jax.ShapeDtypeStruct((b, l, ATTN_WIDTH), bf16),
                   jax.ShapeDtypeStruct((b, wc, KV_WIDTH), f32), jax.ShapeDtypeStruct((b, wc, KV_WIDTH), f32)],
        compiler_params=_cparams(),
        name="attn_sample",
    )(sinks, q, k_new, v_new, cache_k, cache_v, g)


def _ssm_tables(a_re, a_im, log_dt, b_re, b_im):
    dt = jnp.exp(log_dt)[:, None]
    lam_r, lam_i = a_re * dt, a_im * dt

    def power(n):
        mag = jnp.exp(n * lam_r)
        return jnp.stack([mag * jnp.cos(n * lam_i), mag * jnp.sin(n * lam_i)])

    ar, ai = power(1.0)
    den = a_re * a_re + a_im * a_im
    nr, ni = ar - 1.0, ai
    fr = ((nr * a_re + ni * a_im) / den)[..., None]
    fi = ((ni * a_re - nr * a_im) / den)[..., None]
    bbar = jnp.stack([fr * b_re - fi * b_im, fr * b_im + fi * b_re])
    bbar = bbar.transpose(0, 1, 3, 2).reshape(2, SSM_BLOCKS, LANES, SSM_STATE)
    by_block = lambda t: t.reshape(t.shape[:-2] + (SSM_BLOCKS, STATE_COLS))
    steps = by_block(power(jnp.arange(SSM_STEPS + 1, dtype=f32)[:, None, None]))
    levels = by_block(power(SSM_STEPS * 2.0 ** jnp.arange(8, dtype=f32)[:, None, None]))
    return bbar, steps.transpose(2, 0, 1, 3), levels.transpose(2, 1, 0, 3)


def _ssm_chunk_rows(u_ref, nk):
    xs = [u_ref[0, pl.ds(s, nk, stride=SSM_STEPS), :] for s in range(SSM_STEPS)]
    pairs = [jnp.concatenate([xs[2 * p], xs[2 * p + 1]], axis=1).astype(bf16) for p in range(SSM_STEPS // 2)]
    return xs, pairs


def _ssm_intra(pairs, toep_ref, nk):
    nd = len(pairs)
    y = [None] * nd
    stack = nk < LANES
    for d in range(nd):
        if stack:
            r = jnp.dot(jnp.concatenate(pairs[:nd - d], axis=0), toep_ref[d], preferred_element_type=f32)
        for p in range(nd - d):
            blk = r[p * nk:(p + 1) * nk] if stack else jnp.dot(pairs[p], toep_ref[d], preferred_element_type=f32)
            y[p + d] = blk if y[p + d] is None else y[p + d] + blk
    return y


def _shift_rows(x, sh):
    rows = lax.broadcasted_iota(i32, (x.shape[0], 1), 0)
    return jnp.where(rows >= sh, pltpu.roll(x, sh, axis=0), 0.0)


def _group_index(shape, axis, sub):
    idx = lax.broadcasted_iota(i32, shape, axis)
    return jnp.right_shift(idx, int(math.log2(sub))) & (GROUPS_PER_BLOCK - 1)


def _build_ssm_weights(bb_ref, cr_ref, ci_ref, pw_ref, toep_s, wout_s, win_s, w_low):
    hp = lax.Precision.HIGHEST
    k = lax.broadcasted_iota(i32, (SSM_STATE, STATE_COLS), 0)
    n = lax.broadcasted_iota(i32, (SSM_STATE, STATE_COLS), 1)
    spread = ((n & (SSM_STATE - 1)) == k).astype(f32)
    own = _group_index((LANES, STATE_COLS), 0, SSM_GROUP) == _group_index((LANES, STATE_COLS), 1, SSM_STATE)

    def block_diag(compact):
        return jnp.where(own, jnp.dot(compact, spread, precision=hp, preferred_element_type=f32), 0.0)

    bb_r, bb_i = block_diag(bb_ref[0, 0]), block_diag(bb_ref[1, 0])
    c_r, c_i = block_diag(cr_ref[0]), block_diag(ci_ref[0])
    c_stack = jnp.concatenate([c_r.T, -c_i.T], axis=0)
    power = lambda e: (pw_ref[0, 0, e:e + 1, :], pw_ref[0, 1, e:e + 1, :])
    for s in range(SSM_STEPS):
        rows = slice(s * LANES, (s + 1) * LANES)
        pr, pi = power(SSM_STEPS - 1 - s)
        w = jnp.concatenate([bb_r * pr - bb_i * pi, bb_r * pi + bb_i * pr], axis=1)
        wout_s[rows, :] = w.astype(bf16)
        w_low[rows, :] = (w - w.astype(bf16).astype(f32)).astype(bf16)
        pr, pi = power(s + 1)
        g = jnp.concatenate([c_r * pr - c_i * pi, -(c_r * pi + c_i * pr)], axis=1)
        win_s[:, rows] = g.T.astype(bf16)
    c_high = c_stack.astype(bf16)
    c_low = (c_stack - c_high.astype(f32)).astype(bf16)
    lags = (jnp.dot(wout_s[...], c_high, preferred_element_type=f32)
            + (jnp.dot(wout_s[...], c_low, preferred_element_type=f32)
               + jnp.dot(w_low[...], c_high, preferred_element_type=f32)))
    lag_kernel = [lags[(SSM_STEPS - 1 - lag) * LANES:(SSM_STEPS - lag) * LANES] for lag in range(SSM_STEPS)]
    zero = jnp.zeros((LANES, LANES), f32)
    for d in range(SSM_STEPS // 2):
        top = jnp.concatenate([lag_kernel[2 * d], lag_kernel[2 * d + 1]], axis=1)
        bottom = jnp.concatenate([lag_kernel[2 * d - 1] if d > 0 else zero, lag_kernel[2 * d]], axis=1)
        toep_s[d] = jnp.concatenate([top, bottom], axis=0).astype(bf16)


def _ssm_chunks(u_ref, d_ref, y_ref, toep_s, wout_s, win_s, entry_state):
    nk = u_ref.shape[1] // SSM_STEPS
    xs, pairs = _ssm_chunk_rows(u_ref, nk)
    y = _ssm_intra(pairs, toep_s, nk)
    s = jnp.dot(jnp.concatenate(pairs, axis=1), wout_s[...], preferred_element_type=f32)
    hprev, hr, hi = entry_state(s[:, :STATE_COLS], s[:, STATE_COLS:])
    y2 = _bdot(hprev, win_s[...])
    for st in range(SSM_STEPS):
        piece = (y[st // 2][:, (st % 2) * LANES:(st % 2 + 1) * LANES] + y2[:, st * LANES:(st + 1) * LANES]
                 + d_ref[...] * xs[st])
        y_ref[0, pl.ds(st, nk, stride=SSM_STEPS), :] = piece
    return hr, hi


def _ssm_body(u_ref, us_ref, h0r_ref, h0i_ref, bb_ref, cr_ref, ci_ref, pw_ref, lev_ref, d_ref,
              y_ref, hr_ref, hi_ref, ys_ref, hrs_ref, his_ref, toep_s, wout_s, win_s, w_low):
    @pl.when(pl.program_id(1) == 0)
    def _():
        _build_ssm_weights(bb_ref, cr_ref, ci_ref, pw_ref, toep_s, wout_s, win_s, w_low)

        def one_chunk(sr, si):
            h0r, h0i = h0r_ref[...], h0i_ref[...]
            ar, ai = lev_ref[0, 0, 0:1, :], lev_ref[0, 0, 1:2, :]
            return (jnp.concatenate([h0r, h0i], axis=1), sr + ar * h0r - ai * h0i, si + ar * h0i + ai * h0r)
        hrs_ref[...], his_ref[...] = _ssm_chunks(us_ref, d_ref, ys_ref, toep_s, wout_s, win_s, one_chunk)

    def scan_chunks(sr, si):
        nk = sr.shape[0]
        level = 0
        while (1 << level) < nk:
            ar, ai = lev_ref[0, level, 0:1, :], lev_ref[0, level, 1:2, :]
            tr, ti = _shift_rows(sr, 1 << level), _shift_rows(si, 1 << level)
            sr, si = sr + ar * tr - ai * ti, si + ar * ti + ai * tr
            level += 1
        return (jnp.concatenate([_shift_rows(sr, 1), _shift_rows(si, 1)], axis=1), sr[nk - 1:nk], si[nk - 1:nk])
    for b in range(u_ref.shape[0]):
        hr_ref[b, 0], hi_ref[b, 0] = _ssm_chunks(u_ref.at[pl.ds(b, 1)], d_ref, y_ref.at[pl.ds(b, 1)], toep_s, wout_s,
                                                 win_s, scan_chunks)


def _ssm(u, us, h0r, h0i, bbar, c_re, c_im, steps, lev, d):
    b, l, _ = u.shape
    rows = us.shape[1]
    nb = rows // SSM_STEPS
    wspec = lambda a: pl.BlockSpec((1,) + a.shape[1:], lambda j, bi: (j,) + (0,) * (a.ndim - 1))
    assert b % SSM_SEQS == 0
    st = pl.BlockSpec((SSM_SEQS, 1, 1, STATE_COLS), lambda j, bi: (bi, j, 0, 0))
    sst = pl.BlockSpec((nb, STATE_COLS), lambda j, bi: (0, j))
    seq = pl.BlockSpec((SSM_SEQS, l, LANES), lambda j, bi: (bi, 0, j))
    sseq = pl.BlockSpec((1, rows, LANES), lambda j, bi: (0, 0, j))
    return pl.pallas_call(
        _ssm_body,
        grid=(SSM_BLOCKS, b // SSM_SEQS),
        in_specs=[seq, sseq, sst, sst, pl.BlockSpec((2, 1, LANES, SSM_STATE), lambda j, bi: (0, j, 0, 0)),
                  wspec(c_re), wspec(c_im), wspec(steps), wspec(lev), pl.BlockSpec((1, LANES), lambda j, bi: (0, j))],
        out_specs=[seq, st, st, sseq, sst, sst],
        out_shape=[jax.ShapeDtypeStruct((b, l, SSM_WIDTH), f32),
                   jax.ShapeDtypeStruct((b, SSM_BLOCKS, 1, STATE_COLS), f32),
                   jax.ShapeDtypeStruct((b, SSM_BLOCKS, 1, STATE_COLS), f32),
                   jax.ShapeDtypeStruct((1, rows, SSM_WIDTH), f32),
                   jax.ShapeDtypeStruct((nb, SSM_BLOCKS * STATE_COLS), f32),
                   jax.ShapeDtypeStruct((nb, SSM_BLOCKS * STATE_COLS), f32)],
        scratch_shapes=[pltpu.VMEM((SSM_STEPS // 2, 2 * LANES, 2 * LANES), bf16),
                        pltpu.VMEM((SSM_STEPS * LANES, 2 * STATE_COLS), bf16),
                        pltpu.VMEM((2 * STATE_COLS, SSM_STEPS * LANES), bf16),
                        pltpu.VMEM((SSM_STEPS * LANES, 2 * STATE_COLS), bf16)],
        compiler_params=_cparams(2),
        name="ssm",
    )(u, us, h0r, h0i, bbar, c_re, c_im, steps, lev, d)


ROUTER_COLS = LANES


def _mix_body(ap_ref, as_ref, yp_ref, ys_ref, xp_ref, xs_ref, wglu_ref, bglu_ref, gs_ref, wout_ref, gf_ref, wr_ref,
              br_ref, tri_ref, upper_ref, x1_ref, xl_ref, wts_ref, lpos_ref, n_ref, loff_ref, *, prompt_steps):
    is_prompt = pl.program_id(0) < prompt_steps
    pick_rows = lambda p_ref, s_ref: jnp.where(is_prompt, p_ref[...], jnp.concatenate([s_ref[...]] * MIX_TILES, axis=0))
    y = pick_rows(yp_ref, ys_ref)
    y = 0.5 * y * (1.0 + jnp.tanh(math.sqrt(2.0 / math.pi) * (y + 0.044715 * (y * y * y))))
    y = y * jax.nn.sigmoid(_bdot(y, wglu_ref[...]) + bglu_ref[...])
    attn = jnp.where(is_prompt, ap_ref[...].astype(f32),
                     jnp.concatenate([as_ref[...].astype(f32)] * MIX_TILES, axis=0)).astype(bf16)
    cat = jnp.concatenate([attn, _rms(y, gs_ref[...]).astype(bf16)], axis=1)
    x1 = pick_rows(xp_ref, xs_ref) + jnp.dot(cat, wout_ref[...], preferred_element_type=f32)
    x1_ref[...] = x1
    hf = _rms(x1, gf_ref[...])

    logits = _bdot(hf, wr_ref[...]) + br_ref[...]
    le = logits[:, :N_EXPERTS]
    lg = logits[:, N_EXPERTS:N_EXPERTS + N_EXPERT_GROUPS]
    tm = le.shape[0]
    gmax = jnp.max(lg, axis=-1, keepdims=True)
    gi = lax.broadcasted_iota(i32, (tm, N_EXPERT_GROUPS), 1).astype(f32)
    gsel = jnp.min(jnp.where(lg == gmax, gi, float(N_EXPERT_GROUPS)), axis=-1, keepdims=True)
    pg = 1.0 / jnp.sum(jnp.exp(lg - gmax), axis=-1, keepdims=True)
    ei_int = lax.broadcasted_iota(i32, (tm, N_EXPERTS), 1)
    ei = ei_int.astype(f32)
    egroup = jnp.right_shift(ei_int, int(math.log2(EXPERTS_PER_GROUP))).astype(f32)
    lm = jnp.where(egroup == gsel, le, NEG)
    v1 = jnp.max(lm, axis=-1, keepdims=True)
    i1 = jnp.min(jnp.where(lm == v1, ei, float(N_EXPERTS)), axis=-1, keepdims=True)
    lm2 = jnp.where(ei == i1, NEG, lm)
    v2 = jnp.max(lm2, axis=-1, keepdims=True)
    i2 = jnp.min(jnp.where(lm2 == v2, ei, float(N_EXPERTS)), axis=-1, keepdims=True)
    ex = jnp.exp(v2 - v1)
    wts_ref[...] = jnp.concatenate([pg / (1.0 + ex), pg * ex / (1.0 + ex)], axis=1)

    oh1 = (ei == i1).astype(f32)
    oh2 = (ei == i2).astype(f32)
    hf_bf = hf.astype(bf16)
    sorted_row = lax.broadcasted_iota(i32, (ROW_TILE, SLOTS * ROW_TILE), 1).astype(f32)
    for h in range(MIX_TILES):
        rows = slice(h * ROW_TILE, (h + 1) * ROW_TILE)
        both = (oh1[rows] + oh2[rows]).astype(bf16)
        before = jnp.dot(tri_ref[...], both, preferred_element_type=f32)
        count = jnp.sum(oh1[rows] + oh2[rows], axis=0, keepdims=True)
        lower = jnp.sum(jnp.dot(both, upper_ref[...], preferred_element_type=f32), axis=0, keepdims=True)
        lp1 = jnp.sum(oh1[rows] * (before + lower), axis=-1, keepdims=True)
        lp2 = jnp.sum(oh2[rows] * (before + lower), axis=-1, keepdims=True)
        lpos_ref[rows, :] = jnp.concatenate([lp1, lp2], axis=1)
        n_ref[h] = count
        loff_ref[h] = lower
        pick = ((sorted_row == lp1) | (sorted_row == lp2)).astype(bf16)
        xl = lax.dot_general(pick, hf_bf[rows], (((0,), (0,)), ((), ())), preferred_element_type=f32)
        _store_row_major(xl_ref, h * SLOTS * ROW_TILE, xl)


def _mix(a_p, a_s, ys_p, ys_s, x_p, x_s, wglu, bglu, gs, wout, gf, wr, br, tri, upper):
    tm = MIX_TILES * ROW_TILE
    assert x_p.shape[0] % tm == 0 and x_s.shape[0] == ROW_TILE
    prompt_steps = x_p.shape[0] // tm
    t = x_p.shape[0] + x_s.shape[0]
    p_rows = lambda n: pl.BlockSpec((tm, n), lambda i: (jnp.minimum(i, prompt_steps - 1), 0))
    s_rows = lambda n: pl.BlockSpec((ROW_TILE, n), lambda i: (0, 0))
    row = lambda n: pl.BlockSpec((tm, n), lambda i: (i, 0))
    full = lambda arr: pl.BlockSpec(arr.shape, lambda i: (0,) * arr.ndim)
    per_tile = pl.BlockSpec((MIX_TILES, 1, N_EXPERTS), lambda i: (i, 0, 0))
    return pl.pallas_call(
        functools.partial(_mix_body, prompt_steps=prompt_steps),
        grid=(prompt_steps + 1,),
        in_specs=[p_rows(ATTN_WIDTH), s_rows(ATTN_WIDTH), p_rows(SSM_WIDTH), s_rows(SSM_WIDTH), p_rows(D_MODEL),
                  s_rows(D_MODEL), full(wglu), full(bglu), full(gs), full(wout), full(gf), full(wr), full(br),
                  full(tri), full(upper)],
        out_specs=[row(D_MODEL), pl.BlockSpec((SLOTS * tm * PIECES, LANES), lambda i: (i, 0)), row(2), row(2),
                   per_tile, per_tile],
        out_shape=[jax.ShapeDtypeStruct((t, D_MODEL), f32),
                   jax.ShapeDtypeStruct((SLOTS * t * PIECES, LANES), f32),
                   jax.ShapeDtypeStruct((t, 2), f32), jax.ShapeDtypeStruct((t, 2), f32),
                   jax.ShapeDtypeStruct((t // ROW_TILE, 1, N_EXPERTS), f32),
                   jax.ShapeDtypeStruct((t // ROW_TILE, 1, N_EXPERTS), f32)],
        compiler_params=_cparams(),
        name="mix",
    )(a_p, a_s, ys_p, ys_s, x_p, x_s, wglu, bglu, gs, wout, gf, wr, br, tri, upper)


def _store_row_major(ref, first_row, x):
    for c in range(PIECES):
        ref[pl.ds(first_row * PIECES + c, x.shape[0], stride=PIECES), :] = x[:, c * LANES:(c + 1) * LANES]


def _load_row_major(ref, n_rows):
    return jnp.concatenate([ref[pl.ds(c, n_rows, stride=PIECES), :] for c in range(PIECES)], axis=1)


def _copy_rows(src, s_row, dst, d_row, n_rows, sem):
    return pltpu.make_async_copy(src.at[pl.ds(pl.multiple_of(s_row * PIECES, PIECES), n_rows * PIECES), :],
                                 dst.at[pl.ds(pl.multiple_of(d_row * PIECES, PIECES), n_rows * PIECES), :], sem)


def _for_each_piece(n, fn, pieces=RUN_PIECES):
    off = 0 if pieces[0] == RUN_PIECES[0] else n & ~(2 * pieces[0] - 1)
    for piece in pieces:
        @pl.when((n & piece) != 0)
        def _(off=off, piece=piece):
            fn(off, piece)
        off = off + (n & piece)


def _experts_body(te_ref, tpos_ref, tvalid_ref, tlo_ref, thi_ref, wslot_ref, wnext_ref, n_ref, cum_ref, loff_ref,
                  xl_hbm, wg_hbm, wu_hbm, wd_hbm, o_ref, wg_s, wu_s, wd_s, wg_f, wu_f, wd_f, wsem, sem, *xbufs):
    i = pl.program_id(0)
    last = pl.num_programs(0) - 1
    token_tiles = n_ref.shape[0] // N_EXPERTS
    ring = len(xbufs)

    def start_run(t, tau, enabled, buf, buf_sem):
        e, lo = te_ref[t], tpos_ref[t]
        k = tau * N_EXPERTS + e
        s, n = cum_ref[k], n_ref[k]
        a = jnp.maximum(s, lo)
        length = jnp.where(enabled, jnp.maximum(jnp.minimum(s + n, lo + ROW_TILE) - a, 0), 0)
        local = loff_ref[k] + (a - s)
        _for_each_piece(length, lambda off, piece: _copy_rows(
            xl_hbm, tau * (SLOTS * ROW_TILE) + local + off, buf, a - lo + off, piece, buf_sem).start())

    def start_runs_loop(t, first, stop, buf, buf_sem):
        def run(tau, c):
            start_run(t, tau, True, buf, buf_sem)
            return c
        lax.fori_loop(first, stop, run, 0)

    def clear(buf):
        buf[...] = jnp.zeros_like(buf)

    @pl.when(i == 0)
    def _():
        for buf in xbufs:
            clear(buf)
        for t in range(ring - 1):
            @pl.when(tvalid_ref[t] > 0)
            def _(t=t):
                start_runs_loop(t, tlo_ref[t], thi_ref[t] + 1, xbufs[t], sem.at[t])

    valid = tvalid_ref[i]

    def weight_copies(e, slot):
        return [pltpu.make_async_copy(w_hbm.at[e], w_f.at[slot], wsem.at[slot])
                for w_hbm, w_f in ((wg_hbm, wg_f), (wu_hbm, wu_f), (wd_hbm, wd_f))]

    @pl.when((i == 0) & (valid > 0))
    def _():
        for cp in weight_copies(te_ref[0], wslot_ref[0]):
            cp.start()

    @pl.when((valid > 0) & ((i == 0) | (te_ref[i] != te_ref[jnp.maximum(i - 1, 0)])))
    def _():
        slot = wslot_ref[i]
        for cp in weight_copies(te_ref[i], slot):
            cp.wait()
        wg_s[...] = wg_f[slot].astype(bf16)
        wu_s[...] = wu_f[slot].astype(bf16)
        wd_s[...] = wd_f[slot].astype(bf16)

        @pl.when(wnext_ref[i] >= 0)
        def _():
            for cp in weight_copies(wnext_ref[i], 1 - slot):
                cp.start()

    def tile_step(cur):
        ahead = (cur + ring - 1) % ring
        buf, buf_sem, next_buf, next_sem = xbufs[cur], sem.at[cur], xbufs[ahead], sem.at[ahead]
        _for_each_piece(valid, lambda off, piece: _copy_rows(
            xl_hbm, 0, buf, 0, piece, buf_sem).wait())

        nxt = jnp.minimum(i + ring - 1, last)
        go = (i + ring - 1 <= last) & (tvalid_ref[nxt] > 0)
        first, final = tlo_ref[nxt], thi_ref[nxt]
        for j in range(UNROLLED_RUNS):
            start_run(nxt, jnp.minimum(first + j, token_tiles - 1), go & (first + j <= final), next_buf, next_sem)

        x = _load_row_major(buf, ROW_TILE).astype(bf16)
        clear(buf)
        hg = jnp.dot(x, wg_s[...], preferred_element_type=f32)
        hu = jnp.dot(x, wu_s[...], preferred_element_type=f32)
        y = jnp.dot((hg * jax.nn.sigmoid(hg) * hu).astype(bf16), wd_s[...], preferred_element_type=f32)
        _store_row_major(o_ref, 0, y)

        @pl.when(go & (final - first >= UNROLLED_RUNS))
        def _():
            start_runs_loop(nxt, first + UNROLLED_RUNS, final + 1, next_buf, next_sem)

    for cur in range(ring):
        pl.when((valid > 0) & (i % ring == cur))(functools.partial(tile_step, cur))

    @pl.when(valid == 0)
    def _():
        o_ref[...] = jnp.zeros_like(o_ref)


def _experts(tables, xl, wg, wu, wd, tiles):
    hbm = pl.BlockSpec(memory_space=pl.ANY)
    return pl.pallas_call(
        _experts_body,
        grid_spec=pltpu.PrefetchScalarGridSpec(
            num_scalar_prefetch=len(tables),
            grid=(tiles,),
            in_specs=[hbm, hbm, hbm, hbm],
            out_specs=pl.BlockSpec((ROW_TILE * PIECES, LANES), lambda i, *_: (i, 0)),
            scratch_shapes=[pltpu.VMEM(w.shape[1:], bf16) for w in (wg, wu, wd)]
            + [pltpu.VMEM((2,) + w.shape[1:], f32) for w in (wg, wu, wd)]
            + [pltpu.SemaphoreType.DMA((2,)), pltpu.SemaphoreType.DMA((EXPERT_RING,))]
            + [pltpu.VMEM((ROW_TILE * PIECES, LANES), f32)] * EXPERT_RING),
        out_shape=jax.ShapeDtypeStruct((tiles * ROW_TILE * PIECES, LANES), f32),
        compiler_params=_cparams(),
        name="moe_experts",
    )(*tables, xl, wg, wu, wd)


def _combine_body(n_ref, gpos_ref, loff_ref, ys_hbm, x_ref, w_ref, l_ref, g_ref, op_ref, os_ref, sem, *ybufs,
                  prompt_tiles):
    i = pl.program_id(0)
    last = pl.num_programs(0) - 1
    tile_rows = SLOTS * ROW_TILE
    ring = len(ybufs)

    def start_run(t, e, enabled, buf, buf_sem, pieces=RUN_PIECES):
        k = t * N_EXPERTS + e
        _for_each_piece(jnp.where(enabled, n_ref[k], 0), lambda off, piece: _copy_rows(
            ys_hbm, gpos_ref[k] + off, buf, loff_ref[k] + off, piece, buf_sem).start(), pieces)

    @pl.when(i == 0)
    def _():
        for t in range(ring - 1):
            def run(e, c, t=t):
                start_run(t, e, True, ybufs[t], sem.at[t])
                return c
            lax.fori_loop(0, N_EXPERTS, run, 0)

    def step(cur):
        ahead = (cur + ring - 1) % ring
        buf, buf_sem, next_buf, next_sem = ybufs[cur], sem.at[cur], ybufs[ahead], sem.at[ahead]
        _copy_rows(ys_hbm, 0, buf, 0, tile_rows, buf_sem).wait()
        nxt, go = jnp.minimum(i + ring - 1, last), i + ring - 1 <= last
        longest = 0
        for e in range(N_EXPERTS):
            start_run(nxt, e, go, next_buf, next_sem, SHORT_PIECES)
            longest = longest | n_ref[nxt * N_EXPERTS + e]
        yl = _load_row_major(buf, tile_rows).astype(bf16)
        sorted_row = lax.broadcasted_iota(i32, (ROW_TILE, tile_rows), 1).astype(f32)
        w, lp = w_ref[...], l_ref[...]
        y1 = jnp.dot((sorted_row == lp[:, 0:1]).astype(bf16), yl, preferred_element_type=f32)
        y2 = jnp.dot((sorted_row == lp[:, 1:2]).astype(bf16), yl, preferred_element_type=f32)
        out = _rms(x_ref[...] + (w[:, 0:1] * y1 + w[:, 1:2] * y2), g_ref[...])

        @pl.when(i < prompt_tiles)
        def _():
            op_ref[...] = out

        @pl.when(i >= prompt_tiles)
        def _():
            os_ref[...] = out

        @pl.when(go & (longest >= 2 * SHORT_PIECES[0]))
        def _():
            def long_pieces(e, c):
                start_run(nxt, e, True, next_buf, next_sem, LONG_PIECES)
                return c
            lax.fori_loop(0, N_EXPERTS, long_pieces, 0)

    for cur in range(ring):
        pl.when(i % ring == cur)(functools.partial(step, cur))


def _combine(tables, ys, x1, wts, lpos, g, prompt_rows):
    prompt_tiles = prompt_rows // ROW_TILE
    tiles = x1.shape[0] // ROW_TILE
    row = lambda n: pl.BlockSpec((ROW_TILE, n), lambda i, *_: (i, 0))
    p_rows = lambda n: pl.BlockSpec((ROW_TILE, n), lambda i, *_: (jnp.minimum(i, prompt_tiles - 1), 0))
    s_rows = lambda n: pl.BlockSpec((ROW_TILE, n), lambda i, *_: (jnp.maximum(i - prompt_tiles, 0), 0))
    return pl.pallas_call(
        functools.partial(_combine_body, prompt_tiles=prompt_tiles),
        grid_spec=pltpu.PrefetchScalarGridSpec(
            num_scalar_prefetch=len(tables),
            grid=(tiles,),
            in_specs=[pl.BlockSpec(memory_space=pl.ANY), row(D_MODEL), row(2), row(2),
                      pl.BlockSpec((1, D_MODEL), lambda i, *_: (0, 0))],
            out_specs=[p_rows(D_MODEL), s_rows(D_MODEL)],
            scratch_shapes=[pltpu.SemaphoreType.DMA((COMBINE_RING,))]
            + [pltpu.VMEM((SLOTS * ROW_TILE * PIECES, LANES), f32)] * COMBINE_RING),
        out_shape=[jax.ShapeDtypeStruct((prompt_rows, D_MODEL), f32),
                   jax.ShapeDtypeStruct((x1.shape[0] - prompt_rows, D_MODEL), f32)],
        compiler_params=_cparams(),
        name="moe_combine",
    )(*tables, ys, x1, wts, lpos, g)


def _moe_tables(n, loff, tiles):
    cum = jnp.cumsum(n, axis=0) - n
    counts = jnp.sum(n, axis=0)
    padded = (counts + ROW_TILE - 1) // ROW_TILE * ROW_TILE
    ends = jnp.cumsum(padded)
    starts = ends - padded
    first = jnp.arange(tiles, dtype=i32) * ROW_TILE
    expert = jnp.minimum(jnp.sum((first[:, None] >= ends[None, :]).astype(i32), axis=1), N_EXPERTS - 1)
    sel = expert[:, None] == jnp.arange(N_EXPERTS)[None, :]
    pick = lambda v: jnp.sum(jnp.where(sel, v[None, :], 0), axis=1)
    pos = first - pick(starts)
    valid = jnp.where(first < ends[-1], jnp.clip(pick(counts) - pos, 0, ROW_TILE), 0)
    cum_t, n_t = cum.T[expert], n.T[expert]
    touches = (cum_t + n_t > pos[:, None]) & (cum_t < (pos + ROW_TILE)[:, None]) & (n_t > 0)
    tau = jnp.arange(n.shape[0], dtype=i32)[None, :]
    lo = jnp.min(jnp.where(touches, tau, n.shape[0]), axis=1)
    hi = jnp.max(jnp.where(touches, tau, -1), axis=1)
    ids = jnp.arange(N_EXPERTS)
    busy = counts > 0
    ordinal = jnp.sum(busy[None, :] & (ids[None, :] < ids[:, None]), axis=1)
    following = jnp.min(jnp.where(busy[None, :] & (ids[None, :] > ids[:, None]), ids[None, :], N_EXPERTS), axis=1)
    following = jnp.where(following < N_EXPERTS, following, -1)
    as_i32 = lambda v: v.astype(i32)
    flat = lambda v: v.reshape(-1).astype(i32)
    expert_tables = (tuple(map(as_i32, (expert, pos, valid, lo, hi, pick(ordinal) % 2, pick(following))))
                     + (flat(n), flat(cum), flat(loff)))
    combine_tables = (flat(n), flat(starts[None, :] + cum), flat(loff))
    return expert_tables, combine_tables


def kernel(x_prompt, x_sample, cache_k, cache_v, state_ssm_re, state_ssm_im, g_norm_mix, w_in, attn_sinks, ssm_a_re,
           ssm_a_im, ssm_log_dt, ssm_b_re, ssm_b_im, ssm_c_re, ssm_c_im, ssm_d, w_glu, b_glu, g_attn_out, g_ssm_out,
           w_out, g_norm_ffn, w_router_group, b_router_group, w_router_expert, b_router_expert, w_exp_gate, w_exp_up,
           w_exp_down, g_final):
    bp, lp, _ = x_prompt.shape
    bs, ls, _ = x_sample.shape
    depth = w_in.shape[0]
    assert depth == 1 and ls == SSM_STEPS and lp % ATTN_TILE == 0 and (bs * ls) % ROW_TILE == 0
    tp, ts = bp * lp, bs * ls
    wc = cache_k.shape[2]
    row2 = lambda v: v.reshape(1, -1)

    xp = x_prompt.reshape(tp, D_MODEL)
    xs = x_sample.reshape(ts, D_MODEL)
    w_in_bf = w_in[0].astype(bf16)
    qp, kp, vp, up = _proj(xp, row2(g_norm_mix[0]), w_in_bf, 1024)
    qs, kq, vq, us = _proj(xs, row2(g_norm_mix[0]), w_in_bf, ts)

    sinks = attn_sinks[0]
    g_att = row2(g_attn_out[0])
    ap = _attn_prompt(qp.reshape(bp, lp, -1), kp.reshape(bp, lp, -1), vp.reshape(bp, lp, -1), sinks, g_att)
    a_s, k_roll, v_roll = _attn_sample(qs.reshape(bs, ls, -1), kq.reshape(bs, ls, -1), vq.reshape(bs, ls, -1),
                                       cache_k[0].reshape(bs, wc, KV_WIDTH), cache_v[0].reshape(bs, wc, KV_WIDTH),
                                       sinks, g_att)

    bbar, steps, lev = _ssm_tables(ssm_a_re[0], ssm_a_im[0], ssm_log_dt[0], ssm_b_re[0], ssm_b_im[0])
    c_blocks = lambda c: c.reshape(SSM_BLOCKS, LANES, SSM_STATE)
    d_row = row2(ssm_d[0])
    to_blocks = lambda h: h.reshape(bs, SSM_GROUPS * SSM_STATE)
    from_blocks = lambda h: h.reshape(bs, SSM_GROUPS, SSM_STATE)
    yp, hrp, hip, ysm, hrs, his = _ssm(up.reshape(bp, lp, -1), us.reshape(1, ts, -1), to_blocks(state_ssm_re[0]),
                                       to_blocks(state_ssm_im[0]), bbar, c_blocks(ssm_c_re[0]),
                                       c_blocks(ssm_c_im[0]), steps, lev, d_row)

    wr = jnp.zeros((D_MODEL, ROUTER_COLS), f32)
    wr = wr.at[:, :N_EXPERTS].set(w_router_expert[0]).at[:, N_EXPERTS:N_EXPERTS + N_EXPERT_GROUPS].set(w_router_group[0])
    br = jnp.zeros((1, ROUTER_COLS), f32)
    br = br.at[0, :N_EXPERTS].set(b_router_expert[0]).at[0, N_EXPERTS:N_EXPERTS + N_EXPERT_GROUPS].set(b_router_group[0])
    tri = jnp.tril(jnp.ones((ROW_TILE, ROW_TILE), bf16), -1)
    upper = jnp.triu(jnp.ones((N_EXPERTS, N_EXPERTS), bf16), 1)
    mix_w = (w_glu[0].astype(bf16), row2(b_glu[0]), row2(g_ssm_out[0]), w_out[0].astype(bf16), row2(g_norm_ffn[0]),
             wr.astype(bf16), br, tri, upper)
    x1, xl, wts, lpos, n_rows, n_off = _mix(ap.reshape(tp, -1), a_s.reshape(ts, -1), yp.reshape(tp, -1),
                                            ysm.reshape(ts, -1), xp, xs, *mix_w)

    per_tile = lambda v: v.reshape(-1, N_EXPERTS).astype(i32)
    tiles = (SLOTS * (tp + ts)) // ROW_TILE + N_EXPERTS
    expert_tables, combine_tables = _moe_tables(per_tile(n_rows), per_tile(n_off), tiles)
    expert_out = _experts(expert_tables, xl, w_exp_gate[0], w_exp_up[0], w_exp_down[0], tiles)
    y_p, y_s = _combine(combine_tables, expert_out, x1, wts, lpos, row2(g_final), tp)

    kvshape = lambda a, b: a.reshape(1, b, -1, N_KV_HEADS, HEAD_DIM)
    block_state = lambda h: h.reshape(bp, SSM_GROUPS, SSM_STATE)[None]
    wcp = min(WINDOW, lp)
    return (y_p.reshape(bp, lp, D_MODEL), y_s.reshape(bs, ls, D_MODEL),
            kvshape(kp.reshape(bp, lp, -1)[:, lp - wcp:], bp), kvshape(vp.reshape(bp, lp, -1)[:, lp - wcp:], bp),
            block_state(hrp), block_state(hip),
            kvshape(k_roll, bs), kvshape(v_roll, bs),
            from_blocks(hrs)[None], from_blocks(his)[None])
```

```python
import functools
import math

import jax
import jax.numpy as jnp
from jax import lax
from jax.experimental import pallas as pl
from jax.experimental.pallas import tpu as pltpu

f32, bf16, i32 = jnp.float32, jnp.bfloat16, jnp.int32

D_MODEL = 1024
CHUNK = 64
N_BACK = 2
WINDOW = 128
ATTN_WIDTH = 512
HEAD_DIM = 64
N_KV_HEADS = 2
Q_PER_KV = 4
KV_WIDTH = 128
SSM_WIDTH = 512
SSM_GROUP = 16
SSM_GROUPS = 32
SSM_STATE = 64
N_EXPERT_GROUPS = 4
EXPERTS_PER_GROUP = 8
N_EXPERTS = 32
EPS = 1e-6
NEG = -1e30

LANES = 128
SSM_STEPS = 16
SSM_SEQS = 2
SSM_BLOCKS = SSM_WIDTH // LANES
GROUPS_PER_BLOCK = LANES // SSM_GROUP
STATE_COLS = GROUPS_PER_BLOCK * SSM_STATE
ROW_TILE = 256
SLOTS = 2
PIECES = D_MODEL // LANES
RUN_PIECES = tuple(1 << b for b in reversed(range(int(math.log2(ROW_TILE)) + 1)))
SHORT_PIECES = tuple(p for p in RUN_PIECES if p <= 32)
LONG_PIECES = tuple(p for p in RUN_PIECES if p > 32)
MIX_TILES = 2
UNROLLED_RUNS = 20
EXPERT_RING = 4
COMBINE_RING = 4
VMEM_LIMIT = 56 * 1024 * 1024


def _cparams(n_axes=1, limit=VMEM_LIMIT):
    return pltpu.CompilerParams(dimension_semantics=("arbitrary",) * n_axes, vmem_limit_bytes=limit)


def _rms(x, g):
    return x * lax.rsqrt(jnp.mean(x * x, axis=-1, keepdims=True) + EPS) * g


def _bdot(a, b):
    return jnp.dot(a.astype(bf16), b.astype(bf16), preferred_element_type=f32)


def _proj_body(x_ref, g_ref, w_ref, q_ref, k_ref, v_ref, u_ref):
    h = _rms(x_ref[...], g_ref[...])
    z = _bdot(h, w_ref[...])
    q_ref[...] = z[:, :ATTN_WIDTH] * (HEAD_DIM ** -0.5)
    k_ref[...] = z[:, ATTN_WIDTH:ATTN_WIDTH + KV_WIDTH]
    v_ref[...] = z[:, ATTN_WIDTH + KV_WIDTH:ATTN_WIDTH + 2 * KV_WIDTH]
    u_ref[...] = z[:, ATTN_WIDTH + 2 * KV_WIDTH:]


def _proj(x2d, g, w_bf, tm):
    t = x2d.shape[0]
    row = lambda n: pl.BlockSpec((tm, n), lambda i: (i, 0))
    full = lambda a: pl.BlockSpec(a.shape, lambda i: (0,) * a.ndim)
    return pl.pallas_call(
        _proj_body,
        grid=(t // tm,),
        in_specs=[row(D_MODEL), full(g), full(w_bf)],
        out_specs=[row(ATTN_WIDTH), row(KV_WIDTH), row(KV_WIDTH), row(SSM_WIDTH)],
        out_shape=[jax.ShapeDtypeStruct((t, n), f32) for n in (ATTN_WIDTH, KV_WIDTH, KV_WIDTH, SSM_WIDTH)],
        compiler_params=_cparams(),
        name="proj",
    )(x2d, g, w_bf)


def _sink_column(sink_ref, kv, rows_per_head):
    r = lax.broadcasted_iota(i32, (Q_PER_KV * rows_per_head, 1), 0)
    col = jnp.full((Q_PER_KV * rows_per_head, 1), sink_ref[kv * Q_PER_KV], f32)
    for j in range(1, Q_PER_KV):
        col = jnp.where(r >= j * rows_per_head, sink_ref[kv * Q_PER_KV + j], col)
    return col


def _attend(qs, kc, vc, sink_col, valid):
    s = lax.dot_general(qs.astype(bf16), kc.astype(bf16), (((1,), (1,)), ((), ())), preferred_element_type=f32)
    if valid is not None:
        s = jnp.where(valid, s, NEG)
    m = jnp.maximum(jnp.max(s, axis=-1, keepdims=True), sink_col)
    p = jnp.exp(s - m)
    denom = jnp.sum(p, axis=-1, keepdims=True) + jnp.exp(sink_col - m)
    return _bdot(p, vc) / denom


def _heads_attend(q, k, v, sink_ref, valid):
    rows = q.shape[0]
    pieces = []
    for kv in range(N_KV_HEADS):
        qs = jnp.concatenate(
            [q[:, (kv * Q_PER_KV + j) * HEAD_DIM:(kv * Q_PER_KV + j + 1) * HEAD_DIM] for j in range(Q_PER_KV)], axis=0)
        o = _attend(qs, k[:, kv * HEAD_DIM:(kv + 1) * HEAD_DIM], v[:, kv * HEAD_DIM:(kv + 1) * HEAD_DIM],
                    _sink_column(sink_ref, kv, rows), valid)
        pieces += [o[j * rows:(j + 1) * rows] for j in range(Q_PER_KV)]
    return jnp.concatenate(pieces, axis=1)


ATTN_TILE = 256
CHUNKS_PER_TILE = ATTN_TILE // CHUNK
KEY_SPAN = (N_BACK + 1) * CHUNK


def _attn_prompt_body(sink_ref, q_ref, kp_ref, kc_ref, vp_ref, vc_ref, g_ref, o_ref):
    i = pl.program_id(1)
    kwin = jnp.concatenate([kp_ref[0], kc_ref[0]], axis=0)
    vwin = jnp.concatenate([vp_ref[0], vc_ref[0]], axis=0)
    key_chunk = lax.broadcasted_iota(i32, (1, KEY_SPAN), 1) // CHUNK
    for c in range(CHUNKS_PER_TILE):
        valid = (i * CHUNKS_PER_TILE + c - N_BACK + key_chunk) >= 0
        o = _heads_attend(q_ref[0, c * CHUNK:(c + 1) * CHUNK, :], kwin[c * CHUNK:c * CHUNK + KEY_SPAN],
                          vwin[c * CHUNK:c * CHUNK + KEY_SPAN], sink_ref, valid)
        o_ref[0, c * CHUNK:(c + 1) * CHUNK, :] = _rms(o, g_ref[...]).astype(bf16)


def _attn_prompt(q, k, v, sinks, g):
    b, l, _ = q.shape
    back = N_BACK * CHUNK
    per = ATTN_TILE // back
    prev = pl.BlockSpec((1, back, KV_WIDTH), lambda bi, i: (bi, jnp.maximum(i * per - 1, 0), 0))
    cur = pl.BlockSpec((1, ATTN_TILE, KV_WIDTH), lambda bi, i: (bi, i, 0))
    return pl.pallas_call(
        _attn_prompt_body,
        grid=(b, l // ATTN_TILE),
        in_specs=[pl.BlockSpec(memory_space=pltpu.SMEM),
                  pl.BlockSpec((1, ATTN_TILE, ATTN_WIDTH), lambda bi, i: (bi, i, 0)),
                  prev, cur, prev, cur,
                  pl.BlockSpec((1, ATTN_WIDTH), lambda bi, i: (0, 0))],
        out_specs=pl.BlockSpec((1, ATTN_TILE, ATTN_WIDTH), lambda bi, i: (bi, i, 0)),
        out_shape=jax.ShapeDtypeStruct((b, l, ATTN_WIDTH), bf16),
        compiler_params=_cparams(2),
        name="attn_prompt",
    )(sinks, q, k, k, v, v, g)


SAMPLE_STREAMS = 16


def _attn_sample_body(sink_ref, q_ref, kn_ref, vn_ref, ck_ref, cv_ref, g_ref, o_ref, nk_ref, nv_ref):
    n_new = kn_ref.shape[1]
    for b in range(q_ref.shape[0]):
        kall = jnp.concatenate([ck_ref[b], kn_ref[b]], axis=0)
        vall = jnp.concatenate([cv_ref[b], vn_ref[b]], axis=0)
        o = _heads_attend(q_ref[b], kall, vall, sink_ref, None)
        o_ref[b] = _rms(o, g_ref[...]).astype(bf16)
        nk_ref[b] = kall[n_new:]
        nv_ref[b] = vall[n_new:]


def _attn_sample(q, k_new, v_new, cache_k, cache_v, sinks, g):
    b, l, _ = q.shape
    wc = cache_k.shape[1]
    assert b % SAMPLE_STREAMS == 0
    blk = lambda r, n: pl.BlockSpec((SAMPLE_STREAMS, r, n), lambda bi: (bi, 0, 0))
    return pl.pallas_call(
        _attn_sample_body,
        grid=(b // SAMPLE_STREAMS,),
        in_specs=[pl.BlockSpec(memory_space=pltpu.SMEM), blk(l, ATTN_WIDTH), blk(l, KV_WIDTH), blk(l, KV_WIDTH),
                  blk(wc, KV_WIDTH), blk(wc, KV_WIDTH), pl.BlockSpec((1, ATTN_WIDTH), lambda bi: (0, 0))],
        out_specs=[blk(l, ATTN_WIDTH), blk(wc, KV_WIDTH), blk(wc, KV_WIDTH)],
        out_shape=[jax.ShapeDtypeStruct((b, l, ATTN_WIDTH), bf16),
                   jax.ShapeDtypeStruct((b, wc, KV_WIDTH), f32), jax.ShapeDtypeStruct((b, wc, KV_WIDTH), f32)],
        compiler_params=_cparams(),
        name="attn_sample",
    )(sinks, q, k_new, v_new, cache_k, cache_v, g)


def _ssm_tables(a_re, a_im, log_dt, b_re, b_im):
    dt = jnp.exp(log_dt)[:, None]
    lam_r, lam_i = a_re * dt, a_im * dt

    def power(n):
        mag = jnp.exp(n * lam_r)
        return jnp.stack([mag * jnp.cos(n * lam_i), mag * jnp.sin(n * lam_i)])

    ar, ai = power(1.0)
    den = a_re * a_re + a_im * a_im
    nr, ni = ar - 1.0, ai
    fr = ((nr * a_re + ni * a_im) / den)[..., None]
    fi = ((ni * a_re - nr * a_im) / den)[..., None]
    bbar = jnp.stack([fr * b_re - fi * b_im, fr * b_im + fi * b_re])
    bbar = bbar.transpose(0, 1, 3, 2).reshape(2, SSM_BLOCKS, LANES, SSM_STATE)
    by_block = lambda t: t.reshape(t.shape[:-2] + (SSM_BLOCKS, STATE_COLS))
    steps = by_block(power(jnp.arange(SSM_STEPS + 1, dtype=f32)[:, None, None]))
    levels = by_block(power(SSM_STEPS * 2.0 ** jnp.arange(8, dtype=f32)[:, None, None]))
    return bbar, steps.transpose(2, 0, 1, 3), levels.transpose(2, 1, 0, 3)


def _ssm_chunk_rows(u_ref, nk):
    xs = [u_ref[0, pl.ds(s, nk, stride=SSM_STEPS), :] for s in range(SSM_STEPS)]
    pairs = [jnp.concatenate([xs[2 * p], xs[2 * p + 1]], axis=1).astype(bf16) for p in range(SSM_STEPS // 2)]
    return xs, pairs


def _ssm_intra(pairs, toep_ref, nk):
    nd = len(pairs)
    y = [None] * nd
    stack = nk < LANES
    for d in range(nd):
        if stack:
            r = jnp.dot(jnp.concatenate(pairs[:nd - d], axis=0), toep_ref[d], preferred_element_type=f32)
        for p in range(nd - d):
            blk = r[p * nk:(p + 1) * nk] if stack else jnp.dot(pairs[p], toep_ref[d], preferred_element_type=f32)
            y[p + d] = blk if y[p + d] is None else y[p + d] + blk
    return y


def _shift_rows(x, sh):
    rows = lax.broadcasted_iota(i32, (x.shape[0], 1), 0)
    return jnp.where(rows >= sh, pltpu.roll(x, sh, axis=0), 0.0)


def _group_index(shape, axis, sub):
    idx = lax.broadcasted_iota(i32, shape, axis)
    return jnp.right_shift(idx, int(math.log2(sub))) & (GROUPS_PER_BLOCK - 1)


def _build_ssm_weights(bb_ref, cr_ref, ci_ref, pw_ref, toep_s, wout_s, win_s, w_low):
    hp = lax.Precision.HIGHEST
    k = lax.broadcasted_iota(i32, (SSM_STATE, STATE_COLS), 0)
    n = lax.broadcasted_iota(i32, (SSM_STATE, STATE_COLS), 1)
    spread = ((n & (SSM_STATE - 1)) == k).astype(f32)
    own = _group_index((LANES, STATE_COLS), 0, SSM_GROUP) == _group_index((LANES, STATE_COLS), 1, SSM_STATE)

    def block_diag(compact):
        return jnp.where(own, jnp.dot(compact, spread, precision=hp, preferred_element_type=f32), 0.0)

    bb_r, bb_i = block_diag(bb_ref[0, 0]), block_diag(bb_ref[1, 0])
    c_r, c_i = block_diag(cr_ref[0]), block_diag(ci_ref[0])
    c_stack = jnp.concatenate([c_r.T, -c_i.T], axis=0)
    power = lambda e: (pw_ref[0, 0, e:e + 1, :], pw_ref[0, 1, e:e + 1, :])
    for s in range(SSM_STEPS):
        rows = slice(s * LANES, (s + 1) * LANES)
        pr, pi = power(SSM_STEPS - 1 - s)
        w = jnp.concatenate([bb_r * pr - bb_i * pi, bb_r * pi + bb_i * pr], axis=1)
        wout_s[rows, :] = w.astype(bf16)
        w_low[rows, :] = (w - w.astype(bf16).astype(f32)).astype(bf16)
        pr, pi = power(s + 1)
        g = jnp.concatenate([c_r * pr - c_i * pi, -(c_r * pi + c_i * pr)], axis=1)
        win_s[:, rows] = g.T.astype(bf16)
    c_high = c_stack.astype(bf16)
    c_low = (c_stack - c_high.astype(f32)).astype(bf16)
    lags = (jnp.dot(wout_s[...], c_high, preferred_element_type=f32)
            + (jnp.dot(wout_s[...], c_low, preferred_element_type=f32)
               + jnp.dot(w_low[...], c_high, preferred_element_type=f32)))
    lag_kernel = [lags[(SSM_STEPS - 1 - lag) * LANES:(SSM_STEPS - lag) * LANES] for lag in range(SSM_STEPS)]
    zero = jnp.zeros((LANES, LANES), f32)
    for d in range(SSM_STEPS // 2):
        top = jnp.concatenate([lag_kernel[2 * d], lag_kernel[2 * d + 1]], axis=1)
        bottom = jnp.concatenate([lag_kernel[2 * d - 1] if d > 0 else zero, lag_kernel[2 * d]], axis=1)
        toep_s[d] = jnp.concatenate([top, bottom], axis=0).astype(bf16)


def _ssm_chunks(u_ref, d_ref, y_ref, toep_s, wout_s, win_s, entry_state):
    nk = u_ref.shape[1] // SSM_STEPS
    xs, pairs = _ssm_chunk_rows(u_ref, nk)
    y = _ssm_intra(pairs, toep_s, nk)
    s = jnp.dot(jnp.concatenate(pairs, axis=1), wout_s[...], preferred_element_type=f32)
    hprev, hr, hi = entry_state(s[:, :STATE_COLS], s[:, STATE_COLS:])
    y2 = _bdot(hprev, win_s[...])
    for st in range(SSM_STEPS):
        piece = (y[st // 2][:, (st % 2) * LANES:(st % 2 + 1) * LANES] + y2[:, st * LANES:(st + 1) * LANES]
                 + d_ref[...] * xs[st])
        y_ref[0, pl.ds(st, nk, stride=SSM_STEPS), :] = piece
    return hr, hi


def _ssm_body(u_ref, us_ref, h0r_ref, h0i_ref, bb_ref, cr_ref, ci_ref, pw_ref, lev_ref, d_ref,
              y_ref, hr_ref, hi_ref, ys_ref, hrs_ref, his_ref, toep_s, wout_s, win_s, w_low):
    @pl.when(pl.program_id(1) == 0)
    def _():
        _build_ssm_weights(bb_ref, cr_ref, ci_ref, pw_ref, toep_s, wout_s, win_s, w_low)

        def one_chunk(sr, si):
            h0r, h0i = h0r_ref[...], h0i_ref[...]
            ar, ai = lev_ref[0, 0, 0:1, :], lev_ref[0, 0, 1:2, :]
            return (jnp.concatenate([h0r, h0i], axis=1), sr + ar * h0r - ai * h0i, si + ar * h0i + ai * h0r)
        hrs_ref[...], his_ref[...] = _ssm_chunks(us_ref, d_ref, ys_ref, toep_s, wout_s, win_s, one_chunk)

    def scan_chunks(sr, si):
        nk = sr.shape[0]
        level = 0
        while (1 << level) < nk:
            ar, ai = lev_ref[0, level, 0:1, :], lev_ref[0, level, 1:2, :]
            tr, ti = _shift_rows(sr, 1 << level), _shift_rows(si, 1 << level)
            sr, si = sr + ar * tr - ai * ti, si + ar * ti + ai * tr
            level += 1
        return (jnp.concatenate([_shift_rows(sr, 1), _shift_rows(si, 1)], axis=1), sr[nk - 1:nk], si[nk - 1:nk])
    for b in range(u_ref.shape[0]):
        hr_ref[b, 0], hi_ref[b, 0] = _ssm_chunks(u_ref.at[pl.ds(b, 1)], d_ref, y_ref.at[pl.ds(b, 1)], toep_s, wout_s,
                                                 win_s, scan_chunks)


def _ssm(u, us, h0r, h0i, bbar, c_re, c_im, steps, lev, d):
    b, l, _ = u.shape
    rows = us.shape[1]
    nb = rows // SSM_STEPS
    wspec = lambda a: pl.BlockSpec((1,) + a.shape[1:], lambda j, bi: (j,) + (0,) * (a.ndim - 1))
    assert b % SSM_SEQS == 0
    st = pl.BlockSpec((SSM_SEQS, 1, 1, STATE_COLS), lambda j, bi: (bi, j, 0, 0))
    sst = pl.BlockSpec((nb, STATE_COLS), lambda j, bi: (0, j))
    seq = pl.BlockSpec((SSM_SEQS, l, LANES), lambda j, bi: (bi, 0, j))
    sseq = pl.BlockSpec((1, rows, LANES), lambda j, bi: (0, 0, j))
    return pl.pallas_call(
        _ssm_body,
        grid=(SSM_BLOCKS, b // SSM_SEQS),
        in_specs=[seq, sseq, sst, sst, pl.BlockSpec((2, 1, LANES, SSM_STATE), lambda j, bi: (0, j, 0, 0)),
                  wspec(c_re), wspec(c_im), wspec(steps), wspec(lev), pl.BlockSpec((1, LANES), lambda j, bi: (0, j))],
        out_specs=[seq, st, st, sseq, sst, sst],
        out_shape=[jax.ShapeDtypeStruct((b, l, SSM_WIDTH), f32),
                   jax.ShapeDtypeStruct((b, SSM_BLOCKS, 1, STATE_COLS), f32),
                   jax.ShapeDtypeStruct((b, SSM_BLOCKS, 1, STATE_COLS), f32),
                   jax.ShapeDtypeStruct((1, rows, SSM_WIDTH), f32),
                   jax.ShapeDtypeStruct((nb, SSM_BLOCKS * STATE_COLS), f32),
                   jax.ShapeDtypeStruct((nb, SSM_BLOCKS * STATE_COLS), f32)],
        scratch_shapes=[pltpu.VMEM((SSM_STEPS // 2, 2 * LANES, 2 * LANES), bf16),
                        pltpu.VMEM((SSM_STEPS * LANES, 2 * STATE_COLS), bf16),
                        pltpu.VMEM((2 * STATE_COLS, SSM_STEPS * LANES), bf16),
                        pltpu.VMEM((SSM_STEPS * LANES, 2 * STATE_COLS), bf16)],
        compiler_params=_cparams(2),
        name="ssm",
    )(u, us, h0r, h0i, bbar, c_re, c_im, steps, lev, d)


ROUTER_COLS = LANES


def _mix_body(ap_ref, as_ref, yp_ref, ys_ref, xp_ref, xs_ref, wglu_ref, bglu_ref, gs_ref, wout_ref, gf_ref, wr_ref,
              br_ref, tri_ref, upper_ref, x1_ref, xl_ref, wts_ref, lpos_ref, n_ref, loff_ref, *, prompt_steps):
    is_prompt = pl.program_id(0) < prompt_steps
    pick_rows = lambda p_ref, s_ref: jnp.where(is_prompt, p_ref[...], jnp.concatenate([s_ref[...]] * MIX_TILES, axis=0))
    y = pick_rows(yp_ref, ys_ref)
    y = 0.5 * y * (1.0 + jnp.tanh(math.sqrt(2.0 / math.pi) * (y + 0.044715 * (y * y * y))))
    y = y * jax.nn.sigmoid(_bdot(y, wglu_ref[...]) + bglu_ref[...])
    attn = jnp.where(is_prompt, ap_ref[...].astype(f32),
                     jnp.concatenate([as_ref[...].astype(f32)] * MIX_TILES, axis=0)).astype(bf16)
    cat = jnp.concatenate([attn, _rms(y, gs_ref[...]).astype(bf16)], axis=1)
    x1 = pick_rows(xp_ref, xs_ref) + jnp.dot(cat, wout_ref[...], preferred_element_type=f32)
    x1_ref[...] = x1
    hf = _rms(x1, gf_ref[...])

    logits = _bdot(hf, wr_ref[...]) + br_ref[...]
    le = logits[:, :N_EXPERTS]
    lg = logits[:, N_EXPERTS:N_EXPERTS + N_EXPERT_GROUPS]
    tm = le.shape[0]
    gmax = jnp.max(lg, axis=-1, keepdims=True)
    gi = lax.broadcasted_iota(i32, (tm, N_EXPERT_GROUPS), 1).astype(f32)
    gsel = jnp.min(jnp.where(lg == gmax, gi, float(N_EXPERT_GROUPS)), axis=-1, keepdims=True)
    pg = 1.0 / jnp.sum(jnp.exp(lg - gmax), axis=-1, keepdims=True)
    ei_int = lax.broadcasted_iota(i32, (tm, N_EXPERTS), 1)
    ei = ei_int.astype(f32)
    egroup = jnp.right_shift(ei_int, int(math.log2(EXPERTS_PER_GROUP))).astype(f32)
    lm = jnp.where(egroup == gsel, le, NEG)
    v1 = jnp.max(lm, axis=-1, keepdims=True)
    i1 = jnp.min(jnp.where(lm == v1, ei, float(N_EXPERTS)), axis=-1, keepdims=True)
    lm2 = jnp.where(ei == i1, NEG, lm)
    v2 = jnp.max(lm2, axis=-1, keepdims=True)
    i2 = jnp.min(jnp.where(lm2 == v2, ei, float(N_EXPERTS)), axis=-1, keepdims=True)
    ex = jnp.exp(v2 - v1)
    wts_ref[...] = jnp.concatenate([pg / (1.0 + ex), pg * ex / (1.0 + ex)], axis=1)

    oh1 = (ei == i1).astype(f32)
    oh2 = (ei == i2).astype(f32)
    hf_bf = hf.astype(bf16)
    sorted_row = lax.broadcasted_iota(i32, (ROW_TILE, SLOTS * ROW_TILE), 1).astype(f32)
    for h in range(MIX_TILES):
        rows = slice(h * ROW_TILE, (h + 1) * ROW_TILE)
        both = (oh1[rows] + oh2[rows]).astype(bf16)
        before = jnp.dot(tri_ref[...], both, preferred_element_type=f32)
        count = jnp.sum(oh1[rows] + oh2[rows], axis=0, keepdims=True)
        lower = jnp.sum(jnp.dot(both, upper_ref[...], preferred_element_type=f32), axis=0, keepdims=True)
        lp1 = jnp.sum(oh1[rows] * (before + lower), axis=-1, keepdims=True)
        lp2 = jnp.sum(oh2[rows] * (before + lower), axis=-1, keepdims=True)
        lpos_ref[rows, :] = jnp.concatenate([lp1, lp2], axis=1)
        n_ref[h] = count
        loff_ref[h] = lower
        pick = ((sorted_row == lp1) | (sorted_row == lp2)).astype(bf16)
        xl = lax.dot_general(pick, hf_bf[rows], (((0,), (0,)), ((), ())), preferred_element_type=f32)
        _store_row_major(xl_ref, h * SLOTS * ROW_TILE, xl)


def _mix(a_p, a_s, ys_p, ys_s, x_p, x_s, wglu, bglu, gs, wout, gf, wr, br, tri, upper):
    tm = MIX_TILES * ROW_TILE
    assert x_p.shape[0] % tm == 0 and x_s.shape[0] == ROW_TILE
    prompt_steps = x_p.shape[0] // tm
    t = x_p.shape[0] + x_s.shape[0]
    p_rows = lambda n: pl.BlockSpec((tm, n), lambda i: (jnp.minimum(i, prompt_steps - 1), 0))
    s_rows = lambda n: pl.BlockSpec((ROW_TILE, n), lambda i: (0, 0))
    row = lambda n: pl.BlockSpec((tm, n), lambda i: (i, 0))
    full = lambda arr: pl.BlockSpec(arr.shape, lambda i: (0,) * arr.ndim)
    per_tile = pl.BlockSpec((MIX_TILES, 1, N_EXPERTS), lambda i: (i, 0, 0))
    return pl.pallas_call(
        functools.partial(_mix_body, prompt_steps=prompt_steps),
        grid=(prompt_steps + 1,),
        in_specs=[p_rows(ATTN_WIDTH), s_rows(ATTN_WIDTH), p_rows(SSM_WIDTH), s_rows(SSM_WIDTH), p_rows(D_MODEL),
                  s_rows(D_MODEL), full(wglu), full(bglu), full(gs), full(wout), full(gf), full(wr), full(br),
                  full(tri), full(upper)],
        out_specs=[row(D_MODEL), pl.BlockSpec((SLOTS * tm * PIECES, LANES), lambda i: (i, 0)), row(2), row(2),
                   per_tile, per_tile],
        out_shape=[jax.ShapeDtypeStruct((t, D_MODEL), f32),
                   jax.ShapeDtypeStruct((SLOTS * t * PIECES, LANES), f32),
                   jax.ShapeDtypeStruct((t, 2), f32), jax.ShapeDtypeStruct((t, 2), f32),
                   jax.ShapeDtypeStruct((t // ROW_TILE, 1, N_EXPERTS), f32),
                   jax.ShapeDtypeStruct((t // ROW_TILE, 1, N_EXPERTS), f32)],
        compiler_params=_cparams(),
        name="mix",
    )(a_p, a_s, ys_p, ys_s, x_p, x_s, wglu, bglu, gs, wout, gf, wr, br, tri, upper)


def _store_row_major(ref, first_row, x):
    for c in range(PIECES):
        ref[pl.ds(first_row * PIECES + c, x.shape[0], stride=PIECES), :] = x[:, c * LANES:(c + 1) * LANES]


def _load_row_major(ref, n_rows):
    return jnp.concatenate([ref[pl.ds(c, n_rows, stride=PIECES), :] for c in range(PIECES)], axis=1)


def _copy_rows(src, s_row, dst, d_row, n_rows, sem):
    return pltpu.make_async_copy(src.at[pl.ds(pl.multiple_of(s_row * PIECES, PIECES), n_rows * PIECES), :],
                                 dst.at[pl.ds(pl.multiple_of(d_row * PIECES, PIECES), n_rows * PIECES), :], sem)


def _for_each_piece(n, fn, pieces=RUN_PIECES):
    off = 0 if pieces[0] == RUN_PIECES[0] else n & ~(2 * pieces[0] - 1)
    for piece in pieces:
        @pl.when((n & piece) != 0)
        def _(off=off, piece=piece):
            fn(off, piece)
        off = off + (n & piece)


def _experts_body(te_ref, tpos_ref, tvalid_ref, tlo_ref, thi_ref, wslot_ref, wnext_ref, n_ref, cum_ref, loff_ref,
                  xl_hbm, wg_hbm, wu_hbm, wd_hbm, o_ref, wg_s, wu_s, wd_s, wg_f, wu_f, wd_f, wsem, sem, *xbufs):
    i = pl.program_id(0)
    last = pl.num_programs(0) - 1
    token_tiles = n_ref.shape[0] // N_EXPERTS
    ring = len(xbufs)

    def start_run(t, tau, enabled, buf, buf_sem):
        e, lo = te_ref[t], tpos_ref[t]
        k = tau * N_EXPERTS + e
        s, n = cum_ref[k], n_ref[k]
        a = jnp.maximum(s, lo)
        length = jnp.where(enabled, jnp.maximum(jnp.minimum(s + n, lo + ROW_TILE) - a, 0), 0)
        local = loff_ref[k] + (a - s)
        _for_each_piece(length, lambda off, piece: _copy_rows(
            xl_hbm, tau * (SLOTS * ROW_TILE) + local + off, buf, a - lo + off, piece, buf_sem).start())

    def start_runs_loop(t, first, stop, buf, buf_sem):
        def run(tau, c):
            start_run(t, tau, True, buf, buf_sem)
            return c
        lax.fori_loop(first, stop, run, 0)

    def clear(buf):
        buf[...] = jnp.zeros_like(buf)

    @pl.when(i == 0)
    def _():
        for buf in xbufs:
            clear(buf)
        for t in range(ring - 1):
            @pl.when(tvalid_ref[t] > 0)
            def _(t=t):
                start_runs_loop(t, tlo_ref[t], thi_ref[t] + 1, xbufs[t], sem.at[t])

    valid = tvalid_ref[i]

    def weight_copies(e, slot):
        return [pltpu.make_async_copy(w_hbm.at[e], w_f.at[slot], wsem.at[slot])
                for w_hbm, w_f in ((wg_hbm, wg_f), (wu_hbm, wu_f), (wd_hbm, wd_f))]

    @pl.when((i == 0) & (valid > 0))
    def _():
        for cp in weight_copies(te_ref[0], wslot_ref[0]):
            cp.start()

    @pl.when((valid > 0) & ((i == 0) | (te_ref[i] != te_ref[jnp.maximum(i - 1, 0)])))
    def _():
        slot = wslot_ref[i]
        for cp in weight_copies(te_ref[i], slot):
            cp.wait()
        wg_s[...] = wg_f[slot].astype(bf16)
        wu_s[...] = wu_f[slot].astype(bf16)
        wd_s[...] = wd_f[slot].astype(bf16)

        @pl.when(wnext_ref[i] >= 0)
        def _():
            for cp in weight_copies(wnext_ref[i], 1 - slot):
                cp.start()

    def tile_step(cur):
        ahead = (cur + ring - 1) % ring
        buf, buf_sem, next_buf, next_sem = xbufs[cur], sem.at[cur], xbufs[ahead], sem.at[ahead]
        _for_each_piece(valid, lambda off, piece: _copy_rows(
            xl_hbm, 0, buf, 0, piece, buf_sem).wait())

        nxt = jnp.minimum(i + ring - 1, last)
        go = (i + ring - 1 <= last) & (tvalid_ref[nxt] > 0)
        first, final = tlo_ref[nxt], thi_ref[nxt]
        for j in range(UNROLLED_RUNS):
            start_run(nxt, jnp.minimum(first + j, token_tiles - 1), go & (first + j <= final), next_buf, next_sem)

        x = _load_row_major(buf, ROW_TILE).astype(bf16)
        clear(buf)
        hg = jnp.dot(x, wg_s[...], preferred_element_type=f32)
        hu = jnp.dot(x, wu_s[...], preferred_element_type=f32)
        y = jnp.dot((hg * jax.nn.sigmoid(hg) * hu).astype(bf16), wd_s[...], preferred_element_type=f32)
        _store_row_major(o_ref, 0, y)

        @pl.when(go & (final - first >= UNROLLED_RUNS))
        def _():
            start_runs_loop(nxt, first + UNROLLED_RUNS, final + 1, next_buf, next_sem)

    for cur in range(ring):
        pl.when((valid > 0) & (i % ring == cur))(functools.partial(tile_step, cur))

    @pl.when(valid == 0)
    def _():
        o_ref[...] = jnp.zeros_like(o_ref)


def _experts(tables, xl, wg, wu, wd, tiles):
    hbm = pl.BlockSpec(memory_space=pl.ANY)
    return pl.pallas_call(
        _experts_body,
        grid_spec=pltpu.PrefetchScalarGridSpec(
            num_scalar_prefetch=len(tables),
            grid=(tiles,),
            in_specs=[hbm, hbm, hbm, hbm],
            out_specs=pl.BlockSpec((ROW_TILE * PIECES, LANES), lambda i, *_: (i, 0)),
            scratch_shapes=[pltpu.VMEM(w.shape[1:], bf16) for w in (wg, wu, wd)]
            + [pltpu.VMEM((2,) + w.shape[1:], f32) for w in (wg, wu, wd)]
            + [pltpu.SemaphoreType.DMA((2,)), pltpu.SemaphoreType.DMA((EXPERT_RING,))]
            + [pltpu.VMEM((ROW_TILE * PIECES, LANES), f32)] * EXPERT_RING),
        out_shape=jax.ShapeDtypeStruct((tiles * ROW_TILE * PIECES, LANES), f32),
        compiler_params=_cparams(),
        name="moe_experts",
    )(*tables, xl, wg, wu, wd)


def _combine_body(n_ref, gpos_ref, loff_ref, ys_hbm, x_ref, w_ref, l_ref, g_ref, op_ref, os_ref, sem, *ybufs,
                  prompt_tiles):
    i = pl.program_id(0)
    last = pl.num_programs(0) - 1
    tile_rows = SLOTS * ROW_TILE
    ring = len(ybufs)

    def start_run(t, e, enabled, buf, buf_sem, pieces=RUN_PIECES):
        k = t * N_EXPERTS + e
        _for_each_piece(jnp.where(enabled, n_ref[k], 0), lambda off, piece: _copy_rows(
            ys_hbm, gpos_ref[k] + off, buf, loff_ref[k] + off, piece, buf_sem).start(), pieces)

    @pl.when(i == 0)
    def _():
        for t in range(ring - 1):
            def run(e, c, t=t):
                start_run(t, e, True, ybufs[t], sem.at[t])
                return c
            lax.fori_loop(0, N_EXPERTS, run, 0)

    def step(cur):
        ahead = (cur + ring - 1) % ring
        buf, buf_sem, next_buf, next_sem = ybufs[cur], sem.at[cur], ybufs[ahead], sem.at[ahead]
        _copy_rows(ys_hbm, 0, buf, 0, tile_rows, buf_sem).wait()
        nxt, go = jnp.minimum(i + ring - 1, last), i + ring - 1 <= last
        longest = 0
        for e in range(N_EXPERTS):
            start_run(nxt, e, go, next_buf, next_sem, SHORT_PIECES)
            longest = longest | n_ref[nxt * N_EXPERTS + e]
        yl = _load_row_major(buf, tile_rows).astype(bf16)
        sorted_row = lax.broadcasted_iota(i32, (ROW_TILE, tile_rows), 1).astype(f32)
        w, lp = w_ref[...], l_ref[...]
        y1 = jnp.dot((sorted_row == lp[:, 0:1]).astype(bf16), yl, preferred_element_type=f32)
        y2 = jnp.dot((sorted_row == lp[:, 1:2]).astype(bf16), yl, preferred_element_type=f32)
        out = _rms(x_ref[...] + (w[:, 0:1] * y1 + w[:, 1:2] * y2), g_ref[...])

        @pl.when(i < prompt_tiles)
        def _():
            op_ref[...] = out

        @pl.when(i >= prompt_tiles)
        def _():
            os_ref[...] = out

        @pl.when(go & (longest >= 2 * SHORT_PIECES[0]))
        def _():
            def long_pieces(e, c):
                start_run(nxt, e, True, next_buf, next_sem, LONG_PIECES)
                return c
            lax.fori_loop(0, N_EXPERTS, long_pieces, 0)

    for cur in range(ring):
        pl.when(i % ring == cur)(functools.partial(step, cur))


def _combine(tables, ys, x1, wts, lpos, g, prompt_rows):
    prompt_tiles = prompt_rows // ROW_TILE
    tiles = x1.shape[0] // ROW_TILE
    row = lambda n: pl.BlockSpec((ROW_TILE, n), lambda i, *_: (i, 0))
    p_rows = lambda n: pl.BlockSpec((ROW_TILE, n), lambda i, *_: (jnp.minimum(i, prompt_tiles - 1), 0))
    s_rows = lambda n: pl.BlockSpec((ROW_TILE, n), lambda i, *_: (jnp.maximum(i - prompt_tiles, 0), 0))
    return pl.pallas_call(
        functools.partial(_combine_body, prompt_tiles=prompt_tiles),
        grid_spec=pltpu.PrefetchScalarGridSpec(
            num_scalar_prefetch=len(tables),
            grid=(tiles,),
            in_specs=[pl.BlockSpec(memory_space=pl.ANY), row(D_MODEL), row(2), row(2),
                      pl.BlockSpec((1, D_MODEL), lambda i, *_: (0, 0))],
            out_specs=[p_rows(D_MODEL), s_rows(D_MODEL)],
            scratch_shapes=[pltpu.SemaphoreType.DMA((COMBINE_RING,))]
            + [pltpu.VMEM((SLOTS * ROW_TILE * PIECES, LANES), f32)] * COMBINE_RING),
        out_shape=[jax.ShapeDtypeStruct((prompt_rows, D_MODEL), f32),
                   jax.ShapeDtypeStruct((x1.shape[0] - prompt_rows, D_MODEL), f32)],
        compiler_params=_cparams(),
        name="moe_combine",
    )(*tables, ys, x1, wts, lpos, g)


def _moe_tables(n, loff, tiles):
    cum = jnp.cumsum(n, axis=0) - n
    counts = jnp.sum(n, axis=0)
    padded = (counts + ROW_TILE - 1) // ROW_TILE * ROW_TILE
    ends = jnp.cumsum(padded)
    starts = ends - padded
    first = jnp.arange(tiles, dtype=i32) * ROW_TILE
    expert = jnp.minimum(jnp.sum((first[:, None] >= ends[None, :]).astype(i32), axis=1), N_EXPERTS - 1)
    sel = expert[:, None] == jnp.arange(N_EXPERTS)[None, :]
    pick = lambda v: jnp.sum(jnp.where(sel, v[None, :], 0), axis=1)
    pos = first - pick(starts)
    valid = jnp.where(first < ends[-1], jnp.clip(pick(counts) - pos, 0, ROW_TILE), 0)
    cum_t, n_t = cum.T[expert], n.T[expert]
    touches = (cum_t + n_t > pos[:, None]) & (cum_t < (pos + ROW_TILE)[:, None]) & (n_t > 0)
    tau = jnp.arange(n.shape[0], dtype=i32)[None, :]
    lo = jnp.min(jnp.where(touches, tau, n.shape[0]), axis=1)
    hi = jnp.max(jnp.where(touches, tau, -1), axis=1)
    ids = jnp.arange(N_EXPERTS)
    busy = counts > 0
    ordinal = jnp.sum(busy[None, :] & (ids[None, :] < ids[:, None]), axis=1)
    following = jnp.min(jnp.where(busy[None, :] & (ids[None, :] > ids[:, None]), ids[None, :], N_EXPERTS), axis=1)
    following = jnp.where(following < N_EXPERTS, following, -1)
    as_i32 = lambda v: v.astype(i32)
    flat = lambda v: v.reshape(-1).astype(i32)
    expert_tables = (tuple(map(as_i32, (expert, pos, valid, lo, hi, pick(ordinal) % 2, pick(following))))
                     + (flat(n), flat(cum), flat(loff)))
    combine_tables = (flat(n), flat(starts[None, :] + cum), flat(loff))
    return expert_tables, combine_tables


def kernel(x_prompt, x_sample, cache_k, cache_v, state_ssm_re, state_ssm_im, g_norm_mix, w_in, attn_sinks, ssm_a_re,
           ssm_a_im, ssm_log_dt, ssm_b_re, ssm_b_im, ssm_c_re, ssm_c_im, ssm_d, w_glu, b_glu, g_attn_out, g_ssm_out,
           w_out, g_norm_ffn, w_router_group, b_router_group, w_router_expert, b_router_expert, w_exp_gate, w_exp_up,
           w_exp_down, g_final):
    bp, lp, _ = x_prompt.shape
    bs, ls, _ = x_sample.shape
    depth = w_in.shape[0]
    assert depth == 1 and ls == SSM_STEPS and lp % ATTN_TILE == 0 and (bs * ls) % ROW_TILE == 0
    tp, ts = bp * lp, bs * ls
    wc = cache_k.shape[2]
    row2 = lambda v: v.reshape(1, -1)

    xp = x_prompt.reshape(tp, D_MODEL)
    xs = x_sample.reshape(ts, D_MODEL)
    w_in_bf = w_in[0].astype(bf16)
    qp, kp, vp, up = _proj(xp, row2(g_norm_mix[0]), w_in_bf, 1024)
    qs, kq, vq, us = _proj(xs, row2(g_norm_mix[0]), w_in_bf, ts)

    sinks = attn_sinks[0]
    g_att = row2(g_attn_out[0])
    ap = _attn_prompt(qp.reshape(bp, lp, -1), kp.reshape(bp, lp, -1), vp.reshape(bp, lp, -1), sinks, g_att)
    a_s, k_roll, v_roll = _attn_sample(qs.reshape(bs, ls, -1), kq.reshape(bs, ls, -1), vq.reshape(bs, ls, -1),
                                       cache_k[0].reshape(bs, wc, KV_WIDTH), cache_v[0].reshape(bs, wc, KV_WIDTH),
                                       sinks, g_att)

    bbar, steps, lev = _ssm_tables(ssm_a_re[0], ssm_a_im[0], ssm_log_dt[0], ssm_b_re[0], ssm_b_im[0])
    c_blocks = lambda c: c.reshape(SSM_BLOCKS, LANES, SSM_STATE)
    d_row = row2(ssm_d[0])
    to_blocks = lambda h: h.reshape(bs, SSM_GROUPS * SSM_STATE)
    from_blocks = lambda h: h.reshape(bs, SSM_GROUPS, SSM_STATE)
    yp, hrp, hip, ysm, hrs, his = _ssm(up.reshape(bp, lp, -1), us.reshape(1, ts, -1), to_blocks(state_ssm_re[0]),
                                       to_blocks(state_ssm_im[0]), bbar, c_blocks(ssm_c_re[0]),
                                       c_blocks(ssm_c_im[0]), steps, lev, d_row)

    wr = jnp.zeros((D_MODEL, ROUTER_COLS), f32)
    wr = wr.at[:, :N_EXPERTS].set(w_router_expert[0]).at[:, N_EXPERTS:N_EXPERTS + N_EXPERT_GROUPS].set(w_router_group[0])
    br = jnp.zeros((1, ROUTER_COLS), f32)
    br = br.at[0, :N_EXPERTS].set(b_router_expert[0]).at[0, N_EXPERTS:N_EXPERTS + N_EXPERT_GROUPS].set(b_router_group[0])
    tri = jnp.tril(jnp.ones((ROW_TILE, ROW_TILE), bf16), -1)
    upper = jnp.triu(jnp.ones((N_EXPERTS, N_EXPERTS), bf16), 1)
    mix_w = (w_glu[0].astype(bf16), row2(b_glu[0]), row2(g_ssm_out[0]), w_out[0].astype(bf16), row2(g_norm_ffn[0]),
             wr.astype(bf16), br, tri, upper)
    x1, xl, wts, lpos, n_rows, n_off = _mix(ap.reshape(tp, -1), a_s.reshape(ts, -1), yp.reshape(tp, -1),
                                            ysm.reshape(ts, -1), xp, xs, *mix_w)

    per_tile = lambda v: v.reshape(-1, N_EXPERTS).astype(i32)
    tiles = (SLOTS * (tp + ts)) // ROW_TILE + N_EXPERTS
    expert_tables, combine_tables = _moe_tables(per_tile(n_rows), per_tile(n_off), tiles)
    expert_out = _experts(expert_tables, xl, w_exp_gate[0], w_exp_up[0], w_exp_down[0], tiles)
    y_p, y_s = _combine(combine_tables, expert_out, x1, wts, lpos, row2(g_final), tp)

    kvshape = lambda a, b: a.reshape(1, b, -1, N_KV_HEADS, HEAD_DIM)
    block_state = lambda h: h.reshape(bp, SSM_GROUPS, SSM_STATE)[None]
    wcp = min(WINDOW, lp)
    return (y_p.reshape(bp, lp, D_MODEL), y_s.reshape(bs, ls, D_MODEL),
            kvshape(kp.reshape(bp, lp, -1)[:, lp - wcp:], bp), kvshape(vp.reshape(bp, lp, -1)[:, lp - wcp:], bp),
            block_state(hrp), block_state(hip),
            kvshape(k_roll, bs), kvshape(v_roll, bs),
            from_blocks(hrs)[None], from_blocks(his)[None])
```

```python
import functools
import math

import jax
import jax.numpy as jnp
from jax import lax
from jax.experimental import pallas as pl
from jax.experimental.pallas import tpu as pltpu

f32, bf16, i32 = jnp.float32, jnp.bfloat16, jnp.int32

D_MODEL = 1024
CHUNK = 64
N_BACK = 2
WINDOW = 128
ATTN_WIDTH = 512
HEAD_DIM = 64
N_KV_HEADS = 2
Q_PER_KV = 4
KV_WIDTH = 128
SSM_WIDTH = 512
SSM_GROUP = 16
SSM_GROUPS = 32
SSM_STATE = 64
N_EXPERT_GROUPS = 4
EXPERTS_PER_GROUP = 8
N_EXPERTS = 32
EPS = 1e-6
NEG = -1e30

LANES = 128
SSM_STEPS = 16
SSM_SEQS = 2
SSM_BLOCKS = SSM_WIDTH // LANES
GROUPS_PER_BLOCK = LANES // SSM_GROUP
STATE_COLS = GROUPS_PER_BLOCK * SSM_STATE
ROW_TILE = 256
SLOTS = 2
PIECES = D_MODEL // LANES
RUN_PIECES = tuple(1 << b for b in reversed(range(int(math.log2(ROW_TILE)) + 1)))
SHORT_PIECES = tuple(p for p in RUN_PIECES if p <= 32)
LONG_PIECES = tuple(p for p in RUN_PIECES if p > 32)
MIX_TILES = 2
UNROLLED_RUNS = 20
EXPERT_RING = 6
COMBINE_RING = 3
VMEM_LIMIT = 56 * 1024 * 1024


def _cparams(n_axes=1, limit=VMEM_LIMIT):
    return pltpu.CompilerParams(dimension_semantics=("arbitrary",) * n_axes, vmem_limit_bytes=limit)


def _rms(x, g):
    return x * lax.rsqrt(jnp.mean(x * x, axis=-1, keepdims=True) + EPS) * g


def _bdot(a, b):
    return jnp.dot(a.astype(bf16), b.astype(bf16), preferred_element_type=f32)


def _proj_body(x_ref, g_ref, w_ref, q_ref, k_ref, v_ref, u_ref):
    h = _rms(x_ref[...], g_ref[...])
    z = _bdot(h, w_ref[...])
    q_ref[...] = z[:, :ATTN_WIDTH] * (HEAD_DIM ** -0.5)
    k_ref[...] = z[:, ATTN_WIDTH:ATTN_WIDTH + KV_WIDTH]
    v_ref[...] = z[:, ATTN_WIDTH + KV_WIDTH:ATTN_WIDTH + 2 * KV_WIDTH]
    u_ref[...] = z[:, ATTN_WIDTH + 2 * KV_WIDTH:]


def _proj(x2d, g, w_bf, tm):
    t = x2d.shape[0]
    row = lambda n: pl.BlockSpec((tm, n), lambda i: (i, 0))
    full = lambda a: pl.BlockSpec(a.shape, lambda i: (0,) * a.ndim)
    return pl.pallas_call(
        _proj_body,
        grid=(t // tm,),
        in_specs=[row(D_MODEL), full(g), full(w_bf)],
        out_specs=[row(ATTN_WIDTH), row(KV_WIDTH), row(KV_WIDTH), row(SSM_WIDTH)],
        out_shape=[jax.ShapeDtypeStruct((t, n), f32) for n in (ATTN_WIDTH, KV_WIDTH, KV_WIDTH, SSM_WIDTH)],
        compiler_params=_cparams(),
        name="proj",
    )(x2d, g, w_bf)


def _sink_column(sink_ref, kv, rows_per_head):
    r = lax.broadcasted_iota(i32, (Q_PER_KV * rows_per_head, 1), 0)
    col = jnp.full((Q_PER_KV * rows_per_head, 1), sink_ref[kv * Q_PER_KV], f32)
    for j in range(1, Q_PER_KV):
        col = jnp.where(r >= j * rows_per_head, sink_ref[kv * Q_PER_KV + j], col)
    return col


def _attend(qs, kc, vc, sink_col, valid):
    s = lax.dot_general(qs.astype(bf16), kc.astype(bf16), (((1,), (1,)), ((), ())), preferred_element_type=f32)
    if valid is not None:
        s = jnp.where(valid, s, NEG)
    m = jnp.maximum(jnp.max(s, axis=-1, keepdims=True), sink_col)
    p = jnp.exp(s - m)
    denom = jnp.sum(p, axis=-1, keepdims=True) + jnp.exp(sink_col - m)
    return _bdot(p, vc) / denom


def _heads_attend(q, k, v, sink_ref, valid):
    rows = q.shape[0]
    pieces = []
    for kv in range(N_KV_HEADS):
        qs = jnp.concatenate(
            [q[:, (kv * Q_PER_KV + j) * HEAD_DIM:(kv * Q_PER_KV + j + 1) * HEAD_DIM] for j in range(Q_PER_KV)], axis=0)
        o = _attend(qs, k[:, kv * HEAD_DIM:(kv + 1) * HEAD_DIM], v[:, kv * HEAD_DIM:(kv + 1) * HEAD_DIM],
                    _sink_column(sink_ref, kv, rows), valid)
        pieces += [o[j * rows:(j + 1) * rows] for j in range(Q_PER_KV)]
    return jnp.concatenate(pieces, axis=1)


ATTN_TILE = 256
CHUNKS_PER_TILE = ATTN_TILE // CHUNK
KEY_SPAN = (N_BACK + 1) * CHUNK


def _attn_prompt_body(sink_ref, q_ref, kp_ref, kc_ref, vp_ref, vc_ref, g_ref, o_ref):
    i = pl.program_id(1)
    kwin = jnp.concatenate([kp_ref[0], kc_ref[0]], axis=0)
    vwin = jnp.concatenate([vp_ref[0], vc_ref[0]], axis=0)
    key_chunk = lax.broadcasted_iota(i32, (1, KEY_SPAN), 1) // CHUNK
    for c in range(CHUNKS_PER_TILE):
        valid = (i * CHUNKS_PER_TILE + c - N_BACK + key_chunk) >= 0
        o = _heads_attend(q_ref[0, c * CHUNK:(c + 1) * CHUNK, :], kwin[c * CHUNK:c * CHUNK + KEY_SPAN],
                          vwin[c * CHUNK:c * CHUNK + KEY_SPAN], sink_ref, valid)
        o_ref[0, c * CHUNK:(c + 1) * CHUNK, :] = _rms(o, g_ref[...]).astype(bf16)


def _attn_prompt(q, k, v, sinks, g):
    b, l, _ = q.shape
    back = N_BACK * CHUNK
    per = ATTN_TILE // back
    prev = pl.BlockSpec((1, back, KV_WIDTH), lambda bi, i: (bi, jnp.maximum(i * per - 1, 0), 0))
    cur = pl.BlockSpec((1, ATTN_TILE, KV_WIDTH), lambda bi, i: (bi, i, 0))
    return pl.pallas_call(
        _attn_prompt_body,
        grid=(b, l // ATTN_TILE),
        in_specs=[pl.BlockSpec(memory_space=pltpu.SMEM),
                  pl.BlockSpec((1, ATTN_TILE, ATTN_WIDTH), lambda bi, i: (bi, i, 0)),
                  prev, cur, prev, cur,
                  pl.BlockSpec((1, ATTN_WIDTH), lambda bi, i: (0, 0))],
        out_specs=pl.BlockSpec((1, ATTN_TILE, ATTN_WIDTH), lambda bi, i: (bi, i, 0)),
        out_shape=jax.ShapeDtypeStruct((b, l, ATTN_WIDTH), bf16),
        compiler_params=_cparams(2),
        name="attn_prompt",
    )(sinks, q, k, k, v, v, g)


SAMPLE_STREAMS = 16


def _attn_sample_body(sink_ref, q_ref, kn_ref, vn_ref, ck_ref, cv_ref, g_ref, o_ref, nk_ref, nv_ref):
    n_new = kn_ref.shape[1]
    for b in range(q_ref.shape[0]):
        kall = jnp.concatenate([ck_ref[b], kn_ref[b]], axis=0)
        vall = jnp.concatenate([cv_ref[b], vn_ref[b]], axis=0)
        o = _heads_attend(q_ref[b], kall, vall, sink_ref, None)
        o_ref[b] = _rms(o, g_ref[...]).astype(bf16)
        nk_ref[b] = kall[n_new:]
        nv_ref[b] = vall[n_new:]


def _attn_sample(q, k_new, v_new, cache_k, cache_v, sinks, g):
    b, l, _ = q.shape
    wc = cache_k.shape[1]
    assert b % SAMPLE_STREAMS == 0
    blk = lambda r, n: pl.BlockSpec((SAMPLE_STREAMS, r, n), lambda bi: (bi, 0, 0))
    return pl.pallas_call(
        _attn_sample_body,
        grid=(b // SAMPLE_STREAMS,),
        in_specs=[pl.BlockSpec(memory_space=pltpu.SMEM), blk(l, ATTN_WIDTH), blk(l, KV_WIDTH), blk(l, KV_WIDTH),
                  blk(wc, KV_WIDTH), blk(wc, KV_WIDTH), pl.BlockSpec((1, ATTN_WIDTH), lambda bi: (0, 0))],
        out_specs=[blk(l, ATTN_WIDTH), blk(wc, KV_WIDTH), blk(wc, KV_WIDTH)],
        out_shape=[jax.ShapeDtypeStruct((b, l, ATTN_WIDTH), bf16),
                   jax.ShapeDtypeStruct((b, wc, KV_WIDTH), f32), jax.ShapeDtypeStruct((b, wc, KV_WIDTH), f32)],
        compiler_params=_cparams(),
        name="attn_sample",
    )(sinks, q, k_new, v_new, cache_k, cache_v, g)


def _ssm_tables(a_re, a_im, log_dt, b_re, b_im):
    dt = jnp.exp(log_dt)[:, None]
    lam_r, lam_i = a_re * dt, a_im * dt

    def power(n):
        mag = jnp.exp(n * lam_r)
        return jnp.stack([mag * jnp.cos(n * lam_i), mag * jnp.sin(n * lam_i)])

    ar, ai = power(1.0)
    den = a_re * a_re + a_im * a_im
    nr, ni = ar - 1.0, ai
    fr = ((nr * a_re + ni * a_im) / den)[..., None]
    fi = ((ni * a_re - nr * a_im) / den)[..., None]
    bbar = jnp.stack([fr * b_re - fi * b_im, fr * b_im + fi * b_re])
    bbar = bbar.transpose(0, 1, 3, 2).reshape(2, SSM_BLOCKS, LANES, SSM_STATE)
    by_block = lambda t: t.reshape(t.shape[:-2] + (SSM_BLOCKS, STATE_COLS))
    steps = by_block(power(jnp.arange(SSM_STEPS + 1, dtype=f32)[:, None, None]))
    levels = by_block(power(SSM_STEPS * 2.0 ** jnp.arange(8, dtype=f32)[:, None, None]))
    return bbar, steps.transpose(2, 0, 1, 3), levels.transpose(2, 1, 0, 3)


def _ssm_chunk_rows(u_ref, nk):
    xs = [u_ref[0, pl.ds(s, nk, stride=SSM_STEPS), :] for s in range(SSM_STEPS)]
    pairs = [jnp.concatenate([xs[2 * p], xs[2 * p + 1]], axis=1).astype(bf16) for p in range(SSM_STEPS // 2)]
    return xs, pairs


def _ssm_intra(pairs, toep_ref, nk):
    nd = len(pairs)
    y = [None] * nd
    stack = nk < LANES
    for d in range(nd):
        if stack:
            r = jnp.dot(jnp.concatenate(pairs[:nd - d], axis=0), toep_ref[d], preferred_element_type=f32)
        for p in range(nd - d):
            blk = r[p * nk:(p + 1) * nk] if stack else jnp.dot(pairs[p], toep_ref[d], preferred_element_type=f32)
            y[p + d] = blk if y[p + d] is None else y[p + d] + blk
    return y


def _shift_rows(x, sh):
    rows = lax.broadcasted_iota(i32, (x.shape[0], 1), 0)
    return jnp.where(rows >= sh, pltpu.roll(x, sh, axis=0), 0.0)


def _group_index(shape, axis, sub):
    idx = lax.broadcasted_iota(i32, shape, axis)
    return jnp.right_shift(idx, int(math.log2(sub))) & (GROUPS_PER_BLOCK - 1)


def _build_ssm_weights(bb_ref, cr_ref, ci_ref, pw_ref, toep_s, wout_s, win_s, w_low):
    hp = lax.Precision.HIGHEST
    k = lax.broadcasted_iota(i32, (SSM_STATE, STATE_COLS), 0)
    n = lax.broadcasted_iota(i32, (SSM_STATE, STATE_COLS), 1)
    spread = ((n & (SSM_STATE - 1)) == k).astype(f32)
    own = _group_index((LANES, STATE_COLS), 0, SSM_GROUP) == _group_index((LANES, STATE_COLS), 1, SSM_STATE)

    def block_diag(compact):
        return jnp.where(own, jnp.dot(compact, spread, precision=hp, preferred_element_type=f32), 0.0)

    bb_r, bb_i = block_diag(bb_ref[0, 0]), block_diag(bb_ref[1, 0])
    c_r, c_i = block_diag(cr_ref[0]), block_diag(ci_ref[0])
    c_stack = jnp.concatenate([c_r.T, -c_i.T], axis=0)
    power = lambda e: (pw_ref[0, 0, e:e + 1, :], pw_ref[0, 1, e:e + 1, :])
    for s in range(SSM_STEPS):
        rows = slice(s * LANES, (s + 1) * LANES)
        pr, pi = power(SSM_STEPS - 1 - s)
        w = jnp.concatenate([bb_r * pr - bb_i * pi, bb_r * pi + bb_i * pr], axis=1)
        wout_s[rows, :] = w.astype(bf16)
        w_low[rows, :] = (w - w.astype(bf16).astype(f32)).astype(bf16)
        pr, pi = power(s + 1)
        g = jnp.concatenate([c_r * pr - c_i * pi, -(c_r * pi + c_i * pr)], axis=1)
        win_s[:, rows] = g.T.astype(bf16)
    c_high = c_stack.astype(bf16)
    c_low = (c_stack - c_high.astype(f32)).astype(bf16)
    lags = (jnp.dot(wout_s[...], c_high, preferred_element_type=f32)
            + (jnp.dot(wout_s[...], c_low, preferred_element_type=f32)
               + jnp.dot(w_low[...], c_high, preferred_element_type=f32)))
    lag_kernel = [lags[(SSM_STEPS - 1 - lag) * LANES:(SSM_STEPS - lag) * LANES] for lag in range(SSM_STEPS)]
    zero = jnp.zeros((LANES, LANES), f32)
    for d in range(SSM_STEPS // 2):
        top = jnp.concatenate([lag_kernel[2 * d], lag_kernel[2 * d + 1]], axis=1)
        bottom = jnp.concatenate([lag_kernel[2 * d - 1] if d > 0 else zero, lag_kernel[2 * d]], axis=1)
        toep_s[d] = jnp.concatenate([top, bottom], axis=0).astype(bf16)


def _ssm_chunks(u_ref, d_ref, y_ref, toep_s, wout_s, win_s, entry_state):
    nk = u_ref.shape[1] // SSM_STEPS
    xs, pairs = _ssm_chunk_rows(u_ref, nk)
    y = _ssm_intra(pairs, toep_s, nk)
    s = jnp.dot(jnp.concatenate(pairs, axis=1), wout_s[...], preferred_element_type=f32)
    hprev, hr, hi = entry_state(s[:, :STATE_COLS], s[:, STATE_COLS:])
    y2 = _bdot(hprev, win_s[...])
    for st in range(SSM_STEPS):
        piece = (y[st // 2][:, (st % 2) * LANES:(st % 2 + 1) * LANES] + y2[:, st * LANES:(st + 1) * LANES]
                 + d_ref[...] * xs[st])
        y_ref[0, pl.ds(st, nk, stride=SSM_STEPS), :] = piece
    return hr, hi


def _ssm_body(u_ref, us_ref, h0r_ref, h0i_ref, bb_ref, cr_ref, ci_ref, pw_ref, lev_ref, d_ref,
              y_ref, hr_ref, hi_ref, ys_ref, hrs_ref, his_ref, toep_s, wout_s, win_s, w_low):
    @pl.when(pl.program_id(1) == 0)
    def _():
        _build_ssm_weights(bb_ref, cr_ref, ci_ref, pw_ref, toep_s, wout_s, win_s, w_low)

        def one_chunk(sr, si):
            h0r, h0i = h0r_ref[...], h0i_ref[...]
            ar, ai = lev_ref[0, 0, 0:1, :], lev_ref[0, 0, 1:2, :]
            return (jnp.concatenate([h0r, h0i], axis=1), sr + ar * h0r - ai * h0i, si + ar * h0i + ai * h0r)
        hrs_ref[...], his_ref[...] = _ssm_chunks(us_ref, d_ref, ys_ref, toep_s, wout_s, win_s, one_chunk)

    def scan_chunks(sr, si):
        nk = sr.shape[0]
        level = 0
        while (1 << level) < nk:
            ar, ai = lev_ref[0, level, 0:1, :], lev_ref[0, level, 1:2, :]
            tr, ti = _shift_rows(sr, 1 << level), _shift_rows(si, 1 << level)
            sr, si = sr + ar * tr - ai * ti, si + ar * ti + ai * tr
            level += 1
        return (jnp.concatenate([_shift_rows(sr, 1), _shift_rows(si, 1)], axis=1), sr[nk - 1:nk], si[nk - 1:nk])
    for b in range(u_ref.shape[0]):
        hr_ref[b, 0], hi_ref[b, 0] = _ssm_chunks(u_ref.at[pl.ds(b, 1)], d_ref, y_ref.at[pl.ds(b, 1)], toep_s, wout_s,
                                                 win_s, scan_chunks)


def _ssm(u, us, h0r, h0i, bbar, c_re, c_im, steps, lev, d):
    b, l, _ = u.shape
    rows = us.shape[1]
    nb = rows // SSM_STEPS
    wspec = lambda a: pl.BlockSpec((1,) + a.shape[1:], lambda j, bi: (j,) + (0,) * (a.ndim - 1))
    assert b % SSM_SEQS == 0
    st = pl.BlockSpec((SSM_SEQS, 1, 1, STATE_COLS), lambda j, bi: (bi, j, 0, 0))
    sst = pl.BlockSpec((nb, STATE_COLS), lambda j, bi: (0, j))
    seq = pl.BlockSpec((SSM_SEQS, l, LANES), lambda j, bi: (bi, 0, j))
    sseq = pl.BlockSpec((1, rows, LANES), lambda j, bi: (0, 0, j))
    return pl.pallas_call(
        _ssm_body,
        grid=(SSM_BLOCKS, b // SSM_SEQS),
        in_specs=[seq, sseq, sst, sst, pl.BlockSpec((2, 1, LANES, SSM_STATE), lambda j, bi: (0, j, 0, 0)),
                  wspec(c_re), wspec(c_im), wspec(steps), wspec(lev), pl.BlockSpec((1, LANES), lambda j, bi: (0, j))],
        out_specs=[seq, st, st, sseq, sst, sst],
        out_shape=[jax.ShapeDtypeStruct((b, l, SSM_WIDTH), f32),
                   jax.ShapeDtypeStruct((b, SSM_BLOCKS, 1, STATE_COLS), f32),
                   jax.ShapeDtypeStruct((b, SSM_BLOCKS, 1, STATE_COLS), f32),
                   jax.ShapeDtypeStruct((1, rows, SSM_WIDTH), f32),
                   jax.ShapeDtypeStruct((nb, SSM_BLOCKS * STATE_COLS), f32),
                   jax.ShapeDtypeStruct((nb, SSM_BLOCKS * STATE_COLS), f32)],
        scratch_shapes=[pltpu.VMEM((SSM_STEPS // 2, 2 * LANES, 2 * LANES), bf16),
                        pltpu.VMEM((SSM_STEPS * LANES, 2 * STATE_COLS), bf16),
                        pltpu.VMEM((2 * STATE_COLS, SSM_STEPS * LANES), bf16),
                        pltpu.VMEM((SSM_STEPS * LANES, 2 * STATE_COLS), bf16)],
        compiler_params=_cparams(2),
        name="ssm",
    )(u, us, h0r, h0i, bbar, c_re, c_im, steps, lev, d)


ROUTER_COLS = LANES


def _mix_body(ap_ref, as_ref, yp_ref, ys_ref, xp_ref, xs_ref, wglu_ref, bglu_ref, gs_ref, wout_ref, gf_ref, wr_ref,
              br_ref, tri_ref, upper_ref, x1_ref, xl_ref, wts_ref, lpos_ref, n_ref, loff_ref, *, prompt_steps):
    is_prompt = pl.program_id(0) < prompt_steps
    pick_rows = lambda p_ref, s_ref: jnp.where(is_prompt, p_ref[...], jnp.concatenate([s_ref[...]] * MIX_TILES, axis=0))
    y = pick_rows(yp_ref, ys_ref)
    y = 0.5 * y * (1.0 + jnp.tanh(math.sqrt(2.0 / math.pi) * (y + 0.044715 * (y * y * y))))
    y = y * jax.nn.sigmoid(_bdot(y, wglu_ref[...]) + bglu_ref[...])
    attn = jnp.where(is_prompt, ap_ref[...].astype(f32),
                     jnp.concatenate([as_ref[...].astype(f32)] * MIX_TILES, axis=0)).astype(bf16)
    cat = jnp.concatenate([attn, _rms(y, gs_ref[...]).astype(bf16)], axis=1)
    x1 = pick_rows(xp_ref, xs_ref) + jnp.dot(cat, wout_ref[...], preferred_element_type=f32)
    x1_ref[...] = x1
    hf = _rms(x1, gf_ref[...])

    logits = _bdot(hf, wr_ref[...]) + br_ref[...]
    le = logits[:, :N_EXPERTS]
    lg = logits[:, N_EXPERTS:N_EXPERTS + N_EXPERT_GROUPS]
    tm = le.shape[0]
    gmax = jnp.max(lg, axis=-1, keepdims=True)
    gi = lax.broadcasted_iota(i32, (tm, N_EXPERT_GROUPS), 1).astype(f32)
    gsel = jnp.min(jnp.where(lg == gmax, gi, float(N_EXPERT_GROUPS)), axis=-1, keepdims=True)
    pg = 1.0 / jnp.sum(jnp.exp(lg - gmax), axis=-1, keepdims=True)
    ei_int = lax.broadcasted_iota(i32, (tm, N_EXPERTS), 1)
    ei = ei_int.astype(f32)
    egroup = jnp.right_shift(ei_int, int(math.log2(EXPERTS_PER_GROUP))).astype(f32)
    lm = jnp.where(egroup == gsel, le, NEG)
    v1 = jnp.max(lm, axis=-1, keepdims=True)
    i1 = jnp.min(jnp.where(lm == v1, ei, float(N_EXPERTS)), axis=-1, keepdims=True)
    lm2 = jnp.where(ei == i1, NEG, lm)
    v2 = jnp.max(lm2, axis=-1, keepdims=True)
    i2 = jnp.min(jnp.where(lm2 == v2, ei, float(N_EXPERTS)), axis=-1, keepdims=True)
    ex = jnp.exp(v2 - v1)
    wts_ref[...] = jnp.concatenate([pg / (1.0 + ex), pg * ex / (1.0 + ex)], axis=1)

    oh1 = (ei == i1).astype(f32)
    oh2 = (ei == i2).astype(f32)
    hf_bf = hf.astype(bf16)
    sorted_row = lax.broadcasted_iota(i32, (ROW_TILE, SLOTS * ROW_TILE), 1).astype(f32)
    for h in range(MIX_TILES):
        rows = slice(h * ROW_TILE, (h + 1) * ROW_TILE)
        both = (oh1[rows] + oh2[rows]).astype(bf16)
        before = jnp.dot(tri_ref[...], both, preferred_element_type=f32)
        count = jnp.sum(oh1[rows] + oh2[rows], axis=0, keepdims=True)
        lower = jnp.sum(jnp.dot(both, upper_ref[...], preferred_element_type=f32), axis=0, keepdims=True)
        lp1 = jnp.sum(oh1[rows] * (before + lower), axis=-1, keepdims=True)
        lp2 = jnp.sum(oh2[rows] * (before + lower), axis=-1, keepdims=True)
        lpos_ref[rows, :] = jnp.concatenate([lp1, lp2], axis=1)
        n_ref[h] = count
        loff_ref[h] = lower
        pick = ((sorted_row == lp1) | (sorted_row == lp2)).astype(bf16)
        xl = lax.dot_general(pick, hf_bf[rows], (((0,), (0,)), ((), ())), preferred_element_type=f32)
        _store_row_major(xl_ref, h * SLOTS * ROW_TILE, xl)


def _mix(a_p, a_s, ys_p, ys_s, x_p, x_s, wglu, bglu, gs, wout, gf, wr, br, tri, upper):
    tm = MIX_TILES * ROW_TILE
    assert x_p.shape[0] % tm == 0 and x_s.shape[0] == ROW_TILE
    prompt_steps = x_p.shape[0] // tm
    t = x_p.shape[0] + x_s.shape[0]
    p_rows = lambda n: pl.BlockSpec((tm, n), lambda i: (jnp.minimum(i, prompt_steps - 1), 0))
    s_rows = lambda n: pl.BlockSpec((ROW_TILE, n), lambda i: (0, 0))
    row = lambda n: pl.BlockSpec((tm, n), lambda i: (i, 0))
    full = lambda arr: pl.BlockSpec(arr.shape, lambda i: (0,) * arr.ndim)
    per_tile = pl.BlockSpec((MIX_TILES, 1, N_EXPERTS), lambda i: (i, 0, 0))
    return pl.pallas_call(
        functools.partial(_mix_body, prompt_steps=prompt_steps),
        grid=(prompt_steps + 1,),
        in_specs=[p_rows(ATTN_WIDTH), s_rows(ATTN_WIDTH), p_rows(SSM_WIDTH), s_rows(SSM_WIDTH), p_rows(D_MODEL),
                  s_rows(D_MODEL), full(wglu), full(bglu), full(gs), full(wout), full(gf), full(wr), full(br),
                  full(tri), full(upper)],
        out_specs=[row(D_MODEL), pl.BlockSpec((SLOTS * tm * PIECES, LANES), lambda i: (i, 0)), row(2), row(2),
                   per_tile, per_tile],
        out_shape=[jax.ShapeDtypeStruct((t, D_MODEL), f32),
                   jax.ShapeDtypeStruct((SLOTS * t * PIECES, LANES), f32),
                   jax.ShapeDtypeStruct((t, 2), f32), jax.ShapeDtypeStruct((t, 2), f32),
                   jax.ShapeDtypeStruct((t // ROW_TILE, 1, N_EXPERTS), f32),
                   jax.ShapeDtypeStruct((t // ROW_TILE, 1, N_EXPERTS), f32)],
        compiler_params=_cparams(),
        name="mix",
    )(a_p, a_s, ys_p, ys_s, x_p, x_s, wglu, bglu, gs, wout, gf, wr, br, tri, upper)


def _store_row_major(ref, first_row, x):
    for c in range(PIECES):
        ref[pl.ds(first_row * PIECES + c, x.shape[0], stride=PIECES), :] = x[:, c * LANES:(c + 1) * LANES]


def _load_row_major(ref, n_rows):
    return jnp.concatenate([ref[pl.ds(c, n_rows, stride=PIECES), :] for c in range(PIECES)], axis=1)


def _copy_rows(src, s_row, dst, d_row, n_rows, sem):
    return pltpu.make_async_copy(src.at[pl.ds(pl.multiple_of(s_row * PIECES, PIECES), n_rows * PIECES), :],
                                 dst.at[pl.ds(pl.multiple_of(d_row * PIECES, PIECES), n_rows * PIECES), :], sem)


def _for_each_piece(n, fn, pieces=RUN_PIECES):
    off = 0 if pieces[0] == RUN_PIECES[0] else n & ~(2 * pieces[0] - 1)
    for piece in pieces:
        @pl.when((n & piece) != 0)
        def _(off=off, piece=piece):
            fn(off, piece)
        off = off + (n & piece)


def _experts_body(te_ref, tpos_ref, tvalid_ref, tlo_ref, thi_ref, wslot_ref, wnext_ref, n_ref, cum_ref, loff_ref,
                  xl_hbm, wg_hbm, wu_hbm, wd_hbm, o_ref, wg_s, wu_s, wd_s, wg_f, wu_f, wd_f, wsem, sem, *xbufs):
    i = pl.program_id(0)
    last = pl.num_programs(0) - 1
    token_tiles = n_ref.shape[0] // N_EXPERTS
    ring = len(xbufs)

    def start_run(t, tau, enabled, buf, buf_sem):
        e, lo = te_ref[t], tpos_ref[t]
        k = tau * N_EXPERTS + e
        s, n = cum_ref[k], n_ref[k]
        a = jnp.maximum(s, lo)
        length = jnp.where(enabled, jnp.maximum(jnp.minimum(s + n, lo + ROW_TILE) - a, 0), 0)
        local = loff_ref[k] + (a - s)
        _for_each_piece(length, lambda off, piece: _copy_rows(
            xl_hbm, tau * (SLOTS * ROW_TILE) + local + off, buf, a - lo + off, piece, buf_sem).start())

    def start_runs_loop(t, first, stop, buf, buf_sem):
        def run(tau, c):
            start_run(t, tau, True, buf, buf_sem)
            return c
        lax.fori_loop(first, stop, run, 0)

    def clear(buf):
        buf[...] = jnp.zeros_like(buf)

    @pl.when(i == 0)
    def _():
        for buf in xbufs:
            clear(buf)
        for t in range(ring - 1):
            @pl.when(tvalid_ref[t] > 0)
            def _(t=t):
                start_runs_loop(t, tlo_ref[t], thi_ref[t] + 1, xbufs[t], sem.at[t])

    valid = tvalid_ref[i]

    def weight_copies(e, slot):
        return [pltpu.make_async_copy(w_hbm.at[e], w_f.at[slot], wsem.at[slot])
                for w_hbm, w_f in ((wg_hbm, wg_f), (wu_hbm, wu_f), (wd_hbm, wd_f))]

    @pl.when((i == 0) & (valid > 0))
    def _():
        for cp in weight_copies(te_ref[0], wslot_ref[0]):
            cp.start()

    @pl.when((valid > 0) & ((i == 0) | (te_ref[i] != te_ref[jnp.maximum(i - 1, 0)])))
    def _():
        slot = wslot_ref[i]
        for cp in weight_copies(te_ref[i], slot):
            cp.wait()
        wg_s[...] = wg_f[slot].astype(bf16)
        wu_s[...] = wu_f[slot].astype(bf16)
        wd_s[...] = wd_f[slot].astype(bf16)

        @pl.when(wnext_ref[i] >= 0)
        def _():
            for cp in weight_copies(wnext_ref[i], 1 - slot):
                cp.start()

    def tile_step(cur):
        ahead = (cur + ring - 1) % ring
        buf, buf_sem, next_buf, next_sem = xbufs[cur], sem.at[cur], xbufs[ahead], sem.at[ahead]
        _for_each_piece(valid, lambda off, piece: _copy_rows(
            xl_hbm, 0, buf, 0, piece, buf_sem).wait())

        nxt = jnp.minimum(i + ring - 1, last)
        go = (i + ring - 1 <= last) & (tvalid_ref[nxt] > 0)
        first, final = tlo_ref[nxt], thi_ref[nxt]
        for j in range(UNROLLED_RUNS):
            start_run(nxt, jnp.minimum(first + j, token_tiles - 1), go & (first + j <= final), next_buf, next_sem)

        x = _load_row_major(buf, ROW_TILE).astype(bf16)
        clear(buf)
        hg = jnp.dot(x, wg_s[...], preferred_element_type=f32)
        hu = jnp.dot(x, wu_s[...], preferred_element_type=f32)
        y = jnp.dot((hg * jax.nn.sigmoid(hg) * hu).astype(bf16), wd_s[...], preferred_element_type=f32)
        _store_row_major(o_ref, 0, y)

        @pl.when(go & (final - first >= UNROLLED_RUNS))
        def _():
            start_runs_loop(nxt, first + UNROLLED_RUNS, final + 1, next_buf, next_sem)

    for cur in range(ring):
        pl.when((valid > 0) & (i % ring == cur))(functools.partial(tile_step, cur))

    @pl.when(valid == 0)
    def _():
        o_ref[...] = jnp.zeros_like(o_ref)


def _experts(tables, xl, wg, wu, wd, tiles):
    hbm = pl.BlockSpec(memory_space=pl.ANY)
    return pl.pallas_call(
        _experts_body,
        grid_spec=pltpu.PrefetchScalarGridSpec(
            num_scalar_prefetch=len(tables),
            grid=(tiles,),
            in_specs=[hbm, hbm, hbm, hbm],
            out_specs=pl.BlockSpec((ROW_TILE * PIECES, LANES), lambda i, *_: (i, 0)),
            scratch_shapes=[pltpu.VMEM(w.shape[1:], bf16) for w in (wg, wu, wd)]
            + [pltpu.VMEM((2,) + w.shape[1:], f32) for w in (wg, wu, wd)]
            + [pltpu.SemaphoreType.DMA((2,)), pltpu.SemaphoreType.DMA((EXPERT_RING,))]
            + [pltpu.VMEM((ROW_TILE * PIECES, LANES), f32)] * EXPERT_RING),
        out_shape=jax.ShapeDtypeStruct((tiles * ROW_TILE * PIECES, LANES), f32),
        compiler_params=_cparams(),
        name="moe_experts",
    )(*tables, xl, wg, wu, wd)


def _combine_body(n_ref, gpos_ref, loff_ref, ys_hbm, x_ref, w_ref, l_ref, g_ref, op_ref, os_ref, sem, *ybufs,
                  prompt_tiles):
    i = pl.program_id(0)
    last = pl.num_programs(0) - 1
    tile_rows = SLOTS * ROW_TILE
    ring = len(ybufs)

    def start_run(t, e, enabled, buf, buf_sem, pieces=RUN_PIECES):
        k = t * N_EXPERTS + e
        _for_each_piece(jnp.where(enabled, n_ref[k], 0), lambda off, piece: _copy_rows(
            ys_hbm, gpos_ref[k] + off, buf, loff_ref[k] + off, piece, buf_sem).start(), pieces)

    @pl.when(i == 0)
    def _():
        for t in range(ring - 1):
            def run(e, c, t=t):
                start_run(t, e, True, ybufs[t], sem.at[t])
                return c
            lax.fori_loop(0, N_EXPERTS, run, 0)

    def step(cur):
        ahead = (cur + ring - 1) % ring
        buf, buf_sem, next_buf, next_sem = ybufs[cur], sem.at[cur], ybufs[ahead], sem.at[ahead]
        _copy_rows(ys_hbm, 0, buf, 0, tile_rows, buf_sem).wait()
        nxt, go = jnp.minimum(i + ring - 1, last), i + ring - 1 <= last
        longest = 0
        for e in range(N_EXPERTS):
            start_run(nxt, e, go, next_buf, next_sem, SHORT_PIECES)
            longest = longest | n_ref[nxt * N_EXPERTS + e]
        yl = _load_row_major(buf, tile_rows).astype(bf16)
        sorted_row = lax.broadcasted_iota(i32, (ROW_TILE, tile_rows), 1).astype(f32)
        w, lp = w_ref[...], l_ref[...]
        y1 = jnp.dot((sorted_row == lp[:, 0:1]).astype(bf16), yl, preferred_element_type=f32)
        y2 = jnp.dot((sorted_row == lp[:, 1:2]).astype(bf16), yl, preferred_element_type=f32)
        out = _rms(x_ref[...] + (w[:, 0:1] * y1 + w[:, 1:2] * y2), g_ref[...])

        @pl.when(i < prompt_tiles)
        def _():
            op_ref[...] = out

        @pl.when(i >= prompt_tiles)
        def _():
            os_ref[...] = out

        @pl.when(go & (longest >= 2 * SHORT_PIECES[0]))
        def _():
            def long_pieces(e, c):
                start_run(nxt, e, True, next_buf, next_sem, LONG_PIECES)
                return c
            lax.fori_loop(0, N_EXPERTS, long_pieces, 0)

    for cur in range(ring):
        pl.when(i % ring == cur)(functools.partial(step, cur))


def _combine(tables, ys, x1, wts, lpos, g, prompt_rows):
    prompt_tiles = prompt_rows // ROW_TILE
    tiles = x1.shape[0] // ROW_TILE
    row = lambda n: pl.BlockSpec((ROW_TILE, n), lambda i, *_: (i, 0))
    p_rows = lambda n: pl.BlockSpec((ROW_TILE, n), lambda i, *_: (jnp.minimum(i, prompt_tiles - 1), 0))
    s_rows = lambda n: pl.BlockSpec((ROW_TILE, n), lambda i, *_: (jnp.maximum(i - prompt_tiles, 0), 0))
    return pl.pallas_call(
        functools.partial(_combine_body, prompt_tiles=prompt_tiles),
        grid_spec=pltpu.PrefetchScalarGridSpec(
            num_scalar_prefetch=len(tables),
            grid=(tiles,),
            in_specs=[pl.BlockSpec(memory_space=pl.ANY), row(D_MODEL), row(2), row(2),
                      pl.BlockSpec((1, D_MODEL), lambda i, *_: (0, 0))],
            out_specs=[p_rows(D_MODEL), s_rows(D_MODEL)],
            scratch_shapes=[pltpu.SemaphoreType.DMA((COMBINE_RING,))]
            + [pltpu.VMEM((SLOTS * ROW_TILE * PIECES, LANES), f32)] * COMBINE_RING),
        out_shape=[jax.ShapeDtypeStruct((prompt_rows, D_MODEL), f32),
                   jax.ShapeDtypeStruct((x1.shape[0] - prompt_rows, D_MODEL), f32)],
        compiler_params=_cparams(),
        name="moe_combine",
    )(*tables, ys, x1, wts, lpos, g)


def _moe_tables(n, loff, tiles):
    cum = jnp.cumsum(n, axis=0) - n
    counts = jnp.sum(n, axis=0)
    padded = (counts + ROW_TILE - 1) // ROW_TILE * ROW_TILE
    ends = jnp.cumsum(padded)
    starts = ends - padded
    first = jnp.arange(tiles, dtype=i32) * ROW_TILE
    expert = jnp.minimum(jnp.sum((first[:, None] >= ends[None, :]).astype(i32), axis=1), N_EXPERTS - 1)
    sel = expert[:, None] == jnp.arange(N_EXPERTS)[None, :]
    pick = lambda v: jnp.sum(jnp.where(sel, v[None, :], 0), axis=1)
    pos = first - pick(starts)
    valid = jnp.where(first < ends[-1], jnp.clip(pick(counts) - pos, 0, ROW_TILE), 0)
    cum_t, n_t = cum.T[expert], n.T[expert]
    touches = (cum_t + n_t > pos[:, None]) & (cum_t < (pos + ROW_TILE)[:, None]) & (n_t > 0)
    tau = jnp.arange(n.shape[0], dtype=i32)[None, :]
    lo = jnp.min(jnp.where(touches, tau, n.shape[0]), axis=1)
    hi = jnp.max(jnp.where(touches, tau, -1), axis=1)
    ids = jnp.arange(N_EXPERTS)
    busy = counts > 0
    ordinal = jnp.sum(busy[None, :] & (ids[None, :] < ids[:, None]), axis=1)
    following = jnp.min(jnp.where(busy[None, :] & (ids[None, :] > ids[:, None]), ids[None, :], N_EXPERTS), axis=1)
    following = jnp.where(following < N_EXPERTS, following, -1)
    as_i32 = lambda v: v.astype(i32)
    flat = lambda v: v.reshape(-1).astype(i32)
    expert_tables = (tuple(map(as_i32, (expert, pos, valid, lo, hi, pick(ordinal) % 2, pick(following))))
                     + (flat(n), flat(cum), flat(loff)))
    combine_tables = (flat(n), flat(starts[None, :] + cum), flat(loff))
    return expert_tables, combine_tables


def kernel(x_prompt, x_sample, cache_k, cache_v, state_ssm_re, state_ssm_im, g_norm_mix, w_in, attn_sinks, ssm_a_re,
           ssm_a_im, ssm_log_dt, ssm_b_re, ssm_b_im, ssm_c_re, ssm_c_im, ssm_d, w_glu, b_glu, g_attn_out, g_ssm_out,
           w_out, g_norm_ffn, w_router_group, b_router_group, w_router_expert, b_router_expert, w_exp_gate, w_exp_up,
           w_exp_down, g_final):
    bp, lp, _ = x_prompt.shape
    bs, ls, _ = x_sample.shape
    depth = w_in.shape[0]
    assert depth == 1 and ls == SSM_STEPS and lp % ATTN_TILE == 0 and (bs * ls) % ROW_TILE == 0
    tp, ts = bp * lp, bs * ls
    wc = cache_k.shape[2]
    row2 = lambda v: v.reshape(1, -1)

    xp = x_prompt.reshape(tp, D_MODEL)
    xs = x_sample.reshape(ts, D_MODEL)
    w_in_bf = w_in[0].astype(bf16)
    qp, kp, vp, up = _proj(xp, row2(g_norm_mix[0]), w_in_bf, 1024)
    qs, kq, vq, us = _proj(xs, row2(g_norm_mix[0]), w_in_bf, ts)

    sinks = attn_sinks[0]
    g_att = row2(g_attn_out[0])
    ap = _attn_prompt(qp.reshape(bp, lp, -1), kp.reshape(bp, lp, -1), vp.reshape(bp, lp, -1), sinks, g_att)
    a_s, k_roll, v_roll = _attn_sample(qs.reshape(bs, ls, -1), kq.reshape(bs, ls, -1), vq.reshape(bs, ls, -1),
                                       cache_k[0].reshape(bs, wc, KV_WIDTH), cache_v[0].reshape(bs, wc, KV_WIDTH),
                                       sinks, g_att)

    bbar, steps, lev = _ssm_tables(ssm_a_re[0], ssm_a_im[0], ssm_log_dt[0], ssm_b_re[0], ssm_b_im[0])
    c_blocks = lambda c: c.reshape(SSM_BLOCKS, LANES, SSM_STATE)
    d_row = row2(ssm_d[0])
    to_blocks = lambda h: h.reshape(bs, SSM_GROUPS * SSM_STATE)
    from_blocks = lambda h: h.reshape(bs, SSM_GROUPS, SSM_STATE)
    yp, hrp, hip, ysm, hrs, his = _ssm(up.reshape(bp, lp, -1), us.reshape(1, ts, -1), to_blocks(state_ssm_re[0]),
                                       to_blocks(state_ssm_im[0]), bbar, c_blocks(ssm_c_re[0]),
                                       c_blocks(ssm_c_im[0]), steps, lev, d_row)

    wr = jnp.zeros((D_MODEL, ROUTER_COLS), f32)
    wr = wr.at[:, :N_EXPERTS].set(w_router_expert[0]).at[:, N_EXPERTS:N_EXPERTS + N_EXPERT_GROUPS].set(w_router_group[0])
    br = jnp.zeros((1, ROUTER_COLS), f32)
    br = br.at[0, :N_EXPERTS].set(b_router_expert[0]).at[0, N_EXPERTS:N_EXPERTS + N_EXPERT_GROUPS].set(b_router_group[0])
    tri = jnp.tril(jnp.ones((ROW_TILE, ROW_TILE), bf16), -1)
    upper = jnp.triu(jnp.ones((N_EXPERTS, N_EXPERTS), bf16), 1)
    mix_w = (w_glu[0].astype(bf16), row2(b_glu[0]), row2(g_ssm_out[0]), w_out[0].astype(bf16), row2(g_norm_ffn[0]),
             wr.astype(bf16), br, tri, upper)
    x1, xl, wts, lpos, n_rows, n_off = _mix(ap.reshape(tp, -1), a_s.reshape(ts, -1), yp.reshape(tp, -1),
                                            ysm.reshape(ts, -1), xp, xs, *mix_w)

    per_tile = lambda v: v.reshape(-1, N_EXPERTS).astype(i32)
    tiles = (SLOTS * (tp + ts)) // ROW_TILE + N_EXPERTS
    expert_tables, combine_tables = _moe_tables(per_tile(n_rows), per_tile(n_off), tiles)
    expert_out = _experts(expert_tables, xl, w_exp_gate[0], w_exp_up[0], w_exp_down[0], tiles)
    y_p, y_s = _combine(combine_tables, expert_out, x1, wts, lpos, row2(g_final), tp)

    kvshape = lambda a, b: a.reshape(1, b, -1, N_KV_HEADS, HEAD_DIM)
    block_state = lambda h: h.reshape(bp, SSM_GROUPS, SSM_STATE)[None]
    wcp = min(WINDOW, lp)
    return (y_p.reshape(bp, lp, D_MODEL), y_s.reshape(bs, ls, D_MODEL),
            kvshape(kp.reshape(bp, lp, -1)[:, lp - wcp:], bp), kvshape(vp.reshape(bp, lp, -1)[:, lp - wcp:], bp),
            block_state(hrp), block_state(hip),
            kvshape(k_roll, bs), kvshape(v_roll, bs),
            from_blocks(hrs)[None], from_blocks(his)[None])
```

```python
import functools
import math

import jax
import jax.numpy as jnp
from jax import lax
from jax.experimental import pallas as pl
from jax.experimental.pallas import tpu as pltpu

f32, bf16, i32 = jnp.float32, jnp.bfloat16, jnp.int32

D_MODEL = 1024
CHUNK = 64
N_BACK = 2
WINDOW = 128
ATTN_WIDTH = 512
HEAD_DIM = 64
N_KV_HEADS = 2
Q_PER_KV = 4
KV_WIDTH = 128
SSM_WIDTH = 512
SSM_GROUP = 16
SSM_GROUPS = 32
SSM_STATE = 64
N_EXPERT_GROUPS = 4
EXPERTS_PER_GROUP = 8
N_EXPERTS = 32
EPS = 1e-6
NEG = -1e30

LANES = 128
SSM_STEPS = 16
SSM_SEQS = 2
SSM_BLOCKS = SSM_WIDTH // LANES
GROUPS_PER_BLOCK = LANES // SSM_GROUP
STATE_COLS = GROUPS_PER_BLOCK * SSM_STATE
ROW_TILE = 256
SLOTS = 2
PIECES = D_MODEL // LANES
RUN_PIECES = tuple(1 << b for b in reversed(range(int(math.log2(ROW_TILE)) + 1)))
SHORT_PIECES = tuple(p for p in RUN_PIECES if p <= 32)
LONG_PIECES = tuple(p for p in RUN_PIECES if p > 32)
MIX_TILES = 2
UNROLLED_RUNS = 20
EXPERT_RING = 4
COMBINE_RING = 3
VMEM_LIMIT = 56 * 1024 * 1024


def _cparams(n_axes=1, limit=VMEM_LIMIT):
    return pltpu.CompilerParams(dimension_semantics=("arbitrary",) * n_axes, vmem_limit_bytes=limit)


def _rms(x, g):
    return x * lax.rsqrt(jnp.mean(x * x, axis=-1, keepdims=True) + EPS) * g


def _bdot(a, b):
    return jnp.dot(a.astype(bf16), b.astype(bf16), preferred_element_type=f32)


def _proj_body(x_ref, g_ref, w_ref, q_ref, k_ref, v_ref, u_ref):
    h = _rms(x_ref[...], g_ref[...])
    z = _bdot(h, w_ref[...])
    q_ref[...] = z[:, :ATTN_WIDTH] * (HEAD_DIM ** -0.5)
    k_ref[...] = z[:, ATTN_WIDTH:ATTN_WIDTH + KV_WIDTH]
    v_ref[...] = z[:, ATTN_WIDTH + KV_WIDTH:ATTN_WIDTH + 2 * KV_WIDTH]
    u_ref[...] = z[:, ATTN_WIDTH + 2 * KV_WIDTH:]


def _proj(x2d, g, w_bf, tm):
    t = x2d.shape[0]
    row = lambda n: pl.BlockSpec((tm, n), lambda i: (i, 0))
    full = lambda a: pl.BlockSpec(a.shape, lambda i: (0,) * a.ndim)
    return pl.pallas_call(
        _proj_body,
        grid=(t // tm,),
        in_specs=[row(D_MODEL), full(g), full(w_bf)],
        out_specs=[row(ATTN_WIDTH), row(KV_WIDTH), row(KV_WIDTH), row(SSM_WIDTH)],
        out_shape=[jax.ShapeDtypeStruct((t, n), f32) for n in (ATTN_WIDTH, KV_WIDTH, KV_WIDTH, SSM_WIDTH)],
        compiler_params=_cparams(),
        name="proj",
    )(x2d, g, w_bf)


def _sink_column(sink_ref, kv, rows_per_head):
    r = lax.broadcasted_iota(i32, (Q_PER_KV * rows_per_head, 1), 0)
    col = jnp.full((Q_PER_KV * rows_per_head, 1), sink_ref[kv * Q_PER_KV], f32)
    for j in range(1, Q_PER_KV):
        col = jnp.where(r >= j * rows_per_head, sink_ref[kv * Q_PER_KV + j], col)
    return col


def _attend(qs, kc, vc, sink_col, valid):
    s = lax.dot_general(qs.astype(bf16), kc.astype(bf16), (((1,), (1,)), ((), ())), preferred_element_type=f32)
    if valid is not None:
        s = jnp.where(valid, s, NEG)
    m = jnp.maximum(jnp.max(s, axis=-1, keepdims=True), sink_col)
    p = jnp.exp(s - m)
    denom = jnp.sum(p, axis=-1, keepdims=True) + jnp.exp(sink_col - m)
    return _bdot(p, vc) / denom


def _heads_attend(q, k, v, sink_ref, valid):
    rows = q.shape[0]
    pieces = []
    for kv in range(N_KV_HEADS):
        qs = jnp.concatenate(
            [q[:, (kv * Q_PER_KV + j) * HEAD_DIM:(kv * Q_PER_KV + j + 1) * HEAD_DIM] for j in range(Q_PER_KV)], axis=0)
        o = _attend(qs, k[:, kv * HEAD_DIM:(kv + 1) * HEAD_DIM], v[:, kv * HEAD_DIM:(kv + 1) * HEAD_DIM],
                    _sink_column(sink_ref, kv, rows), valid)
        pieces += [o[j * rows:(j + 1) * rows] for j in range(Q_PER_KV)]
    return jnp.concatenate(pieces, axis=1)


ATTN_TILE = 256
CHUNKS_PER_TILE = ATTN_TILE // CHUNK
KEY_SPAN = (N_BACK + 1) * CHUNK


def _attn_prompt_body(sink_ref, q_ref, kp_ref, kc_ref, vp_ref, vc_ref, g_ref, o_ref):
    i = pl.program_id(1)
    kwin = jnp.concatenate([kp_ref[0], kc_ref[0]], axis=0)
    vwin = jnp.concatenate([vp_ref[0], vc_ref[0]], axis=0)
    key_chunk = lax.broadcasted_iota(i32, (1, KEY_SPAN), 1) // CHUNK
    for c in range(CHUNKS_PER_TILE):
        valid = (i * CHUNKS_PER_TILE + c - N_BACK + key_chunk) >= 0
        o = _heads_attend(q_ref[0, c * CHUNK:(c + 1) * CHUNK, :], kwin[c * CHUNK:c * CHUNK + KEY_SPAN],
                          vwin[c * CHUNK:c * CHUNK + KEY_SPAN], sink_ref, valid)
        o_ref[0, c * CHUNK:(c + 1) * CHUNK, :] = _rms(o, g_ref[...]).astype(bf16)


def _attn_prompt(q, k, v, sinks, g):
    b, l, _ = q.shape
    back = N_BACK * CHUNK
    per = ATTN_TILE // back
    prev = pl.BlockSpec((1, back, KV_WIDTH), lambda bi, i: (bi, jnp.maximum(i * per - 1, 0), 0))
    cur = pl.BlockSpec((1, ATTN_TILE, KV_WIDTH), lambda bi, i: (bi, i, 0))
    return pl.pallas_call(
        _attn_prompt_body,
        grid=(b, l // ATTN_TILE),
        in_specs=[pl.BlockSpec(memory_space=pltpu.SMEM),
                  pl.BlockSpec((1, ATTN_TILE, ATTN_WIDTH), lambda bi, i: (bi, i, 0)),
                  prev, cur, prev, cur,
                  pl.BlockSpec((1, ATTN_WIDTH), lambda bi, i: (0, 0))],
        out_specs=pl.BlockSpec((1, ATTN_TILE, ATTN_WIDTH), lambda bi, i: (bi, i, 0)),
        out_shape=jax.ShapeDtypeStruct((b, l, ATTN_WIDTH), bf16),
        compiler_params=_cparams(2),
        name="attn_prompt",
    )(sinks, q, k, k, v, v, g)


SAMPLE_STREAMS = 16


def _attn_sample_body(sink_ref, q_ref, kn_ref, vn_ref, ck_ref, cv_ref, g_ref, o_ref, nk_ref, nv_ref):
    n_new = kn_ref.shape[1]
    for b in range(q_ref.shape[0]):
        kall = jnp.concatenate([ck_ref[b], kn_ref[b]], axis=0)
        vall = jnp.concatenate([cv_ref[b], vn_ref[b]], axis=0)
        o = _heads_attend(q_ref[b], kall, vall, sink_ref, None)
        o_ref[b] = _rms(o, g_ref[...]).astype(bf16)
        nk_ref[b] = kall[n_new:]
        nv_ref[b] = vall[n_new:]


def _attn_sample(q, k_new, v_new, cache_k, cache_v, sinks, g):
    b, l, _ = q.shape
    wc = cache_k.shape[1]
    assert b % SAMPLE_STREAMS == 0
    blk = lambda r, n: pl.BlockSpec((SAMPLE_STREAMS, r, n), lambda bi: (bi, 0, 0))
    return pl.pallas_call(
        _attn_sample_body,
        grid=(b // SAMPLE_STREAMS,),
        in_specs=[pl.BlockSpec(memory_space=pltpu.SMEM), blk(l, ATTN_WIDTH), blk(l, KV_WIDTH), blk(l, KV_WIDTH),
                  blk(wc, KV_WIDTH), blk(wc, KV_WIDTH), pl.BlockSpec((1, ATTN_WIDTH), lambda bi: (0, 0))],
        out_specs=[blk(l, ATTN_WIDTH), blk(wc, KV_WIDTH), blk(wc, KV_WIDTH)],
        out_shape=[jax.ShapeDtypeStruct((b, l, ATTN_WIDTH), bf16),
                   jax.ShapeDtypeStruct((b, wc, KV_WIDTH), f32), jax.ShapeDtypeStruct((b, wc, KV_WIDTH), f32)],
        compiler_params=_cparams(),
        name="attn_sample",
    )(sinks, q, k_new, v_new, cache_k, cache_v, g)


def _ssm_tables(a_re, a_im, log_dt, b_re, b_im):
    dt = jnp.exp(log_dt)[:, None]
    lam_r, lam_i = a_re * dt, a_im * dt

    def power(n):
        mag = jnp.exp(n * lam_r)
        return jnp.stack([mag * jnp.cos(n * lam_i), mag * jnp.sin(n * lam_i)])

    ar, ai = power(1.0)
    den = a_re * a_re + a_im * a_im
    nr, ni = ar - 1.0, ai
    fr = ((nr * a_re + ni * a_im) / den)[..., None]
    fi = ((ni * a_re - nr * a_im) / den)[..., None]
    bbar = jnp.stack([fr * b_re - fi * b_im, fr * b_im + fi * b_re])
    bbar = bbar.transpose(0, 1, 3, 2).reshape(2, SSM_BLOCKS, LANES, SSM_STATE)
    by_block = lambda t: t.reshape(t.shape[:-2] + (SSM_BLOCKS, STATE_COLS))
    steps = by_block(power(jnp.arange(SSM_STEPS + 1, dtype=f32)[:, None, None]))
    levels = by_block(power(SSM_STEPS * 2.0 ** jnp.arange(8, dtype=f32)[:, None, None]))
    return bbar, steps.transpose(2, 0, 1, 3), levels.transpose(2, 1, 0, 3)


def _ssm_chunk_rows(u_ref, nk):
    xs = [u_ref[0, pl.ds(s, nk, stride=SSM_STEPS), :] for s in range(SSM_STEPS)]
    pairs = [jnp.concatenate([xs[2 * p], xs[2 * p + 1]], axis=1).astype(bf16) for p in range(SSM_STEPS // 2)]
    return xs, pairs


def _ssm_intra(pairs, toep_ref, nk):
    nd = len(pairs)
    y = [None] * nd
    stack = nk < LANES
    for d in range(nd):
        if stack:
            r = jnp.dot(jnp.concatenate(pairs[:nd - d], axis=0), toep_ref[d], preferred_element_type=f32)
        for p in range(nd - d):
            blk = r[p * nk:(p + 1) * nk] if stack else jnp.dot(pairs[p], toep_ref[d], preferred_element_type=f32)
            y[p + d] = blk if y[p + d] is None else y[p + d] + blk
    return y


def _shift_rows(x, sh):
    rows = lax.broadcasted_iota(i32, (x.shape[0], 1), 0)
    return jnp.where(rows >= sh, pltpu.roll(x, sh, axis=0), 0.0)


def _group_index(shape, axis, sub):
    idx = lax.broadcasted_iota(i32, shape, axis)
    return jnp.right_shift(idx, int(math.log2(sub))) & (GROUPS_PER_BLOCK - 1)


def _build_ssm_weights(bb_ref, cr_ref, ci_ref, pw_ref, toep_s, wout_s, win_s, w_low):
    hp = lax.Precision.HIGHEST
    k = lax.broadcasted_iota(i32, (SSM_STATE, STATE_COLS), 0)
    n = lax.broadcasted_iota(i32, (SSM_STATE, STATE_COLS), 1)
    spread = ((n & (SSM_STATE - 1)) == k).astype(f32)
    own = _group_index((LANES, STATE_COLS), 0, SSM_GROUP) == _group_index((LANES, STATE_COLS), 1, SSM_STATE)

    def block_diag(compact):
        return jnp.where(own, jnp.dot(compact, spread, precision=hp, preferred_element_type=f32), 0.0)

    bb_r, bb_i = block_diag(bb_ref[0, 0]), block_diag(bb_ref[1, 0])
    c_r, c_i = block_diag(cr_ref[0]), block_diag(ci_ref[0])
    c_stack = jnp.concatenate([c_r.T, -c_i.T], axis=0)
    power = lambda e: (pw_ref[0, 0, e:e + 1, :], pw_ref[0, 1, e:e + 1, :])
    for s in range(SSM_STEPS):
        rows = slice(s * LANES, (s + 1) * LANES)
        pr, pi = power(SSM_STEPS - 1 - s)
        w = jnp.concatenate([bb_r * pr - bb_i * pi, bb_r * pi + bb_i * pr], axis=1)
        wout_s[rows, :] = w.astype(bf16)
        w_low[rows, :] = (w - w.astype(bf16).astype(f32)).astype(bf16)
        pr, pi = power(s + 1)
        g = jnp.concatenate([c_r * pr - c_i * pi, -(c_r * pi + c_i * pr)], axis=1)
        win_s[:, rows] = g.T.astype(bf16)
    c_high = c_stack.astype(bf16)
    c_low = (c_stack - c_high.astype(f32)).astype(bf16)
    lags = (jnp.dot(wout_s[...], c_high, preferred_element_type=f32)
            + (jnp.dot(wout_s[...], c_low, preferred_element_type=f32)
               + jnp.dot(w_low[...], c_high, preferred_element_type=f32)))
    lag_kernel = [lags[(SSM_STEPS - 1 - lag) * LANES:(SSM_STEPS - lag) * LANES] for lag in range(SSM_STEPS)]
    zero = jnp.zeros((LANES, LANES), f32)
    for d in range(SSM_STEPS // 2):
        top = jnp.concatenate([lag_kernel[2 * d], lag_kernel[2 * d + 1]], axis=1)
        bottom = jnp.concatenate([lag_kernel[2 * d - 1] if d > 0 else zero, lag_kernel[2 * d]], axis=1)
        toep_s[d] = jnp.concatenate([top, bottom], axis=0).astype(bf16)


def _ssm_chunks(u_ref, d_ref, y_ref, toep_s, wout_s, win_s, entry_state):
    nk = u_ref.shape[1] // SSM_STEPS
    xs, pairs = _ssm_chunk_rows(u_ref, nk)
    y = _ssm_intra(pairs, toep_s, nk)
    s = jnp.dot(jnp.concatenate(pairs, axis=1), wout_s[...], preferred_element_type=f32)
    hprev, hr, hi = entry_state(s[:, :STATE_COLS], s[:, STATE_COLS:])
    y2 = _bdot(hprev, win_s[...])
    for st in range(SSM_STEPS):
        piece = (y[st // 2][:, (st % 2) * LANES:(st % 2 + 1) * LANES] + y2[:, st * LANES:(st + 1) * LANES]
                 + d_ref[...] * xs[st])
        y_ref[0, pl.ds(st, nk, stride=SSM_STEPS), :] = piece
    return hr, hi


def _ssm_body(u_ref, us_ref, h0r_ref, h0i_ref, bb_ref, cr_ref, ci_ref, pw_ref, lev_ref, d_ref,
              y_ref, hr_ref, hi_ref, ys_ref, hrs_ref, his_ref, toep_s, wout_s, win_s, w_low):
    @pl.when(pl.program_id(1) == 0)
    def _():
        _build_ssm_weights(bb_ref, cr_ref, ci_ref, pw_ref, toep_s, wout_s, win_s, w_low)

        def one_chunk(sr, si):
            h0r, h0i = h0r_ref[...], h0i_ref[...]
            ar, ai = lev_ref[0, 0, 0:1, :], lev_ref[0, 0, 1:2, :]
            return (jnp.concatenate([h0r, h0i], axis=1), sr + ar * h0r - ai * h0i, si + ar * h0i + ai * h0r)
        hrs_ref[...], his_ref[...] = _ssm_chunks(us_ref, d_ref, ys_ref, toep_s, wout_s, win_s, one_chunk)

    def scan_chunks(sr, si):
        nk = sr.shape[0]
        level = 0
        while (1 << level) < nk:
            ar, ai = lev_ref[0, level, 0:1, :], lev_ref[0, level, 1:2, :]
            tr, ti = _shift_rows(sr, 1 << level), _shift_rows(si, 1 << level)
            sr, si = sr + ar * tr - ai * ti, si + ar * ti + ai * tr
            level += 1
        return (jnp.concatenate([_shift_rows(sr, 1), _shift_rows(si, 1)], axis=1), sr[nk - 1:nk], si[nk - 1:nk])
    for b in range(u_ref.shape[0]):
        hr_ref[b, 0], hi_ref[b, 0] = _ssm_chunks(u_ref.at[pl.ds(b, 1)], d_ref, y_ref.at[pl.ds(b, 1)], toep_s, wout_s,
                                                 win_s, scan_chunks)


def _ssm(u, us, h0r, h0i, bbar, c_re, c_im, steps, lev, d):
    b, l, _ = u.shape
    rows = us.shape[1]
    nb = rows // SSM_STEPS
    wspec = lambda a: pl.BlockSpec((1,) + a.shape[1:], lambda j, bi: (j,) + (0,) * (a.ndim - 1))
    assert b % SSM_SEQS == 0
    st = pl.BlockSpec((SSM_SEQS, 1, 1, STATE_COLS), lambda j, bi: (bi, j, 0, 0))
    sst = pl.BlockSpec((nb, STATE_COLS), lambda j, bi: (0, j))
    seq = pl.BlockSpec((SSM_SEQS, l, LANES), lambda j, bi: (bi, 0, j))
    sseq = pl.BlockSpec((1, rows, LANES), lambda j, bi: (0, 0, j))
    return pl.pallas_call(
        _ssm_body,
        grid=(SSM_BLOCKS, b // SSM_SEQS),
        in_specs=[seq, sseq, sst, sst, pl.BlockSpec((2, 1, LANES, SSM_STATE), lambda j, bi: (0, j, 0, 0)),
                  wspec(c_re), wspec(c_im), wspec(steps), wspec(lev), pl.BlockSpec((1, LANES), lambda j, bi: (0, j))],
        out_specs=[seq, st, st, sseq, sst, sst],
        out_shape=[jax.ShapeDtypeStruct((b, l, SSM_WIDTH), f32),
                   jax.ShapeDtypeStruct((b, SSM_BLOCKS, 1, STATE_COLS), f32),
                   jax.ShapeDtypeStruct((b, SSM_BLOCKS, 1, STATE_COLS), f32),
                   jax.ShapeDtypeStruct((1, rows, SSM_WIDTH), f32),
                   jax.ShapeDtypeStruct((nb, SSM_BLOCKS * STATE_COLS), f32),
                   jax.ShapeDtypeStruct((nb, SSM_BLOCKS * STATE_COLS), f32)],
        scratch_shapes=[pltpu.VMEM((SSM_STEPS // 2, 2 * LANES, 2 * LANES), bf16),
                        pltpu.VMEM((SSM_STEPS * LANES, 2 * STATE_COLS), bf16),
                        pltpu.VMEM((2 * STATE_COLS, SSM_STEPS * LANES), bf16),
                        pltpu.VMEM((SSM_STEPS * LANES, 2 * STATE_COLS), bf16)],
        compiler_params=_cparams(2),
        name="ssm",
    )(u, us, h0r, h0i, bbar, c_re, c_im, steps, lev, d)


ROUTER_COLS = LANES


def _mix_body(ap_ref, as_ref, yp_ref, ys_ref, xp_ref, xs_ref, wglu_ref, bglu_ref, gs_ref, wout_ref, gf_ref, wr_ref,
              br_ref, tri_ref, upper_ref, x1_ref, xl_ref, wts_ref, lpos_ref, n_ref, loff_ref, *, prompt_steps):
    is_prompt = pl.program_id(0) < prompt_steps
    pick_rows = lambda p_ref, s_ref: jnp.where(is_prompt, p_ref[...], jnp.concatenate([s_ref[...]] * MIX_TILES, axis=0))
    y = pick_rows(yp_ref, ys_ref)
    y = 0.5 * y * (1.0 + jnp.tanh(math.sqrt(2.0 / math.pi) * (y + 0.044715 * (y * y * y))))
    y = y * jax.nn.sigmoid(_bdot(y, wglu_ref[...]) + bglu_ref[...])
    attn = jnp.where(is_prompt, ap_ref[...].astype(f32),
                     jnp.concatenate([as_ref[...].astype(f32)] * MIX_TILES, axis=0)).astype(bf16)
    cat = jnp.concatenate([attn, _rms(y, gs_ref[...]).astype(bf16)], axis=1)
    x1 = pick_rows(xp_ref, xs_ref) + jnp.dot(cat, wout_ref[...], preferred_element_type=f32)
    x1_ref[...] = x1
    hf = _rms(x1, gf_ref[...])

    logits = _bdot(hf, wr_ref[...]) + br_ref[...]
    le = logits[:, :N_EXPERTS]
    lg = logits[:, N_EXPERTS:N_EXPERTS + N_EXPERT_GROUPS]
    tm = le.shape[0]
    gmax = jnp.max(lg, axis=-1, keepdims=True)
    gi = lax.broadcasted_iota(i32, (tm, N_EXPERT_GROUPS), 1).astype(f32)
    gsel = jnp.min(jnp.where(lg == gmax, gi, float(N_EXPERT_GROUPS)), axis=-1, keepdims=True)
    pg = 1.0 / jnp.sum(jnp.exp(lg - gmax), axis=-1, keepdims=True)
    ei_int = lax.broadcasted_iota(i32, (tm, N_EXPERTS), 1)
    ei = ei_int.astype(f32)
    egroup = jnp.right_shift(ei_int, int(math.log2(EXPERTS_PER_GROUP))).astype(f32)
    lm = jnp.where(egroup == gsel, le, NEG)
    v1 = jnp.max(lm, axis=-1, keepdims=True)
    i1 = jnp.min(jnp.where(lm == v1, ei, float(N_EXPERTS)), axis=-1, keepdims=True)
    lm2 = jnp.where(ei == i1, NEG, lm)
    v2 = jnp.max(lm2, axis=-1, keepdims=True)
    i2 = jnp.min(jnp.where(lm2 == v2, ei, float(N_EXPERTS)), axis=-1, keepdims=True)
    ex = jnp.exp(v2 - v1)
    wts_ref[...] = jnp.concatenate([pg / (1.0 + ex), pg * ex / (1.0 + ex)], axis=1)

    oh1 = (ei == i1).astype(f32)
    oh2 = (ei == i2).astype(f32)
    hf_bf = hf.astype(bf16)
    sorted_row = lax.broadcasted_iota(i32, (ROW_TILE, SLOTS * ROW_TILE), 1).astype(f32)
    for h in range(MIX_TILES):
        rows = slice(h * ROW_TILE, (h + 1) * ROW_TILE)
        both = (oh1[rows] + oh2[rows]).astype(bf16)
        before = jnp.dot(tri_ref[...], both, preferred_element_type=f32)
        count = jnp.sum(oh1[rows] + oh2[rows], axis=0, keepdims=True)
        lower = jnp.sum(jnp.dot(both, upper_ref[...], preferred_element_type=f32), axis=0, keepdims=True)
        lp1 = jnp.sum(oh1[rows] * (before + lower), axis=-1, keepdims=True)
        lp2 = jnp.sum(oh2[rows] * (before + lower), axis=-1, keepdims=True)
        lpos_ref[rows, :] = jnp.concatenate([lp1, lp2], axis=1)
        n_ref[h] = count
        loff_ref[h] = lower
        pick = ((sorted_row == lp1) | (sorted_row == lp2)).astype(bf16)
        xl = lax.dot_general(pick, hf_bf[rows], (((0,), (0,)), ((), ())), preferred_element_type=f32)
        _store_row_major(xl_ref, h * SLOTS * ROW_TILE, xl)


def _mix(a_p, a_s, ys_p, ys_s, x_p, x_s, wglu, bglu, gs, wout, gf, wr, br, tri, upper):
    tm = MIX_TILES * ROW_TILE
    assert x_p.shape[0] % tm == 0 and x_s.shape[0] == ROW_TILE
    prompt_steps = x_p.shape[0] // tm
    t = x_p.shape[0] + x_s.shape[0]
    p_rows = lambda n: pl.BlockSpec((tm, n), lambda i: (jnp.minimum(i, prompt_steps - 1), 0))
    s_rows = lambda n: pl.BlockSpec((ROW_TILE, n), lambda i: (0, 0))
    row = lambda n: pl.BlockSpec((tm, n), lambda i: (i, 0))
    full = lambda arr: pl.BlockSpec(arr.shape, lambda i: (0,) * arr.ndim)
    per_tile = pl.BlockSpec((MIX_TILES, 1, N_EXPERTS), lambda i: (i, 0, 0))
    return pl.pallas_call(
        functools.partial(_mix_body, prompt_steps=prompt_steps),
        grid=(prompt_steps + 1,),
        in_specs=[p_rows(ATTN_WIDTH), s_rows(ATTN_WIDTH), p_rows(SSM_WIDTH), s_rows(SSM_WIDTH), p_rows(D_MODEL),
                  s_rows(D_MODEL), full(wglu), full(bglu), full(gs), full(wout), full(gf), full(wr), full(br),
                  full(tri), full(upper)],
        out_specs=[row(D_MODEL), pl.BlockSpec((SLOTS * tm * PIECES, LANES), lambda i: (i, 0)), row(2), row(2),
                   per_tile, per_tile],
        out_shape=[jax.ShapeDtypeStruct((t, D_MODEL), f32),
                   jax.ShapeDtypeStruct((SLOTS * t * PIECES, LANES), f32),
                   jax.ShapeDtypeStruct((t, 2), f32), jax.ShapeDtypeStruct((t, 2), f32),
                   jax.ShapeDtypeStruct((t // ROW_TILE, 1, N_EXPERTS), f32),
                   jax.ShapeDtypeStruct((t // ROW_TILE, 1, N_EXPERTS), f32)],
        compiler_params=_cparams(),
        name="mix",
    )(a_p, a_s, ys_p, ys_s, x_p, x_s, wglu, bglu, gs, wout, gf, wr, br, tri, upper)


def _store_row_major(ref, first_row, x):
    for c in range(PIECES):
        ref[pl.ds(first_row * PIECES + c, x.shape[0], stride=PIECES), :] = x[:, c * LANES:(c + 1) * LANES]


def _load_row_major(ref, n_rows):
    return jnp.concatenate([ref[pl.ds(c, n_rows, stride=PIECES), :] for c in range(PIECES)], axis=1)


def _copy_rows(src, s_row, dst, d_row, n_rows, sem):
    return pltpu.make_async_copy(src.at[pl.ds(pl.multiple_of(s_row * PIECES, PIECES), n_rows * PIECES), :],
                                 dst.at[pl.ds(pl.multiple_of(d_row * PIECES, PIECES), n_rows * PIECES), :], sem)


def _for_each_piece(n, fn, pieces=RUN_PIECES):
    off = 0 if pieces[0] == RUN_PIECES[0] else n & ~(2 * pieces[0] - 1)
    for piece in pieces:
        @pl.when((n & piece) != 0)
        def _(off=off, piece=piece):
            fn(off, piece)
        off = off + (n & piece)


def _experts_body(te_ref, tpos_ref, tvalid_ref, tlo_ref, thi_ref, wslot_ref, wnext_ref, n_ref, cum_ref, loff_ref,
                  xl_hbm, wg_hbm, wu_hbm, wd_hbm, o_ref, wg_s, wu_s, wd_s, wg_f, wu_f, wd_f, wsem, sem, *xbufs):
    i = pl.program_id(0)
    last = pl.num_programs(0) - 1
    token_tiles = n_ref.shape[0] // N_EXPERTS
    ring = len(xbufs)

    def start_run(t, tau, enabled, buf, buf_sem):
        e, lo = te_ref[t], tpos_ref[t]
        k = tau * N_EXPERTS + e
        s, n = cum_ref[k], n_ref[k]
        a = jnp.maximum(s, lo)
        length = jnp.where(enabled, jnp.maximum(jnp.minimum(s + n, lo + ROW_TILE) - a, 0), 0)
        local = loff_ref[k] + (a - s)
        _for_each_piece(length, lambda off, piece: _copy_rows(
            xl_hbm, tau * (SLOTS * ROW_TILE) + local + off, buf, a - lo + off, piece, buf_sem).start())

    def start_runs_loop(t, first, stop, buf, buf_sem):
        def run(tau, c):
            start_run(t, tau, True, buf, buf_sem)
            return c
        lax.fori_loop(first, stop, run, 0)

    def clear(buf):
        buf[...] = jnp.zeros_like(buf)

    @pl.when(i == 0)
    def _():
        for buf in xbufs:
            clear(buf)
        for t in range(ring - 1):
            @pl.when(tvalid_ref[t] > 0)
            def _(t=t):
                start_runs_loop(t, tlo_ref[t], thi_ref[t] + 1, xbufs[t], sem.at[t])

    valid = tvalid_ref[i]

    def weight_copies(e, slot):
        return [pltpu.make_async_copy(w_hbm.at[e], w_f.at[slot], wsem.at[slot])
                for w_hbm, w_f in ((wg_hbm, wg_f), (wu_hbm, wu_f), (wd_hbm, wd_f))]

    @pl.when((i == 0) & (valid > 0))
    def _():
        for cp in weight_copies(te_ref[0], wslot_ref[0]):
            cp.start()

    @pl.when((valid > 0) & ((i == 0) | (te_ref[i] != te_ref[jnp.maximum(i - 1, 0)])))
    def _():
        slot = wslot_ref[i]
        for cp in weight_copies(te_ref[i], slot):
            cp.wait()
        wg_s[...] = wg_f[slot].astype(bf16)
        wu_s[...] = wu_f[slot].astype(bf16)
        wd_s[...] = wd_f[slot].astype(bf16)

        @pl.when(wnext_ref[i] >= 0)
        def _():
            for cp in weight_copies(wnext_ref[i], 1 - slot):
                cp.start()

    def tile_step(cur):
        ahead = (cur + ring - 1) % ring
        buf, buf_sem, next_buf, next_sem = xbufs[cur], sem.at[cur], xbufs[ahead], sem.at[ahead]
        _for_each_piece(valid, lambda off, piece: _copy_rows(
            xl_hbm, 0, buf, 0, piece, buf_sem).wait())

        nxt = jnp.minimum(i + ring - 1, last)
        go = (i + ring - 1 <= last) & (tvalid_ref[nxt] > 0)
        first, final = tlo_ref[nxt], thi_ref[nxt]
        for j in range(UNROLLED_RUNS):
            start_run(nxt, jnp.minimum(first + j, token_tiles - 1), go & (first + j <= final), next_buf, next_sem)

        x = _load_row_major(buf, ROW_TILE).astype(bf16)
        clear(buf)
        hg = jnp.dot(x, wg_s[...], preferred_element_type=f32)
        hu = jnp.dot(x, wu_s[...], preferred_element_type=f32)
        y = jnp.dot((hg * jax.nn.sigmoid(hg) * hu).astype(bf16), wd_s[...], preferred_element_type=f32)
        _store_row_major(o_ref, 0, y)

        @pl.when(go & (final - first >= UNROLLED_RUNS))
        def _():
            start_runs_loop(nxt, first + UNROLLED_RUNS, final + 1, next_buf, next_sem)

    for cur in range(ring):
        pl.when((valid > 0) & (i % ring == cur))(functools.partial(tile_step, cur))

    @pl.when(valid == 0)
    def _():
        o_ref[...] = jnp.zeros_like(o_ref)


def _experts(tables, xl, wg, wu, wd, tiles):
    hbm = pl.BlockSpec(memory_space=pl.ANY)
    return pl.pallas_call(
        _experts_body,
        grid_spec=pltpu.PrefetchScalarGridSpec(
            num_scalar_prefetch=len(tables),
            grid=(tiles,),
            in_specs=[hbm, hbm, hbm, hbm],
            out_specs=pl.BlockSpec((ROW_TILE * PIECES, LANES), lambda i, *_: (i, 0)),
            scratch_shapes=[pltpu.VMEM(w.shape[1:], bf16) for w in (wg, wu, wd)]
            + [pltpu.VMEM((2,) + w.shape[1:], f32) for w in (wg, wu, wd)]
            + [pltpu.SemaphoreType.DMA((2,)), pltpu.SemaphoreType.DMA((EXPERT_RING,))]
            + [pltpu.VMEM((ROW_TILE * PIECES, LANES), f32)] * EXPERT_RING),
        out_shape=jax.ShapeDtypeStruct((tiles * ROW_TILE * PIECES, LANES), f32),
        compiler_params=_cparams(),
        name="moe_experts",
    )(*tables, xl, wg, wu, wd)


def _combine_body(n_ref, gpos_ref, loff_ref, ys_hbm, x_ref, w_ref, l_ref, g_ref, op_ref, os_ref, sem, *ybufs,
                  prompt_tiles):
    i = pl.program_id(0)
    last = pl.num_programs(0) - 1
    tile_rows = SLOTS * ROW_TILE
    ring = len(ybufs)

    def start_run(t, e, enabled, buf, buf_sem, pieces=RUN_PIECES):
        k = t * N_EXPERTS + e
        _for_each_piece(jnp.where(enabled, n_ref[k], 0), lambda off, piece: _copy_rows(
            ys_hbm, gpos_ref[k] + off, buf, loff_ref[k] + off, piece, buf_sem).start(), pieces)

    @pl.when(i == 0)
    def _():
        for t in range(ring - 1):
            def run(e, c, t=t):
                start_run(t, e, True, ybufs[t], sem.at[t])
                return c
            lax.fori_loop(0, N_EXPERTS, run, 0)

    def step(cur):
        ahead = (cur + ring - 1) % ring
        buf, buf_sem, next_buf, next_sem = ybufs[cur], sem.at[cur], ybufs[ahead], sem.at[ahead]
        _copy_rows(ys_hbm, 0, buf, 0, tile_rows, buf_sem).wait()
        nxt, go = jnp.minimum(i + ring - 1, last), i + ring - 1 <= last
        longest = 0
        for e in range(N_EXPERTS):
            start_run(nxt, e, go, next_buf, next_sem, SHORT_PIECES)
            longest = longest | n_ref[nxt * N_EXPERTS + e]
        yl = _load_row_major(buf, tile_rows).astype(bf16)
        sorted_row = lax.broadcasted_iota(i32, (ROW_TILE, tile_rows), 1).astype(f32)
        w, lp = w_ref[...], l_ref[...]
        y1 = jnp.dot((sorted_row == lp[:, 0:1]).astype(bf16), yl, preferred_element_type=f32)
        y2 = jnp.dot((sorted_row == lp[:, 1:2]).astype(bf16), yl, preferred_element_type=f32)
        out = _rms(x_ref[...] + (w[:, 0:1] * y1 + w[:, 1:2] * y2), g_ref[...])

        @pl.when(i < prompt_tiles)
        def _():
            op_ref[...] = out

        @pl.when(i >= prompt_tiles)
        def _():
            os_ref[...] = out

        @pl.when(go & (longest >= 2 * SHORT_PIECES[0]))
        def _():
            def long_pieces(e, c):
                start_run(nxt, e, True, next_buf, next_sem, LONG_PIECES)
                return c
            lax.fori_loop(0, N_EXPERTS, long_pieces, 0)

    for cur in range(ring):
        pl.when(i % ring == cur)(functools.partial(step, cur))


def _combine(tables, ys, x1, wts, lpos, g, prompt_rows):
    prompt_tiles = prompt_rows // ROW_TILE
    tiles = x1.shape[0] // ROW_TILE
    row = lambda n: pl.BlockSpec((ROW_TILE, n), lambda i, *_: (i, 0))
    p_rows = lambda n: pl.BlockSpec((ROW_TILE, n), lambda i, *_: (jnp.minimum(i, prompt_tiles - 1), 0))
    s_rows = lambda n: pl.BlockSpec((ROW_TILE, n), lambda i, *_: (jnp.maximum(i - prompt_tiles, 0), 0))
    return pl.pallas_call(
        functools.partial(_combine_body, prompt_tiles=prompt_tiles),
        grid_spec=pltpu.PrefetchScalarGridSpec(
            num_scalar_prefetch=len(tables),
            grid=(tiles,),
            in_specs=[pl.BlockSpec(memory_space=pl.ANY), row(D_MODEL), row(2), row(2),
                      pl.BlockSpec((1, D_MODEL), lambda i, *_: (0, 0))],
            out_specs=[p_rows(D_MODEL), s_rows(D_MODEL)],
            scratch_shapes=[pltpu.SemaphoreType.DMA((COMBINE_RING,))]
            + [pltpu.VMEM((SLOTS * ROW_TILE * PIECES, LANES), f32)] * COMBINE_RING),
        out_shape=[jax.ShapeDtypeStruct((prompt_rows, D_MODEL), f32),
                   jax.ShapeDtypeStruct((x1.shape[0] - prompt_rows, D_MODEL), f32)],
        compiler_params=_cparams(),
        name="moe_combine",
    )(*tables, ys, x1, wts, lpos, g)


def _moe_tables(n, loff, tiles):
    cum = jnp.cumsum(n, axis=0) - n
    counts = jnp.sum(n, axis=0)
    padded = (counts + ROW_TILE - 1) // ROW_TILE * ROW_TILE
    ends = jnp.cumsum(padded)
    starts = ends - padded
    first = jnp.arange(tiles, dtype=i32) * ROW_TILE
    expert = jnp.minimum(jnp.sum((first[:, None] >= ends[None, :]).astype(i32), axis=1), N_EXPERTS - 1)
    sel = expert[:, None] == jnp.arange(N_EXPERTS)[None, :]
    pick = lambda v: jnp.sum(jnp.where(sel, v[None, :], 0), axis=1)
    pos = first - pick(starts)
    valid = jnp.where(first < ends[-1], jnp.clip(pick(counts) - pos, 0, ROW_TILE), 0)
    cum_t, n_t = cum.T[expert], n.T[expert]
    touches = (cum_t + n_t > pos[:, None]) & (cum_t < (pos + ROW_TILE)[:, None]) & (n_t > 0)
    tau = jnp.arange(n.shape[0], dtype=i32)[None, :]
    lo = jnp.min(jnp.where(touches, tau, n.shape[0]), axis=1)
    hi = jnp.max(jnp.where(touches, tau, -1), axis=1)
    ids = jnp.arange(N_EXPERTS)
    busy = counts > 0
    ordinal = jnp.sum(busy[None, :] & (ids[None, :] < ids[:, None]), axis=1)
    following = jnp.min(jnp.where(busy[None, :] & (ids[None, :] > ids[:, None]), ids[None, :], N_EXPERTS), axis=1)
    following = jnp.where(following < N_EXPERTS, following, -1)
    as_i32 = lambda v: v.astype(i32)
    flat = lambda v: v.reshape(-1).astype(i32)
    expert_tables = (tuple(map(as_i32, (expert, pos, valid, lo, hi, pick(ordinal) % 2, pick(following))))
                     + (flat(n), flat(cum), flat(loff)))
    combine_tables = (flat(n), flat(starts[None, :] + cum), flat(loff))
    return expert_tables, combine_tables


def kernel(x_prompt, x_sample, cache_k, cache_v, state_ssm_re, state_ssm_im, g_norm_mix, w_in, attn_sinks, ssm_a_re,
           ssm_a_im, ssm_log_dt, ssm_b_re, ssm_b_im, ssm_c_re, ssm_c_im, ssm_d, w_glu, b_glu, g_attn_out, g_ssm_out,
           w_out, g_norm_ffn, w_router_group, b_router_group, w_router_expert, b_router_expert, w_exp_gate, w_exp_up,
           w_exp_down, g_final):
    bp, lp, _ = x_prompt.shape
    bs, ls, _ = x_sample.shape
    depth = w_in.shape[0]
    assert depth == 1 and ls == SSM_STEPS and lp % ATTN_TILE == 0 and (bs * ls) % ROW_TILE == 0
    tp, ts = bp * lp, bs * ls
    wc = cache_k.shape[2]
    row2 = lambda v: v.reshape(1, -1)

    xp = x_prompt.reshape(tp, D_MODEL)
    xs = x_sample.reshape(ts, D_MODEL)
    w_in_bf = w_in[0].astype(bf16)
    qp, kp, vp, up = _proj(xp, row2(g_norm_mix[0]), w_in_bf, 1024)
    qs, kq, vq, us = _proj(xs, row2(g_norm_mix[0]), w_in_bf, ts)

    sinks = attn_sinks[0]
    g_att = row2(g_attn_out[0])
    ap = _attn_prompt(qp.reshape(bp, lp, -1), kp.reshape(bp, lp, -1), vp.reshape(bp, lp, -1), sinks, g_att)
    a_s, k_roll, v_roll = _attn_sample(qs.reshape(bs, ls, -1), kq.reshape(bs, ls, -1), vq.reshape(bs, ls, -1),
                                       cache_k[0].reshape(bs, wc, KV_WIDTH), cache_v[0].reshape(bs, wc, KV_WIDTH),
                                       sinks, g_att)

    bbar, steps, lev = _ssm_tables(ssm_a_re[0], ssm_a_im[0], ssm_log_dt[0], ssm_b_re[0], ssm_b_im[0])
    c_blocks = lambda c: c.reshape(SSM_BLOCKS, LANES, SSM_STATE)
    d_row = row2(ssm_d[0])
    to_blocks = lambda h: h.reshape(bs, SSM_GROUPS * SSM_STATE)
    from_blocks = lambda h: h.reshape(bs, SSM_GROUPS, SSM_STATE)
    yp, hrp, hip, ysm, hrs, his = _ssm(up.reshape(bp, lp, -1), us.reshape(1, ts, -1), to_blocks(state_ssm_re[0]),
                                       to_blocks(state_ssm_im[0]), bbar, c_blocks(ssm_c_re[0]),
                                       c_blocks(ssm_c_im[0]), steps, lev, d_row)

    wr = jnp.zeros((D_MODEL, ROUTER_COLS), f32)
    wr = wr.at[:, :N_EXPERTS].set(w_router_expert[0]).at[:, N_EXPERTS:N_EXPERTS + N_EXPERT_GROUPS].set(w_router_group[0])
    br = jnp.zeros((1, ROUTER_COLS), f32)
    br = br.at[0, :N_EXPERTS].set(b_router_expert[0]).at[0, N_EXPERTS:N_EXPERTS + N_EXPERT_GROUPS].set(b_router_group[0])
    tri = jnp.tril(jnp.ones((ROW_TILE, ROW_TILE), bf16), -1)
    upper = jnp.triu(jnp.ones((N_EXPERTS, N_EXPERTS), bf16), 1)
    mix_w = (w_glu[0].astype(bf16), row2(b_glu[0]), row2(g_ssm_out[0]), w_out[0].astype(bf16), row2(g_norm_ffn[0]),
             wr.astype(bf16), br, tri, upper)
    x1, xl, wts, lpos, n_rows, n_off = _mix(ap.reshape(tp, -1), a_s.reshape(ts, -1), yp.reshape(tp, -1),
                                            ysm.reshape(ts, -1), xp, xs, *mix_w)

    per_tile = lambda v: v.reshape(-1, N_EXPERTS).astype(i32)
    tiles = (SLOTS * (tp + ts)) // ROW_TILE + N_EXPERTS
    expert_tables, combine_tables = _moe_tables(per_tile(n_rows), per_tile(n_off), tiles)
    expert_out = _experts(expert_tables, xl, w_exp_gate[0], w_exp_up[0], w_exp_down[0], tiles)
    y_p, y_s = _combine(combine_tables, expert_out, x1, wts, lpos, row2(g_final), tp)

    kvshape = lambda a, b: a.reshape(1, b, -1, N_KV_HEADS, HEAD_DIM)
    block_state = lambda h: h.reshape(bp, SSM_GROUPS, SSM_STATE)[None]
    wcp = min(WINDOW, lp)
    return (y_p.reshape(bp, lp, D_MODEL), y_s.reshape(bs, ls, D_MODEL),
            kvshape(kp.reshape(bp, lp, -1)[:, lp - wcp:], bp), kvshape(vp.reshape(bp, lp, -1)[:, lp - wcp:], bp),
            block_state(hrp), block_state(hip),
            kvshape(k_roll, bs), kvshape(v_roll, bs),
            from_blocks(hrs)[None], from_blocks(his)[None])
```
